```python
import jax, jax.numpy as jnp
from jax import lax
import numpy as np

D_MODEL = 1024
BATCH = 8
SEQ = 4096
DEPTH = 2

A_WIDTH = D_MODEL // 2
B_WIDTH = D_MODEL - A_WIDTH
A_HEADS = 8
B_HEADS = 8
A_CONV = 31
B_CONV = 3
IN_WIDTH = 2 * A_WIDTH + 3 * B_WIDTH
POOL_WINDOWS = (2, 4, 8, 16)
POOL_GROUPS = len(POOL_WINDOWS)
POOL_GROUP = D_MODEL // POOL_GROUPS
D_FF = ((8 * D_MODEL // 3 + 255) // 256) * 256
FFN_CONV = 3
RMS_EPS = 1e-6
LN_EPS = 1e-5
N_EVEN = (DEPTH + 1) // 2
N_ODD = DEPTH // 2

kernel_name = "hybrid_conv_pool_convffn_trunk"


def rmsnorm(x, g):
    xf = x.astype(jnp.float32)
    y = xf * lax.rsqrt(jnp.mean(xf * xf, axis=-1, keepdims=True) + RMS_EPS)
    return (y * g.astype(jnp.float32)).astype(x.dtype)


def layernorm(x, g, b):
    xf = x.astype(jnp.float32)
    mu = jnp.mean(xf, axis=-1, keepdims=True)
    xc = xf - mu
    var = jnp.mean(xc * xc, axis=-1, keepdims=True)
    y = xc * lax.rsqrt(var + LN_EPS) * g.astype(jnp.float32) + b.astype(jnp.float32)
    return y.astype(x.dtype)


def causal_dwconv(x, w):
    k, c = w.shape
    return lax.conv_general_dilated(
        x, w[:, None, :].astype(x.dtype), window_strides=(1,), padding=[(k - 1, 0)],
        dimension_numbers=("NWC", "WIO", "NWC"), feature_group_count=c)


def conv_mixer(h, w_in, conv_a, ln_a_g, ln_a_b, conv_b, w_out):
    z = h @ w_in
    a_val, a_gate, b_gate, c_gate, bc_val = jnp.split(
        z, [A_WIDTH, 2 * A_WIDTH, 2 * A_WIDTH + B_WIDTH, 2 * A_WIDTH + 2 * B_WIDTH], axis=-1)
    a = causal_dwconv(a_val * jax.nn.sigmoid(a_gate), conv_a)
    a = jax.nn.silu(layernorm(a, ln_a_g, ln_a_b))
    b = b_gate * causal_dwconv(c_gate * bc_val, conv_b)
    return jnp.concatenate([a, b], axis=-1) @ w_out


def pool_mixer(h, w_pool, pool_scale):
    s = h.shape[1]
    hf = h.astype(jnp.float32)
    cs = jnp.cumsum(hf, axis=1)
    t = jnp.arange(1, s + 1, dtype=jnp.float32)[:, None]
    outs = []
    for g, w in enumerate(POOL_WINDOWS):
        sl = slice(g * POOL_GROUP, (g + 1) * POOL_GROUP)
        c = cs[..., sl]
        prev = jnp.pad(c, ((0, 0), (w, 0), (0, 0)))[:, :s]
        mean = (c - prev) / jnp.minimum(t, float(w))
        outs.append(mean - hf[..., sl])
    p = jnp.stack(outs, axis=2).astype(h.dtype)
    y = jnp.einsum("bsgc,gcd->bsgd", p, w_pool).reshape(h.shape)
    return y * pool_scale


def conv_ffn(h, w_up, w_conv, w_down):
    u = causal_dwconv(h @ w_up, w_conv)
    g, v = jnp.split(u, 2, axis=-1)
    return (jax.nn.silu(g) * v) @ w_down


def _fwd_setup_inputs(seed: int = 0) -> dict:
    key = jax.random.key(seed)
    ks = jax.random.split(key, 20)
    f32 = jnp.float32
    nrm = lambda k, shape, scale: jax.random.normal(k, shape, f32) * scale
    return {
        "x": nrm(ks[0], (BATCH, SEQ, D_MODEL), 1.0),
        "norm_mix_even": 1.0 + nrm(ks[1], (N_EVEN, D_MODEL), 0.02),
        "w_in": nrm(ks[2], (N_EVEN, D_MODEL, IN_WIDTH), D_MODEL ** -0.5),
        "conv_a": nrm(ks[3], (N_EVEN, A_CONV, A_WIDTH), A_CONV ** -0.5),
        "ln_a_g": 1.0 + nrm(ks[4], (N_EVEN, A_WIDTH), 0.02),
        "ln_a_b": nrm(ks[5], (N_EVEN, A_WIDTH), 0.02),
        "conv_b": nrm(ks[6], (N_EVEN, B_CONV, B_WIDTH), B_CONV ** -0.5),
        "w_out": nrm(ks[7], (N_EVEN, D_MODEL, D_MODEL), D_MODEL ** -0.5),
        "norm_mix_odd": 1.0 + nrm(ks[8], (N_ODD, D_MODEL), 0.02),
        "w_pool": nrm(ks[9], (N_ODD, POOL_GROUPS, POOL_GROUP, POOL_GROUP), POOL_GROUP ** -0.5),
        "pool_scale": 1.0 + nrm(ks[10], (N_ODD, D_MODEL), 0.1),
        "norm_ffn": 1.0 + nrm(ks[11], (DEPTH, D_MODEL), 0.02),
        "w_up": nrm(ks[12], (DEPTH, D_MODEL, 2 * D_FF), D_MODEL ** -0.5),
        "conv_ffn_w": nrm(ks[13], (DEPTH, FFN_CONV, 2 * D_FF), FFN_CONV ** -0.5),
        "w_down": nrm(ks[14], (DEPTH, D_FF, D_MODEL), D_FF ** -0.5),
        "norm_final": 1.0 + nrm(ks[15], (D_MODEL,), 0.02),
    }


def _fwd_reference(x, norm_mix_even, w_in, conv_a, ln_a_g, ln_a_b, conv_b, w_out,
              norm_mix_odd, w_pool, pool_scale, norm_ffn, w_up, conv_ffn_w, w_down,
              norm_final):
    for layer in range(DEPTH):
        i = layer // 2
        if layer % 2 == 0:
            h = rmsnorm(x, norm_mix_even[i])
            x = x + conv_mixer(h, w_in[i], conv_a[i], ln_a_g[i], ln_a_b[i], conv_b[i], w_out[i])
        else:
            h = rmsnorm(x, norm_mix_odd[i])
            x = x + pool_mixer(h, w_pool[i], pool_scale[i])
        x = x + conv_ffn(rmsnorm(x, norm_ffn[layer]), w_up[layer], conv_ffn_w[layer], w_down[layer])
    return rmsnorm(x, norm_final)


import jax as _jax
import jax.numpy as _jnp

TWIN_FORMAT = 'train_step'
FWD_PARAMS = ['x', 'norm_mix_even', 'w_in', 'conv_a', 'ln_a_g', 'ln_a_b', 'conv_b', 'w_out', 'norm_mix_odd', 'w_pool', 'pool_scale', 'norm_ffn', 'w_up', 'conv_ffn_w', 'w_down', 'norm_final']
TWIN_WEIGHTS = ['norm_mix_even', 'w_in', 'conv_a', 'ln_a_g', 'ln_a_b', 'conv_b', 'w_out', 'norm_mix_odd', 'w_pool', 'pool_scale', 'norm_ffn', 'w_up', 'conv_ffn_w', 'w_down', 'norm_final']
TWIN_DIFF_INPUT = 'x'
TWIN_INPUTS = ['x', 'norm_mix_even', 'w_in', 'conv_a', 'ln_a_g', 'ln_a_b', 'conv_b', 'w_out', 'norm_mix_odd', 'w_pool', 'pool_scale', 'norm_ffn', 'w_up', 'conv_ffn_w', 'w_down', 'norm_final', 'loss_target', 'm_norm_mix_even', 'm_w_in', 'm_conv_a', 'm_ln_a_g', 'm_ln_a_b', 'm_conv_b', 'm_w_out', 'm_norm_mix_odd', 'm_w_pool', 'm_pool_scale', 'm_norm_ffn', 'm_w_up', 'm_conv_ffn_w', 'm_w_down', 'm_norm_final', 'v_norm_mix_even', 'v_w_in', 'v_conv_a', 'v_ln_a_g', 'v_ln_a_b', 'v_conv_b', 'v_w_out', 'v_norm_mix_odd', 'v_w_pool', 'v_pool_scale', 'v_norm_ffn', 'v_w_up', 'v_conv_ffn_w', 'v_w_down', 'v_norm_final']
TWIN_OUTPUTS = ['loss', 'grad_x', 'grad_norm_mix_even', 'grad_w_in', 'grad_conv_a', 'grad_ln_a_g', 'grad_ln_a_b', 'grad_conv_b', 'grad_w_out', 'grad_norm_mix_odd', 'grad_w_pool', 'grad_pool_scale', 'grad_norm_ffn', 'grad_w_up', 'grad_conv_ffn_w', 'grad_w_down', 'grad_norm_final', 'delta_norm_mix_even', 'delta_w_in', 'delta_conv_a', 'delta_ln_a_g', 'delta_ln_a_b', 'delta_conv_b', 'delta_w_out', 'delta_norm_mix_odd', 'delta_w_pool', 'delta_pool_scale', 'delta_norm_ffn', 'delta_w_up', 'delta_conv_ffn_w', 'delta_w_down', 'delta_norm_final', 'new_m_norm_mix_even', 'new_m_w_in', 'new_m_conv_a', 'new_m_ln_a_g', 'new_m_ln_a_b', 'new_m_conv_b', 'new_m_w_out', 'new_m_norm_mix_odd', 'new_m_w_pool', 'new_m_pool_scale', 'new_m_norm_ffn', 'new_m_w_up', 'new_m_conv_ffn_w', 'new_m_w_down', 'new_m_norm_final', 'new_v_norm_mix_even', 'new_v_w_in', 'new_v_conv_a', 'new_v_ln_a_g', 'new_v_ln_a_b', 'new_v_conv_b', 'new_v_w_out', 'new_v_norm_mix_odd', 'new_v_w_pool', 'new_v_pool_scale', 'new_v_norm_ffn', 'new_v_w_up', 'new_v_conv_ffn_w', 'new_v_w_down', 'new_v_norm_final']
TWIN_LEAF_KINDS = {'loss': 'loss', 'grad_x': 'grad_x', 'grad_norm_mix_even': 'grad_w', 'grad_w_in': 'grad_w', 'grad_conv_a': 'grad_w', 'grad_ln_a_g': 'grad_w', 'grad_ln_a_b': 'grad_w', 'grad_conv_b': 'grad_w', 'grad_w_out': 'grad_w', 'grad_norm_mix_odd': 'grad_w', 'grad_w_pool': 'grad_w', 'grad_pool_scale': 'grad_w', 'grad_norm_ffn': 'grad_w', 'grad_w_up': 'grad_w', 'grad_conv_ffn_w': 'grad_w', 'grad_w_down': 'grad_w', 'grad_norm_final': 'grad_w', 'delta_norm_mix_even': 'delta_w', 'delta_w_in': 'delta_w', 'delta_conv_a': 'delta_w', 'delta_ln_a_g': 'delta_w', 'delta_ln_a_b': 'delta_w', 'delta_conv_b': 'delta_w', 'delta_w_out': 'delta_w', 'delta_norm_mix_odd': 'delta_w', 'delta_w_pool': 'delta_w', 'delta_pool_scale': 'delta_w', 'delta_norm_ffn': 'delta_w', 'delta_w_up': 'delta_w', 'delta_conv_ffn_w': 'delta_w', 'delta_w_down': 'delta_w', 'delta_norm_final': 'delta_w', 'new_m_norm_mix_even': 'new_m', 'new_m_w_in': 'new_m', 'new_m_conv_a': 'new_m', 'new_m_ln_a_g': 'new_m', 'new_m_ln_a_b': 'new_m', 'new_m_conv_b': 'new_m', 'new_m_w_out': 'new_m', 'new_m_norm_mix_odd': 'new_m', 'new_m_w_pool': 'new_m', 'new_m_pool_scale': 'new_m', 'new_m_norm_ffn': 'new_m', 'new_m_w_up': 'new_m', 'new_m_conv_ffn_w': 'new_m', 'new_m_w_down': 'new_m', 'new_m_norm_final': 'new_m', 'new_v_norm_mix_even': 'new_v', 'new_v_w_in': 'new_v', 'new_v_conv_a': 'new_v', 'new_v_ln_a_g': 'new_v', 'new_v_ln_a_b': 'new_v', 'new_v_conv_b': 'new_v', 'new_v_w_out': 'new_v', 'new_v_norm_mix_odd': 'new_v', 'new_v_w_pool': 'new_v', 'new_v_pool_scale': 'new_v', 'new_v_norm_ffn': 'new_v', 'new_v_w_up': 'new_v', 'new_v_conv_ffn_w': 'new_v', 'new_v_w_down': 'new_v', 'new_v_norm_final': 'new_v'}


def _forward(args):
    return _fwd_reference(*[args[k] for k in FWD_PARAMS])


def _output_shape():
    out = _jax.eval_shape(lambda: _forward(_fwd_setup_inputs(0)))
    return out.shape, out.dtype

N_MICROBATCH = 1
ADAM_LR = 0.001
ADAM_B1 = 0.9
ADAM_B2 = 0.999
ADAM_EPS = 1e-08
ADAM_WD = 0.01
ADAM_STEP = 10
PER_EXAMPLE_BATCH_AXIS = {'x': 0, 'loss_target': 0}
SHARED_INPUTS = []
_WEIGHT_DTYPES = {'norm_mix_even': _jnp.float32, 'w_in': _jnp.float32, 'conv_a': _jnp.float32, 'ln_a_g': _jnp.float32, 'ln_a_b': _jnp.float32, 'conv_b': _jnp.float32, 'w_out': _jnp.float32, 'norm_mix_odd': _jnp.float32, 'w_pool': _jnp.float32, 'pool_scale': _jnp.float32, 'norm_ffn': _jnp.float32, 'w_up': _jnp.float32, 'conv_ffn_w': _jnp.float32, 'w_down': _jnp.float32, 'norm_final': _jnp.float32}
MOMENT_SCALE = {'norm_mix_even': 2.091541e-01, 'w_in': 1.328028e-01, 'conv_a': 9.856732e-02, 'ln_a_g': 1.285373e-01, 'ln_a_b': 8.917511e-02, 'conv_b': 1.617676e-01, 'w_out': 1.308386e-01, 'norm_mix_odd': 1.065751e-01, 'w_pool': 9.929432e-02, 'pool_scale': 2.291514e-01, 'norm_ffn': 1.060647e-01, 'w_up': 4.422581e-02, 'conv_ffn_w': 4.410786e-02, 'w_down': 7.221180e-02, 'norm_final': 3.206186e+01}


def _to_microbatches(a, axis):
    t = _jnp.moveaxis(a, axis, 0)
    t = t.reshape((N_MICROBATCH, t.shape[0] // N_MICROBATCH) + t.shape[1:])
    return _jnp.moveaxis(t, 1, axis + 1)


def setup_inputs(seed: int = 0) -> dict:
    inp = _fwd_setup_inputs(seed)
    key = _jax.random.fold_in(_jax.random.key(seed), 7919)
    shape, _ = _output_shape()
    out = dict(inp)
    out["loss_target"] = _jax.random.normal(_jax.random.fold_in(key, 0), shape, _jnp.float32)
    for i, name in enumerate(TWIN_WEIGHTS):
        w = inp[name].astype(_jnp.float32)
        if MOMENT_SCALE is None:
            s = _jnp.sqrt(_jnp.mean(_jnp.square(w)) + 1e-30)
        else:
            s = MOMENT_SCALE[name]
        km, kv = _jax.random.split(_jax.random.fold_in(key, i + 1))
        out[name] = w
        out["m_" + name] = s * _jax.random.normal(km, w.shape, _jnp.float32)
        out["v_" + name] = (s * s) * _jax.random.uniform(kv, w.shape, _jnp.float32, 0.5, 1.5)
    if N_MICROBATCH > 1:
        for name, axis in PER_EXAMPLE_BATCH_AXIS.items():
            out[name] = _to_microbatches(out[name], axis)
    return {'x': out['x'], 'norm_mix_even': out['norm_mix_even'], 'w_in': out['w_in'], 'conv_a': out['conv_a'], 'ln_a_g': out['ln_a_g'], 'ln_a_b': out['ln_a_b'], 'conv_b': out['conv_b'], 'w_out': out['w_out'], 'norm_mix_odd': out['norm_mix_odd'], 'w_pool': out['w_pool'], 'pool_scale': out['pool_scale'], 'norm_ffn': out['norm_ffn'], 'w_up': out['w_up'], 'conv_ffn_w': out['conv_ffn_w'], 'w_down': out['w_down'], 'norm_final': out['norm_final'], 'loss_target': out['loss_target'], 'm_norm_mix_even': out['m_norm_mix_even'], 'm_w_in': out['m_w_in'], 'm_conv_a': out['m_conv_a'], 'm_ln_a_g': out['m_ln_a_g'], 'm_ln_a_b': out['m_ln_a_b'], 'm_conv_b': out['m_conv_b'], 'm_w_out': out['m_w_out'], 'm_norm_mix_odd': out['m_norm_mix_odd'], 'm_w_pool': out['m_w_pool'], 'm_pool_scale': out['m_pool_scale'], 'm_norm_ffn': out['m_norm_ffn'], 'm_w_up': out['m_w_up'], 'm_conv_ffn_w': out['m_conv_ffn_w'], 'm_w_down': out['m_w_down'], 'm_norm_final': out['m_norm_final'], 'v_norm_mix_even': out['v_norm_mix_even'], 'v_w_in': out['v_w_in'], 'v_conv_a': out['v_conv_a'], 'v_ln_a_g': out['v_ln_a_g'], 'v_ln_a_b': out['v_ln_a_b'], 'v_conv_b': out['v_conv_b'], 'v_w_out': out['v_w_out'], 'v_norm_mix_odd': out['v_norm_mix_odd'], 'v_w_pool': out['v_w_pool'], 'v_pool_scale': out['v_pool_scale'], 'v_norm_ffn': out['v_norm_ffn'], 'v_w_up': out['v_w_up'], 'v_conv_ffn_w': out['v_conv_ffn_w'], 'v_w_down': out['v_w_down'], 'v_norm_final': out['v_norm_final']}


def _loss(weights, diff, rest, loss_target):
    with _jax.named_scope("forward"):
        args = {**rest, TWIN_DIFF_INPUT: diff, **{k: w.astype(_WEIGHT_DTYPES[k]) for k, w in weights.items()}}
        y = _forward(args)
    with _jax.named_scope("loss_head"):
        err = _jnp.square(y.astype(_jnp.float32) - loss_target)
        return 0.5 * _jnp.sum(_jnp.mean(err, axis=-1)) if err.ndim else 0.5 * err


def _adamw(w, g, m, v):
    m = ADAM_B1 * m + (1.0 - ADAM_B1) * g
    v = ADAM_B2 * v + (1.0 - ADAM_B2) * _jnp.square(g)
    m_hat = m / (1.0 - ADAM_B1 ** ADAM_STEP)
    v_hat = v / (1.0 - ADAM_B2 ** ADAM_STEP)
    delta = -ADAM_LR * (m_hat / (_jnp.sqrt(v_hat) + ADAM_EPS) + ADAM_WD * w)
    return delta, m, v


def reference(x, norm_mix_even, w_in, conv_a, ln_a_g, ln_a_b, conv_b, w_out, norm_mix_odd, w_pool, pool_scale, norm_ffn, w_up, conv_ffn_w, w_down, norm_final, loss_target, m_norm_mix_even, m_w_in, m_conv_a, m_ln_a_g, m_ln_a_b, m_conv_b, m_w_out, m_norm_mix_odd, m_w_pool, m_pool_scale, m_norm_ffn, m_w_up, m_conv_ffn_w, m_w_down, m_norm_final, v_norm_mix_even, v_w_in, v_conv_a, v_ln_a_g, v_ln_a_b, v_conv_b, v_w_out, v_norm_mix_odd, v_w_pool, v_pool_scale, v_norm_ffn, v_w_up, v_conv_ffn_w, v_w_down, v_norm_final):
    given = dict(x=x, norm_mix_even=norm_mix_even, w_in=w_in, conv_a=conv_a, ln_a_g=ln_a_g, ln_a_b=ln_a_b, conv_b=conv_b, w_out=w_out, norm_mix_odd=norm_mix_odd, w_pool=w_pool, pool_scale=pool_scale, norm_ffn=norm_ffn, w_up=w_up, conv_ffn_w=conv_ffn_w, w_down=w_down, norm_final=norm_final, loss_target=loss_target, m_norm_mix_even=m_norm_mix_even, m_w_in=m_w_in, m_conv_a=m_conv_a, m_ln_a_g=m_ln_a_g, m_ln_a_b=m_ln_a_b, m_conv_b=m_conv_b, m_w_out=m_w_out, m_norm_mix_odd=m_norm_mix_odd, m_w_pool=m_w_pool, m_pool_scale=m_pool_scale, m_norm_ffn=m_norm_ffn, m_w_up=m_w_up, m_conv_ffn_w=m_conv_ffn_w, m_w_down=m_w_down, m_norm_final=m_norm_final, v_norm_mix_even=v_norm_mix_even, v_w_in=v_w_in, v_conv_a=v_conv_a, v_ln_a_g=v_ln_a_g, v_ln_a_b=v_ln_a_b, v_conv_b=v_conv_b, v_w_out=v_w_out, v_norm_mix_odd=v_norm_mix_odd, v_w_pool=v_w_pool, v_pool_scale=v_pool_scale, v_norm_ffn=v_norm_ffn, v_w_up=v_w_up, v_conv_ffn_w=v_conv_ffn_w, v_w_down=v_w_down, v_norm_final=v_norm_final)
    weights = {n: given[n] for n in TWIN_WEIGHTS}
    shared = {n: given[n] for n in SHARED_INPUTS}
    per_example = {n: given[n] for n in ['x']}
    grad_fn = _jax.value_and_grad(_loss, argnums=(0, 1))

    def one_microbatch(ex, loss_target):
        ex = dict(ex)
        diff = ex.pop(TWIN_DIFF_INPUT)
        return grad_fn(weights, diff, {**shared, **ex}, loss_target)

    if N_MICROBATCH == 1:
        loss, (grad_w, grad_x) = one_microbatch(per_example, given["loss_target"])
    else:
        def body(carry, xs):
            loss_sum, grad_sum = carry
            l_k, (gw_k, gx_k) = one_microbatch(xs[0], xs[1])
            with _jax.named_scope("update"):
                return (loss_sum + l_k, _jax.tree.map(_jnp.add, grad_sum, gw_k)), gx_k

        init = (_jnp.zeros((), _jnp.float32), _jax.tree.map(_jnp.zeros_like, weights))
        (loss, grad_w), grad_x = _jax.lax.scan(body, init, (per_example, given["loss_target"]))
    with _jax.named_scope("update"):
        delta_w, new_m, new_v = {}, {}, {}
        for n in TWIN_WEIGHTS:
            delta_w[n], new_m[n], new_v[n] = _adamw(weights[n], grad_w[n], given["m_" + n], given["v_" + n])
    return (loss, grad_x, *[grad_w[n] for n in TWIN_WEIGHTS], *[delta_w[n] for n in TWIN_WEIGHTS],
            *[new_m[n] for n in TWIN_WEIGHTS], *[new_v[n] for n in TWIN_WEIGHTS])
```

```python
import jax
import jax.numpy as jnp
from jax import lax
from jax.experimental import pallas as pl
from jax.experimental.pallas import tpu as pltpu

F32 = jnp.float32
BF16 = jnp.bfloat16

RMS_EPS = 1e-6
LN_EPS = 1e-5
ADAM_LR = 0.001
ADAM_B1 = 0.9
ADAM_B2 = 0.999
ADAM_EPS = 1e-08
ADAM_WD = 0.01
ADAM_STEP = 10

N_DEV = 8
A_WIDTH = 512
A_TAPS = 31
POOL_WINDOWS = (2, 4, 8, 16)
POOL_GROUP = 256
FFN_CW = 256
HALO_A = 32
HALO_S = 16
VMEM_LIMIT = 56 * 1024 * 1024


def _params(sem, vmem=VMEM_LIMIT):
    return pltpu.CompilerParams(dimension_semantics=sem, vmem_limit_bytes=vmem)


def _sigmoid(x):
    return 1.0 / (1.0 + jnp.exp(-x))


def _prev_blk(i, ts, hb):
    return jnp.maximum(i * (ts // hb) - 1, 0)


def _next_blk(i, ts, hb, s):
    return jnp.minimum((i + 1) * (ts // hb), s // hb - 1)


def _mm(a, b, *, name, ta=False, tb=False, add=None, out_dtype=F32, tm=512, tn=512, tk=512):
    m, k = (a.shape[1], a.shape[0]) if ta else a.shape
    n = b.shape[0] if tb else b.shape[1]
    tm, tn, tk = min(tm, m), min(tn, n), min(tk, k)
    assert m % tm == 0 and n % tn == 0 and k % tk == 0, (name, m, n, k, tm, tn, tk)
    nk = k // tk
    dims = (((0,) if ta else (1,), (1,) if tb else (0,)), ((), ()))

    def body(*refs):
        if add is None:
            a_ref, b_ref, o_ref, acc_ref = refs
        else:
            a_ref, b_ref, add_ref, o_ref, acc_ref = refs
        kk = pl.program_id(2)

        @pl.when(kk == 0)
        def _():
            acc_ref[...] = jnp.zeros_like(acc_ref)

        acc_ref[...] += lax.dot_general(a_ref[...].astype(BF16), b_ref[...].astype(BF16), dims,
                                        preferred_element_type=F32)

        @pl.when(kk == nk - 1)
        def _():
            r = acc_ref[...]
            if add is not None:
                r = r + add_ref[...]
            o_ref[...] = r.astype(out_dtype)

    a_spec = pl.BlockSpec((tk, tm), lambda i, j, kk: (kk, i)) if ta else pl.BlockSpec((tm, tk), lambda i, j, kk: (i, kk))
    b_spec = pl.BlockSpec((tn, tk), lambda i, j, kk: (j, kk)) if tb else pl.BlockSpec((tk, tn), lambda i, j, kk: (kk, j))
    in_specs = [a_spec, b_spec]
    args = [a, b]
    if add is not None:
        in_specs.append(pl.BlockSpec((tm, tn), lambda i, j, kk: (i, j)))
        args.append(add)
    return pl.pallas_call(
        body, name=name, grid=(m // tm, n // tn, nk),
        in_specs=in_specs, out_specs=pl.BlockSpec((tm, tn), lambda i, j, kk: (i, j)),
        out_shape=jax.ShapeDtypeStruct((m, n), out_dtype),
        scratch_shapes=[pltpu.VMEM((tm, tn), F32)],
        compiler_params=_params(("parallel", "parallel", "arbitrary")),
    )(*args)


def _rms_mm(x, g, w, *, name, out_dtype, tm=512, tn=512):
    s, d = x.shape
    n = w.shape[1]
    assert s % tm == 0 and n % tn == 0

    def body(x_ref, g_ref, w_ref, h_ref, z_ref, hs_ref):
        @pl.when(pl.program_id(1) == 0)
        def _():
            xv = x_ref[...]
            r = lax.rsqrt(jnp.mean(xv * xv, axis=-1, keepdims=True) + RMS_EPS)
            hv = (xv * r * g_ref[...]).astype(BF16)
            hs_ref[...] = hv
            h_ref[...] = hv

        z_ref[...] = jnp.dot(hs_ref[...], w_ref[...], preferred_element_type=F32).astype(out_dtype)

    return pl.pallas_call(
        body, name=name, grid=(s // tm, n // tn),
        in_specs=[pl.BlockSpec((tm, d), lambda i, j: (i, 0)),
                  pl.BlockSpec((1, d), lambda i, j: (0, 0)),
                  pl.BlockSpec((d, tn), lambda i, j: (0, j))],
        out_specs=[pl.BlockSpec((tm, d), lambda i, j: (i, 0)),
                   pl.BlockSpec((tm, tn), lambda i, j: (i, j))],
        out_shape=[jax.ShapeDtypeStruct((s, d), BF16), jax.ShapeDtypeStruct((s, n), out_dtype)],
        scratch_shapes=[pltpu.VMEM((tm, d), BF16)],
        compiler_params=_params(("parallel", "arbitrary")),
    )(x, g, w)


def _rms_bwd_rows(xv, gv, dh):
    r = lax.rsqrt(jnp.mean(xv * xv, axis=-1, keepdims=True) + RMS_EPS)
    xh = xv * r
    dn = dh * gv
    dx = r * (dn - xh * jnp.mean(dn * xh, axis=-1, keepdims=True))
    return dx, dh * xh


def _rms_bwd(x, g, dh, dres, *, name, ts=256):
    s, d = x.shape

    def body(x_ref, g_ref, dh_ref, dres_ref, dx_ref, dg_ref):
        @pl.when(pl.program_id(0) == 0)
        def _():
            dg_ref[...] = jnp.zeros_like(dg_ref)

        dx, dgr = _rms_bwd_rows(x_ref[...], g_ref[...], dh_ref[...])
        dx_ref[...] = dres_ref[...] + dx
        dg_ref[...] += jnp.sum(dgr, axis=0, keepdims=True)

    row = pl.BlockSpec((ts, d), lambda i: (i, 0))
    vec = pl.BlockSpec((1, d), lambda i: (0, 0))
    return pl.pallas_call(
        body, name=name, grid=(s // ts,),
        in_specs=[row, vec, row, row], out_specs=[row, vec],
        out_shape=[jax.ShapeDtypeStruct((s, d), F32), jax.ShapeDtypeStruct((1, d), F32)],
        compiler_params=_params(("arbitrary",)),
    )(x, g, dh, dres)


def _final_loss(x, g, target, *, name, ts=256):
    s, d = x.shape

    def body(x_ref, g_ref, t_ref, loss_ref, dx_ref, dg_ref):
        @pl.when(pl.program_id(0) == 0)
        def _():
            loss_ref[...] = jnp.zeros_like(loss_ref)
            dg_ref[...] = jnp.zeros_like(dg_ref)

        xv = x_ref[...]
        gv = g_ref[...]
        r = lax.rsqrt(jnp.mean(xv * xv, axis=-1, keepdims=True) + RMS_EPS)
        xh = xv * r
        err = xh * gv - t_ref[...]
        loss_ref[...] += 0.5 * jnp.sum(jnp.mean(err * err, axis=-1, keepdims=True), axis=0, keepdims=True)
        dy = err * (1.0 / d)
        dn = dy * gv
        dx_ref[...] = r * (dn - xh * jnp.mean(dn * xh, axis=-1, keepdims=True))
        dg_ref[...] += jnp.sum(dy * xh, axis=0, keepdims=True)

    row = pl.BlockSpec((ts, d), lambda i: (i, 0))
    vec = pl.BlockSpec((1, d), lambda i: (0, 0))
    return pl.pallas_call(
        body, name=name, grid=(s // ts,),
        in_specs=[row, vec, row],
        out_specs=[pl.BlockSpec((1, 1), lambda i: (0, 0)), row, vec],
        out_shape=[jax.ShapeDtypeStruct((1, 1), F32), jax.ShapeDtypeStruct((s, d), F32),
                   jax.ShapeDtypeStruct((1, d), F32)],
        compiler_params=_params(("arbitrary",)),
    )(x, g, target)


def _conv_taps(ext_ref, w_ref, n_taps, base, r0, rows, reverse=False):
    acc = None
    for k in range(n_taps):
        off = r0 + (base - k if reverse else base + k)
        term = w_ref[k:k + 1, :] * ext_ref[pl.ds(off, rows), :]
        acc = term if acc is None else acc + term
    return acc


def _mix0_fwd(z, conv_a, ln_g, ln_b, conv_b, *, name, ts=256, rc=32):
    s = z.shape[0]
    c = A_WIDTH
    hb = HALO_A

    def body(z_ref, zp_ref, wa_ref, lg_ref, lb_ref, wb_ref, ab_ref, ca_ref, exta, extb):
        keep = jnp.where(pl.program_id(0) > 0, 1.0, 0.0)
        zp = zp_ref[...]
        exta[0:hb, :] = zp[:, 0:c] * _sigmoid(zp[:, c:2 * c]) * keep
        extb[0:hb, :] = zp[:, 3 * c:4 * c] * zp[:, 4 * c:5 * c] * keep
        exta[hb:hb + ts, :] = z_ref[:, 0:c] * _sigmoid(z_ref[:, c:2 * c])
        extb[hb:hb + ts, :] = z_ref[:, 3 * c:4 * c] * z_ref[:, 4 * c:5 * c]
        lg = lg_ref[...]
        lb = lb_ref[...]
        for q in range(ts // rc):
            r0 = q * rc
            ca = _conv_taps(exta, wa_ref, A_TAPS, hb - (A_TAPS - 1), r0, rc)
            ca_ref[r0:r0 + rc, :] = ca
            mu = jnp.mean(ca, axis=-1, keepdims=True)
            xc = ca - mu
            rs = lax.rsqrt(jnp.mean(xc * xc, axis=-1, keepdims=True) + LN_EPS)
            l = xc * rs * lg + lb
            ab_ref[r0:r0 + rc, 0:c] = (l * _sigmoid(l)).astype(BF16)
            cbc = _conv_taps(extb, wb_ref, 3, hb - 2, r0, rc)
            ab_ref[r0:r0 + rc, c:2 * c] = (z_ref[r0:r0 + rc, 2 * c:3 * c] * cbc).astype(BF16)

    return pl.pallas_call(
        body, name=name, grid=(s // ts,),
        in_specs=[pl.BlockSpec((ts, 5 * c), lambda i: (i, 0)),
                  pl.BlockSpec((hb, 5 * c), lambda i: (_prev_blk(i, ts, hb), 0)),
                  pl.BlockSpec((32, c), lambda i: (0, 0)),
                  pl.BlockSpec((1, c), lambda i: (0, 0)),
                  pl.BlockSpec((1, c), lambda i: (0, 0)),
                  pl.BlockSpec((8, c), lambda i: (0, 0))],
        out_specs=[pl.BlockSpec((ts, 2 * c), lambda i: (i, 0)),
                   pl.BlockSpec((ts, c), lambda i: (i, 0))],
        out_shape=[jax.ShapeDtypeStruct((s, 2 * c), BF16), jax.ShapeDtypeStruct((s, c), F32)],
        scratch_shapes=[pltpu.VMEM((hb + ts, c), F32), pltpu.VMEM((hb + ts, c), F32)],
        compiler_params=_params(("parallel",)),
    )(z, z, conv_a, ln_g, ln_b, conv_b)


def _mix0_bwd(z, ca, dab, conv_a, ln_g, ln_b, conv_b, *, name, ts=256, rc=32):
    s = z.shape[0]
    c = A_WIDTH
    hb = HALO_A
    ta = A_TAPS

    def body(z_ref, zp_ref, zn_ref, ca_ref, can_ref, d_ref, dn_ref, wa_ref, lg_ref, lb_ref, wb_ref,
             dz_ref, dwa_ref, dwb_ref, dlg_ref, dlb_ref, exta, extb, extdca, extdcb):
        i = pl.program_id(0)
        keep_p = jnp.where(i > 0, 1.0, 0.0)
        keep_n = jnp.where(i < s // ts - 1, 1.0, 0.0)

        @pl.when(i == 0)
        def _():
            dwa_ref[...] = jnp.zeros_like(dwa_ref)
            dwb_ref[...] = jnp.zeros_like(dwb_ref)
            dlg_ref[...] = jnp.zeros_like(dlg_ref)
            dlb_ref[...] = jnp.zeros_like(dlb_ref)

        lg = lg_ref[...]
        lb = lb_ref[...]
        zp = zp_ref[...]
        exta[0:hb, :] = zp[:, 0:c] * _sigmoid(zp[:, c:2 * c]) * keep_p
        extb[0:hb, :] = zp[:, 3 * c:4 * c] * zp[:, 4 * c:5 * c] * keep_p
        exta[hb:hb + ts, :] = z_ref[:, 0:c] * _sigmoid(z_ref[:, c:2 * c])
        extb[hb:hb + ts, :] = z_ref[:, 3 * c:4 * c] * z_ref[:, 4 * c:5 * c]

        def ln_bwd(cav, dav):
            mu = jnp.mean(cav, axis=-1, keepdims=True)
            xc = cav - mu
            rs = lax.rsqrt(jnp.mean(xc * xc, axis=-1, keepdims=True) + LN_EPS)
            nv = xc * rs
            l = nv * lg + lb
            sg = _sigmoid(l)
            dl = dav * (sg * (1.0 + l * (1.0 - sg)))
            dnv = dl * lg
            dca = rs * (dnv - jnp.mean(dnv, axis=-1, keepdims=True)
                        - nv * jnp.mean(dnv * nv, axis=-1, keepdims=True))
            return dca, dl, nv

        dlg_acc = jnp.zeros((1, c), F32)
        dlb_acc = jnp.zeros((1, c), F32)
        for q in range(ts // rc):
            r0 = q * rc
            dca, dl, nv = ln_bwd(ca_ref[r0:r0 + rc, :], d_ref[r0:r0 + rc, 0:c])
            extdca[r0:r0 + rc, :] = dca
            dlg_acc = dlg_acc + jnp.sum(dl * nv, axis=0, keepdims=True)
            dlb_acc = dlb_acc + jnp.sum(dl, axis=0, keepdims=True)
            extdcb[r0:r0 + rc, :] = d_ref[r0:r0 + rc, c:2 * c] * z_ref[r0:r0 + rc, 2 * c:3 * c]
        dca_n, _, _ = ln_bwd(can_ref[...], dn_ref[:, 0:c])
        extdca[ts:ts + hb, :] = dca_n * keep_n
        extdcb[ts:ts + hb, :] = dn_ref[:, c:2 * c] * zn_ref[:, 2 * c:3 * c] * keep_n
        dlg_ref[...] += dlg_acc
        dlb_ref[...] += dlb_acc

        for q in range(ts // rc):
            r0 = q * rc
            zr = z_ref[r0:r0 + rc, :]
            dga = _conv_taps(extdca, wa_ref, ta, ta - 1, r0, rc, reverse=True)
            sg = _sigmoid(zr[:, c:2 * c])
            dz_ref[r0:r0 + rc, 0:c] = (dga * sg).astype(BF16)
            dz_ref[r0:r0 + rc, c:2 * c] = (dga * zr[:, 0:c] * sg * (1.0 - sg)).astype(BF16)
            cbc = _conv_taps(extb, wb_ref, 3, hb - 2, r0, rc)
            dz_ref[r0:r0 + rc, 2 * c:3 * c] = (d_ref[r0:r0 + rc, c:2 * c] * cbc).astype(BF16)
            dcb = _conv_taps(extdcb, wb_ref, 3, 2, r0, rc, reverse=True)
            dz_ref[r0:r0 + rc, 3 * c:4 * c] = (dcb * zr[:, 4 * c:5 * c]).astype(BF16)
            dz_ref[r0:r0 + rc, 4 * c:5 * c] = (dcb * zr[:, 3 * c:4 * c]).astype(BF16)

        dca_t = extdca[0:ts, :]
        for k in range(ta):
            dwa_ref[k:k + 1, :] += jnp.sum(dca_t * exta[pl.ds(hb - (ta - 1) + k, ts), :], axis=0, keepdims=True)
        dcb_t = extdcb[0:ts, :]
        for k in range(3):
            dwb_ref[k:k + 1, :] += jnp.sum(dcb_t * extb[pl.ds(hb - 2 + k, ts), :], axis=0, keepdims=True)

    def tile(w):
        return pl.BlockSpec((ts, w), lambda i: (i, 0))

    def prev(w):
        return pl.BlockSpec((hb, w), lambda i: (_prev_blk(i, ts, hb), 0))

    def nxt(w):
        return pl.BlockSpec((hb, w), lambda i: (_next_blk(i, ts, hb, s), 0))

    def const(r, w):
        return pl.BlockSpec((r, w), lambda i: (0, 0))

    return pl.pallas_call(
        body, name=name, grid=(s // ts,),
        in_specs=[tile(5 * c), prev(5 * c), nxt(5 * c), tile(c), nxt(c), tile(2 * c), nxt(2 * c),
                  const(32, c), const(1, c), const(1, c), const(8, c)],
        out_specs=[tile(5 * c), const(32, c), const(8, c), const(1, c), const(1, c)],
        out_shape=[jax.ShapeDtypeStruct((s, 5 * c), BF16), jax.ShapeDtypeStruct((32, c), F32),
                   jax.ShapeDtypeStruct((8, c), F32), jax.ShapeDtypeStruct((1, c), F32),
                   jax.ShapeDtypeStruct((1, c), F32)],
        scratch_shapes=[pltpu.VMEM((hb + ts, c), F32), pltpu.VMEM((hb + ts, c), F32),
                        pltpu.VMEM((ts + hb, c), F32), pltpu.VMEM((ts + hb, c), F32)],
        compiler_params=_params(("arbitrary",)),
    )(z, z, z, ca, ca, dab, dab, conv_a, ln_g, ln_b, conv_b)


def _ffn_fwd(up, wc, *, name, ts=512, rc=64):
    s, n2 = up.shape
    cw = FFN_CW
    nj = n2 // (2 * cw)
    hb = HALO_S

    def body(u_ref, up_ref, w_ref, act_ref, ext):
        keep = jnp.where(pl.program_id(0) > 0, 1.0, 0.0)
        ext[0:hb, :] = up_ref[...].astype(F32) * keep
        ext[hb:hb + ts, :] = u_ref[...].astype(F32)
        for q in range(ts // rc):
            r0 = q * rc
            u = _conv_taps(ext, w_ref, 3, hb - 2, r0, rc)
            gg = u[:, 0:cw]
            act_ref[r0:r0 + rc, :] = (gg * _sigmoid(gg) * u[:, cw:2 * cw]).astype(BF16)

    return pl.pallas_call(
        body, name=name, grid=(s // ts, nj),
        in_specs=[pl.BlockSpec((ts, 2 * cw), lambda i, j: (i, j)),
                  pl.BlockSpec((hb, 2 * cw), lambda i, j: (_prev_blk(i, ts, hb), j)),
                  pl.BlockSpec((8, 2 * cw), lambda i, j: (0, j))],
        out_specs=pl.BlockSpec((ts, cw), lambda i, j: (i, j)),
        out_shape=jax.ShapeDtypeStruct((s, nj * cw), BF16),
        scratch_shapes=[pltpu.VMEM((hb + ts, 2 * cw), F32)],
        compiler_params=_params(("parallel", "parallel")),
    )(up, up, wc)


def _ffn_bwd(up, dact, wc, *, name, ts=512, rc=64):
    s, n2 = up.shape
    cw = FFN_CW
    nj = n2 // (2 * cw)
    hb = HALO_S
    n_i = s // ts

    def body(u_ref, up_ref, un_ref, d_ref, dn_ref, w_ref, dup_ref, dw_ref, ext, extdu):
        i = pl.program_id(1)
        keep_p = jnp.where(i > 0, 1.0, 0.0)
        keep_n = jnp.where(i < n_i - 1, 1.0, 0.0)

        @pl.when(i == 0)
        def _():
            dw_ref[...] = jnp.zeros_like(dw_ref)

        ext[0:hb, :] = up_ref[...].astype(F32) * keep_p
        ext[hb:hb + ts, :] = u_ref[...].astype(F32)
        ext[hb + ts:hb + ts + hb, :] = un_ref[...].astype(F32)

        def du_rows(r0, rows, dav):
            u = _conv_taps(ext, w_ref, 3, hb - 2, r0, rows)
            gg = u[:, 0:cw]
            sg = _sigmoid(gg)
            extdu[r0:r0 + rows, 0:cw] = dav * u[:, cw:2 * cw] * (sg * (1.0 + gg * (1.0 - sg)))
            extdu[r0:r0 + rows, cw:2 * cw] = dav * (gg * sg)

        for q in range(ts // rc):
            du_rows(q * rc, rc, d_ref[q * rc:q * rc + rc, :].astype(F32))
        du_rows(ts, 8, dn_ref[0:8, :].astype(F32) * keep_n)

        for q in range(ts // rc):
            r0 = q * rc
            dup_ref[r0:r0 + rc, :] = _conv_taps(extdu, w_ref, 3, 2, r0, rc, reverse=True).astype(BF16)
        du_t = extdu[0:ts, :]
        for k in range(3):
            dw_ref[k:k + 1, :] += jnp.sum(du_t * ext[pl.ds(hb - 2 + k, ts), :], axis=0, keepdims=True)

    return pl.pallas_call(
        body, name=name, grid=(nj, n_i),
        in_specs=[pl.BlockSpec((ts, 2 * cw), lambda j, i: (i, j)),
                  pl.BlockSpec((hb, 2 * cw), lambda j, i: (_prev_blk(i, ts, hb), j)),
                  pl.BlockSpec((hb, 2 * cw), lambda j, i: (_next_blk(i, ts, hb, s), j)),
                  pl.BlockSpec((ts, cw), lambda j, i: (i, j)),
                  pl.BlockSpec((hb, cw), lambda j, i: (_next_blk(i, ts, hb, s), j)),
                  pl.BlockSpec((8, 2 * cw), lambda j, i: (0, j))],
        out_specs=[pl.BlockSpec((ts, 2 * cw), lambda j, i: (i, j)),
                   pl.BlockSpec((8, 2 * cw), lambda j, i: (0, j))],
        out_shape=[jax.ShapeDtypeStruct((s, n2), BF16), jax.ShapeDtypeStruct((8, n2), F32)],
        scratch_shapes=[pltpu.VMEM((hb + ts + hb, 2 * cw), F32), pltpu.VMEM((ts + 8, 2 * cw), F32)],
        compiler_params=_params(("parallel", "arbitrary")),
    )(up, up, up, dact, dact, wc)


def _pool_counts(i, ts, rows, window):
    t = lax.broadcasted_iota(jnp.int32, (rows, 1), 0) + i * ts + 1
    return jnp.minimum(t, window).astype(F32)


def _pool_fwd(x, g, *, name, ts=256):
    s, d = x.shape
    hb = HALO_S
    pg = POOL_GROUP

    def body(x_ref, xp_ref, g_ref, p_ref, ext):
        i = pl.program_id(0)
        keep = jnp.where(i > 0, 1.0, 0.0)
        gv = g_ref[...]

        def norm(xv):
            return xv * lax.rsqrt(jnp.mean(xv * xv, axis=-1, keepdims=True) + RMS_EPS) * gv

        ext[0:hb, :] = norm(xp_ref[...]) * keep
        ext[hb:hb + ts, :] = norm(x_ref[...])
        for gi, w in enumerate(POOL_WINDOWS):
            cols = slice(gi * pg, (gi + 1) * pg)
            h = ext[hb:hb + ts, cols]
            acc = h
            for q in range(1, w):
                acc = acc + ext[pl.ds(hb - q, ts), cols]
            p_ref[:, cols] = (acc / _pool_counts(i, ts, ts, w) - h).astype(BF16)

    return pl.pallas_call(
        body, name=name, grid=(s // ts,),
        in_specs=[pl.BlockSpec((ts, d), lambda i: (i, 0)),
                  pl.BlockSpec((hb, d), lambda i: (_prev_blk(i, ts, hb), 0)),
                  pl.BlockSpec((1, d), lambda i: (0, 0))],
        out_specs=pl.BlockSpec((ts, d), lambda i: (i, 0)),
        out_shape=jax.ShapeDtypeStruct((s, d), BF16),
        scratch_shapes=[pltpu.VMEM((hb + ts, d), F32)],
        compiler_params=_params(("parallel",)),
    )(x, x, g)


def _pool_mm_fwd(p, w_pool, scale, x, *, name, ts=512):
    s, d = x.shape
    pg = POOL_GROUP

    def body(p_ref, w_ref, s_ref, x_ref, o_ref, yu_ref):
        yu = jnp.dot(p_ref[...], w_ref[...], preferred_element_type=F32)
        yu_ref[...] = yu.astype(BF16)
        o_ref[...] = x_ref[...] + yu * s_ref[...]

    blk = pl.BlockSpec((ts, pg), lambda i, gi: (i, gi))
    return pl.pallas_call(
        body, name=name, grid=(s // ts, d // pg),
        in_specs=[blk, pl.BlockSpec((None, pg, pg), lambda i, gi: (gi, 0, 0)),
                  pl.BlockSpec((1, pg), lambda i, gi: (0, gi)), blk],
        out_specs=[blk, blk],
        out_shape=[jax.ShapeDtypeStruct((s, d), F32), jax.ShapeDtypeStruct((s, d), BF16)],
        compiler_params=_params(("parallel", "parallel")),
    )(p, w_pool, scale, x)


def _pool_mm_bwd(dres, w_pool, scale, yu, *, name, ts=512):
    s, d = dres.shape
    pg = POOL_GROUP

    def body(d_ref, w_ref, s_ref, yu_ref, dyc_ref, dp_ref, ds_ref):
        @pl.when(pl.program_id(1) == 0)
        def _():
            ds_ref[...] = jnp.zeros_like(ds_ref)

        dv = d_ref[...]
        dyc = (dv * s_ref[...]).astype(BF16)
        dyc_ref[...] = dyc
        dp_ref[...] = lax.dot_general(dyc, w_ref[...], (((1,), (1,)), ((), ())), preferred_element_type=F32)
        ds_ref[...] += jnp.sum(dv * yu_ref[...].astype(F32), axis=0, keepdims=True)

    blk = pl.BlockSpec((ts, pg), lambda gi, i: (i, gi))
    vec = pl.BlockSpec((1, pg), lambda gi, i: (0, gi))
    return pl.pallas_call(
        body, name=name, grid=(d // pg, s // ts),
        in_specs=[blk, pl.BlockSpec((None, pg, pg), lambda gi, i: (gi, 0, 0)), vec, blk],
        out_specs=[blk, blk, vec],
        out_shape=[jax.ShapeDtypeStruct((s, d), BF16), jax.ShapeDtypeStruct((s, d), F32),
                   jax.ShapeDtypeStruct((1, d), F32)],
        compiler_params=_params(("parallel", "arbitrary")),
    )(dres, w_pool, scale, yu)


def _pool_bwd(dp, x, g, dres, *, name, ts=256):
    s, d = x.shape
    hb = HALO_S
    pg = POOL_GROUP
    n_i = s // ts

    def body(dp_ref, dpn_ref, x_ref, g_ref, dres_ref, dx_ref, dg_ref, ext, dh):
        i = pl.program_id(0)
        keep_n = jnp.where(i < n_i - 1, 1.0, 0.0)

        @pl.when(i == 0)
        def _():
            dg_ref[...] = jnp.zeros_like(dg_ref)

        for gi, w in enumerate(POOL_WINDOWS):
            cols = slice(gi * pg, (gi + 1) * pg)
            ext[0:ts, cols] = dp_ref[:, cols] / _pool_counts(i, ts, ts, w)
            ext[ts:ts + hb, cols] = dpn_ref[:, cols] / _pool_counts(i + 1, ts, hb, w) * keep_n
        for gi, w in enumerate(POOL_WINDOWS):
            cols = slice(gi * pg, (gi + 1) * pg)
            acc = ext[0:ts, cols]
            for q in range(1, w):
                acc = acc + ext[pl.ds(q, ts), cols]
            dh[:, cols] = acc - dp_ref[:, cols]
        dx, dgr = _rms_bwd_rows(x_ref[...], g_ref[...], dh[...])
        dx_ref[...] = dres_ref[...] + dx
        dg_ref[...] += jnp.sum(dgr, axis=0, keepdims=True)

    row = pl.BlockSpec((ts, d), lambda i: (i, 0))
    vec = pl.BlockSpec((1, d), lambda i: (0, 0))
    return pl.pallas_call(
        body, name=name, grid=(n_i,),
        in_specs=[row, pl.BlockSpec((hb, d), lambda i: (_next_blk(i, ts, hb, s), 0)), row, vec, row],
        out_specs=[row, vec],
        out_shape=[jax.ShapeDtypeStruct((s, d), F32), jax.ShapeDtypeStruct((1, d), F32)],
        scratch_shapes=[pltpu.VMEM((ts + hb, d), F32), pltpu.VMEM((ts, d), F32)],
        compiler_params=_params(("arbitrary",)),
    )(dp, dp, x, g, dres)


def _pad_rows(w, rows):
    return jnp.pad(w, ((0, rows - w.shape[0]), (0, 0)))


def _interleave(w):
    *lead, n = w.shape
    nj = n // (2 * FFN_CW)
    return jnp.swapaxes(w.reshape(*lead, 2, nj, FFN_CW), -3, -2).reshape(*lead, n)


def _deinterleave(w):
    *lead, n = w.shape
    nj = n // (2 * FFN_CW)
    return jnp.swapaxes(w.reshape(*lead, nj, 2, FFN_CW), -3, -2).reshape(*lead, n)


def _ffn_layer_fwd(x, nf, w_up, wc, w_down, tag):
    h, up = _rms_mm(x, nf, w_up, name=f"ffn{tag}_up", out_dtype=BF16)
    act = _ffn_fwd(up, wc, name=f"ffn{tag}_mid")
    x_out = _mm(act, w_down, add=x, name=f"ffn{tag}_down", tk=256)
    return x_out, (h, up, act)


def _ffn_layer_bwd(d, x, nf, w_up, wc, w_down, saved, tag):
    h, up, act = saved
    dact = _mm(d, w_down, tb=True, out_dtype=BF16, name=f"ffn{tag}_dact", tn=256, tk=1024)
    dw_down = _mm(act, d, ta=True, out_dtype=BF16, name=f"ffn{tag}_dwdown", tm=256, tn=1024, tk=512)
    dup, dwc = _ffn_bwd(up, dact, wc, name=f"ffn{tag}_midbwd")
    dh = _mm(dup, w_up, tb=True, name=f"ffn{tag}_dh", tn=1024, tk=512)
    dw_up = _mm(h, dup, ta=True, out_dtype=BF16, name=f"ffn{tag}_dwup", tm=1024, tn=512, tk=512)
    dx, dnf = _rms_bwd(x, nf, dh, d, name=f"ffn{tag}_rmsbwd")
    return dx, dnf, dw_up, dwc, dw_down


def _local_step(x, target, w):
    wa = _pad_rows(w["conv_a"], 32)
    wb = _pad_rows(w["conv_b"], 8)
    wc = [_pad_rows(w["conv_ffn"][l], 8) for l in range(2)]

    h0, z = _rms_mm(x, w["norm_mix_even"], w["w_in"], name="mix0_in", out_dtype=F32)
    ab, ca = _mix0_fwd(z, wa, w["ln_a_g"], w["ln_a_b"], wb, name="mix0_mid")
    x1 = _mm(ab, w["w_out"], add=x, name="mix0_out", tk=1024)
    x2, ffn0 = _ffn_layer_fwd(x1, w["norm_ffn"][0:1], w["w_up"][0], wc[0], w["w_down"][0], 0)
    p = _pool_fwd(x2, w["norm_mix_odd"], name="pool_mid")
    x3, yu = _pool_mm_fwd(p, w["w_pool"], w["pool_scale"], x2, name="pool_mm")
    x4, ffn1 = _ffn_layer_fwd(x3, w["norm_ffn"][1:2], w["w_up"][1], wc[1], w["w_down"][1], 1)
    loss, d4, g_norm_final = _final_loss(x4, w["norm_final"], target, name="final_loss")

    d3, g_nf1, g_up1, g_wc1, g_down1 = _ffn_layer_bwd(
        d4, x3, w["norm_ffn"][1:2], w["w_up"][1], wc[1], w["w_down"][1], ffn1, 1)
    dyc, dp, g_scale = _pool_mm_bwd(d3, w["w_pool"], w["pool_scale"], yu, name="pool_mm_bwd")
    pg = POOL_GROUP
    g_pool = jnp.stack([
        _mm(p[:, gi * pg:(gi + 1) * pg], dyc[:, gi * pg:(gi + 1) * pg], ta=True, out_dtype=BF16,
            name=f"pool_dw{gi}", tk=1024) for gi in range(len(POOL_WINDOWS))])
    d2, g_nmo = _pool_bwd(dp, x2, w["norm_mix_odd"], d3, name="pool_midbwd")
    d1, g_nf0, g_up0, g_wc0, g_down0 = _ffn_layer_bwd(
        d2, x1, w["norm_ffn"][0:1], w["w_up"][0], wc[0], w["w_down"][0], ffn0, 0)
    dab = _mm(d1, w["w_out"], tb=True, name="mix0_dab", tn=1024, tk=1024)
    g_out = _mm(ab, d1, ta=True, out_dtype=BF16, name="mix0_dwout", tm=1024, tn=1024, tk=512)
    dz, g_wa, g_wb, g_lg, g_lb = _mix0_bwd(z, ca, dab, wa, w["ln_a_g"], w["ln_a_b"], wb, name="mix0_midbwd")
    dh0 = _mm(dz, w["w_in"], tb=True, name="mix0_dh", tn=1024, tk=512)
    g_in = _mm(h0, dz, ta=True, out_dtype=BF16, name="mix0_dwin", tm=1024, tn=512, tk=512)
    dx, g_nme = _rms_bwd(x, w["norm_mix_even"], dh0, d1, name="mix0_rmsbwd")

    grads = {
        "norm_mix_even": g_nme, "w_in": g_in, "conv_a": g_wa[:A_TAPS], "ln_a_g": g_lg, "ln_a_b": g_lb,
        "conv_b": g_wb[:3], "w_out": g_out, "norm_mix_odd": g_nmo, "w_pool": g_pool, "pool_scale": g_scale,
        "norm_ffn": jnp.concatenate([g_nf0, g_nf1], axis=0),
        "w_up": jnp.stack([g_up0, g_up1]), "conv_ffn": jnp.stack([g_wc0[:3], g_wc1[:3]]),
        "w_down": jnp.stack([g_down0, g_down1]), "norm_final": g_norm_final,
    }
    return loss[0, 0], dx, grads


def _my_pos():
    return lax.axis_index("x"), lax.axis_index("y"), lax.axis_index("c")


def _flip(pos, r):
    x, y, c = pos
    return (1 - x if r & 4 else x, 1 - y if r & 2 else y, 1 - c if r & 1 else c)


def _dev_index(pos):
    return 4 * pos[0] + 2 * pos[1] + pos[2]


def _exchange(arrays, scatter, *, name):
    n = len(arrays)

    def body(*refs):
        ins, outs = refs[:n], refs[n:2 * n]
        send_sems, recv_sems, local_sems = refs[2 * n:]
        pos = _my_pos()
        me = _dev_index(pos)
        own = []
        for a in range(n):
            cp = pltpu.make_async_copy(ins[a].at[me] if scatter[a] else ins[a], outs[a].at[me], local_sems.at[a])
            cp.start()
            own.append(cp)
        sent = []
        for a in range(n):
            for r in range(1, N_DEV):
                peer = _flip(pos, r)
                src = ins[a].at[_dev_index(peer)] if scatter[a] else ins[a]
                cp = pltpu.make_async_remote_copy(
                    src_ref=src, dst_ref=outs[a].at[me], send_sem=send_sems.at[a, r - 1],
                    recv_sem=recv_sems.at[a, r - 1], device_id=peer, device_id_type=pl.DeviceIdType.MESH)
                cp.start()
                sent.append(cp)
        for a in range(n):
            for r in range(1, N_DEV):
                peer = _flip(pos, r)
                src = ins[a].at[me] if scatter[a] else ins[a]
                pltpu.make_async_remote_copy(
                    src_ref=src, dst_ref=outs[a].at[_dev_index(peer)], send_sem=send_sems.at[a, r - 1],
                    recv_sem=recv_sems.at[a, r - 1], device_id=peer, device_id_type=pl.DeviceIdType.MESH).wait_recv()
        for cp in sent:
            cp.wait_send()
        for cp in own:
            cp.wait()

    out_shape = [jax.ShapeDtypeStruct(a.shape if sc else (N_DEV,) + a.shape, a.dtype)
                 for a, sc in zip(arrays, scatter)]
    any_spec = pl.BlockSpec(memory_space=pl.ANY)
    return pl.pallas_call(
        body, name=name, in_specs=[any_spec] * n, out_specs=[any_spec] * n, out_shape=out_shape,
        scratch_shapes=[pltpu.SemaphoreType.DMA((n, N_DEV - 1)), pltpu.SemaphoreType.DMA((n, N_DEV - 1)),
                        pltpu.SemaphoreType.DMA((n,))],
    )(*arrays)


def _adamw(parts, w, m, v, *, name, tr):
    r, c = w.shape
    assert r % tr == 0

    def body(p_ref, w_ref, m_ref, v_ref, g_ref, d_ref, mo_ref, vo_ref):
        g = p_ref[0].astype(F32)
        for k in range(1, N_DEV):
            g = g + p_ref[k].astype(F32)
        mn = ADAM_B1 * m_ref[...] + (1.0 - ADAM_B1) * g
        vn = ADAM_B2 * v_ref[...] + (1.0 - ADAM_B2) * (g * g)
        m_hat = mn / (1.0 - ADAM_B1 ** ADAM_STEP)
        v_hat = vn / (1.0 - ADAM_B2 ** ADAM_STEP)
        g_ref[...] = g
        d_ref[...] = -ADAM_LR * (m_hat / (jnp.sqrt(v_hat) + ADAM_EPS) + ADAM_WD * w_ref[...])
        mo_ref[...] = mn
        vo_ref[...] = vn

    blk = pl.BlockSpec((tr, c), lambda i: (i, 0))
    return pl.pallas_call(
        body, name=name, grid=(r // tr,),
        in_specs=[pl.BlockSpec((N_DEV, tr, c), lambda i: (0, i, 0)), blk, blk, blk],
        out_specs=[blk] * 4, out_shape=[jax.ShapeDtypeStruct((r, c), F32)] * 4,
        compiler_params=_params(("parallel",)),
    )(parts, w, m, v)


def _pack(parts, lead=()):
    flat = jnp.concatenate([p.reshape(lead + (-1,)) for p in parts], axis=-1)
    n = flat.shape[-1]
    rows = -(-n // (8 * 128)) * 8
    flat = jnp.pad(flat, [(0, 0)] * len(lead) + [(0, rows * 128 - n)])
    return flat.reshape(lead + (rows, 128))


def _unpack(slab, shapes):
    flat = slab.reshape(-1)
    out, off = [], 0
    for shp in shapes:
        size = 1
        for dim in shp:
            size *= dim
        out.append(flat[off:off + size].reshape(shp))
        off += size
    return out


def _to_dev_major(g, axis):
    shp = g.shape
    g = g.reshape(shp[:axis] + (N_DEV, shp[axis] // N_DEV) + shp[axis + 1:])
    return jnp.moveaxis(g, axis, 0)


def _from_dev_major(g, axis):
    g = jnp.moveaxis(g, 0, axis)
    shp = g.shape
    return g.reshape(shp[:axis] + (shp[axis] * shp[axis + 1],) + shp[axis + 2:])


SMALL_SHARDED = ("conv_a", "conv_b", "norm_mix_odd", "pool_scale", "conv_ffn_w")
SMALL_REPLICATED = ("norm_mix_even", "ln_a_g", "ln_a_b", "norm_ffn", "norm_final")
BIG = ("w_in", "w_out", "w_pool", "w_up", "w_down")
BIG_ROWS = {"w_in": 512, "w_out": 128, "w_pool": 128, "w_up": 256, "w_down": 352}


def kernel(x, norm_mix_even, w_in, conv_a, ln_a_g, ln_a_b, conv_b, w_out, norm_mix_odd, w_pool, pool_scale, norm_ffn, w_up, conv_ffn_w, w_down, norm_final, loss_target, m_norm_mix_even, m_w_in, m_conv_a, m_ln_a_g, m_ln_a_b, m_conv_b, m_w_out, m_norm_mix_odd, m_w_pool, m_pool_scale, m_norm_ffn, m_w_up, m_conv_ffn_w, m_w_down, m_norm_final, v_norm_mix_even, v_w_in, v_conv_a, v_ln_a_g, v_ln_a_b, v_conv_b, v_w_out, v_norm_mix_odd, v_w_pool, v_pool_scale, v_norm_ffn, v_w_up, v_conv_ffn_w, v_w_down, v_norm_final):
    names = ("norm_mix_even", "w_in", "conv_a", "ln_a_g", "ln_a_b", "conv_b", "w_out", "norm_mix_odd", "w_pool",
             "pool_scale", "norm_ffn", "w_up", "conv_ffn_w", "w_down", "norm_final")
    wts = dict(zip(names, (norm_mix_even, w_in, conv_a, ln_a_g, ln_a_b, conv_b, w_out, norm_mix_odd, w_pool,
                           pool_scale, norm_ffn, w_up, conv_ffn_w, w_down, norm_final)))
    mom = dict(zip(names, (m_norm_mix_even, m_w_in, m_conv_a, m_ln_a_g, m_ln_a_b, m_conv_b, m_w_out, m_norm_mix_odd,
                           m_w_pool, m_pool_scale, m_norm_ffn, m_w_up, m_conv_ffn_w, m_w_down, m_norm_final)))
    var = dict(zip(names, (v_norm_mix_even, v_w_in, v_conv_a, v_ln_a_g, v_ln_a_b, v_conv_b, v_w_out, v_norm_mix_odd,
                           v_w_pool, v_pool_scale, v_norm_ffn, v_w_up, v_conv_ffn_w, v_w_down, v_norm_final)))
    d = x.shape[-1]

    def shard2d(t):
        return {"w_in": t["w_in"][0], "w_out": t["w_out"][0], "w_pool": t["w_pool"].reshape(-1, POOL_GROUP),
                "w_up": t["w_up"].reshape(-1, t["w_up"].shape[-1]), "w_down": t["w_down"].reshape(-1, d)}

    w2d, m2d, v2d = shard2d(wts), shard2d(mom), shard2d(var)

    small_w = _pack([wts[k] for k in SMALL_SHARDED])
    gathered = _exchange([w2d[k].astype(BF16) for k in BIG] + [small_w], [False] * (len(BIG) + 1),
                         name="weights_allgather")
    g_in, g_out, g_pool, g_up, g_down, g_small = gathered
    per_dev = g_small.reshape(N_DEV, -1)
    sizes = [wts[k].size for k in SMALL_SHARDED]
    offs = [sum(sizes[:i]) for i in range(len(sizes))]
    small_full = {k: per_dev[:, o:o + n_].reshape((N_DEV,) + wts[k].shape)
                  for k, o, n_ in zip(SMALL_SHARDED, offs, sizes)}
    n_up = w_up.shape[-1]
    full = {
        "norm_mix_even": norm_mix_even, "ln_a_g": ln_a_g, "ln_a_b": ln_a_b, "norm_ffn": norm_ffn,
        "norm_final": norm_final[None],
        "w_in": _from_dev_major(g_in, 1),
        "w_out": g_out.reshape(-1, d),
        "w_pool": _from_dev_major(g_pool.reshape(N_DEV, len(POOL_WINDOWS), -1, POOL_GROUP), 1),
        "w_up": _interleave(_from_dev_major(g_up.reshape(N_DEV, 2, d, n_up), 2)),
        "w_down": _from_dev_major(g_down.reshape(N_DEV, 2, -1, d), 1),
        "conv_a": _from_dev_major(small_full["conv_a"][:, 0], 1),
        "conv_b": _from_dev_major(small_full["conv_b"][:, 0], 1),
        "norm_mix_odd": _from_dev_major(small_full["norm_mix_odd"], 1),
        "pool_scale": _from_dev_major(small_full["pool_scale"], 1),
        "conv_ffn": _interleave(_from_dev_major(small_full["conv_ffn_w"], 2)),
    }

    loss, dx, g = _local_step(x[0], loss_target[0], full)

    big_parts = {
        "w_in": _to_dev_major(g["w_in"], 1),
        "w_out": g["w_out"].reshape(N_DEV, -1, d),
        "w_pool": _to_dev_major(g["w_pool"], 1).reshape(N_DEV, -1, POOL_GROUP),
        "w_up": _to_dev_major(_deinterleave(g["w_up"]), 2).reshape(N_DEV, -1, n_up),
        "w_down": _to_dev_major(g["w_down"], 1).reshape(N_DEV, -1, d),
    }
    small_parts = _pack([
        _to_dev_major(g["conv_a"], 1), _to_dev_major(g["conv_b"], 1), _to_dev_major(g["norm_mix_odd"], 1),
        _to_dev_major(g["pool_scale"], 1), _to_dev_major(_deinterleave(g["conv_ffn"]), 2)], lead=(N_DEV,))
    repl_parts = _pack([g[k] for k in SMALL_REPLICATED])
    landed = _exchange([big_parts[k] for k in BIG] + [small_parts, repl_parts],
                       [True] * (len(BIG) + 1) + [False], name="grads_exchange")

    out = {}
    for k, parts in zip(BIG, landed[:len(BIG)]):
        res = _adamw(parts, w2d[k], m2d[k], v2d[k], name=f"adamw_{k}", tr=BIG_ROWS[k])
        out[k] = [t.reshape(wts[k].shape) for t in res]
    res = _adamw(landed[len(BIG)], small_w, _pack([mom[k] for k in SMALL_SHARDED]),
                 _pack([var[k] for k in SMALL_SHARDED]), name="adamw_small", tr=small_w.shape[0])
    unpacked = [_unpack(t, [wts[k].shape for k in SMALL_SHARDED]) for t in res]
    for j, k in enumerate(SMALL_SHARDED):
        out[k] = [u[j] for u in unpacked]
    repl_w = _pack([wts[k] for k in SMALL_REPLICATED])
    res = _adamw(landed[len(BIG) + 1], repl_w, _pack([mom[k] for k in SMALL_REPLICATED]),
                 _pack([var[k] for k in SMALL_REPLICATED]), name="adamw_replicated", tr=repl_w.shape[0])
    unpacked = [_unpack(t, [wts[k].shape for k in SMALL_REPLICATED]) for t in res]
    for j, k in enumerate(SMALL_REPLICATED):
        out[k] = [u[j] for u in unpacked]

    loss = lax.psum(loss, ("x", "y", "c"))
    return (loss, dx[None], *[out[k][0] for k in names], *[out[k][1] for k in names],
            *[out[k][2] for k in names], *[out[k][3] for k in names])
```

```python
import jax
import jax.numpy as jnp
from jax import lax
from jax.experimental import pallas as pl
from jax.experimental.pallas import tpu as pltpu

F32 = jnp.float32
BF16 = jnp.bfloat16

RMS_EPS = 1e-6
LN_EPS = 1e-5
ADAM_LR = 0.001
ADAM_B1 = 0.9
ADAM_B2 = 0.999
ADAM_EPS = 1e-08
ADAM_WD = 0.01
ADAM_STEP = 10

N_DEV = 8
N_PAIR = N_DEV // 2
A_WIDTH = 512
A_TAPS = 31
POOL_WINDOWS = (2, 4, 8, 16)
POOL_GROUP = 256
HALO_A = 32
HALO_S = 16
VMEM_LIMIT = 56 * 1024 * 1024


def _params(sem, vmem=VMEM_LIMIT):
    return pltpu.CompilerParams(dimension_semantics=sem, vmem_limit_bytes=vmem)


def _sigmoid(x):
    return 1.0 / (1.0 + jnp.exp(-x))


def _prev_blk(i, ts, hb):
    return jnp.maximum(i * (ts // hb) - 1, 0)


def _next_blk(i, ts, hb, s):
    return jnp.minimum((i + 1) * (ts // hb), s // hb - 1)


def _mm(a, b, *, name, ta=False, tb=False, add=None, out_dtype=F32, tm=512, tn=512, tk=None):
    m, k = (a.shape[1], a.shape[0]) if ta else a.shape
    n = b.shape[0] if tb else b.shape[1]
    tk = k if tk is None else tk
    tm, tn, tk = min(tm, m), min(tn, n), min(tk, k)
    assert m % tm == 0 and n % tn == 0 and k % tk == 0, (name, m, n, k, tm, tn, tk)
    nk = k // tk
    dims = (((0,) if ta else (1,), (1,) if tb else (0,)), ((), ()))

    def body(*refs):
        a_ref, b_ref = refs[0], refs[1]
        add_ref = refs[2] if add is not None else None
        o_ref = refs[3] if add is not None else refs[2]
        part = lax.dot_general(a_ref[...].astype(BF16), b_ref[...].astype(BF16), dims, preferred_element_type=F32)

        def finish(r):
            if add_ref is not None:
                r = r + add_ref[...]
            o_ref[...] = r.astype(out_dtype)

        if nk == 1:
            finish(part)
            return
        acc_ref = refs[-1]
        kk = pl.program_id(2)

        @pl.when(kk == 0)
        def _():
            acc_ref[...] = part

        @pl.when(kk > 0)
        def _():
            acc_ref[...] += part

        @pl.when(kk == nk - 1)
        def _():
            finish(acc_ref[...])

    a_spec = pl.BlockSpec((tk, tm), lambda i, j, kk: (kk, i)) if ta else pl.BlockSpec((tm, tk), lambda i, j, kk: (i, kk))
    b_spec = pl.BlockSpec((tn, tk), lambda i, j, kk: (j, kk)) if tb else pl.BlockSpec((tk, tn), lambda i, j, kk: (kk, j))
    in_specs = [a_spec, b_spec]
    args = [a, b]
    if add is not None:
        in_specs.append(pl.BlockSpec((tm, tn), lambda i, j, kk: (i, j)))
        args.append(add)
    return pl.pallas_call(
        body, name=name, grid=(m // tm, n // tn, nk),
        in_specs=in_specs, out_specs=pl.BlockSpec((tm, tn), lambda i, j, kk: (i, j)),
        out_shape=jax.ShapeDtypeStruct((m, n), out_dtype),
        scratch_shapes=[pltpu.VMEM((tm, tn), F32)] if nk > 1 else [],
        compiler_params=_params(("parallel", "parallel", "arbitrary")),
    )(*args)


def _rms_rows(xv, gv):
    return xv * lax.rsqrt(jnp.mean(xv * xv, axis=-1, keepdims=True) + RMS_EPS) * gv


def _rms_mm(x, g, w, *, name, out_dtype, tm=1024, tn=512):
    s, d = x.shape
    n = w.shape[1]
    tm = min(tm, s)
    assert s % tm == 0 and n % tn == 0

    def body(x_ref, g_ref, w_ref, h_ref, z_ref, hs_ref):
        @pl.when(pl.program_id(1) == 0)
        def _():
            hv = _rms_rows(x_ref[...], g_ref[...]).astype(BF16)
            hs_ref[...] = hv
            h_ref[...] = hv

        z_ref[...] = jnp.dot(hs_ref[...], w_ref[...], preferred_element_type=F32).astype(out_dtype)

    return pl.pallas_call(
        body, name=name, grid=(s // tm, n // tn),
        in_specs=[pl.BlockSpec((tm, d), lambda i, j: (i, 0)),
                  pl.BlockSpec((1, d), lambda i, j: (0, 0)),
                  pl.BlockSpec((d, tn), lambda i, j: (0, j))],
        out_specs=[pl.BlockSpec((tm, d), lambda i, j: (i, 0)),
                   pl.BlockSpec((tm, tn), lambda i, j: (i, j))],
        out_shape=[jax.ShapeDtypeStruct((s, d), BF16), jax.ShapeDtypeStruct((s, n), out_dtype)],
        scratch_shapes=[pltpu.VMEM((tm, d), BF16)],
        compiler_params=_params(("parallel", "arbitrary")),
    )(x, g, w)


def _rms_bwd_rows(xv, gv, dh):
    r = lax.rsqrt(jnp.mean(xv * xv, axis=-1, keepdims=True) + RMS_EPS)
    xh = xv * r
    dn = dh * gv
    dx = r * (dn - xh * jnp.mean(dn * xh, axis=-1, keepdims=True))
    return dx, dh * xh


def _rms_bwd(x, g, dh, dres, *, name, ts=256):
    s, d = x.shape
    ts = min(ts, s)

    def body(x_ref, g_ref, dh_ref, dres_ref, dx_ref, dxb_ref, dg_ref):
        @pl.when(pl.program_id(0) == 0)
        def _():
            dg_ref[...] = jnp.zeros_like(dg_ref)

        dx, dgr = _rms_bwd_rows(x_ref[...], g_ref[...], dh_ref[...])
        dx = dres_ref[...] + dx
        dx_ref[...] = dx
        dxb_ref[...] = dx.astype(BF16)
        dg_ref[...] += jnp.sum(dgr, axis=0, keepdims=True)

    row = pl.BlockSpec((ts, d), lambda i: (i, 0))
    vec = pl.BlockSpec((1, d), lambda i: (0, 0))
    return pl.pallas_call(
        body, name=name, grid=(s // ts,),
        in_specs=[row, vec, row, row], out_specs=[row, row, vec],
        out_shape=[jax.ShapeDtypeStruct((s, d), F32), jax.ShapeDtypeStruct((s, d), BF16),
                   jax.ShapeDtypeStruct((1, d), F32)],
        compiler_params=_params(("arbitrary",)),
    )(x, g, dh, dres)


def _final_loss(x, g, target, *, name, ts=256):
    s, d = x.shape
    ts = min(ts, s)

    def body(x_ref, g_ref, t_ref, loss_ref, dx_ref, dxb_ref, dg_ref):
        @pl.when(pl.program_id(0) == 0)
        def _():
            loss_ref[...] = jnp.zeros_like(loss_ref)
            dg_ref[...] = jnp.zeros_like(dg_ref)

        xv = x_ref[...]
        gv = g_ref[...]
        r = lax.rsqrt(jnp.mean(xv * xv, axis=-1, keepdims=True) + RMS_EPS)
        xh = xv * r
        err = xh * gv - t_ref[...]
        loss_ref[...] += 0.5 * jnp.sum(jnp.mean(err * err, axis=-1, keepdims=True), axis=0, keepdims=True)
        dy = err * (1.0 / d)
        dn = dy * gv
        dx = r * (dn - xh * jnp.mean(dn * xh, axis=-1, keepdims=True))
        dx_ref[...] = dx
        dxb_ref[...] = dx.astype(BF16)
        dg_ref[...] += jnp.sum(dy * xh, axis=0, keepdims=True)

    row = pl.BlockSpec((ts, d), lambda i: (i, 0))
    vec = pl.BlockSpec((1, d), lambda i: (0, 0))
    return pl.pallas_call(
        body, name=name, grid=(s // ts,),
        in_specs=[row, vec, row],
        out_specs=[pl.BlockSpec((1, 1), lambda i: (0, 0)), row, row, vec],
        out_shape=[jax.ShapeDtypeStruct((1, 1), F32), jax.ShapeDtypeStruct((s, d), F32),
                   jax.ShapeDtypeStruct((s, d), BF16), jax.ShapeDtypeStruct((1, d), F32)],
        compiler_params=_params(("arbitrary",)),
    )(x, g, target)


def _conv_taps(ext_ref, w_ref, n_taps, base, r0, rows, reverse=False):
    acc = None
    for k in range(n_taps):
        off = r0 + (base - k if reverse else base + k)
        term = w_ref[k:k + 1, :] * ext_ref[pl.ds(off, rows), :]
        acc = term if acc is None else acc + term
    return acc


def _mix0_fwd(z, conv_a, ln_g, ln_b, conv_b, *, name, ts=256, rc=32):
    s = z.shape[0]
    c = A_WIDTH
    hb = HALO_A

    def body(z_ref, zp_ref, wa_ref, lg_ref, lb_ref, wb_ref, ab_ref, ca_ref, exta, extb):
        keep = jnp.where(pl.program_id(0) > 0, 1.0, 0.0)
        zp = zp_ref[...]
        exta[0:hb, :] = zp[:, 0:c] * _sigmoid(zp[:, c:2 * c]) * keep
        extb[0:hb, :] = zp[:, 3 * c:4 * c] * zp[:, 4 * c:5 * c] * keep
        exta[hb:hb + ts, :] = z_ref[:, 0:c] * _sigmoid(z_ref[:, c:2 * c])
        extb[hb:hb + ts, :] = z_ref[:, 3 * c:4 * c] * z_ref[:, 4 * c:5 * c]
        lg = lg_ref[...]
        lb = lb_ref[...]
        for q in range(ts // rc):
            r0 = q * rc
            ca = _conv_taps(exta, wa_ref, A_TAPS, hb - (A_TAPS - 1), r0, rc)
            ca_ref[r0:r0 + rc, :] = ca
            mu = jnp.mean(ca, axis=-1, keepdims=True)
            xc = ca - mu
            rs = lax.rsqrt(jnp.mean(xc * xc, axis=-1, keepdims=True) + LN_EPS)
            l = xc * rs * lg + lb
            ab_ref[r0:r0 + rc, 0:c] = (l * _sigmoid(l)).astype(BF16)
            cbc = _conv_taps(extb, wb_ref, 3, hb - 2, r0, rc)
            ab_ref[r0:r0 + rc, c:2 * c] = (z_ref[r0:r0 + rc, 2 * c:3 * c] * cbc).astype(BF16)

    return pl.pallas_call(
        body, name=name, grid=(s // ts,),
        in_specs=[pl.BlockSpec((ts, 5 * c), lambda i: (i, 0)),
                  pl.BlockSpec((hb, 5 * c), lambda i: (_prev_blk(i, ts, hb), 0)),
                  pl.BlockSpec((32, c), lambda i: (0, 0)),
                  pl.BlockSpec((1, c), lambda i: (0, 0)),
                  pl.BlockSpec((1, c), lambda i: (0, 0)),
                  pl.BlockSpec((8, c), lambda i: (0, 0))],
        out_specs=[pl.BlockSpec((ts, 2 * c), lambda i: (i, 0)),
                   pl.BlockSpec((ts, c), lambda i: (i, 0))],
        out_shape=[jax.ShapeDtypeStruct((s, 2 * c), BF16), jax.ShapeDtypeStruct((s, c), F32)],
        scratch_shapes=[pltpu.VMEM((hb + ts, c), F32), pltpu.VMEM((hb + ts, c), F32)],
        compiler_params=_params(("parallel",)),
    )(z, z, conv_a, ln_g, ln_b, conv_b)


def _mix0_bwd(z, ca, dab, conv_a, ln_g, ln_b, conv_b, *, name, ts=256, rc=32):
    s = z.shape[0]
    c = A_WIDTH
    hb = HALO_A
    ta = A_TAPS

    def body(z_ref, zp_ref, zn_ref, ca_ref, can_ref, d_ref, dn_ref, wa_ref, lg_ref, lb_ref, wb_ref,
             dz_ref, dwa_ref, dwb_ref, dlg_ref, dlb_ref, exta, extb, extdca, extdcb):
        i = pl.program_id(0)
        keep_p = jnp.where(i > 0, 1.0, 0.0)
        keep_n = jnp.where(i < s // ts - 1, 1.0, 0.0)

        @pl.when(i == 0)
        def _():
            dwa_ref[...] = jnp.zeros_like(dwa_ref)
            dwb_ref[...] = jnp.zeros_like(dwb_ref)
            dlg_ref[...] = jnp.zeros_like(dlg_ref)
            dlb_ref[...] = jnp.zeros_like(dlb_ref)

        lg = lg_ref[...]
        lb = lb_ref[...]
        zp = zp_ref[...]
        exta[0:hb, :] = zp[:, 0:c] * _sigmoid(zp[:, c:2 * c]) * keep_p
        extb[0:hb, :] = zp[:, 3 * c:4 * c] * zp[:, 4 * c:5 * c] * keep_p
        exta[hb:hb + ts, :] = z_ref[:, 0:c] * _sigmoid(z_ref[:, c:2 * c])
        extb[hb:hb + ts, :] = z_ref[:, 3 * c:4 * c] * z_ref[:, 4 * c:5 * c]

        def ln_bwd(cav, dav):
            mu = jnp.mean(cav, axis=-1, keepdims=True)
            xc = cav - mu
            rs = lax.rsqrt(jnp.mean(xc * xc, axis=-1, keepdims=True) + LN_EPS)
            nv = xc * rs
            l = nv * lg + lb
            sg = _sigmoid(l)
            dl = dav * (sg * (1.0 + l * (1.0 - sg)))
            dnv = dl * lg
            dca = rs * (dnv - jnp.mean(dnv, axis=-1, keepdims=True)
                        - nv * jnp.mean(dnv * nv, axis=-1, keepdims=True))
            return dca, dl, nv

        dlg_acc = jnp.zeros((1, c), F32)
        dlb_acc = jnp.zeros((1, c), F32)
        for q in range(ts // rc):
            r0 = q * rc
            dca, dl, nv = ln_bwd(ca_ref[r0:r0 + rc, :], d_ref[r0:r0 + rc, 0:c])
            extdca[r0:r0 + rc, :] = dca
            dlg_acc = dlg_acc + jnp.sum(dl * nv, axis=0, keepdims=True)
            dlb_acc = dlb_acc + jnp.sum(dl, axis=0, keepdims=True)
            extdcb[r0:r0 + rc, :] = d_ref[r0:r0 + rc, c:2 * c] * z_ref[r0:r0 + rc, 2 * c:3 * c]
        dca_n, _, _ = ln_bwd(can_ref[...], dn_ref[:, 0:c])
        extdca[ts:ts + hb, :] = dca_n * keep_n
        extdcb[ts:ts + hb, :] = dn_ref[:, c:2 * c] * zn_ref[:, 2 * c:3 * c] * keep_n
        dlg_ref[...] += dlg_acc
        dlb_ref[...] += dlb_acc

        for q in range(ts // rc):
            r0 = q * rc
            zr = z_ref[r0:r0 + rc, :]
            dga = _conv_taps(extdca, wa_ref, ta, ta - 1, r0, rc, reverse=True)
            sg = _sigmoid(zr[:, c:2 * c])
            dz_ref[r0:r0 + rc, 0:c] = (dga * sg).astype(BF16)
            dz_ref[r0:r0 + rc, c:2 * c] = (dga * zr[:, 0:c] * sg * (1.0 - sg)).astype(BF16)
            cbc = _conv_taps(extb, wb_ref, 3, hb - 2, r0, rc)
            dz_ref[r0:r0 + rc, 2 * c:3 * c] = (d_ref[r0:r0 + rc, c:2 * c] * cbc).astype(BF16)
            dcb = _conv_taps(extdcb, wb_ref, 3, 2, r0, rc, reverse=True)
            dz_ref[r0:r0 + rc, 3 * c:4 * c] = (dcb * zr[:, 4 * c:5 * c]).astype(BF16)
            dz_ref[r0:r0 + rc, 4 * c:5 * c] = (dcb * zr[:, 3 * c:4 * c]).astype(BF16)

        dca_t = extdca[0:ts, :]
        for k in range(ta):
            dwa_ref[k:k + 1, :] += jnp.sum(dca_t * exta[pl.ds(hb - (ta - 1) + k, ts), :], axis=0, keepdims=True)
        dcb_t = extdcb[0:ts, :]
        for k in range(3):
            dwb_ref[k:k + 1, :] += jnp.sum(dcb_t * extb[pl.ds(hb - 2 + k, ts), :], axis=0, keepdims=True)

    def tile(w):
        return pl.BlockSpec((ts, w), lambda i: (i, 0))

    def prev(w):
        return pl.BlockSpec((hb, w), lambda i: (_prev_blk(i, ts, hb), 0))

    def nxt(w):
        return pl.BlockSpec((hb, w), lambda i: (_next_blk(i, ts, hb, s), 0))

    def const(r, w):
        return pl.BlockSpec((r, w), lambda i: (0, 0))

    return pl.pallas_call(
        body, name=name, grid=(s // ts,),
        in_specs=[tile(5 * c), prev(5 * c), nxt(5 * c), tile(c), nxt(c), tile(2 * c), nxt(2 * c),
                  const(32, c), const(1, c), const(1, c), const(8, c)],
        out_specs=[tile(5 * c), const(32, c), const(8, c), const(1, c), const(1, c)],
        out_shape=[jax.ShapeDtypeStruct((s, 5 * c), BF16), jax.ShapeDtypeStruct((32, c), F32),
                   jax.ShapeDtypeStruct((8, c), F32), jax.ShapeDtypeStruct((1, c), F32),
                   jax.ShapeDtypeStruct((1, c), F32)],
        scratch_shapes=[pltpu.VMEM((hb + ts, c), F32), pltpu.VMEM((hb + ts, c), F32),
                        pltpu.VMEM((ts + hb, c), F32), pltpu.VMEM((ts + hb, c), F32)],
        compiler_params=_params(("arbitrary",)),
    )(z, z, z, ca, ca, dab, dab, conv_a, ln_g, ln_b, conv_b)


def _ffn_up(x, g, w8, *, name, tm=1024):
    s, d = x.shape
    nb, _, c = w8.shape
    tm = min(tm, s)

    def body(x_ref, g_ref, w_ref, h_ref, u_ref, hs_ref):
        @pl.when(pl.program_id(1) == 0)
        def _():
            hv = _rms_rows(x_ref[...], g_ref[...]).astype(BF16)
            hs_ref[...] = hv
            h_ref[...] = hv

        u_ref[...] = jnp.dot(hs_ref[...], w_ref[...], preferred_element_type=F32).astype(BF16)

    return pl.pallas_call(
        body, name=name, grid=(s // tm, nb),
        in_specs=[pl.BlockSpec((tm, d), lambda i, k: (i, 0)),
                  pl.BlockSpec((1, d), lambda i, k: (0, 0)),
                  pl.BlockSpec((None, d, c), lambda i, k: (k, 0, 0))],
        out_specs=[pl.BlockSpec((tm, d), lambda i, k: (i, 0)),
                   pl.BlockSpec((None, tm, c), lambda i, k: (k, i, 0))],
        out_shape=[jax.ShapeDtypeStruct((s, d), BF16), jax.ShapeDtypeStruct((nb, s, c), BF16)],
        scratch_shapes=[pltpu.VMEM((tm, d), BF16)],
        compiler_params=_params(("parallel", "arbitrary")),
    )(x, g, w8)


def _ffn_mid(up8, wc8, *, name, ts=512, rc=32):
    nb, s, c = up8.shape
    hb = HALO_S
    ts = min(ts, s)

    def body(g_ref, gp_ref, v_ref, vp_ref, wg_ref, wv_ref, act_ref, extg, extv):
        keep = jnp.where(pl.program_id(0) > 0, 1.0, 0.0)
        extg[0:hb, :] = gp_ref[...].astype(F32) * keep
        extv[0:hb, :] = vp_ref[...].astype(F32) * keep
        extg[hb:hb + ts, :] = g_ref[...].astype(F32)
        extv[hb:hb + ts, :] = v_ref[...].astype(F32)
        for q in range(ts // rc):
            r0 = q * rc
            gg = _conv_taps(extg, wg_ref, 3, hb - 2, r0, rc)
            vv = _conv_taps(extv, wv_ref, 3, hb - 2, r0, rc)
            act_ref[r0:r0 + rc, :] = (gg * _sigmoid(gg) * vv).astype(BF16)

    def blk(rows, off, halo):
        if halo:
            return pl.BlockSpec((None, rows, c), lambda i, j: (j + off, _prev_blk(i, ts, hb), 0))
        return pl.BlockSpec((None, rows, c), lambda i, j: (j + off, i, 0))

    def taps(off):
        return pl.BlockSpec((None, 8, c), lambda i, j: (j + off, 0, 0))

    return pl.pallas_call(
        body, name=name, grid=(s // ts, N_PAIR),
        in_specs=[blk(ts, 0, False), blk(hb, 0, True), blk(ts, N_PAIR, False), blk(hb, N_PAIR, True),
                  taps(0), taps(N_PAIR)],
        out_specs=pl.BlockSpec((None, ts, c), lambda i, j: (j, i, 0)),
        out_shape=jax.ShapeDtypeStruct((N_PAIR, s, c), BF16),
        scratch_shapes=[pltpu.VMEM((hb + ts, c), F32), pltpu.VMEM((hb + ts, c), F32)],
        compiler_params=_params(("parallel", "parallel")),
    )(up8, up8, up8, up8, wc8, wc8)


def _ffn_down(act4, wd4, x, *, name, tm=512):
    npair, s, c = act4.shape
    d = x.shape[1]
    tm = min(tm, s)

    def body(a_ref, w_ref, x_ref, o_ref):
        acc = x_ref[...]
        for j in range(npair):
            acc = acc + jnp.dot(a_ref[j], w_ref[j], preferred_element_type=F32)
        o_ref[...] = acc

    return pl.pallas_call(
        body, name=name, grid=(s // tm,),
        in_specs=[pl.BlockSpec((npair, tm, c), lambda i: (0, i, 0)),
                  pl.BlockSpec((npair, c, d), lambda i: (0, 0, 0)),
                  pl.BlockSpec((tm, d), lambda i: (i, 0))],
        out_specs=pl.BlockSpec((tm, d), lambda i: (i, 0)),
        out_shape=jax.ShapeDtypeStruct((s, d), F32),
        compiler_params=_params(("parallel",)),
    )(act4, wd4, x)


def _ffn_dact(db, wd4, *, name, tm=1024):
    s, d = db.shape
    npair, c, _ = wd4.shape
    tm = min(tm, s)

    def body(d_ref, w_ref, o_ref):
        o_ref[...] = lax.dot_general(d_ref[...], w_ref[...], (((1,), (1,)), ((), ())),
                                     preferred_element_type=F32).astype(BF16)

    return pl.pallas_call(
        body, name=name, grid=(s // tm, npair),
        in_specs=[pl.BlockSpec((tm, d), lambda i, j: (i, 0)),
                  pl.BlockSpec((None, c, d), lambda i, j: (j, 0, 0))],
        out_specs=pl.BlockSpec((None, tm, c), lambda i, j: (j, i, 0)),
        out_shape=jax.ShapeDtypeStruct((npair, s, c), BF16),
        compiler_params=_params(("parallel", "parallel")),
    )(db, wd4)


def _ffn_dwdown(act4, db, *, name):
    npair, s, c = act4.shape
    d = db.shape[1]

    def body(a_ref, d_ref, o_ref):
        o_ref[...] = lax.dot_general(a_ref[...], d_ref[...], (((0,), (0,)), ((), ())),
                                     preferred_element_type=F32).astype(BF16)

    return pl.pallas_call(
        body, name=name, grid=(npair,),
        in_specs=[pl.BlockSpec((None, s, c), lambda j: (j, 0, 0)),
                  pl.BlockSpec((s, d), lambda j: (0, 0))],
        out_specs=pl.BlockSpec((None, c, d), lambda j: (j, 0, 0)),
        out_shape=jax.ShapeDtypeStruct((npair, c, d), BF16),
        compiler_params=_params(("parallel",)),
    )(act4, db)


def _ffn_midbwd(up8, dact4, wc8, *, name, ts=512, rc=32):
    nb, s, c = up8.shape
    hb = HALO_S
    ts = min(ts, s)
    n_i = s // ts

    def body(g_ref, gp_ref, gn_ref, v_ref, vp_ref, vn_ref, d_ref, dn_ref, wg_ref, wv_ref,
             dg_ref, dv_ref, dwg_ref, dwv_ref, extg, extv, extdg, extdv):
        i = pl.program_id(1)
        keep_p = jnp.where(i > 0, 1.0, 0.0)
        keep_n = jnp.where(i < n_i - 1, 1.0, 0.0)

        @pl.when(i == 0)
        def _():
            dwg_ref[...] = jnp.zeros_like(dwg_ref)
            dwv_ref[...] = jnp.zeros_like(dwv_ref)

        for ext, cur, prv, nxt in ((extg, g_ref, gp_ref, gn_ref), (extv, v_ref, vp_ref, vn_ref)):
            ext[0:hb, :] = prv[...].astype(F32) * keep_p
            ext[hb:hb + ts, :] = cur[...].astype(F32)
            ext[hb + ts:hb + ts + hb, :] = nxt[...].astype(F32)

        def du_rows(r0, rows, dav):
            gg = _conv_taps(extg, wg_ref, 3, hb - 2, r0, rows)
            vv = _conv_taps(extv, wv_ref, 3, hb - 2, r0, rows)
            sg = _sigmoid(gg)
            extdg[r0:r0 + rows, :] = dav * vv * (sg * (1.0 + gg * (1.0 - sg)))
            extdv[r0:r0 + rows, :] = dav * (gg * sg)

        for q in range(ts // rc):
            du_rows(q * rc, rc, d_ref[q * rc:q * rc + rc, :].astype(F32))
        du_rows(ts, 8, dn_ref[0:8, :].astype(F32) * keep_n)

        for q in range(ts // rc):
            r0 = q * rc
            dg_ref[r0:r0 + rc, :] = _conv_taps(extdg, wg_ref, 3, 2, r0, rc, reverse=True).astype(BF16)
            dv_ref[r0:r0 + rc, :] = _conv_taps(extdv, wv_ref, 3, 2, r0, rc, reverse=True).astype(BF16)
        for ext, extd, dw in ((extg, extdg, dwg_ref), (extv, extdv, dwv_ref)):
            du_t = extd[0:ts, :]
            for k in range(3):
                dw[k:k + 1, :] += jnp.sum(du_t * ext[pl.ds(hb - 2 + k, ts), :], axis=0, keepdims=True)

    def blk(off, which):
        if which == "prev":
            return pl.BlockSpec((None, hb, c), lambda j, i: (j + off, _prev_blk(i, ts, hb), 0))
        if which == "next":
            return pl.BlockSpec((None, hb, c), lambda j, i: (j + off, _next_blk(i, ts, hb, s), 0))
        return pl.BlockSpec((None, ts, c), lambda j, i: (j + off, i, 0))

    def taps(off):
        return pl.BlockSpec((None, 8, c), lambda j, i: (j + off, 0, 0))

    tile = pl.BlockSpec((None, ts, c), lambda j, i: (j, i, 0))
    acc = pl.BlockSpec((None, 8, c), lambda j, i: (j, 0, 0))
    return pl.pallas_call(
        body, name=name, grid=(N_PAIR, n_i),
        in_specs=[blk(0, "tile"), blk(0, "prev"), blk(0, "next"),
                  blk(N_PAIR, "tile"), blk(N_PAIR, "prev"), blk(N_PAIR, "next"),
                  blk(0, "tile"), blk(0, "next"), taps(0), taps(N_PAIR)],
        out_specs=[tile, tile, acc, acc],
        out_shape=[jax.ShapeDtypeStruct((N_PAIR, s, c), BF16), jax.ShapeDtypeStruct((N_PAIR, s, c), BF16),
                   jax.ShapeDtypeStruct((N_PAIR, 8, c), F32), jax.ShapeDtypeStruct((N_PAIR, 8, c), F32)],
        scratch_shapes=[pltpu.VMEM((hb + ts + hb, c), F32), pltpu.VMEM((hb + ts + hb, c), F32),
                        pltpu.VMEM((ts + 8, c), F32), pltpu.VMEM((ts + 8, c), F32)],
        compiler_params=_params(("parallel", "arbitrary")),
    )(up8, up8, up8, up8, up8, up8, dact4, dact4, wc8, wc8)


def _ffn_dh(dupg, dupv, w8, x, g, dres, *, name, tm=256):
    npair, s, c = dupg.shape
    d = x.shape[1]
    tm = min(tm, s)

    def body(dg_ref, dv_ref, w_ref, x_ref, g_ref, dres_ref, dx_ref, dxb_ref, dgain_ref):
        @pl.when(pl.program_id(0) == 0)
        def _():
            dgain_ref[...] = jnp.zeros_like(dgain_ref)

        nt = (((1,), (1,)), ((), ()))
        dh = None
        for j in range(npair):
            for src, k in ((dg_ref, j), (dv_ref, j + npair)):
                part = lax.dot_general(src[j], w_ref[k], nt, preferred_element_type=F32)
                dh = part if dh is None else dh + part
        dx, dgr = _rms_bwd_rows(x_ref[...], g_ref[...], dh)
        dx = dres_ref[...] + dx
        dx_ref[...] = dx
        dxb_ref[...] = dx.astype(BF16)
        dgain_ref[...] += jnp.sum(dgr, axis=0, keepdims=True)

    row = pl.BlockSpec((tm, d), lambda i: (i, 0))
    vec = pl.BlockSpec((1, d), lambda i: (0, 0))
    dup = pl.BlockSpec((npair, tm, c), lambda i: (0, i, 0))
    return pl.pallas_call(
        body, name=name, grid=(s // tm,),
        in_specs=[dup, dup, pl.BlockSpec((2 * npair, d, c), lambda i: (0, 0, 0)), row, vec, row],
        out_specs=[row, row, vec],
        out_shape=[jax.ShapeDtypeStruct((s, d), F32), jax.ShapeDtypeStruct((s, d), BF16),
                   jax.ShapeDtypeStruct((1, d), F32)],
        compiler_params=_params(("arbitrary",)),
    )(dupg, dupv, w8, x, g, dres)


def _ffn_dwup(h, dupg, dupv, *, name, tm=512):
    npair, s, c = dupg.shape
    d = h.shape[1]

    def body(h_ref, dg_ref, dv_ref, o_ref):
        tn = (((0,), (0,)), ((), ()))
        k = pl.program_id(1)

        @pl.when(k < npair)
        def _():
            o_ref[...] = lax.dot_general(h_ref[...], dg_ref[...], tn, preferred_element_type=F32).astype(BF16)

        @pl.when(k >= npair)
        def _():
            o_ref[...] = lax.dot_general(h_ref[...], dv_ref[...], tn, preferred_element_type=F32).astype(BF16)

    return pl.pallas_call(
        body, name=name, grid=(d // tm, 2 * npair),
        in_specs=[pl.BlockSpec((s, tm), lambda m, k: (0, m)),
                  pl.BlockSpec((None, s, c), lambda m, k: (jnp.minimum(k, npair - 1), 0, 0)),
                  pl.BlockSpec((None, s, c), lambda m, k: (jnp.maximum(k - npair, 0), 0, 0))],
        out_specs=pl.BlockSpec((None, tm, c), lambda m, k: (k, m, 0)),
        out_shape=jax.ShapeDtypeStruct((2 * npair, d, c), BF16),
        compiler_params=_params(("parallel", "arbitrary")),
    )(h, dupg, dupv)


def _pool_counts(i, ts, rows, window):
    t = lax.broadcasted_iota(jnp.int32, (rows, 1), 0) + i * ts + 1
    return jnp.minimum(t, window).astype(F32)


def _pool_fwd(x, g, *, name, ts=256):
    s, d = x.shape
    hb = HALO_S
    pg = POOL_GROUP
    ts = min(ts, s)

    def body(x_ref, xp_ref, g_ref, p_ref, ext):
        i = pl.program_id(0)
        keep = jnp.where(i > 0, 1.0, 0.0)
        gv = g_ref[...]
        ext[0:hb, :] = _rms_rows(xp_ref[...], gv) * keep
        ext[hb:hb + ts, :] = _rms_rows(x_ref[...], gv)
        for gi, w in enumerate(POOL_WINDOWS):
            cols = slice(gi * pg, (gi + 1) * pg)
            h = ext[hb:hb + ts, cols]
            acc = h
            for q in range(1, w):
                acc = acc + ext[pl.ds(hb - q, ts), cols]
            p_ref[:, cols] = (acc / _pool_counts(i, ts, ts, w) - h).astype(BF16)

    return pl.pallas_call(
        body, name=name, grid=(s // ts,),
        in_specs=[pl.BlockSpec((ts, d), lambda i: (i, 0)),
                  pl.BlockSpec((hb, d), lambda i: (_prev_blk(i, ts, hb), 0)),
                  pl.BlockSpec((1, d), lambda i: (0, 0))],
        out_specs=pl.BlockSpec((ts, d), lambda i: (i, 0)),
        out_shape=jax.ShapeDtypeStruct((s, d), BF16),
        scratch_shapes=[pltpu.VMEM((hb + ts, d), F32)],
        compiler_params=_params(("parallel",)),
    )(x, x, g)


def _pool_mm_fwd(p, w_pool, scale, x, *, name, ts=1024):
    s, d = x.shape
    pg = POOL_GROUP
    ts = min(ts, s)

    def body(p_ref, w_ref, s_ref, x_ref, o_ref, yu_ref):
        yu = jnp.dot(p_ref[...], w_ref[...], preferred_element_type=F32)
        yu_ref[...] = yu.astype(BF16)
        o_ref[...] = x_ref[...] + yu * s_ref[...]

    blk = pl.BlockSpec((ts, pg), lambda i, gi: (i, gi))
    return pl.pallas_call(
        body, name=name, grid=(s // ts, d // pg),
        in_specs=[blk, pl.BlockSpec((None, pg, pg), lambda i, gi: (gi, 0, 0)),
                  pl.BlockSpec((1, pg), lambda i, gi: (0, gi)), blk],
        out_specs=[blk, blk],
        out_shape=[jax.ShapeDtypeStruct((s, d), F32), jax.ShapeDtypeStruct((s, d), BF16)],
        compiler_params=_params(("parallel", "parallel")),
    )(p, w_pool, scale, x)


def _pool_mm_bwd(dres, w_pool, scale, yu, *, name, ts=1024):
    s, d = dres.shape
    pg = POOL_GROUP
    ts = min(ts, s)

    def body(d_ref, w_ref, s_ref, yu_ref, dyc_ref, dp_ref, ds_ref):
        @pl.when(pl.program_id(1) == 0)
        def _():
            ds_ref[...] = jnp.zeros_like(ds_ref)

        dv = d_ref[...]
        dyc = (dv * s_ref[...]).astype(BF16)
        dyc_ref[...] = dyc
        dp_ref[...] = lax.dot_general(dyc, w_ref[...], (((1,), (1,)), ((), ())), preferred_element_type=F32)
        ds_ref[...] += jnp.sum(dv * yu_ref[...].astype(F32), axis=0, keepdims=True)

    blk = pl.BlockSpec((ts, pg), lambda gi, i: (i, gi))
    vec = pl.BlockSpec((1, pg), lambda gi, i: (0, gi))
    return pl.pallas_call(
        body, name=name, grid=(d // pg, s // ts),
        in_specs=[blk, pl.BlockSpec((None, pg, pg), lambda gi, i: (gi, 0, 0)), vec, blk],
        out_specs=[blk, blk, vec],
        out_shape=[jax.ShapeDtypeStruct((s, d), BF16), jax.ShapeDtypeStruct((s, d), F32),
                   jax.ShapeDtypeStruct((1, d), F32)],
        compiler_params=_params(("parallel", "arbitrary")),
    )(dres, w_pool, scale, yu)


def _pool_dw(p, dyc, *, name):
    s, d = p.shape
    pg = POOL_GROUP

    def body(p_ref, d_ref, o_ref):
        o_ref[...] = lax.dot_general(p_ref[...], d_ref[...], (((0,), (0,)), ((), ())),
                                     preferred_element_type=F32).astype(BF16)

    blk = pl.BlockSpec((s, pg), lambda gi: (0, gi))
    return pl.pallas_call(
        body, name=name, grid=(d // pg,),
        in_specs=[blk, blk], out_specs=pl.BlockSpec((None, pg, pg), lambda gi: (gi, 0, 0)),
        out_shape=jax.ShapeDtypeStruct((d // pg, pg, pg), BF16),
        compiler_params=_params(("parallel",)),
    )(p, dyc)


def _pool_bwd(dp, x, g, dres, *, name, ts=256):
    s, d = x.shape
    hb = HALO_S
    pg = POOL_GROUP
    ts = min(ts, s)
    n_i = s // ts

    def body(dp_ref, dpn_ref, x_ref, g_ref, dres_ref, dx_ref, dxb_ref, dg_ref, ext, dh):
        i = pl.program_id(0)
        keep_n = jnp.where(i < n_i - 1, 1.0, 0.0)

        @pl.when(i == 0)
        def _():
            dg_ref[...] = jnp.zeros_like(dg_ref)

        for gi, w in enumerate(POOL_WINDOWS):
            cols = slice(gi * pg, (gi + 1) * pg)
            ext[0:ts, cols] = dp_ref[:, cols] / _pool_counts(i, ts, ts, w)
            ext[ts:ts + hb, cols] = dpn_ref[:, cols] / _pool_counts(i + 1, ts, hb, w) * keep_n
        for gi, w in enumerate(POOL_WINDOWS):
            cols = slice(gi * pg, (gi + 1) * pg)
            acc = ext[0:ts, cols]
            for q in range(1, w):
                acc = acc + ext[pl.ds(q, ts), cols]
            dh[:, cols] = acc - dp_ref[:, cols]
        dx, dgr = _rms_bwd_rows(x_ref[...], g_ref[...], dh[...])
        dx = dres_ref[...] + dx
        dx_ref[...] = dx
        dxb_ref[...] = dx.astype(BF16)
        dg_ref[...] += jnp.sum(dgr, axis=0, keepdims=True)

    row = pl.BlockSpec((ts, d), lambda i: (i, 0))
    vec = pl.BlockSpec((1, d), lambda i: (0, 0))
    return pl.pallas_call(
        body, name=name, grid=(n_i,),
        in_specs=[row, pl.BlockSpec((hb, d), lambda i: (_next_blk(i, ts, hb, s), 0)), row, vec, row],
        out_specs=[row, row, vec],
        out_shape=[jax.ShapeDtypeStruct((s, d), F32), jax.ShapeDtypeStruct((s, d), BF16),
                   jax.ShapeDtypeStruct((1, d), F32)],
        scratch_shapes=[pltpu.VMEM((ts + hb, d), F32), pltpu.VMEM((ts, d), F32)],
        compiler_params=_params(("arbitrary",)),
    )(dp, dp, x, g, dres)


def _pad_rows(w, rows):
    pad = [(0, 0)] * (w.ndim - 2) + [(0, rows - w.shape[-2]), (0, 0)]
    return jnp.pad(w, pad)


def _ffn_layer_fwd(x, nf, w8, wc8, wd4, tag):
    h, up8 = _ffn_up(x, nf, w8, name=f"ffn{tag}_up")
    act4 = _ffn_mid(up8, wc8, name=f"ffn{tag}_mid")
    x_out = _ffn_down(act4, wd4, x, name=f"ffn{tag}_down")
    return x_out, (h, up8, act4)


def _ffn_layer_bwd(d, db, x, nf, w8, wc8, wd4, saved, tag):
    h, up8, act4 = saved
    dact4 = _ffn_dact(db, wd4, name=f"ffn{tag}_dact")
    dwd4 = _ffn_dwdown(act4, db, name=f"ffn{tag}_dwdown")
    dupg, dupv, dwg, dwv = _ffn_midbwd(up8, dact4, wc8, name=f"ffn{tag}_midbwd")
    dx, dxb, dnf = _ffn_dh(dupg, dupv, w8, x, nf, d, name=f"ffn{tag}_dh")
    dw8 = _ffn_dwup(h, dupg, dupv, name=f"ffn{tag}_dwup")
    dwc8 = jnp.concatenate([dwg, dwv], axis=0)[:, :3]
    return dx, dxb, dnf, dw8, dwc8, dwd4


def _local_step(x, target, w):
    wa = _pad_rows(w["conv_a"], 32)
    wb = _pad_rows(w["conv_b"], 8)
    wc = [_pad_rows(w["conv_ffn"][l], 8) for l in range(2)]

    h0, z = _rms_mm(x, w["norm_mix_even"], w["w_in"], name="mix0_in", out_dtype=F32, tn=1280)
    ab, ca = _mix0_fwd(z, wa, w["ln_a_g"], w["ln_a_b"], wb, name="mix0_mid")
    x1 = _mm(ab, w["w_out"], add=x, name="mix0_out", tm=1024, tn=1024)
    x2, ffn0 = _ffn_layer_fwd(x1, w["norm_ffn"][0:1], w["w_up"][0], wc[0], w["w_down"][0], 0)
    p = _pool_fwd(x2, w["norm_mix_odd"], name="pool_mid")
    x3, yu = _pool_mm_fwd(p, w["w_pool"], w["pool_scale"], x2, name="pool_mm")
    x4, ffn1 = _ffn_layer_fwd(x3, w["norm_ffn"][1:2], w["w_up"][1], wc[1], w["w_down"][1], 1)
    loss, d4, d4b, g_norm_final = _final_loss(x4, w["norm_final"], target, name="final_loss")

    d3, d3b, g_nf1, g_up1, g_wc1, g_down1 = _ffn_layer_bwd(
        d4, d4b, x3, w["norm_ffn"][1:2], w["w_up"][1], wc[1], w["w_down"][1], ffn1, 1)
    dyc, dp, g_scale = _pool_mm_bwd(d3, w["w_pool"], w["pool_scale"], yu, name="pool_mm_bwd")
    g_pool = _pool_dw(p, dyc, name="pool_dw")
    d2, d2b, g_nmo = _pool_bwd(dp, x2, w["norm_mix_odd"], d3, name="pool_midbwd")
    d1, d1b, g_nf0, g_up0, g_wc0, g_down0 = _ffn_layer_bwd(
        d2, d2b, x1, w["norm_ffn"][0:1], w["w_up"][0], wc[0], w["w_down"][0], ffn0, 0)
    dab = _mm(d1b, w["w_out"], tb=True, name="mix0_dab", tm=1024, tn=1024)
    g_out = _mm(ab, d1b, ta=True, out_dtype=BF16, name="mix0_dwout", tm=1024, tn=512)
    dz, g_wa, g_wb, g_lg, g_lb = _mix0_bwd(z, ca, dab, wa, w["ln_a_g"], w["ln_a_b"], wb, name="mix0_midbwd")
    dh0 = _mm(dz, w["w_in"], tb=True, name="mix0_dh", tm=1024, tn=1024)
    g_in = _mm(h0, dz, ta=True, out_dtype=BF16, name="mix0_dwin", tm=1024, tn=512)
    dx, _, g_nme = _rms_bwd(x, w["norm_mix_even"], dh0, d1, name="mix0_rmsbwd")

    grads = {
        "norm_mix_even": g_nme, "w_in": g_in, "conv_a": g_wa[:A_TAPS], "ln_a_g": g_lg, "ln_a_b": g_lb,
        "conv_b": g_wb[:3], "w_out": g_out, "norm_mix_odd": g_nmo, "w_pool": g_pool, "pool_scale": g_scale,
        "norm_ffn": jnp.concatenate([g_nf0, g_nf1], axis=0),
        "w_up": [g_up0, g_up1], "conv_ffn": [g_wc0, g_wc1], "w_down": [g_down0, g_down1],
        "norm_final": g_norm_final,
    }
    return loss[0, 0], dx, grads


def _my_pos():
    return lax.axis_index("x"), lax.axis_index("y"), lax.axis_index("c")


def _flip(pos, r):
    x, y, c = pos
    return (1 - x if r & 4 else x, 1 - y if r & 2 else y, 1 - c if r & 1 else c)


def _dev_index(pos):
    return 4 * pos[0] + 2 * pos[1] + pos[2]


def _exchange(arrays, scatter, *, name):
    n = len(arrays)

    def body(*refs):
        ins, outs = refs[:n], refs[n:2 * n]
        send_sems, recv_sems, local_sems = refs[2 * n:]
        pos = _my_pos()
        me = _dev_index(pos)
        own = []
        for a in range(n):
            cp = pltpu.make_async_copy(ins[a].at[me] if scatter[a] else ins[a], outs[a].at[me], local_sems.at[a])
            cp.start()
            own.append(cp)
        sent = []
        for a in range(n):
            for r in range(1, N_DEV):
                peer = _flip(pos, r)
                src = ins[a].at[_dev_index(peer)] if scatter[a] else ins[a]
                cp = pltpu.make_async_remote_copy(
                    src_ref=src, dst_ref=outs[a].at[me], send_sem=send_sems.at[a, r - 1],
                    recv_sem=recv_sems.at[a, r - 1], device_id=peer, device_id_type=pl.DeviceIdType.MESH)
                cp.start()
                sent.append(cp)
        for a in range(n):
            for r in range(1, N_DEV):
                peer = _flip(pos, r)
                src = ins[a].at[me] if scatter[a] else ins[a]
                pltpu.make_async_remote_copy(
                    src_ref=src, dst_ref=outs[a].at[_dev_index(peer)], send_sem=send_sems.at[a, r - 1],
                    recv_sem=recv_sems.at[a, r - 1], device_id=peer, device_id_type=pl.DeviceIdType.MESH).wait_recv()
        for cp in sent:
            cp.wait_send()
        for cp in own:
            cp.wait()

    out_shape = [jax.ShapeDtypeStruct(a.shape if sc else (N_DEV,) + a.shape, a.dtype)
                 for a, sc in zip(arrays, scatter)]
    any_spec = pl.BlockSpec(memory_space=pl.ANY)
    return pl.pallas_call(
        body, name=name, in_specs=[any_spec] * n, out_specs=[any_spec] * n, out_shape=out_shape,
        scratch_shapes=[pltpu.SemaphoreType.DMA((n, N_DEV - 1)), pltpu.SemaphoreType.DMA((n, N_DEV - 1)),
                        pltpu.SemaphoreType.DMA((n,))],
    )(*arrays)


def _adamw(parts, w, m, v, *, name, tr):
    r, c = w.shape
    assert r % tr == 0

    def body(p_ref, w_ref, m_ref, v_ref, g_ref, d_ref, mo_ref, vo_ref):
        g = p_ref[0].astype(F32)
        for k in range(1, N_DEV):
            g = g + p_ref[k].astype(F32)
        mn = ADAM_B1 * m_ref[...] + (1.0 - ADAM_B1) * g
        vn = ADAM_B2 * v_ref[...] + (1.0 - ADAM_B2) * (g * g)
        m_hat = mn / (1.0 - ADAM_B1 ** ADAM_STEP)
        v_hat = vn / (1.0 - ADAM_B2 ** ADAM_STEP)
        g_ref[...] = g
        d_ref[...] = -ADAM_LR * (m_hat / (jnp.sqrt(v_hat) + ADAM_EPS) + ADAM_WD * w_ref[...])
        mo_ref[...] = mn
        vo_ref[...] = vn

    blk = pl.BlockSpec((tr, c), lambda i: (i, 0))
    return pl.pallas_call(
        body, name=name, grid=(r // tr,),
        in_specs=[pl.BlockSpec((N_DEV, tr, c), lambda i: (0, i, 0)), blk, blk, blk],
        out_specs=[blk] * 4, out_shape=[jax.ShapeDtypeStruct((r, c), F32)] * 4,
        compiler_params=_params(("parallel",)),
    )(parts, w, m, v)


def _pack(parts, lead=()):
    flat = jnp.concatenate([p.reshape(lead + (-1,)) for p in parts], axis=-1)
    n = flat.shape[-1]
    rows = -(-n // (8 * 128)) * 8
    flat = jnp.pad(flat, [(0, 0)] * len(lead) + [(0, rows * 128 - n)])
    return flat.reshape(lead + (rows, 128))


def _unpack(slab, shapes):
    flat = slab.reshape(-1)
    out, off = [], 0
    for shp in shapes:
        size = 1
        for dim in shp:
            size *= dim
        out.append(flat[off:off + size].reshape(shp))
        off += size
    return out


def _to_dev_major(g, axis):
    shp = g.shape
    g = g.reshape(shp[:axis] + (N_DEV, shp[axis] // N_DEV) + shp[axis + 1:])
    return jnp.moveaxis(g, axis, 0)


def _from_dev_major(g, axis):
    g = jnp.moveaxis(g, 0, axis)
    shp = g.shape
    return g.reshape(shp[:axis] + (shp[axis] * shp[axis + 1],) + shp[axis + 2:])


SMALL_SHARDED = ("conv_a", "conv_b", "norm_mix_odd", "pool_scale", "conv_ffn_w")
SMALL_REPLICATED = ("norm_mix_even", "ln_a_g", "ln_a_b", "norm_ffn", "norm_final")
BIG = {"w_in": ("w_in", None, 512), "w_out": ("w_out", None, 128), "w_pool": ("w_pool", None, 128),
       "w_up0": ("w_up", 0, 256), "w_up1": ("w_up", 1, 256),
       "w_down0": ("w_down", 0, 352), "w_down1": ("w_down", 1, 352)}


def kernel(x, norm_mix_even, w_in, conv_a, ln_a_g, ln_a_b, conv_b, w_out, norm_mix_odd, w_pool, pool_scale, norm_ffn, w_up, conv_ffn_w, w_down, norm_final, loss_target, m_norm_mix_even, m_w_in, m_conv_a, m_ln_a_g, m_ln_a_b, m_conv_b, m_w_out, m_norm_mix_odd, m_w_pool, m_pool_scale, m_norm_ffn, m_w_up, m_conv_ffn_w, m_w_down, m_norm_final, v_norm_mix_even, v_w_in, v_conv_a, v_ln_a_g, v_ln_a_b, v_conv_b, v_w_out, v_norm_mix_odd, v_w_pool, v_pool_scale, v_norm_ffn, v_w_up, v_conv_ffn_w, v_w_down, v_norm_final):
    names = ("norm_mix_even", "w_in", "conv_a", "ln_a_g", "ln_a_b", "conv_b", "w_out", "norm_mix_odd", "w_pool",
             "pool_scale", "norm_ffn", "w_up", "conv_ffn_w", "w_down", "norm_final")
    wts = dict(zip(names, (norm_mix_even, w_in, conv_a, ln_a_g, ln_a_b, conv_b, w_out, norm_mix_odd, w_pool,
                           pool_scale, norm_ffn, w_up, conv_ffn_w, w_down, norm_final)))
    mom = dict(zip(names, (m_norm_mix_even, m_w_in, m_conv_a, m_ln_a_g, m_ln_a_b, m_conv_b, m_w_out, m_norm_mix_odd,
                           m_w_pool, m_pool_scale, m_norm_ffn, m_w_up, m_conv_ffn_w, m_w_down, m_norm_final)))
    var = dict(zip(names, (v_norm_mix_even, v_w_in, v_conv_a, v_ln_a_g, v_ln_a_b, v_conv_b, v_w_out, v_norm_mix_odd,
                           v_w_pool, v_pool_scale, v_norm_ffn, v_w_up, v_conv_ffn_w, v_w_down, v_norm_final)))
    d = x.shape[-1]
    n_up = w_up.shape[-1]

    def shard2d(t, key):
        param, layer, _ = BIG[key]
        a = t[param][0] if layer is None else t[param][layer]
        return a.reshape(-1, a.shape[-1])

    small_w = _pack([wts[k] for k in SMALL_SHARDED])
    gathered = _exchange([shard2d(wts, k).astype(BF16) for k in BIG] + [small_w], [False] * (len(BIG) + 1),
                         name="weights_allgather")
    gw = dict(zip(BIG, gathered))
    per_dev = gathered[-1].reshape(N_DEV, -1)
    sizes = [wts[k].size for k in SMALL_SHARDED]
    offs = [sum(sizes[:i]) for i in range(len(sizes))]
    small_full = {k: per_dev[:, o:o + n_].reshape((N_DEV,) + wts[k].shape)
                  for k, o, n_ in zip(SMALL_SHARDED, offs, sizes)}
    full = {
        "norm_mix_even": norm_mix_even, "ln_a_g": ln_a_g, "ln_a_b": ln_a_b, "norm_ffn": norm_ffn,
        "norm_final": norm_final[None],
        "w_in": _from_dev_major(gw["w_in"], 1),
        "w_out": gw["w_out"].reshape(-1, d),
        "w_pool": _from_dev_major(gw["w_pool"].reshape(N_DEV, len(POOL_WINDOWS), -1, POOL_GROUP), 1),
        "w_up": [gw["w_up0"], gw["w_up1"]],
        "w_down": [gw[k].reshape(N_PAIR, -1, d) for k in ("w_down0", "w_down1")],
        "conv_a": _from_dev_major(small_full["conv_a"][:, 0], 1),
        "conv_b": _from_dev_major(small_full["conv_b"][:, 0], 1),
        "norm_mix_odd": _from_dev_major(small_full["norm_mix_odd"], 1),
        "pool_scale": _from_dev_major(small_full["pool_scale"], 1),
        "conv_ffn": [small_full["conv_ffn_w"][:, l] for l in range(2)],
    }

    loss, dx, g = _local_step(x[0], loss_target[0], full)

    big_parts = {
        "w_in": _to_dev_major(g["w_in"], 1),
        "w_out": g["w_out"].reshape(N_DEV, -1, d),
        "w_pool": _to_dev_major(g["w_pool"], 1).reshape(N_DEV, -1, POOL_GROUP),
        "w_up0": g["w_up"][0], "w_up1": g["w_up"][1],
        "w_down0": g["w_down"][0].reshape(N_DEV, -1, d), "w_down1": g["w_down"][1].reshape(N_DEV, -1, d),
    }
    small_parts = _pack([
        _to_dev_major(g["conv_a"], 1), _to_dev_major(g["conv_b"], 1), _to_dev_major(g["norm_mix_odd"], 1),
        _to_dev_major(g["pool_scale"], 1), jnp.stack(g["conv_ffn"], axis=1)], lead=(N_DEV,))
    repl_parts = _pack([g[k] for k in SMALL_REPLICATED])
    landed = _exchange([big_parts[k] for k in BIG] + [small_parts, repl_parts],
                       [True] * (len(BIG) + 1) + [False], name="grads_exchange")

    big_out = {}
    for k, parts in zip(BIG, landed[:len(BIG)]):
        big_out[k] = _adamw(parts, shard2d(wts, k), shard2d(mom, k), shard2d(var, k), name=f"adamw_{k}", tr=BIG[k][2])
    out = {}
    for k in ("w_in", "w_out", "w_pool"):
        out[k] = [t.reshape(wts[k].shape) for t in big_out[k]]
    for k in ("w_up", "w_down"):
        out[k] = [jnp.stack([a, b]).reshape(wts[k].shape) for a, b in zip(big_out[k + "0"], big_out[k + "1"])]
    res = _adamw(landed[len(BIG)], small_w, _pack([mom[k] for k in SMALL_SHARDED]),
                 _pack([var[k] for k in SMALL_SHARDED]), name="adamw_small", tr=small_w.shape[0])
    unpacked = [_unpack(t, [wts[k].shape for k in SMALL_SHARDED]) for t in res]
    for j, k in enumerate(SMALL_SHARDED):
        out[k] = [u[j] for u in unpacked]
    repl_w = _pack([wts[k] for k in SMALL_REPLICATED])
    res = _adamw(landed[len(BIG) + 1], repl_w, _pack([mom[k] for k in SMALL_REPLICATED]),
                 _pack([var[k] for k in SMALL_REPLICATED]), name="adamw_replicated", tr=repl_w.shape[0])
    unpacked = [_unpack(t, [wts[k].shape for k in SMALL_REPLICATED]) for t in res]
    for j, k in enumerate(SMALL_REPLICATED):
        out[k] = [u[j] for u in unpacked]

    loss = lax.psum(loss, ("x", "y", "c"))
    return (loss, dx[None], *[out[k][0] for k in names], *[out[k][1] for k in names],
            *[out[k][2] for k in names], *[out[k][3] for k in names])
```

```python
import jax
import jax.numpy as jnp
from jax import lax
from jax.experimental import pallas as pl
from jax.experimental.pallas import tpu as pltpu

F32 = jnp.float32
BF16 = jnp.bfloat16

RMS_EPS = 1e-6
LN_EPS = 1e-5
ADAM_LR = 0.001
ADAM_B1 = 0.9
ADAM_B2 = 0.999
ADAM_EPS = 1e-08
ADAM_WD = 0.01
ADAM_STEP = 10

N_DEV = 8
N_PAIR = N_DEV // 2
A_WIDTH = 512
A_TAPS = 31
POOL_WINDOWS = (2, 4, 8, 16)
POOL_GROUP = 256
HALO_A = 32
HALO_S = 16
VMEM_LIMIT = 56 * 1024 * 1024


def _params(sem, vmem=VMEM_LIMIT):
    return pltpu.CompilerParams(dimension_semantics=sem, vmem_limit_bytes=vmem)


def _sigmoid(x):
    return 1.0 / (1.0 + jnp.exp(-x))


def _prev_blk(i, ts, hb):
    return jnp.maximum(i * (ts // hb) - 1, 0)


def _next_blk(i, ts, hb, s):
    return jnp.minimum((i + 1) * (ts // hb), s // hb - 1)


def _mm(a, b, *, name, ta=False, tb=False, add=None, out_dtype=F32, tm=512, tn=512, tk=None):
    m, k = (a.shape[1], a.shape[0]) if ta else a.shape
    n = b.shape[0] if tb else b.shape[1]
    tk = k if tk is None else tk
    tm, tn, tk = min(tm, m), min(tn, n), min(tk, k)
    assert m % tm == 0 and n % tn == 0 and k % tk == 0, (name, m, n, k, tm, tn, tk)
    nk = k // tk
    dims = (((0,) if ta else (1,), (1,) if tb else (0,)), ((), ()))

    def body(*refs):
        a_ref, b_ref = refs[0], refs[1]
        add_ref = refs[2] if add is not None else None
        o_ref = refs[3] if add is not None else refs[2]
        part = lax.dot_general(a_ref[...].astype(BF16), b_ref[...].astype(BF16), dims, preferred_element_type=F32)

        def finish(r):
            if add_ref is not None:
                r = r + add_ref[...]
            o_ref[...] = r.astype(out_dtype)

        if nk == 1:
            finish(part)
            return
        acc_ref = refs[-1]
        kk = pl.program_id(2)

        @pl.when(kk == 0)
        def _():
            acc_ref[...] = part

        @pl.when(kk > 0)
        def _():
            acc_ref[...] += part

        @pl.when(kk == nk - 1)
        def _():
            finish(acc_ref[...])

    a_spec = pl.BlockSpec((tk, tm), lambda i, j, kk: (kk, i)) if ta else pl.BlockSpec((tm, tk), lambda i, j, kk: (i, kk))
    b_spec = pl.BlockSpec((tn, tk), lambda i, j, kk: (j, kk)) if tb else pl.BlockSpec((tk, tn), lambda i, j, kk: (kk, j))
    in_specs = [a_spec, b_spec]
    args = [a, b]
    if add is not None:
        in_specs.append(pl.BlockSpec((tm, tn), lambda i, j, kk: (i, j)))
        args.append(add)
    return pl.pallas_call(
        body, name=name, grid=(m // tm, n // tn, nk),
        in_specs=in_specs, out_specs=pl.BlockSpec((tm, tn), lambda i, j, kk: (i, j)),
        out_shape=jax.ShapeDtypeStruct((m, n), out_dtype),
        scratch_shapes=[pltpu.VMEM((tm, tn), F32)] if nk > 1 else [],
        compiler_params=_params(("parallel", "parallel", "arbitrary")),
    )(*args)


def _rms_rows(xv, gv):
    return xv * lax.rsqrt(jnp.mean(xv * xv, axis=-1, keepdims=True) + RMS_EPS) * gv


def _rms_mm(x, g, w, *, name, out_dtype, tm=1024, tn=512):
    s, d = x.shape
    n = w.shape[1]
    tm = min(tm, s)
    assert s % tm == 0 and n % tn == 0

    def body(x_ref, g_ref, w_ref, h_ref, z_ref, hs_ref):
        @pl.when(pl.program_id(1) == 0)
        def _():
            hv = _rms_rows(x_ref[...], g_ref[...]).astype(BF16)
            hs_ref[...] = hv
            h_ref[...] = hv

        z_ref[...] = jnp.dot(hs_ref[...], w_ref[...], preferred_element_type=F32).astype(out_dtype)

    return pl.pallas_call(
        body, name=name, grid=(s // tm, n // tn),
        in_specs=[pl.BlockSpec((tm, d), lambda i, j: (i, 0)),
                  pl.BlockSpec((1, d), lambda i, j: (0, 0)),
                  pl.BlockSpec((d, tn), lambda i, j: (0, j))],
        out_specs=[pl.BlockSpec((tm, d), lambda i, j: (i, 0)),
                   pl.BlockSpec((tm, tn), lambda i, j: (i, j))],
        out_shape=[jax.ShapeDtypeStruct((s, d), BF16), jax.ShapeDtypeStruct((s, n), out_dtype)],
        scratch_shapes=[pltpu.VMEM((tm, d), BF16)],
        compiler_params=_params(("parallel", "arbitrary")),
    )(x, g, w)


def _rms_bwd_rows(xv, gv, dh):
    r = lax.rsqrt(jnp.mean(xv * xv, axis=-1, keepdims=True) + RMS_EPS)
    xh = xv * r
    dn = dh * gv
    dx = r * (dn - xh * jnp.mean(dn * xh, axis=-1, keepdims=True))
    return dx, dh * xh


def _rms_bwd(x, g, dh, dres, *, name, ts=256):
    s, d = x.shape
    ts = min(ts, s)

    def body(x_ref, g_ref, dh_ref, dres_ref, dx_ref, dxb_ref, dg_ref):
        @pl.when(pl.program_id(0) == 0)
        def _():
            dg_ref[...] = jnp.zeros_like(dg_ref)

        dx, dgr = _rms_bwd_rows(x_ref[...], g_ref[...], dh_ref[...])
        dx = dres_ref[...] + dx
        dx_ref[...] = dx
        dxb_ref[...] = dx.astype(BF16)
        dg_ref[...] += jnp.sum(dgr, axis=0, keepdims=True)

    row = pl.BlockSpec((ts, d), lambda i: (i, 0))
    vec = pl.BlockSpec((1, d), lambda i: (0, 0))
    return pl.pallas_call(
        body, name=name, grid=(s // ts,),
        in_specs=[row, vec, row, row], out_specs=[row, row, vec],
        out_shape=[jax.ShapeDtypeStruct((s, d), F32), jax.ShapeDtypeStruct((s, d), BF16),
                   jax.ShapeDtypeStruct((1, d), F32)],
        compiler_params=_params(("arbitrary",)),
    )(x, g, dh, dres)


def _final_loss(x, g, target, *, name, ts=256):
    s, d = x.shape
    ts = min(ts, s)

    def body(x_ref, g_ref, t_ref, loss_ref, dx_ref, dxb_ref, dg_ref):
        @pl.when(pl.program_id(0) == 0)
        def _():
            loss_ref[...] = jnp.zeros_like(loss_ref)
            dg_ref[...] = jnp.zeros_like(dg_ref)

        xv = x_ref[...]
        gv = g_ref[...]
        r = lax.rsqrt(jnp.mean(xv * xv, axis=-1, keepdims=True) + RMS_EPS)
        xh = xv * r
        err = xh * gv - t_ref[...]
        loss_ref[...] += 0.5 * jnp.sum(jnp.mean(err * err, axis=-1, keepdims=True), axis=0, keepdims=True)
        dy = err * (1.0 / d)
        dn = dy * gv
        dx = r * (dn - xh * jnp.mean(dn * xh, axis=-1, keepdims=True))
        dx_ref[...] = dx
        dxb_ref[...] = dx.astype(BF16)
        dg_ref[...] += jnp.sum(dy * xh, axis=0, keepdims=True)

    row = pl.BlockSpec((ts, d), lambda i: (i, 0))
    vec = pl.BlockSpec((1, d), lambda i: (0, 0))
    return pl.pallas_call(
        body, name=name, grid=(s // ts,),
        in_specs=[row, vec, row],
        out_specs=[pl.BlockSpec((1, 1), lambda i: (0, 0)), row, row, vec],
        out_shape=[jax.ShapeDtypeStruct((1, 1), F32), jax.ShapeDtypeStruct((s, d), F32),
                   jax.ShapeDtypeStruct((s, d), BF16), jax.ShapeDtypeStruct((1, d), F32)],
        compiler_params=_params(("arbitrary",)),
    )(x, g, target)


def _conv_taps(ext_ref, w_ref, n_taps, base, r0, rows, reverse=False):
    acc = None
    for k in range(n_taps):
        off = r0 + (base - k if reverse else base + k)
        term = w_ref[k:k + 1, :] * ext_ref[pl.ds(off, rows), :]
        acc = term if acc is None else acc + term
    return acc


def _mix0_fwd(z, conv_a, ln_g, ln_b, conv_b, *, name, ts=256, rc=32):
    s = z.shape[0]
    c = A_WIDTH
    hb = HALO_A

    def body(z_ref, zp_ref, wa_ref, lg_ref, lb_ref, wb_ref, ab_ref, ca_ref, exta, extb):
        keep = jnp.where(pl.program_id(0) > 0, 1.0, 0.0)
        zp = zp_ref[...]
        exta[0:hb, :] = zp[:, 0:c] * _sigmoid(zp[:, c:2 * c]) * keep
        extb[0:hb, :] = zp[:, 3 * c:4 * c] * zp[:, 4 * c:5 * c] * keep
        exta[hb:hb + ts, :] = z_ref[:, 0:c] * _sigmoid(z_ref[:, c:2 * c])
        extb[hb:hb + ts, :] = z_ref[:, 3 * c:4 * c] * z_ref[:, 4 * c:5 * c]
        lg = lg_ref[...]
        lb = lb_ref[...]
        for q in range(ts // rc):
            r0 = q * rc
            ca = _conv_taps(exta, wa_ref, A_TAPS, hb - (A_TAPS - 1), r0, rc)
            ca_ref[r0:r0 + rc, :] = ca
            mu = jnp.mean(ca, axis=-1, keepdims=True)
            xc = ca - mu
            rs = lax.rsqrt(jnp.mean(xc * xc, axis=-1, keepdims=True) + LN_EPS)
            l = xc * rs * lg + lb
            ab_ref[r0:r0 + rc, 0:c] = (l * _sigmoid(l)).astype(BF16)
            cbc = _conv_taps(extb, wb_ref, 3, hb - 2, r0, rc)
            ab_ref[r0:r0 + rc, c:2 * c] = (z_ref[r0:r0 + rc, 2 * c:3 * c] * cbc).astype(BF16)

    return pl.pallas_call(
        body, name=name, grid=(s // ts,),
        in_specs=[pl.BlockSpec((ts, 5 * c), lambda i: (i, 0)),
                  pl.BlockSpec((hb, 5 * c), lambda i: (_prev_blk(i, ts, hb), 0)),
                  pl.BlockSpec((32, c), lambda i: (0, 0)),
                  pl.BlockSpec((1, c), lambda i: (0, 0)),
                  pl.BlockSpec((1, c), lambda i: (0, 0)),
                  pl.BlockSpec((8, c), lambda i: (0, 0))],
        out_specs=[pl.BlockSpec((ts, 2 * c), lambda i: (i, 0)),
                   pl.BlockSpec((ts, c), lambda i: (i, 0))],
        out_shape=[jax.ShapeDtypeStruct((s, 2 * c), BF16), jax.ShapeDtypeStruct((s, c), F32)],
        scratch_shapes=[pltpu.VMEM((hb + ts, c), F32), pltpu.VMEM((hb + ts, c), F32)],
        compiler_params=_params(("parallel",)),
    )(z, z, conv_a, ln_g, ln_b, conv_b)


def _mix0_bwd(z, ca, dab, conv_a, ln_g, ln_b, conv_b, *, name, ts=256, rc=32):
    s = z.shape[0]
    c = A_WIDTH
    hb = HALO_A
    ta = A_TAPS

    def body(z_ref, zp_ref, zn_ref, ca_ref, can_ref, d_ref, dn_ref, wa_ref, lg_ref, lb_ref, wb_ref,
             dz_ref, dwa_ref, dwb_ref, dlg_ref, dlb_ref, exta, extb, extdca, extdcb):
        i = pl.program_id(0)
        keep_p = jnp.where(i > 0, 1.0, 0.0)
        keep_n = jnp.where(i < s // ts - 1, 1.0, 0.0)

        @pl.when(i == 0)
        def _():
            dwa_ref[...] = jnp.zeros_like(dwa_ref)
            dwb_ref[...] = jnp.zeros_like(dwb_ref)
            dlg_ref[...] = jnp.zeros_like(dlg_ref)
            dlb_ref[...] = jnp.zeros_like(dlb_ref)

        lg = lg_ref[...]
        lb = lb_ref[...]
        zp = zp_ref[...]
        exta[0:hb, :] = zp[:, 0:c] * _sigmoid(zp[:, c:2 * c]) * keep_p
        extb[0:hb, :] = zp[:, 3 * c:4 * c] * zp[:, 4 * c:5 * c] * keep_p
        exta[hb:hb + ts, :] = z_ref[:, 0:c] * _sigmoid(z_ref[:, c:2 * c])
        extb[hb:hb + ts, :] = z_ref[:, 3 * c:4 * c] * z_ref[:, 4 * c:5 * c]

        def ln_bwd(cav, dav):
            mu = jnp.mean(cav, axis=-1, keepdims=True)
            xc = cav - mu
            rs = lax.rsqrt(jnp.mean(xc * xc, axis=-1, keepdims=True) + LN_EPS)
            nv = xc * rs
            l = nv * lg + lb
            sg = _sigmoid(l)
            dl = dav * (sg * (1.0 + l * (1.0 - sg)))
            dnv = dl * lg
            dca = rs * (dnv - jnp.mean(dnv, axis=-1, keepdims=True)
                        - nv * jnp.mean(dnv * nv, axis=-1, keepdims=True))
            return dca, dl, nv

        dlg_acc = jnp.zeros((1, c), F32)
        dlb_acc = jnp.zeros((1, c), F32)
        for q in range(ts // rc):
            r0 = q * rc
            dca, dl, nv = ln_bwd(ca_ref[r0:r0 + rc, :], d_ref[r0:r0 + rc, 0:c])
            extdca[r0:r0 + rc, :] = dca
            dlg_acc = dlg_acc + jnp.sum(dl * nv, axis=0, keepdims=True)
            dlb_acc = dlb_acc + jnp.sum(dl, axis=0, keepdims=True)
            extdcb[r0:r0 + rc, :] = d_ref[r0:r0 + rc, c:2 * c] * z_ref[r0:r0 + rc, 2 * c:3 * c]
        dca_n, _, _ = ln_bwd(can_ref[...], dn_ref[:, 0:c])
        extdca[ts:ts + hb, :] = dca_n * keep_n
        extdcb[ts:ts + hb, :] = dn_ref[:, c:2 * c] * zn_ref[:, 2 * c:3 * c] * keep_n
        dlg_ref[...] += dlg_acc
        dlb_ref[...] += dlb_acc

        for q in range(ts // rc):
            r0 = q * rc
            zr = z_ref[r0:r0 + rc, :]
            dga = _conv_taps(extdca, wa_ref, ta, ta - 1, r0, rc, reverse=True)
            sg = _sigmoid(zr[:, c:2 * c])
            dz_ref[r0:r0 + rc, 0:c] = (dga * sg).astype(BF16)
            dz_ref[r0:r0 + rc, c:2 * c] = (dga * zr[:, 0:c] * sg * (1.0 - sg)).astype(BF16)
            cbc = _conv_taps(extb, wb_ref, 3, hb - 2, r0, rc)
            dz_ref[r0:r0 + rc, 2 * c:3 * c] = (d_ref[r0:r0 + rc, c:2 * c] * cbc).astype(BF16)
            dcb = _conv_taps(extdcb, wb_ref, 3, 2, r0, rc, reverse=True)
            dz_ref[r0:r0 + rc, 3 * c:4 * c] = (dcb * zr[:, 4 * c:5 * c]).astype(BF16)
            dz_ref[r0:r0 + rc, 4 * c:5 * c] = (dcb * zr[:, 3 * c:4 * c]).astype(BF16)

        dca_t = extdca[0:ts, :]
        for k in range(ta):
            dwa_ref[k:k + 1, :] += jnp.sum(dca_t * exta[pl.ds(hb - (ta - 1) + k, ts), :], axis=0, keepdims=True)
        dcb_t = extdcb[0:ts, :]
        for k in range(3):
            dwb_ref[k:k + 1, :] += jnp.sum(dcb_t * extb[pl.ds(hb - 2 + k, ts), :], axis=0, keepdims=True)

    def tile(w):
        return pl.BlockSpec((ts, w), lambda i: (i, 0))

    def prev(w):
        return pl.BlockSpec((hb, w), lambda i: (_prev_blk(i, ts, hb), 0))

    def nxt(w):
        return pl.BlockSpec((hb, w), lambda i: (_next_blk(i, ts, hb, s), 0))

    def const(r, w):
        return pl.BlockSpec((r, w), lambda i: (0, 0))

    return pl.pallas_call(
        body, name=name, grid=(s // ts,),
        in_specs=[tile(5 * c), prev(5 * c), nxt(5 * c), tile(c), nxt(c), tile(2 * c), nxt(2 * c),
                  const(32, c), const(1, c), const(1, c), const(8, c)],
        out_specs=[tile(5 * c), const(32, c), const(8, c), const(1, c), const(1, c)],
        out_shape=[jax.ShapeDtypeStruct((s, 5 * c), BF16), jax.ShapeDtypeStruct((32, c), F32),
                   jax.ShapeDtypeStruct((8, c), F32), jax.ShapeDtypeStruct((1, c), F32),
                   jax.ShapeDtypeStruct((1, c), F32)],
        scratch_shapes=[pltpu.VMEM((hb + ts, c), F32), pltpu.VMEM((hb + ts, c), F32),
                        pltpu.VMEM((ts + hb, c), F32), pltpu.VMEM((ts + hb, c), F32)],
        compiler_params=_params(("arbitrary",)),
    )(z, z, z, ca, ca, dab, dab, conv_a, ln_g, ln_b, conv_b)


def _ffn_up(x, g, w8, *, name, tm=1024):
    s, d = x.shape
    nb, _, c = w8.shape
    tm = min(tm, s)

    def body(x_ref, g_ref, w_ref, h_ref, u_ref, hs_ref):
        @pl.when(pl.program_id(1) == 0)
        def _():
            hv = _rms_rows(x_ref[...], g_ref[...]).astype(BF16)
            hs_ref[...] = hv
            h_ref[...] = hv

        u_ref[...] = jnp.dot(hs_ref[...], w_ref[...], preferred_element_type=F32).astype(BF16)

    return pl.pallas_call(
        body, name=name, grid=(s // tm, nb),
        in_specs=[pl.BlockSpec((tm, d), lambda i, k: (i, 0)),
                  pl.BlockSpec((1, d), lambda i, k: (0, 0)),
                  pl.BlockSpec((None, d, c), lambda i, k: (k, 0, 0))],
        out_specs=[pl.BlockSpec((tm, d), lambda i, k: (i, 0)),
                   pl.BlockSpec((None, tm, c), lambda i, k: (k, i, 0))],
        out_shape=[jax.ShapeDtypeStruct((s, d), BF16), jax.ShapeDtypeStruct((nb, s, c), BF16)],
        scratch_shapes=[pltpu.VMEM((tm, d), BF16)],
        compiler_params=_params(("parallel", "arbitrary")),
    )(x, g, w8)


def _ffn_mid(up8, wc8, *, name, ts=512, rc=32):
    nb, s, c = up8.shape
    hb = HALO_S
    ts = min(ts, s)

    def body(g_ref, gp_ref, v_ref, vp_ref, wg_ref, wv_ref, act_ref, extg, extv):
        keep = jnp.where(pl.program_id(0) > 0, 1.0, 0.0)
        extg[0:hb, :] = gp_ref[...].astype(F32) * keep
        extv[0:hb, :] = vp_ref[...].astype(F32) * keep
        extg[hb:hb + ts, :] = g_ref[...].astype(F32)
        extv[hb:hb + ts, :] = v_ref[...].astype(F32)
        for q in range(ts // rc):
            r0 = q * rc
            gg = _conv_taps(extg, wg_ref, 3, hb - 2, r0, rc)
            vv = _conv_taps(extv, wv_ref, 3, hb - 2, r0, rc)
            act_ref[r0:r0 + rc, :] = (gg * _sigmoid(gg) * vv).astype(BF16)

    def blk(rows, off, halo):
        if halo:
            return pl.BlockSpec((None, rows, c), lambda i, j: (j + off, _prev_blk(i, ts, hb), 0))
        return pl.BlockSpec((None, rows, c), lambda i, j: (j + off, i, 0))

    def taps(off):
        return pl.BlockSpec((None, 8, c), lambda i, j: (j + off, 0, 0))

    return pl.pallas_call(
        body, name=name, grid=(s // ts, N_PAIR),
        in_specs=[blk(ts, 0, False), blk(hb, 0, True), blk(ts, N_PAIR, False), blk(hb, N_PAIR, True),
                  taps(0), taps(N_PAIR)],
        out_specs=pl.BlockSpec((None, ts, c), lambda i, j: (j, i, 0)),
        out_shape=jax.ShapeDtypeStruct((N_PAIR, s, c), BF16),
        scratch_shapes=[pltpu.VMEM((hb + ts, c), F32), pltpu.VMEM((hb + ts, c), F32)],
        compiler_params=_params(("parallel", "parallel")),
    )(up8, up8, up8, up8, wc8, wc8)


def _ffn_down(act4, wd4, x, *, name, tm=512):
    npair, s, c = act4.shape
    d = x.shape[1]
    tm = min(tm, s)

    def body(a_ref, w_ref, x_ref, o_ref):
        acc = x_ref[...]
        for j in range(npair):
            acc = acc + jnp.dot(a_ref[j], w_ref[j], preferred_element_type=F32)
        o_ref[...] = acc

    return pl.pallas_call(
        body, name=name, grid=(s // tm,),
        in_specs=[pl.BlockSpec((npair, tm, c), lambda i: (0, i, 0)),
                  pl.BlockSpec((npair, c, d), lambda i: (0, 0, 0)),
                  pl.BlockSpec((tm, d), lambda i: (i, 0))],
        out_specs=pl.BlockSpec((tm, d), lambda i: (i, 0)),
        out_shape=jax.ShapeDtypeStruct((s, d), F32),
        compiler_params=_params(("parallel",)),
    )(act4, wd4, x)


def _ffn_dact(db, wd4, *, name, tm=1024):
    s, d = db.shape
    npair, c, _ = wd4.shape
    tm = min(tm, s)

    def body(d_ref, w_ref, o_ref):
        o_ref[...] = lax.dot_general(d_ref[...], w_ref[...], (((1,), (1,)), ((), ())),
                                     preferred_element_type=F32).astype(BF16)

    return pl.pallas_call(
        body, name=name, grid=(s // tm, npair),
        in_specs=[pl.BlockSpec((tm, d), lambda i, j: (i, 0)),
                  pl.BlockSpec((None, c, d), lambda i, j: (j, 0, 0))],
        out_specs=pl.BlockSpec((None, tm, c), lambda i, j: (j, i, 0)),
        out_shape=jax.ShapeDtypeStruct((npair, s, c), BF16),
        compiler_params=_params(("parallel", "parallel")),
    )(db, wd4)


def _ffn_dwdown(act4, db, *, name):
    npair, s, c = act4.shape
    d = db.shape[1]

    def body(a_ref, d_ref, o_ref):
        o_ref[...] = lax.dot_general(a_ref[...], d_ref[...], (((0,), (0,)), ((), ())),
                                     preferred_element_type=F32).astype(BF16)

    return pl.pallas_call(
        body, name=name, grid=(npair,),
        in_specs=[pl.BlockSpec((None, s, c), lambda j: (j, 0, 0)),
                  pl.BlockSpec((s, d), lambda j: (0, 0))],
        out_specs=pl.BlockSpec((None, c, d), lambda j: (j, 0, 0)),
        out_shape=jax.ShapeDtypeStruct((npair, c, d), BF16),
        compiler_params=_params(("parallel",)),
    )(act4, db)


def _ffn_midbwd(up8, dact4, wc8, *, name, ts=512, rc=32):
    nb, s, c = up8.shape
    hb = HALO_S
    ts = min(ts, s)
    n_i = s // ts

    def body(g_ref, gp_ref, gn_ref, v_ref, vp_ref, vn_ref, d_ref, dn_ref, wg_ref, wv_ref,
             dg_ref, dv_ref, dwg_ref, dwv_ref, extg, extv, extdg, extdv):
        i = pl.program_id(1)
        keep_p = jnp.where(i > 0, 1.0, 0.0)
        keep_n = jnp.where(i < n_i - 1, 1.0, 0.0)

        @pl.when(i == 0)
        def _():
            dwg_ref[...] = jnp.zeros_like(dwg_ref)
            dwv_ref[...] = jnp.zeros_like(dwv_ref)

        for ext, cur, prv, nxt in ((extg, g_ref, gp_ref, gn_ref), (extv, v_ref, vp_ref, vn_ref)):
            ext[0:hb, :] = prv[...].astype(F32) * keep_p
            ext[hb:hb + ts, :] = cur[...].astype(F32)
            ext[hb + ts:hb + ts + hb, :] = nxt[...].astype(F32)

        def du_rows(r0, rows, dav):
            gg = _conv_taps(extg, wg_ref, 3, hb - 2, r0, rows)
            vv = _conv_taps(extv, wv_ref, 3, hb - 2, r0, rows)
            sg = _sigmoid(gg)
            extdg[r0:r0 + rows, :] = dav * vv * (sg * (1.0 + gg * (1.0 - sg)))
            extdv[r0:r0 + rows, :] = dav * (gg * sg)

        for q in range(ts // rc):
            du_rows(q * rc, rc, d_ref[q * rc:q * rc + rc, :].astype(F32))
        du_rows(ts, 8, dn_ref[0:8, :].astype(F32) * keep_n)

        for q in range(ts // rc):
            r0 = q * rc
            dg_ref[r0:r0 + rc, :] = _conv_taps(extdg, wg_ref, 3, 2, r0, rc, reverse=True).astype(BF16)
            dv_ref[r0:r0 + rc, :] = _conv_taps(extdv, wv_ref, 3, 2, r0, rc, reverse=True).astype(BF16)
        for ext, extd, dw in ((extg, extdg, dwg_ref), (extv, extdv, dwv_ref)):
            du_t = extd[0:ts, :]
            for k in range(3):
                dw[k:k + 1, :] += jnp.sum(du_t * ext[pl.ds(hb - 2 + k, ts), :], axis=0, keepdims=True)

    def blk(off, which):
        if which == "prev":
            return pl.BlockSpec((None, hb, c), lambda j, i: (j + off, _prev_blk(i, ts, hb), 0))
        if which == "next":
            return pl.BlockSpec((None, hb, c), lambda j, i: (j + off, _next_blk(i, ts, hb, s), 0))
        return pl.BlockSpec((None, ts, c), lambda j, i: (j + off, i, 0))

    def taps(off):
        return pl.BlockSpec((None, 8, c), lambda j, i: (j + off, 0, 0))

    tile = pl.BlockSpec((None, ts, c), lambda j, i: (j, i, 0))
    acc = pl.BlockSpec((None, 8, c), lambda j, i: (j, 0, 0))
    return pl.pallas_call(
        body, name=name, grid=(N_PAIR, n_i),
        in_specs=[blk(0, "tile"), blk(0, "prev"), blk(0, "next"),
                  blk(N_PAIR, "tile"), blk(N_PAIR, "prev"), blk(N_PAIR, "next"),
                  blk(0, "tile"), blk(0, "next"), taps(0), taps(N_PAIR)],
        out_specs=[tile, tile, acc, acc],
        out_shape=[jax.ShapeDtypeStruct((N_PAIR, s, c), BF16), jax.ShapeDtypeStruct((N_PAIR, s, c), BF16),
                   jax.ShapeDtypeStruct((N_PAIR, 8, c), F32), jax.ShapeDtypeStruct((N_PAIR, 8, c), F32)],
        scratch_shapes=[pltpu.VMEM((hb + ts + hb, c), F32), pltpu.VMEM((hb + ts + hb, c), F32),
                        pltpu.VMEM((ts + 8, c), F32), pltpu.VMEM((ts + 8, c), F32)],
        compiler_params=_params(("parallel", "arbitrary")),
    )(up8, up8, up8, up8, up8, up8, dact4, dact4, wc8, wc8)


def _ffn_dh(dupg, dupv, w8, x, g, dres, *, name, tm=256):
    npair, s, c = dupg.shape
    d = x.shape[1]
    tm = min(tm, s)

    def body(dg_ref, dv_ref, w_ref, x_ref, g_ref, dres_ref, dx_ref, dxb_ref, dgain_ref):
        @pl.when(pl.program_id(0) == 0)
        def _():
            dgain_ref[...] = jnp.zeros_like(dgain_ref)

        nt = (((1,), (1,)), ((), ()))
        dh = None
        for j in range(npair):
            for src, k in ((dg_ref, j), (dv_ref, j + npair)):
                part = lax.dot_general(src[j], w_ref[k], nt, preferred_element_type=F32)
                dh = part if dh is None else dh + part
        dx, dgr = _rms_bwd_rows(x_ref[...], g_ref[...], dh)
        dx = dres_ref[...] + dx
        dx_ref[...] = dx
        dxb_ref[...] = dx.astype(BF16)
        dgain_ref[...] += jnp.sum(dgr, axis=0, keepdims=True)

    row = pl.BlockSpec((tm, d), lambda i: (i, 0))
    vec = pl.BlockSpec((1, d), lambda i: (0, 0))
    dup = pl.BlockSpec((npair, tm, c), lambda i: (0, i, 0))
    return pl.pallas_call(
        body, name=name, grid=(s // tm,),
        in_specs=[dup, dup, pl.BlockSpec((2 * npair, d, c), lambda i: (0, 0, 0)), row, vec, row],
        out_specs=[row, row, vec],
        out_shape=[jax.ShapeDtypeStruct((s, d), F32), jax.ShapeDtypeStruct((s, d), BF16),
                   jax.ShapeDtypeStruct((1, d), F32)],
        compiler_params=_params(("arbitrary",)),
    )(dupg, dupv, w8, x, g, dres)


def _ffn_dwup(h, dupg, dupv, *, name, tm=512):
    npair, s, c = dupg.shape
    d = h.shape[1]

    def body(h_ref, dg_ref, dv_ref, o_ref):
        tn = (((0,), (0,)), ((), ()))
        k = pl.program_id(1)

        @pl.when(k < npair)
        def _():
            o_ref[...] = lax.dot_general(h_ref[...], dg_ref[...], tn, preferred_element_type=F32).astype(BF16)

        @pl.when(k >= npair)
        def _():
            o_ref[...] = lax.dot_general(h_ref[...], dv_ref[...], tn, preferred_element_type=F32).astype(BF16)

    return pl.pallas_call(
        body, name=name, grid=(d // tm, 2 * npair),
        in_specs=[pl.BlockSpec((s, tm), lambda m, k: (0, m)),
                  pl.BlockSpec((None, s, c), lambda m, k: (jnp.minimum(k, npair - 1), 0, 0)),
                  pl.BlockSpec((None, s, c), lambda m, k: (jnp.maximum(k - npair, 0), 0, 0))],
        out_specs=pl.BlockSpec((None, tm, c), lambda m, k: (k, m, 0)),
        out_shape=jax.ShapeDtypeStruct((2 * npair, d, c), BF16),
        compiler_params=_params(("parallel", "arbitrary")),
    )(h, dupg, dupv)


def _pool_counts(i, ts, rows, window):
    t = lax.broadcasted_iota(jnp.int32, (rows, 1), 0) + i * ts + 1
    return jnp.minimum(t, window).astype(F32)


def _pool_fwd(x, g, *, name, ts=256):
    s, d = x.shape
    hb = HALO_S
    pg = POOL_GROUP
    ts = min(ts, s)

    def body(x_ref, xp_ref, g_ref, p_ref, ext):
        i = pl.program_id(0)
        keep = jnp.where(i > 0, 1.0, 0.0)
        gv = g_ref[...]
        ext[0:hb, :] = _rms_rows(xp_ref[...], gv) * keep
        ext[hb:hb + ts, :] = _rms_rows(x_ref[...], gv)
        for gi, w in enumerate(POOL_WINDOWS):
            cols = slice(gi * pg, (gi + 1) * pg)
            h = ext[hb:hb + ts, cols]
            acc = h
            for q in range(1, w):
                acc = acc + ext[pl.ds(hb - q, ts), cols]
            p_ref[:, cols] = (acc / _pool_counts(i, ts, ts, w) - h).astype(BF16)

    return pl.pallas_call(
        body, name=name, grid=(s // ts,),
        in_specs=[pl.BlockSpec((ts, d), lambda i: (i, 0)),
                  pl.BlockSpec((hb, d), lambda i: (_prev_blk(i, ts, hb), 0)),
                  pl.BlockSpec((1, d), lambda i: (0, 0))],
        out_specs=pl.BlockSpec((ts, d), lambda i: (i, 0)),
        out_shape=jax.ShapeDtypeStruct((s, d), BF16),
        scratch_shapes=[pltpu.VMEM((hb + ts, d), F32)],
        compiler_params=_params(("parallel",)),
    )(x, x, g)


def _pool_mm_fwd(p, w_pool, scale, x, *, name, ts=1024):
    s, d = x.shape
    pg = POOL_GROUP
    ts = min(ts, s)

    def body(p_ref, w_ref, s_ref, x_ref, o_ref, yu_ref):
        yu = jnp.dot(p_ref[...], w_ref[...], preferred_element_type=F32)
        yu_ref[...] = yu.astype(BF16)
        o_ref[...] = x_ref[...] + yu * s_ref[...]

    blk = pl.BlockSpec((ts, pg), lambda i, gi: (i, gi))
    return pl.pallas_call(
        body, name=name, grid=(s // ts, d // pg),
        in_specs=[blk, pl.BlockSpec((None, pg, pg), lambda i, gi: (gi, 0, 0)),
                  pl.BlockSpec((1, pg), lambda i, gi: (0, gi)), blk],
        out_specs=[blk, blk],
        out_shape=[jax.ShapeDtypeStruct((s, d), F32), jax.ShapeDtypeStruct((s, d), BF16)],
        compiler_params=_params(("parallel", "parallel")),
    )(p, w_pool, scale, x)


def _pool_mm_bwd(dres, w_pool, scale, yu, *, name, ts=1024):
    s, d = dres.shape
    pg = POOL_GROUP
    ts = min(ts, s)

    def body(d_ref, w_ref, s_ref, yu_ref, dyc_ref, dp_ref, ds_ref):
        @pl.when(pl.program_id(1) == 0)
        def _():
            ds_ref[...] = jnp.zeros_like(ds_ref)

        dv = d_ref[...]
        dyc = (dv * s_ref[...]).astype(BF16)
        dyc_ref[...] = dyc
        dp_ref[...] = lax.dot_general(dyc, w_ref[...], (((1,), (1,)), ((), ())), preferred_element_type=F32)
        ds_ref[...] += jnp.sum(dv * yu_ref[...].astype(F32), axis=0, keepdims=True)

    blk = pl.BlockSpec((ts, pg), lambda gi, i: (i, gi))
    vec = pl.BlockSpec((1, pg), lambda gi, i: (0, gi))
    return pl.pallas_call(
        body, name=name, grid=(d // pg, s // ts),
        in_specs=[blk, pl.BlockSpec((None, pg, pg), lambda gi, i: (gi, 0, 0)), vec, blk],
        out_specs=[blk, blk, vec],
        out_shape=[jax.ShapeDtypeStruct((s, d), BF16), jax.ShapeDtypeStruct((s, d), F32),
                   jax.ShapeDtypeStruct((1, d), F32)],
        compiler_params=_params(("parallel", "arbitrary")),
    )(dres, w_pool, scale, yu)


def _pool_dw(p, dyc, *, name):
    s, d = p.shape
    pg = POOL_GROUP

    def body(p_ref, d_ref, o_ref):
        o_ref[...] = lax.dot_general(p_ref[...], d_ref[...], (((0,), (0,)), ((), ())),
                                     preferred_element_type=F32).astype(BF16)

    blk = pl.BlockSpec((s, pg), lambda gi: (0, gi))
    return pl.pallas_call(
        body, name=name, grid=(d // pg,),
        in_specs=[blk, blk], out_specs=pl.BlockSpec((None, pg, pg), lambda gi: (gi, 0, 0)),
        out_shape=jax.ShapeDtypeStruct((d // pg, pg, pg), BF16),
        compiler_params=_params(("parallel",)),
    )(p, dyc)


def _pool_bwd(dp, x, g, dres, *, name, ts=256):
    s, d = x.shape
    hb = HALO_S
    pg = POOL_GROUP
    ts = min(ts, s)
    n_i = s // ts

    def body(dp_ref, dpn_ref, x_ref, g_ref, dres_ref, dx_ref, dxb_ref, dg_ref, ext, dh):
        i = pl.program_id(0)
        keep_n = jnp.where(i < n_i - 1, 1.0, 0.0)

        @pl.when(i == 0)
        def _():
            dg_ref[...] = jnp.zeros_like(dg_ref)

        for gi, w in enumerate(POOL_WINDOWS):
            cols = slice(gi * pg, (gi + 1) * pg)
            ext[0:ts, cols] = dp_ref[:, cols] / _pool_counts(i, ts, ts, w)
            ext[ts:ts + hb, cols] = dpn_ref[:, cols] / _pool_counts(i + 1, ts, hb, w) * keep_n
        for gi, w in enumerate(POOL_WINDOWS):
            cols = slice(gi * pg, (gi + 1) * pg)
            acc = ext[0:ts, cols]
            for q in range(1, w):
                acc = acc + ext[pl.ds(q, ts), cols]
            dh[:, cols] = acc - dp_ref[:, cols]
        dx, dgr = _rms_bwd_rows(x_ref[...], g_ref[...], dh[...])
        dx = dres_ref[...] + dx
        dx_ref[...] = dx
        dxb_ref[...] = dx.astype(BF16)
        dg_ref[...] += jnp.sum(dgr, axis=0, keepdims=True)

    row = pl.BlockSpec((ts, d), lambda i: (i, 0))
    vec = pl.BlockSpec((1, d), lambda i: (0, 0))
    return pl.pallas_call(
        body, name=name, grid=(n_i,),
        in_specs=[row, pl.BlockSpec((hb, d), lambda i: (_next_blk(i, ts, hb, s), 0)), row, vec, row],
        out_specs=[row, row, vec],
        out_shape=[jax.ShapeDtypeStruct((s, d), F32), jax.ShapeDtypeStruct((s, d), BF16),
                   jax.ShapeDtypeStruct((1, d), F32)],
        scratch_shapes=[pltpu.VMEM((ts + hb, d), F32), pltpu.VMEM((ts, d), F32)],
        compiler_params=_params(("arbitrary",)),
    )(dp, dp, x, g, dres)


def _pad_rows(w, rows):
    pad = [(0, 0)] * (w.ndim - 2) + [(0, rows - w.shape[-2]), (0, 0)]
    return jnp.pad(w, pad)


def _ffn_layer_fwd(x, nf, w8, wc8, wd4, tag):
    h, up8 = _ffn_up(x, nf, w8, name=f"ffn{tag}_up")
    act4 = _ffn_mid(up8, wc8, name=f"ffn{tag}_mid")
    x_out = _ffn_down(act4, wd4, x, name=f"ffn{tag}_down")
    return x_out, (h, up8, act4)


def _ffn_layer_bwd(d, db, x, nf, w8, wc8, wd4, saved, tag):
    h, up8, act4 = saved
    dact4 = _ffn_dact(db, wd4, name=f"ffn{tag}_dact")
    dwd4 = _ffn_dwdown(act4, db, name=f"ffn{tag}_dwdown")
    dupg, dupv, dwg, dwv = _ffn_midbwd(up8, dact4, wc8, name=f"ffn{tag}_midbwd")
    dx, dxb, dnf = _ffn_dh(dupg, dupv, w8, x, nf, d, name=f"ffn{tag}_dh")
    dw8 = _ffn_dwup(h, dupg, dupv, name=f"ffn{tag}_dwup")
    dwc8 = jnp.concatenate([dwg, dwv], axis=0)[:, :3]
    return dx, dxb, dnf, dw8, dwc8, dwd4


def _local_step(x, target, weights, grads_ready):
    w0 = weights("mix0", x)
    wa = _pad_rows(w0["conv_a"], 32)
    wb = _pad_rows(w0["conv_b"], 8)
    wc = [_pad_rows(w0["conv_ffn"][l], 8) for l in range(2)]
    h0, z = _rms_mm(x, w0["norm_mix_even"], w0["w_in"], name="mix0_in", out_dtype=F32, tn=1280)
    ab, ca = _mix0_fwd(z, wa, w0["ln_a_g"], w0["ln_a_b"], wb, name="mix0_mid")
    x1 = _mm(ab, w0["w_out"], add=x, name="mix0_out", tm=1024, tn=1024)
    w1 = weights("ffn0", x1)
    x2, ffn0 = _ffn_layer_fwd(x1, w0["norm_ffn"][0:1], w1["w_up"], wc[0], w1["w_down"], 0)
    w2 = weights("ffn1", x2)
    p = _pool_fwd(x2, w0["norm_mix_odd"], name="pool_mid")
    x3, yu = _pool_mm_fwd(p, w2["w_pool"], w0["pool_scale"], x2, name="pool_mm")
    x4, ffn1 = _ffn_layer_fwd(x3, w0["norm_ffn"][1:2], w2["w_up"], wc[1], w2["w_down"], 1)
    loss, d4, d4b, g_norm_final = _final_loss(x4, w0["norm_final"], target, name="final_loss")

    d3, d3b, g_nf1, g_up1, g_wc1, g_down1 = _ffn_layer_bwd(
        d4, d4b, x3, w0["norm_ffn"][1:2], w2["w_up"], wc[1], w2["w_down"], ffn1, 1)
    d3 = grads_ready("ffn1", {"w_up1": g_up1, "w_down1": g_down1}, d3)
    dyc, dp, g_scale = _pool_mm_bwd(d3, w2["w_pool"], w0["pool_scale"], yu, name="pool_mm_bwd")
    g_pool = _pool_dw(p, dyc, name="pool_dw")
    d2, d2b, g_nmo = _pool_bwd(dp, x2, w0["norm_mix_odd"], d3, name="pool_midbwd")
    d1, d1b, g_nf0, g_up0, g_wc0, g_down0 = _ffn_layer_bwd(
        d2, d2b, x1, w0["norm_ffn"][0:1], w1["w_up"], wc[0], w1["w_down"], ffn0, 0)
    d1b = grads_ready("ffn0", {"w_pool": g_pool, "w_up0": g_up0, "w_down0": g_down0}, d1b)
    dab = _mm(d1b, w0["w_out"], tb=True, name="mix0_dab", tm=1024, tn=1024)
    g_out = _mm(ab, d1b, ta=True, out_dtype=BF16, name="mix0_dwout", tm=1024, tn=512)
    dz, g_wa, g_wb, g_lg, g_lb = _mix0_bwd(z, ca, dab, wa, w0["ln_a_g"], w0["ln_a_b"], wb, name="mix0_midbwd")
    dh0 = _mm(dz, w0["w_in"], tb=True, name="mix0_dh", tm=1024, tn=1024)
    g_in = _mm(h0, dz, ta=True, out_dtype=BF16, name="mix0_dwin", tm=1024, tn=512)
    dx, _, g_nme = _rms_bwd(x, w0["norm_mix_even"], dh0, d1, name="mix0_rmsbwd")
    dx = grads_ready("mix0", {"w_in": g_in, "w_out": g_out}, dx)

    small = {
        "norm_mix_even": g_nme, "conv_a": g_wa[:A_TAPS], "ln_a_g": g_lg, "ln_a_b": g_lb, "conv_b": g_wb[:3],
        "norm_mix_odd": g_nmo, "pool_scale": g_scale, "norm_ffn": jnp.concatenate([g_nf0, g_nf1], axis=0),
        "conv_ffn": [g_wc0, g_wc1], "norm_final": g_norm_final,
    }
    return loss[0, 0], dx, small


def _my_pos():
    return lax.axis_index("x"), lax.axis_index("y"), lax.axis_index("c")


def _flip(pos, r):
    x, y, c = pos
    return (1 - x if r & 4 else x, 1 - y if r & 2 else y, 1 - c if r & 1 else c)


def _dev_index(pos):
    return 4 * pos[0] + 2 * pos[1] + pos[2]


_HBM = pl.BlockSpec(memory_space=pltpu.HBM)
_SEM = pl.BlockSpec(memory_space=pltpu.SEMAPHORE)
_EFFECT = pltpu.SideEffectType.DATAFLOW_SIDE_EFFECTING


def _exchange_copy(ins, lands, send_sems, recv_sems, scatter, pos, a, r, receiving):
    me = _dev_index(pos)
    peer = _flip(pos, r)
    dest = _dev_index(pos) if receiving else _dev_index(peer)
    src = ins[a].at[dest] if scatter[a] else ins[a]
    slot = _dev_index(peer) if receiving else me
    return pltpu.make_async_remote_copy(
        src_ref=src, dst_ref=lands[a].at[slot], send_sem=send_sems.at[a * (N_DEV - 1) + r - 1],
        recv_sem=recv_sems.at[a * (N_DEV - 1) + r - 1],
        device_id=peer, device_id_type=pl.DeviceIdType.MESH)


def _exchange_start(arrays, scatter, *, name):
    n = len(arrays)
    me = _dev_index(_my_pos())
    lands = []
    for arr, sc in zip(arrays, scatter):
        own = lax.dynamic_index_in_dim(arr, me, 0, keepdims=True) if sc else arr[None]
        shape = arr.shape if sc else (N_DEV,) + arr.shape
        lands.append(lax.dynamic_update_slice(lax.empty(shape, arr.dtype), own, (me,) + (0,) * (len(shape) - 1)))

    def body(*refs):
        ins, lnd = refs[:n], refs[n:2 * n]
        send_sems, recv_sems = refs[2 * n], refs[2 * n + 1]
        token = refs[-1]
        pos = _my_pos()
        for a in range(n):
            for r in range(1, N_DEV):
                _exchange_copy(ins, lnd, send_sems, recv_sems, scatter, pos, a, r, receiving=False).start()
        token[...] = jnp.zeros_like(token)

    bufs = [pltpu.with_memory_space_constraint(t, pltpu.HBM) for t in list(arrays) + lands]
    sems = pltpu.SemaphoreType.DMA((n * (N_DEV - 1),))
    res = pl.pallas_call(
        body, name=name,
        out_shape=(sems, sems, *[pltpu.HBM(t.shape, t.dtype) for t in bufs], jax.ShapeDtypeStruct((8, 128), F32)),
        in_specs=[_HBM] * (2 * n),
        out_specs=(_SEM, _SEM, *[_HBM] * (2 * n), pl.BlockSpec(memory_space=pltpu.VMEM)),
        input_output_aliases={i: 2 + i for i in range(2 * n)},
        compiler_params=pltpu.CompilerParams(has_side_effects=_EFFECT),
    )(*bufs)
    return res[0], res[1], list(res[2:2 + n]), list(res[2 + n:2 + 2 * n]), res[-1]


def _exchange_wait(started, scatter, after, *, name):
    send_sems, recv_sems, arrays, lands, _ = started
    n = len(arrays)

    def body(*refs):
        ins, lnd = refs[:n], refs[n:2 * n]
        send_sems, recv_sems = refs[2 * n], refs[2 * n + 1]
        pos = _my_pos()
        for a in range(n):
            for r in range(1, N_DEV):
                _exchange_copy(ins, lnd, send_sems, recv_sems, scatter, pos, a, r, receiving=False).wait_send()
                _exchange_copy(ins, lnd, send_sems, recv_sems, scatter, pos, a, r, receiving=True).wait_recv()

    bufs = list(arrays) + list(lands)
    res = pl.pallas_call(
        body, name=name,
        out_shape=tuple(pltpu.HBM(t.shape, t.dtype) for t in bufs),
        in_specs=[_HBM] * (2 * n) + [_SEM, _SEM, pl.BlockSpec(memory_space=pl.ANY)],
        out_specs=tuple([_HBM] * (2 * n)),
        input_output_aliases={i: i for i in range(2 * n)},
        compiler_params=pltpu.CompilerParams(has_side_effects=_EFFECT),
    )(*bufs, send_sems, recv_sems, after)
    return list(res[n:])


def _adamw(parts, w, m, v, *, name, tr):
    r, c = w.shape
    assert r % tr == 0

    def body(p_ref, w_ref, m_ref, v_ref, g_ref, d_ref, mo_ref, vo_ref):
        g = p_ref[0].astype(F32)
        for k in range(1, N_DEV):
            g = g + p_ref[k].astype(F32)
        mn = ADAM_B1 * m_ref[...] + (1.0 - ADAM_B1) * g
        vn = ADAM_B2 * v_ref[...] + (1.0 - ADAM_B2) * (g * g)
        m_hat = mn / (1.0 - ADAM_B1 ** ADAM_STEP)
        v_hat = vn / (1.0 - ADAM_B2 ** ADAM_STEP)
        g_ref[...] = g
        d_ref[...] = -ADAM_LR * (m_hat / (jnp.sqrt(v_hat) + ADAM_EPS) + ADAM_WD * w_ref[...])
        mo_ref[...] = mn
        vo_ref[...] = vn

    blk = pl.BlockSpec((tr, c), lambda i: (i, 0))
    return pl.pallas_call(
        body, name=name, grid=(r // tr,),
        in_specs=[pl.BlockSpec((N_DEV, tr, c), lambda i: (0, i, 0)), blk, blk, blk],
        out_specs=[blk] * 4, out_shape=[jax.ShapeDtypeStruct((r, c), F32)] * 4,
        compiler_params=_params(("parallel",)),
    )(parts, w, m, v)


def _pack(parts, lead=()):
    flat = jnp.concatenate([p.reshape(lead + (-1,)) for p in parts], axis=-1)
    n = flat.shape[-1]
    rows = -(-n // (8 * 128)) * 8
    flat = jnp.pad(flat, [(0, 0)] * len(lead) + [(0, rows * 128 - n)])
    return flat.reshape(lead + (rows, 128))


def _unpack(slab, shapes):
    flat = slab.reshape(-1)
    out, off = [], 0
    for shp in shapes:
        size = 1
        for dim in shp:
            size *= dim
        out.append(flat[off:off + size].reshape(shp))
        off += size
    return out


def _to_dev_major(g, axis):
    shp = g.shape
    g = g.reshape(shp[:axis] + (N_DEV, shp[axis] // N_DEV) + shp[axis + 1:])
    return jnp.moveaxis(g, axis, 0)


def _from_dev_major(g, axis):
    g = jnp.moveaxis(g, 0, axis)
    shp = g.shape
    return g.reshape(shp[:axis] + (shp[axis] * shp[axis + 1],) + shp[axis + 2:])


SMALL_SHARDED = ("conv_a", "conv_b", "norm_mix_odd", "pool_scale", "conv_ffn_w")
SMALL_REPLICATED = ("norm_mix_even", "ln_a_g", "ln_a_b", "norm_ffn", "norm_final")
BIG = {"w_in": ("w_in", None, 512), "w_out": ("w_out", None, 128), "w_pool": ("w_pool", None, 128),
       "w_up0": ("w_up", 0, 256), "w_up1": ("w_up", 1, 256),
       "w_down0": ("w_down", 0, 352), "w_down1": ("w_down", 1, 352)}


def kernel(x, norm_mix_even, w_in, conv_a, ln_a_g, ln_a_b, conv_b, w_out, norm_mix_odd, w_pool, pool_scale, norm_ffn, w_up, conv_ffn_w, w_down, norm_final, loss_target, m_norm_mix_even, m_w_in, m_conv_a, m_ln_a_g, m_ln_a_b, m_conv_b, m_w_out, m_norm_mix_odd, m_w_pool, m_pool_scale, m_norm_ffn, m_w_up, m_conv_ffn_w, m_w_down, m_norm_final, v_norm_mix_even, v_w_in, v_conv_a, v_ln_a_g, v_ln_a_b, v_conv_b, v_w_out, v_norm_mix_odd, v_w_pool, v_pool_scale, v_norm_ffn, v_w_up, v_conv_ffn_w, v_w_down, v_norm_final):
    names = ("norm_mix_even", "w_in", "conv_a", "ln_a_g", "ln_a_b", "conv_b", "w_out", "norm_mix_odd", "w_pool",
             "pool_scale", "norm_ffn", "w_up", "conv_ffn_w", "w_down", "norm_final")
    wts = dict(zip(names, (norm_mix_even, w_in, conv_a, ln_a_g, ln_a_b, conv_b, w_out, norm_mix_odd, w_pool,
                           pool_scale, norm_ffn, w_up, conv_ffn_w, w_down, norm_final)))
    mom = dict(zip(names, (m_norm_mix_even, m_w_in, m_conv_a, m_ln_a_g, m_ln_a_b, m_conv_b, m_w_out, m_norm_mix_odd,
                           m_w_pool, m_pool_scale, m_norm_ffn, m_w_up, m_conv_ffn_w, m_w_down, m_norm_final)))
    var = dict(zip(names, (v_norm_mix_even, v_w_in, v_conv_a, v_ln_a_g, v_ln_a_b, v_conv_b, v_w_out, v_norm_mix_odd,
                           v_w_pool, v_pool_scale, v_norm_ffn, v_w_up, v_conv_ffn_w, v_w_down, v_norm_final)))
    d = x.shape[-1]
    n_up = w_up.shape[-1]

    def shard2d(t, key):
        param, layer, _ = BIG[key]
        a = t[param][0] if layer is None else t[param][layer]
        return a.reshape(-1, a.shape[-1])

    small_w = _pack([wts[k] for k in SMALL_SHARDED])
    bf = {k: shard2d(wts, k).astype(BF16) for k in BIG}
    gather_groups = {"mix0": ("w_in", "w_out", "small"), "ffn0": ("w_up0", "w_down0"),
                     "ffn1": ("w_pool", "w_up1", "w_down1")}
    started, token = {}, None
    for grp, keys in gather_groups.items():
        arrs = [small_w if k == "small" else bf[k] for k in keys]
        if token is not None:
            arrs = list(lax.optimization_barrier((tuple(arrs), token))[0])
        started[grp] = _exchange_start(arrs, [False] * len(arrs), name=f"gather_{grp}_start")
        token = started[grp][-1]
    all_started = token

    def weights(grp, after):
        keys = gather_groups[grp]
        after = lax.optimization_barrier((after, all_started))[0]
        gw = dict(zip(keys, _exchange_wait(started[grp], [False] * len(keys), after, name=f"gather_{grp}_wait")))
        if grp == "ffn0":
            return {"w_up": gw["w_up0"], "w_down": gw["w_down0"].reshape(N_PAIR, -1, d)}
        if grp == "ffn1":
            return {"w_up": gw["w_up1"], "w_down": gw["w_down1"].reshape(N_PAIR, -1, d),
                    "w_pool": _from_dev_major(gw["w_pool"].reshape(N_DEV, len(POOL_WINDOWS), -1, POOL_GROUP), 1)}
        per_dev = gw["small"].reshape(N_DEV, -1)
        sizes = [wts[k].size for k in SMALL_SHARDED]
        offs = [sum(sizes[:i]) for i in range(len(sizes))]
        small_full = {k: per_dev[:, o:o + n_].reshape((N_DEV,) + wts[k].shape)
                      for k, o, n_ in zip(SMALL_SHARDED, offs, sizes)}
        return {
            "norm_mix_even": norm_mix_even, "ln_a_g": ln_a_g, "ln_a_b": ln_a_b, "norm_ffn": norm_ffn,
            "norm_final": norm_final[None],
            "w_in": _from_dev_major(gw["w_in"], 1),
            "w_out": gw["w_out"].reshape(-1, d),
            "conv_a": _from_dev_major(small_full["conv_a"][:, 0], 1),
            "conv_b": _from_dev_major(small_full["conv_b"][:, 0], 1),
            "norm_mix_odd": _from_dev_major(small_full["norm_mix_odd"], 1),
            "pool_scale": _from_dev_major(small_full["pool_scale"], 1),
            "conv_ffn": [small_full["conv_ffn_w"][:, l] for l in range(2)],
        }

    def dev_major(k, g):
        if k == "w_in":
            return _to_dev_major(g, 1)
        if k == "w_pool":
            return _to_dev_major(g, 1).reshape(N_DEV, -1, POOL_GROUP)
        return g.reshape(N_DEV, -1, g.shape[-1])

    sent, big_grads = {}, {}

    def grads_ready(grp, grads, act):
        big_grads.update(grads)
        if grp == "mix0":
            return act
        keys = tuple(grads)
        sent[grp] = (keys, _exchange_start([dev_major(k, grads[k]) for k in keys], [True] * len(keys),
                                           name=f"grads_{grp}_start"))
        return lax.optimization_barrier((act, sent[grp][1][-1]))[0]

    loss, dx, g = _local_step(x[0], loss_target[0], weights, grads_ready)

    small_parts = _pack([
        _to_dev_major(g["conv_a"], 1), _to_dev_major(g["conv_b"], 1), _to_dev_major(g["norm_mix_odd"], 1),
        _to_dev_major(g["pool_scale"], 1), jnp.stack(g["conv_ffn"], axis=1)], lead=(N_DEV,))
    repl_parts = _pack([g[k] for k in SMALL_REPLICATED])
    last_keys = ("w_in", "w_out", "small", "replicated")
    last = _exchange_start([dev_major("w_in", big_grads["w_in"]), dev_major("w_out", big_grads["w_out"]),
                            small_parts, repl_parts], [True, True, True, False], name="grads_mix0_start")
    sent["mix0"] = (last_keys, last)

    landed, after = {}, last[-1]
    for grp in ("ffn1", "ffn0", "mix0"):
        keys, st = sent[grp]
        scat = [k != "replicated" for k in keys]
        landed.update(zip(keys, _exchange_wait(st, scat, after, name=f"grads_{grp}_wait")))
    big_out = {}
    for k in BIG:
        big_out[k] = _adamw(landed[k], shard2d(wts, k), shard2d(mom, k), shard2d(var, k), name=f"adamw_{k}",
                            tr=BIG[k][2])
    out = {}
    for k in ("w_in", "w_out", "w_pool"):
        out[k] = [t.reshape(wts[k].shape) for t in big_out[k]]
    for k in ("w_up", "w_down"):
        out[k] = [jnp.stack([a, b]).reshape(wts[k].shape) for a, b in zip(big_out[k + "0"], big_out[k + "1"])]
    res = _adamw(landed["small"], small_w, _pack([mom[k] for k in SMALL_SHARDED]),
                 _pack([var[k] for k in SMALL_SHARDED]), name="adamw_small", tr=small_w.shape[0])
    unpacked = [_unpack(t, [wts[k].shape for k in SMALL_SHARDED]) for t in res]
    for j, k in enumerate(SMALL_SHARDED):
        out[k] = [u[j] for u in unpacked]
    repl_w = _pack([wts[k] for k in SMALL_REPLICATED])
    res = _adamw(landed["replicated"], repl_w, _pack([mom[k] for k in SMALL_REPLICATED]),
                 _pack([var[k] for k in SMALL_REPLICATED]), name="adamw_replicated", tr=repl_w.shape[0])
    unpacked = [_unpack(t, [wts[k].shape for k in SMALL_REPLICATED]) for t in res]
    for j, k in enumerate(SMALL_REPLICATED):
        out[k] = [u[j] for u in unpacked]

    loss = lax.psum(loss, ("x", "y", "c"))
    return (loss, dx[None], *[out[k][0] for k in names], *[out[k][1] for k in names],
            *[out[k][2] for k in names], *[out[k][3] for k in names])
```

```python
import jax
import jax.numpy as jnp
from jax import lax
from jax.experimental import pallas as pl
from jax.experimental.pallas import tpu as pltpu

F32 = jnp.float32
BF16 = jnp.bfloat16

RMS_EPS = 1e-6
LN_EPS = 1e-5
ADAM_LR = 0.001
ADAM_B1 = 0.9
ADAM_B2 = 0.999
ADAM_EPS = 1e-08
ADAM_WD = 0.01
ADAM_STEP = 10

N_DEV = 8
N_PAIR = N_DEV // 2
A_WIDTH = 512
A_TAPS = 31
POOL_WINDOWS = (2, 4, 8, 16)
POOL_GROUP = 256
HALO_A = 32
HALO_S = 16
VMEM_LIMIT = 56 * 1024 * 1024


def _params(sem, vmem=VMEM_LIMIT):
    return pltpu.CompilerParams(dimension_semantics=sem, vmem_limit_bytes=vmem)


def _sigmoid(x):
    return 1.0 / (1.0 + jnp.exp(-x))


def _prev_blk(i, ts, hb):
    return jnp.maximum(i * (ts // hb) - 1, 0)


def _next_blk(i, ts, hb, s):
    return jnp.minimum((i + 1) * (ts // hb), s // hb - 1)


def _mm(a, b, *, name, ta=False, tb=False, add=None, dep=None, out_dtype=F32, tm=512, tn=512, tk=None):
    m, k = (a.shape[1], a.shape[0]) if ta else a.shape
    n = b.shape[0] if tb else b.shape[1]
    tk = k if tk is None else tk
    tm, tn, tk = min(tm, m), min(tn, n), min(tk, k)
    assert m % tm == 0 and n % tn == 0 and k % tk == 0, (name, m, n, k, tm, tn, tk)
    nk = k // tk
    dims = (((0,) if ta else (1,), (1,) if tb else (0,)), ((), ()))
    n_in = 2 + (add is not None) + (dep is not None)

    def body(*refs):
        a_ref, b_ref = refs[0], refs[1]
        add_ref = refs[2] if add is not None else None
        o_ref = refs[n_in]
        part = lax.dot_general(a_ref[...].astype(BF16), b_ref[...].astype(BF16), dims, preferred_element_type=F32)

        def finish(r):
            if add_ref is not None:
                r = r + add_ref[...]
            o_ref[...] = r.astype(out_dtype)

        if nk == 1:
            finish(part)
            return
        acc_ref = refs[-1]
        kk = pl.program_id(2)

        @pl.when(kk == 0)
        def _():
            acc_ref[...] = part

        @pl.when(kk > 0)
        def _():
            acc_ref[...] += part

        @pl.when(kk == nk - 1)
        def _():
            finish(acc_ref[...])

    a_spec = pl.BlockSpec((tk, tm), lambda i, j, kk: (kk, i)) if ta else pl.BlockSpec((tm, tk), lambda i, j, kk: (i, kk))
    b_spec = pl.BlockSpec((tn, tk), lambda i, j, kk: (j, kk)) if tb else pl.BlockSpec((tk, tn), lambda i, j, kk: (kk, j))
    in_specs = [a_spec, b_spec]
    args = [a, b]
    if add is not None:
        in_specs.append(pl.BlockSpec((tm, tn), lambda i, j, kk: (i, j)))
        args.append(add)
    if dep is not None:
        in_specs.append(pl.BlockSpec(memory_space=pl.ANY))
        args.append(dep)
    return pl.pallas_call(
        body, name=name, grid=(m // tm, n // tn, nk),
        in_specs=in_specs, out_specs=pl.BlockSpec((tm, tn), lambda i, j, kk: (i, j)),
        out_shape=jax.ShapeDtypeStruct((m, n), out_dtype),
        scratch_shapes=[pltpu.VMEM((tm, tn), F32)] if nk > 1 else [],
        compiler_params=_params(("parallel", "parallel", "arbitrary")),
    )(*args)


def _rms_rows(xv, gv):
    return xv * lax.rsqrt(jnp.mean(xv * xv, axis=-1, keepdims=True) + RMS_EPS) * gv


def _rms_mm(x, g, w, *, name, out_dtype, tm=1024, tn=512):
    s, d = x.shape
    n = w.shape[1]
    tm = min(tm, s)
    assert s % tm == 0 and n % tn == 0

    def body(x_ref, g_ref, w_ref, h_ref, z_ref, hs_ref):
        @pl.when(pl.program_id(1) == 0)
        def _():
            hv = _rms_rows(x_ref[...], g_ref[...]).astype(BF16)
            hs_ref[...] = hv
            h_ref[...] = hv

        z_ref[...] = jnp.dot(hs_ref[...], w_ref[...], preferred_element_type=F32).astype(out_dtype)

    return pl.pallas_call(
        body, name=name, grid=(s // tm, n // tn),
        in_specs=[pl.BlockSpec((tm, d), lambda i, j: (i, 0)),
                  pl.BlockSpec((1, d), lambda i, j: (0, 0)),
                  pl.BlockSpec((d, tn), lambda i, j: (0, j))],
        out_specs=[pl.BlockSpec((tm, d), lambda i, j: (i, 0)),
                   pl.BlockSpec((tm, tn), lambda i, j: (i, j))],
        out_shape=[jax.ShapeDtypeStruct((s, d), BF16), jax.ShapeDtypeStruct((s, n), out_dtype)],
        scratch_shapes=[pltpu.VMEM((tm, d), BF16)],
        compiler_params=_params(("parallel", "arbitrary")),
    )(x, g, w)


def _rms_bwd_rows(xv, gv, dh):
    r = lax.rsqrt(jnp.mean(xv * xv, axis=-1, keepdims=True) + RMS_EPS)
    xh = xv * r
    dn = dh * gv
    dx = r * (dn - xh * jnp.mean(dn * xh, axis=-1, keepdims=True))
    return dx, dh * xh


def _rms_bwd(x, g, dh, dres, *, name, ts=256):
    s, d = x.shape
    ts = min(ts, s)

    def body(x_ref, g_ref, dh_ref, dres_ref, dx_ref, dxb_ref, dg_ref):
        @pl.when(pl.program_id(0) == 0)
        def _():
            dg_ref[...] = jnp.zeros_like(dg_ref)

        dx, dgr = _rms_bwd_rows(x_ref[...], g_ref[...], dh_ref[...])
        dx = dres_ref[...] + dx
        dx_ref[...] = dx
        dxb_ref[...] = dx.astype(BF16)
        dg_ref[...] += jnp.sum(dgr, axis=0, keepdims=True)

    row = pl.BlockSpec((ts, d), lambda i: (i, 0))
    vec = pl.BlockSpec((1, d), lambda i: (0, 0))
    return pl.pallas_call(
        body, name=name, grid=(s // ts,),
        in_specs=[row, vec, row, row], out_specs=[row, row, vec],
        out_shape=[jax.ShapeDtypeStruct((s, d), F32), jax.ShapeDtypeStruct((s, d), BF16),
                   jax.ShapeDtypeStruct((1, d), F32)],
        compiler_params=_params(("arbitrary",)),
    )(x, g, dh, dres)


def _final_loss(x, g, target, *, name, ts=256):
    s, d = x.shape
    ts = min(ts, s)

    def body(x_ref, g_ref, t_ref, loss_ref, dx_ref, dxb_ref, dg_ref):
        @pl.when(pl.program_id(0) == 0)
        def _():
            loss_ref[...] = jnp.zeros_like(loss_ref)
            dg_ref[...] = jnp.zeros_like(dg_ref)

        xv = x_ref[...]
        gv = g_ref[...]
        r = lax.rsqrt(jnp.mean(xv * xv, axis=-1, keepdims=True) + RMS_EPS)
        xh = xv * r
        err = xh * gv - t_ref[...]
        loss_ref[...] += 0.5 * jnp.sum(jnp.mean(err * err, axis=-1, keepdims=True), axis=0, keepdims=True)
        dy = err * (1.0 / d)
        dn = dy * gv
        dx = r * (dn - xh * jnp.mean(dn * xh, axis=-1, keepdims=True))
        dx_ref[...] = dx
        dxb_ref[...] = dx.astype(BF16)
        dg_ref[...] += jnp.sum(dy * xh, axis=0, keepdims=True)

    row = pl.BlockSpec((ts, d), lambda i: (i, 0))
    vec = pl.BlockSpec((1, d), lambda i: (0, 0))
    return pl.pallas_call(
        body, name=name, grid=(s // ts,),
        in_specs=[row, vec, row],
        out_specs=[pl.BlockSpec((1, 1), lambda i: (0, 0)), row, row, vec],
        out_shape=[jax.ShapeDtypeStruct((1, 1), F32), jax.ShapeDtypeStruct((s, d), F32),
                   jax.ShapeDtypeStruct((s, d), BF16), jax.ShapeDtypeStruct((1, d), F32)],
        compiler_params=_params(("arbitrary",)),
    )(x, g, target)


def _conv_taps(ext_ref, w_ref, n_taps, base, r0, rows, reverse=False):
    acc = None
    for k in range(n_taps):
        off = r0 + (base - k if reverse else base + k)
        term = w_ref[k:k + 1, :] * ext_ref[pl.ds(off, rows), :]
        acc = term if acc is None else acc + term
    return acc


def _mix0_fwd(z, conv_a, ln_g, ln_b, conv_b, *, name, ts=256, rc=32):
    s = z.shape[0]
    c = A_WIDTH
    hb = HALO_A

    def body(z_ref, zp_ref, wa_ref, lg_ref, lb_ref, wb_ref, ab_ref, ca_ref, exta, extb):
        keep = jnp.where(pl.program_id(0) > 0, 1.0, 0.0)
        zp = zp_ref[...]
        exta[0:hb, :] = zp[:, 0:c] * _sigmoid(zp[:, c:2 * c]) * keep
        extb[0:hb, :] = zp[:, 3 * c:4 * c] * zp[:, 4 * c:5 * c] * keep
        exta[hb:hb + ts, :] = z_ref[:, 0:c] * _sigmoid(z_ref[:, c:2 * c])
        extb[hb:hb + ts, :] = z_ref[:, 3 * c:4 * c] * z_ref[:, 4 * c:5 * c]
        lg = lg_ref[...]
        lb = lb_ref[...]
        for q in range(ts // rc):
            r0 = q * rc
            ca = _conv_taps(exta, wa_ref, A_TAPS, hb - (A_TAPS - 1), r0, rc)
            ca_ref[r0:r0 + rc, :] = ca
            mu = jnp.mean(ca, axis=-1, keepdims=True)
            xc = ca - mu
            rs = lax.rsqrt(jnp.mean(xc * xc, axis=-1, keepdims=True) + LN_EPS)
            l = xc * rs * lg + lb
            ab_ref[r0:r0 + rc, 0:c] = (l * _sigmoid(l)).astype(BF16)
            cbc = _conv_taps(extb, wb_ref, 3, hb - 2, r0, rc)
            ab_ref[r0:r0 + rc, c:2 * c] = (z_ref[r0:r0 + rc, 2 * c:3 * c] * cbc).astype(BF16)

    return pl.pallas_call(
        body, name=name, grid=(s // ts,),
        in_specs=[pl.BlockSpec((ts, 5 * c), lambda i: (i, 0)),
                  pl.BlockSpec((hb, 5 * c), lambda i: (_prev_blk(i, ts, hb), 0)),
                  pl.BlockSpec((32, c), lambda i: (0, 0)),
                  pl.BlockSpec((1, c), lambda i: (0, 0)),
                  pl.BlockSpec((1, c), lambda i: (0, 0)),
                  pl.BlockSpec((8, c), lambda i: (0, 0))],
        out_specs=[pl.BlockSpec((ts, 2 * c), lambda i: (i, 0)),
                   pl.BlockSpec((ts, c), lambda i: (i, 0))],
        out_shape=[jax.ShapeDtypeStruct((s, 2 * c), BF16), jax.ShapeDtypeStruct((s, c), F32)],
        scratch_shapes=[pltpu.VMEM((hb + ts, c), F32), pltpu.VMEM((hb + ts, c), F32)],
        compiler_params=_params(("parallel",)),
    )(z, z, conv_a, ln_g, ln_b, conv_b)


def _mix0_bwd(z, ca, dab, conv_a, ln_g, ln_b, conv_b, *, name, ts=256, rc=32):
    s = z.shape[0]
    c = A_WIDTH
    hb = HALO_A
    ta = A_TAPS

    def body(z_ref, zp_ref, zn_ref, ca_ref, can_ref, d_ref, dn_ref, wa_ref, lg_ref, lb_ref, wb_ref,
             dz_ref, dwa_ref, dwb_ref, dlg_ref, dlb_ref, exta, extb, extdca, extdcb):
        i = pl.program_id(0)
        keep_p = jnp.where(i > 0, 1.0, 0.0)
        keep_n = jnp.where(i < s // ts - 1, 1.0, 0.0)

        @pl.when(i == 0)
        def _():
            dwa_ref[...] = jnp.zeros_like(dwa_ref)
            dwb_ref[...] = jnp.zeros_like(dwb_ref)
            dlg_ref[...] = jnp.zeros_like(dlg_ref)
            dlb_ref[...] = jnp.zeros_like(dlb_ref)

        lg = lg_ref[...]
        lb = lb_ref[...]
        zp = zp_ref[...]
        exta[0:hb, :] = zp[:, 0:c] * _sigmoid(zp[:, c:2 * c]) * keep_p
        extb[0:hb, :] = zp[:, 3 * c:4 * c] * zp[:, 4 * c:5 * c] * keep_p
        exta[hb:hb + ts, :] = z_ref[:, 0:c] * _sigmoid(z_ref[:, c:2 * c])
        extb[hb:hb + ts, :] = z_ref[:, 3 * c:4 * c] * z_ref[:, 4 * c:5 * c]

        def ln_bwd(cav, dav):
            mu = jnp.mean(cav, axis=-1, keepdims=True)
            xc = cav - mu
            rs = lax.rsqrt(jnp.mean(xc * xc, axis=-1, keepdims=True) + LN_EPS)
            nv = xc * rs
            l = nv * lg + lb
            sg = _sigmoid(l)
            dl = dav * (sg * (1.0 + l * (1.0 - sg)))
            dnv = dl * lg
            dca = rs * (dnv - jnp.mean(dnv, axis=-1, keepdims=True)
                        - nv * jnp.mean(dnv * nv, axis=-1, keepdims=True))
            return dca, dl, nv

        dlg_acc = jnp.zeros((1, c), F32)
        dlb_acc = jnp.zeros((1, c), F32)
        for q in range(ts // rc):
            r0 = q * rc
            dca, dl, nv = ln_bwd(ca_ref[r0:r0 + rc, :], d_ref[r0:r0 + rc, 0:c])
            extdca[r0:r0 + rc, :] = dca
            dlg_acc = dlg_acc + jnp.sum(dl * nv, axis=0, keepdims=True)
            dlb_acc = dlb_acc + jnp.sum(dl, axis=0, keepdims=True)
            extdcb[r0:r0 + rc, :] = d_ref[r0:r0 + rc, c:2 * c] * z_ref[r0:r0 + rc, 2 * c:3 * c]
        dca_n, _, _ = ln_bwd(can_ref[...], dn_ref[:, 0:c])
        extdca[ts:ts + hb, :] = dca_n * keep_n
        extdcb[ts:ts + hb, :] = dn_ref[:, c:2 * c] * zn_ref[:, 2 * c:3 * c] * keep_n
        dlg_ref[...] += dlg_acc
        dlb_ref[...] += dlb_acc

        for q in range(ts // rc):
            r0 = q * rc
            zr = z_ref[r0:r0 + rc, :]
            dga = _conv_taps(extdca, wa_ref, ta, ta - 1, r0, rc, reverse=True)
            sg = _sigmoid(zr[:, c:2 * c])
            dz_ref[r0:r0 + rc, 0:c] = (dga * sg).astype(BF16)
            dz_ref[r0:r0 + rc, c:2 * c] = (dga * zr[:, 0:c] * sg * (1.0 - sg)).astype(BF16)
            cbc = _conv_taps(extb, wb_ref, 3, hb - 2, r0, rc)
            dz_ref[r0:r0 + rc, 2 * c:3 * c] = (d_ref[r0:r0 + rc, c:2 * c] * cbc).astype(BF16)
            dcb = _conv_taps(extdcb, wb_ref, 3, 2, r0, rc, reverse=True)
            dz_ref[r0:r0 + rc, 3 * c:4 * c] = (dcb * zr[:, 4 * c:5 * c]).astype(BF16)
            dz_ref[r0:r0 + rc, 4 * c:5 * c] = (dcb * zr[:, 3 * c:4 * c]).astype(BF16)

        dca_t = extdca[0:ts, :]
        for k in range(ta):
            dwa_ref[k:k + 1, :] += jnp.sum(dca_t * exta[pl.ds(hb - (ta - 1) + k, ts), :], axis=0, keepdims=True)
        dcb_t = extdcb[0:ts, :]
        for k in range(3):
            dwb_ref[k:k + 1, :] += jnp.sum(dcb_t * extb[pl.ds(hb - 2 + k, ts), :], axis=0, keepdims=True)

    def tile(w):
        return pl.BlockSpec((ts, w), lambda i: (i, 0))

    def prev(w):
        return pl.BlockSpec((hb, w), lambda i: (_prev_blk(i, ts, hb), 0))

    def nxt(w):
        return pl.BlockSpec((hb, w), lambda i: (_next_blk(i, ts, hb, s), 0))

    def const(r, w):
        return pl.BlockSpec((r, w), lambda i: (0, 0))

    return pl.pallas_call(
        body, name=name, grid=(s // ts,),
        in_specs=[tile(5 * c), prev(5 * c), nxt(5 * c), tile(c), nxt(c), tile(2 * c), nxt(2 * c),
                  const(32, c), const(1, c), const(1, c), const(8, c)],
        out_specs=[tile(5 * c), const(32, c), const(8, c), const(1, c), const(1, c)],
        out_shape=[jax.ShapeDtypeStruct((s, 5 * c), BF16), jax.ShapeDtypeStruct((32, c), F32),
                   jax.ShapeDtypeStruct((8, c), F32), jax.ShapeDtypeStruct((1, c), F32),
                   jax.ShapeDtypeStruct((1, c), F32)],
        scratch_shapes=[pltpu.VMEM((hb + ts, c), F32), pltpu.VMEM((hb + ts, c), F32),
                        pltpu.VMEM((ts + hb, c), F32), pltpu.VMEM((ts + hb, c), F32)],
        compiler_params=_params(("arbitrary",)),
    )(z, z, z, ca, ca, dab, dab, conv_a, ln_g, ln_b, conv_b)


def _ffn_up(x, g, w8, *, name, tm=1024):
    s, d = x.shape
    nb, _, c = w8.shape
    tm = min(tm, s)

    def body(x_ref, g_ref, w_ref, h_ref, u_ref, hs_ref):
        @pl.when(pl.program_id(1) == 0)
        def _():
            hv = _rms_rows(x_ref[...], g_ref[...]).astype(BF16)
            hs_ref[...] = hv
            h_ref[...] = hv

        u_ref[...] = jnp.dot(hs_ref[...], w_ref[...], preferred_element_type=F32).astype(BF16)

    return pl.pallas_call(
        body, name=name, grid=(s // tm, nb),
        in_specs=[pl.BlockSpec((tm, d), lambda i, k: (i, 0)),
                  pl.BlockSpec((1, d), lambda i, k: (0, 0)),
                  pl.BlockSpec((None, d, c), lambda i, k: (k, 0, 0))],
        out_specs=[pl.BlockSpec((tm, d), lambda i, k: (i, 0)),
                   pl.BlockSpec((None, tm, c), lambda i, k: (k, i, 0))],
        out_shape=[jax.ShapeDtypeStruct((s, d), BF16), jax.ShapeDtypeStruct((nb, s, c), BF16)],
        scratch_shapes=[pltpu.VMEM((tm, d), BF16)],
        compiler_params=_params(("parallel", "arbitrary")),
    )(x, g, w8)


def _ffn_mid(up8, wc8, *, name, ts=512, rc=32):
    nb, s, c = up8.shape
    hb = HALO_S
    ts = min(ts, s)

    def body(g_ref, gp_ref, v_ref, vp_ref, wg_ref, wv_ref, act_ref, extg, extv):
        keep = jnp.where(pl.program_id(0) > 0, 1.0, 0.0)
        extg[0:hb, :] = gp_ref[...].astype(F32) * keep
        extv[0:hb, :] = vp_ref[...].astype(F32) * keep
        extg[hb:hb + ts, :] = g_ref[...].astype(F32)
        extv[hb:hb + ts, :] = v_ref[...].astype(F32)
        for q in range(ts // rc):
            r0 = q * rc
            gg = _conv_taps(extg, wg_ref, 3, hb - 2, r0, rc)
            vv = _conv_taps(extv, wv_ref, 3, hb - 2, r0, rc)
            act_ref[r0:r0 + rc, :] = (gg * _sigmoid(gg) * vv).astype(BF16)

    def blk(rows, off, halo):
        if halo:
            return pl.BlockSpec((None, rows, c), lambda i, j: (j + off, _prev_blk(i, ts, hb), 0))
        return pl.BlockSpec((None, rows, c), lambda i, j: (j + off, i, 0))

    def taps(off):
        return pl.BlockSpec((None, 8, c), lambda i, j: (j + off, 0, 0))

    return pl.pallas_call(
        body, name=name, grid=(s // ts, N_PAIR),
        in_specs=[blk(ts, 0, False), blk(hb, 0, True), blk(ts, N_PAIR, False), blk(hb, N_PAIR, True),
                  taps(0), taps(N_PAIR)],
        out_specs=pl.BlockSpec((None, ts, c), lambda i, j: (j, i, 0)),
        out_shape=jax.ShapeDtypeStruct((N_PAIR, s, c), BF16),
        scratch_shapes=[pltpu.VMEM((hb + ts, c), F32), pltpu.VMEM((hb + ts, c), F32)],
        compiler_params=_params(("parallel", "parallel")),
    )(up8, up8, up8, up8, wc8, wc8)


def _ffn_down(act4, wd4, x, *, name, tm=512):
    npair, s, c = act4.shape
    d = x.shape[1]
    tm = min(tm, s)

    def body(a_ref, w_ref, x_ref, o_ref):
        acc = x_ref[...]
        for j in range(npair):
            acc = acc + jnp.dot(a_ref[j], w_ref[j], preferred_element_type=F32)
        o_ref[...] = acc

    return pl.pallas_call(
        body, name=name, grid=(s // tm,),
        in_specs=[pl.BlockSpec((npair, tm, c), lambda i: (0, i, 0)),
                  pl.BlockSpec((npair, c, d), lambda i: (0, 0, 0)),
                  pl.BlockSpec((tm, d), lambda i: (i, 0))],
        out_specs=pl.BlockSpec((tm, d), lambda i: (i, 0)),
        out_shape=jax.ShapeDtypeStruct((s, d), F32),
        compiler_params=_params(("parallel",)),
    )(act4, wd4, x)


def _ffn_dact(db, wd4, *, name, tm=1024):
    s, d = db.shape
    npair, c, _ = wd4.shape
    tm = min(tm, s)

    def body(d_ref, w_ref, o_ref):
        o_ref[...] = lax.dot_general(d_ref[...], w_ref[...], (((1,), (1,)), ((), ())),
                                     preferred_element_type=F32).astype(BF16)

    return pl.pallas_call(
        body, name=name, grid=(s // tm, npair),
        in_specs=[pl.BlockSpec((tm, d), lambda i, j: (i, 0)),
                  pl.BlockSpec((None, c, d), lambda i, j: (j, 0, 0))],
        out_specs=pl.BlockSpec((None, tm, c), lambda i, j: (j, i, 0)),
        out_shape=jax.ShapeDtypeStruct((npair, s, c), BF16),
        compiler_params=_params(("parallel", "parallel")),
    )(db, wd4)


def _ffn_dwdown(act4, db, *, name):
    npair, s, c = act4.shape
    d = db.shape[1]

    def body(a_ref, d_ref, o_ref):
        o_ref[...] = lax.dot_general(a_ref[...], d_ref[...], (((0,), (0,)), ((), ())),
                                     preferred_element_type=F32).astype(BF16)

    return pl.pallas_call(
        body, name=name, grid=(npair,),
        in_specs=[pl.BlockSpec((None, s, c), lambda j: (j, 0, 0)),
                  pl.BlockSpec((s, d), lambda j: (0, 0))],
        out_specs=pl.BlockSpec((None, c, d), lambda j: (j, 0, 0)),
        out_shape=jax.ShapeDtypeStruct((npair, c, d), BF16),
        compiler_params=_params(("parallel",)),
    )(act4, db)


def _ffn_midbwd(up8, dact4, wc8, *, name, ts=512, rc=32):
    nb, s, c = up8.shape
    hb = HALO_S
    ts = min(ts, s)
    n_i = s // ts

    def body(g_ref, gp_ref, gn_ref, v_ref, vp_ref, vn_ref, d_ref, dn_ref, wg_ref, wv_ref,
             dg_ref, dv_ref, dwg_ref, dwv_ref, extg, extv, extdg, extdv):
        i = pl.program_id(1)
        keep_p = jnp.where(i > 0, 1.0, 0.0)
        keep_n = jnp.where(i < n_i - 1, 1.0, 0.0)

        @pl.when(i == 0)
        def _():
            dwg_ref[...] = jnp.zeros_like(dwg_ref)
            dwv_ref[...] = jnp.zeros_like(dwv_ref)

        for ext, cur, prv, nxt in ((extg, g_ref, gp_ref, gn_ref), (extv, v_ref, vp_ref, vn_ref)):
            ext[0:hb, :] = prv[...].astype(F32) * keep_p
            ext[hb:hb + ts, :] = cur[...].astype(F32)
            ext[hb + ts:hb + ts + hb, :] = nxt[...].astype(F32)

        def du_rows(r0, rows, dav):
            gg = _conv_taps(extg, wg_ref, 3, hb - 2, r0, rows)
            vv = _conv_taps(extv, wv_ref, 3, hb - 2, r0, rows)
            sg = _sigmoid(gg)
            extdg[r0:r0 + rows, :] = dav * vv * (sg * (1.0 + gg * (1.0 - sg)))
            extdv[r0:r0 + rows, :] = dav * (gg * sg)

        for q in range(ts // rc):
            du_rows(q * rc, rc, d_ref[q * rc:q * rc + rc, :].astype(F32))
        du_rows(ts, 8, dn_ref[0:8, :].astype(F32) * keep_n)

        for q in range(ts // rc):
            r0 = q * rc
            dg_ref[r0:r0 + rc, :] = _conv_taps(extdg, wg_ref, 3, 2, r0, rc, reverse=True).astype(BF16)
            dv_ref[r0:r0 + rc, :] = _conv_taps(extdv, wv_ref, 3, 2, r0, rc, reverse=True).astype(BF16)
        for ext, extd, dw in ((extg, extdg, dwg_ref), (extv, extdv, dwv_ref)):
            du_t = extd[0:ts, :]
            for k in range(3):
                dw[k:k + 1, :] += jnp.sum(du_t * ext[pl.ds(hb - 2 + k, ts), :], axis=0, keepdims=True)

    def blk(off, which):
        if which == "prev":
            return pl.BlockSpec((None, hb, c), lambda j, i: (j + off, _prev_blk(i, ts, hb), 0))
        if which == "next":
            return pl.BlockSpec((None, hb, c), lambda j, i: (j + off, _next_blk(i, ts, hb, s), 0))
        return pl.BlockSpec((None, ts, c), lambda j, i: (j + off, i, 0))

    def taps(off):
        return pl.BlockSpec((None, 8, c), lambda j, i: (j + off, 0, 0))

    tile = pl.BlockSpec((None, ts, c), lambda j, i: (j, i, 0))
    acc = pl.BlockSpec((None, 8, c), lambda j, i: (j, 0, 0))
    return pl.pallas_call(
        body, name=name, grid=(N_PAIR, n_i),
        in_specs=[blk(0, "tile"), blk(0, "prev"), blk(0, "next"),
                  blk(N_PAIR, "tile"), blk(N_PAIR, "prev"), blk(N_PAIR, "next"),
                  blk(0, "tile"), blk(0, "next"), taps(0), taps(N_PAIR)],
        out_specs=[tile, tile, acc, acc],
        out_shape=[jax.ShapeDtypeStruct((N_PAIR, s, c), BF16), jax.ShapeDtypeStruct((N_PAIR, s, c), BF16),
                   jax.ShapeDtypeStruct((N_PAIR, 8, c), F32), jax.ShapeDtypeStruct((N_PAIR, 8, c), F32)],
        scratch_shapes=[pltpu.VMEM((hb + ts + hb, c), F32), pltpu.VMEM((hb + ts + hb, c), F32),
                        pltpu.VMEM((ts + 8, c), F32), pltpu.VMEM((ts + 8, c), F32)],
        compiler_params=_params(("parallel", "arbitrary")),
    )(up8, up8, up8, up8, up8, up8, dact4, dact4, wc8, wc8)


def _ffn_dh(dupg, dupv, w8, x, g, dres, *, name, tm=256):
    npair, s, c = dupg.shape
    d = x.shape[1]
    tm = min(tm, s)

    def body(dg_ref, dv_ref, w_ref, x_ref, g_ref, dres_ref, dx_ref, dxb_ref, dgain_ref):
        @pl.when(pl.program_id(0) == 0)
        def _():
            dgain_ref[...] = jnp.zeros_like(dgain_ref)

        nt = (((1,), (1,)), ((), ()))
        dh = None
        for j in range(npair):
            for src, k in ((dg_ref, j), (dv_ref, j + npair)):
                part = lax.dot_general(src[j], w_ref[k], nt, preferred_element_type=F32)
                dh = part if dh is None else dh + part
        dx, dgr = _rms_bwd_rows(x_ref[...], g_ref[...], dh)
        dx = dres_ref[...] + dx
        dx_ref[...] = dx
        dxb_ref[...] = dx.astype(BF16)
        dgain_ref[...] += jnp.sum(dgr, axis=0, keepdims=True)

    row = pl.BlockSpec((tm, d), lambda i: (i, 0))
    vec = pl.BlockSpec((1, d), lambda i: (0, 0))
    dup = pl.BlockSpec((npair, tm, c), lambda i: (0, i, 0))
    return pl.pallas_call(
        body, name=name, grid=(s // tm,),
        in_specs=[dup, dup, pl.BlockSpec((2 * npair, d, c), lambda i: (0, 0, 0)), row, vec, row],
        out_specs=[row, row, vec],
        out_shape=[jax.ShapeDtypeStruct((s, d), F32), jax.ShapeDtypeStruct((s, d), BF16),
                   jax.ShapeDtypeStruct((1, d), F32)],
        compiler_params=_params(("arbitrary",)),
    )(dupg, dupv, w8, x, g, dres)


def _ffn_dwup(h, dupg, dupv, *, name, tm=512):
    npair, s, c = dupg.shape
    d = h.shape[1]

    def body(h_ref, dg_ref, dv_ref, o_ref):
        tn = (((0,), (0,)), ((), ()))
        k = pl.program_id(1)

        @pl.when(k < npair)
        def _():
            o_ref[...] = lax.dot_general(h_ref[...], dg_ref[...], tn, preferred_element_type=F32).astype(BF16)

        @pl.when(k >= npair)
        def _():
            o_ref[...] = lax.dot_general(h_ref[...], dv_ref[...], tn, preferred_element_type=F32).astype(BF16)

    return pl.pallas_call(
        body, name=name, grid=(d // tm, 2 * npair),
        in_specs=[pl.BlockSpec((s, tm), lambda m, k: (0, m)),
                  pl.BlockSpec((None, s, c), lambda m, k: (jnp.minimum(k, npair - 1), 0, 0)),
                  pl.BlockSpec((None, s, c), lambda m, k: (jnp.maximum(k - npair, 0), 0, 0))],
        out_specs=pl.BlockSpec((None, tm, c), lambda m, k: (k, m, 0)),
        out_shape=jax.ShapeDtypeStruct((2 * npair, d, c), BF16),
        compiler_params=_params(("parallel", "arbitrary")),
    )(h, dupg, dupv)


def _pool_counts(i, ts, rows, window):
    t = lax.broadcasted_iota(jnp.int32, (rows, 1), 0) + i * ts + 1
    return jnp.minimum(t, window).astype(F32)


def _pool_fwd(x, g, *, name, ts=256):
    s, d = x.shape
    hb = HALO_S
    pg = POOL_GROUP
    ts = min(ts, s)

    def body(x_ref, xp_ref, g_ref, p_ref, ext):
        i = pl.program_id(0)
        keep = jnp.where(i > 0, 1.0, 0.0)
        gv = g_ref[...]
        ext[0:hb, :] = _rms_rows(xp_ref[...], gv) * keep
        ext[hb:hb + ts, :] = _rms_rows(x_ref[...], gv)
        for gi, w in enumerate(POOL_WINDOWS):
            cols = slice(gi * pg, (gi + 1) * pg)
            h = ext[hb:hb + ts, cols]
            acc = h
            for q in range(1, w):
                acc = acc + ext[pl.ds(hb - q, ts), cols]
            p_ref[:, cols] = (acc / _pool_counts(i, ts, ts, w) - h).astype(BF16)

    return pl.pallas_call(
        body, name=name, grid=(s // ts,),
        in_specs=[pl.BlockSpec((ts, d), lambda i: (i, 0)),
                  pl.BlockSpec((hb, d), lambda i: (_prev_blk(i, ts, hb), 0)),
                  pl.BlockSpec((1, d), lambda i: (0, 0))],
        out_specs=pl.BlockSpec((ts, d), lambda i: (i, 0)),
        out_shape=jax.ShapeDtypeStruct((s, d), BF16),
        scratch_shapes=[pltpu.VMEM((hb + ts, d), F32)],
        compiler_params=_params(("parallel",)),
    )(x, x, g)


def _pool_mm_fwd(p, w_pool, scale, x, *, name, ts=1024):
    s, d = x.shape
    pg = POOL_GROUP
    ts = min(ts, s)

    def body(p_ref, w_ref, s_ref, x_ref, o_ref, yu_ref):
        yu = jnp.dot(p_ref[...], w_ref[...], preferred_element_type=F32)
        yu_ref[...] = yu.astype(BF16)
        o_ref[...] = x_ref[...] + yu * s_ref[...]

    blk = pl.BlockSpec((ts, pg), lambda i, gi: (i, gi))
    return pl.pallas_call(
        body, name=name, grid=(s // ts, d // pg),
        in_specs=[blk, pl.BlockSpec((None, pg, pg), lambda i, gi: (gi, 0, 0)),
                  pl.BlockSpec((1, pg), lambda i, gi: (0, gi)), blk],
        out_specs=[blk, blk],
        out_shape=[jax.ShapeDtypeStruct((s, d), F32), jax.ShapeDtypeStruct((s, d), BF16)],
        compiler_params=_params(("parallel", "parallel")),
    )(p, w_pool, scale, x)


def _pool_mm_bwd(dres, w_pool, scale, yu, dep, *, name, ts=1024):
    s, d = dres.shape
    pg = POOL_GROUP
    ts = min(ts, s)

    def body(d_ref, w_ref, s_ref, yu_ref, dep_ref, dyc_ref, dp_ref, ds_ref):
        @pl.when(pl.program_id(1) == 0)
        def _():
            ds_ref[...] = jnp.zeros_like(ds_ref)

        dv = d_ref[...]
        dyc = (dv * s_ref[...]).astype(BF16)
        dyc_ref[...] = dyc
        dp_ref[...] = lax.dot_general(dyc, w_ref[...], (((1,), (1,)), ((), ())), preferred_element_type=F32)
        ds_ref[...] += jnp.sum(dv * yu_ref[...].astype(F32), axis=0, keepdims=True)

    blk = pl.BlockSpec((ts, pg), lambda gi, i: (i, gi))
    vec = pl.BlockSpec((1, pg), lambda gi, i: (0, gi))
    return pl.pallas_call(
        body, name=name, grid=(d // pg, s // ts),
        in_specs=[blk, pl.BlockSpec((None, pg, pg), lambda gi, i: (gi, 0, 0)), vec, blk,
                  pl.BlockSpec(memory_space=pl.ANY)],
        out_specs=[blk, blk, vec],
        out_shape=[jax.ShapeDtypeStruct((s, d), BF16), jax.ShapeDtypeStruct((s, d), F32),
                   jax.ShapeDtypeStruct((1, d), F32)],
        compiler_params=_params(("parallel", "arbitrary")),
    )(dres, w_pool, scale, yu, dep)


def _pool_dw(p, dyc, *, name):
    s, d = p.shape
    pg = POOL_GROUP

    def body(p_ref, d_ref, o_ref):
        o_ref[...] = lax.dot_general(p_ref[...], d_ref[...], (((0,), (0,)), ((), ())),
                                     preferred_element_type=F32).astype(BF16)

    blk = pl.BlockSpec((s, pg), lambda gi: (0, gi))
    return pl.pallas_call(
        body, name=name, grid=(d // pg,),
        in_specs=[blk, blk], out_specs=pl.BlockSpec((None, pg, pg), lambda gi: (gi, 0, 0)),
        out_shape=jax.ShapeDtypeStruct((d // pg, pg, pg), BF16),
        compiler_params=_params(("parallel",)),
    )(p, dyc)


def _pool_bwd(dp, x, g, dres, *, name, ts=256):
    s, d = x.shape
    hb = HALO_S
    pg = POOL_GROUP
    ts = min(ts, s)
    n_i = s // ts

    def body(dp_ref, dpn_ref, x_ref, g_ref, dres_ref, dx_ref, dxb_ref, dg_ref, ext, dh):
        i = pl.program_id(0)
        keep_n = jnp.where(i < n_i - 1, 1.0, 0.0)

        @pl.when(i == 0)
        def _():
            dg_ref[...] = jnp.zeros_like(dg_ref)

        for gi, w in enumerate(POOL_WINDOWS):
            cols = slice(gi * pg, (gi + 1) * pg)
            ext[0:ts, cols] = dp_ref[:, cols] / _pool_counts(i, ts, ts, w)
            ext[ts:ts + hb, cols] = dpn_ref[:, cols] / _pool_counts(i + 1, ts, hb, w) * keep_n
        for gi, w in enumerate(POOL_WINDOWS):
            cols = slice(gi * pg, (gi + 1) * pg)
            acc = ext[0:ts, cols]
            for q in range(1, w):
                acc = acc + ext[pl.ds(q, ts), cols]
            dh[:, cols] = acc - dp_ref[:, cols]
        dx, dgr = _rms_bwd_rows(x_ref[...], g_ref[...], dh[...])
        dx = dres_ref[...] + dx
        dx_ref[...] = dx
        dxb_ref[...] = dx.astype(BF16)
        dg_ref[...] += jnp.sum(dgr, axis=0, keepdims=True)

    row = pl.BlockSpec((ts, d), lambda i: (i, 0))
    vec = pl.BlockSpec((1, d), lambda i: (0, 0))
    return pl.pallas_call(
        body, name=name, grid=(n_i,),
        in_specs=[row, pl.BlockSpec((hb, d), lambda i: (_next_blk(i, ts, hb, s), 0)), row, vec, row],
        out_specs=[row, row, vec],
        out_shape=[jax.ShapeDtypeStruct((s, d), F32), jax.ShapeDtypeStruct((s, d), BF16),
                   jax.ShapeDtypeStruct((1, d), F32)],
        scratch_shapes=[pltpu.VMEM((ts + hb, d), F32), pltpu.VMEM((ts, d), F32)],
        compiler_params=_params(("arbitrary",)),
    )(dp, dp, x, g, dres)


def _pad_rows(w, rows):
    pad = [(0, 0)] * (w.ndim - 2) + [(0, rows - w.shape[-2]), (0, 0)]
    return jnp.pad(w, pad)


def _ffn_layer_fwd(x, nf, w8, wc8, wd4, tag):
    h, up8 = _ffn_up(x, nf, w8, name=f"ffn{tag}_up")
    act4 = _ffn_mid(up8, wc8, name=f"ffn{tag}_mid")
    x_out = _ffn_down(act4, wd4, x, name=f"ffn{tag}_down")
    return x_out, (h, up8, act4)


def _ffn_layer_bwd(d, db, x, nf, w8, wc8, wd4, saved, tag):
    h, up8, act4 = saved
    dact4 = _ffn_dact(db, wd4, name=f"ffn{tag}_dact")
    dwd4 = _ffn_dwdown(act4, db, name=f"ffn{tag}_dwdown")
    dupg, dupv, dwg, dwv = _ffn_midbwd(up8, dact4, wc8, name=f"ffn{tag}_midbwd")
    dx, dxb, dnf = _ffn_dh(dupg, dupv, w8, x, nf, d, name=f"ffn{tag}_dh")
    dw8 = _ffn_dwup(h, dupg, dupv, name=f"ffn{tag}_dwup")
    dwc8 = jnp.concatenate([dwg, dwv], axis=0)[:, :3]
    return dx, dxb, dnf, dw8, dwc8, dwd4


def _local_step(x, target, weights, grads_ready):
    w0 = weights("mix0", x)
    wa = _pad_rows(w0["conv_a"], 32)
    wb = _pad_rows(w0["conv_b"], 8)
    wc = [_pad_rows(w0["conv_ffn"][l], 8) for l in range(2)]
    h0, z = _rms_mm(x, w0["norm_mix_even"], w0["w_in"], name="mix0_in", out_dtype=F32, tn=1280)
    ab, ca = _mix0_fwd(z, wa, w0["ln_a_g"], w0["ln_a_b"], wb, name="mix0_mid")
    x1 = _mm(ab, w0["w_out"], add=x, name="mix0_out", tm=1024, tn=1024)
    w1 = weights("ffn0", x1)
    x2, ffn0 = _ffn_layer_fwd(x1, w0["norm_ffn"][0:1], w1["w_up"], wc[0], w1["w_down"], 0)
    w2 = weights("ffn1", x2)
    p = _pool_fwd(x2, w0["norm_mix_odd"], name="pool_mid")
    x3, yu = _pool_mm_fwd(p, w2["w_pool"], w0["pool_scale"], x2, name="pool_mm")
    x4, ffn1 = _ffn_layer_fwd(x3, w0["norm_ffn"][1:2], w2["w_up"], wc[1], w2["w_down"], 1)
    loss, d4, d4b, g_norm_final = _final_loss(x4, w0["norm_final"], target, name="final_loss")

    d3, d3b, g_nf1, g_up1, g_wc1, g_down1 = _ffn_layer_bwd(
        d4, d4b, x3, w0["norm_ffn"][1:2], w2["w_up"], wc[1], w2["w_down"], ffn1, 1)
    sent1 = grads_ready("ffn1", {"w_up1": g_up1, "w_down1": g_down1})
    dyc, dp, g_scale = _pool_mm_bwd(d3, w2["w_pool"], w0["pool_scale"], yu, sent1, name="pool_mm_bwd")
    g_pool = _pool_dw(p, dyc, name="pool_dw")
    d2, d2b, g_nmo = _pool_bwd(dp, x2, w0["norm_mix_odd"], d3, name="pool_midbwd")
    d1, d1b, g_nf0, g_up0, g_wc0, g_down0 = _ffn_layer_bwd(
        d2, d2b, x1, w0["norm_ffn"][0:1], w1["w_up"], wc[0], w1["w_down"], ffn0, 0)
    sent0 = grads_ready("ffn0", {"w_pool": g_pool, "w_up0": g_up0, "w_down0": g_down0})
    dab = _mm(d1b, w0["w_out"], tb=True, dep=sent0, name="mix0_dab", tm=1024, tn=1024)
    g_out = _mm(ab, d1b, ta=True, out_dtype=BF16, name="mix0_dwout", tm=1024, tn=512)
    dz, g_wa, g_wb, g_lg, g_lb = _mix0_bwd(z, ca, dab, wa, w0["ln_a_g"], w0["ln_a_b"], wb, name="mix0_midbwd")
    dh0 = _mm(dz, w0["w_in"], tb=True, name="mix0_dh", tm=1024, tn=1024)
    g_in = _mm(h0, dz, ta=True, out_dtype=BF16, name="mix0_dwin", tm=1024, tn=512)
    dx, _, g_nme = _rms_bwd(x, w0["norm_mix_even"], dh0, d1, name="mix0_rmsbwd")
    grads_ready("mix0", {"w_in": g_in, "w_out": g_out})

    small = {
        "norm_mix_even": g_nme, "conv_a": g_wa[:A_TAPS], "ln_a_g": g_lg, "ln_a_b": g_lb, "conv_b": g_wb[:3],
        "norm_mix_odd": g_nmo, "pool_scale": g_scale, "norm_ffn": jnp.concatenate([g_nf0, g_nf1], axis=0),
        "conv_ffn": [g_wc0, g_wc1], "norm_final": g_norm_final,
    }
    return loss[0, 0], dx, small


def _my_pos():
    return lax.axis_index("x"), lax.axis_index("y"), lax.axis_index("c")


def _flip(pos, r):
    x, y, c = pos
    return (1 - x if r & 4 else x, 1 - y if r & 2 else y, 1 - c if r & 1 else c)


def _dev_index(pos):
    return 4 * pos[0] + 2 * pos[1] + pos[2]


_HBM = pl.BlockSpec(memory_space=pltpu.HBM)
_SEM = pl.BlockSpec(memory_space=pltpu.SEMAPHORE)
_EFFECT = pltpu.SideEffectType.DATAFLOW_SIDE_EFFECTING


def _exchange_copy(ins, lands, send_sems, recv_sems, scatter, pos, a, r, receiving):
    me = _dev_index(pos)
    peer = _flip(pos, r)
    dest = _dev_index(pos) if receiving else _dev_index(peer)
    src = ins[a].at[dest] if scatter[a] else ins[a]
    slot = _dev_index(peer) if receiving else me
    return pltpu.make_async_remote_copy(
        src_ref=src, dst_ref=lands[a].at[slot], send_sem=send_sems.at[a * (N_DEV - 1) + r - 1],
        recv_sem=recv_sems.at[a * (N_DEV - 1) + r - 1],
        device_id=peer, device_id_type=pl.DeviceIdType.MESH)


def _exchange_start(arrays, scatter, after, *, name):
    n = len(arrays)
    me = _dev_index(_my_pos())
    lands = []
    for arr, sc in zip(arrays, scatter):
        own = lax.dynamic_index_in_dim(arr, me, 0, keepdims=True) if sc else arr[None]
        shape = arr.shape if sc else (N_DEV,) + arr.shape
        lands.append(lax.dynamic_update_slice(lax.empty(shape, arr.dtype), own, (me,) + (0,) * (len(shape) - 1)))

    def body(*refs):
        ins, lnd = refs[:n], refs[n:2 * n]
        send_sems, recv_sems = refs[2 * n + 1], refs[2 * n + 2]
        token = refs[-1]
        pos = _my_pos()
        for a in range(n):
            for r in range(1, N_DEV):
                _exchange_copy(ins, lnd, send_sems, recv_sems, scatter, pos, a, r, receiving=False).start()
        token[...] = jnp.zeros_like(token)

    bufs = [pltpu.with_memory_space_constraint(t, pltpu.HBM) for t in list(arrays) + lands]
    sems = pltpu.SemaphoreType.DMA((n * (N_DEV - 1),))
    res = pl.pallas_call(
        body, name=name,
        out_shape=(sems, sems, *[pltpu.HBM(t.shape, t.dtype) for t in bufs], jax.ShapeDtypeStruct((8, 128), F32)),
        in_specs=[_HBM] * (2 * n) + [pl.BlockSpec(memory_space=pl.ANY)],
        out_specs=(_SEM, _SEM, *[_HBM] * (2 * n), pl.BlockSpec(memory_space=pltpu.VMEM)),
        input_output_aliases={i: 2 + i for i in range(2 * n)},
        compiler_params=pltpu.CompilerParams(has_side_effects=_EFFECT),
    )(*bufs, after)
    return res[0], res[1], list(res[2:2 + n]), list(res[2 + n:2 + 2 * n]), res[-1]


def _exchange_wait(started, scatter, after, *, name):
    send_sems, recv_sems, arrays, lands, _ = started
    n = len(arrays)

    def body(*refs):
        ins, lnd = refs[:n], refs[n:2 * n]
        send_sems, recv_sems = refs[2 * n], refs[2 * n + 1]
        pos = _my_pos()
        for a in range(n):
            for r in range(1, N_DEV):
                _exchange_copy(ins, lnd, send_sems, recv_sems, scatter, pos, a, r, receiving=False).wait_send()
                _exchange_copy(ins, lnd, send_sems, recv_sems, scatter, pos, a, r, receiving=True).wait_recv()

    bufs = list(arrays) + list(lands)
    res = pl.pallas_call(
        body, name=name,
        out_shape=tuple(pltpu.HBM(t.shape, t.dtype) for t in bufs),
        in_specs=[_HBM] * (2 * n) + [_SEM, _SEM, pl.BlockSpec(memory_space=pl.ANY)],
        out_specs=tuple([_HBM] * (2 * n)),
        input_output_aliases={i: i for i in range(2 * n)},
        compiler_params=pltpu.CompilerParams(has_side_effects=_EFFECT),
    )(*bufs, send_sems, recv_sems, after)
    return list(res[n:])


def _adamw(parts, w, m, v, *, name, tr):
    r, c = w.shape
    assert r % tr == 0

    def body(p_ref, w_ref, m_ref, v_ref, g_ref, d_ref, mo_ref, vo_ref):
        g = p_ref[0].astype(F32)
        for k in range(1, N_DEV):
            g = g + p_ref[k].astype(F32)
        mn = ADAM_B1 * m_ref[...] + (1.0 - ADAM_B1) * g
        vn = ADAM_B2 * v_ref[...] + (1.0 - ADAM_B2) * (g * g)
        m_hat = mn / (1.0 - ADAM_B1 ** ADAM_STEP)
        v_hat = vn / (1.0 - ADAM_B2 ** ADAM_STEP)
        g_ref[...] = g
        d_ref[...] = -ADAM_LR * (m_hat / (jnp.sqrt(v_hat) + ADAM_EPS) + ADAM_WD * w_ref[...])
        mo_ref[...] = mn
        vo_ref[...] = vn

    blk = pl.BlockSpec((tr, c), lambda i: (i, 0))
    return pl.pallas_call(
        body, name=name, grid=(r // tr,),
        in_specs=[pl.BlockSpec((N_DEV, tr, c), lambda i: (0, i, 0)), blk, blk, blk],
        out_specs=[blk] * 4, out_shape=[jax.ShapeDtypeStruct((r, c), F32)] * 4,
        compiler_params=_params(("parallel",)),
    )(parts, w, m, v)


def _pack(parts, lead=()):
    flat = jnp.concatenate([p.reshape(lead + (-1,)) for p in parts], axis=-1)
    n = flat.shape[-1]
    rows = -(-n // (8 * 128)) * 8
    flat = jnp.pad(flat, [(0, 0)] * len(lead) + [(0, rows * 128 - n)])
    return flat.reshape(lead + (rows, 128))


def _unpack(slab, shapes):
    flat = slab.reshape(-1)
    out, off = [], 0
    for shp in shapes:
        size = 1
        for dim in shp:
            size *= dim
        out.append(flat[off:off + size].reshape(shp))
        off += size
    return out


def _to_dev_major(g, axis):
    shp = g.shape
    g = g.reshape(shp[:axis] + (N_DEV, shp[axis] // N_DEV) + shp[axis + 1:])
    return jnp.moveaxis(g, axis, 0)


def _from_dev_major(g, axis):
    g = jnp.moveaxis(g, 0, axis)
    shp = g.shape
    return g.reshape(shp[:axis] + (shp[axis] * shp[axis + 1],) + shp[axis + 2:])


SMALL_SHARDED = ("conv_a", "conv_b", "norm_mix_odd", "pool_scale", "conv_ffn_w")
SMALL_REPLICATED = ("norm_mix_even", "ln_a_g", "ln_a_b", "norm_ffn", "norm_final")
BIG = {"w_in": ("w_in", None, 512), "w_out": ("w_out", None, 128), "w_pool": ("w_pool", None, 128),
       "w_up0": ("w_up", 0, 256), "w_up1": ("w_up", 1, 256),
       "w_down0": ("w_down", 0, 352), "w_down1": ("w_down", 1, 352)}


def kernel(x, norm_mix_even, w_in, conv_a, ln_a_g, ln_a_b, conv_b, w_out, norm_mix_odd, w_pool, pool_scale, norm_ffn, w_up, conv_ffn_w, w_down, norm_final, loss_target, m_norm_mix_even, m_w_in, m_conv_a, m_ln_a_g, m_ln_a_b, m_conv_b, m_w_out, m_norm_mix_odd, m_w_pool, m_pool_scale, m_norm_ffn, m_w_up, m_conv_ffn_w, m_w_down, m_norm_final, v_norm_mix_even, v_w_in, v_conv_a, v_ln_a_g, v_ln_a_b, v_conv_b, v_w_out, v_norm_mix_odd, v_w_pool, v_pool_scale, v_norm_ffn, v_w_up, v_conv_ffn_w, v_w_down, v_norm_final):
    names = ("norm_mix_even", "w_in", "conv_a", "ln_a_g", "ln_a_b", "conv_b", "w_out", "norm_mix_odd", "w_pool",
             "pool_scale", "norm_ffn", "w_up", "conv_ffn_w", "w_down", "norm_final")
    wts = dict(zip(names, (norm_mix_even, w_in, conv_a, ln_a_g, ln_a_b, conv_b, w_out, norm_mix_odd, w_pool,
                           pool_scale, norm_ffn, w_up, conv_ffn_w, w_down, norm_final)))
    mom = dict(zip(names, (m_norm_mix_even, m_w_in, m_conv_a, m_ln_a_g, m_ln_a_b, m_conv_b, m_w_out, m_norm_mix_odd,
                           m_w_pool, m_pool_scale, m_norm_ffn, m_w_up, m_conv_ffn_w, m_w_down, m_norm_final)))
    var = dict(zip(names, (v_norm_mix_even, v_w_in, v_conv_a, v_ln_a_g, v_ln_a_b, v_conv_b, v_w_out, v_norm_mix_odd,
                           v_w_pool, v_pool_scale, v_norm_ffn, v_w_up, v_conv_ffn_w, v_w_down, v_norm_final)))
    d = x.shape[-1]
    n_up = w_up.shape[-1]

    def shard2d(t, key):
        param, layer, _ = BIG[key]
        a = t[param][0] if layer is None else t[param][layer]
        return a.reshape(-1, a.shape[-1])

    small_w = _pack([wts[k] for k in SMALL_SHARDED])
    bf = {k: shard2d(wts, k).astype(BF16) for k in BIG}
    gather_groups = {"mix0": ("w_in", "w_out", "small"), "ffn0": ("w_up0", "w_down0"),
                     "ffn1": ("w_pool", "w_up1", "w_down1")}
    order = list(gather_groups)
    started = {}

    def start_gather(grp, after):
        arrs = [small_w if k == "small" else bf[k] for k in gather_groups[grp]]
        started[grp] = _exchange_start(arrs, [False] * len(arrs), after, name=f"gather_{grp}_start")

    start_gather(order[0], small_w)

    def weights(grp, after):
        keys = gather_groups[grp]
        lands = _exchange_wait(started[grp], [False] * len(keys), after, name=f"gather_{grp}_wait")
        gw = dict(zip(keys, lands))
        if grp != order[-1]:
            start_gather(order[order.index(grp) + 1], lands[0])
        if grp == "ffn0":
            return {"w_up": gw["w_up0"], "w_down": gw["w_down0"].reshape(N_PAIR, -1, d)}
        if grp == "ffn1":
            return {"w_up": gw["w_up1"], "w_down": gw["w_down1"].reshape(N_PAIR, -1, d),
                    "w_pool": _from_dev_major(gw["w_pool"].reshape(N_DEV, len(POOL_WINDOWS), -1, POOL_GROUP), 1)}
        per_dev = gw["small"].reshape(N_DEV, -1)
        sizes = [wts[k].size for k in SMALL_SHARDED]
        offs = [sum(sizes[:i]) for i in range(len(sizes))]
        small_full = {k: per_dev[:, o:o + n_].reshape((N_DEV,) + wts[k].shape)
                      for k, o, n_ in zip(SMALL_SHARDED, offs, sizes)}
        return {
            "norm_mix_even": norm_mix_even, "ln_a_g": ln_a_g, "ln_a_b": ln_a_b, "norm_ffn": norm_ffn,
            "norm_final": norm_final[None],
            "w_in": _from_dev_major(gw["w_in"], 1),
            "w_out": gw["w_out"].reshape(-1, d),
            "conv_a": _from_dev_major(small_full["conv_a"][:, 0], 1),
            "conv_b": _from_dev_major(small_full["conv_b"][:, 0], 1),
            "norm_mix_odd": _from_dev_major(small_full["norm_mix_odd"], 1),
            "pool_scale": _from_dev_major(small_full["pool_scale"], 1),
            "conv_ffn": [small_full["conv_ffn_w"][:, l] for l in range(2)],
        }

    def dev_major(k, g):
        if k == "w_in":
            return _to_dev_major(g, 1)
        if k == "w_pool":
            return _to_dev_major(g, 1).reshape(N_DEV, -1, POOL_GROUP)
        return g.reshape(N_DEV, -1, g.shape[-1])

    sent, big_grads = {}, {}

    def grads_ready(grp, grads):
        big_grads.update(grads)
        if grp == "mix0":
            return None
        keys = tuple(grads)
        parts = [dev_major(k, grads[k]) for k in keys]
        sent[grp] = (keys, _exchange_start(parts, [True] * len(keys), parts[0], name=f"grads_{grp}_start"))
        return sent[grp][1][-1]

    loss, dx, g = _local_step(x[0], loss_target[0], weights, grads_ready)

    small_parts = _pack([
        _to_dev_major(g["conv_a"], 1), _to_dev_major(g["conv_b"], 1), _to_dev_major(g["norm_mix_odd"], 1),
        _to_dev_major(g["pool_scale"], 1), jnp.stack(g["conv_ffn"], axis=1)], lead=(N_DEV,))
    repl_parts = _pack([g[k] for k in SMALL_REPLICATED])
    last_keys = ("w_in", "w_out", "small", "replicated")
    last = _exchange_start([dev_major("w_in", big_grads["w_in"]), dev_major("w_out", big_grads["w_out"]),
                            small_parts, repl_parts], [True, True, True, False], dx, name="grads_mix0_start")
    sent["mix0"] = (last_keys, last)

    landed, after = {}, last[-1]
    for grp in ("ffn1", "ffn0", "mix0"):
        keys, st = sent[grp]
        scat = [k != "replicated" for k in keys]
        landed.update(zip(keys, _exchange_wait(st, scat, after, name=f"grads_{grp}_wait")))
    big_out = {}
    for k in BIG:
        big_out[k] = _adamw(landed[k], shard2d(wts, k), shard2d(mom, k), shard2d(var, k), name=f"adamw_{k}",
                            tr=BIG[k][2])
    out = {}
    for k in ("w_in", "w_out", "w_pool"):
        out[k] = [t.reshape(wts[k].shape) for t in big_out[k]]
    for k in ("w_up", "w_down"):
        out[k] = [jnp.stack([a, b]).reshape(wts[k].shape) for a, b in zip(big_out[k + "0"], big_out[k + "1"])]
    res = _adamw(landed["small"], small_w, _pack([mom[k] for k in SMALL_SHARDED]),
                 _pack([var[k] for k in SMALL_SHARDED]), name="adamw_small", tr=small_w.shape[0])
    unpacked = [_unpack(t, [wts[k].shape for k in SMALL_SHARDED]) for t in res]
    for j, k in enumerate(SMALL_SHARDED):
        out[k] = [u[j] for u in unpacked]
    repl_w = _pack([wts[k] for k in SMALL_REPLICATED])
    res = _adamw(landed["replicated"], repl_w, _pack([mom[k] for k in SMALL_REPLICATED]),
                 _pack([var[k] for k in SMALL_REPLICATED]), name="adamw_replicated", tr=repl_w.shape[0])
    unpacked = [_unpack(t, [wts[k].shape for k in SMALL_REPLICATED]) for t in res]
    for j, k in enumerate(SMALL_REPLICATED):
        out[k] = [u[j] for u in unpacked]

    loss = lax.psum(loss, ("x", "y", "c"))
    return (loss, dx[None], *[out[k][0] for k in names], *[out[k][1] for k in names],
            *[out[k][2] for k in names], *[out[k][3] for k in names])
```

```python
import jax
import jax.numpy as jnp
from jax import lax
from jax.experimental import pallas as pl
from jax.experimental.pallas import tpu as pltpu

F32 = jnp.float32
BF16 = jnp.bfloat16

RMS_EPS = 1e-6
LN_EPS = 1e-5
ADAM_LR = 0.001
ADAM_B1 = 0.9
ADAM_B2 = 0.999
ADAM_EPS = 1e-08
ADAM_WD = 0.01
ADAM_STEP = 10

N_DEV = 8
N_PAIR = N_DEV // 2
A_WIDTH = 512
A_TAPS = 31
POOL_WINDOWS = (2, 4, 8, 16)
POOL_GROUP = 256
HALO_A = 32
HALO_S = 16
VMEM_LIMIT = 56 * 1024 * 1024


def _params(sem, vmem=VMEM_LIMIT):
    return pltpu.CompilerParams(dimension_semantics=sem, vmem_limit_bytes=vmem)


def _sigmoid(x):
    return 1.0 / (1.0 + jnp.exp(-x))


def _prev_blk(i, ts, hb):
    return jnp.maximum(i * (ts // hb) - 1, 0)


def _next_blk(i, ts, hb, s):
    return jnp.minimum((i + 1) * (ts // hb), s // hb - 1)


def _mm(a, b, *, name, ta=False, tb=False, add=None, dep=None, out_dtype=F32, tm=512, tn=512, tk=None):
    m, k = (a.shape[1], a.shape[0]) if ta else a.shape
    n = b.shape[0] if tb else b.shape[1]
    tk = k if tk is None else tk
    tm, tn, tk = min(tm, m), min(tn, n), min(tk, k)
    assert m % tm == 0 and n % tn == 0 and k % tk == 0, (name, m, n, k, tm, tn, tk)
    nk = k // tk
    dims = (((0,) if ta else (1,), (1,) if tb else (0,)), ((), ()))
    n_in = 2 + (add is not None) + (dep is not None)

    def body(*refs):
        a_ref, b_ref = refs[0], refs[1]
        add_ref = refs[2] if add is not None else None
        o_ref = refs[n_in]
        part = lax.dot_general(a_ref[...].astype(BF16), b_ref[...].astype(BF16), dims, preferred_element_type=F32)

        def finish(r):
            if add_ref is not None:
                r = r + add_ref[...]
            o_ref[...] = r.astype(out_dtype)

        if nk == 1:
            finish(part)
            return
        acc_ref = refs[-1]
        kk = pl.program_id(2)

        @pl.when(kk == 0)
        def _():
            acc_ref[...] = part

        @pl.when(kk > 0)
        def _():
            acc_ref[...] += part

        @pl.when(kk == nk - 1)
        def _():
            finish(acc_ref[...])

    a_spec = pl.BlockSpec((tk, tm), lambda i, j, kk: (kk, i)) if ta else pl.BlockSpec((tm, tk), lambda i, j, kk: (i, kk))
    b_spec = pl.BlockSpec((tn, tk), lambda i, j, kk: (j, kk)) if tb else pl.BlockSpec((tk, tn), lambda i, j, kk: (kk, j))
    in_specs = [a_spec, b_spec]
    args = [a, b]
    if add is not None:
        in_specs.append(pl.BlockSpec((tm, tn), lambda i, j, kk: (i, j)))
        args.append(add)
    if dep is not None:
        in_specs.append(pl.BlockSpec(memory_space=pl.ANY))
        args.append(dep)
    return pl.pallas_call(
        body, name=name, grid=(m // tm, n // tn, nk),
        in_specs=in_specs, out_specs=pl.BlockSpec((tm, tn), lambda i, j, kk: (i, j)),
        out_shape=jax.ShapeDtypeStruct((m, n), out_dtype),
        scratch_shapes=[pltpu.VMEM((tm, tn), F32)] if nk > 1 else [],
        compiler_params=_params(("parallel", "parallel", "arbitrary")),
    )(*args)


def _rms_rows(xv, gv):
    return xv * lax.rsqrt(jnp.mean(xv * xv, axis=-1, keepdims=True) + RMS_EPS) * gv


def _rms_mm(x, g, w, *, name, out_dtype, tm=1024, tn=512):
    s, d = x.shape
    n = w.shape[1]
    tm = min(tm, s)
    assert s % tm == 0 and n % tn == 0

    def body(x_ref, g_ref, w_ref, h_ref, z_ref, hs_ref):
        @pl.when(pl.program_id(1) == 0)
        def _():
            hv = _rms_rows(x_ref[...], g_ref[...]).astype(BF16)
            hs_ref[...] = hv
            h_ref[...] = hv

        z_ref[...] = jnp.dot(hs_ref[...], w_ref[...], preferred_element_type=F32).astype(out_dtype)

    return pl.pallas_call(
        body, name=name, grid=(s // tm, n // tn),
        in_specs=[pl.BlockSpec((tm, d), lambda i, j: (i, 0)),
                  pl.BlockSpec((1, d), lambda i, j: (0, 0)),
                  pl.BlockSpec((d, tn), lambda i, j: (0, j))],
        out_specs=[pl.BlockSpec((tm, d), lambda i, j: (i, 0)),
                   pl.BlockSpec((tm, tn), lambda i, j: (i, j))],
        out_shape=[jax.ShapeDtypeStruct((s, d), BF16), jax.ShapeDtypeStruct((s, n), out_dtype)],
        scratch_shapes=[pltpu.VMEM((tm, d), BF16)],
        compiler_params=_params(("parallel", "arbitrary")),
    )(x, g, w)


def _rms_bwd_rows(xv, gv, dh):
    r = lax.rsqrt(jnp.mean(xv * xv, axis=-1, keepdims=True) + RMS_EPS)
    xh = xv * r
    dn = dh * gv
    dx = r * (dn - xh * jnp.mean(dn * xh, axis=-1, keepdims=True))
    return dx, dh * xh


def _rms_bwd(x, g, dh, dres, *, name, ts=256):
    s, d = x.shape
    ts = min(ts, s)

    def body(x_ref, g_ref, dh_ref, dres_ref, dx_ref, dxb_ref, dg_ref):
        @pl.when(pl.program_id(0) == 0)
        def _():
            dg_ref[...] = jnp.zeros_like(dg_ref)

        dx, dgr = _rms_bwd_rows(x_ref[...], g_ref[...], dh_ref[...])
        dx = dres_ref[...] + dx
        dx_ref[...] = dx
        dxb_ref[...] = dx.astype(BF16)
        dg_ref[...] += jnp.sum(dgr, axis=0, keepdims=True)

    row = pl.BlockSpec((ts, d), lambda i: (i, 0))
    vec = pl.BlockSpec((1, d), lambda i: (0, 0))
    return pl.pallas_call(
        body, name=name, grid=(s // ts,),
        in_specs=[row, vec, row, row], out_specs=[row, row, vec],
        out_shape=[jax.ShapeDtypeStruct((s, d), F32), jax.ShapeDtypeStruct((s, d), BF16),
                   jax.ShapeDtypeStruct((1, d), F32)],
        compiler_params=_params(("arbitrary",)),
    )(x, g, dh, dres)


def _final_loss(x, g, target, *, name, ts=256):
    s, d = x.shape
    ts = min(ts, s)

    def body(x_ref, g_ref, t_ref, loss_ref, dx_ref, dxb_ref, dg_ref):
        @pl.when(pl.program_id(0) == 0)
        def _():
            loss_ref[...] = jnp.zeros_like(loss_ref)
            dg_ref[...] = jnp.zeros_like(dg_ref)

        xv = x_ref[...]
        gv = g_ref[...]
        r = lax.rsqrt(jnp.mean(xv * xv, axis=-1, keepdims=True) + RMS_EPS)
        xh = xv * r
        err = xh * gv - t_ref[...]
        loss_ref[...] += 0.5 * jnp.sum(jnp.mean(err * err, axis=-1, keepdims=True), axis=0, keepdims=True)
        dy = err * (1.0 / d)
        dn = dy * gv
        dx = r * (dn - xh * jnp.mean(dn * xh, axis=-1, keepdims=True))
        dx_ref[...] = dx
        dxb_ref[...] = dx.astype(BF16)
        dg_ref[...] += jnp.sum(dy * xh, axis=0, keepdims=True)

    row = pl.BlockSpec((ts, d), lambda i: (i, 0))
    vec = pl.BlockSpec((1, d), lambda i: (0, 0))
    return pl.pallas_call(
        body, name=name, grid=(s // ts,),
        in_specs=[row, vec, row],
        out_specs=[pl.BlockSpec((1, 1), lambda i: (0, 0)), row, row, vec],
        out_shape=[jax.ShapeDtypeStruct((1, 1), F32), jax.ShapeDtypeStruct((s, d), F32),
                   jax.ShapeDtypeStruct((s, d), BF16), jax.ShapeDtypeStruct((1, d), F32)],
        compiler_params=_params(("arbitrary",)),
    )(x, g, target)


def _conv_taps(ext_ref, w_ref, n_taps, base, r0, rows, reverse=False):
    acc = None
    for k in range(n_taps):
        off = r0 + (base - k if reverse else base + k)
        term = w_ref[k:k + 1, :] * ext_ref[pl.ds(off, rows), :]
        acc = term if acc is None else acc + term
    return acc


def _mix0_fwd(z, conv_a, ln_g, ln_b, conv_b, *, name, ts=256, rc=32):
    s = z.shape[0]
    c = A_WIDTH
    hb = HALO_A

    def body(z_ref, zp_ref, wa_ref, lg_ref, lb_ref, wb_ref, ab_ref, ca_ref, exta, extb):
        keep = jnp.where(pl.program_id(0) > 0, 1.0, 0.0)
        zp = zp_ref[...]
        exta[0:hb, :] = zp[:, 0:c] * _sigmoid(zp[:, c:2 * c]) * keep
        extb[0:hb, :] = zp[:, 3 * c:4 * c] * zp[:, 4 * c:5 * c] * keep
        exta[hb:hb + ts, :] = z_ref[:, 0:c] * _sigmoid(z_ref[:, c:2 * c])
        extb[hb:hb + ts, :] = z_ref[:, 3 * c:4 * c] * z_ref[:, 4 * c:5 * c]
        lg = lg_ref[...]
        lb = lb_ref[...]
        for q in range(ts // rc):
            r0 = q * rc
            ca = _conv_taps(exta, wa_ref, A_TAPS, hb - (A_TAPS - 1), r0, rc)
            ca_ref[r0:r0 + rc, :] = ca
            mu = jnp.mean(ca, axis=-1, keepdims=True)
            xc = ca - mu
            rs = lax.rsqrt(jnp.mean(xc * xc, axis=-1, keepdims=True) + LN_EPS)
            l = xc * rs * lg + lb
            ab_ref[r0:r0 + rc, 0:c] = (l * _sigmoid(l)).astype(BF16)
            cbc = _conv_taps(extb, wb_ref, 3, hb - 2, r0, rc)
            ab_ref[r0:r0 + rc, c:2 * c] = (z_ref[r0:r0 + rc, 2 * c:3 * c] * cbc).astype(BF16)

    return pl.pallas_call(
        body, name=name, grid=(s // ts,),
        in_specs=[pl.BlockSpec((ts, 5 * c), lambda i: (i, 0)),
                  pl.BlockSpec((hb, 5 * c), lambda i: (_prev_blk(i, ts, hb), 0)),
                  pl.BlockSpec((32, c), lambda i: (0, 0)),
                  pl.BlockSpec((1, c), lambda i: (0, 0)),
                  pl.BlockSpec((1, c), lambda i: (0, 0)),
                  pl.BlockSpec((8, c), lambda i: (0, 0))],
        out_specs=[pl.BlockSpec((ts, 2 * c), lambda i: (i, 0)),
                   pl.BlockSpec((ts, c), lambda i: (i, 0))],
        out_shape=[jax.ShapeDtypeStruct((s, 2 * c), BF16), jax.ShapeDtypeStruct((s, c), F32)],
        scratch_shapes=[pltpu.VMEM((hb + ts, c), F32), pltpu.VMEM((hb + ts, c), F32)],
        compiler_params=_params(("parallel",)),
    )(z, z, conv_a, ln_g, ln_b, conv_b)


def _mix0_bwd(z, ca, dab, conv_a, ln_g, ln_b, conv_b, *, name, ts=256, rc=32):
    s = z.shape[0]
    c = A_WIDTH
    hb = HALO_A
    ta = A_TAPS

    def body(z_ref, zp_ref, zn_ref, ca_ref, can_ref, d_ref, dn_ref, wa_ref, lg_ref, lb_ref, wb_ref,
             dz_ref, dwa_ref, dwb_ref, dlg_ref, dlb_ref, exta, extb, extdca, extdcb):
        i = pl.program_id(0)
        keep_p = jnp.where(i > 0, 1.0, 0.0)
        keep_n = jnp.where(i < s // ts - 1, 1.0, 0.0)

        @pl.when(i == 0)
        def _():
            dwa_ref[...] = jnp.zeros_like(dwa_ref)
            dwb_ref[...] = jnp.zeros_like(dwb_ref)
            dlg_ref[...] = jnp.zeros_like(dlg_ref)
            dlb_ref[...] = jnp.zeros_like(dlb_ref)

        lg = lg_ref[...]
        lb = lb_ref[...]
        zp = zp_ref[...]
        exta[0:hb, :] = zp[:, 0:c] * _sigmoid(zp[:, c:2 * c]) * keep_p
        extb[0:hb, :] = zp[:, 3 * c:4 * c] * zp[:, 4 * c:5 * c] * keep_p
        exta[hb:hb + ts, :] = z_ref[:, 0:c] * _sigmoid(z_ref[:, c:2 * c])
        extb[hb:hb + ts, :] = z_ref[:, 3 * c:4 * c] * z_ref[:, 4 * c:5 * c]

        def ln_bwd(cav, dav):
            mu = jnp.mean(cav, axis=-1, keepdims=True)
            xc = cav - mu
            rs = lax.rsqrt(jnp.mean(xc * xc, axis=-1, keepdims=True) + LN_EPS)
            nv = xc * rs
            l = nv * lg + lb
            sg = _sigmoid(l)
            dl = dav * (sg * (1.0 + l * (1.0 - sg)))
            dnv = dl * lg
            dca = rs * (dnv - jnp.mean(dnv, axis=-1, keepdims=True)
                        - nv * jnp.mean(dnv * nv, axis=-1, keepdims=True))
            return dca, dl, nv

        dlg_acc = jnp.zeros((1, c), F32)
        dlb_acc = jnp.zeros((1, c), F32)
        for q in range(ts // rc):
            r0 = q * rc
            dca, dl, nv = ln_bwd(ca_ref[r0:r0 + rc, :], d_ref[r0:r0 + rc, 0:c])
            extdca[r0:r0 + rc, :] = dca
            dlg_acc = dlg_acc + jnp.sum(dl * nv, axis=0, keepdims=True)
            dlb_acc = dlb_acc + jnp.sum(dl, axis=0, keepdims=True)
            extdcb[r0:r0 + rc, :] = d_ref[r0:r0 + rc, c:2 * c] * z_ref[r0:r0 + rc, 2 * c:3 * c]
        dca_n, _, _ = ln_bwd(can_ref[...], dn_ref[:, 0:c])
        extdca[ts:ts + hb, :] = dca_n * keep_n
        extdcb[ts:ts + hb, :] = dn_ref[:, c:2 * c] * zn_ref[:, 2 * c:3 * c] * keep_n
        dlg_ref[...] += dlg_acc
        dlb_ref[...] += dlb_acc

        for q in range(ts // rc):
            r0 = q * rc
            zr = z_ref[r0:r0 + rc, :]
            dga = _conv_taps(extdca, wa_ref, ta, ta - 1, r0, rc, reverse=True)
            sg = _sigmoid(zr[:, c:2 * c])
            dz_ref[r0:r0 + rc, 0:c] = (dga * sg).astype(BF16)
            dz_ref[r0:r0 + rc, c:2 * c] = (dga * zr[:, 0:c] * sg * (1.0 - sg)).astype(BF16)
            cbc = _conv_taps(extb, wb_ref, 3, hb - 2, r0, rc)
            dz_ref[r0:r0 + rc, 2 * c:3 * c] = (d_ref[r0:r0 + rc, c:2 * c] * cbc).astype(BF16)
            dcb = _conv_taps(extdcb, wb_ref, 3, 2, r0, rc, reverse=True)
            dz_ref[r0:r0 + rc, 3 * c:4 * c] = (dcb * zr[:, 4 * c:5 * c]).astype(BF16)
            dz_ref[r0:r0 + rc, 4 * c:5 * c] = (dcb * zr[:, 3 * c:4 * c]).astype(BF16)

        dca_t = extdca[0:ts, :]
        for k in range(ta):
            dwa_ref[k:k + 1, :] += jnp.sum(dca_t * exta[pl.ds(hb - (ta - 1) + k, ts), :], axis=0, keepdims=True)
        dcb_t = extdcb[0:ts, :]
        for k in range(3):
            dwb_ref[k:k + 1, :] += jnp.sum(dcb_t * extb[pl.ds(hb - 2 + k, ts), :], axis=0, keepdims=True)

    def tile(w):
        return pl.BlockSpec((ts, w), lambda i: (i, 0))

    def prev(w):
        return pl.BlockSpec((hb, w), lambda i: (_prev_blk(i, ts, hb), 0))

    def nxt(w):
        return pl.BlockSpec((hb, w), lambda i: (_next_blk(i, ts, hb, s), 0))

    def const(r, w):
        return pl.BlockSpec((r, w), lambda i: (0, 0))

    return pl.pallas_call(
        body, name=name, grid=(s // ts,),
        in_specs=[tile(5 * c), prev(5 * c), nxt(5 * c), tile(c), nxt(c), tile(2 * c), nxt(2 * c),
                  const(32, c), const(1, c), const(1, c), const(8, c)],
        out_specs=[tile(5 * c), const(32, c), const(8, c), const(1, c), const(1, c)],
        out_shape=[jax.ShapeDtypeStruct((s, 5 * c), BF16), jax.ShapeDtypeStruct((32, c), F32),
                   jax.ShapeDtypeStruct((8, c), F32), jax.ShapeDtypeStruct((1, c), F32),
                   jax.ShapeDtypeStruct((1, c), F32)],
        scratch_shapes=[pltpu.VMEM((hb + ts, c), F32), pltpu.VMEM((hb + ts, c), F32),
                        pltpu.VMEM((ts + hb, c), F32), pltpu.VMEM((ts + hb, c), F32)],
        compiler_params=_params(("arbitrary",)),
    )(z, z, z, ca, ca, dab, dab, conv_a, ln_g, ln_b, conv_b)


def _ffn_up(x, g, w8, *, name, tm=1024):
    s, d = x.shape
    nb, _, c = w8.shape
    tm = min(tm, s)

    def body(x_ref, g_ref, w_ref, h_ref, u_ref, hs_ref):
        @pl.when(pl.program_id(1) == 0)
        def _():
            hv = _rms_rows(x_ref[...], g_ref[...]).astype(BF16)
            hs_ref[...] = hv
            h_ref[...] = hv

        u_ref[...] = jnp.dot(hs_ref[...], w_ref[...], preferred_element_type=F32).astype(BF16)

    return pl.pallas_call(
        body, name=name, grid=(s // tm, nb),
        in_specs=[pl.BlockSpec((tm, d), lambda i, k: (i, 0)),
                  pl.BlockSpec((1, d), lambda i, k: (0, 0)),
                  pl.BlockSpec((None, d, c), lambda i, k: (k, 0, 0))],
        out_specs=[pl.BlockSpec((tm, d), lambda i, k: (i, 0)),
                   pl.BlockSpec((None, tm, c), lambda i, k: (k, i, 0))],
        out_shape=[jax.ShapeDtypeStruct((s, d), BF16), jax.ShapeDtypeStruct((nb, s, c), BF16)],
        scratch_shapes=[pltpu.VMEM((tm, d), BF16)],
        compiler_params=_params(("parallel", "arbitrary")),
    )(x, g, w8)


def _ffn_mid(up8, wc8, *, name, ts=512, rc=32):
    nb, s, c = up8.shape
    hb = HALO_S
    ts = min(ts, s)

    def body(g_ref, gp_ref, v_ref, vp_ref, wg_ref, wv_ref, act_ref, extg, extv):
        keep = jnp.where(pl.program_id(0) > 0, 1.0, 0.0)
        extg[0:hb, :] = gp_ref[...].astype(F32) * keep
        extv[0:hb, :] = vp_ref[...].astype(F32) * keep
        extg[hb:hb + ts, :] = g_ref[...].astype(F32)
        extv[hb:hb + ts, :] = v_ref[...].astype(F32)
        for q in range(ts // rc):
            r0 = q * rc
            gg = _conv_taps(extg, wg_ref, 3, hb - 2, r0, rc)
            vv = _conv_taps(extv, wv_ref, 3, hb - 2, r0, rc)
            act_ref[r0:r0 + rc, :] = (gg * _sigmoid(gg) * vv).astype(BF16)

    def blk(rows, off, halo):
        if halo:
            return pl.BlockSpec((None, rows, c), lambda i, j: (j + off, _prev_blk(i, ts, hb), 0))
        return pl.BlockSpec((None, rows, c), lambda i, j: (j + off, i, 0))

    def taps(off):
        return pl.BlockSpec((None, 8, c), lambda i, j: (j + off, 0, 0))

    return pl.pallas_call(
        body, name=name, grid=(s // ts, N_PAIR),
        in_specs=[blk(ts, 0, False), blk(hb, 0, True), blk(ts, N_PAIR, False), blk(hb, N_PAIR, True),
                  taps(0), taps(N_PAIR)],
        out_specs=pl.BlockSpec((None, ts, c), lambda i, j: (j, i, 0)),
        out_shape=jax.ShapeDtypeStruct((N_PAIR, s, c), BF16),
        scratch_shapes=[pltpu.VMEM((hb + ts, c), F32), pltpu.VMEM((hb + ts, c), F32)],
        compiler_params=_params(("parallel", "parallel")),
    )(up8, up8, up8, up8, wc8, wc8)


def _ffn_down(act4, wd4, x, *, name, tm=512):
    npair, s, c = act4.shape
    d = x.shape[1]
    tm = min(tm, s)

    def body(a_ref, w_ref, x_ref, o_ref):
        acc = x_ref[...]
        for j in range(npair):
            acc = acc + jnp.dot(a_ref[j], w_ref[j], preferred_element_type=F32)
        o_ref[...] = acc

    return pl.pallas_call(
        body, name=name, grid=(s // tm,),
        in_specs=[pl.BlockSpec((npair, tm, c), lambda i: (0, i, 0)),
                  pl.BlockSpec((npair, c, d), lambda i: (0, 0, 0)),
                  pl.BlockSpec((tm, d), lambda i: (i, 0))],
        out_specs=pl.BlockSpec((tm, d), lambda i: (i, 0)),
        out_shape=jax.ShapeDtypeStruct((s, d), F32),
        compiler_params=_params(("parallel",)),
    )(act4, wd4, x)


def _ffn_dact(db, wd4, *, name, tm=1024):
    s, d = db.shape
    npair, c, _ = wd4.shape
    tm = min(tm, s)

    def body(d_ref, w_ref, o_ref):
        o_ref[...] = lax.dot_general(d_ref[...], w_ref[...], (((1,), (1,)), ((), ())),
                                     preferred_element_type=F32).astype(BF16)

    return pl.pallas_call(
        body, name=name, grid=(s // tm, npair),
        in_specs=[pl.BlockSpec((tm, d), lambda i, j: (i, 0)),
                  pl.BlockSpec((None, c, d), lambda i, j: (j, 0, 0))],
        out_specs=pl.BlockSpec((None, tm, c), lambda i, j: (j, i, 0)),
        out_shape=jax.ShapeDtypeStruct((npair, s, c), BF16),
        compiler_params=_params(("parallel", "parallel")),
    )(db, wd4)


def _ffn_dwdown(act4, db, *, name):
    npair, s, c = act4.shape
    d = db.shape[1]

    def body(a_ref, d_ref, o_ref):
        o_ref[...] = lax.dot_general(a_ref[...], d_ref[...], (((0,), (0,)), ((), ())),
                                     preferred_element_type=F32).astype(BF16)

    return pl.pallas_call(
        body, name=name, grid=(npair,),
        in_specs=[pl.BlockSpec((None, s, c), lambda j: (j, 0, 0)),
                  pl.BlockSpec((s, d), lambda j: (0, 0))],
        out_specs=pl.BlockSpec((None, c, d), lambda j: (j, 0, 0)),
        out_shape=jax.ShapeDtypeStruct((npair, c, d), BF16),
        compiler_params=_params(("parallel",)),
    )(act4, db)


def _ffn_midbwd(up8, dact4, wc8, *, name, ts=512, rc=32):
    nb, s, c = up8.shape
    hb = HALO_S
    ts = min(ts, s)
    n_i = s // ts

    def body(g_ref, gp_ref, gn_ref, v_ref, vp_ref, vn_ref, d_ref, dn_ref, wg_ref, wv_ref,
             dg_ref, dv_ref, dwg_ref, dwv_ref, extg, extv, extdg, extdv):
        i = pl.program_id(1)
        keep_p = jnp.where(i > 0, 1.0, 0.0)
        keep_n = jnp.where(i < n_i - 1, 1.0, 0.0)

        @pl.when(i == 0)
        def _():
            dwg_ref[...] = jnp.zeros_like(dwg_ref)
            dwv_ref[...] = jnp.zeros_like(dwv_ref)

        for ext, cur, prv, nxt in ((extg, g_ref, gp_ref, gn_ref), (extv, v_ref, vp_ref, vn_ref)):
            ext[0:hb, :] = prv[...].astype(F32) * keep_p
            ext[hb:hb + ts, :] = cur[...].astype(F32)
            ext[hb + ts:hb + ts + hb, :] = nxt[...].astype(F32)

        def du_rows(r0, rows, dav):
            gg = _conv_taps(extg, wg_ref, 3, hb - 2, r0, rows)
            vv = _conv_taps(extv, wv_ref, 3, hb - 2, r0, rows)
            sg = _sigmoid(gg)
            extdg[r0:r0 + rows, :] = dav * vv * (sg * (1.0 + gg * (1.0 - sg)))
            extdv[r0:r0 + rows, :] = dav * (gg * sg)

        for q in range(ts // rc):
            du_rows(q * rc, rc, d_ref[q * rc:q * rc + rc, :].astype(F32))
        du_rows(ts, 8, dn_ref[0:8, :].astype(F32) * keep_n)

        for q in range(ts // rc):
            r0 = q * rc
            dg_ref[r0:r0 + rc, :] = _conv_taps(extdg, wg_ref, 3, 2, r0, rc, reverse=True).astype(BF16)
            dv_ref[r0:r0 + rc, :] = _conv_taps(extdv, wv_ref, 3, 2, r0, rc, reverse=True).astype(BF16)
        for ext, extd, dw in ((extg, extdg, dwg_ref), (extv, extdv, dwv_ref)):
            du_t = extd[0:ts, :]
            for k in range(3):
                dw[k:k + 1, :] += jnp.sum(du_t * ext[pl.ds(hb - 2 + k, ts), :], axis=0, keepdims=True)

    def blk(off, which):
        if which == "prev":
            return pl.BlockSpec((None, hb, c), lambda j, i: (j + off, _prev_blk(i, ts, hb), 0))
        if which == "next":
            return pl.BlockSpec((None, hb, c), lambda j, i: (j + off, _next_blk(i, ts, hb, s), 0))
        return pl.BlockSpec((None, ts, c), lambda j, i: (j + off, i, 0))

    def taps(off):
        return pl.BlockSpec((None, 8, c), lambda j, i: (j + off, 0, 0))

    tile = pl.BlockSpec((None, ts, c), lambda j, i: (j, i, 0))
    acc = pl.BlockSpec((None, 8, c), lambda j, i: (j, 0, 0))
    return pl.pallas_call(
        body, name=name, grid=(N_PAIR, n_i),
        in_specs=[blk(0, "tile"), blk(0, "prev"), blk(0, "next"),
                  blk(N_PAIR, "tile"), blk(N_PAIR, "prev"), blk(N_PAIR, "next"),
                  blk(0, "tile"), blk(0, "next"), taps(0), taps(N_PAIR)],
        out_specs=[tile, tile, acc, acc],
        out_shape=[jax.ShapeDtypeStruct((N_PAIR, s, c), BF16), jax.ShapeDtypeStruct((N_PAIR, s, c), BF16),
                   jax.ShapeDtypeStruct((N_PAIR, 8, c), F32), jax.ShapeDtypeStruct((N_PAIR, 8, c), F32)],
        scratch_shapes=[pltpu.VMEM((hb + ts + hb, c), F32), pltpu.VMEM((hb + ts + hb, c), F32),
                        pltpu.VMEM((ts + 8, c), F32), pltpu.VMEM((ts + 8, c), F32)],
        compiler_params=_params(("parallel", "arbitrary")),
    )(up8, up8, up8, up8, up8, up8, dact4, dact4, wc8, wc8)


def _ffn_dh(dupg, dupv, w8, x, g, dres, *, name, tm=256):
    npair, s, c = dupg.shape
    d = x.shape[1]
    tm = min(tm, s)

    def body(dg_ref, dv_ref, w_ref, x_ref, g_ref, dres_ref, dx_ref, dxb_ref, dgain_ref):
        @pl.when(pl.program_id(0) == 0)
        def _():
            dgain_ref[...] = jnp.zeros_like(dgain_ref)

        nt = (((1,), (1,)), ((), ()))
        dh = None
        for j in range(npair):
            for src, k in ((dg_ref, j), (dv_ref, j + npair)):
                part = lax.dot_general(src[j], w_ref[k], nt, preferred_element_type=F32)
                dh = part if dh is None else dh + part
        dx, dgr = _rms_bwd_rows(x_ref[...], g_ref[...], dh)
        dx = dres_ref[...] + dx
        dx_ref[...] = dx
        dxb_ref[...] = dx.astype(BF16)
        dgain_ref[...] += jnp.sum(dgr, axis=0, keepdims=True)

    row = pl.BlockSpec((tm, d), lambda i: (i, 0))
    vec = pl.BlockSpec((1, d), lambda i: (0, 0))
    dup = pl.BlockSpec((npair, tm, c), lambda i: (0, i, 0))
    return pl.pallas_call(
        body, name=name, grid=(s // tm,),
        in_specs=[dup, dup, pl.BlockSpec((2 * npair, d, c), lambda i: (0, 0, 0)), row, vec, row],
        out_specs=[row, row, vec],
        out_shape=[jax.ShapeDtypeStruct((s, d), F32), jax.ShapeDtypeStruct((s, d), BF16),
                   jax.ShapeDtypeStruct((1, d), F32)],
        compiler_params=_params(("arbitrary",)),
    )(dupg, dupv, w8, x, g, dres)


def _ffn_dwup(h, dupg, dupv, *, name, tm=512):
    npair, s, c = dupg.shape
    d = h.shape[1]

    def body(h_ref, dg_ref, dv_ref, o_ref):
        tn = (((0,), (0,)), ((), ()))
        k = pl.program_id(1)

        @pl.when(k < npair)
        def _():
            o_ref[...] = lax.dot_general(h_ref[...], dg_ref[...], tn, preferred_element_type=F32).astype(BF16)

        @pl.when(k >= npair)
        def _():
            o_ref[...] = lax.dot_general(h_ref[...], dv_ref[...], tn, preferred_element_type=F32).astype(BF16)

    return pl.pallas_call(
        body, name=name, grid=(d // tm, 2 * npair),
        in_specs=[pl.BlockSpec((s, tm), lambda m, k: (0, m)),
                  pl.BlockSpec((None, s, c), lambda m, k: (jnp.minimum(k, npair - 1), 0, 0)),
                  pl.BlockSpec((None, s, c), lambda m, k: (jnp.maximum(k - npair, 0), 0, 0))],
        out_specs=pl.BlockSpec((None, tm, c), lambda m, k: (k, m, 0)),
        out_shape=jax.ShapeDtypeStruct((2 * npair, d, c), BF16),
        compiler_params=_params(("parallel", "arbitrary")),
    )(h, dupg, dupv)


def _pool_counts(i, ts, rows, window):
    t = lax.broadcasted_iota(jnp.int32, (rows, 1), 0) + i * ts + 1
    return jnp.minimum(t, window).astype(F32)


def _pool_fwd(x, g, *, name, ts=256):
    s, d = x.shape
    hb = HALO_S
    pg = POOL_GROUP
    ts = min(ts, s)

    def body(x_ref, xp_ref, g_ref, p_ref, ext):
        i = pl.program_id(0)
        keep = jnp.where(i > 0, 1.0, 0.0)
        gv = g_ref[...]
        ext[0:hb, :] = _rms_rows(xp_ref[...], gv) * keep
        ext[hb:hb + ts, :] = _rms_rows(x_ref[...], gv)
        for gi, w in enumerate(POOL_WINDOWS):
            cols = slice(gi * pg, (gi + 1) * pg)
            h = ext[hb:hb + ts, cols]
            acc = h
            for q in range(1, w):
                acc = acc + ext[pl.ds(hb - q, ts), cols]
            p_ref[:, cols] = (acc / _pool_counts(i, ts, ts, w) - h).astype(BF16)

    return pl.pallas_call(
        body, name=name, grid=(s // ts,),
        in_specs=[pl.BlockSpec((ts, d), lambda i: (i, 0)),
                  pl.BlockSpec((hb, d), lambda i: (_prev_blk(i, ts, hb), 0)),
                  pl.BlockSpec((1, d), lambda i: (0, 0))],
        out_specs=pl.BlockSpec((ts, d), lambda i: (i, 0)),
        out_shape=jax.ShapeDtypeStruct((s, d), BF16),
        scratch_shapes=[pltpu.VMEM((hb + ts, d), F32)],
        compiler_params=_params(("parallel",)),
    )(x, x, g)


def _pool_mm_fwd(p, w_pool, scale, x, *, name, ts=1024):
    s, d = x.shape
    pg = POOL_GROUP
    ts = min(ts, s)

    def body(p_ref, w_ref, s_ref, x_ref, o_ref, yu_ref):
        yu = jnp.dot(p_ref[...], w_ref[...], preferred_element_type=F32)
        yu_ref[...] = yu.astype(BF16)
        o_ref[...] = x_ref[...] + yu * s_ref[...]

    blk = pl.BlockSpec((ts, pg), lambda i, gi: (i, gi))
    return pl.pallas_call(
        body, name=name, grid=(s // ts, d // pg),
        in_specs=[blk, pl.BlockSpec((None, pg, pg), lambda i, gi: (gi, 0, 0)),
                  pl.BlockSpec((1, pg), lambda i, gi: (0, gi)), blk],
        out_specs=[blk, blk],
        out_shape=[jax.ShapeDtypeStruct((s, d), F32), jax.ShapeDtypeStruct((s, d), BF16)],
        compiler_params=_params(("parallel", "parallel")),
    )(p, w_pool, scale, x)


def _pool_mm_bwd(dres, w_pool, scale, yu, dep, *, name, ts=1024):
    s, d = dres.shape
    pg = POOL_GROUP
    ts = min(ts, s)

    def body(d_ref, w_ref, s_ref, yu_ref, dep_ref, dyc_ref, dp_ref, ds_ref):
        @pl.when(pl.program_id(1) == 0)
        def _():
            ds_ref[...] = jnp.zeros_like(ds_ref)

        dv = d_ref[...]
        dyc = (dv * s_ref[...]).astype(BF16)
        dyc_ref[...] = dyc
        dp_ref[...] = lax.dot_general(dyc, w_ref[...], (((1,), (1,)), ((), ())), preferred_element_type=F32)
        ds_ref[...] += jnp.sum(dv * yu_ref[...].astype(F32), axis=0, keepdims=True)

    blk = pl.BlockSpec((ts, pg), lambda gi, i: (i, gi))
    vec = pl.BlockSpec((1, pg), lambda gi, i: (0, gi))
    return pl.pallas_call(
        body, name=name, grid=(d // pg, s // ts),
        in_specs=[blk, pl.BlockSpec((None, pg, pg), lambda gi, i: (gi, 0, 0)), vec, blk,
                  pl.BlockSpec(memory_space=pl.ANY)],
        out_specs=[blk, blk, vec],
        out_shape=[jax.ShapeDtypeStruct((s, d), BF16), jax.ShapeDtypeStruct((s, d), F32),
                   jax.ShapeDtypeStruct((1, d), F32)],
        compiler_params=_params(("parallel", "arbitrary")),
    )(dres, w_pool, scale, yu, dep)


def _pool_dw(p, dyc, *, name):
    s, d = p.shape
    pg = POOL_GROUP

    def body(p_ref, d_ref, o_ref):
        o_ref[...] = lax.dot_general(p_ref[...], d_ref[...], (((0,), (0,)), ((), ())),
                                     preferred_element_type=F32).astype(BF16)

    blk = pl.BlockSpec((s, pg), lambda gi: (0, gi))
    return pl.pallas_call(
        body, name=name, grid=(d // pg,),
        in_specs=[blk, blk], out_specs=pl.BlockSpec((None, pg, pg), lambda gi: (gi, 0, 0)),
        out_shape=jax.ShapeDtypeStruct((d // pg, pg, pg), BF16),
        compiler_params=_params(("parallel",)),
    )(p, dyc)


def _pool_bwd(dp, x, g, dres, *, name, ts=256):
    s, d = x.shape
    hb = HALO_S
    pg = POOL_GROUP
    ts = min(ts, s)
    n_i = s // ts

    def body(dp_ref, dpn_ref, x_ref, g_ref, dres_ref, dx_ref, dxb_ref, dg_ref, ext, dh):
        i = pl.program_id(0)
        keep_n = jnp.where(i < n_i - 1, 1.0, 0.0)

        @pl.when(i == 0)
        def _():
            dg_ref[...] = jnp.zeros_like(dg_ref)

        for gi, w in enumerate(POOL_WINDOWS):
            cols = slice(gi * pg, (gi + 1) * pg)
            ext[0:ts, cols] = dp_ref[:, cols] / _pool_counts(i, ts, ts, w)
            ext[ts:ts + hb, cols] = dpn_ref[:, cols] / _pool_counts(i + 1, ts, hb, w) * keep_n
        for gi, w in enumerate(POOL_WINDOWS):
            cols = slice(gi * pg, (gi + 1) * pg)
            acc = ext[0:ts, cols]
            for q in range(1, w):
                acc = acc + ext[pl.ds(q, ts), cols]
            dh[:, cols] = acc - dp_ref[:, cols]
        dx, dgr = _rms_bwd_rows(x_ref[...], g_ref[...], dh[...])
        dx = dres_ref[...] + dx
        dx_ref[...] = dx
        dxb_ref[...] = dx.astype(BF16)
        dg_ref[...] += jnp.sum(dgr, axis=0, keepdims=True)

    row = pl.BlockSpec((ts, d), lambda i: (i, 0))
    vec = pl.BlockSpec((1, d), lambda i: (0, 0))
    return pl.pallas_call(
        body, name=name, grid=(n_i,),
        in_specs=[row, pl.BlockSpec((hb, d), lambda i: (_next_blk(i, ts, hb, s), 0)), row, vec, row],
        out_specs=[row, row, vec],
        out_shape=[jax.ShapeDtypeStruct((s, d), F32), jax.ShapeDtypeStruct((s, d), BF16),
                   jax.ShapeDtypeStruct((1, d), F32)],
        scratch_shapes=[pltpu.VMEM((ts + hb, d), F32), pltpu.VMEM((ts, d), F32)],
        compiler_params=_params(("arbitrary",)),
    )(dp, dp, x, g, dres)


def _pad_rows(w, rows):
    pad = [(0, 0)] * (w.ndim - 2) + [(0, rows - w.shape[-2]), (0, 0)]
    return jnp.pad(w, pad)


def _ffn_layer_fwd(x, nf, w8, wc8, wd4, tag):
    h, up8 = _ffn_up(x, nf, w8, name=f"ffn{tag}_up")
    act4 = _ffn_mid(up8, wc8, name=f"ffn{tag}_mid")
    x_out = _ffn_down(act4, wd4, x, name=f"ffn{tag}_down")
    return x_out, (h, up8, act4)


def _ffn_layer_bwd(d, db, x, nf, w8, wc8, wd4, saved, tag):
    h, up8, act4 = saved
    dact4 = _ffn_dact(db, wd4, name=f"ffn{tag}_dact")
    dwd4 = _ffn_dwdown(act4, db, name=f"ffn{tag}_dwdown")
    dupg, dupv, dwg, dwv = _ffn_midbwd(up8, dact4, wc8, name=f"ffn{tag}_midbwd")
    dx, dxb, dnf = _ffn_dh(dupg, dupv, w8, x, nf, d, name=f"ffn{tag}_dh")
    dw8 = _ffn_dwup(h, dupg, dupv, name=f"ffn{tag}_dwup")
    dwc8 = jnp.concatenate([dwg, dwv], axis=0)[:, :3]
    return dx, dxb, dnf, dw8, dwc8, dwd4


def _local_step(x, target, weights, grads_ready):
    w0 = weights("mix0", x)
    wa = _pad_rows(w0["conv_a"], 32)
    wb = _pad_rows(w0["conv_b"], 8)
    wc = [_pad_rows(w0["conv_ffn"][l], 8) for l in range(2)]
    h0, z = _rms_mm(x, w0["norm_mix_even"], w0["w_in"], name="mix0_in", out_dtype=F32, tn=1280)
    ab, ca = _mix0_fwd(z, wa, w0["ln_a_g"], w0["ln_a_b"], wb, name="mix0_mid")
    x1 = _mm(ab, w0["w_out"], add=x, name="mix0_out", tm=1024, tn=1024)
    w1 = weights("ffn0", x1)
    x2, ffn0 = _ffn_layer_fwd(x1, w0["norm_ffn"][0:1], w1["w_up"], wc[0], w1["w_down"], 0)
    w2 = weights("ffn1", x2)
    p = _pool_fwd(x2, w0["norm_mix_odd"], name="pool_mid")
    x3, yu = _pool_mm_fwd(p, w2["w_pool"], w0["pool_scale"], x2, name="pool_mm")
    x4, ffn1 = _ffn_layer_fwd(x3, w0["norm_ffn"][1:2], w2["w_up"], wc[1], w2["w_down"], 1)
    loss, d4, d4b, g_norm_final = _final_loss(x4, w0["norm_final"], target, name="final_loss")

    d3, d3b, g_nf1, g_up1, g_wc1, g_down1 = _ffn_layer_bwd(
        d4, d4b, x3, w0["norm_ffn"][1:2], w2["w_up"], wc[1], w2["w_down"], ffn1, 1)
    sent1 = grads_ready("ffn1", {"w_up1": g_up1, "w_down1": g_down1})
    dyc, dp, g_scale = _pool_mm_bwd(d3, w2["w_pool"], w0["pool_scale"], yu, sent1, name="pool_mm_bwd")
    g_pool = _pool_dw(p, dyc, name="pool_dw")
    d2, d2b, g_nmo = _pool_bwd(dp, x2, w0["norm_mix_odd"], d3, name="pool_midbwd")
    d1, d1b, g_nf0, g_up0, g_wc0, g_down0 = _ffn_layer_bwd(
        d2, d2b, x1, w0["norm_ffn"][0:1], w1["w_up"], wc[0], w1["w_down"], ffn0, 0)
    sent0 = grads_ready("ffn0", {"w_pool": g_pool, "w_up0": g_up0, "w_down0": g_down0})
    dab = _mm(d1b, w0["w_out"], tb=True, dep=sent0, name="mix0_dab", tm=1024, tn=1024)
    g_out = _mm(ab, d1b, ta=True, out_dtype=BF16, name="mix0_dwout", tm=1024, tn=512)
    dz, g_wa, g_wb, g_lg, g_lb = _mix0_bwd(z, ca, dab, wa, w0["ln_a_g"], w0["ln_a_b"], wb, name="mix0_midbwd")
    dh0 = _mm(dz, w0["w_in"], tb=True, name="mix0_dh", tm=1024, tn=1024)
    g_in = _mm(h0, dz, ta=True, out_dtype=BF16, name="mix0_dwin", tm=1024, tn=512)
    dx, _, g_nme = _rms_bwd(x, w0["norm_mix_even"], dh0, d1, name="mix0_rmsbwd")
    grads_ready("mix0", {"w_in": g_in, "w_out": g_out})

    small = {
        "norm_mix_even": g_nme, "conv_a": g_wa[:A_TAPS], "ln_a_g": g_lg, "ln_a_b": g_lb, "conv_b": g_wb[:3],
        "norm_mix_odd": g_nmo, "pool_scale": g_scale, "norm_ffn": jnp.concatenate([g_nf0, g_nf1], axis=0),
        "conv_ffn": [g_wc0, g_wc1], "norm_final": g_norm_final,
    }
    return loss[0, 0], dx, small


def _my_pos():
    return lax.axis_index("x"), lax.axis_index("y"), lax.axis_index("c")


def _flip(pos, r):
    x, y, c = pos
    return (1 - x if r & 4 else x, 1 - y if r & 2 else y, 1 - c if r & 1 else c)


def _dev_index(pos):
    return 4 * pos[0] + 2 * pos[1] + pos[2]


_HBM = pl.BlockSpec(memory_space=pltpu.HBM)
_SEM = pl.BlockSpec(memory_space=pltpu.SEMAPHORE)
_EFFECT = pltpu.SideEffectType.DATAFLOW_SIDE_EFFECTING


def _exchange_copy(ins, lands, send_sems, recv_sems, scatter, pos, a, r, receiving):
    me = _dev_index(pos)
    peer = _flip(pos, r)
    dest = _dev_index(pos) if receiving else _dev_index(peer)
    src = ins[a].at[dest] if scatter[a] else ins[a]
    slot = _dev_index(peer) if receiving else me
    return pltpu.make_async_remote_copy(
        src_ref=src, dst_ref=lands[a].at[slot], send_sem=send_sems.at[a * (N_DEV - 1) + r - 1],
        recv_sem=recv_sems.at[a * (N_DEV - 1) + r - 1],
        device_id=peer, device_id_type=pl.DeviceIdType.MESH)


ALL_PEERS = tuple(range(1, N_DEV))
CHIP_PEERS = (1, 2, 4, 6)
FORWARDED = (2, 4, 6)


def _exchange_start(arrays, scatter, after, *, name, peers=ALL_PEERS):
    n = len(arrays)
    me = _dev_index(_my_pos())
    lands = []
    for arr, sc in zip(arrays, scatter):
        own = lax.dynamic_index_in_dim(arr, me, 0, keepdims=True) if sc else arr[None]
        shape = arr.shape if sc else (N_DEV,) + arr.shape
        lands.append(lax.dynamic_update_slice(lax.empty(shape, arr.dtype), own, (me,) + (0,) * (len(shape) - 1)))

    def body(*refs):
        ins, lnd = refs[:n], refs[n:2 * n]
        send_sems, recv_sems = refs[2 * n + 1], refs[2 * n + 2]
        token = refs[-1]
        pos = _my_pos()
        for a in range(n):
            for r in peers:
                _exchange_copy(ins, lnd, send_sems, recv_sems, scatter, pos, a, r, receiving=False).start()
        token[...] = jnp.zeros_like(token)

    bufs = [pltpu.with_memory_space_constraint(t, pltpu.HBM) for t in list(arrays) + lands]
    sems = pltpu.SemaphoreType.DMA((n * (N_DEV - 1),))
    res = pl.pallas_call(
        body, name=name,
        out_shape=(sems, sems, *[pltpu.HBM(t.shape, t.dtype) for t in bufs], jax.ShapeDtypeStruct((8, 128), F32)),
        in_specs=[_HBM] * (2 * n) + [pl.BlockSpec(memory_space=pl.ANY)],
        out_specs=(_SEM, _SEM, *[_HBM] * (2 * n), pl.BlockSpec(memory_space=pltpu.VMEM)),
        input_output_aliases={i: 2 + i for i in range(2 * n)},
        compiler_params=pltpu.CompilerParams(has_side_effects=_EFFECT),
    )(*bufs, after)
    return res[0], res[1], list(res[2:2 + n]), list(res[2 + n:2 + 2 * n]), res[-1]


def _exchange_wait(started, scatter, after, *, name, peers=ALL_PEERS):
    send_sems, recv_sems, arrays, lands, _ = started
    n = len(arrays)

    def body(*refs):
        ins, lnd = refs[:n], refs[n:2 * n]
        send_sems, recv_sems = refs[2 * n], refs[2 * n + 1]
        pos = _my_pos()
        for a in range(n):
            for r in peers:
                _exchange_copy(ins, lnd, send_sems, recv_sems, scatter, pos, a, r, receiving=False).wait_send()
                _exchange_copy(ins, lnd, send_sems, recv_sems, scatter, pos, a, r, receiving=True).wait_recv()

    bufs = list(arrays) + list(lands)
    res = pl.pallas_call(
        body, name=name,
        out_shape=tuple(pltpu.HBM(t.shape, t.dtype) for t in bufs),
        in_specs=[_HBM] * (2 * n) + [_SEM, _SEM, pl.BlockSpec(memory_space=pl.ANY)],
        out_specs=tuple([_HBM] * (2 * n)),
        input_output_aliases={i: i for i in range(2 * n)},
        compiler_params=pltpu.CompilerParams(has_side_effects=_EFFECT),
    )(*bufs, send_sems, recv_sems, after)
    return list(res[n:])


def _forward_copy(lands, send_sems, recv_sems, pos, a, q, receiving):
    slot = _dev_index(_flip(pos, q ^ 1 if receiving else q))
    idx = a * len(FORWARDED) + FORWARDED.index(q)
    return pltpu.make_async_remote_copy(
        src_ref=lands[a].at[slot], dst_ref=lands[a].at[slot], send_sem=send_sems.at[idx], recv_sem=recv_sems.at[idx],
        device_id=_flip(pos, 1), device_id_type=pl.DeviceIdType.MESH)


def _forward_start(lands, after, *, name):
    n = len(lands)

    def body(*refs):
        lnd = refs[:n]
        send_sems, recv_sems = refs[n + 1], refs[n + 2]
        token = refs[-1]
        pos = _my_pos()
        for a in range(n):
            for q in FORWARDED:
                _forward_copy(lnd, send_sems, recv_sems, pos, a, q, receiving=False).start()
        token[...] = jnp.zeros_like(token)

    sems = pltpu.SemaphoreType.DMA((n * len(FORWARDED),))
    res = pl.pallas_call(
        body, name=name,
        out_shape=(sems, sems, *[pltpu.HBM(t.shape, t.dtype) for t in lands], jax.ShapeDtypeStruct((8, 128), F32)),
        in_specs=[_HBM] * n + [pl.BlockSpec(memory_space=pl.ANY)],
        out_specs=(_SEM, _SEM, *[_HBM] * n, pl.BlockSpec(memory_space=pltpu.VMEM)),
        input_output_aliases={i: 2 + i for i in range(n)},
        compiler_params=pltpu.CompilerParams(has_side_effects=_EFFECT),
    )(*lands, after)
    return res[0], res[1], list(res[2:2 + n]), res[-1]


def _forward_wait(forwarded, after, *, name):
    send_sems, recv_sems, lands, _ = forwarded
    n = len(lands)

    def body(*refs):
        lnd = refs[:n]
        send_sems, recv_sems = refs[n], refs[n + 1]
        pos = _my_pos()
        for a in range(n):
            for q in FORWARDED:
                _forward_copy(lnd, send_sems, recv_sems, pos, a, q, receiving=False).wait_send()
                _forward_copy(lnd, send_sems, recv_sems, pos, a, q, receiving=True).wait_recv()

    res = pl.pallas_call(
        body, name=name,
        out_shape=tuple(pltpu.HBM(t.shape, t.dtype) for t in lands),
        in_specs=[_HBM] * n + [_SEM, _SEM, pl.BlockSpec(memory_space=pl.ANY)],
        out_specs=tuple([_HBM] * n),
        input_output_aliases={i: i for i in range(n)},
        compiler_params=pltpu.CompilerParams(has_side_effects=_EFFECT),
    )(*lands, send_sems, recv_sems, after)
    return list(res)


def _adamw(parts, w, m, v, *, name, tr):
    r, c = w.shape
    assert r % tr == 0

    def body(p_ref, w_ref, m_ref, v_ref, g_ref, d_ref, mo_ref, vo_ref):
        g = p_ref[0].astype(F32)
        for k in range(1, N_DEV):
            g = g + p_ref[k].astype(F32)
        mn = ADAM_B1 * m_ref[...] + (1.0 - ADAM_B1) * g
        vn = ADAM_B2 * v_ref[...] + (1.0 - ADAM_B2) * (g * g)
        m_hat = mn / (1.0 - ADAM_B1 ** ADAM_STEP)
        v_hat = vn / (1.0 - ADAM_B2 ** ADAM_STEP)
        g_ref[...] = g
        d_ref[...] = -ADAM_LR * (m_hat / (jnp.sqrt(v_hat) + ADAM_EPS) + ADAM_WD * w_ref[...])
        mo_ref[...] = mn
        vo_ref[...] = vn

    blk = pl.BlockSpec((tr, c), lambda i: (i, 0))
    return pl.pallas_call(
        body, name=name, grid=(r // tr,),
        in_specs=[pl.BlockSpec((N_DEV, tr, c), lambda i: (0, i, 0)), blk, blk, blk],
        out_specs=[blk] * 4, out_shape=[jax.ShapeDtypeStruct((r, c), F32)] * 4,
        compiler_params=_params(("parallel",)),
    )(parts, w, m, v)


def _pack(parts, lead=()):
    flat = jnp.concatenate([p.reshape(lead + (-1,)) for p in parts], axis=-1)
    n = flat.shape[-1]
    rows = -(-n // (8 * 128)) * 8
    flat = jnp.pad(flat, [(0, 0)] * len(lead) + [(0, rows * 128 - n)])
    return flat.reshape(lead + (rows, 128))


def _unpack(slab, shapes):
    flat = slab.reshape(-1)
    out, off = [], 0
    for shp in shapes:
        size = 1
        for dim in shp:
            size *= dim
        out.append(flat[off:off + size].reshape(shp))
        off += size
    return out


def _to_dev_major(g, axis):
    shp = g.shape
    g = g.reshape(shp[:axis] + (N_DEV, shp[axis] // N_DEV) + shp[axis + 1:])
    return jnp.moveaxis(g, axis, 0)


def _from_dev_major(g, axis):
    g = jnp.moveaxis(g, 0, axis)
    shp = g.shape
    return g.reshape(shp[:axis] + (shp[axis] * shp[axis + 1],) + shp[axis + 2:])


SMALL_SHARDED = ("conv_a", "conv_b", "norm_mix_odd", "pool_scale", "conv_ffn_w")
SMALL_REPLICATED = ("norm_mix_even", "ln_a_g", "ln_a_b", "norm_ffn", "norm_final")
BIG = {"w_in": ("w_in", None, 512), "w_out": ("w_out", None, 128), "w_pool": ("w_pool", None, 128),
       "w_up0": ("w_up", 0, 256), "w_up1": ("w_up", 1, 256),
       "w_down0": ("w_down", 0, 352), "w_down1": ("w_down", 1, 352)}


def kernel(x, norm_mix_even, w_in, conv_a, ln_a_g, ln_a_b, conv_b, w_out, norm_mix_odd, w_pool, pool_scale, norm_ffn, w_up, conv_ffn_w, w_down, norm_final, loss_target, m_norm_mix_even, m_w_in, m_conv_a, m_ln_a_g, m_ln_a_b, m_conv_b, m_w_out, m_norm_mix_odd, m_w_pool, m_pool_scale, m_norm_ffn, m_w_up, m_conv_ffn_w, m_w_down, m_norm_final, v_norm_mix_even, v_w_in, v_conv_a, v_ln_a_g, v_ln_a_b, v_conv_b, v_w_out, v_norm_mix_odd, v_w_pool, v_pool_scale, v_norm_ffn, v_w_up, v_conv_ffn_w, v_w_down, v_norm_final):
    names = ("norm_mix_even", "w_in", "conv_a", "ln_a_g", "ln_a_b", "conv_b", "w_out", "norm_mix_odd", "w_pool",
             "pool_scale", "norm_ffn", "w_up", "conv_ffn_w", "w_down", "norm_final")
    wts = dict(zip(names, (norm_mix_even, w_in, conv_a, ln_a_g, ln_a_b, conv_b, w_out, norm_mix_odd, w_pool,
                           pool_scale, norm_ffn, w_up, conv_ffn_w, w_down, norm_final)))
    mom = dict(zip(names, (m_norm_mix_even, m_w_in, m_conv_a, m_ln_a_g, m_ln_a_b, m_conv_b, m_w_out, m_norm_mix_odd,
                           m_w_pool, m_pool_scale, m_norm_ffn, m_w_up, m_conv_ffn_w, m_w_down, m_norm_final)))
    var = dict(zip(names, (v_norm_mix_even, v_w_in, v_conv_a, v_ln_a_g, v_ln_a_b, v_conv_b, v_w_out, v_norm_mix_odd,
                           v_w_pool, v_pool_scale, v_norm_ffn, v_w_up, v_conv_ffn_w, v_w_down, v_norm_final)))
    d = x.shape[-1]
    n_up = w_up.shape[-1]

    def shard2d(t, key):
        param, layer, _ = BIG[key]
        a = t[param][0] if layer is None else t[param][layer]
        return a.reshape(-1, a.shape[-1])

    small_w = _pack([wts[k] for k in SMALL_SHARDED])
    bf = {k: shard2d(wts, k).astype(BF16) for k in BIG}
    gather_groups = {"mix0": ("w_in", "w_out", "small"), "ffn0": ("w_up0", "w_down0"),
                     "ffn1": ("w_pool", "w_up1", "w_down1")}
    order = list(gather_groups)
    started = {}

    def start_gather(grp, after):
        arrs = [small_w if k == "small" else bf[k] for k in gather_groups[grp]]
        started[grp] = _exchange_start(arrs, [False] * len(arrs), after, name=f"gather_{grp}_start", peers=CHIP_PEERS)

    start_gather(order[0], small_w)

    def weights(grp, after):
        keys = gather_groups[grp]
        lands = _exchange_wait(started[grp], [False] * len(keys), after, name=f"gather_{grp}_wait", peers=CHIP_PEERS)
        forwarded = _forward_start(lands, lands[0], name=f"gather_{grp}_forward")
        if grp != order[-1]:
            start_gather(order[order.index(grp) + 1], forwarded[-1])
            behind = started[order[order.index(grp) + 1]][-1]
        else:
            behind = forwarded[-1]
        gw = dict(zip(keys, _forward_wait(forwarded, behind, name=f"gather_{grp}_forward_wait")))
        if grp == "ffn0":
            return {"w_up": gw["w_up0"], "w_down": gw["w_down0"].reshape(N_PAIR, -1, d)}
        if grp == "ffn1":
            return {"w_up": gw["w_up1"], "w_down": gw["w_down1"].reshape(N_PAIR, -1, d),
                    "w_pool": _from_dev_major(gw["w_pool"].reshape(N_DEV, len(POOL_WINDOWS), -1, POOL_GROUP), 1)}
        per_dev = gw["small"].reshape(N_DEV, -1)
        sizes = [wts[k].size for k in SMALL_SHARDED]
        offs = [sum(sizes[:i]) for i in range(len(sizes))]
        small_full = {k: per_dev[:, o:o + n_].reshape((N_DEV,) + wts[k].shape)
                      for k, o, n_ in zip(SMALL_SHARDED, offs, sizes)}
        return {
            "norm_mix_even": norm_mix_even, "ln_a_g": ln_a_g, "ln_a_b": ln_a_b, "norm_ffn": norm_ffn,
            "norm_final": norm_final[None],
            "w_in": _from_dev_major(gw["w_in"], 1),
            "w_out": gw["w_out"].reshape(-1, d),
            "conv_a": _from_dev_major(small_full["conv_a"][:, 0], 1),
            "conv_b": _from_dev_major(small_full["conv_b"][:, 0], 1),
            "norm_mix_odd": _from_dev_major(small_full["norm_mix_odd"], 1),
            "pool_scale": _from_dev_major(small_full["pool_scale"], 1),
            "conv_ffn": [small_full["conv_ffn_w"][:, l] for l in range(2)],
        }

    def dev_major(k, g):
        if k == "w_in":
            return _to_dev_major(g, 1)
        if k == "w_pool":
            return _to_dev_major(g, 1).reshape(N_DEV, -1, POOL_GROUP)
        return g.reshape(N_DEV, -1, g.shape[-1])

    sent, big_grads = {}, {}

    def grads_ready(grp, grads):
        big_grads.update(grads)
        if grp == "mix0":
            return None
        keys = tuple(grads)
        parts = [dev_major(k, grads[k]) for k in keys]
        sent[grp] = (keys, _exchange_start(parts, [True] * len(keys), parts[0], name=f"grads_{grp}_start"))
        return sent[grp][1][-1]

    loss, dx, g = _local_step(x[0], loss_target[0], weights, grads_ready)

    small_parts = _pack([
        _to_dev_major(g["conv_a"], 1), _to_dev_major(g["conv_b"], 1), _to_dev_major(g["norm_mix_odd"], 1),
        _to_dev_major(g["pool_scale"], 1), jnp.stack(g["conv_ffn"], axis=1)], lead=(N_DEV,))
    repl_parts = _pack([g[k] for k in SMALL_REPLICATED])
    last_keys = ("w_in", "w_out", "small", "replicated")
    last = _exchange_start([dev_major("w_in", big_grads["w_in"]), dev_major("w_out", big_grads["w_out"]),
                            small_parts, repl_parts], [True, True, True, False], dx, name="grads_mix0_start")
    sent["mix0"] = (last_keys, last)

    landed, after = {}, last[-1]
    for grp in ("ffn1", "ffn0", "mix0"):
        keys, st = sent[grp]
        scat = [k != "replicated" for k in keys]
        landed.update(zip(keys, _exchange_wait(st, scat, after, name=f"grads_{grp}_wait")))
    big_out = {}
    for k in BIG:
        big_out[k] = _adamw(landed[k], shard2d(wts, k), shard2d(mom, k), shard2d(var, k), name=f"adamw_{k}",
                            tr=BIG[k][2])
    out = {}
    for k in ("w_in", "w_out", "w_pool"):
        out[k] = [t.reshape(wts[k].shape) for t in big_out[k]]
    for k in ("w_up", "w_down"):
        out[k] = [jnp.stack([a, b]).reshape(wts[k].shape) for a, b in zip(big_out[k + "0"], big_out[k + "1"])]
    res = _adamw(landed["small"], small_w, _pack([mom[k] for k in SMALL_SHARDED]),
                 _pack([var[k] for k in SMALL_SHARDED]), name="adamw_small", tr=small_w.shape[0])
    unpacked = [_unpack(t, [wts[k].shape for k in SMALL_SHARDED]) for t in res]
    for j, k in enumerate(SMALL_SHARDED):
        out[k] = [u[j] for u in unpacked]
    repl_w = _pack([wts[k] for k in SMALL_REPLICATED])
    res = _adamw(landed["replicated"], repl_w, _pack([mom[k] for k in SMALL_REPLICATED]),
                 _pack([var[k] for k in SMALL_REPLICATED]), name="adamw_replicated", tr=repl_w.shape[0])
    unpacked = [_unpack(t, [wts[k].shape for k in SMALL_REPLICATED]) for t in res]
    for j, k in enumerate(SMALL_REPLICATED):
        out[k] = [u[j] for u in unpacked]

    loss = lax.psum(loss, ("x", "y", "c"))
    return (loss, dx[None], *[out[k][0] for k in names], *[out[k][1] for k in names],
            *[out[k][2] for k in names], *[out[k][3] for k in names])
```

```python
import jax
import jax.numpy as jnp
from jax import lax
from jax.experimental import pallas as pl
from jax.experimental.pallas import tpu as pltpu

F32 = jnp.float32
BF16 = jnp.bfloat16

RMS_EPS = 1e-6
LN_EPS = 1e-5
ADAM_LR = 0.001
ADAM_B1 = 0.9
ADAM_B2 = 0.999
ADAM_EPS = 1e-08
ADAM_WD = 0.01
ADAM_STEP = 10

N_DEV = 8
N_PAIR = N_DEV // 2
A_WIDTH = 512
A_TAPS = 31
POOL_WINDOWS = (2, 4, 8, 16)
POOL_GROUP = 256
HALO_A = 32
HALO_S = 16
VMEM_LIMIT = 56 * 1024 * 1024


def _params(sem, vmem=VMEM_LIMIT):
    return pltpu.CompilerParams(dimension_semantics=sem, vmem_limit_bytes=vmem)


def _sigmoid(x):
    return 0.5 * jnp.tanh(0.5 * x) + 0.5


def _prev_blk(i, ts, hb):
    return jnp.maximum(i * (ts // hb) - 1, 0)


def _next_blk(i, ts, hb, s):
    return jnp.minimum((i + 1) * (ts // hb), s // hb - 1)


def _mm(a, b, *, name, ta=False, tb=False, add=None, dep=None, out_dtype=F32, tm=512, tn=512, tk=None):
    m, k = (a.shape[1], a.shape[0]) if ta else a.shape
    n = b.shape[0] if tb else b.shape[1]
    tk = k if tk is None else tk
    tm, tn, tk = min(tm, m), min(tn, n), min(tk, k)
    assert m % tm == 0 and n % tn == 0 and k % tk == 0, (name, m, n, k, tm, tn, tk)
    nk = k // tk
    dims = (((0,) if ta else (1,), (1,) if tb else (0,)), ((), ()))
    n_in = 2 + (add is not None) + (dep is not None)

    def body(*refs):
        a_ref, b_ref = refs[0], refs[1]
        add_ref = refs[2] if add is not None else None
        o_ref = refs[n_in]
        part = lax.dot_general(a_ref[...].astype(BF16), b_ref[...].astype(BF16), dims, preferred_element_type=F32)

        def finish(r):
            if add_ref is not None:
                r = r + add_ref[...]
            o_ref[...] = r.astype(out_dtype)

        if nk == 1:
            finish(part)
            return
        acc_ref = refs[-1]
        kk = pl.program_id(2)

        @pl.when(kk == 0)
        def _():
            acc_ref[...] = part

        @pl.when(kk > 0)
        def _():
            acc_ref[...] += part

        @pl.when(kk == nk - 1)
        def _():
            finish(acc_ref[...])

    a_spec = pl.BlockSpec((tk, tm), lambda i, j, kk: (kk, i)) if ta else pl.BlockSpec((tm, tk), lambda i, j, kk: (i, kk))
    b_spec = pl.BlockSpec((tn, tk), lambda i, j, kk: (j, kk)) if tb else pl.BlockSpec((tk, tn), lambda i, j, kk: (kk, j))
    in_specs = [a_spec, b_spec]
    args = [a, b]
    if add is not None:
        in_specs.append(pl.BlockSpec((tm, tn), lambda i, j, kk: (i, j)))
        args.append(add)
    if dep is not None:
        in_specs.append(pl.BlockSpec(memory_space=pl.ANY))
        args.append(dep)
    return pl.pallas_call(
        body, name=name, grid=(m // tm, n // tn, nk),
        in_specs=in_specs, out_specs=pl.BlockSpec((tm, tn), lambda i, j, kk: (i, j)),
        out_shape=jax.ShapeDtypeStruct((m, n), out_dtype),
        scratch_shapes=[pltpu.VMEM((tm, tn), F32)] if nk > 1 else [],
        compiler_params=_params(("parallel", "parallel", "arbitrary")),
    )(*args)


def _rms_rows(xv, gv):
    return xv * lax.rsqrt(jnp.mean(xv * xv, axis=-1, keepdims=True) + RMS_EPS) * gv


def _rms_mm(x, g, w, *, name, out_dtype, tm=1024, tn=512):
    s, d = x.shape
    n = w.shape[1]
    tm = min(tm, s)
    assert s % tm == 0 and n % tn == 0

    def body(x_ref, g_ref, w_ref, h_ref, z_ref, hs_ref):
        @pl.when(pl.program_id(1) == 0)
        def _():
            hv = _rms_rows(x_ref[...], g_ref[...]).astype(BF16)
            hs_ref[...] = hv
            h_ref[...] = hv

        z_ref[...] = jnp.dot(hs_ref[...], w_ref[...], preferred_element_type=F32).astype(out_dtype)

    return pl.pallas_call(
        body, name=name, grid=(s // tm, n // tn),
        in_specs=[pl.BlockSpec((tm, d), lambda i, j: (i, 0)),
                  pl.BlockSpec((1, d), lambda i, j: (0, 0)),
                  pl.BlockSpec((d, tn), lambda i, j: (0, j))],
        out_specs=[pl.BlockSpec((tm, d), lambda i, j: (i, 0)),
                   pl.BlockSpec((tm, tn), lambda i, j: (i, j))],
        out_shape=[jax.ShapeDtypeStruct((s, d), BF16), jax.ShapeDtypeStruct((s, n), out_dtype)],
        scratch_shapes=[pltpu.VMEM((tm, d), BF16)],
        compiler_params=_params(("parallel", "arbitrary")),
    )(x, g, w)


def _rms_bwd_rows(xv, gv, dh):
    r = lax.rsqrt(jnp.mean(xv * xv, axis=-1, keepdims=True) + RMS_EPS)
    xh = xv * r
    dn = dh * gv
    dx = r * (dn - xh * jnp.mean(dn * xh, axis=-1, keepdims=True))
    return dx, dh * xh


def _rms_bwd(x, g, dh, dres, *, name, ts=256):
    s, d = x.shape
    ts = min(ts, s)

    def body(x_ref, g_ref, dh_ref, dres_ref, dx_ref, dxb_ref, dg_ref):
        @pl.when(pl.program_id(0) == 0)
        def _():
            dg_ref[...] = jnp.zeros_like(dg_ref)

        dx, dgr = _rms_bwd_rows(x_ref[...], g_ref[...], dh_ref[...])
        dx = dres_ref[...] + dx
        dx_ref[...] = dx
        dxb_ref[...] = dx.astype(BF16)
        dg_ref[...] += jnp.sum(dgr, axis=0, keepdims=True)

    row = pl.BlockSpec((ts, d), lambda i: (i, 0))
    vec = pl.BlockSpec((1, d), lambda i: (0, 0))
    return pl.pallas_call(
        body, name=name, grid=(s // ts,),
        in_specs=[row, vec, row, row], out_specs=[row, row, vec],
        out_shape=[jax.ShapeDtypeStruct((s, d), F32), jax.ShapeDtypeStruct((s, d), BF16),
                   jax.ShapeDtypeStruct((1, d), F32)],
        compiler_params=_params(("arbitrary",)),
    )(x, g, dh, dres)


def _final_loss(x, g, target, *, name, ts=256):
    s, d = x.shape
    ts = min(ts, s)

    def body(x_ref, g_ref, t_ref, loss_ref, dx_ref, dxb_ref, dg_ref):
        @pl.when(pl.program_id(0) == 0)
        def _():
            loss_ref[...] = jnp.zeros_like(loss_ref)
            dg_ref[...] = jnp.zeros_like(dg_ref)

        xv = x_ref[...]
        gv = g_ref[...]
        r = lax.rsqrt(jnp.mean(xv * xv, axis=-1, keepdims=True) + RMS_EPS)
        xh = xv * r
        err = xh * gv - t_ref[...]
        loss_ref[...] += 0.5 * jnp.sum(jnp.mean(err * err, axis=-1, keepdims=True), axis=0, keepdims=True)
        dy = err * (1.0 / d)
        dn = dy * gv
        dx = r * (dn - xh * jnp.mean(dn * xh, axis=-1, keepdims=True))
        dx_ref[...] = dx
        dxb_ref[...] = dx.astype(BF16)
        dg_ref[...] += jnp.sum(dy * xh, axis=0, keepdims=True)

    row = pl.BlockSpec((ts, d), lambda i: (i, 0))
    vec = pl.BlockSpec((1, d), lambda i: (0, 0))
    return pl.pallas_call(
        body, name=name, grid=(s // ts,),
        in_specs=[row, vec, row],
        out_specs=[pl.BlockSpec((1, 1), lambda i: (0, 0)), row, row, vec],
        out_shape=[jax.ShapeDtypeStruct((1, 1), F32), jax.ShapeDtypeStruct((s, d), F32),
                   jax.ShapeDtypeStruct((s, d), BF16), jax.ShapeDtypeStruct((1, d), F32)],
        compiler_params=_params(("arbitrary",)),
    )(x, g, target)


def _conv_taps(ext_ref, w_ref, n_taps, base, r0, rows, reverse=False):
    acc = None
    for k in range(n_taps):
        off = r0 + (base - k if reverse else base + k)
        term = w_ref[k:k + 1, :] * ext_ref[pl.ds(off, rows), :]
        acc = term if acc is None else acc + term
    return acc


def _shift_copies(src_ref, sh_ref, rows):
    for b in range(1, 8):
        sh_ref[b, 0:rows, :] = src_ref[pl.ds(b, rows), :]


def _shifted(src_ref, sh_ref, start, rows, off):
    a, b = divmod(off, 8)
    ref = src_ref if b == 0 else sh_ref.at[b]
    return ref[pl.ds(start + 8 * a, rows), :]


def _mix0_fwd(z, conv_a, ln_g, ln_b, conv_b, *, name, ts=256, rc=32):
    s = z.shape[0]
    c = A_WIDTH
    hb = HALO_A

    def body(z_ref, zp_ref, wa_ref, lg_ref, lb_ref, wb_ref, ab_ref, ca_ref, exta, extb, sha):
        keep = jnp.where(pl.program_id(0) > 0, 1.0, 0.0)
        zp = zp_ref[...]
        exta[0:hb, :] = zp[:, 0:c] * _sigmoid(zp[:, c:2 * c]) * keep
        extb[0:hb, :] = zp[:, 3 * c:4 * c] * zp[:, 4 * c:5 * c] * keep
        exta[hb:hb + ts, :] = z_ref[:, 0:c] * _sigmoid(z_ref[:, c:2 * c])
        extb[hb:hb + ts, :] = z_ref[:, 3 * c:4 * c] * z_ref[:, 4 * c:5 * c]
        _shift_copies(exta, sha, hb + ts - 8)
        lg = lg_ref[...]
        lb = lb_ref[...]
        for q in range(ts // rc):
            r0 = q * rc
            ca = None
            for k in range(A_TAPS):
                term = wa_ref[k:k + 1, :] * _shifted(exta, sha, r0, rc, hb - (A_TAPS - 1) + k)
                ca = term if ca is None else ca + term
            ca_ref[r0:r0 + rc, :] = ca
            mu = jnp.mean(ca, axis=-1, keepdims=True)
            xc = ca - mu
            rs = lax.rsqrt(jnp.mean(xc * xc, axis=-1, keepdims=True) + LN_EPS)
            l = xc * rs * lg + lb
            ab_ref[r0:r0 + rc, 0:c] = (l * _sigmoid(l)).astype(BF16)
            cbc = _conv_taps(extb, wb_ref, 3, hb - 2, r0, rc)
            ab_ref[r0:r0 + rc, c:2 * c] = (z_ref[r0:r0 + rc, 2 * c:3 * c] * cbc).astype(BF16)

    return pl.pallas_call(
        body, name=name, grid=(s // ts,),
        in_specs=[pl.BlockSpec((ts, 5 * c), lambda i: (i, 0)),
                  pl.BlockSpec((hb, 5 * c), lambda i: (_prev_blk(i, ts, hb), 0)),
                  pl.BlockSpec((32, c), lambda i: (0, 0)),
                  pl.BlockSpec((1, c), lambda i: (0, 0)),
                  pl.BlockSpec((1, c), lambda i: (0, 0)),
                  pl.BlockSpec((8, c), lambda i: (0, 0))],
        out_specs=[pl.BlockSpec((ts, 2 * c), lambda i: (i, 0)),
                   pl.BlockSpec((ts, c), lambda i: (i, 0))],
        out_shape=[jax.ShapeDtypeStruct((s, 2 * c), BF16), jax.ShapeDtypeStruct((s, c), F32)],
        scratch_shapes=[pltpu.VMEM((hb + ts, c), F32), pltpu.VMEM((hb + ts, c), F32),
                        pltpu.VMEM((8, hb + ts - 8, c), F32)],
        compiler_params=_params(("parallel",)),
    )(z, z, conv_a, ln_g, ln_b, conv_b)


def _mix0_bwd(z, ca, dab, conv_a, ln_g, ln_b, conv_b, *, name, ts=256, rc=32):
    s = z.shape[0]
    c = A_WIDTH
    hb = HALO_A
    ta = A_TAPS

    def body(z_ref, zp_ref, zn_ref, ca_ref, can_ref, d_ref, dn_ref, wa_ref, lg_ref, lb_ref, wb_ref,
             dz_ref, dwa_ref, dwb_ref, dlg_ref, dlb_ref, exta, extb, extdca, extdcb, shd):
        i = pl.program_id(0)
        keep_p = jnp.where(i > 0, 1.0, 0.0)
        keep_n = jnp.where(i < s // ts - 1, 1.0, 0.0)

        @pl.when(i == 0)
        def _():
            dwa_ref[...] = jnp.zeros_like(dwa_ref)
            dwb_ref[...] = jnp.zeros_like(dwb_ref)
            dlg_ref[...] = jnp.zeros_like(dlg_ref)
            dlb_ref[...] = jnp.zeros_like(dlb_ref)

        lg = lg_ref[...]
        lb = lb_ref[...]
        zp = zp_ref[...]
        exta[0:hb, :] = zp[:, 0:c] * _sigmoid(zp[:, c:2 * c]) * keep_p
        extb[0:hb, :] = zp[:, 3 * c:4 * c] * zp[:, 4 * c:5 * c] * keep_p
        exta[hb:hb + ts, :] = z_ref[:, 0:c] * _sigmoid(z_ref[:, c:2 * c])
        extb[hb:hb + ts, :] = z_ref[:, 3 * c:4 * c] * z_ref[:, 4 * c:5 * c]

        def ln_bwd(cav, dav):
            mu = jnp.mean(cav, axis=-1, keepdims=True)
            xc = cav - mu
            rs = lax.rsqrt(jnp.mean(xc * xc, axis=-1, keepdims=True) + LN_EPS)
            nv = xc * rs
            l = nv * lg + lb
            sg = _sigmoid(l)
            dl = dav * (sg * (1.0 + l * (1.0 - sg)))
            dnv = dl * lg
            dca = rs * (dnv - jnp.mean(dnv, axis=-1, keepdims=True)
                        - nv * jnp.mean(dnv * nv, axis=-1, keepdims=True))
            return dca, dl, nv

        dlg_acc = jnp.zeros((1, c), F32)
        dlb_acc = jnp.zeros((1, c), F32)
        for q in range(ts // rc):
            r0 = q * rc
            dca, dl, nv = ln_bwd(ca_ref[r0:r0 + rc, :], d_ref[r0:r0 + rc, 0:c])
            extdca[r0:r0 + rc, :] = dca
            dlg_acc = dlg_acc + jnp.sum(dl * nv, axis=0, keepdims=True)
            dlb_acc = dlb_acc + jnp.sum(dl, axis=0, keepdims=True)
            extdcb[r0:r0 + rc, :] = d_ref[r0:r0 + rc, c:2 * c] * z_ref[r0:r0 + rc, 2 * c:3 * c]
        dca_n, _, _ = ln_bwd(can_ref[...], dn_ref[:, 0:c])
        extdca[ts:ts + hb, :] = dca_n * keep_n
        extdcb[ts:ts + hb, :] = dn_ref[:, c:2 * c] * zn_ref[:, 2 * c:3 * c] * keep_n
        dlg_ref[...] += dlg_acc
        dlb_ref[...] += dlb_acc
        _shift_copies(extdca, shd, ts + hb - 8)

        for q in range(ts // rc):
            r0 = q * rc
            zr = z_ref[r0:r0 + rc, :]
            dga = None
            for k in range(ta):
                term = wa_ref[k:k + 1, :] * _shifted(extdca, shd, r0, rc, ta - 1 - k)
                dga = term if dga is None else dga + term
            sg = _sigmoid(zr[:, c:2 * c])
            dz_ref[r0:r0 + rc, 0:c] = (dga * sg).astype(BF16)
            dz_ref[r0:r0 + rc, c:2 * c] = (dga * zr[:, 0:c] * sg * (1.0 - sg)).astype(BF16)
            cbc = _conv_taps(extb, wb_ref, 3, hb - 2, r0, rc)
            dz_ref[r0:r0 + rc, 2 * c:3 * c] = (d_ref[r0:r0 + rc, c:2 * c] * cbc).astype(BF16)
            dcb = _conv_taps(extdcb, wb_ref, 3, 2, r0, rc, reverse=True)
            dz_ref[r0:r0 + rc, 3 * c:4 * c] = (dcb * zr[:, 4 * c:5 * c]).astype(BF16)
            dz_ref[r0:r0 + rc, 4 * c:5 * c] = (dcb * zr[:, 3 * c:4 * c]).astype(BF16)

        for k in range(ta):
            part = None
            for q in range(ts // rc):
                r0 = q * rc
                p = exta[hb + r0:hb + r0 + rc, :] * _shifted(extdca, shd, r0, rc, ta - 1 - k)
                for r in range(0, rc, 8):
                    part = p[r:r + 8, :] if part is None else part + p[r:r + 8, :]
            dwa_ref[k:k + 1, :] += jnp.sum(part, axis=0, keepdims=True)
        dcb_t = extdcb[0:ts, :]
        for k in range(3):
            dwb_ref[k:k + 1, :] += jnp.sum(dcb_t * extb[pl.ds(hb - 2 + k, ts), :], axis=0, keepdims=True)

    def tile(w):
        return pl.BlockSpec((ts, w), lambda i: (i, 0))

    def prev(w):
        return pl.BlockSpec((hb, w), lambda i: (_prev_blk(i, ts, hb), 0))

    def nxt(w):
        return pl.BlockSpec((hb, w), lambda i: (_next_blk(i, ts, hb, s), 0))

    def const(r, w):
        return pl.BlockSpec((r, w), lambda i: (0, 0))

    return pl.pallas_call(
        body, name=name, grid=(s // ts,),
        in_specs=[tile(5 * c), prev(5 * c), nxt(5 * c), tile(c), nxt(c), tile(2 * c), nxt(2 * c),
                  const(32, c), const(1, c), const(1, c), const(8, c)],
        out_specs=[tile(5 * c), const(32, c), const(8, c), const(1, c), const(1, c)],
        out_shape=[jax.ShapeDtypeStruct((s, 5 * c), BF16), jax.ShapeDtypeStruct((32, c), F32),
                   jax.ShapeDtypeStruct((8, c), F32), jax.ShapeDtypeStruct((1, c), F32),
                   jax.ShapeDtypeStruct((1, c), F32)],
        scratch_shapes=[pltpu.VMEM((hb + ts, c), F32), pltpu.VMEM((hb + ts, c), F32),
                        pltpu.VMEM((ts + hb, c), F32), pltpu.VMEM((ts + hb, c), F32),
                        pltpu.VMEM((8, ts + hb - 8, c), F32)],
        compiler_params=_params(("arbitrary",)),
    )(z, z, z, ca, ca, dab, dab, conv_a, ln_g, ln_b, conv_b)


def _ffn_up(x, g, w8, *, name, tm=1024):
    s, d = x.shape
    nb, _, c = w8.shape
    tm = min(tm, s)

    def body(x_ref, g_ref, w_ref, h_ref, u_ref, hs_ref):
        @pl.when(pl.program_id(1) == 0)
        def _():
            hv = _rms_rows(x_ref[...], g_ref[...]).astype(BF16)
            hs_ref[...] = hv
            h_ref[...] = hv

        u_ref[...] = jnp.dot(hs_ref[...], w_ref[...], preferred_element_type=F32).astype(BF16)

    return pl.pallas_call(
        body, name=name, grid=(s // tm, nb),
        in_specs=[pl.BlockSpec((tm, d), lambda i, k: (i, 0)),
                  pl.BlockSpec((1, d), lambda i, k: (0, 0)),
                  pl.BlockSpec((None, d, c), lambda i, k: (k, 0, 0))],
        out_specs=[pl.BlockSpec((tm, d), lambda i, k: (i, 0)),
                   pl.BlockSpec((None, tm, c), lambda i, k: (k, i, 0))],
        out_shape=[jax.ShapeDtypeStruct((s, d), BF16), jax.ShapeDtypeStruct((nb, s, c), BF16)],
        scratch_shapes=[pltpu.VMEM((tm, d), BF16)],
        compiler_params=_params(("parallel", "arbitrary")),
    )(x, g, w8)


def _ffn_mid(up8, wc8, *, name, ts=512, rc=32):
    nb, s, c = up8.shape
    hb = HALO_S
    ts = min(ts, s)

    def body(g_ref, gp_ref, v_ref, vp_ref, wg_ref, wv_ref, act_ref, ug_ref, uv_ref, extg, extv):
        keep = jnp.where(pl.program_id(0) > 0, 1.0, 0.0)
        extg[0:hb, :] = gp_ref[...].astype(F32) * keep
        extv[0:hb, :] = vp_ref[...].astype(F32) * keep
        extg[hb:hb + ts, :] = g_ref[...].astype(F32)
        extv[hb:hb + ts, :] = v_ref[...].astype(F32)
        for q in range(ts // rc):
            r0 = q * rc
            gg = _conv_taps(extg, wg_ref, 3, hb - 2, r0, rc)
            vv = _conv_taps(extv, wv_ref, 3, hb - 2, r0, rc)
            ug_ref[r0:r0 + rc, :] = gg.astype(BF16)
            uv_ref[r0:r0 + rc, :] = vv.astype(BF16)
            act_ref[r0:r0 + rc, :] = (gg * _sigmoid(gg) * vv).astype(BF16)

    def blk(rows, off, halo):
        if halo:
            return pl.BlockSpec((None, rows, c), lambda i, j: (j + off, _prev_blk(i, ts, hb), 0))
        return pl.BlockSpec((None, rows, c), lambda i, j: (j + off, i, 0))

    def taps(off):
        return pl.BlockSpec((None, 8, c), lambda i, j: (j + off, 0, 0))

    tile = pl.BlockSpec((None, ts, c), lambda i, j: (j, i, 0))
    out = jax.ShapeDtypeStruct((N_PAIR, s, c), BF16)
    return pl.pallas_call(
        body, name=name, grid=(s // ts, N_PAIR),
        in_specs=[blk(ts, 0, False), blk(hb, 0, True), blk(ts, N_PAIR, False), blk(hb, N_PAIR, True),
                  taps(0), taps(N_PAIR)],
        out_specs=[tile, tile, tile], out_shape=[out, out, out],
        scratch_shapes=[pltpu.VMEM((hb + ts, c), F32), pltpu.VMEM((hb + ts, c), F32)],
        compiler_params=_params(("parallel", "parallel")),
    )(up8, up8, up8, up8, wc8, wc8)


def _ffn_down(act4, wd4, x, *, name, tm=512):
    npair, s, c = act4.shape
    d = x.shape[1]
    tm = min(tm, s)

    def body(a_ref, w_ref, x_ref, o_ref):
        acc = x_ref[...]
        for j in range(npair):
            acc = acc + jnp.dot(a_ref[j], w_ref[j], preferred_element_type=F32)
        o_ref[...] = acc

    return pl.pallas_call(
        body, name=name, grid=(s // tm,),
        in_specs=[pl.BlockSpec((npair, tm, c), lambda i: (0, i, 0)),
                  pl.BlockSpec((npair, c, d), lambda i: (0, 0, 0)),
                  pl.BlockSpec((tm, d), lambda i: (i, 0))],
        out_specs=pl.BlockSpec((tm, d), lambda i: (i, 0)),
        out_shape=jax.ShapeDtypeStruct((s, d), F32),
        compiler_params=_params(("parallel",)),
    )(act4, wd4, x)


def _ffn_dact(db, wd4, *, name, tm=1024):
    s, d = db.shape
    npair, c, _ = wd4.shape
    tm = min(tm, s)

    def body(d_ref, w_ref, o_ref):
        o_ref[...] = lax.dot_general(d_ref[...], w_ref[...], (((1,), (1,)), ((), ())),
                                     preferred_element_type=F32).astype(BF16)

    return pl.pallas_call(
        body, name=name, grid=(s // tm, npair),
        in_specs=[pl.BlockSpec((tm, d), lambda i, j: (i, 0)),
                  pl.BlockSpec((None, c, d), lambda i, j: (j, 0, 0))],
        out_specs=pl.BlockSpec((None, tm, c), lambda i, j: (j, i, 0)),
        out_shape=jax.ShapeDtypeStruct((npair, s, c), BF16),
        compiler_params=_params(("parallel", "parallel")),
    )(db, wd4)


def _ffn_dwdown(act4, db, *, name):
    npair, s, c = act4.shape
    d = db.shape[1]

    def body(a_ref, d_ref, o_ref):
        o_ref[...] = lax.dot_general(a_ref[...], d_ref[...], (((0,), (0,)), ((), ())),
                                     preferred_element_type=F32).astype(BF16)

    return pl.pallas_call(
        body, name=name, grid=(npair,),
        in_specs=[pl.BlockSpec((None, s, c), lambda j: (j, 0, 0)),
                  pl.BlockSpec((s, d), lambda j: (0, 0))],
        out_specs=pl.BlockSpec((None, c, d), lambda j: (j, 0, 0)),
        out_shape=jax.ShapeDtypeStruct((npair, c, d), BF16),
        compiler_params=_params(("parallel",)),
    )(act4, db)


def _ffn_midbwd(up8, ug4, uv4, dact4, wc8, *, name, ts=512, rc=32):
    nb, s, c = up8.shape
    hb = HALO_S
    ts = min(ts, s)
    n_i = s // ts

    def body(pg_ref, pv_ref, ug_ref, ugn_ref, uv_ref, uvn_ref, d_ref, dn_ref, wg_ref, wv_ref,
             dg_ref, dv_ref, dwg_ref, dwv_ref, extdg, extdv):
        i = pl.program_id(1)
        keep_n = jnp.where(i < n_i - 1, 1.0, 0.0)

        @pl.when(i == 0)
        def _():
            dwg_ref[...] = jnp.zeros_like(dwg_ref)
            dwv_ref[...] = jnp.zeros_like(dwv_ref)

        def du_rows(r0, rows, gg, vv, dav):
            sg = _sigmoid(gg)
            extdg[r0:r0 + rows, :] = dav * vv * (sg * (1.0 + gg * (1.0 - sg)))
            extdv[r0:r0 + rows, :] = dav * (gg * sg)

        for q in range(ts // rc):
            rows = slice(q * rc, q * rc + rc)
            du_rows(q * rc, rc, ug_ref[rows, :].astype(F32), uv_ref[rows, :].astype(F32),
                    d_ref[rows, :].astype(F32))
        du_rows(ts, 8, ugn_ref[0:8, :].astype(F32), uvn_ref[0:8, :].astype(F32),
                dn_ref[0:8, :].astype(F32) * keep_n)

        def fold(p):
            acc = p[0:8, :]
            for r in range(8, rc, 8):
                acc = acc + p[r:r + 8, :]
            return acc

        for extd, p_ref, w_ref, out_ref, dw_ref in ((extdg, pg_ref, wg_ref, dg_ref, dwg_ref),
                                                    (extdv, pv_ref, wv_ref, dv_ref, dwv_ref)):
            part = [None, None, None]
            for q in range(ts // rc):
                r0 = q * rc
                pre = p_ref[r0:r0 + rc, :].astype(F32)
                dup = None
                for k in range(3):
                    sh = extd[pl.ds(r0 + 2 - k, rc), :]
                    term = w_ref[k:k + 1, :] * sh
                    dup = term if dup is None else dup + term
                    prod = fold(pre * sh)
                    part[k] = prod if part[k] is None else part[k] + prod
                out_ref[r0:r0 + rc, :] = dup.astype(BF16)
            for k in range(3):
                dw_ref[k:k + 1, :] += jnp.sum(part[k], axis=0, keepdims=True)

    def blk(off, nxt):
        if nxt:
            return pl.BlockSpec((None, hb, c), lambda j, i: (j + off, _next_blk(i, ts, hb, s), 0))
        return pl.BlockSpec((None, ts, c), lambda j, i: (j + off, i, 0))

    def taps(off):
        return pl.BlockSpec((None, 8, c), lambda j, i: (j + off, 0, 0))

    tile = blk(0, False)
    acc = pl.BlockSpec((None, 8, c), lambda j, i: (j, 0, 0))
    return pl.pallas_call(
        body, name=name, grid=(N_PAIR, n_i),
        in_specs=[tile, blk(N_PAIR, False), tile, blk(0, True), tile, blk(0, True), tile, blk(0, True),
                  taps(0), taps(N_PAIR)],
        out_specs=[tile, tile, acc, acc],
        out_shape=[jax.ShapeDtypeStruct((N_PAIR, s, c), BF16), jax.ShapeDtypeStruct((N_PAIR, s, c), BF16),
                   jax.ShapeDtypeStruct((N_PAIR, 8, c), F32), jax.ShapeDtypeStruct((N_PAIR, 8, c), F32)],
        scratch_shapes=[pltpu.VMEM((ts + 8, c), F32), pltpu.VMEM((ts + 8, c), F32)],
        compiler_params=_params(("parallel", "arbitrary")),
    )(up8, up8, ug4, ug4, uv4, uv4, dact4, dact4, wc8, wc8)


def _ffn_dh(dupg, dupv, w8, x, g, dres, *, name, tm=256):
    npair, s, c = dupg.shape
    d = x.shape[1]
    tm = min(tm, s)

    def body(dg_ref, dv_ref, w_ref, x_ref, g_ref, dres_ref, dx_ref, dxb_ref, dgain_ref):
        @pl.when(pl.program_id(0) == 0)
        def _():
            dgain_ref[...] = jnp.zeros_like(dgain_ref)

        nt = (((1,), (1,)), ((), ()))
        dh = None
        for j in range(npair):
            for src, k in ((dg_ref, j), (dv_ref, j + npair)):
                part = lax.dot_general(src[j], w_ref[k], nt, preferred_element_type=F32)
                dh = part if dh is None else dh + part
        dx, dgr = _rms_bwd_rows(x_ref[...], g_ref[...], dh)
        dx = dres_ref[...] + dx
        dx_ref[...] = dx
        dxb_ref[...] = dx.astype(BF16)
        dgain_ref[...] += jnp.sum(dgr, axis=0, keepdims=True)

    row = pl.BlockSpec((tm, d), lambda i: (i, 0))
    vec = pl.BlockSpec((1, d), lambda i: (0, 0))
    dup = pl.BlockSpec((npair, tm, c), lambda i: (0, i, 0))
    return pl.pallas_call(
        body, name=name, grid=(s // tm,),
        in_specs=[dup, dup, pl.BlockSpec((2 * npair, d, c), lambda i: (0, 0, 0)), row, vec, row],
        out_specs=[row, row, vec],
        out_shape=[jax.ShapeDtypeStruct((s, d), F32), jax.ShapeDtypeStruct((s, d), BF16),
                   jax.ShapeDtypeStruct((1, d), F32)],
        compiler_params=_params(("arbitrary",)),
    )(dupg, dupv, w8, x, g, dres)


def _ffn_dwup(h, dupg, dupv, *, name, tm=512):
    npair, s, c = dupg.shape
    d = h.shape[1]

    def body(h_ref, dg_ref, dv_ref, o_ref):
        tn = (((0,), (0,)), ((), ()))
        k = pl.program_id(1)

        @pl.when(k < npair)
        def _():
            o_ref[...] = lax.dot_general(h_ref[...], dg_ref[...], tn, preferred_element_type=F32).astype(BF16)

        @pl.when(k >= npair)
        def _():
            o_ref[...] = lax.dot_general(h_ref[...], dv_ref[...], tn, preferred_element_type=F32).astype(BF16)

    return pl.pallas_call(
        body, name=name, grid=(d // tm, 2 * npair),
        in_specs=[pl.BlockSpec((s, tm), lambda m, k: (0, m)),
                  pl.BlockSpec((None, s, c), lambda m, k: (jnp.minimum(k, npair - 1), 0, 0)),
                  pl.BlockSpec((None, s, c), lambda m, k: (jnp.maximum(k - npair, 0), 0, 0))],
        out_specs=pl.BlockSpec((None, tm, c), lambda m, k: (k, m, 0)),
        out_shape=jax.ShapeDtypeStruct((2 * npair, d, c), BF16),
        compiler_params=_params(("parallel", "arbitrary")),
    )(h, dupg, dupv)


def _pool_counts(i, ts, rows, window):
    t = lax.broadcasted_iota(jnp.int32, (rows, 1), 0) + i * ts + 1
    return jnp.minimum(t, window).astype(F32)


def _pool_fwd(x, g, *, name, ts=256):
    s, d = x.shape
    hb = HALO_S
    pg = POOL_GROUP
    ts = min(ts, s)

    def body(x_ref, xp_ref, g_ref, p_ref, ext):
        i = pl.program_id(0)
        keep = jnp.where(i > 0, 1.0, 0.0)
        gv = g_ref[...]
        ext[0:hb, :] = _rms_rows(xp_ref[...], gv) * keep
        ext[hb:hb + ts, :] = _rms_rows(x_ref[...], gv)
        for gi, w in enumerate(POOL_WINDOWS):
            cols = slice(gi * pg, (gi + 1) * pg)
            h = ext[hb:hb + ts, cols]
            acc = h
            for q in range(1, w):
                acc = acc + ext[pl.ds(hb - q, ts), cols]
            p_ref[:, cols] = (acc / _pool_counts(i, ts, ts, w) - h).astype(BF16)

    return pl.pallas_call(
        body, name=name, grid=(s // ts,),
        in_specs=[pl.BlockSpec((ts, d), lambda i: (i, 0)),
                  pl.BlockSpec((hb, d), lambda i: (_prev_blk(i, ts, hb), 0)),
                  pl.BlockSpec((1, d), lambda i: (0, 0))],
        out_specs=pl.BlockSpec((ts, d), lambda i: (i, 0)),
        out_shape=jax.ShapeDtypeStruct((s, d), BF16),
        scratch_shapes=[pltpu.VMEM((hb + ts, d), F32)],
        compiler_params=_params(("parallel",)),
    )(x, x, g)


def _pool_mm_fwd(p, w_pool, scale, x, *, name, ts=1024):
    s, d = x.shape
    pg = POOL_GROUP
    ts = min(ts, s)

    def body(p_ref, w_ref, s_ref, x_ref, o_ref, yu_ref):
        yu = jnp.dot(p_ref[...], w_ref[...], preferred_element_type=F32)
        yu_ref[...] = yu.astype(BF16)
        o_ref[...] = x_ref[...] + yu * s_ref[...]

    blk = pl.BlockSpec((ts, pg), lambda i, gi: (i, gi))
    return pl.pallas_call(
        body, name=name, grid=(s // ts, d // pg),
        in_specs=[blk, pl.BlockSpec((None, pg, pg), lambda i, gi: (gi, 0, 0)),
                  pl.BlockSpec((1, pg), lambda i, gi: (0, gi)), blk],
        out_specs=[blk, blk],
        out_shape=[jax.ShapeDtypeStruct((s, d), F32), jax.ShapeDtypeStruct((s, d), BF16)],
        compiler_params=_params(("parallel", "parallel")),
    )(p, w_pool, scale, x)


def _pool_mm_bwd(dres, w_pool, scale, yu, dep, *, name, ts=1024):
    s, d = dres.shape
    pg = POOL_GROUP
    ts = min(ts, s)

    def body(d_ref, w_ref, s_ref, yu_ref, dep_ref, dyc_ref, dp_ref, ds_ref):
        @pl.when(pl.program_id(1) == 0)
        def _():
            ds_ref[...] = jnp.zeros_like(ds_ref)

        dv = d_ref[...]
        dyc = (dv * s_ref[...]).astype(BF16)
        dyc_ref[...] = dyc
        dp_ref[...] = lax.dot_general(dyc, w_ref[...], (((1,), (1,)), ((), ())), preferred_element_type=F32)
        ds_ref[...] += jnp.sum(dv * yu_ref[...].astype(F32), axis=0, keepdims=True)

    blk = pl.BlockSpec((ts, pg), lambda gi, i: (i, gi))
    vec = pl.BlockSpec((1, pg), lambda gi, i: (0, gi))
    return pl.pallas_call(
        body, name=name, grid=(d // pg, s // ts),
        in_specs=[blk, pl.BlockSpec((None, pg, pg), lambda gi, i: (gi, 0, 0)), vec, blk,
                  pl.BlockSpec(memory_space=pl.ANY)],
        out_specs=[blk, blk, vec],
        out_shape=[jax.ShapeDtypeStruct((s, d), BF16), jax.ShapeDtypeStruct((s, d), F32),
                   jax.ShapeDtypeStruct((1, d), F32)],
        compiler_params=_params(("parallel", "arbitrary")),
    )(dres, w_pool, scale, yu, dep)


def _pool_dw(p, dyc, *, name):
    s, d = p.shape
    pg = POOL_GROUP

    def body(p_ref, d_ref, o_ref):
        o_ref[...] = lax.dot_general(p_ref[...], d_ref[...], (((0,), (0,)), ((), ())),
                                     preferred_element_type=F32).astype(BF16)

    blk = pl.BlockSpec((s, pg), lambda gi: (0, gi))
    return pl.pallas_call(
        body, name=name, grid=(d // pg,),
        in_specs=[blk, blk], out_specs=pl.BlockSpec((None, pg, pg), lambda gi: (gi, 0, 0)),
        out_shape=jax.ShapeDtypeStruct((d // pg, pg, pg), BF16),
        compiler_params=_params(("parallel",)),
    )(p, dyc)


def _pool_bwd(dp, x, g, dres, *, name, ts=256):
    s, d = x.shape
    hb = HALO_S
    pg = POOL_GROUP
    ts = min(ts, s)
    n_i = s // ts

    def body(dp_ref, dpn_ref, x_ref, g_ref, dres_ref, dx_ref, dxb_ref, dg_ref, ext, dh):
        i = pl.program_id(0)
        keep_n = jnp.where(i < n_i - 1, 1.0, 0.0)

        @pl.when(i == 0)
        def _():
            dg_ref[...] = jnp.zeros_like(dg_ref)

        for gi, w in enumerate(POOL_WINDOWS):
            cols = slice(gi * pg, (gi + 1) * pg)
            ext[0:ts, cols] = dp_ref[:, cols] / _pool_counts(i, ts, ts, w)
            ext[ts:ts + hb, cols] = dpn_ref[:, cols] / _pool_counts(i + 1, ts, hb, w) * keep_n
        for gi, w in enumerate(POOL_WINDOWS):
            cols = slice(gi * pg, (gi + 1) * pg)
            acc = ext[0:ts, cols]
            for q in range(1, w):
                acc = acc + ext[pl.ds(q, ts), cols]
            dh[:, cols] = acc - dp_ref[:, cols]
        dx, dgr = _rms_bwd_rows(x_ref[...], g_ref[...], dh[...])
        dx = dres_ref[...] + dx
        dx_ref[...] = dx
        dxb_ref[...] = dx.astype(BF16)
        dg_ref[...] += jnp.sum(dgr, axis=0, keepdims=True)

    row = pl.BlockSpec((ts, d), lambda i: (i, 0))
    vec = pl.BlockSpec((1, d), lambda i: (0, 0))
    return pl.pallas_call(
        body, name=name, grid=(n_i,),
        in_specs=[row, pl.BlockSpec((hb, d), lambda i: (_next_blk(i, ts, hb, s), 0)), row, vec, row],
        out_specs=[row, row, vec],
        out_shape=[jax.ShapeDtypeStruct((s, d), F32), jax.ShapeDtypeStruct((s, d), BF16),
                   jax.ShapeDtypeStruct((1, d), F32)],
        scratch_shapes=[pltpu.VMEM((ts + hb, d), F32), pltpu.VMEM((ts, d), F32)],
        compiler_params=_params(("arbitrary",)),
    )(dp, dp, x, g, dres)


def _pad_rows(w, rows):
    pad = [(0, 0)] * (w.ndim - 2) + [(0, rows - w.shape[-2]), (0, 0)]
    return jnp.pad(w, pad)


def _ffn_layer_fwd(x, nf, w8, wc8, wd4, tag):
    h, up8 = _ffn_up(x, nf, w8, name=f"ffn{tag}_up")
    act4, ug4, uv4 = _ffn_mid(up8, wc8, name=f"ffn{tag}_mid")
    x_out = _ffn_down(act4, wd4, x, name=f"ffn{tag}_down")
    return x_out, (h, up8, ug4, uv4, act4)


def _ffn_layer_bwd(d, db, x, nf, w8, wc8, wd4, saved, tag):
    h, up8, ug4, uv4, act4 = saved
    dact4 = _ffn_dact(db, wd4, name=f"ffn{tag}_dact")
    dwd4 = _ffn_dwdown(act4, db, name=f"ffn{tag}_dwdown")
    dupg, dupv, dwg, dwv = _ffn_midbwd(up8, ug4, uv4, dact4, wc8, name=f"ffn{tag}_midbwd")
    dx, dxb, dnf = _ffn_dh(dupg, dupv, w8, x, nf, d, name=f"ffn{tag}_dh")
    dw8 = _ffn_dwup(h, dupg, dupv, name=f"ffn{tag}_dwup")
    dwc8 = jnp.concatenate([dwg, dwv], axis=0)[:, :3]
    return dx, dxb, dnf, dw8, dwc8, dwd4


def _local_step(x, target, weights, grads_ready):
    w0 = weights("mix0", x)
    wa = _pad_rows(w0["conv_a"], 32)
    wb = _pad_rows(w0["conv_b"], 8)
    wc = [_pad_rows(w0["conv_ffn"][l], 8) for l in range(2)]
    h0, z = _rms_mm(x, w0["norm_mix_even"], w0["w_in"], name="mix0_in", out_dtype=F32, tn=1280)
    ab, ca = _mix0_fwd(z, wa, w0["ln_a_g"], w0["ln_a_b"], wb, name="mix0_mid")
    x1 = _mm(ab, w0["w_out"], add=x, name="mix0_out", tm=1024, tn=1024)
    w1 = weights("ffn0", x1)
    x2, ffn0 = _ffn_layer_fwd(x1, w0["norm_ffn"][0:1], w1["w_up"], wc[0], w1["w_down"], 0)
    w2 = weights("ffn1", x2)
    p = _pool_fwd(x2, w0["norm_mix_odd"], name="pool_mid")
    x3, yu = _pool_mm_fwd(p, w2["w_pool"], w0["pool_scale"], x2, name="pool_mm")
    x4, ffn1 = _ffn_layer_fwd(x3, w0["norm_ffn"][1:2], w2["w_up"], wc[1], w2["w_down"], 1)
    loss, d4, d4b, g_norm_final = _final_loss(x4, w0["norm_final"], target, name="final_loss")

    d3, d3b, g_nf1, g_up1, g_wc1, g_down1 = _ffn_layer_bwd(
        d4, d4b, x3, w0["norm_ffn"][1:2], w2["w_up"], wc[1], w2["w_down"], ffn1, 1)
    sent1 = grads_ready("ffn1", {"w_up1": g_up1, "w_down1": g_down1})
    dyc, dp, g_scale = _pool_mm_bwd(d3, w2["w_pool"], w0["pool_scale"], yu, sent1, name="pool_mm_bwd")
    g_pool = _pool_dw(p, dyc, name="pool_dw")
    d2, d2b, g_nmo = _pool_bwd(dp, x2, w0["norm_mix_odd"], d3, name="pool_midbwd")
    d1, d1b, g_nf0, g_up0, g_wc0, g_down0 = _ffn_layer_bwd(
        d2, d2b, x1, w0["norm_ffn"][0:1], w1["w_up"], wc[0], w1["w_down"], ffn0, 0)
    sent0 = grads_ready("ffn0", {"w_pool": g_pool, "w_up0": g_up0, "w_down0": g_down0})
    dab = _mm(d1b, w0["w_out"], tb=True, dep=sent0, name="mix0_dab", tm=1024, tn=1024)
    g_out = _mm(ab, d1b, ta=True, out_dtype=BF16, name="mix0_dwout", tm=1024, tn=512)
    dz, g_wa, g_wb, g_lg, g_lb = _mix0_bwd(z, ca, dab, wa, w0["ln_a_g"], w0["ln_a_b"], wb, name="mix0_midbwd")
    dh0 = _mm(dz, w0["w_in"], tb=True, name="mix0_dh", tm=1024, tn=1024)
    g_in = _mm(h0, dz, ta=True, out_dtype=BF16, name="mix0_dwin", tm=1024, tn=512)
    dx, _, g_nme = _rms_bwd(x, w0["norm_mix_even"], dh0, d1, name="mix0_rmsbwd")
    grads_ready("mix0", {"w_in": g_in, "w_out": g_out})

    small = {
        "norm_mix_even": g_nme, "conv_a": g_wa[:A_TAPS], "ln_a_g": g_lg, "ln_a_b": g_lb, "conv_b": g_wb[:3],
        "norm_mix_odd": g_nmo, "pool_scale": g_scale, "norm_ffn": jnp.concatenate([g_nf0, g_nf1], axis=0),
        "conv_ffn": [g_wc0, g_wc1], "norm_final": g_norm_final,
    }
    return loss[0, 0], dx, small


def _my_pos():
    return lax.axis_index("x"), lax.axis_index("y"), lax.axis_index("c")


def _flip(pos, r):
    x, y, c = pos
    return (1 - x if r & 4 else x, 1 - y if r & 2 else y, 1 - c if r & 1 else c)


def _dev_index(pos):
    return 4 * pos[0] + 2 * pos[1] + pos[2]


_HBM = pl.BlockSpec(memory_space=pltpu.HBM)
_SEM = pl.BlockSpec(memory_space=pltpu.SEMAPHORE)
_EFFECT = pltpu.SideEffectType.DATAFLOW_SIDE_EFFECTING


def _exchange_copy(ins, lands, send_sems, recv_sems, scatter, pos, a, r, receiving):
    me = _dev_index(pos)
    peer = _flip(pos, r)
    dest = _dev_index(pos) if receiving else _dev_index(peer)
    src = ins[a].at[dest] if scatter[a] else ins[a]
    slot = _dev_index(peer) if receiving else me
    return pltpu.make_async_remote_copy(
        src_ref=src, dst_ref=lands[a].at[slot], send_sem=send_sems.at[a * (N_DEV - 1) + r - 1],
        recv_sem=recv_sems.at[a * (N_DEV - 1) + r - 1],
        device_id=peer, device_id_type=pl.DeviceIdType.MESH)


ALL_PEERS = tuple(range(1, N_DEV))
CHIP_PEERS = (1, 2, 4, 6)
FORWARDED = (2, 4, 6)


def _exchange_start(arrays, scatter, after, *, name, peers=ALL_PEERS):
    n = len(arrays)
    me = _dev_index(_my_pos())
    lands = []
    for arr, sc in zip(arrays, scatter):
        own = lax.dynamic_index_in_dim(arr, me, 0, keepdims=True) if sc else arr[None]
        shape = arr.shape if sc else (N_DEV,) + arr.shape
        lands.append(lax.dynamic_update_slice(lax.empty(shape, arr.dtype), own, (me,) + (0,) * (len(shape) - 1)))

    def body(*refs):
        ins, lnd = refs[:n], refs[n:2 * n]
        send_sems, recv_sems = refs[2 * n + 1], refs[2 * n + 2]
        token = refs[-1]
        pos = _my_pos()
        for a in range(n):
            for r in peers:
                _exchange_copy(ins, lnd, send_sems, recv_sems, scatter, pos, a, r, receiving=False).start()
        token[...] = jnp.zeros_like(token)

    bufs = [pltpu.with_memory_space_constraint(t, pltpu.HBM) for t in list(arrays) + lands]
    sems = pltpu.SemaphoreType.DMA((n * (N_DEV - 1),))
    res = pl.pallas_call(
        body, name=name,
        out_shape=(sems, sems, *[pltpu.HBM(t.shape, t.dtype) for t in bufs], jax.ShapeDtypeStruct((8, 128), F32)),
        in_specs=[_HBM] * (2 * n) + [pl.BlockSpec(memory_space=pl.ANY)],
        out_specs=(_SEM, _SEM, *[_HBM] * (2 * n), pl.BlockSpec(memory_space=pltpu.VMEM)),
        input_output_aliases={i: 2 + i for i in range(2 * n)},
        compiler_params=pltpu.CompilerParams(has_side_effects=_EFFECT),
    )(*bufs, after)
    return res[0], res[1], list(res[2:2 + n]), list(res[2 + n:2 + 2 * n]), res[-1]


def _exchange_wait(started, scatter, after, *, name, peers=ALL_PEERS):
    send_sems, recv_sems, arrays, lands, _ = started
    n = len(arrays)

    def body(*refs):
        ins, lnd = refs[:n], refs[n:2 * n]
        send_sems, recv_sems = refs[2 * n], refs[2 * n + 1]
        pos = _my_pos()
        for a in range(n):
            for r in peers:
                _exchange_copy(ins, lnd, send_sems, recv_sems, scatter, pos, a, r, receiving=False).wait_send()
                _exchange_copy(ins, lnd, send_sems, recv_sems, scatter, pos, a, r, receiving=True).wait_recv()

    bufs = list(arrays) + list(lands)
    res = pl.pallas_call(
        body, name=name,
        out_shape=tuple(pltpu.HBM(t.shape, t.dtype) for t in bufs),
        in_specs=[_HBM] * (2 * n) + [_SEM, _SEM, pl.BlockSpec(memory_space=pl.ANY)],
        out_specs=tuple([_HBM] * (2 * n)),
        input_output_aliases={i: i for i in range(2 * n)},
        compiler_params=pltpu.CompilerParams(has_side_effects=_EFFECT),
    )(*bufs, send_sems, recv_sems, after)
    return list(res[n:])


def _forward_copy(lands, send_sems, recv_sems, pos, a, q, receiving):
    slot = _dev_index(_flip(pos, q ^ 1 if receiving else q))
    idx = a * len(FORWARDED) + FORWARDED.index(q)
    return pltpu.make_async_remote_copy(
        src_ref=lands[a].at[slot], dst_ref=lands[a].at[slot], send_sem=send_sems.at[idx], recv_sem=recv_sems.at[idx],
        device_id=_flip(pos, 1), device_id_type=pl.DeviceIdType.MESH)


def _forward_start(lands, after, *, name):
    n = len(lands)

    def body(*refs):
        lnd = refs[:n]
        send_sems, recv_sems = refs[n + 1], refs[n + 2]
        token = refs[-1]
        pos = _my_pos()
        for a in range(n):
            for q in FORWARDED:
                _forward_copy(lnd, send_sems, recv_sems, pos, a, q, receiving=False).start()
        token[...] = jnp.zeros_like(token)

    sems = pltpu.SemaphoreType.DMA((n * len(FORWARDED),))
    res = pl.pallas_call(
        body, name=name,
        out_shape=(sems, sems, *[pltpu.HBM(t.shape, t.dtype) for t in lands], jax.ShapeDtypeStruct((8, 128), F32)),
        in_specs=[_HBM] * n + [pl.BlockSpec(memory_space=pl.ANY)],
        out_specs=(_SEM, _SEM, *[_HBM] * n, pl.BlockSpec(memory_space=pltpu.VMEM)),
        input_output_aliases={i: 2 + i for i in range(n)},
        compiler_params=pltpu.CompilerParams(has_side_effects=_EFFECT),
    )(*lands, after)
    return res[0], res[1], list(res[2:2 + n]), res[-1]


def _forward_wait(forwarded, after, *, name):
    send_sems, recv_sems, lands, _ = forwarded
    n = len(lands)

    def body(*refs):
        lnd = refs[:n]
        send_sems, recv_sems = refs[n], refs[n + 1]
        pos = _my_pos()
        for a in range(n):
            for q in FORWARDED:
                _forward_copy(lnd, send_sems, recv_sems, pos, a, q, receiving=False).wait_send()
                _forward_copy(lnd, send_sems, recv_sems, pos, a, q, receiving=True).wait_recv()

    res = pl.pallas_call(
        body, name=name,
        out_shape=tuple(pltpu.HBM(t.shape, t.dtype) for t in lands),
        in_specs=[_HBM] * n + [_SEM, _SEM, pl.BlockSpec(memory_space=pl.ANY)],
        out_specs=tuple([_HBM] * n),
        input_output_aliases={i: i for i in range(n)},
        compiler_params=pltpu.CompilerParams(has_side_effects=_EFFECT),
    )(*lands, send_sems, recv_sems, after)
    return list(res)


def _adamw(parts, w, m, v, *, name, tr):
    r, c = w.shape
    assert r % tr == 0

    def body(p_ref, w_ref, m_ref, v_ref, g_ref, d_ref, mo_ref, vo_ref):
        g = p_ref[0].astype(F32)
        for k in range(1, N_DEV):
            g = g + p_ref[k].astype(F32)
        mn = ADAM_B1 * m_ref[...] + (1.0 - ADAM_B1) * g
        vn = ADAM_B2 * v_ref[...] + (1.0 - ADAM_B2) * (g * g)
        m_hat = mn / (1.0 - ADAM_B1 ** ADAM_STEP)
        v_hat = vn / (1.0 - ADAM_B2 ** ADAM_STEP)
        g_ref[...] = g
        d_ref[...] = -ADAM_LR * (m_hat / (jnp.sqrt(v_hat) + ADAM_EPS) + ADAM_WD * w_ref[...])
        mo_ref[...] = mn
        vo_ref[...] = vn

    blk = pl.BlockSpec((tr, c), lambda i: (i, 0))
    return pl.pallas_call(
        body, name=name, grid=(r // tr,),
        in_specs=[pl.BlockSpec((N_DEV, tr, c), lambda i: (0, i, 0)), blk, blk, blk],
        out_specs=[blk] * 4, out_shape=[jax.ShapeDtypeStruct((r, c), F32)] * 4,
        compiler_params=_params(("parallel",)),
    )(parts, w, m, v)


def _pack(parts, lead=()):
    flat = jnp.concatenate([p.reshape(lead + (-1,)) for p in parts], axis=-1)
    n = flat.shape[-1]
    rows = -(-n // (8 * 128)) * 8
    flat = jnp.pad(flat, [(0, 0)] * len(lead) + [(0, rows * 128 - n)])
    return flat.reshape(lead + (rows, 128))


def _unpack(slab, shapes):
    flat = slab.reshape(-1)
    out, off = [], 0
    for shp in shapes:
        size = 1
        for dim in shp:
            size *= dim
        out.append(flat[off:off + size].reshape(shp))
        off += size
    return out


def _to_dev_major(g, axis):
    shp = g.shape
    g = g.reshape(shp[:axis] + (N_DEV, shp[axis] // N_DEV) + shp[axis + 1:])
    return jnp.moveaxis(g, axis, 0)


def _from_dev_major(g, axis):
    g = jnp.moveaxis(g, 0, axis)
    shp = g.shape
    return g.reshape(shp[:axis] + (shp[axis] * shp[axis + 1],) + shp[axis + 2:])


SMALL_SHARDED = ("conv_a", "conv_b", "norm_mix_odd", "pool_scale", "conv_ffn_w")
SMALL_REPLICATED = ("norm_mix_even", "ln_a_g", "ln_a_b", "norm_ffn", "norm_final")
BIG = {"w_in": ("w_in", None, 512), "w_out": ("w_out", None, 128), "w_pool": ("w_pool", None, 128),
       "w_up0": ("w_up", 0, 256), "w_up1": ("w_up", 1, 256),
       "w_down0": ("w_down", 0, 352), "w_down1": ("w_down", 1, 352)}


def kernel(x, norm_mix_even, w_in, conv_a, ln_a_g, ln_a_b, conv_b, w_out, norm_mix_odd, w_pool, pool_scale, norm_ffn, w_up, conv_ffn_w, w_down, norm_final, loss_target, m_norm_mix_even, m_w_in, m_conv_a, m_ln_a_g, m_ln_a_b, m_conv_b, m_w_out, m_norm_mix_odd, m_w_pool, m_pool_scale, m_norm_ffn, m_w_up, m_conv_ffn_w, m_w_down, m_norm_final, v_norm_mix_even, v_w_in, v_conv_a, v_ln_a_g, v_ln_a_b, v_conv_b, v_w_out, v_norm_mix_odd, v_w_pool, v_pool_scale, v_norm_ffn, v_w_up, v_conv_ffn_w, v_w_down, v_norm_final):
    names = ("norm_mix_even", "w_in", "conv_a", "ln_a_g", "ln_a_b", "conv_b", "w_out", "norm_mix_odd", "w_pool",
             "pool_scale", "norm_ffn", "w_up", "conv_ffn_w", "w_down", "norm_final")
    wts = dict(zip(names, (norm_mix_even, w_in, conv_a, ln_a_g, ln_a_b, conv_b, w_out, norm_mix_odd, w_pool,
                           pool_scale, norm_ffn, w_up, conv_ffn_w, w_down, norm_final)))
    mom = dict(zip(names, (m_norm_mix_even, m_w_in, m_conv_a, m_ln_a_g, m_ln_a_b, m_conv_b, m_w_out, m_norm_mix_odd,
                           m_w_pool, m_pool_scale, m_norm_ffn, m_w_up, m_conv_ffn_w, m_w_down, m_norm_final)))
    var = dict(zip(names, (v_norm_mix_even, v_w_in, v_conv_a, v_ln_a_g, v_ln_a_b, v_conv_b, v_w_out, v_norm_mix_odd,
                           v_w_pool, v_pool_scale, v_norm_ffn, v_w_up, v_conv_ffn_w, v_w_down, v_norm_final)))
    d = x.shape[-1]
    n_up = w_up.shape[-1]

    def shard2d(t, key):
        param, layer, _ = BIG[key]
        a = t[param][0] if layer is None else t[param][layer]
        return a.reshape(-1, a.shape[-1])

    small_w = _pack([wts[k] for k in SMALL_SHARDED])
    bf = {k: shard2d(wts, k).astype(BF16) for k in BIG}
    gather_groups = {"mix0": ("w_in", "w_out", "small"), "ffn0": ("w_up0", "w_down0"),
                     "ffn1": ("w_pool", "w_up1", "w_down1")}
    order = list(gather_groups)
    started = {}

    def start_gather(grp, after):
        arrs = [small_w if k == "small" else bf[k] for k in gather_groups[grp]]
        started[grp] = _exchange_start(arrs, [False] * len(arrs), after, name=f"gather_{grp}_start", peers=CHIP_PEERS)

    start_gather(order[0], small_w)

    def weights(grp, after):
        keys = gather_groups[grp]
        lands = _exchange_wait(started[grp], [False] * len(keys), after, name=f"gather_{grp}_wait", peers=CHIP_PEERS)
        forwarded = _forward_start(lands, lands[0], name=f"gather_{grp}_forward")
        if grp != order[-1]:
            start_gather(order[order.index(grp) + 1], forwarded[-1])
            behind = started[order[order.index(grp) + 1]][-1]
        else:
            behind = forwarded[-1]
        gw = dict(zip(keys, _forward_wait(forwarded, behind, name=f"gather_{grp}_forward_wait")))
        if grp == "ffn0":
            return {"w_up": gw["w_up0"], "w_down": gw["w_down0"].reshape(N_PAIR, -1, d)}
        if grp == "ffn1":
            return {"w_up": gw["w_up1"], "w_down": gw["w_down1"].reshape(N_PAIR, -1, d),
                    "w_pool": _from_dev_major(gw["w_pool"].reshape(N_DEV, len(POOL_WINDOWS), -1, POOL_GROUP), 1)}
        per_dev = gw["small"].reshape(N_DEV, -1)
        sizes = [wts[k].size for k in SMALL_SHARDED]
        offs = [sum(sizes[:i]) for i in range(len(sizes))]
        small_full = {k: per_dev[:, o:o + n_].reshape((N_DEV,) + wts[k].shape)
                      for k, o, n_ in zip(SMALL_SHARDED, offs, sizes)}
        return {
            "norm_mix_even": norm_mix_even, "ln_a_g": ln_a_g, "ln_a_b": ln_a_b, "norm_ffn": norm_ffn,
            "norm_final": norm_final[None],
            "w_in": _from_dev_major(gw["w_in"], 1),
            "w_out": gw["w_out"].reshape(-1, d),
            "conv_a": _from_dev_major(small_full["conv_a"][:, 0], 1),
            "conv_b": _from_dev_major(small_full["conv_b"][:, 0], 1),
            "norm_mix_odd": _from_dev_major(small_full["norm_mix_odd"], 1),
            "pool_scale": _from_dev_major(small_full["pool_scale"], 1),
            "conv_ffn": [small_full["conv_ffn_w"][:, l] for l in range(2)],
        }

    def dev_major(k, g):
        if k == "w_in":
            return _to_dev_major(g, 1)
        if k == "w_pool":
            return _to_dev_major(g, 1).reshape(N_DEV, -1, POOL_GROUP)
        return g.reshape(N_DEV, -1, g.shape[-1])

    sent, big_grads = {}, {}

    def grads_ready(grp, grads):
        big_grads.update(grads)
        if grp == "mix0":
            return None
        keys = tuple(grads)
        parts = [dev_major(k, grads[k]) for k in keys]
        sent[grp] = (keys, _exchange_start(parts, [True] * len(keys), parts[0], name=f"grads_{grp}_start"))
        return sent[grp][1][-1]

    loss, dx, g = _local_step(x[0], loss_target[0], weights, grads_ready)

    small_parts = _pack([
        _to_dev_major(g["conv_a"], 1), _to_dev_major(g["conv_b"], 1), _to_dev_major(g["norm_mix_odd"], 1),
        _to_dev_major(g["pool_scale"], 1), jnp.stack(g["conv_ffn"], axis=1)], lead=(N_DEV,))
    repl_parts = _pack([g[k] for k in SMALL_REPLICATED])
    last_keys = ("w_in", "w_out", "small", "replicated")
    last = _exchange_start([dev_major("w_in", big_grads["w_in"]), dev_major("w_out", big_grads["w_out"]),
                            small_parts, repl_parts], [True, True, True, False], dx, name="grads_mix0_start")
    sent["mix0"] = (last_keys, last)

    landed, after = {}, last[-1]
    for grp in ("ffn1", "ffn0", "mix0"):
        keys, st = sent[grp]
        scat = [k != "replicated" for k in keys]
        landed.update(zip(keys, _exchange_wait(st, scat, after, name=f"grads_{grp}_wait")))
    big_out = {}
    for k in BIG:
        big_out[k] = _adamw(landed[k], shard2d(wts, k), shard2d(mom, k), shard2d(var, k), name=f"adamw_{k}",
                            tr=BIG[k][2])
    out = {}
    for k in ("w_in", "w_out", "w_pool"):
        out[k] = [t.reshape(wts[k].shape) for t in big_out[k]]
    for k in ("w_up", "w_down"):
        out[k] = [jnp.stack([a, b]).reshape(wts[k].shape) for a, b in zip(big_out[k + "0"], big_out[k + "1"])]
    res = _adamw(landed["small"], small_w, _pack([mom[k] for k in SMALL_SHARDED]),
                 _pack([var[k] for k in SMALL_SHARDED]), name="adamw_small", tr=small_w.shape[0])
    unpacked = [_unpack(t, [wts[k].shape for k in SMALL_SHARDED]) for t in res]
    for j, k in enumerate(SMALL_SHARDED):
        out[k] = [u[j] for u in unpacked]
    repl_w = _pack([wts[k] for k in SMALL_REPLICATED])
    res = _adamw(landed["replicated"], repl_w, _pack([mom[k] for k in SMALL_REPLICATED]),
                 _pack([var[k] for k in SMALL_REPLICATED]), name="adamw_replicated", tr=repl_w.shape[0])
    unpacked = [_unpack(t, [wts[k].shape for k in SMALL_REPLICATED]) for t in res]
    for j, k in enumerate(SMALL_REPLICATED):
        out[k] = [u[j] for u in unpacked]

    loss = lax.psum(loss, ("x", "y", "c"))
    return (loss, dx[None], *[out[k][0] for k in names], *[out[k][1] for k in names],
            *[out[k][2] for k in names], *[out[k][3] for k in names])
```

```python
import jax
import jax.numpy as jnp
from jax import lax
from jax.experimental import pallas as pl
from jax.experimental.pallas import tpu as pltpu

F32 = jnp.float32
BF16 = jnp.bfloat16

RMS_EPS = 1e-6
LN_EPS = 1e-5
ADAM_LR = 0.001
ADAM_B1 = 0.9
ADAM_B2 = 0.999
ADAM_EPS = 1e-08
ADAM_WD = 0.01
ADAM_STEP = 10

N_DEV = 8
N_PAIR = N_DEV // 2
A_WIDTH = 512
A_TAPS = 31
POOL_WINDOWS = (2, 4, 8, 16)
POOL_GROUP = 256
HALO_A = 32
HALO_S = 16
VMEM_LIMIT = 56 * 1024 * 1024


def _params(sem, vmem=VMEM_LIMIT):
    return pltpu.CompilerParams(dimension_semantics=sem, vmem_limit_bytes=vmem)


def _sigmoid(x):
    return 0.5 * jnp.tanh(0.5 * x) + 0.5


def _prev_blk(i, ts, hb):
    return jnp.maximum(i * (ts // hb) - 1, 0)


def _next_blk(i, ts, hb, s):
    return jnp.minimum((i + 1) * (ts // hb), s // hb - 1)


def _mm(a, b, *, name, ta=False, tb=False, add=None, dep=None, out_dtype=F32, tm=512, tn=512, tk=None):
    m, k = (a.shape[1], a.shape[0]) if ta else a.shape
    n = b.shape[0] if tb else b.shape[1]
    tk = k if tk is None else tk
    tm, tn, tk = min(tm, m), min(tn, n), min(tk, k)
    assert m % tm == 0 and n % tn == 0 and k % tk == 0, (name, m, n, k, tm, tn, tk)
    nk = k // tk
    dims = (((0,) if ta else (1,), (1,) if tb else (0,)), ((), ()))
    n_in = 2 + (add is not None) + (dep is not None)

    def body(*refs):
        a_ref, b_ref = refs[0], refs[1]
        add_ref = refs[2] if add is not None else None
        o_ref = refs[n_in]
        part = lax.dot_general(a_ref[...].astype(BF16), b_ref[...].astype(BF16), dims, preferred_element_type=F32)

        def finish(r):
            if add_ref is not None:
                r = r + add_ref[...]
            o_ref[...] = r.astype(out_dtype)

        if nk == 1:
            finish(part)
            return
        acc_ref = refs[-1]
        kk = pl.program_id(2)

        @pl.when(kk == 0)
        def _():
            acc_ref[...] = part

        @pl.when(kk > 0)
        def _():
            acc_ref[...] += part

        @pl.when(kk == nk - 1)
        def _():
            finish(acc_ref[...])

    a_spec = pl.BlockSpec((tk, tm), lambda i, j, kk: (kk, i)) if ta else pl.BlockSpec((tm, tk), lambda i, j, kk: (i, kk))
    b_spec = pl.BlockSpec((tn, tk), lambda i, j, kk: (j, kk)) if tb else pl.BlockSpec((tk, tn), lambda i, j, kk: (kk, j))
    in_specs = [a_spec, b_spec]
    args = [a, b]
    if add is not None:
        in_specs.append(pl.BlockSpec((tm, tn), lambda i, j, kk: (i, j)))
        args.append(add)
    if dep is not None:
        in_specs.append(pl.BlockSpec(memory_space=pl.ANY))
        args.append(dep)
    return pl.pallas_call(
        body, name=name, grid=(m // tm, n // tn, nk),
        in_specs=in_specs, out_specs=pl.BlockSpec((tm, tn), lambda i, j, kk: (i, j)),
        out_shape=jax.ShapeDtypeStruct((m, n), out_dtype),
        scratch_shapes=[pltpu.VMEM((tm, tn), F32)] if nk > 1 else [],
        compiler_params=_params(("parallel", "parallel", "arbitrary")),
    )(*args)


def _rms_rows(xv, gv):
    return xv * lax.rsqrt(jnp.mean(xv * xv, axis=-1, keepdims=True) + RMS_EPS) * gv


def _rms_mm(x, g, w, *, name, out_dtype, tm=1024, tn=512):
    s, d = x.shape
    n = w.shape[1]
    tm = min(tm, s)
    assert s % tm == 0 and n % tn == 0

    def body(x_ref, g_ref, w_ref, h_ref, z_ref, hs_ref):
        @pl.when(pl.program_id(1) == 0)
        def _():
            hv = _rms_rows(x_ref[...], g_ref[...]).astype(BF16)
            hs_ref[...] = hv
            h_ref[...] = hv

        z_ref[...] = jnp.dot(hs_ref[...], w_ref[...], preferred_element_type=F32).astype(out_dtype)

    return pl.pallas_call(
        body, name=name, grid=(s // tm, n // tn),
        in_specs=[pl.BlockSpec((tm, d), lambda i, j: (i, 0)),
                  pl.BlockSpec((1, d), lambda i, j: (0, 0)),
                  pl.BlockSpec((d, tn), lambda i, j: (0, j))],
        out_specs=[pl.BlockSpec((tm, d), lambda i, j: (i, 0)),
                   pl.BlockSpec((tm, tn), lambda i, j: (i, j))],
        out_shape=[jax.ShapeDtypeStruct((s, d), BF16), jax.ShapeDtypeStruct((s, n), out_dtype)],
        scratch_shapes=[pltpu.VMEM((tm, d), BF16)],
        compiler_params=_params(("parallel", "arbitrary")),
    )(x, g, w)


def _rms_bwd_rows(xv, gv, dh):
    r = lax.rsqrt(jnp.mean(xv * xv, axis=-1, keepdims=True) + RMS_EPS)
    xh = xv * r
    dn = dh * gv
    dx = r * (dn - xh * jnp.mean(dn * xh, axis=-1, keepdims=True))
    return dx, dh * xh


def _mm_rms_bwd(a, b, x, g, dres, dep, *, name, tm=512):
    s, k = a.shape
    d = b.shape[0]
    tm = min(tm, s)

    def body(a_ref, b_ref, x_ref, g_ref, dres_ref, dep_ref, dx_ref, dg_ref):
        @pl.when(pl.program_id(0) == 0)
        def _():
            dg_ref[...] = jnp.zeros_like(dg_ref)

        dh = lax.dot_general(a_ref[...], b_ref[...], (((1,), (1,)), ((), ())), preferred_element_type=F32)
        dx, dgr = _rms_bwd_rows(x_ref[...], g_ref[...], dh)
        dx_ref[...] = dres_ref[...] + dx
        dg_ref[...] += jnp.sum(dgr, axis=0, keepdims=True)

    row = pl.BlockSpec((tm, d), lambda i: (i, 0))
    vec = pl.BlockSpec((1, d), lambda i: (0, 0))
    return pl.pallas_call(
        body, name=name, grid=(s // tm,),
        in_specs=[pl.BlockSpec((tm, k), lambda i: (i, 0)), pl.BlockSpec((d, k), lambda i: (0, 0)), row, vec, row,
                  pl.BlockSpec(memory_space=pl.ANY)],
        out_specs=[row, vec],
        out_shape=[jax.ShapeDtypeStruct((s, d), F32), jax.ShapeDtypeStruct((1, d), F32)],
        compiler_params=_params(("arbitrary",)),
    )(a, b, x, g, dres, dep)


def _final_loss(x, g, target, *, name, ts=256):
    s, d = x.shape
    ts = min(ts, s)

    def body(x_ref, g_ref, t_ref, loss_ref, dx_ref, dxb_ref, dg_ref):
        @pl.when(pl.program_id(0) == 0)
        def _():
            loss_ref[...] = jnp.zeros_like(loss_ref)
            dg_ref[...] = jnp.zeros_like(dg_ref)

        xv = x_ref[...]
        gv = g_ref[...]
        r = lax.rsqrt(jnp.mean(xv * xv, axis=-1, keepdims=True) + RMS_EPS)
        xh = xv * r
        err = xh * gv - t_ref[...]
        loss_ref[...] += 0.5 * jnp.sum(jnp.mean(err * err, axis=-1, keepdims=True), axis=0, keepdims=True)
        dy = err * (1.0 / d)
        dn = dy * gv
        dx = r * (dn - xh * jnp.mean(dn * xh, axis=-1, keepdims=True))
        dx_ref[...] = dx
        dxb_ref[...] = dx.astype(BF16)
        dg_ref[...] += jnp.sum(dy * xh, axis=0, keepdims=True)

    row = pl.BlockSpec((ts, d), lambda i: (i, 0))
    vec = pl.BlockSpec((1, d), lambda i: (0, 0))
    return pl.pallas_call(
        body, name=name, grid=(s // ts,),
        in_specs=[row, vec, row],
        out_specs=[pl.BlockSpec((1, 1), lambda i: (0, 0)), row, row, vec],
        out_shape=[jax.ShapeDtypeStruct((1, 1), F32), jax.ShapeDtypeStruct((s, d), F32),
                   jax.ShapeDtypeStruct((s, d), BF16), jax.ShapeDtypeStruct((1, d), F32)],
        compiler_params=_params(("arbitrary",)),
    )(x, g, target)


def _conv_taps(ext_ref, w_ref, n_taps, base, r0, rows, reverse=False):
    acc = None
    for k in range(n_taps):
        off = r0 + (base - k if reverse else base + k)
        term = w_ref[k:k + 1, :] * ext_ref[pl.ds(off, rows), :]
        acc = term if acc is None else acc + term
    return acc


def _shift_copies(src_ref, sh_ref, rows):
    for b in range(1, 8):
        sh_ref[b, 0:rows, :] = src_ref[pl.ds(b, rows), :]


def _shifted(src_ref, sh_ref, start, rows, off):
    a, b = divmod(off, 8)
    ref = src_ref if b == 0 else sh_ref.at[b]
    return ref[pl.ds(start + 8 * a, rows), :]


def _mix0_fwd(z, conv_a, ln_g, ln_b, conv_b, *, name, ts=256, rc=32):
    s = z.shape[0]
    c = A_WIDTH
    hb = HALO_A

    def body(z_ref, zp_ref, wa_ref, lg_ref, lb_ref, wb_ref, ab_ref, ca_ref, exta, extb, sha):
        keep = jnp.where(pl.program_id(0) > 0, 1.0, 0.0)
        zp = zp_ref[...]
        exta[0:hb, :] = zp[:, 0:c] * _sigmoid(zp[:, c:2 * c]) * keep
        extb[0:hb, :] = zp[:, 3 * c:4 * c] * zp[:, 4 * c:5 * c] * keep
        exta[hb:hb + ts, :] = z_ref[:, 0:c] * _sigmoid(z_ref[:, c:2 * c])
        extb[hb:hb + ts, :] = z_ref[:, 3 * c:4 * c] * z_ref[:, 4 * c:5 * c]
        _shift_copies(exta, sha, hb + ts - 8)
        lg = lg_ref[...]
        lb = lb_ref[...]
        for q in range(ts // rc):
            r0 = q * rc
            ca = None
            for k in range(A_TAPS):
                term = wa_ref[k:k + 1, :] * _shifted(exta, sha, r0, rc, hb - (A_TAPS - 1) + k)
                ca = term if ca is None else ca + term
            ca_ref[r0:r0 + rc, :] = ca
            mu = jnp.mean(ca, axis=-1, keepdims=True)
            xc = ca - mu
            rs = lax.rsqrt(jnp.mean(xc * xc, axis=-1, keepdims=True) + LN_EPS)
            l = xc * rs * lg + lb
            ab_ref[r0:r0 + rc, 0:c] = (l * _sigmoid(l)).astype(BF16)
            cbc = _conv_taps(extb, wb_ref, 3, hb - 2, r0, rc)
            ab_ref[r0:r0 + rc, c:2 * c] = (z_ref[r0:r0 + rc, 2 * c:3 * c] * cbc).astype(BF16)

    return pl.pallas_call(
        body, name=name, grid=(s // ts,),
        in_specs=[pl.BlockSpec((ts, 5 * c), lambda i: (i, 0)),
                  pl.BlockSpec((hb, 5 * c), lambda i: (_prev_blk(i, ts, hb), 0)),
                  pl.BlockSpec((32, c), lambda i: (0, 0)),
                  pl.BlockSpec((1, c), lambda i: (0, 0)),
                  pl.BlockSpec((1, c), lambda i: (0, 0)),
                  pl.BlockSpec((8, c), lambda i: (0, 0))],
        out_specs=[pl.BlockSpec((ts, 2 * c), lambda i: (i, 0)),
                   pl.BlockSpec((ts, c), lambda i: (i, 0))],
        out_shape=[jax.ShapeDtypeStruct((s, 2 * c), BF16), jax.ShapeDtypeStruct((s, c), F32)],
        scratch_shapes=[pltpu.VMEM((hb + ts, c), F32), pltpu.VMEM((hb + ts, c), F32),
                        pltpu.VMEM((8, hb + ts - 8, c), F32)],
        compiler_params=_params(("parallel",)),
    )(z, z, conv_a, ln_g, ln_b, conv_b)


def _mix0_bwd(z, ca, dab, conv_a, ln_g, ln_b, conv_b, *, name, ts=256, rc=32):
    s = z.shape[0]
    c = A_WIDTH
    hb = HALO_A
    ta = A_TAPS

    def body(z_ref, zp_ref, zn_ref, ca_ref, can_ref, d_ref, dn_ref, wa_ref, lg_ref, lb_ref, wb_ref,
             dz_ref, dwa_ref, dwb_ref, dlg_ref, dlb_ref, exta, extb, extdca, extdcb, shd):
        i = pl.program_id(0)
        keep_p = jnp.where(i > 0, 1.0, 0.0)
        keep_n = jnp.where(i < s // ts - 1, 1.0, 0.0)

        @pl.when(i == 0)
        def _():
            dwa_ref[...] = jnp.zeros_like(dwa_ref)
            dwb_ref[...] = jnp.zeros_like(dwb_ref)
            dlg_ref[...] = jnp.zeros_like(dlg_ref)
            dlb_ref[...] = jnp.zeros_like(dlb_ref)

        lg = lg_ref[...]
        lb = lb_ref[...]
        zp = zp_ref[...]
        exta[0:hb, :] = zp[:, 0:c] * _sigmoid(zp[:, c:2 * c]) * keep_p
        extb[0:hb, :] = zp[:, 3 * c:4 * c] * zp[:, 4 * c:5 * c] * keep_p
        exta[hb:hb + ts, :] = z_ref[:, 0:c] * _sigmoid(z_ref[:, c:2 * c])
        extb[hb:hb + ts, :] = z_ref[:, 3 * c:4 * c] * z_ref[:, 4 * c:5 * c]

        def ln_bwd(cav, dav):
            mu = jnp.mean(cav, axis=-1, keepdims=True)
            xc = cav - mu
            rs = lax.rsqrt(jnp.mean(xc * xc, axis=-1, keepdims=True) + LN_EPS)
            nv = xc * rs
            l = nv * lg + lb
            sg = _sigmoid(l)
            dl = dav * (sg * (1.0 + l * (1.0 - sg)))
            dnv = dl * lg
            dca = rs * (dnv - jnp.mean(dnv, axis=-1, keepdims=True)
                        - nv * jnp.mean(dnv * nv, axis=-1, keepdims=True))
            return dca, dl, nv

        dlg_acc = jnp.zeros((1, c), F32)
        dlb_acc = jnp.zeros((1, c), F32)
        for q in range(ts // rc):
            r0 = q * rc
            dca, dl, nv = ln_bwd(ca_ref[r0:r0 + rc, :], d_ref[r0:r0 + rc, 0:c])
            extdca[r0:r0 + rc, :] = dca
            dlg_acc = dlg_acc + jnp.sum(dl * nv, axis=0, keepdims=True)
            dlb_acc = dlb_acc + jnp.sum(dl, axis=0, keepdims=True)
            extdcb[r0:r0 + rc, :] = d_ref[r0:r0 + rc, c:2 * c] * z_ref[r0:r0 + rc, 2 * c:3 * c]
        dca_n, _, _ = ln_bwd(can_ref[...], dn_ref[:, 0:c])
        extdca[ts:ts + hb, :] = dca_n * keep_n
        extdcb[ts:ts + hb, :] = dn_ref[:, c:2 * c] * zn_ref[:, 2 * c:3 * c] * keep_n
        dlg_ref[...] += dlg_acc
        dlb_ref[...] += dlb_acc
        _shift_copies(extdca, shd, ts + hb - 8)

        for q in range(ts // rc):
            r0 = q * rc
            zr = z_ref[r0:r0 + rc, :]
            dga = None
            for k in range(ta):
                term = wa_ref[k:k + 1, :] * _shifted(extdca, shd, r0, rc, ta - 1 - k)
                dga = term if dga is None else dga + term
            sg = _sigmoid(zr[:, c:2 * c])
            dz_ref[r0:r0 + rc, 0:c] = (dga * sg).astype(BF16)
            dz_ref[r0:r0 + rc, c:2 * c] = (dga * zr[:, 0:c] * sg * (1.0 - sg)).astype(BF16)
            cbc = _conv_taps(extb, wb_ref, 3, hb - 2, r0, rc)
            dz_ref[r0:r0 + rc, 2 * c:3 * c] = (d_ref[r0:r0 + rc, c:2 * c] * cbc).astype(BF16)
            dcb = _conv_taps(extdcb, wb_ref, 3, 2, r0, rc, reverse=True)
            dz_ref[r0:r0 + rc, 3 * c:4 * c] = (dcb * zr[:, 4 * c:5 * c]).astype(BF16)
            dz_ref[r0:r0 + rc, 4 * c:5 * c] = (dcb * zr[:, 3 * c:4 * c]).astype(BF16)

        for k in range(ta):
            part = None
            for q in range(ts // rc):
                r0 = q * rc
                p = exta[hb + r0:hb + r0 + rc, :] * _shifted(extdca, shd, r0, rc, ta - 1 - k)
                for r in range(0, rc, 8):
                    part = p[r:r + 8, :] if part is None else part + p[r:r + 8, :]
            dwa_ref[k:k + 1, :] += jnp.sum(part, axis=0, keepdims=True)
        dcb_t = extdcb[0:ts, :]
        for k in range(3):
            dwb_ref[k:k + 1, :] += jnp.sum(dcb_t * extb[pl.ds(hb - 2 + k, ts), :], axis=0, keepdims=True)

    def tile(w):
        return pl.BlockSpec((ts, w), lambda i: (i, 0))

    def prev(w):
        return pl.BlockSpec((hb, w), lambda i: (_prev_blk(i, ts, hb), 0))

    def nxt(w):
        return pl.BlockSpec((hb, w), lambda i: (_next_blk(i, ts, hb, s), 0))

    def const(r, w):
        return pl.BlockSpec((r, w), lambda i: (0, 0))

    return pl.pallas_call(
        body, name=name, grid=(s // ts,),
        in_specs=[tile(5 * c), prev(5 * c), nxt(5 * c), tile(c), nxt(c), tile(2 * c), nxt(2 * c),
                  const(32, c), const(1, c), const(1, c), const(8, c)],
        out_specs=[tile(5 * c), const(32, c), const(8, c), const(1, c), const(1, c)],
        out_shape=[jax.ShapeDtypeStruct((s, 5 * c), BF16), jax.ShapeDtypeStruct((32, c), F32),
                   jax.ShapeDtypeStruct((8, c), F32), jax.ShapeDtypeStruct((1, c), F32),
                   jax.ShapeDtypeStruct((1, c), F32)],
        scratch_shapes=[pltpu.VMEM((hb + ts, c), F32), pltpu.VMEM((hb + ts, c), F32),
                        pltpu.VMEM((ts + hb, c), F32), pltpu.VMEM((ts + hb, c), F32),
                        pltpu.VMEM((8, ts + hb - 8, c), F32)],
        compiler_params=_params(("arbitrary",)),
    )(z, z, z, ca, ca, dab, dab, conv_a, ln_g, ln_b, conv_b)


def _ffn_up(x, g, w8, *, name, tm=1024):
    s, d = x.shape
    nb, _, c = w8.shape
    tm = min(tm, s)

    def body(x_ref, g_ref, w_ref, h_ref, u_ref, hs_ref):
        @pl.when(pl.program_id(1) == 0)
        def _():
            hv = _rms_rows(x_ref[...], g_ref[...]).astype(BF16)
            hs_ref[...] = hv
            h_ref[...] = hv

        u_ref[...] = jnp.dot(hs_ref[...], w_ref[...], preferred_element_type=F32).astype(BF16)

    return pl.pallas_call(
        body, name=name, grid=(s // tm, nb),
        in_specs=[pl.BlockSpec((tm, d), lambda i, k: (i, 0)),
                  pl.BlockSpec((1, d), lambda i, k: (0, 0)),
                  pl.BlockSpec((None, d, c), lambda i, k: (k, 0, 0))],
        out_specs=[pl.BlockSpec((tm, d), lambda i, k: (i, 0)),
                   pl.BlockSpec((None, tm, c), lambda i, k: (k, i, 0))],
        out_shape=[jax.ShapeDtypeStruct((s, d), BF16), jax.ShapeDtypeStruct((nb, s, c), BF16)],
        scratch_shapes=[pltpu.VMEM((tm, d), BF16)],
        compiler_params=_params(("parallel", "arbitrary")),
    )(x, g, w8)


def _ffn_mid(up8, wc8, *, name, ts=512, rc=32):
    nb, s, c = up8.shape
    hb = HALO_S
    ts = min(ts, s)

    def body(g_ref, gp_ref, v_ref, vp_ref, wg_ref, wv_ref, act_ref, ug_ref, uv_ref, extg, extv):
        keep = jnp.where(pl.program_id(0) > 0, 1.0, 0.0)
        extg[0:hb, :] = gp_ref[...].astype(F32) * keep
        extv[0:hb, :] = vp_ref[...].astype(F32) * keep
        extg[hb:hb + ts, :] = g_ref[...].astype(F32)
        extv[hb:hb + ts, :] = v_ref[...].astype(F32)
        for q in range(ts // rc):
            r0 = q * rc
            gg = _conv_taps(extg, wg_ref, 3, hb - 2, r0, rc)
            vv = _conv_taps(extv, wv_ref, 3, hb - 2, r0, rc)
            ug_ref[r0:r0 + rc, :] = gg.astype(BF16)
            uv_ref[r0:r0 + rc, :] = vv.astype(BF16)
            act_ref[r0:r0 + rc, :] = (gg * _sigmoid(gg) * vv).astype(BF16)

    def blk(rows, off, halo):
        if halo:
            return pl.BlockSpec((None, rows, c), lambda i, j: (j + off, _prev_blk(i, ts, hb), 0))
        return pl.BlockSpec((None, rows, c), lambda i, j: (j + off, i, 0))

    def taps(off):
        return pl.BlockSpec((None, 8, c), lambda i, j: (j + off, 0, 0))

    tile = pl.BlockSpec((None, ts, c), lambda i, j: (j, i, 0))
    out = jax.ShapeDtypeStruct((N_PAIR, s, c), BF16)
    return pl.pallas_call(
        body, name=name, grid=(s // ts, N_PAIR),
        in_specs=[blk(ts, 0, False), blk(hb, 0, True), blk(ts, N_PAIR, False), blk(hb, N_PAIR, True),
                  taps(0), taps(N_PAIR)],
        out_specs=[tile, tile, tile], out_shape=[out, out, out],
        scratch_shapes=[pltpu.VMEM((hb + ts, c), F32), pltpu.VMEM((hb + ts, c), F32)],
        compiler_params=_params(("parallel", "parallel")),
    )(up8, up8, up8, up8, wc8, wc8)


def _ffn_down(act4, wd4, x, *, name, tm=512):
    npair, s, c = act4.shape
    d = x.shape[1]
    tm = min(tm, s)

    def body(a_ref, w_ref, x_ref, o_ref):
        acc = x_ref[...]
        for j in range(npair):
            acc = acc + jnp.dot(a_ref[j], w_ref[j], preferred_element_type=F32)
        o_ref[...] = acc

    return pl.pallas_call(
        body, name=name, grid=(s // tm,),
        in_specs=[pl.BlockSpec((npair, tm, c), lambda i: (0, i, 0)),
                  pl.BlockSpec((npair, c, d), lambda i: (0, 0, 0)),
                  pl.BlockSpec((tm, d), lambda i: (i, 0))],
        out_specs=pl.BlockSpec((tm, d), lambda i: (i, 0)),
        out_shape=jax.ShapeDtypeStruct((s, d), F32),
        compiler_params=_params(("parallel",)),
    )(act4, wd4, x)


def _ffn_dact(db, wd4, *, name, tm=1024):
    s, d = db.shape
    npair, c, _ = wd4.shape
    tm = min(tm, s)

    def body(d_ref, w_ref, o_ref):
        o_ref[...] = lax.dot_general(d_ref[...], w_ref[...], (((1,), (1,)), ((), ())),
                                     preferred_element_type=F32).astype(BF16)

    return pl.pallas_call(
        body, name=name, grid=(s // tm, npair),
        in_specs=[pl.BlockSpec((tm, d), lambda i, j: (i, 0)),
                  pl.BlockSpec((None, c, d), lambda i, j: (j, 0, 0))],
        out_specs=pl.BlockSpec((None, tm, c), lambda i, j: (j, i, 0)),
        out_shape=jax.ShapeDtypeStruct((npair, s, c), BF16),
        compiler_params=_params(("parallel", "parallel")),
    )(db, wd4)


def _ffn_dwdown(act4, db, *, name):
    npair, s, c = act4.shape
    d = db.shape[1]

    def body(a_ref, d_ref, o_ref):
        o_ref[...] = lax.dot_general(a_ref[...], d_ref[...], (((0,), (0,)), ((), ())),
                                     preferred_element_type=F32).astype(BF16)

    return pl.pallas_call(
        body, name=name, grid=(npair,),
        in_specs=[pl.BlockSpec((None, s, c), lambda j: (j, 0, 0)),
                  pl.BlockSpec((s, d), lambda j: (0, 0))],
        out_specs=pl.BlockSpec((None, c, d), lambda j: (j, 0, 0)),
        out_shape=jax.ShapeDtypeStruct((npair, c, d), BF16),
        compiler_params=_params(("parallel",)),
    )(act4, db)


def _ffn_midbwd(up8, ug4, uv4, dact4, wc8, *, name, ts=512, rc=32):
    nb, s, c = up8.shape
    hb = HALO_S
    ts = min(ts, s)
    n_i = s // ts

    def body(pg_ref, pv_ref, ug_ref, ugn_ref, uv_ref, uvn_ref, d_ref, dn_ref, wg_ref, wv_ref,
             dg_ref, dv_ref, dwg_ref, dwv_ref, extdg, extdv):
        i = pl.program_id(1)
        keep_n = jnp.where(i < n_i - 1, 1.0, 0.0)

        @pl.when(i == 0)
        def _():
            dwg_ref[...] = jnp.zeros_like(dwg_ref)
            dwv_ref[...] = jnp.zeros_like(dwv_ref)

        def du_rows(r0, rows, gg, vv, dav):
            sg = _sigmoid(gg)
            extdg[r0:r0 + rows, :] = dav * vv * (sg * (1.0 + gg * (1.0 - sg)))
            extdv[r0:r0 + rows, :] = dav * (gg * sg)

        for q in range(ts // rc):
            rows = slice(q * rc, q * rc + rc)
            du_rows(q * rc, rc, ug_ref[rows, :].astype(F32), uv_ref[rows, :].astype(F32),
                    d_ref[rows, :].astype(F32))
        du_rows(ts, 8, ugn_ref[0:8, :].astype(F32), uvn_ref[0:8, :].astype(F32),
                dn_ref[0:8, :].astype(F32) * keep_n)

        def fold(p):
            acc = p[0:8, :]
            for r in range(8, rc, 8):
                acc = acc + p[r:r + 8, :]
            return acc

        for extd, p_ref, w_ref, out_ref, dw_ref in ((extdg, pg_ref, wg_ref, dg_ref, dwg_ref),
                                                    (extdv, pv_ref, wv_ref, dv_ref, dwv_ref)):
            part = [None, None, None]
            for q in range(ts // rc):
                r0 = q * rc
                pre = p_ref[r0:r0 + rc, :].astype(F32)
                dup = None
                for k in range(3):
                    sh = extd[pl.ds(r0 + 2 - k, rc), :]
                    term = w_ref[k:k + 1, :] * sh
                    dup = term if dup is None else dup + term
                    prod = fold(pre * sh)
                    part[k] = prod if part[k] is None else part[k] + prod
                out_ref[r0:r0 + rc, :] = dup.astype(BF16)
            for k in range(3):
                dw_ref[k:k + 1, :] += jnp.sum(part[k], axis=0, keepdims=True)

    def blk(off, nxt):
        if nxt:
            return pl.BlockSpec((None, hb, c), lambda j, i: (j + off, _next_blk(i, ts, hb, s), 0))
        return pl.BlockSpec((None, ts, c), lambda j, i: (j + off, i, 0))

    def taps(off):
        return pl.BlockSpec((None, 8, c), lambda j, i: (j + off, 0, 0))

    tile = blk(0, False)
    acc = pl.BlockSpec((None, 8, c), lambda j, i: (j, 0, 0))
    return pl.pallas_call(
        body, name=name, grid=(N_PAIR, n_i),
        in_specs=[tile, blk(N_PAIR, False), tile, blk(0, True), tile, blk(0, True), tile, blk(0, True),
                  taps(0), taps(N_PAIR)],
        out_specs=[tile, tile, acc, acc],
        out_shape=[jax.ShapeDtypeStruct((N_PAIR, s, c), BF16), jax.ShapeDtypeStruct((N_PAIR, s, c), BF16),
                   jax.ShapeDtypeStruct((N_PAIR, 8, c), F32), jax.ShapeDtypeStruct((N_PAIR, 8, c), F32)],
        scratch_shapes=[pltpu.VMEM((ts + 8, c), F32), pltpu.VMEM((ts + 8, c), F32)],
        compiler_params=_params(("parallel", "arbitrary")),
    )(up8, up8, ug4, ug4, uv4, uv4, dact4, dact4, wc8, wc8)


def _ffn_dh(dupg, dupv, w8, x, g, dres, *, name, tm=256):
    npair, s, c = dupg.shape
    d = x.shape[1]
    tm = min(tm, s)

    def body(dg_ref, dv_ref, w_ref, x_ref, g_ref, dres_ref, dx_ref, dxb_ref, dgain_ref):
        @pl.when(pl.program_id(0) == 0)
        def _():
            dgain_ref[...] = jnp.zeros_like(dgain_ref)

        nt = (((1,), (1,)), ((), ()))
        dh = None
        for j in range(npair):
            for src, k in ((dg_ref, j), (dv_ref, j + npair)):
                part = lax.dot_general(src[j], w_ref[k], nt, preferred_element_type=F32)
                dh = part if dh is None else dh + part
        dx, dgr = _rms_bwd_rows(x_ref[...], g_ref[...], dh)
        dx = dres_ref[...] + dx
        dx_ref[...] = dx
        dxb_ref[...] = dx.astype(BF16)
        dgain_ref[...] += jnp.sum(dgr, axis=0, keepdims=True)

    row = pl.BlockSpec((tm, d), lambda i: (i, 0))
    vec = pl.BlockSpec((1, d), lambda i: (0, 0))
    dup = pl.BlockSpec((npair, tm, c), lambda i: (0, i, 0))
    return pl.pallas_call(
        body, name=name, grid=(s // tm,),
        in_specs=[dup, dup, pl.BlockSpec((2 * npair, d, c), lambda i: (0, 0, 0)), row, vec, row],
        out_specs=[row, row, vec],
        out_shape=[jax.ShapeDtypeStruct((s, d), F32), jax.ShapeDtypeStruct((s, d), BF16),
                   jax.ShapeDtypeStruct((1, d), F32)],
        compiler_params=_params(("arbitrary",)),
    )(dupg, dupv, w8, x, g, dres)


def _ffn_dwup(h, dupg, dupv, *, name, tm=512):
    npair, s, c = dupg.shape
    d = h.shape[1]

    def body(h_ref, dg_ref, dv_ref, o_ref):
        tn = (((0,), (0,)), ((), ()))
        k = pl.program_id(1)

        @pl.when(k < npair)
        def _():
            o_ref[...] = lax.dot_general(h_ref[...], dg_ref[...], tn, preferred_element_type=F32).astype(BF16)

        @pl.when(k >= npair)
        def _():
            o_ref[...] = lax.dot_general(h_ref[...], dv_ref[...], tn, preferred_element_type=F32).astype(BF16)

    return pl.pallas_call(
        body, name=name, grid=(d // tm, 2 * npair),
        in_specs=[pl.BlockSpec((s, tm), lambda m, k: (0, m)),
                  pl.BlockSpec((None, s, c), lambda m, k: (jnp.minimum(k, npair - 1), 0, 0)),
                  pl.BlockSpec((None, s, c), lambda m, k: (jnp.maximum(k - npair, 0), 0, 0))],
        out_specs=pl.BlockSpec((None, tm, c), lambda m, k: (k, m, 0)),
        out_shape=jax.ShapeDtypeStruct((2 * npair, d, c), BF16),
        compiler_params=_params(("parallel", "arbitrary")),
    )(h, dupg, dupv)


def _pool_counts(i, ts, rows, window):
    t = lax.broadcasted_iota(jnp.int32, (rows, 1), 0) + i * ts + 1
    return jnp.minimum(t, window).astype(F32)


def _pool_fwd(x, g, *, name, ts=256):
    s, d = x.shape
    hb = HALO_S
    pg = POOL_GROUP
    ts = min(ts, s)

    def body(x_ref, xp_ref, g_ref, p_ref, ext):
        i = pl.program_id(0)
        keep = jnp.where(i > 0, 1.0, 0.0)
        gv = g_ref[...]
        ext[0:hb, :] = _rms_rows(xp_ref[...], gv) * keep
        ext[hb:hb + ts, :] = _rms_rows(x_ref[...], gv)
        for gi, w in enumerate(POOL_WINDOWS):
            cols = slice(gi * pg, (gi + 1) * pg)
            h = ext[hb:hb + ts, cols]
            acc = h
            for q in range(1, w):
                acc = acc + ext[pl.ds(hb - q, ts), cols]
            p_ref[:, cols] = (acc / _pool_counts(i, ts, ts, w) - h).astype(BF16)

    return pl.pallas_call(
        body, name=name, grid=(s // ts,),
        in_specs=[pl.BlockSpec((ts, d), lambda i: (i, 0)),
                  pl.BlockSpec((hb, d), lambda i: (_prev_blk(i, ts, hb), 0)),
                  pl.BlockSpec((1, d), lambda i: (0, 0))],
        out_specs=pl.BlockSpec((ts, d), lambda i: (i, 0)),
        out_shape=jax.ShapeDtypeStruct((s, d), BF16),
        scratch_shapes=[pltpu.VMEM((hb + ts, d), F32)],
        compiler_params=_params(("parallel",)),
    )(x, x, g)


def _pool_mm_fwd(p, w_pool, scale, x, *, name, ts=1024):
    s, d = x.shape
    pg = POOL_GROUP
    ts = min(ts, s)

    def body(p_ref, w_ref, s_ref, x_ref, o_ref, yu_ref):
        yu = jnp.dot(p_ref[...], w_ref[...], preferred_element_type=F32)
        yu_ref[...] = yu.astype(BF16)
        o_ref[...] = x_ref[...] + yu * s_ref[...]

    blk = pl.BlockSpec((ts, pg), lambda i, gi: (i, gi))
    return pl.pallas_call(
        body, name=name, grid=(s // ts, d // pg),
        in_specs=[blk, pl.BlockSpec((None, pg, pg), lambda i, gi: (gi, 0, 0)),
                  pl.BlockSpec((1, pg), lambda i, gi: (0, gi)), blk],
        out_specs=[blk, blk],
        out_shape=[jax.ShapeDtypeStruct((s, d), F32), jax.ShapeDtypeStruct((s, d), BF16)],
        compiler_params=_params(("parallel", "parallel")),
    )(p, w_pool, scale, x)


def _pool_mm_bwd(dres, w_pool, scale, yu, dep, *, name, ts=1024):
    s, d = dres.shape
    pg = POOL_GROUP
    ts = min(ts, s)

    def body(d_ref, w_ref, s_ref, yu_ref, dep_ref, dyc_ref, dp_ref, ds_ref):
        @pl.when(pl.program_id(1) == 0)
        def _():
            ds_ref[...] = jnp.zeros_like(ds_ref)

        dv = d_ref[...]
        dyc = (dv * s_ref[...]).astype(BF16)
        dyc_ref[...] = dyc
        dp_ref[...] = lax.dot_general(dyc, w_ref[...], (((1,), (1,)), ((), ())), preferred_element_type=F32)
        ds_ref[...] += jnp.sum(dv * yu_ref[...].astype(F32), axis=0, keepdims=True)

    blk = pl.BlockSpec((ts, pg), lambda gi, i: (i, gi))
    vec = pl.BlockSpec((1, pg), lambda gi, i: (0, gi))
    return pl.pallas_call(
        body, name=name, grid=(d // pg, s // ts),
        in_specs=[blk, pl.BlockSpec((None, pg, pg), lambda gi, i: (gi, 0, 0)), vec, blk,
                  pl.BlockSpec(memory_space=pl.ANY)],
        out_specs=[blk, blk, vec],
        out_shape=[jax.ShapeDtypeStruct((s, d), BF16), jax.ShapeDtypeStruct((s, d), F32),
                   jax.ShapeDtypeStruct((1, d), F32)],
        compiler_params=_params(("parallel", "arbitrary")),
    )(dres, w_pool, scale, yu, dep)


def _pool_dw(p, dyc, *, name):
    s, d = p.shape
    pg = POOL_GROUP

    def body(p_ref, d_ref, o_ref):
        o_ref[...] = lax.dot_general(p_ref[...], d_ref[...], (((0,), (0,)), ((), ())),
                                     preferred_element_type=F32).astype(BF16)

    blk = pl.BlockSpec((s, pg), lambda gi: (0, gi))
    return pl.pallas_call(
        body, name=name, grid=(d // pg,),
        in_specs=[blk, blk], out_specs=pl.BlockSpec((None, pg, pg), lambda gi: (gi, 0, 0)),
        out_shape=jax.ShapeDtypeStruct((d // pg, pg, pg), BF16),
        compiler_params=_params(("parallel",)),
    )(p, dyc)


def _pool_bwd(dp, x, g, dres, *, name, ts=256):
    s, d = x.shape
    hb = HALO_S
    pg = POOL_GROUP
    ts = min(ts, s)
    n_i = s // ts

    def body(dp_ref, dpn_ref, x_ref, g_ref, dres_ref, dx_ref, dxb_ref, dg_ref, ext, dh):
        i = pl.program_id(0)
        keep_n = jnp.where(i < n_i - 1, 1.0, 0.0)

        @pl.when(i == 0)
        def _():
            dg_ref[...] = jnp.zeros_like(dg_ref)

        for gi, w in enumerate(POOL_WINDOWS):
            cols = slice(gi * pg, (gi + 1) * pg)
            ext[0:ts, cols] = dp_ref[:, cols] / _pool_counts(i, ts, ts, w)
            ext[ts:ts + hb, cols] = dpn_ref[:, cols] / _pool_counts(i + 1, ts, hb, w) * keep_n
        for gi, w in enumerate(POOL_WINDOWS):
            cols = slice(gi * pg, (gi + 1) * pg)
            acc = ext[0:ts, cols]
            for q in range(1, w):
                acc = acc + ext[pl.ds(q, ts), cols]
            dh[:, cols] = acc - dp_ref[:, cols]
        dx, dgr = _rms_bwd_rows(x_ref[...], g_ref[...], dh[...])
        dx = dres_ref[...] + dx
        dx_ref[...] = dx
        dxb_ref[...] = dx.astype(BF16)
        dg_ref[...] += jnp.sum(dgr, axis=0, keepdims=True)

    row = pl.BlockSpec((ts, d), lambda i: (i, 0))
    vec = pl.BlockSpec((1, d), lambda i: (0, 0))
    return pl.pallas_call(
        body, name=name, grid=(n_i,),
        in_specs=[row, pl.BlockSpec((hb, d), lambda i: (_next_blk(i, ts, hb, s), 0)), row, vec, row],
        out_specs=[row, row, vec],
        out_shape=[jax.ShapeDtypeStruct((s, d), F32), jax.ShapeDtypeStruct((s, d), BF16),
                   jax.ShapeDtypeStruct((1, d), F32)],
        scratch_shapes=[pltpu.VMEM((ts + hb, d), F32), pltpu.VMEM((ts, d), F32)],
        compiler_params=_params(("arbitrary",)),
    )(dp, dp, x, g, dres)


def _pad_rows(w, rows):
    pad = [(0, 0)] * (w.ndim - 2) + [(0, rows - w.shape[-2]), (0, 0)]
    return jnp.pad(w, pad)


def _ffn_layer_fwd(x, nf, w8, wc8, wd4, tag):
    h, up8 = _ffn_up(x, nf, w8, name=f"ffn{tag}_up")
    act4, ug4, uv4 = _ffn_mid(up8, wc8, name=f"ffn{tag}_mid")
    x_out = _ffn_down(act4, wd4, x, name=f"ffn{tag}_down")
    return x_out, (h, up8, ug4, uv4, act4)


def _ffn_layer_bwd(d, db, x, nf, w8, wc8, wd4, saved, tag):
    h, up8, ug4, uv4, act4 = saved
    dact4 = _ffn_dact(db, wd4, name=f"ffn{tag}_dact")
    dwd4 = _ffn_dwdown(act4, db, name=f"ffn{tag}_dwdown")
    dupg, dupv, dwg, dwv = _ffn_midbwd(up8, ug4, uv4, dact4, wc8, name=f"ffn{tag}_midbwd")
    dx, dxb, dnf = _ffn_dh(dupg, dupv, w8, x, nf, d, name=f"ffn{tag}_dh")
    dw8 = _ffn_dwup(h, dupg, dupv, name=f"ffn{tag}_dwup")
    dwc8 = jnp.concatenate([dwg, dwv], axis=0)[:, :3]
    return dx, dxb, dnf, dw8, dwc8, dwd4


def _local_step(x, target, weights, grads_ready):
    w0 = weights("mix0", x)
    wa = _pad_rows(w0["conv_a"], 32)
    wb = _pad_rows(w0["conv_b"], 8)
    wc = [_pad_rows(w0["conv_ffn"][l], 8) for l in range(2)]
    h0, z = _rms_mm(x, w0["norm_mix_even"], w0["w_in"], name="mix0_in", out_dtype=F32, tn=1280)
    ab, ca = _mix0_fwd(z, wa, w0["ln_a_g"], w0["ln_a_b"], wb, name="mix0_mid")
    x1 = _mm(ab, w0["w_out"], add=x, name="mix0_out", tm=1024, tn=1024)
    w1 = weights("ffn0", x1)
    x2, ffn0 = _ffn_layer_fwd(x1, w0["norm_ffn"][0:1], w1["w_up"], wc[0], w1["w_down"], 0)
    w2 = weights("ffn1", x2)
    p = _pool_fwd(x2, w0["norm_mix_odd"], name="pool_mid")
    x3, yu = _pool_mm_fwd(p, w2["w_pool"], w0["pool_scale"], x2, name="pool_mm")
    x4, ffn1 = _ffn_layer_fwd(x3, w0["norm_ffn"][1:2], w2["w_up"], wc[1], w2["w_down"], 1)
    loss, d4, d4b, g_norm_final = _final_loss(x4, w0["norm_final"], target, name="final_loss")

    d3, d3b, g_nf1, g_up1, g_wc1, g_down1 = _ffn_layer_bwd(
        d4, d4b, x3, w0["norm_ffn"][1:2], w2["w_up"], wc[1], w2["w_down"], ffn1, 1)
    sent1 = grads_ready("ffn1", {"w_up1": g_up1, "w_down1": g_down1})
    dyc, dp, g_scale = _pool_mm_bwd(d3, w2["w_pool"], w0["pool_scale"], yu, sent1, name="pool_mm_bwd")
    g_pool = _pool_dw(p, dyc, name="pool_dw")
    d2, d2b, g_nmo = _pool_bwd(dp, x2, w0["norm_mix_odd"], d3, name="pool_midbwd")
    d1, d1b, g_nf0, g_up0, g_wc0, g_down0 = _ffn_layer_bwd(
        d2, d2b, x1, w0["norm_ffn"][0:1], w1["w_up"], wc[0], w1["w_down"], ffn0, 0)
    sent0 = grads_ready("ffn0", {"w_pool": g_pool, "w_up0": g_up0, "w_down0": g_down0})
    dab = _mm(d1b, w0["w_out"], tb=True, dep=sent0, name="mix0_dab", tm=1024, tn=1024)
    g_out = _mm(ab, d1b, ta=True, out_dtype=BF16, name="mix0_dwout", tm=1024, tn=512)
    dz, g_wa, g_wb, g_lg, g_lb = _mix0_bwd(z, ca, dab, wa, w0["ln_a_g"], w0["ln_a_b"], wb, name="mix0_midbwd")
    g_in = _mm(h0, dz, ta=True, out_dtype=BF16, name="mix0_dwin", tm=1024, tn=512)
    sent_mix = grads_ready("mix0", {"w_in": g_in, "w_out": g_out})
    dx, g_nme = _mm_rms_bwd(dz, w0["w_in"], x, w0["norm_mix_even"], d1, sent_mix, name="mix0_dh")

    small = {
        "norm_mix_even": g_nme, "conv_a": g_wa[:A_TAPS], "ln_a_g": g_lg, "ln_a_b": g_lb, "conv_b": g_wb[:3],
        "norm_mix_odd": g_nmo, "pool_scale": g_scale, "norm_ffn": jnp.concatenate([g_nf0, g_nf1], axis=0),
        "conv_ffn": [g_wc0, g_wc1], "norm_final": g_norm_final,
    }
    return loss[0, 0], dx, small


def _my_pos():
    return lax.axis_index("x"), lax.axis_index("y"), lax.axis_index("c")


def _flip(pos, r):
    x, y, c = pos
    return (1 - x if r & 4 else x, 1 - y if r & 2 else y, 1 - c if r & 1 else c)


def _dev_index(pos):
    return 4 * pos[0] + 2 * pos[1] + pos[2]


_HBM = pl.BlockSpec(memory_space=pltpu.HBM)
_SEM = pl.BlockSpec(memory_space=pltpu.SEMAPHORE)
_EFFECT = pltpu.SideEffectType.DATAFLOW_SIDE_EFFECTING


def _exchange_copy(ins, lands, send_sems, recv_sems, scatter, pos, a, r, receiving):
    me = _dev_index(pos)
    peer = _flip(pos, r)
    dest = _dev_index(pos) if receiving else _dev_index(peer)
    src = ins[a].at[dest] if scatter[a] else ins[a]
    slot = _dev_index(peer) if receiving else me
    return pltpu.make_async_remote_copy(
        src_ref=src, dst_ref=lands[a].at[slot], send_sem=send_sems.at[a * (N_DEV - 1) + r - 1],
        recv_sem=recv_sems.at[a * (N_DEV - 1) + r - 1],
        device_id=peer, device_id_type=pl.DeviceIdType.MESH)


ALL_PEERS = tuple(range(1, N_DEV))
CHIP_PEERS = (1, 2, 4, 6)
FORWARDED = (2, 4, 6)


def _exchange_start(arrays, scatter, after, *, name, peers=ALL_PEERS):
    n = len(arrays)
    me = _dev_index(_my_pos())
    lands = []
    for arr, sc in zip(arrays, scatter):
        own = lax.dynamic_index_in_dim(arr, me, 0, keepdims=True) if sc else arr[None]
        shape = arr.shape if sc else (N_DEV,) + arr.shape
        lands.append(lax.dynamic_update_slice(lax.empty(shape, arr.dtype), own, (me,) + (0,) * (len(shape) - 1)))

    def body(*refs):
        ins, lnd = refs[:n], refs[n:2 * n]
        send_sems, recv_sems = refs[2 * n + 1], refs[2 * n + 2]
        token = refs[-1]
        pos = _my_pos()
        for a in range(n):
            for r in peers:
                _exchange_copy(ins, lnd, send_sems, recv_sems, scatter, pos, a, r, receiving=False).start()
        token[...] = jnp.zeros_like(token)

    bufs = [pltpu.with_memory_space_constraint(t, pltpu.HBM) for t in list(arrays) + lands]
    sems = pltpu.SemaphoreType.DMA((n * (N_DEV - 1),))
    res = pl.pallas_call(
        body, name=name,
        out_shape=(sems, sems, *[pltpu.HBM(t.shape, t.dtype) for t in bufs], jax.ShapeDtypeStruct((8, 128), F32)),
        in_specs=[_HBM] * (2 * n) + [pl.BlockSpec(memory_space=pl.ANY)],
        out_specs=(_SEM, _SEM, *[_HBM] * (2 * n), pl.BlockSpec(memory_space=pltpu.VMEM)),
        input_output_aliases={i: 2 + i for i in range(2 * n)},
        compiler_params=pltpu.CompilerParams(has_side_effects=_EFFECT),
    )(*bufs, after)
    return res[0], res[1], list(res[2:2 + n]), list(res[2 + n:2 + 2 * n]), res[-1]


def _exchange_wait(started, scatter, after, *, name, peers=ALL_PEERS):
    send_sems, recv_sems, arrays, lands, _ = started
    n = len(arrays)

    def body(*refs):
        ins, lnd = refs[:n], refs[n:2 * n]
        send_sems, recv_sems = refs[2 * n], refs[2 * n + 1]
        pos = _my_pos()
        for a in range(n):
            for r in peers:
                _exchange_copy(ins, lnd, send_sems, recv_sems, scatter, pos, a, r, receiving=False).wait_send()
                _exchange_copy(ins, lnd, send_sems, recv_sems, scatter, pos, a, r, receiving=True).wait_recv()

    bufs = list(arrays) + list(lands)
    res = pl.pallas_call(
        body, name=name,
        out_shape=tuple(pltpu.HBM(t.shape, t.dtype) for t in bufs),
        in_specs=[_HBM] * (2 * n) + [_SEM, _SEM, pl.BlockSpec(memory_space=pl.ANY)],
        out_specs=tuple([_HBM] * (2 * n)),
        input_output_aliases={i: i for i in range(2 * n)},
        compiler_params=pltpu.CompilerParams(has_side_effects=_EFFECT),
    )(*bufs, send_sems, recv_sems, after)
    return list(res[n:])


def _forward_copy(lands, send_sems, recv_sems, pos, a, q, receiving):
    slot = _dev_index(_flip(pos, q ^ 1 if receiving else q))
    idx = a * len(FORWARDED) + FORWARDED.index(q)
    return pltpu.make_async_remote_copy(
        src_ref=lands[a].at[slot], dst_ref=lands[a].at[slot], send_sem=send_sems.at[idx], recv_sem=recv_sems.at[idx],
        device_id=_flip(pos, 1), device_id_type=pl.DeviceIdType.MESH)


def _forward_start(lands, after, *, name):
    n = len(lands)

    def body(*refs):
        lnd = refs[:n]
        send_sems, recv_sems = refs[n + 1], refs[n + 2]
        token = refs[-1]
        pos = _my_pos()
        for a in range(n):
            for q in FORWARDED:
                _forward_copy(lnd, send_sems, recv_sems, pos, a, q, receiving=False).start()
        token[...] = jnp.zeros_like(token)

    sems = pltpu.SemaphoreType.DMA((n * len(FORWARDED),))
    res = pl.pallas_call(
        body, name=name,
        out_shape=(sems, sems, *[pltpu.HBM(t.shape, t.dtype) for t in lands], jax.ShapeDtypeStruct((8, 128), F32)),
        in_specs=[_HBM] * n + [pl.BlockSpec(memory_space=pl.ANY)],
        out_specs=(_SEM, _SEM, *[_HBM] * n, pl.BlockSpec(memory_space=pltpu.VMEM)),
        input_output_aliases={i: 2 + i for i in range(n)},
        compiler_params=pltpu.CompilerParams(has_side_effects=_EFFECT),
    )(*lands, after)
    return res[0], res[1], list(res[2:2 + n]), res[-1]


def _forward_wait(forwarded, after, *, name):
    send_sems, recv_sems, lands, _ = forwarded
    n = len(lands)

    def body(*refs):
        lnd = refs[:n]
        send_sems, recv_sems = refs[n], refs[n + 1]
        pos = _my_pos()
        for a in range(n):
            for q in FORWARDED:
                _forward_copy(lnd, send_sems, recv_sems, pos, a, q, receiving=False).wait_send()
                _forward_copy(lnd, send_sems, recv_sems, pos, a, q, receiving=True).wait_recv()

    res = pl.pallas_call(
        body, name=name,
        out_shape=tuple(pltpu.HBM(t.shape, t.dtype) for t in lands),
        in_specs=[_HBM] * n + [_SEM, _SEM, pl.BlockSpec(memory_space=pl.ANY)],
        out_specs=tuple([_HBM] * n),
        input_output_aliases={i: i for i in range(n)},
        compiler_params=pltpu.CompilerParams(has_side_effects=_EFFECT),
    )(*lands, send_sems, recv_sems, after)
    return list(res)


def _adamw(parts, w, m, v, *, name, tr):
    nl, r, c = w.shape
    assert len(parts) == nl and r % tr == 0
    n_i = r // tr

    def body(*refs):
        p_refs = refs[:nl]
        w_ref, m_ref, v_ref, g_ref, d_ref, mo_ref, vo_ref = refs[nl:]

        def update(p_ref):
            g = p_ref[0].astype(F32)
            for k in range(1, N_DEV):
                g = g + p_ref[k].astype(F32)
            mn = ADAM_B1 * m_ref[...] + (1.0 - ADAM_B1) * g
            vn = ADAM_B2 * v_ref[...] + (1.0 - ADAM_B2) * (g * g)
            m_hat = mn / (1.0 - ADAM_B1 ** ADAM_STEP)
            v_hat = vn / (1.0 - ADAM_B2 ** ADAM_STEP)
            g_ref[...] = g
            d_ref[...] = -ADAM_LR * (m_hat / (jnp.sqrt(v_hat) + ADAM_EPS) + ADAM_WD * w_ref[...])
            mo_ref[...] = mn
            vo_ref[...] = vn

        if nl == 1:
            update(p_refs[0])
        else:
            for layer in range(nl):
                pl.when(pl.program_id(0) == layer)(lambda p_ref=p_refs[layer]: update(p_ref))

    def parts_spec(layer):
        def index(l, i):
            return (0, jnp.where(l < layer, 0, jnp.where(l > layer, n_i - 1, i)), 0)
        return pl.BlockSpec((N_DEV, tr, c), index)

    blk = pl.BlockSpec((None, tr, c), lambda l, i: (l, i, 0))
    return pl.pallas_call(
        body, name=name, grid=(nl, n_i),
        in_specs=[parts_spec(layer) for layer in range(nl)] + [blk, blk, blk],
        out_specs=[blk] * 4, out_shape=[jax.ShapeDtypeStruct((nl, r, c), F32)] * 4,
        compiler_params=_params(("arbitrary", "arbitrary")),
    )(*parts, w, m, v)


def _pack(parts, lead=()):
    flat = jnp.concatenate([p.reshape(lead + (-1,)) for p in parts], axis=-1)
    n = flat.shape[-1]
    rows = -(-n // (8 * 128)) * 8
    flat = jnp.pad(flat, [(0, 0)] * len(lead) + [(0, rows * 128 - n)])
    return flat.reshape(lead + (rows, 128))


def _unpack(slab, shapes):
    flat = slab.reshape(-1)
    out, off = [], 0
    for shp in shapes:
        size = 1
        for dim in shp:
            size *= dim
        out.append(flat[off:off + size].reshape(shp))
        off += size
    return out


def _to_dev_major(g, axis):
    shp = g.shape
    g = g.reshape(shp[:axis] + (N_DEV, shp[axis] // N_DEV) + shp[axis + 1:])
    return jnp.moveaxis(g, axis, 0)


def _from_dev_major(g, axis):
    g = jnp.moveaxis(g, 0, axis)
    shp = g.shape
    return g.reshape(shp[:axis] + (shp[axis] * shp[axis + 1],) + shp[axis + 2:])


SMALL_SHARDED = ("conv_a", "conv_b", "norm_mix_odd", "pool_scale", "conv_ffn_w")
SMALL_REPLICATED = ("norm_mix_even", "ln_a_g", "ln_a_b", "norm_ffn", "norm_final")
BIG = {"w_in": ("w_in", None, 512), "w_out": ("w_out", None, 128), "w_pool": ("w_pool", None, 128),
       "w_up0": ("w_up", 0, 256), "w_up1": ("w_up", 1, 256),
       "w_down0": ("w_down", 0, 352), "w_down1": ("w_down", 1, 352)}


def kernel(x, norm_mix_even, w_in, conv_a, ln_a_g, ln_a_b, conv_b, w_out, norm_mix_odd, w_pool, pool_scale, norm_ffn, w_up, conv_ffn_w, w_down, norm_final, loss_target, m_norm_mix_even, m_w_in, m_conv_a, m_ln_a_g, m_ln_a_b, m_conv_b, m_w_out, m_norm_mix_odd, m_w_pool, m_pool_scale, m_norm_ffn, m_w_up, m_conv_ffn_w, m_w_down, m_norm_final, v_norm_mix_even, v_w_in, v_conv_a, v_ln_a_g, v_ln_a_b, v_conv_b, v_w_out, v_norm_mix_odd, v_w_pool, v_pool_scale, v_norm_ffn, v_w_up, v_conv_ffn_w, v_w_down, v_norm_final):
    names = ("norm_mix_even", "w_in", "conv_a", "ln_a_g", "ln_a_b", "conv_b", "w_out", "norm_mix_odd", "w_pool",
             "pool_scale", "norm_ffn", "w_up", "conv_ffn_w", "w_down", "norm_final")
    wts = dict(zip(names, (norm_mix_even, w_in, conv_a, ln_a_g, ln_a_b, conv_b, w_out, norm_mix_odd, w_pool,
                           pool_scale, norm_ffn, w_up, conv_ffn_w, w_down, norm_final)))
    mom = dict(zip(names, (m_norm_mix_even, m_w_in, m_conv_a, m_ln_a_g, m_ln_a_b, m_conv_b, m_w_out, m_norm_mix_odd,
                           m_w_pool, m_pool_scale, m_norm_ffn, m_w_up, m_conv_ffn_w, m_w_down, m_norm_final)))
    var = dict(zip(names, (v_norm_mix_even, v_w_in, v_conv_a, v_ln_a_g, v_ln_a_b, v_conv_b, v_w_out, v_norm_mix_odd,
                           v_w_pool, v_pool_scale, v_norm_ffn, v_w_up, v_conv_ffn_w, v_w_down, v_norm_final)))
    d = x.shape[-1]
    n_up = w_up.shape[-1]

    def shard2d(t, key):
        param, layer, _ = BIG[key]
        a = t[param][0] if layer is None else t[param][layer]
        return a.reshape(-1, a.shape[-1])

    small_w = _pack([wts[k] for k in SMALL_SHARDED])
    bf = {k: shard2d(wts, k).astype(BF16) for k in BIG}
    gather_groups = {"mix0": ("w_in", "w_out", "small"), "ffn0": ("w_up0", "w_down0"),
                     "ffn1": ("w_pool", "w_up1", "w_down1")}
    order = list(gather_groups)
    started = {}

    def start_gather(grp, after):
        arrs = [small_w if k == "small" else bf[k] for k in gather_groups[grp]]
        started[grp] = _exchange_start(arrs, [False] * len(arrs), after, name=f"gather_{grp}_start", peers=CHIP_PEERS)

    start_gather(order[0], small_w)

    def weights(grp, after):
        keys = gather_groups[grp]
        lands = _exchange_wait(started[grp], [False] * len(keys), after, name=f"gather_{grp}_wait", peers=CHIP_PEERS)
        forwarded = _forward_start(lands, lands[0], name=f"gather_{grp}_forward")
        if grp != order[-1]:
            start_gather(order[order.index(grp) + 1], forwarded[-1])
            behind = started[order[order.index(grp) + 1]][-1]
        else:
            behind = forwarded[-1]
        gw = dict(zip(keys, _forward_wait(forwarded, behind, name=f"gather_{grp}_forward_wait")))
        if grp == "ffn0":
            return {"w_up": gw["w_up0"], "w_down": gw["w_down0"].reshape(N_PAIR, -1, d)}
        if grp == "ffn1":
            return {"w_up": gw["w_up1"], "w_down": gw["w_down1"].reshape(N_PAIR, -1, d),
                    "w_pool": _from_dev_major(gw["w_pool"].reshape(N_DEV, len(POOL_WINDOWS), -1, POOL_GROUP), 1)}
        per_dev = gw["small"].reshape(N_DEV, -1)
        sizes = [wts[k].size for k in SMALL_SHARDED]
        offs = [sum(sizes[:i]) for i in range(len(sizes))]
        small_full = {k: per_dev[:, o:o + n_].reshape((N_DEV,) + wts[k].shape)
                      for k, o, n_ in zip(SMALL_SHARDED, offs, sizes)}
        return {
            "norm_mix_even": norm_mix_even, "ln_a_g": ln_a_g, "ln_a_b": ln_a_b, "norm_ffn": norm_ffn,
            "norm_final": norm_final[None],
            "w_in": _from_dev_major(gw["w_in"], 1),
            "w_out": gw["w_out"].reshape(-1, d),
            "conv_a": _from_dev_major(small_full["conv_a"][:, 0], 1),
            "conv_b": _from_dev_major(small_full["conv_b"][:, 0], 1),
            "norm_mix_odd": _from_dev_major(small_full["norm_mix_odd"], 1),
            "pool_scale": _from_dev_major(small_full["pool_scale"], 1),
            "conv_ffn": [small_full["conv_ffn_w"][:, l] for l in range(2)],
        }

    def dev_major(k, g):
        if k == "w_in":
            return _to_dev_major(g, 1)
        if k == "w_pool":
            return _to_dev_major(g, 1).reshape(N_DEV, -1, POOL_GROUP)
        return g.reshape(N_DEV, -1, g.shape[-1])

    sent = {}

    def grads_ready(grp, grads):
        keys = tuple(grads)
        parts = [dev_major(k, grads[k]) for k in keys]
        sent[grp] = (keys, _exchange_start(parts, [True] * len(keys), parts[0], name=f"grads_{grp}_start"))
        return sent[grp][1][-1]

    loss, dx, g = _local_step(x[0], loss_target[0], weights, grads_ready)

    small_parts = _pack([
        _to_dev_major(g["conv_a"], 1), _to_dev_major(g["conv_b"], 1), _to_dev_major(g["norm_mix_odd"], 1),
        _to_dev_major(g["pool_scale"], 1), jnp.stack(g["conv_ffn"], axis=1)], lead=(N_DEV,))
    repl_parts = _pack([g[k] for k in SMALL_REPLICATED])
    sent["small"] = (("small", "replicated"),
                     _exchange_start([small_parts, repl_parts], [True, False], dx, name="grads_small_start"))

    landed, after = {}, sent["small"][1][-1]
    for grp in ("ffn1", "ffn0", "mix0"):
        keys, st = sent[grp]
        landed.update(zip(keys, _exchange_wait(st, [True] * len(keys), after, name=f"grads_{grp}_wait")))

    def shard3d(t, param):
        a = t[param]
        return a.reshape(a.shape[0], -1, a.shape[-1])

    out = {}
    for param, keys in (("w_up", ("w_up0", "w_up1")), ("w_down", ("w_down0", "w_down1")), ("w_in", ("w_in",)),
                        ("w_out", ("w_out",)), ("w_pool", ("w_pool",))):
        res = _adamw([landed[k] for k in keys], shard3d(wts, param), shard3d(mom, param), shard3d(var, param),
                     name=f"adamw_{param}", tr=BIG[keys[0]][2])
        out[param] = [t.reshape(wts[param].shape) for t in res]
    keys, st = sent["small"]
    landed.update(zip(keys, _exchange_wait(st, [True, False], out["w_pool"][1], name="grads_small_wait")))

    def slab(t):
        return t[None]

    res = _adamw([landed["small"]], slab(small_w), slab(_pack([mom[k] for k in SMALL_SHARDED])),
                 slab(_pack([var[k] for k in SMALL_SHARDED])), name="adamw_small", tr=small_w.shape[0])
    unpacked = [_unpack(t, [wts[k].shape for k in SMALL_SHARDED]) for t in res]
    for j, k in enumerate(SMALL_SHARDED):
        out[k] = [u[j] for u in unpacked]
    repl_w = _pack([wts[k] for k in SMALL_REPLICATED])
    res = _adamw([landed["replicated"]], slab(repl_w), slab(_pack([mom[k] for k in SMALL_REPLICATED])),
                 slab(_pack([var[k] for k in SMALL_REPLICATED])), name="adamw_replicated", tr=repl_w.shape[0])
    unpacked = [_unpack(t, [wts[k].shape for k in SMALL_REPLICATED]) for t in res]
    for j, k in enumerate(SMALL_REPLICATED):
        out[k] = [u[j] for u in unpacked]

    loss = lax.psum(loss, ("x", "y", "c"))
    return (loss, dx[None], *[out[k][0] for k in names], *[out[k][1] for k in names],
            *[out[k][2] for k in names], *[out[k][3] for k in names])
```

```python
import jax
import jax.numpy as jnp
from jax import lax
from jax.experimental import pallas as pl
from jax.experimental.pallas import tpu as pltpu

F32 = jnp.float32
BF16 = jnp.bfloat16

RMS_EPS = 1e-6
LN_EPS = 1e-5
ADAM_LR = 0.001
ADAM_B1 = 0.9
ADAM_B2 = 0.999
ADAM_EPS = 1e-08
ADAM_WD = 0.01
ADAM_STEP = 10

N_DEV = 8
N_PAIR = N_DEV // 2
A_WIDTH = 512
A_TAPS = 31
POOL_WINDOWS = (2, 4, 8, 16)
POOL_GROUP = 256
HALO_A = 32
HALO_S = 16
VMEM_LIMIT = 56 * 1024 * 1024


def _params(sem, vmem=VMEM_LIMIT):
    return pltpu.CompilerParams(dimension_semantics=sem, vmem_limit_bytes=vmem)


def _sigmoid(x):
    return 0.5 * jnp.tanh(0.5 * x) + 0.5


def _prev_blk(i, ts, hb):
    return jnp.maximum(i * (ts // hb) - 1, 0)


def _next_blk(i, ts, hb, s):
    return jnp.minimum((i + 1) * (ts // hb), s // hb - 1)


def _mm(a, b, *, name, ta=False, tb=False, add=None, dep=None, out_dtype=F32, tm=512, tn=512, tk=None):
    m, k = (a.shape[1], a.shape[0]) if ta else a.shape
    n = b.shape[0] if tb else b.shape[1]
    tk = k if tk is None else tk
    tm, tn, tk = min(tm, m), min(tn, n), min(tk, k)
    assert m % tm == 0 and n % tn == 0 and k % tk == 0, (name, m, n, k, tm, tn, tk)
    nk = k // tk
    dims = (((0,) if ta else (1,), (1,) if tb else (0,)), ((), ()))
    n_in = 2 + (add is not None) + (dep is not None)

    def body(*refs):
        a_ref, b_ref = refs[0], refs[1]
        add_ref = refs[2] if add is not None else None
        o_ref = refs[n_in]
        part = lax.dot_general(a_ref[...].astype(BF16), b_ref[...].astype(BF16), dims, preferred_element_type=F32)

        def finish(r):
            if add_ref is not None:
                r = r + add_ref[...]
            o_ref[...] = r.astype(out_dtype)

        if nk == 1:
            finish(part)
            return
        acc_ref = refs[-1]
        kk = pl.program_id(2)

        @pl.when(kk == 0)
        def _():
            acc_ref[...] = part

        @pl.when(kk > 0)
        def _():
            acc_ref[...] += part

        @pl.when(kk == nk - 1)
        def _():
            finish(acc_ref[...])

    a_spec = pl.BlockSpec((tk, tm), lambda i, j, kk: (kk, i)) if ta else pl.BlockSpec((tm, tk), lambda i, j, kk: (i, kk))
    b_spec = pl.BlockSpec((tn, tk), lambda i, j, kk: (j, kk)) if tb else pl.BlockSpec((tk, tn), lambda i, j, kk: (kk, j))
    in_specs = [a_spec, b_spec]
    args = [a, b]
    if add is not None:
        in_specs.append(pl.BlockSpec((tm, tn), lambda i, j, kk: (i, j)))
        args.append(add)
    if dep is not None:
        in_specs.append(pl.BlockSpec(memory_space=pl.ANY))
        args.append(dep)
    return pl.pallas_call(
        body, name=name, grid=(m // tm, n // tn, nk),
        in_specs=in_specs, out_specs=pl.BlockSpec((tm, tn), lambda i, j, kk: (i, j)),
        out_shape=jax.ShapeDtypeStruct((m, n), out_dtype),
        scratch_shapes=[pltpu.VMEM((tm, tn), F32)] if nk > 1 else [],
        compiler_params=_params(("parallel", "parallel", "arbitrary")),
    )(*args)


def _rms_rows(xv, gv):
    return xv * lax.rsqrt(jnp.mean(xv * xv, axis=-1, keepdims=True) + RMS_EPS) * gv


_NT = (((1,), (1,)), ((), ()))
_TN = (((0,), (0,)), ((), ()))


def _rms_mm(x, g, wt, *, name, out_dtype, tm=1024, tn=512):
    s, d = x.shape
    n = wt.shape[0]
    tm = min(tm, s)
    assert s % tm == 0 and n % tn == 0

    def body(x_ref, g_ref, w_ref, h_ref, z_ref, hs_ref):
        @pl.when(pl.program_id(1) == 0)
        def _():
            hv = _rms_rows(x_ref[...], g_ref[...]).astype(BF16)
            hs_ref[...] = hv
            h_ref[...] = hv

        z_ref[...] = lax.dot_general(hs_ref[...], w_ref[...], _NT, preferred_element_type=F32).astype(out_dtype)

    return pl.pallas_call(
        body, name=name, grid=(s // tm, n // tn),
        in_specs=[pl.BlockSpec((tm, d), lambda i, j: (i, 0)),
                  pl.BlockSpec((1, d), lambda i, j: (0, 0)),
                  pl.BlockSpec((tn, d), lambda i, j: (j, 0))],
        out_specs=[pl.BlockSpec((tm, d), lambda i, j: (i, 0)),
                   pl.BlockSpec((tm, tn), lambda i, j: (i, j))],
        out_shape=[jax.ShapeDtypeStruct((s, d), BF16), jax.ShapeDtypeStruct((s, n), out_dtype)],
        scratch_shapes=[pltpu.VMEM((tm, d), BF16)],
        compiler_params=_params(("parallel", "arbitrary")),
    )(x, g, wt)


def _rms_bwd_rows(xv, gv, dh):
    r = lax.rsqrt(jnp.mean(xv * xv, axis=-1, keepdims=True) + RMS_EPS)
    xh = xv * r
    dn = dh * gv
    dx = r * (dn - xh * jnp.mean(dn * xh, axis=-1, keepdims=True))
    return dx, dh * xh


def _mm_rms_bwd(a, b, x, g, dres, dep, *, name, tm=512):
    s, k = a.shape
    d = b.shape[1]
    tm = min(tm, s)

    def body(a_ref, b_ref, x_ref, g_ref, dres_ref, dep_ref, dx_ref, dg_ref):
        @pl.when(pl.program_id(0) == 0)
        def _():
            dg_ref[...] = jnp.zeros_like(dg_ref)

        dh = jnp.dot(a_ref[...], b_ref[...], preferred_element_type=F32)
        dx, dgr = _rms_bwd_rows(x_ref[...], g_ref[...], dh)
        dx_ref[...] = dres_ref[...] + dx
        dg_ref[...] += jnp.sum(dgr, axis=0, keepdims=True)

    row = pl.BlockSpec((tm, d), lambda i: (i, 0))
    vec = pl.BlockSpec((1, d), lambda i: (0, 0))
    return pl.pallas_call(
        body, name=name, grid=(s // tm,),
        in_specs=[pl.BlockSpec((tm, k), lambda i: (i, 0)), pl.BlockSpec((k, d), lambda i: (0, 0)), row, vec, row,
                  pl.BlockSpec(memory_space=pl.ANY)],
        out_specs=[row, vec],
        out_shape=[jax.ShapeDtypeStruct((s, d), F32), jax.ShapeDtypeStruct((1, d), F32)],
        compiler_params=_params(("arbitrary",)),
    )(a, b, x, g, dres, dep)


def _final_loss(x, g, target, *, name, ts=256):
    s, d = x.shape
    ts = min(ts, s)

    def body(x_ref, g_ref, t_ref, loss_ref, dx_ref, dxb_ref, dg_ref):
        @pl.when(pl.program_id(0) == 0)
        def _():
            loss_ref[...] = jnp.zeros_like(loss_ref)
            dg_ref[...] = jnp.zeros_like(dg_ref)

        xv = x_ref[...]
        gv = g_ref[...]
        r = lax.rsqrt(jnp.mean(xv * xv, axis=-1, keepdims=True) + RMS_EPS)
        xh = xv * r
        err = xh * gv - t_ref[...]
        loss_ref[...] += 0.5 * jnp.sum(jnp.mean(err * err, axis=-1, keepdims=True), axis=0, keepdims=True)
        dy = err * (1.0 / d)
        dn = dy * gv
        dx = r * (dn - xh * jnp.mean(dn * xh, axis=-1, keepdims=True))
        dx_ref[...] = dx
        dxb_ref[...] = dx.astype(BF16)
        dg_ref[...] += jnp.sum(dy * xh, axis=0, keepdims=True)

    row = pl.BlockSpec((ts, d), lambda i: (i, 0))
    vec = pl.BlockSpec((1, d), lambda i: (0, 0))
    return pl.pallas_call(
        body, name=name, grid=(s // ts,),
        in_specs=[row, vec, row],
        out_specs=[pl.BlockSpec((1, 1), lambda i: (0, 0)), row, row, vec],
        out_shape=[jax.ShapeDtypeStruct((1, 1), F32), jax.ShapeDtypeStruct((s, d), F32),
                   jax.ShapeDtypeStruct((s, d), BF16), jax.ShapeDtypeStruct((1, d), F32)],
        compiler_params=_params(("arbitrary",)),
    )(x, g, target)


def _conv_taps(ext_ref, w_ref, n_taps, base, r0, rows, reverse=False):
    acc = None
    for k in range(n_taps):
        off = r0 + (base - k if reverse else base + k)
        term = w_ref[k:k + 1, :] * ext_ref[pl.ds(off, rows), :]
        acc = term if acc is None else acc + term
    return acc


def _shift_copies(src_ref, sh_ref, rows):
    for b in range(1, 8):
        sh_ref[b, 0:rows, :] = src_ref[pl.ds(b, rows), :]


def _shifted(src_ref, sh_ref, start, rows, off):
    a, b = divmod(off, 8)
    ref = src_ref if b == 0 else sh_ref.at[b]
    return ref[pl.ds(start + 8 * a, rows), :]


def _mix0_fwd(z, conv_a, ln_g, ln_b, conv_b, *, name, ts=256, rc=32):
    s = z.shape[0]
    c = A_WIDTH
    hb = HALO_A

    def body(z_ref, zp_ref, wa_ref, lg_ref, lb_ref, wb_ref, ab_ref, ca_ref, exta, extb, sha):
        keep = jnp.where(pl.program_id(0) > 0, 1.0, 0.0)
        zp = zp_ref[...]
        exta[0:hb, :] = zp[:, 0:c] * _sigmoid(zp[:, c:2 * c]) * keep
        extb[0:hb, :] = zp[:, 3 * c:4 * c] * zp[:, 4 * c:5 * c] * keep
        exta[hb:hb + ts, :] = z_ref[:, 0:c] * _sigmoid(z_ref[:, c:2 * c])
        extb[hb:hb + ts, :] = z_ref[:, 3 * c:4 * c] * z_ref[:, 4 * c:5 * c]
        _shift_copies(exta, sha, hb + ts - 8)
        lg = lg_ref[...]
        lb = lb_ref[...]
        for q in range(ts // rc):
            r0 = q * rc
            ca = None
            for k in range(A_TAPS):
                term = wa_ref[k:k + 1, :] * _shifted(exta, sha, r0, rc, hb - (A_TAPS - 1) + k)
                ca = term if ca is None else ca + term
            ca_ref[r0:r0 + rc, :] = ca
            mu = jnp.mean(ca, axis=-1, keepdims=True)
            xc = ca - mu
            rs = lax.rsqrt(jnp.mean(xc * xc, axis=-1, keepdims=True) + LN_EPS)
            l = xc * rs * lg + lb
            ab_ref[r0:r0 + rc, 0:c] = (l * _sigmoid(l)).astype(BF16)
            cbc = _conv_taps(extb, wb_ref, 3, hb - 2, r0, rc)
            ab_ref[r0:r0 + rc, c:2 * c] = (z_ref[r0:r0 + rc, 2 * c:3 * c] * cbc).astype(BF16)

    return pl.pallas_call(
        body, name=name, grid=(s // ts,),
        in_specs=[pl.BlockSpec((ts, 5 * c), lambda i: (i, 0)),
                  pl.BlockSpec((hb, 5 * c), lambda i: (_prev_blk(i, ts, hb), 0)),
                  pl.BlockSpec((32, c), lambda i: (0, 0)),
                  pl.BlockSpec((1, c), lambda i: (0, 0)),
                  pl.BlockSpec((1, c), lambda i: (0, 0)),
                  pl.BlockSpec((8, c), lambda i: (0, 0))],
        out_specs=[pl.BlockSpec((ts, 2 * c), lambda i: (i, 0)),
                   pl.BlockSpec((ts, c), lambda i: (i, 0))],
        out_shape=[jax.ShapeDtypeStruct((s, 2 * c), BF16), jax.ShapeDtypeStruct((s, c), F32)],
        scratch_shapes=[pltpu.VMEM((hb + ts, c), F32), pltpu.VMEM((hb + ts, c), F32),
                        pltpu.VMEM((8, hb + ts - 8, c), F32)],
        compiler_params=_params(("parallel",)),
    )(z, z, conv_a, ln_g, ln_b, conv_b)


def _mix0_bwd(z, ca, dab, conv_a, ln_g, ln_b, conv_b, *, name, ts=256, rc=32):
    s = z.shape[0]
    c = A_WIDTH
    hb = HALO_A
    ta = A_TAPS

    def body(z_ref, zp_ref, zn_ref, ca_ref, can_ref, d_ref, dn_ref, wa_ref, lg_ref, lb_ref, wb_ref,
             dz_ref, dwa_ref, dwb_ref, dlg_ref, dlb_ref, exta, extb, extdca, extdcb, shd):
        i = pl.program_id(0)
        keep_p = jnp.where(i > 0, 1.0, 0.0)
        keep_n = jnp.where(i < s // ts - 1, 1.0, 0.0)

        @pl.when(i == 0)
        def _():
            dwa_ref[...] = jnp.zeros_like(dwa_ref)
            dwb_ref[...] = jnp.zeros_like(dwb_ref)
            dlg_ref[...] = jnp.zeros_like(dlg_ref)
            dlb_ref[...] = jnp.zeros_like(dlb_ref)

        lg = lg_ref[...]
        lb = lb_ref[...]
        zp = zp_ref[...]
        exta[0:hb, :] = zp[:, 0:c] * _sigmoid(zp[:, c:2 * c]) * keep_p
        extb[0:hb, :] = zp[:, 3 * c:4 * c] * zp[:, 4 * c:5 * c] * keep_p
        exta[hb:hb + ts, :] = z_ref[:, 0:c] * _sigmoid(z_ref[:, c:2 * c])
        extb[hb:hb + ts, :] = z_ref[:, 3 * c:4 * c] * z_ref[:, 4 * c:5 * c]

        def ln_bwd(cav, dav):
            mu = jnp.mean(cav, axis=-1, keepdims=True)
            xc = cav - mu
            rs = lax.rsqrt(jnp.mean(xc * xc, axis=-1, keepdims=True) + LN_EPS)
            nv = xc * rs
            l = nv * lg + lb
            sg = _sigmoid(l)
            dl = dav * (sg * (1.0 + l * (1.0 - sg)))
            dnv = dl * lg
            dca = rs * (dnv - jnp.mean(dnv, axis=-1, keepdims=True)
                        - nv * jnp.mean(dnv * nv, axis=-1, keepdims=True))
            return dca, dl, nv

        dlg_acc = jnp.zeros((1, c), F32)
        dlb_acc = jnp.zeros((1, c), F32)
        for q in range(ts // rc):
            r0 = q * rc
            dca, dl, nv = ln_bwd(ca_ref[r0:r0 + rc, :], d_ref[r0:r0 + rc, 0:c])
            extdca[r0:r0 + rc, :] = dca
            dlg_acc = dlg_acc + jnp.sum(dl * nv, axis=0, keepdims=True)
            dlb_acc = dlb_acc + jnp.sum(dl, axis=0, keepdims=True)
            extdcb[r0:r0 + rc, :] = d_ref[r0:r0 + rc, c:2 * c] * z_ref[r0:r0 + rc, 2 * c:3 * c]
        dca_n, _, _ = ln_bwd(can_ref[...], dn_ref[:, 0:c])
        extdca[ts:ts + hb, :] = dca_n * keep_n
        extdcb[ts:ts + hb, :] = dn_ref[:, c:2 * c] * zn_ref[:, 2 * c:3 * c] * keep_n
        dlg_ref[...] += dlg_acc
        dlb_ref[...] += dlb_acc
        _shift_copies(extdca, shd, ts + hb - 8)

        for q in range(ts // rc):
            r0 = q * rc
            zr = z_ref[r0:r0 + rc, :]
            dga = None
            for k in range(ta):
                term = wa_ref[k:k + 1, :] * _shifted(extdca, shd, r0, rc, ta - 1 - k)
                dga = term if dga is None else dga + term
            sg = _sigmoid(zr[:, c:2 * c])
            dz_ref[r0:r0 + rc, 0:c] = (dga * sg).astype(BF16)
            dz_ref[r0:r0 + rc, c:2 * c] = (dga * zr[:, 0:c] * sg * (1.0 - sg)).astype(BF16)
            cbc = _conv_taps(extb, wb_ref, 3, hb - 2, r0, rc)
            dz_ref[r0:r0 + rc, 2 * c:3 * c] = (d_ref[r0:r0 + rc, c:2 * c] * cbc).astype(BF16)
            dcb = _conv_taps(extdcb, wb_ref, 3, 2, r0, rc, reverse=True)
            dz_ref[r0:r0 + rc, 3 * c:4 * c] = (dcb * zr[:, 4 * c:5 * c]).astype(BF16)
            dz_ref[r0:r0 + rc, 4 * c:5 * c] = (dcb * zr[:, 3 * c:4 * c]).astype(BF16)

        for k in range(ta):
            part = None
            for q in range(ts // rc):
                r0 = q * rc
                p = exta[hb + r0:hb + r0 + rc, :] * _shifted(extdca, shd, r0, rc, ta - 1 - k)
                for r in range(0, rc, 8):
                    part = p[r:r + 8, :] if part is None else part + p[r:r + 8, :]
            dwa_ref[k:k + 1, :] += jnp.sum(part, axis=0, keepdims=True)
        dcb_t = extdcb[0:ts, :]
        for k in range(3):
            dwb_ref[k:k + 1, :] += jnp.sum(dcb_t * extb[pl.ds(hb - 2 + k, ts), :], axis=0, keepdims=True)

    def tile(w):
        return pl.BlockSpec((ts, w), lambda i: (i, 0))

    def prev(w):
        return pl.BlockSpec((hb, w), lambda i: (_prev_blk(i, ts, hb), 0))

    def nxt(w):
        return pl.BlockSpec((hb, w), lambda i: (_next_blk(i, ts, hb, s), 0))

    def const(r, w):
        return pl.BlockSpec((r, w), lambda i: (0, 0))

    return pl.pallas_call(
        body, name=name, grid=(s // ts,),
        in_specs=[tile(5 * c), prev(5 * c), nxt(5 * c), tile(c), nxt(c), tile(2 * c), nxt(2 * c),
                  const(32, c), const(1, c), const(1, c), const(8, c)],
        out_specs=[tile(5 * c), const(32, c), const(8, c), const(1, c), const(1, c)],
        out_shape=[jax.ShapeDtypeStruct((s, 5 * c), BF16), jax.ShapeDtypeStruct((32, c), F32),
                   jax.ShapeDtypeStruct((8, c), F32), jax.ShapeDtypeStruct((1, c), F32),
                   jax.ShapeDtypeStruct((1, c), F32)],
        scratch_shapes=[pltpu.VMEM((hb + ts, c), F32), pltpu.VMEM((hb + ts, c), F32),
                        pltpu.VMEM((ts + hb, c), F32), pltpu.VMEM((ts + hb, c), F32),
                        pltpu.VMEM((8, ts + hb - 8, c), F32)],
        compiler_params=_params(("arbitrary",)),
    )(z, z, z, ca, ca, dab, dab, conv_a, ln_g, ln_b, conv_b)


def _ffn_up(x, g, w8, *, name, tm=1024):
    s, d = x.shape
    nb, c, _ = w8.shape
    tm = min(tm, s)

    def body(x_ref, g_ref, w_ref, h_ref, u_ref, hs_ref):
        @pl.when(pl.program_id(1) == 0)
        def _():
            hv = _rms_rows(x_ref[...], g_ref[...]).astype(BF16)
            hs_ref[...] = hv
            h_ref[...] = hv

        u_ref[...] = lax.dot_general(hs_ref[...], w_ref[...], _NT, preferred_element_type=F32).astype(BF16)

    return pl.pallas_call(
        body, name=name, grid=(s // tm, nb),
        in_specs=[pl.BlockSpec((tm, d), lambda i, k: (i, 0)),
                  pl.BlockSpec((1, d), lambda i, k: (0, 0)),
                  pl.BlockSpec((None, c, d), lambda i, k: (k, 0, 0))],
        out_specs=[pl.BlockSpec((tm, d), lambda i, k: (i, 0)),
                   pl.BlockSpec((None, tm, c), lambda i, k: (k, i, 0))],
        out_shape=[jax.ShapeDtypeStruct((s, d), BF16), jax.ShapeDtypeStruct((nb, s, c), BF16)],
        scratch_shapes=[pltpu.VMEM((tm, d), BF16)],
        compiler_params=_params(("parallel", "arbitrary")),
    )(x, g, w8)


def _ffn_mid(up8, wc8, *, name, ts=512, rc=32):
    nb, s, c = up8.shape
    hb = HALO_S
    ts = min(ts, s)

    def body(g_ref, gp_ref, v_ref, vp_ref, wg_ref, wv_ref, act_ref, ug_ref, uv_ref, extg, extv):
        keep = jnp.where(pl.program_id(0) > 0, 1.0, 0.0)
        extg[0:hb, :] = gp_ref[...].astype(F32) * keep
        extv[0:hb, :] = vp_ref[...].astype(F32) * keep
        extg[hb:hb + ts, :] = g_ref[...].astype(F32)
        extv[hb:hb + ts, :] = v_ref[...].astype(F32)
        for q in range(ts // rc):
            r0 = q * rc
            gg = _conv_taps(extg, wg_ref, 3, hb - 2, r0, rc)
            vv = _conv_taps(extv, wv_ref, 3, hb - 2, r0, rc)
            ug_ref[r0:r0 + rc, :] = gg.astype(BF16)
            uv_ref[r0:r0 + rc, :] = vv.astype(BF16)
            act_ref[r0:r0 + rc, :] = (gg * _sigmoid(gg) * vv).astype(BF16)

    def blk(rows, off, halo):
        if halo:
            return pl.BlockSpec((None, rows, c), lambda i, j: (j + off, _prev_blk(i, ts, hb), 0))
        return pl.BlockSpec((None, rows, c), lambda i, j: (j + off, i, 0))

    def taps(off):
        return pl.BlockSpec((None, 8, c), lambda i, j: (j + off, 0, 0))

    tile = pl.BlockSpec((None, ts, c), lambda i, j: (j, i, 0))
    out = jax.ShapeDtypeStruct((N_PAIR, s, c), BF16)
    return pl.pallas_call(
        body, name=name, grid=(s // ts, N_PAIR),
        in_specs=[blk(ts, 0, False), blk(hb, 0, True), blk(ts, N_PAIR, False), blk(hb, N_PAIR, True),
                  taps(0), taps(N_PAIR)],
        out_specs=[tile, tile, tile], out_shape=[out, out, out],
        scratch_shapes=[pltpu.VMEM((hb + ts, c), F32), pltpu.VMEM((hb + ts, c), F32)],
        compiler_params=_params(("parallel", "parallel")),
    )(up8, up8, up8, up8, wc8, wc8)


def _ffn_down(act4, wd4, x, *, name, tm=512):
    npair, s, c = act4.shape
    d = x.shape[1]
    tm = min(tm, s)

    def body(a_ref, w_ref, x_ref, o_ref):
        acc = x_ref[...]
        for j in range(npair):
            acc = acc + jnp.dot(a_ref[j], w_ref[j], preferred_element_type=F32)
        o_ref[...] = acc

    return pl.pallas_call(
        body, name=name, grid=(s // tm,),
        in_specs=[pl.BlockSpec((npair, tm, c), lambda i: (0, i, 0)),
                  pl.BlockSpec((npair, c, d), lambda i: (0, 0, 0)),
                  pl.BlockSpec((tm, d), lambda i: (i, 0))],
        out_specs=pl.BlockSpec((tm, d), lambda i: (i, 0)),
        out_shape=jax.ShapeDtypeStruct((s, d), F32),
        compiler_params=_params(("parallel",)),
    )(act4, wd4, x)


def _ffn_dact(db, wd4, *, name, tm=1024):
    s, d = db.shape
    npair, c, _ = wd4.shape
    tm = min(tm, s)

    def body(d_ref, w_ref, o_ref):
        o_ref[...] = lax.dot_general(d_ref[...], w_ref[...], (((1,), (1,)), ((), ())),
                                     preferred_element_type=F32).astype(BF16)

    return pl.pallas_call(
        body, name=name, grid=(s // tm, npair),
        in_specs=[pl.BlockSpec((tm, d), lambda i, j: (i, 0)),
                  pl.BlockSpec((None, c, d), lambda i, j: (j, 0, 0))],
        out_specs=pl.BlockSpec((None, tm, c), lambda i, j: (j, i, 0)),
        out_shape=jax.ShapeDtypeStruct((npair, s, c), BF16),
        compiler_params=_params(("parallel", "parallel")),
    )(db, wd4)


def _ffn_dwdown(act4, db, *, name):
    npair, s, c = act4.shape
    d = db.shape[1]

    def body(a_ref, d_ref, o_ref):
        o_ref[...] = lax.dot_general(a_ref[...], d_ref[...], (((0,), (0,)), ((), ())),
                                     preferred_element_type=F32).astype(BF16)

    return pl.pallas_call(
        body, name=name, grid=(npair,),
        in_specs=[pl.BlockSpec((None, s, c), lambda j: (j, 0, 0)),
                  pl.BlockSpec((s, d), lambda j: (0, 0))],
        out_specs=pl.BlockSpec((None, c, d), lambda j: (j, 0, 0)),
        out_shape=jax.ShapeDtypeStruct((npair, c, d), BF16),
        compiler_params=_params(("parallel",)),
    )(act4, db)


def _ffn_midbwd(up8, ug4, uv4, dact4, wc8, *, name, ts=512, rc=32):
    nb, s, c = up8.shape
    hb = HALO_S
    ts = min(ts, s)
    n_i = s // ts

    def body(pg_ref, pv_ref, ug_ref, ugn_ref, uv_ref, uvn_ref, d_ref, dn_ref, wg_ref, wv_ref,
             dg_ref, dv_ref, dwg_ref, dwv_ref, extdg, extdv):
        i = pl.program_id(1)
        keep_n = jnp.where(i < n_i - 1, 1.0, 0.0)

        @pl.when(i == 0)
        def _():
            dwg_ref[...] = jnp.zeros_like(dwg_ref)
            dwv_ref[...] = jnp.zeros_like(dwv_ref)

        def du_rows(r0, rows, gg, vv, dav):
            sg = _sigmoid(gg)
            extdg[r0:r0 + rows, :] = dav * vv * (sg * (1.0 + gg * (1.0 - sg)))
            extdv[r0:r0 + rows, :] = dav * (gg * sg)

        for q in range(ts // rc):
            rows = slice(q * rc, q * rc + rc)
            du_rows(q * rc, rc, ug_ref[rows, :].astype(F32), uv_ref[rows, :].astype(F32),
                    d_ref[rows, :].astype(F32))
        du_rows(ts, 8, ugn_ref[0:8, :].astype(F32), uvn_ref[0:8, :].astype(F32),
                dn_ref[0:8, :].astype(F32) * keep_n)

        def fold(p):
            acc = p[0:8, :]
            for r in range(8, rc, 8):
                acc = acc + p[r:r + 8, :]
            return acc

        for extd, p_ref, w_ref, out_ref, dw_ref in ((extdg, pg_ref, wg_ref, dg_ref, dwg_ref),
                                                    (extdv, pv_ref, wv_ref, dv_ref, dwv_ref)):
            part = [None, None, None]
            for q in range(ts // rc):
                r0 = q * rc
                pre = p_ref[r0:r0 + rc, :].astype(F32)
                dup = None
                for k in range(3):
                    sh = extd[pl.ds(r0 + 2 - k, rc), :]
                    term = w_ref[k:k + 1, :] * sh
                    dup = term if dup is None else dup + term
                    prod = fold(pre * sh)
                    part[k] = prod if part[k] is None else part[k] + prod
                out_ref[r0:r0 + rc, :] = dup.astype(BF16)
            for k in range(3):
                dw_ref[k:k + 1, :] += jnp.sum(part[k], axis=0, keepdims=True)

    def blk(off, nxt):
        if nxt:
            return pl.BlockSpec((None, hb, c), lambda j, i: (j + off, _next_blk(i, ts, hb, s), 0))
        return pl.BlockSpec((None, ts, c), lambda j, i: (j + off, i, 0))

    def taps(off):
        return pl.BlockSpec((None, 8, c), lambda j, i: (j + off, 0, 0))

    tile = blk(0, False)
    acc = pl.BlockSpec((None, 8, c), lambda j, i: (j, 0, 0))
    return pl.pallas_call(
        body, name=name, grid=(N_PAIR, n_i),
        in_specs=[tile, blk(N_PAIR, False), tile, blk(0, True), tile, blk(0, True), tile, blk(0, True),
                  taps(0), taps(N_PAIR)],
        out_specs=[tile, tile, acc, acc],
        out_shape=[jax.ShapeDtypeStruct((N_PAIR, s, c), BF16), jax.ShapeDtypeStruct((N_PAIR, s, c), BF16),
                   jax.ShapeDtypeStruct((N_PAIR, 8, c), F32), jax.ShapeDtypeStruct((N_PAIR, 8, c), F32)],
        scratch_shapes=[pltpu.VMEM((ts + 8, c), F32), pltpu.VMEM((ts + 8, c), F32)],
        compiler_params=_params(("parallel", "arbitrary")),
    )(up8, up8, ug4, ug4, uv4, uv4, dact4, dact4, wc8, wc8)


def _ffn_dh(dupg, dupv, w8, x, g, dres, *, name, tm=256):
    npair, s, c = dupg.shape
    d = x.shape[1]
    tm = min(tm, s)

    def body(dg_ref, dv_ref, w_ref, x_ref, g_ref, dres_ref, dx_ref, dxb_ref, dgain_ref):
        @pl.when(pl.program_id(0) == 0)
        def _():
            dgain_ref[...] = jnp.zeros_like(dgain_ref)

        dh = None
        for j in range(npair):
            for src, k in ((dg_ref, j), (dv_ref, j + npair)):
                part = jnp.dot(src[j], w_ref[k], preferred_element_type=F32)
                dh = part if dh is None else dh + part
        dx, dgr = _rms_bwd_rows(x_ref[...], g_ref[...], dh)
        dx = dres_ref[...] + dx
        dx_ref[...] = dx
        dxb_ref[...] = dx.astype(BF16)
        dgain_ref[...] += jnp.sum(dgr, axis=0, keepdims=True)

    row = pl.BlockSpec((tm, d), lambda i: (i, 0))
    vec = pl.BlockSpec((1, d), lambda i: (0, 0))
    dup = pl.BlockSpec((npair, tm, c), lambda i: (0, i, 0))
    return pl.pallas_call(
        body, name=name, grid=(s // tm,),
        in_specs=[dup, dup, pl.BlockSpec((2 * npair, c, d), lambda i: (0, 0, 0)), row, vec, row],
        out_specs=[row, row, vec],
        out_shape=[jax.ShapeDtypeStruct((s, d), F32), jax.ShapeDtypeStruct((s, d), BF16),
                   jax.ShapeDtypeStruct((1, d), F32)],
        compiler_params=_params(("arbitrary",)),
    )(dupg, dupv, w8, x, g, dres)


def _ffn_dwup(h, dupg, dupv, *, name, tm=512):
    npair, s, c = dupg.shape
    d = h.shape[1]

    def body(h_ref, dg_ref, dv_ref, o_ref):
        k = pl.program_id(1)

        @pl.when(k < npair)
        def _():
            o_ref[...] = lax.dot_general(dg_ref[...], h_ref[...], _TN, preferred_element_type=F32).astype(BF16)

        @pl.when(k >= npair)
        def _():
            o_ref[...] = lax.dot_general(dv_ref[...], h_ref[...], _TN, preferred_element_type=F32).astype(BF16)

    return pl.pallas_call(
        body, name=name, grid=(d // tm, 2 * npair),
        in_specs=[pl.BlockSpec((s, tm), lambda m, k: (0, m)),
                  pl.BlockSpec((None, s, c), lambda m, k: (jnp.minimum(k, npair - 1), 0, 0)),
                  pl.BlockSpec((None, s, c), lambda m, k: (jnp.maximum(k - npair, 0), 0, 0))],
        out_specs=pl.BlockSpec((None, c, tm), lambda m, k: (k, 0, m)),
        out_shape=jax.ShapeDtypeStruct((2 * npair, c, d), BF16),
        compiler_params=_params(("parallel", "arbitrary")),
    )(h, dupg, dupv)


def _pool_counts(i, ts, rows, window):
    t = lax.broadcasted_iota(jnp.int32, (rows, 1), 0) + i * ts + 1
    return jnp.minimum(t, window).astype(F32)


def _pool_fwd(x, g, *, name, ts=256):
    s, d = x.shape
    hb = HALO_S
    pg = POOL_GROUP
    ts = min(ts, s)

    def body(x_ref, xp_ref, g_ref, p_ref, ext):
        i = pl.program_id(0)
        keep = jnp.where(i > 0, 1.0, 0.0)
        gv = g_ref[...]
        ext[0:hb, :] = _rms_rows(xp_ref[...], gv) * keep
        ext[hb:hb + ts, :] = _rms_rows(x_ref[...], gv)
        for gi, w in enumerate(POOL_WINDOWS):
            cols = slice(gi * pg, (gi + 1) * pg)
            h = ext[hb:hb + ts, cols]
            acc = h
            for q in range(1, w):
                acc = acc + ext[pl.ds(hb - q, ts), cols]
            p_ref[:, cols] = (acc / _pool_counts(i, ts, ts, w) - h).astype(BF16)

    return pl.pallas_call(
        body, name=name, grid=(s // ts,),
        in_specs=[pl.BlockSpec((ts, d), lambda i: (i, 0)),
                  pl.BlockSpec((hb, d), lambda i: (_prev_blk(i, ts, hb), 0)),
                  pl.BlockSpec((1, d), lambda i: (0, 0))],
        out_specs=pl.BlockSpec((ts, d), lambda i: (i, 0)),
        out_shape=jax.ShapeDtypeStruct((s, d), BF16),
        scratch_shapes=[pltpu.VMEM((hb + ts, d), F32)],
        compiler_params=_params(("parallel",)),
    )(x, x, g)


def _pool_mm_fwd(p, w_pool, scale, x, *, name, ts=1024):
    s, d = x.shape
    pg = POOL_GROUP
    ts = min(ts, s)

    def body(p_ref, w_ref, s_ref, x_ref, o_ref, yu_ref):
        yu = jnp.dot(p_ref[...], w_ref[...], preferred_element_type=F32)
        yu_ref[...] = yu.astype(BF16)
        o_ref[...] = x_ref[...] + yu * s_ref[...]

    blk = pl.BlockSpec((ts, pg), lambda i, gi: (i, gi))
    return pl.pallas_call(
        body, name=name, grid=(s // ts, d // pg),
        in_specs=[blk, pl.BlockSpec((None, pg, pg), lambda i, gi: (gi, 0, 0)),
                  pl.BlockSpec((1, pg), lambda i, gi: (0, gi)), blk],
        out_specs=[blk, blk],
        out_shape=[jax.ShapeDtypeStruct((s, d), F32), jax.ShapeDtypeStruct((s, d), BF16)],
        compiler_params=_params(("parallel", "parallel")),
    )(p, w_pool, scale, x)


def _pool_mm_bwd(dres, w_pool, scale, yu, dep, *, name, ts=1024):
    s, d = dres.shape
    pg = POOL_GROUP
    ts = min(ts, s)

    def body(d_ref, w_ref, s_ref, yu_ref, dep_ref, dyc_ref, dp_ref, ds_ref):
        @pl.when(pl.program_id(1) == 0)
        def _():
            ds_ref[...] = jnp.zeros_like(ds_ref)

        dv = d_ref[...]
        dyc = (dv * s_ref[...]).astype(BF16)
        dyc_ref[...] = dyc
        dp_ref[...] = lax.dot_general(dyc, w_ref[...], (((1,), (1,)), ((), ())), preferred_element_type=F32)
        ds_ref[...] += jnp.sum(dv * yu_ref[...].astype(F32), axis=0, keepdims=True)

    blk = pl.BlockSpec((ts, pg), lambda gi, i: (i, gi))
    vec = pl.BlockSpec((1, pg), lambda gi, i: (0, gi))
    return pl.pallas_call(
        body, name=name, grid=(d // pg, s // ts),
        in_specs=[blk, pl.BlockSpec((None, pg, pg), lambda gi, i: (gi, 0, 0)), vec, blk,
                  pl.BlockSpec(memory_space=pl.ANY)],
        out_specs=[blk, blk, vec],
        out_shape=[jax.ShapeDtypeStruct((s, d), BF16), jax.ShapeDtypeStruct((s, d), F32),
                   jax.ShapeDtypeStruct((1, d), F32)],
        compiler_params=_params(("parallel", "arbitrary")),
    )(dres, w_pool, scale, yu, dep)


def _pool_dw(p, dyc, *, name):
    s, d = p.shape
    pg = POOL_GROUP

    def body(p_ref, d_ref, o_ref):
        o_ref[...] = lax.dot_general(p_ref[...], d_ref[...], (((0,), (0,)), ((), ())),
                                     preferred_element_type=F32).astype(BF16)

    blk = pl.BlockSpec((s, pg), lambda gi: (0, gi))
    return pl.pallas_call(
        body, name=name, grid=(d // pg,),
        in_specs=[blk, blk], out_specs=pl.BlockSpec((None, pg, pg), lambda gi: (gi, 0, 0)),
        out_shape=jax.ShapeDtypeStruct((d // pg, pg, pg), BF16),
        compiler_params=_params(("parallel",)),
    )(p, dyc)


def _pool_bwd(dp, x, g, dres, *, name, ts=256):
    s, d = x.shape
    hb = HALO_S
    pg = POOL_GROUP
    ts = min(ts, s)
    n_i = s // ts

    def body(dp_ref, dpn_ref, x_ref, g_ref, dres_ref, dx_ref, dxb_ref, dg_ref, ext, dh):
        i = pl.program_id(0)
        keep_n = jnp.where(i < n_i - 1, 1.0, 0.0)

        @pl.when(i == 0)
        def _():
            dg_ref[...] = jnp.zeros_like(dg_ref)

        for gi, w in enumerate(POOL_WINDOWS):
            cols = slice(gi * pg, (gi + 1) * pg)
            ext[0:ts, cols] = dp_ref[:, cols] / _pool_counts(i, ts, ts, w)
            ext[ts:ts + hb, cols] = dpn_ref[:, cols] / _pool_counts(i + 1, ts, hb, w) * keep_n
        for gi, w in enumerate(POOL_WINDOWS):
            cols = slice(gi * pg, (gi + 1) * pg)
            acc = ext[0:ts, cols]
            for q in range(1, w):
                acc = acc + ext[pl.ds(q, ts), cols]
            dh[:, cols] = acc - dp_ref[:, cols]
        dx, dgr = _rms_bwd_rows(x_ref[...], g_ref[...], dh[...])
        dx = dres_ref[...] + dx
        dx_ref[...] = dx
        dxb_ref[...] = dx.astype(BF16)
        dg_ref[...] += jnp.sum(dgr, axis=0, keepdims=True)

    row = pl.BlockSpec((ts, d), lambda i: (i, 0))
    vec = pl.BlockSpec((1, d), lambda i: (0, 0))
    return pl.pallas_call(
        body, name=name, grid=(n_i,),
        in_specs=[row, pl.BlockSpec((hb, d), lambda i: (_next_blk(i, ts, hb, s), 0)), row, vec, row],
        out_specs=[row, row, vec],
        out_shape=[jax.ShapeDtypeStruct((s, d), F32), jax.ShapeDtypeStruct((s, d), BF16),
                   jax.ShapeDtypeStruct((1, d), F32)],
        scratch_shapes=[pltpu.VMEM((ts + hb, d), F32), pltpu.VMEM((ts, d), F32)],
        compiler_params=_params(("arbitrary",)),
    )(dp, dp, x, g, dres)


def _pad_rows(w, rows):
    pad = [(0, 0)] * (w.ndim - 2) + [(0, rows - w.shape[-2]), (0, 0)]
    return jnp.pad(w, pad)


def _ffn_layer_fwd(x, nf, w8, wc8, wd4, tag):
    h, up8 = _ffn_up(x, nf, w8, name=f"ffn{tag}_up")
    act4, ug4, uv4 = _ffn_mid(up8, wc8, name=f"ffn{tag}_mid")
    x_out = _ffn_down(act4, wd4, x, name=f"ffn{tag}_down")
    return x_out, (h, up8, ug4, uv4, act4)


def _ffn_layer_bwd(d, db, x, nf, w8, wc8, wd4, saved, tag):
    h, up8, ug4, uv4, act4 = saved
    dact4 = _ffn_dact(db, wd4, name=f"ffn{tag}_dact")
    dwd4 = _ffn_dwdown(act4, db, name=f"ffn{tag}_dwdown")
    dupg, dupv, dwg, dwv = _ffn_midbwd(up8, ug4, uv4, dact4, wc8, name=f"ffn{tag}_midbwd")
    dx, dxb, dnf = _ffn_dh(dupg, dupv, w8, x, nf, d, name=f"ffn{tag}_dh")
    dw8 = _ffn_dwup(h, dupg, dupv, name=f"ffn{tag}_dwup")
    dwc8 = jnp.concatenate([dwg, dwv], axis=0)[:, :3]
    return dx, dxb, dnf, dw8, dwc8, dwd4


def _local_step(x, target, weights, grads_ready):
    w0 = weights("mix0", x)
    wa = _pad_rows(w0["conv_a"], 32)
    wb = _pad_rows(w0["conv_b"], 8)
    wc = [_pad_rows(w0["conv_ffn"][l], 8) for l in range(2)]
    h0, z = _rms_mm(x, w0["norm_mix_even"], w0["w_in_t"], name="mix0_in", out_dtype=F32, tn=1280)
    ab, ca = _mix0_fwd(z, wa, w0["ln_a_g"], w0["ln_a_b"], wb, name="mix0_mid")
    x1 = _mm(ab, w0["w_out"], add=x, name="mix0_out", tm=1024, tn=1024)
    w1 = weights("ffn0", x1)
    x2, ffn0 = _ffn_layer_fwd(x1, w0["norm_ffn"][0:1], w1["w_up"], wc[0], w1["w_down"], 0)
    w2 = weights("ffn1", x2)
    p = _pool_fwd(x2, w0["norm_mix_odd"], name="pool_mid")
    x3, yu = _pool_mm_fwd(p, w2["w_pool"], w0["pool_scale"], x2, name="pool_mm")
    x4, ffn1 = _ffn_layer_fwd(x3, w0["norm_ffn"][1:2], w2["w_up"], wc[1], w2["w_down"], 1)
    loss, d4, d4b, g_norm_final = _final_loss(x4, w0["norm_final"], target, name="final_loss")

    d3, d3b, g_nf1, g_up1, g_wc1, g_down1 = _ffn_layer_bwd(
        d4, d4b, x3, w0["norm_ffn"][1:2], w2["w_up"], wc[1], w2["w_down"], ffn1, 1)
    sent1 = grads_ready("ffn1", {"w_up1": g_up1, "w_down1": g_down1})
    dyc, dp, g_scale = _pool_mm_bwd(d3, w2["w_pool"], w0["pool_scale"], yu, sent1, name="pool_mm_bwd")
    g_pool = _pool_dw(p, dyc, name="pool_dw")
    d2, d2b, g_nmo = _pool_bwd(dp, x2, w0["norm_mix_odd"], d3, name="pool_midbwd")
    d1, d1b, g_nf0, g_up0, g_wc0, g_down0 = _ffn_layer_bwd(
        d2, d2b, x1, w0["norm_ffn"][0:1], w1["w_up"], wc[0], w1["w_down"], ffn0, 0)
    sent0 = grads_ready("ffn0", {"w_pool": g_pool, "w_up0": g_up0, "w_down0": g_down0})
    dab = _mm(d1b, w0["w_out"], tb=True, dep=sent0, name="mix0_dab", tm=1024, tn=1024)
    g_out = _mm(ab, d1b, ta=True, out_dtype=BF16, name="mix0_dwout", tm=1024, tn=512)
    dz, g_wa, g_wb, g_lg, g_lb = _mix0_bwd(z, ca, dab, wa, w0["ln_a_g"], w0["ln_a_b"], wb, name="mix0_midbwd")
    g_in = _mm(dz, h0, ta=True, out_dtype=BF16, name="mix0_dwin", tm=512, tn=1024)
    sent_mix = grads_ready("mix0", {"w_in": g_in, "w_out": g_out})
    dx, g_nme = _mm_rms_bwd(dz, w0["w_in_t"], x, w0["norm_mix_even"], d1, sent_mix, name="mix0_dh")

    small = {
        "norm_mix_even": g_nme, "conv_a": g_wa[:A_TAPS], "ln_a_g": g_lg, "ln_a_b": g_lb, "conv_b": g_wb[:3],
        "norm_mix_odd": g_nmo, "pool_scale": g_scale, "norm_ffn": jnp.concatenate([g_nf0, g_nf1], axis=0),
        "conv_ffn": [g_wc0, g_wc1], "norm_final": g_norm_final,
    }
    return loss[0, 0], dx, small


def _my_pos():
    return lax.axis_index("x"), lax.axis_index("y"), lax.axis_index("c")


def _flip(pos, r):
    x, y, c = pos
    return (1 - x if r & 4 else x, 1 - y if r & 2 else y, 1 - c if r & 1 else c)


def _dev_index(pos):
    return 4 * pos[0] + 2 * pos[1] + pos[2]


_HBM = pl.BlockSpec(memory_space=pltpu.HBM)
_SEM = pl.BlockSpec(memory_space=pltpu.SEMAPHORE)
_EFFECT = pltpu.SideEffectType.DATAFLOW_SIDE_EFFECTING


def _exchange_copy(ins, lands, send_sems, recv_sems, scatter, pos, a, r, receiving):
    me = _dev_index(pos)
    peer = _flip(pos, r)
    dest = _dev_index(pos) if receiving else _dev_index(peer)
    src = ins[a].at[dest] if scatter[a] else ins[a]
    slot = _dev_index(peer) if receiving else me
    return pltpu.make_async_remote_copy(
        src_ref=src, dst_ref=lands[a].at[slot], send_sem=send_sems.at[a * (N_DEV - 1) + r - 1],
        recv_sem=recv_sems.at[a * (N_DEV - 1) + r - 1],
        device_id=peer, device_id_type=pl.DeviceIdType.MESH)


ALL_PEERS = tuple(range(1, N_DEV))
CHIP_PEERS = (1, 2, 4, 6)
FORWARDED = (2, 4, 6)


def _exchange_start(arrays, scatter, after, *, name, peers=ALL_PEERS):
    n = len(arrays)
    me = _dev_index(_my_pos())
    lands = []
    for arr, sc in zip(arrays, scatter):
        own = lax.dynamic_index_in_dim(arr, me, 0, keepdims=True) if sc else arr[None]
        shape = arr.shape if sc else (N_DEV,) + arr.shape
        lands.append(lax.dynamic_update_slice(lax.empty(shape, arr.dtype), own, (me,) + (0,) * (len(shape) - 1)))

    def body(*refs):
        ins, lnd = refs[:n], refs[n:2 * n]
        send_sems, recv_sems = refs[2 * n + 1], refs[2 * n + 2]
        token = refs[-1]
        pos = _my_pos()
        for a in range(n):
            for r in peers:
                _exchange_copy(ins, lnd, send_sems, recv_sems, scatter, pos, a, r, receiving=False).start()
        token[...] = jnp.zeros_like(token)

    bufs = [pltpu.with_memory_space_constraint(t, pltpu.HBM) for t in list(arrays) + lands]
    sems = pltpu.SemaphoreType.DMA((n * (N_DEV - 1),))
    res = pl.pallas_call(
        body, name=name,
        out_shape=(sems, sems, *[pltpu.HBM(t.shape, t.dtype) for t in bufs], jax.ShapeDtypeStruct((8, 128), F32)),
        in_specs=[_HBM] * (2 * n) + [pl.BlockSpec(memory_space=pl.ANY)],
        out_specs=(_SEM, _SEM, *[_HBM] * (2 * n), pl.BlockSpec(memory_space=pltpu.VMEM)),
        input_output_aliases={i: 2 + i for i in range(2 * n)},
        compiler_params=pltpu.CompilerParams(has_side_effects=_EFFECT),
    )(*bufs, after)
    return res[0], res[1], list(res[2:2 + n]), list(res[2 + n:2 + 2 * n]), res[-1]


def _exchange_wait(started, scatter, after, *, name, peers=ALL_PEERS):
    send_sems, recv_sems, arrays, lands, _ = started
    n = len(arrays)

    def body(*refs):
        ins, lnd = refs[:n], refs[n:2 * n]
        send_sems, recv_sems = refs[2 * n], refs[2 * n + 1]
        pos = _my_pos()
        for a in range(n):
            for r in peers:
                _exchange_copy(ins, lnd, send_sems, recv_sems, scatter, pos, a, r, receiving=False).wait_send()
                _exchange_copy(ins, lnd, send_sems, recv_sems, scatter, pos, a, r, receiving=True).wait_recv()

    bufs = list(arrays) + list(lands)
    res = pl.pallas_call(
        body, name=name,
        out_shape=tuple(pltpu.HBM(t.shape, t.dtype) for t in bufs),
        in_specs=[_HBM] * (2 * n) + [_SEM, _SEM, pl.BlockSpec(memory_space=pl.ANY)],
        out_specs=tuple([_HBM] * (2 * n)),
        input_output_aliases={i: i for i in range(2 * n)},
        compiler_params=pltpu.CompilerParams(has_side_effects=_EFFECT),
    )(*bufs, send_sems, recv_sems, after)
    return list(res[n:])


def _forward_copy(lands, send_sems, recv_sems, pos, a, q, receiving):
    slot = _dev_index(_flip(pos, q ^ 1 if receiving else q))
    idx = a * len(FORWARDED) + FORWARDED.index(q)
    return pltpu.make_async_remote_copy(
        src_ref=lands[a].at[slot], dst_ref=lands[a].at[slot], send_sem=send_sems.at[idx], recv_sem=recv_sems.at[idx],
        device_id=_flip(pos, 1), device_id_type=pl.DeviceIdType.MESH)


def _forward_start(lands, after, *, name):
    n = len(lands)

    def body(*refs):
        lnd = refs[:n]
        send_sems, recv_sems = refs[n + 1], refs[n + 2]
        token = refs[-1]
        pos = _my_pos()
        for a in range(n):
            for q in FORWARDED:
                _forward_copy(lnd, send_sems, recv_sems, pos, a, q, receiving=False).start()
        token[...] = jnp.zeros_like(token)

    sems = pltpu.SemaphoreType.DMA((n * len(FORWARDED),))
    res = pl.pallas_call(
        body, name=name,
        out_shape=(sems, sems, *[pltpu.HBM(t.shape, t.dtype) for t in lands], jax.ShapeDtypeStruct((8, 128), F32)),
        in_specs=[_HBM] * n + [pl.BlockSpec(memory_space=pl.ANY)],
        out_specs=(_SEM, _SEM, *[_HBM] * n, pl.BlockSpec(memory_space=pltpu.VMEM)),
        input_output_aliases={i: 2 + i for i in range(n)},
        compiler_params=pltpu.CompilerParams(has_side_effects=_EFFECT),
    )(*lands, after)
    return res[0], res[1], list(res[2:2 + n]), res[-1]


def _forward_wait(forwarded, after, *, name):
    send_sems, recv_sems, lands, _ = forwarded
    n = len(lands)

    def body(*refs):
        lnd = refs[:n]
        send_sems, recv_sems = refs[n], refs[n + 1]
        pos = _my_pos()
        for a in range(n):
            for q in FORWARDED:
                _forward_copy(lnd, send_sems, recv_sems, pos, a, q, receiving=False).wait_send()
                _forward_copy(lnd, send_sems, recv_sems, pos, a, q, receiving=True).wait_recv()

    res = pl.pallas_call(
        body, name=name,
        out_shape=tuple(pltpu.HBM(t.shape, t.dtype) for t in lands),
        in_specs=[_HBM] * n + [_SEM, _SEM, pl.BlockSpec(memory_space=pl.ANY)],
        out_specs=tuple([_HBM] * n),
        input_output_aliases={i: i for i in range(n)},
        compiler_params=pltpu.CompilerParams(has_side_effects=_EFFECT),
    )(*lands, send_sems, recv_sems, after)
    return list(res)


def _adamw(parts, w, m, v, *, name, tr):
    nl, r, c = w.shape
    assert len(parts) == nl and r % tr == 0
    n_i = r // tr

    def body(*refs):
        p_refs = refs[:nl]
        w_ref, m_ref, v_ref, g_ref, d_ref, mo_ref, vo_ref = refs[nl:]

        def update(p_ref):
            g = p_ref[0].astype(F32)
            for k in range(1, N_DEV):
                g = g + p_ref[k].astype(F32)
            mn = ADAM_B1 * m_ref[...] + (1.0 - ADAM_B1) * g
            vn = ADAM_B2 * v_ref[...] + (1.0 - ADAM_B2) * (g * g)
            m_hat = mn / (1.0 - ADAM_B1 ** ADAM_STEP)
            v_hat = vn / (1.0 - ADAM_B2 ** ADAM_STEP)
            g_ref[...] = g
            d_ref[...] = -ADAM_LR * (m_hat / (jnp.sqrt(v_hat) + ADAM_EPS) + ADAM_WD * w_ref[...])
            mo_ref[...] = mn
            vo_ref[...] = vn

        if nl == 1:
            update(p_refs[0])
        else:
            for layer in range(nl):
                pl.when(pl.program_id(0) == layer)(lambda p_ref=p_refs[layer]: update(p_ref))

    def parts_spec(layer):
        def index(l, i):
            return (0, jnp.where(l < layer, 0, jnp.where(l > layer, n_i - 1, i)), 0)
        return pl.BlockSpec((N_DEV, tr, c), index)

    blk = pl.BlockSpec((None, tr, c), lambda l, i: (l, i, 0))
    return pl.pallas_call(
        body, name=name, grid=(nl, n_i),
        in_specs=[parts_spec(layer) for layer in range(nl)] + [blk, blk, blk],
        out_specs=[blk] * 4, out_shape=[jax.ShapeDtypeStruct((nl, r, c), F32)] * 4,
        compiler_params=_params(("arbitrary", "arbitrary")),
    )(*parts, w, m, v)


def _pack(parts, lead=()):
    flat = jnp.concatenate([p.reshape(lead + (-1,)) for p in parts], axis=-1)
    n = flat.shape[-1]
    rows = -(-n // (8 * 128)) * 8
    flat = jnp.pad(flat, [(0, 0)] * len(lead) + [(0, rows * 128 - n)])
    return flat.reshape(lead + (rows, 128))


def _unpack(slab, shapes):
    flat = slab.reshape(-1)
    out, off = [], 0
    for shp in shapes:
        size = 1
        for dim in shp:
            size *= dim
        out.append(flat[off:off + size].reshape(shp))
        off += size
    return out


def _to_dev_major(g, axis):
    shp = g.shape
    g = g.reshape(shp[:axis] + (N_DEV, shp[axis] // N_DEV) + shp[axis + 1:])
    return jnp.moveaxis(g, axis, 0)


def _from_dev_major(g, axis):
    g = jnp.moveaxis(g, 0, axis)
    shp = g.shape
    return g.reshape(shp[:axis] + (shp[axis] * shp[axis + 1],) + shp[axis + 2:])


SMALL_SHARDED = ("conv_a", "conv_b", "norm_mix_odd", "pool_scale", "conv_ffn_w")
SMALL_REPLICATED = ("norm_mix_even", "ln_a_g", "ln_a_b", "norm_ffn", "norm_final")
BIG = {"w_in": ("w_in", 0, 320), "w_out": ("w_out", 0, 128), "w_pool": ("w_pool", 0, 128),
       "w_up0": ("w_up", 0, 352), "w_up1": ("w_up", 1, 352),
       "w_down0": ("w_down", 0, 352), "w_down1": ("w_down", 1, 352)}
COLUMN_SHARDED = ("w_in", "w_up")


def kernel(x, norm_mix_even, w_in, conv_a, ln_a_g, ln_a_b, conv_b, w_out, norm_mix_odd, w_pool, pool_scale, norm_ffn, w_up, conv_ffn_w, w_down, norm_final, loss_target, m_norm_mix_even, m_w_in, m_conv_a, m_ln_a_g, m_ln_a_b, m_conv_b, m_w_out, m_norm_mix_odd, m_w_pool, m_pool_scale, m_norm_ffn, m_w_up, m_conv_ffn_w, m_w_down, m_norm_final, v_norm_mix_even, v_w_in, v_conv_a, v_ln_a_g, v_ln_a_b, v_conv_b, v_w_out, v_norm_mix_odd, v_w_pool, v_pool_scale, v_norm_ffn, v_w_up, v_conv_ffn_w, v_w_down, v_norm_final):
    names = ("norm_mix_even", "w_in", "conv_a", "ln_a_g", "ln_a_b", "conv_b", "w_out", "norm_mix_odd", "w_pool",
             "pool_scale", "norm_ffn", "w_up", "conv_ffn_w", "w_down", "norm_final")
    wts = dict(zip(names, (norm_mix_even, w_in, conv_a, ln_a_g, ln_a_b, conv_b, w_out, norm_mix_odd, w_pool,
                           pool_scale, norm_ffn, w_up, conv_ffn_w, w_down, norm_final)))
    mom = dict(zip(names, (m_norm_mix_even, m_w_in, m_conv_a, m_ln_a_g, m_ln_a_b, m_conv_b, m_w_out, m_norm_mix_odd,
                           m_w_pool, m_pool_scale, m_norm_ffn, m_w_up, m_conv_ffn_w, m_w_down, m_norm_final)))
    var = dict(zip(names, (v_norm_mix_even, v_w_in, v_conv_a, v_ln_a_g, v_ln_a_b, v_conv_b, v_w_out, v_norm_mix_odd,
                           v_w_pool, v_pool_scale, v_norm_ffn, v_w_up, v_conv_ffn_w, v_w_down, v_norm_final)))
    d = x.shape[-1]

    def shard3d(t, param):
        a = t[param]
        if param in COLUMN_SHARDED:
            return jnp.swapaxes(a, 1, 2)
        return a.reshape(a.shape[0], -1, a.shape[-1])

    def unshard3d(a, param):
        if param in COLUMN_SHARDED:
            return jnp.swapaxes(a, 1, 2)
        return a.reshape(wts[param].shape)

    def shard2d(t, key):
        param, layer, _ = BIG[key]
        return shard3d(t, param)[layer]

    small_w = _pack([wts[k] for k in SMALL_SHARDED])
    bf = {k: shard2d(wts, k).astype(BF16) for k in BIG}
    gather_groups = {"mix0": ("w_in", "w_out", "small"), "ffn0": ("w_up0", "w_down0"),
                     "ffn1": ("w_pool", "w_up1", "w_down1")}
    order = list(gather_groups)
    started = {}

    def start_gather(grp, after):
        arrs = [small_w if k == "small" else bf[k] for k in gather_groups[grp]]
        started[grp] = _exchange_start(arrs, [False] * len(arrs), after, name=f"gather_{grp}_start", peers=CHIP_PEERS)

    start_gather(order[0], small_w)

    def weights(grp, after):
        keys = gather_groups[grp]
        lands = _exchange_wait(started[grp], [False] * len(keys), after, name=f"gather_{grp}_wait", peers=CHIP_PEERS)
        forwarded = _forward_start(lands, lands[0], name=f"gather_{grp}_forward")
        if grp != order[-1]:
            start_gather(order[order.index(grp) + 1], forwarded[-1])
            behind = started[order[order.index(grp) + 1]][-1]
        else:
            behind = forwarded[-1]
        gw = dict(zip(keys, _forward_wait(forwarded, behind, name=f"gather_{grp}_forward_wait")))
        if grp == "ffn0":
            return {"w_up": gw["w_up0"], "w_down": gw["w_down0"].reshape(N_PAIR, -1, d)}
        if grp == "ffn1":
            return {"w_up": gw["w_up1"], "w_down": gw["w_down1"].reshape(N_PAIR, -1, d),
                    "w_pool": _from_dev_major(gw["w_pool"].reshape(N_DEV, len(POOL_WINDOWS), -1, POOL_GROUP), 1)}
        per_dev = gw["small"].reshape(N_DEV, -1)
        sizes = [wts[k].size for k in SMALL_SHARDED]
        offs = [sum(sizes[:i]) for i in range(len(sizes))]
        small_full = {k: per_dev[:, o:o + n_].reshape((N_DEV,) + wts[k].shape)
                      for k, o, n_ in zip(SMALL_SHARDED, offs, sizes)}
        return {
            "norm_mix_even": norm_mix_even, "ln_a_g": ln_a_g, "ln_a_b": ln_a_b, "norm_ffn": norm_ffn,
            "norm_final": norm_final[None],
            "w_in_t": gw["w_in"].reshape(-1, d),
            "w_out": gw["w_out"].reshape(-1, d),
            "conv_a": _from_dev_major(small_full["conv_a"][:, 0], 1),
            "conv_b": _from_dev_major(small_full["conv_b"][:, 0], 1),
            "norm_mix_odd": _from_dev_major(small_full["norm_mix_odd"], 1),
            "pool_scale": _from_dev_major(small_full["pool_scale"], 1),
            "conv_ffn": [small_full["conv_ffn_w"][:, l] for l in range(2)],
        }

    def dev_major(k, g):
        if k == "w_pool":
            return _to_dev_major(g, 1).reshape(N_DEV, -1, POOL_GROUP)
        return g.reshape(N_DEV, -1, g.shape[-1])

    sent = {}

    def grads_ready(grp, grads):
        keys = tuple(grads)
        parts = [dev_major(k, grads[k]) for k in keys]
        sent[grp] = (keys, _exchange_start(parts, [True] * len(keys), parts[0], name=f"grads_{grp}_start"))
        return sent[grp][1][-1]

    loss, dx, g = _local_step(x[0], loss_target[0], weights, grads_ready)

    small_parts = _pack([
        _to_dev_major(g["conv_a"], 1), _to_dev_major(g["conv_b"], 1), _to_dev_major(g["norm_mix_odd"], 1),
        _to_dev_major(g["pool_scale"], 1), jnp.stack(g["conv_ffn"], axis=1)], lead=(N_DEV,))
    repl_parts = _pack([g[k] for k in SMALL_REPLICATED])
    sent["small"] = (("small", "replicated"),
                     _exchange_start([small_parts, repl_parts], [True, False], dx, name="grads_small_start"))

    landed, after = {}, sent["small"][1][-1]
    for grp in ("ffn1", "ffn0", "mix0"):
        keys, st = sent[grp]
        landed.update(zip(keys, _exchange_wait(st, [True] * len(keys), after, name=f"grads_{grp}_wait")))

    out = {}
    for param, keys in (("w_up", ("w_up0", "w_up1")), ("w_down", ("w_down0", "w_down1")), ("w_in", ("w_in",)),
                        ("w_out", ("w_out",)), ("w_pool", ("w_pool",))):
        res = _adamw([landed[k] for k in keys], shard3d(wts, param), shard3d(mom, param), shard3d(var, param),
                     name=f"adamw_{param}", tr=BIG[keys[0]][2])
        out[param] = [unshard3d(t, param) for t in res]
    keys, st = sent["small"]
    landed.update(zip(keys, _exchange_wait(st, [True, False], out["w_pool"][1], name="grads_small_wait")))

    def slab(t):
        return t[None]

    res = _adamw([landed["small"]], slab(small_w), slab(_pack([mom[k] for k in SMALL_SHARDED])),
                 slab(_pack([var[k] for k in SMALL_SHARDED])), name="adamw_small", tr=small_w.shape[0])
    unpacked = [_unpack(t, [wts[k].shape for k in SMALL_SHARDED]) for t in res]
    for j, k in enumerate(SMALL_SHARDED):
        out[k] = [u[j] for u in unpacked]
    repl_w = _pack([wts[k] for k in SMALL_REPLICATED])
    res = _adamw([landed["replicated"]], slab(repl_w), slab(_pack([mom[k] for k in SMALL_REPLICATED])),
                 slab(_pack([var[k] for k in SMALL_REPLICATED])), name="adamw_replicated", tr=repl_w.shape[0])
    unpacked = [_unpack(t, [wts[k].shape for k in SMALL_REPLICATED]) for t in res]
    for j, k in enumerate(SMALL_REPLICATED):
        out[k] = [u[j] for u in unpacked]

    loss = lax.psum(loss, ("x", "y", "c"))
    return (loss, dx[None], *[out[k][0] for k in names], *[out[k][1] for k in names],
            *[out[k][2] for k in names], *[out[k][3] for k in names])
```

```python
import jax
import jax.numpy as jnp
from jax import lax
from jax.experimental import pallas as pl
from jax.experimental.pallas import tpu as pltpu

F32 = jnp.float32
BF16 = jnp.bfloat16

RMS_EPS = 1e-6
LN_EPS = 1e-5
ADAM_LR = 0.001
ADAM_B1 = 0.9
ADAM_B2 = 0.999
ADAM_EPS = 1e-08
ADAM_WD = 0.01
ADAM_STEP = 10

N_DEV = 8
N_PAIR = N_DEV // 2
A_WIDTH = 512
A_TAPS = 31
POOL_WINDOWS = (2, 4, 8, 16)
POOL_GROUP = 256
HALO_A = 32
HALO_S = 16
VMEM_LIMIT = 56 * 1024 * 1024


def _params(sem, vmem=VMEM_LIMIT):
    return pltpu.CompilerParams(dimension_semantics=sem, vmem_limit_bytes=vmem)


def _sigmoid(x):
    return 0.5 * jnp.tanh(0.5 * x) + 0.5


def _prev_blk(i, ts, hb):
    return jnp.maximum(i * (ts // hb) - 1, 0)


def _next_blk(i, ts, hb, s):
    return jnp.minimum((i + 1) * (ts // hb), s // hb - 1)


def _mm(a, b, *, name, ta=False, tb=False, add=None, out_dtype=F32, tm=512, tn=512, tk=None):
    m, k = (a.shape[1], a.shape[0]) if ta else a.shape
    n = b.shape[0] if tb else b.shape[1]
    tk = k if tk is None else tk
    tm, tn, tk = min(tm, m), min(tn, n), min(tk, k)
    assert m % tm == 0 and n % tn == 0 and k % tk == 0, (name, m, n, k, tm, tn, tk)
    nk = k // tk
    dims = (((0,) if ta else (1,), (1,) if tb else (0,)), ((), ()))
    n_in = 2 + (add is not None)

    def body(*refs):
        a_ref, b_ref = refs[0], refs[1]
        add_ref = refs[2] if add is not None else None
        o_ref = refs[n_in]
        part = lax.dot_general(a_ref[...].astype(BF16), b_ref[...].astype(BF16), dims, preferred_element_type=F32)

        def finish(r):
            if add_ref is not None:
                r = r + add_ref[...]
            o_ref[...] = r.astype(out_dtype)

        if nk == 1:
            finish(part)
            return
        acc_ref = refs[-1]
        kk = pl.program_id(2)

        @pl.when(kk == 0)
        def _():
            acc_ref[...] = part

        @pl.when(kk > 0)
        def _():
            acc_ref[...] += part

        @pl.when(kk == nk - 1)
        def _():
            finish(acc_ref[...])

    a_spec = pl.BlockSpec((tk, tm), lambda i, j, kk: (kk, i)) if ta else pl.BlockSpec((tm, tk), lambda i, j, kk: (i, kk))
    b_spec = pl.BlockSpec((tn, tk), lambda i, j, kk: (j, kk)) if tb else pl.BlockSpec((tk, tn), lambda i, j, kk: (kk, j))
    in_specs = [a_spec, b_spec]
    args = [a, b]
    if add is not None:
        in_specs.append(pl.BlockSpec((tm, tn), lambda i, j, kk: (i, j)))
        args.append(add)
    return pl.pallas_call(
        body, name=name, grid=(m // tm, n // tn, nk),
        in_specs=in_specs, out_specs=pl.BlockSpec((tm, tn), lambda i, j, kk: (i, j)),
        out_shape=jax.ShapeDtypeStruct((m, n), out_dtype),
        scratch_shapes=[pltpu.VMEM((tm, tn), F32)] if nk > 1 else [],
        compiler_params=_params(("parallel", "parallel", "arbitrary")),
    )(*args)


def _rms_rows(xv, gv):
    return xv * lax.rsqrt(jnp.mean(xv * xv, axis=-1, keepdims=True) + RMS_EPS) * gv


_NT = (((1,), (1,)), ((), ()))
_TN = (((0,), (0,)), ((), ()))


def _rms_mm(x, g, wt, *, name, out_dtype, tm=1024, tn=512):
    s, d = x.shape
    n = wt.shape[0]
    tm = min(tm, s)
    assert s % tm == 0 and n % tn == 0

    def body(x_ref, g_ref, w_ref, h_ref, z_ref, hs_ref):
        @pl.when(pl.program_id(1) == 0)
        def _():
            hv = _rms_rows(x_ref[...], g_ref[...]).astype(BF16)
            hs_ref[...] = hv
            h_ref[...] = hv

        z_ref[...] = lax.dot_general(hs_ref[...], w_ref[...], _NT, preferred_element_type=F32).astype(out_dtype)

    return pl.pallas_call(
        body, name=name, grid=(s // tm, n // tn),
        in_specs=[pl.BlockSpec((tm, d), lambda i, j: (i, 0)),
                  pl.BlockSpec((1, d), lambda i, j: (0, 0)),
                  pl.BlockSpec((tn, d), lambda i, j: (j, 0))],
        out_specs=[pl.BlockSpec((tm, d), lambda i, j: (i, 0)),
                   pl.BlockSpec((tm, tn), lambda i, j: (i, j))],
        out_shape=[jax.ShapeDtypeStruct((s, d), BF16), jax.ShapeDtypeStruct((s, n), out_dtype)],
        scratch_shapes=[pltpu.VMEM((tm, d), BF16)],
        compiler_params=_params(("parallel", "arbitrary")),
    )(x, g, wt)


def _rms_bwd_rows(xv, gv, dh):
    r = lax.rsqrt(jnp.mean(xv * xv, axis=-1, keepdims=True) + RMS_EPS)
    xh = xv * r
    dn = dh * gv
    dx = r * (dn - xh * jnp.mean(dn * xh, axis=-1, keepdims=True))
    return dx, dh * xh


def _mm_rms_bwd(a, b, x, g, dres, dep, *, name, tm=512):
    s, k = a.shape
    d = b.shape[1]
    tm = min(tm, s)

    def body(a_ref, b_ref, x_ref, g_ref, dres_ref, dep_ref, dx_ref, dg_ref):
        @pl.when(pl.program_id(0) == 0)
        def _():
            dg_ref[...] = jnp.zeros_like(dg_ref)

        dh = jnp.dot(a_ref[...], b_ref[...], preferred_element_type=F32)
        dx, dgr = _rms_bwd_rows(x_ref[...], g_ref[...], dh)
        dx_ref[...] = dres_ref[...] + dx
        dg_ref[...] += jnp.sum(dgr, axis=0, keepdims=True)

    row = pl.BlockSpec((tm, d), lambda i: (i, 0))
    vec = pl.BlockSpec((1, d), lambda i: (0, 0))
    return pl.pallas_call(
        body, name=name, grid=(s // tm,),
        in_specs=[pl.BlockSpec((tm, k), lambda i: (i, 0)), pl.BlockSpec((k, d), lambda i: (0, 0)), row, vec, row,
                  pl.BlockSpec(memory_space=pl.ANY)],
        out_specs=[row, vec],
        out_shape=[jax.ShapeDtypeStruct((s, d), F32), jax.ShapeDtypeStruct((1, d), F32)],
        compiler_params=_params(("arbitrary",)),
    )(a, b, x, g, dres, dep)


def _final_loss(x, g, target, *, name, ts=256):
    s, d = x.shape
    ts = min(ts, s)

    def body(x_ref, g_ref, t_ref, loss_ref, dx_ref, dxb_ref, dg_ref):
        @pl.when(pl.program_id(0) == 0)
        def _():
            loss_ref[...] = jnp.zeros_like(loss_ref)
            dg_ref[...] = jnp.zeros_like(dg_ref)

        xv = x_ref[...]
        gv = g_ref[...]
        r = lax.rsqrt(jnp.mean(xv * xv, axis=-1, keepdims=True) + RMS_EPS)
        xh = xv * r
        err = xh * gv - t_ref[...]
        loss_ref[...] += 0.5 * jnp.sum(jnp.mean(err * err, axis=-1, keepdims=True), axis=0, keepdims=True)
        dy = err * (1.0 / d)
        dn = dy * gv
        dx = r * (dn - xh * jnp.mean(dn * xh, axis=-1, keepdims=True))
        dx_ref[...] = dx
        dxb_ref[...] = dx.astype(BF16)
        dg_ref[...] += jnp.sum(dy * xh, axis=0, keepdims=True)

    row = pl.BlockSpec((ts, d), lambda i: (i, 0))
    vec = pl.BlockSpec((1, d), lambda i: (0, 0))
    return pl.pallas_call(
        body, name=name, grid=(s // ts,),
        in_specs=[row, vec, row],
        out_specs=[pl.BlockSpec((1, 1), lambda i: (0, 0)), row, row, vec],
        out_shape=[jax.ShapeDtypeStruct((1, 1), F32), jax.ShapeDtypeStruct((s, d), F32),
                   jax.ShapeDtypeStruct((s, d), BF16), jax.ShapeDtypeStruct((1, d), F32)],
        compiler_params=_params(("arbitrary",)),
    )(x, g, target)


def _conv_taps(ext_ref, w_ref, n_taps, base, r0, rows, reverse=False):
    acc = None
    for k in range(n_taps):
        off = r0 + (base - k if reverse else base + k)
        term = w_ref[k:k + 1, :] * ext_ref[pl.ds(off, rows), :]
        acc = term if acc is None else acc + term
    return acc


def _shift_copies(src_ref, sh_ref, rows):
    for b in range(1, 8):
        sh_ref[b, 0:rows, :] = src_ref[pl.ds(b, rows), :]


def _shifted(src_ref, sh_ref, start, rows, off):
    a, b = divmod(off, 8)
    ref = src_ref if b == 0 else sh_ref.at[b]
    return ref[pl.ds(start + 8 * a, rows), :]


def _mix0_fwd(z, conv_a, ln_g, ln_b, conv_b, *, name, ts=256, rc=32):
    s = z.shape[0]
    c = A_WIDTH
    hb = HALO_A

    def body(z_ref, zp_ref, wa_ref, lg_ref, lb_ref, wb_ref, ab_ref, ca_ref, exta, extb, sha):
        keep = jnp.where(pl.program_id(0) > 0, 1.0, 0.0)
        zp = zp_ref[...]
        exta[0:hb, :] = zp[:, 0:c] * _sigmoid(zp[:, c:2 * c]) * keep
        extb[0:hb, :] = zp[:, 3 * c:4 * c] * zp[:, 4 * c:5 * c] * keep
        exta[hb:hb + ts, :] = z_ref[:, 0:c] * _sigmoid(z_ref[:, c:2 * c])
        extb[hb:hb + ts, :] = z_ref[:, 3 * c:4 * c] * z_ref[:, 4 * c:5 * c]
        _shift_copies(exta, sha, hb + ts - 8)
        lg = lg_ref[...]
        lb = lb_ref[...]
        for q in range(ts // rc):
            r0 = q * rc
            ca = None
            for k in range(A_TAPS):
                term = wa_ref[k:k + 1, :] * _shifted(exta, sha, r0, rc, hb - (A_TAPS - 1) + k)
                ca = term if ca is None else ca + term
            ca_ref[r0:r0 + rc, :] = ca
            mu = jnp.mean(ca, axis=-1, keepdims=True)
            xc = ca - mu
            rs = lax.rsqrt(jnp.mean(xc * xc, axis=-1, keepdims=True) + LN_EPS)
            l = xc * rs * lg + lb
            ab_ref[r0:r0 + rc, 0:c] = (l * _sigmoid(l)).astype(BF16)
            cbc = _conv_taps(extb, wb_ref, 3, hb - 2, r0, rc)
            ab_ref[r0:r0 + rc, c:2 * c] = (z_ref[r0:r0 + rc, 2 * c:3 * c] * cbc).astype(BF16)

    return pl.pallas_call(
        body, name=name, grid=(s // ts,),
        in_specs=[pl.BlockSpec((ts, 5 * c), lambda i: (i, 0)),
                  pl.BlockSpec((hb, 5 * c), lambda i: (_prev_blk(i, ts, hb), 0)),
                  pl.BlockSpec((32, c), lambda i: (0, 0)),
                  pl.BlockSpec((1, c), lambda i: (0, 0)),
                  pl.BlockSpec((1, c), lambda i: (0, 0)),
                  pl.BlockSpec((8, c), lambda i: (0, 0))],
        out_specs=[pl.BlockSpec((ts, 2 * c), lambda i: (i, 0)),
                   pl.BlockSpec((ts, c), lambda i: (i, 0))],
        out_shape=[jax.ShapeDtypeStruct((s, 2 * c), BF16), jax.ShapeDtypeStruct((s, c), F32)],
        scratch_shapes=[pltpu.VMEM((hb + ts, c), F32), pltpu.VMEM((hb + ts, c), F32),
                        pltpu.VMEM((8, hb + ts - 8, c), F32)],
        compiler_params=_params(("parallel",)),
    )(z, z, conv_a, ln_g, ln_b, conv_b)


def _mix0_bwd(z, ca, dab, conv_a, ln_g, ln_b, conv_b, *, name, ts=256, rc=32):
    s = z.shape[0]
    c = A_WIDTH
    hb = HALO_A
    ta = A_TAPS

    def body(z_ref, zp_ref, zn_ref, ca_ref, can_ref, d_ref, dn_ref, wa_ref, lg_ref, lb_ref, wb_ref,
             dz_ref, dwa_ref, dwb_ref, dlg_ref, dlb_ref, exta, extb, extdca, extdcb, shd):
        i = pl.program_id(0)
        keep_p = jnp.where(i > 0, 1.0, 0.0)
        keep_n = jnp.where(i < s // ts - 1, 1.0, 0.0)

        @pl.when(i == 0)
        def _():
            dwa_ref[...] = jnp.zeros_like(dwa_ref)
            dwb_ref[...] = jnp.zeros_like(dwb_ref)
            dlg_ref[...] = jnp.zeros_like(dlg_ref)
            dlb_ref[...] = jnp.zeros_like(dlb_ref)

        lg = lg_ref[...]
        lb = lb_ref[...]
        zp = zp_ref[...]
        exta[0:hb, :] = zp[:, 0:c] * _sigmoid(zp[:, c:2 * c]) * keep_p
        extb[0:hb, :] = zp[:, 3 * c:4 * c] * zp[:, 4 * c:5 * c] * keep_p
        exta[hb:hb + ts, :] = z_ref[:, 0:c] * _sigmoid(z_ref[:, c:2 * c])
        extb[hb:hb + ts, :] = z_ref[:, 3 * c:4 * c] * z_ref[:, 4 * c:5 * c]

        def ln_bwd(cav, dav):
            mu = jnp.mean(cav, axis=-1, keepdims=True)
            xc = cav - mu
            rs = lax.rsqrt(jnp.mean(xc * xc, axis=-1, keepdims=True) + LN_EPS)
            nv = xc * rs
            l = nv * lg + lb
            sg = _sigmoid(l)
            dl = dav * (sg * (1.0 + l * (1.0 - sg)))
            dnv = dl * lg
            dca = rs * (dnv - jnp.mean(dnv, axis=-1, keepdims=True)
                        - nv * jnp.mean(dnv * nv, axis=-1, keepdims=True))
            return dca, dl, nv

        dlg_acc = jnp.zeros((1, c), F32)
        dlb_acc = jnp.zeros((1, c), F32)
        for q in range(ts // rc):
            r0 = q * rc
            dca, dl, nv = ln_bwd(ca_ref[r0:r0 + rc, :], d_ref[r0:r0 + rc, 0:c])
            extdca[r0:r0 + rc, :] = dca
            dlg_acc = dlg_acc + jnp.sum(dl * nv, axis=0, keepdims=True)
            dlb_acc = dlb_acc + jnp.sum(dl, axis=0, keepdims=True)
            extdcb[r0:r0 + rc, :] = d_ref[r0:r0 + rc, c:2 * c] * z_ref[r0:r0 + rc, 2 * c:3 * c]
        dca_n, _, _ = ln_bwd(can_ref[...], dn_ref[:, 0:c])
        extdca[ts:ts + hb, :] = dca_n * keep_n
        extdcb[ts:ts + hb, :] = dn_ref[:, c:2 * c] * zn_ref[:, 2 * c:3 * c] * keep_n
        dlg_ref[...] += dlg_acc
        dlb_ref[...] += dlb_acc
        _shift_copies(extdca, shd, ts + hb - 8)

        for q in range(ts // rc):
            r0 = q * rc
            zr = z_ref[r0:r0 + rc, :]
            dga = None
            for k in range(ta):
                term = wa_ref[k:k + 1, :] * _shifted(extdca, shd, r0, rc, ta - 1 - k)
                dga = term if dga is None else dga + term
            sg = _sigmoid(zr[:, c:2 * c])
            dz_ref[r0:r0 + rc, 0:c] = (dga * sg).astype(BF16)
            dz_ref[r0:r0 + rc, c:2 * c] = (dga * zr[:, 0:c] * sg * (1.0 - sg)).astype(BF16)
            cbc = _conv_taps(extb, wb_ref, 3, hb - 2, r0, rc)
            dz_ref[r0:r0 + rc, 2 * c:3 * c] = (d_ref[r0:r0 + rc, c:2 * c] * cbc).astype(BF16)
            dcb = _conv_taps(extdcb, wb_ref, 3, 2, r0, rc, reverse=True)
            dz_ref[r0:r0 + rc, 3 * c:4 * c] = (dcb * zr[:, 4 * c:5 * c]).astype(BF16)
            dz_ref[r0:r0 + rc, 4 * c:5 * c] = (dcb * zr[:, 3 * c:4 * c]).astype(BF16)

        for k in range(ta):
            part = None
            for q in range(ts // rc):
                r0 = q * rc
                p = exta[hb + r0:hb + r0 + rc, :] * _shifted(extdca, shd, r0, rc, ta - 1 - k)
                for r in range(0, rc, 8):
                    part = p[r:r + 8, :] if part is None else part + p[r:r + 8, :]
            dwa_ref[k:k + 1, :] += jnp.sum(part, axis=0, keepdims=True)
        dcb_t = extdcb[0:ts, :]
        for k in range(3):
            dwb_ref[k:k + 1, :] += jnp.sum(dcb_t * extb[pl.ds(hb - 2 + k, ts), :], axis=0, keepdims=True)

    def tile(w):
        return pl.BlockSpec((ts, w), lambda i: (i, 0))

    def prev(w):
        return pl.BlockSpec((hb, w), lambda i: (_prev_blk(i, ts, hb), 0))

    def nxt(w):
        return pl.BlockSpec((hb, w), lambda i: (_next_blk(i, ts, hb, s), 0))

    def const(r, w):
        return pl.BlockSpec((r, w), lambda i: (0, 0))

    return pl.pallas_call(
        body, name=name, grid=(s // ts,),
        in_specs=[tile(5 * c), prev(5 * c), nxt(5 * c), tile(c), nxt(c), tile(2 * c), nxt(2 * c),
                  const(32, c), const(1, c), const(1, c), const(8, c)],
        out_specs=[tile(5 * c), const(32, c), const(8, c), const(1, c), const(1, c)],
        out_shape=[jax.ShapeDtypeStruct((s, 5 * c), BF16), jax.ShapeDtypeStruct((32, c), F32),
                   jax.ShapeDtypeStruct((8, c), F32), jax.ShapeDtypeStruct((1, c), F32),
                   jax.ShapeDtypeStruct((1, c), F32)],
        scratch_shapes=[pltpu.VMEM((hb + ts, c), F32), pltpu.VMEM((hb + ts, c), F32),
                        pltpu.VMEM((ts + hb, c), F32), pltpu.VMEM((ts + hb, c), F32),
                        pltpu.VMEM((8, ts + hb - 8, c), F32)],
        compiler_params=_params(("arbitrary",)),
    )(z, z, z, ca, ca, dab, dab, conv_a, ln_g, ln_b, conv_b)


def _ffn_up(x, g, w8, *, name, tm=1024):
    s, d = x.shape
    nb, c, _ = w8.shape
    tm = min(tm, s)

    def body(x_ref, g_ref, w_ref, h_ref, u_ref, hs_ref):
        @pl.when(pl.program_id(1) == 0)
        def _():
            hv = _rms_rows(x_ref[...], g_ref[...]).astype(BF16)
            hs_ref[...] = hv
            h_ref[...] = hv

        u_ref[...] = lax.dot_general(hs_ref[...], w_ref[...], _NT, preferred_element_type=F32).astype(BF16)

    return pl.pallas_call(
        body, name=name, grid=(s // tm, nb),
        in_specs=[pl.BlockSpec((tm, d), lambda i, k: (i, 0)),
                  pl.BlockSpec((1, d), lambda i, k: (0, 0)),
                  pl.BlockSpec((None, c, d), lambda i, k: (k, 0, 0))],
        out_specs=[pl.BlockSpec((tm, d), lambda i, k: (i, 0)),
                   pl.BlockSpec((None, tm, c), lambda i, k: (k, i, 0))],
        out_shape=[jax.ShapeDtypeStruct((s, d), BF16), jax.ShapeDtypeStruct((nb, s, c), BF16)],
        scratch_shapes=[pltpu.VMEM((tm, d), BF16)],
        compiler_params=_params(("parallel", "arbitrary")),
    )(x, g, w8)


def _ffn_mid(up8, wc8, *, name, ts=512, rc=32):
    nb, s, c = up8.shape
    hb = HALO_S
    ts = min(ts, s)

    def body(g_ref, gp_ref, v_ref, vp_ref, wg_ref, wv_ref, act_ref, ug_ref, uv_ref, extg, extv):
        keep = jnp.where(pl.program_id(0) > 0, 1.0, 0.0)
        extg[0:hb, :] = gp_ref[...].astype(F32) * keep
        extv[0:hb, :] = vp_ref[...].astype(F32) * keep
        extg[hb:hb + ts, :] = g_ref[...].astype(F32)
        extv[hb:hb + ts, :] = v_ref[...].astype(F32)
        for q in range(ts // rc):
            r0 = q * rc
            gg = _conv_taps(extg, wg_ref, 3, hb - 2, r0, rc)
            vv = _conv_taps(extv, wv_ref, 3, hb - 2, r0, rc)
            ug_ref[r0:r0 + rc, :] = gg.astype(BF16)
            uv_ref[r0:r0 + rc, :] = vv.astype(BF16)
            act_ref[r0:r0 + rc, :] = (gg * _sigmoid(gg) * vv).astype(BF16)

    def blk(rows, off, halo):
        if halo:
            return pl.BlockSpec((None, rows, c), lambda i, j: (j + off, _prev_blk(i, ts, hb), 0))
        return pl.BlockSpec((None, rows, c), lambda i, j: (j + off, i, 0))

    def taps(off):
        return pl.BlockSpec((None, 8, c), lambda i, j: (j + off, 0, 0))

    tile = pl.BlockSpec((None, ts, c), lambda i, j: (j, i, 0))
    out = jax.ShapeDtypeStruct((N_PAIR, s, c), BF16)
    return pl.pallas_call(
        body, name=name, grid=(s // ts, N_PAIR),
        in_specs=[blk(ts, 0, False), blk(hb, 0, True), blk(ts, N_PAIR, False), blk(hb, N_PAIR, True),
                  taps(0), taps(N_PAIR)],
        out_specs=[tile, tile, tile], out_shape=[out, out, out],
        scratch_shapes=[pltpu.VMEM((hb + ts, c), F32), pltpu.VMEM((hb + ts, c), F32)],
        compiler_params=_params(("parallel", "parallel")),
    )(up8, up8, up8, up8, wc8, wc8)


def _ffn_down(act4, wd4, x, *, name, tm=512):
    npair, s, c = act4.shape
    d = x.shape[1]
    tm = min(tm, s)

    def body(a_ref, w_ref, x_ref, o_ref):
        acc = x_ref[...]
        for j in range(npair):
            acc = acc + jnp.dot(a_ref[j], w_ref[j], preferred_element_type=F32)
        o_ref[...] = acc

    return pl.pallas_call(
        body, name=name, grid=(s // tm,),
        in_specs=[pl.BlockSpec((npair, tm, c), lambda i: (0, i, 0)),
                  pl.BlockSpec((npair, c, d), lambda i: (0, 0, 0)),
                  pl.BlockSpec((tm, d), lambda i: (i, 0))],
        out_specs=pl.BlockSpec((tm, d), lambda i: (i, 0)),
        out_shape=jax.ShapeDtypeStruct((s, d), F32),
        compiler_params=_params(("parallel",)),
    )(act4, wd4, x)


def _ffn_dact(db, wd4, *, name, tm=1024):
    s, d = db.shape
    npair, c, _ = wd4.shape
    tm = min(tm, s)

    def body(d_ref, w_ref, o_ref):
        o_ref[...] = lax.dot_general(d_ref[...], w_ref[...], (((1,), (1,)), ((), ())),
                                     preferred_element_type=F32).astype(BF16)

    return pl.pallas_call(
        body, name=name, grid=(s // tm, npair),
        in_specs=[pl.BlockSpec((tm, d), lambda i, j: (i, 0)),
                  pl.BlockSpec((None, c, d), lambda i, j: (j, 0, 0))],
        out_specs=pl.BlockSpec((None, tm, c), lambda i, j: (j, i, 0)),
        out_shape=jax.ShapeDtypeStruct((npair, s, c), BF16),
        compiler_params=_params(("parallel", "parallel")),
    )(db, wd4)


def _ffn_dwdown(act4, db, *, name):
    npair, s, c = act4.shape
    d = db.shape[1]

    def body(a_ref, d_ref, o_ref):
        o_ref[...] = lax.dot_general(a_ref[...], d_ref[...], (((0,), (0,)), ((), ())),
                                     preferred_element_type=F32).astype(BF16)

    return pl.pallas_call(
        body, name=name, grid=(npair,),
        in_specs=[pl.BlockSpec((None, s, c), lambda j: (j, 0, 0)),
                  pl.BlockSpec((s, d), lambda j: (0, 0))],
        out_specs=pl.BlockSpec((None, c, d), lambda j: (j, 0, 0)),
        out_shape=jax.ShapeDtypeStruct((npair, c, d), BF16),
        compiler_params=_params(("parallel",)),
    )(act4, db)


def _ffn_midbwd(up8, ug4, uv4, dact4, wc8, *, name, ts=512, rc=32):
    nb, s, c = up8.shape
    hb = HALO_S
    ts = min(ts, s)
    n_i = s // ts

    def body(pg_ref, pv_ref, ug_ref, ugn_ref, uv_ref, uvn_ref, d_ref, dn_ref, wg_ref, wv_ref,
             dg_ref, dv_ref, dwg_ref, dwv_ref, extdg, extdv):
        i = pl.program_id(1)
        keep_n = jnp.where(i < n_i - 1, 1.0, 0.0)

        @pl.when(i == 0)
        def _():
            dwg_ref[...] = jnp.zeros_like(dwg_ref)
            dwv_ref[...] = jnp.zeros_like(dwv_ref)

        def du_rows(r0, rows, gg, vv, dav):
            sg = _sigmoid(gg)
            extdg[r0:r0 + rows, :] = dav * vv * (sg * (1.0 + gg * (1.0 - sg)))
            extdv[r0:r0 + rows, :] = dav * (gg * sg)

        for q in range(ts // rc):
            rows = slice(q * rc, q * rc + rc)
            du_rows(q * rc, rc, ug_ref[rows, :].astype(F32), uv_ref[rows, :].astype(F32),
                    d_ref[rows, :].astype(F32))
        du_rows(ts, 8, ugn_ref[0:8, :].astype(F32), uvn_ref[0:8, :].astype(F32),
                dn_ref[0:8, :].astype(F32) * keep_n)

        def fold(p):
            acc = p[0:8, :]
            for r in range(8, rc, 8):
                acc = acc + p[r:r + 8, :]
            return acc

        for extd, p_ref, w_ref, out_ref, dw_ref in ((extdg, pg_ref, wg_ref, dg_ref, dwg_ref),
                                                    (extdv, pv_ref, wv_ref, dv_ref, dwv_ref)):
            part = [None, None, None]
            for q in range(ts // rc):
                r0 = q * rc
                pre = p_ref[r0:r0 + rc, :].astype(F32)
                dup = None
                for k in range(3):
                    sh = extd[pl.ds(r0 + 2 - k, rc), :]
                    term = w_ref[k:k + 1, :] * sh
                    dup = term if dup is None else dup + term
                    prod = fold(pre * sh)
                    part[k] = prod if part[k] is None else part[k] + prod
                out_ref[r0:r0 + rc, :] = dup.astype(BF16)
            for k in range(3):
                dw_ref[k:k + 1, :] += jnp.sum(part[k], axis=0, keepdims=True)

    def blk(off, nxt):
        if nxt:
            return pl.BlockSpec((None, hb, c), lambda j, i: (j + off, _next_blk(i, ts, hb, s), 0))
        return pl.BlockSpec((None, ts, c), lambda j, i: (j + off, i, 0))

    def taps(off):
        return pl.BlockSpec((None, 8, c), lambda j, i: (j + off, 0, 0))

    tile = blk(0, False)
    acc = pl.BlockSpec((None, 8, c), lambda j, i: (j, 0, 0))
    return pl.pallas_call(
        body, name=name, grid=(N_PAIR, n_i),
        in_specs=[tile, blk(N_PAIR, False), tile, blk(0, True), tile, blk(0, True), tile, blk(0, True),
                  taps(0), taps(N_PAIR)],
        out_specs=[tile, tile, acc, acc],
        out_shape=[jax.ShapeDtypeStruct((N_PAIR, s, c), BF16), jax.ShapeDtypeStruct((N_PAIR, s, c), BF16),
                   jax.ShapeDtypeStruct((N_PAIR, 8, c), F32), jax.ShapeDtypeStruct((N_PAIR, 8, c), F32)],
        scratch_shapes=[pltpu.VMEM((ts + 8, c), F32), pltpu.VMEM((ts + 8, c), F32)],
        compiler_params=_params(("parallel", "arbitrary")),
    )(up8, up8, ug4, ug4, uv4, uv4, dact4, dact4, wc8, wc8)


def _ffn_dh(dupg, dupv, w8, x, g, dres, dep, *, name, tm=256):
    npair, s, c = dupg.shape
    d = x.shape[1]
    tm = min(tm, s)

    def body(dg_ref, dv_ref, w_ref, x_ref, g_ref, dres_ref, dep_ref, dx_ref, dxb_ref, dgain_ref):
        @pl.when(pl.program_id(0) == 0)
        def _():
            dgain_ref[...] = jnp.zeros_like(dgain_ref)

        dh = None
        for j in range(npair):
            for src, k in ((dg_ref, j), (dv_ref, j + npair)):
                part = jnp.dot(src[j], w_ref[k], preferred_element_type=F32)
                dh = part if dh is None else dh + part
        dx, dgr = _rms_bwd_rows(x_ref[...], g_ref[...], dh)
        dx = dres_ref[...] + dx
        dx_ref[...] = dx
        dxb_ref[...] = dx.astype(BF16)
        dgain_ref[...] += jnp.sum(dgr, axis=0, keepdims=True)

    row = pl.BlockSpec((tm, d), lambda i: (i, 0))
    vec = pl.BlockSpec((1, d), lambda i: (0, 0))
    dup = pl.BlockSpec((npair, tm, c), lambda i: (0, i, 0))
    return pl.pallas_call(
        body, name=name, grid=(s // tm,),
        in_specs=[dup, dup, pl.BlockSpec((2 * npair, c, d), lambda i: (0, 0, 0)), row, vec, row,
                  pl.BlockSpec(memory_space=pl.ANY)],
        out_specs=[row, row, vec],
        out_shape=[jax.ShapeDtypeStruct((s, d), F32), jax.ShapeDtypeStruct((s, d), BF16),
                   jax.ShapeDtypeStruct((1, d), F32)],
        compiler_params=_params(("arbitrary",)),
    )(dupg, dupv, w8, x, g, dres, dep)


def _ffn_dwup(h, dupg, dupv, *, name, tm=512):
    npair, s, c = dupg.shape
    d = h.shape[1]

    def body(h_ref, dg_ref, dv_ref, o_ref):
        k = pl.program_id(1)

        @pl.when(k < npair)
        def _():
            o_ref[...] = lax.dot_general(dg_ref[...], h_ref[...], _TN, preferred_element_type=F32).astype(BF16)

        @pl.when(k >= npair)
        def _():
            o_ref[...] = lax.dot_general(dv_ref[...], h_ref[...], _TN, preferred_element_type=F32).astype(BF16)

    return pl.pallas_call(
        body, name=name, grid=(d // tm, 2 * npair),
        in_specs=[pl.BlockSpec((s, tm), lambda m, k: (0, m)),
                  pl.BlockSpec((None, s, c), lambda m, k: (jnp.minimum(k, npair - 1), 0, 0)),
                  pl.BlockSpec((None, s, c), lambda m, k: (jnp.maximum(k - npair, 0), 0, 0))],
        out_specs=pl.BlockSpec((None, c, tm), lambda m, k: (k, 0, m)),
        out_shape=jax.ShapeDtypeStruct((2 * npair, c, d), BF16),
        compiler_params=_params(("parallel", "arbitrary")),
    )(h, dupg, dupv)


def _pool_counts(i, ts, rows, window):
    t = lax.broadcasted_iota(jnp.int32, (rows, 1), 0) + i * ts + 1
    return jnp.minimum(t, window).astype(F32)


def _pool_fwd(x, g, *, name, ts=256):
    s, d = x.shape
    hb = HALO_S
    pg = POOL_GROUP
    ts = min(ts, s)

    def body(x_ref, xp_ref, g_ref, p_ref, ext):
        i = pl.program_id(0)
        keep = jnp.where(i > 0, 1.0, 0.0)
        gv = g_ref[...]
        ext[0:hb, :] = _rms_rows(xp_ref[...], gv) * keep
        ext[hb:hb + ts, :] = _rms_rows(x_ref[...], gv)
        for gi, w in enumerate(POOL_WINDOWS):
            cols = slice(gi * pg, (gi + 1) * pg)
            h = ext[hb:hb + ts, cols]
            acc = h
            for q in range(1, w):
                acc = acc + ext[pl.ds(hb - q, ts), cols]
            p_ref[:, cols] = (acc / _pool_counts(i, ts, ts, w) - h).astype(BF16)

    return pl.pallas_call(
        body, name=name, grid=(s // ts,),
        in_specs=[pl.BlockSpec((ts, d), lambda i: (i, 0)),
                  pl.BlockSpec((hb, d), lambda i: (_prev_blk(i, ts, hb), 0)),
                  pl.BlockSpec((1, d), lambda i: (0, 0))],
        out_specs=pl.BlockSpec((ts, d), lambda i: (i, 0)),
        out_shape=jax.ShapeDtypeStruct((s, d), BF16),
        scratch_shapes=[pltpu.VMEM((hb + ts, d), F32)],
        compiler_params=_params(("parallel",)),
    )(x, x, g)


def _pool_mm_fwd(p, w_pool, scale, x, *, name, ts=1024):
    s, d = x.shape
    pg = POOL_GROUP
    ts = min(ts, s)

    def body(p_ref, w_ref, s_ref, x_ref, o_ref, yu_ref):
        yu = jnp.dot(p_ref[...], w_ref[...], preferred_element_type=F32)
        yu_ref[...] = yu.astype(BF16)
        o_ref[...] = x_ref[...] + yu * s_ref[...]

    blk = pl.BlockSpec((ts, pg), lambda i, gi: (i, gi))
    return pl.pallas_call(
        body, name=name, grid=(s // ts, d // pg),
        in_specs=[blk, pl.BlockSpec((None, pg, pg), lambda i, gi: (gi, 0, 0)),
                  pl.BlockSpec((1, pg), lambda i, gi: (0, gi)), blk],
        out_specs=[blk, blk],
        out_shape=[jax.ShapeDtypeStruct((s, d), F32), jax.ShapeDtypeStruct((s, d), BF16)],
        compiler_params=_params(("parallel", "parallel")),
    )(p, w_pool, scale, x)


def _pool_mm_bwd(dres, w_pool, scale, yu, *, name, ts=1024):
    s, d = dres.shape
    pg = POOL_GROUP
    ts = min(ts, s)

    def body(d_ref, w_ref, s_ref, yu_ref, dyc_ref, dp_ref, ds_ref):
        @pl.when(pl.program_id(1) == 0)
        def _():
            ds_ref[...] = jnp.zeros_like(ds_ref)

        dv = d_ref[...]
        dyc = (dv * s_ref[...]).astype(BF16)
        dyc_ref[...] = dyc
        dp_ref[...] = lax.dot_general(dyc, w_ref[...], (((1,), (1,)), ((), ())), preferred_element_type=F32)
        ds_ref[...] += jnp.sum(dv * yu_ref[...].astype(F32), axis=0, keepdims=True)

    blk = pl.BlockSpec((ts, pg), lambda gi, i: (i, gi))
    vec = pl.BlockSpec((1, pg), lambda gi, i: (0, gi))
    return pl.pallas_call(
        body, name=name, grid=(d // pg, s // ts),
        in_specs=[blk, pl.BlockSpec((None, pg, pg), lambda gi, i: (gi, 0, 0)), vec, blk],
        out_specs=[blk, blk, vec],
        out_shape=[jax.ShapeDtypeStruct((s, d), BF16), jax.ShapeDtypeStruct((s, d), F32),
                   jax.ShapeDtypeStruct((1, d), F32)],
        compiler_params=_params(("parallel", "arbitrary")),
    )(dres, w_pool, scale, yu)


def _pool_dw(p, dyc, *, name):
    s, d = p.shape
    pg = POOL_GROUP

    def body(p_ref, d_ref, o_ref):
        o_ref[...] = lax.dot_general(p_ref[...], d_ref[...], (((0,), (0,)), ((), ())),
                                     preferred_element_type=F32).astype(BF16)

    blk = pl.BlockSpec((s, pg), lambda gi: (0, gi))
    return pl.pallas_call(
        body, name=name, grid=(d // pg,),
        in_specs=[blk, blk], out_specs=pl.BlockSpec((None, pg, pg), lambda gi: (gi, 0, 0)),
        out_shape=jax.ShapeDtypeStruct((d // pg, pg, pg), BF16),
        compiler_params=_params(("parallel",)),
    )(p, dyc)


def _pool_bwd(dp, x, g, dres, *, name, ts=256):
    s, d = x.shape
    hb = HALO_S
    pg = POOL_GROUP
    ts = min(ts, s)
    n_i = s // ts

    def body(dp_ref, dpn_ref, x_ref, g_ref, dres_ref, dx_ref, dxb_ref, dg_ref, ext, dh):
        i = pl.program_id(0)
        keep_n = jnp.where(i < n_i - 1, 1.0, 0.0)

        @pl.when(i == 0)
        def _():
            dg_ref[...] = jnp.zeros_like(dg_ref)

        for gi, w in enumerate(POOL_WINDOWS):
            cols = slice(gi * pg, (gi + 1) * pg)
            ext[0:ts, cols] = dp_ref[:, cols] / _pool_counts(i, ts, ts, w)
            ext[ts:ts + hb, cols] = dpn_ref[:, cols] / _pool_counts(i + 1, ts, hb, w) * keep_n
        for gi, w in enumerate(POOL_WINDOWS):
            cols = slice(gi * pg, (gi + 1) * pg)
            acc = ext[0:ts, cols]
            for q in range(1, w):
                acc = acc + ext[pl.ds(q, ts), cols]
            dh[:, cols] = acc - dp_ref[:, cols]
        dx, dgr = _rms_bwd_rows(x_ref[...], g_ref[...], dh[...])
        dx = dres_ref[...] + dx
        dx_ref[...] = dx
        dxb_ref[...] = dx.astype(BF16)
        dg_ref[...] += jnp.sum(dgr, axis=0, keepdims=True)

    row = pl.BlockSpec((ts, d), lambda i: (i, 0))
    vec = pl.BlockSpec((1, d), lambda i: (0, 0))
    return pl.pallas_call(
        body, name=name, grid=(n_i,),
        in_specs=[row, pl.BlockSpec((hb, d), lambda i: (_next_blk(i, ts, hb, s), 0)), row, vec, row],
        out_specs=[row, row, vec],
        out_shape=[jax.ShapeDtypeStruct((s, d), F32), jax.ShapeDtypeStruct((s, d), BF16),
                   jax.ShapeDtypeStruct((1, d), F32)],
        scratch_shapes=[pltpu.VMEM((ts + hb, d), F32), pltpu.VMEM((ts, d), F32)],
        compiler_params=_params(("arbitrary",)),
    )(dp, dp, x, g, dres)


def _pad_rows(w, rows):
    pad = [(0, 0)] * (w.ndim - 2) + [(0, rows - w.shape[-2]), (0, 0)]
    return jnp.pad(w, pad)


def _ffn_layer_fwd(x, nf, w8, wc8, wd4, tag):
    h, up8 = _ffn_up(x, nf, w8, name=f"ffn{tag}_up")
    act4, ug4, uv4 = _ffn_mid(up8, wc8, name=f"ffn{tag}_mid")
    x_out = _ffn_down(act4, wd4, x, name=f"ffn{tag}_down")
    return x_out, (h, up8, ug4, uv4, act4)


def _ffn_layer_bwd(d, db, x, nf, w8, wc8, wd4, saved, tag, grads_ready):
    h, up8, ug4, uv4, act4 = saved
    dact4 = _ffn_dact(db, wd4, name=f"ffn{tag}_dact")
    dwd4 = _ffn_dwdown(act4, db, name=f"ffn{tag}_dwdown")
    dupg, dupv, dwg, dwv = _ffn_midbwd(up8, ug4, uv4, dact4, wc8, name=f"ffn{tag}_midbwd")
    dw8 = _ffn_dwup(h, dupg, dupv, name=f"ffn{tag}_dwup")
    sent = grads_ready(f"ffn{tag}", {f"w_up{tag}": dw8, f"w_down{tag}": dwd4})
    dx, dxb, dnf = _ffn_dh(dupg, dupv, w8, x, nf, d, sent, name=f"ffn{tag}_dh")
    dwc8 = jnp.concatenate([dwg, dwv], axis=0)[:, :3]
    return dx, dxb, dnf, dwc8


def _local_step(x, target, weights, grads_ready):
    w0 = weights("mix0", x)
    wa = _pad_rows(w0["conv_a"], 32)
    wb = _pad_rows(w0["conv_b"], 8)
    wc = [_pad_rows(w0["conv_ffn"][l], 8) for l in range(2)]
    h0, z = _rms_mm(x, w0["norm_mix_even"], w0["w_in_t"], name="mix0_in", out_dtype=F32, tn=1280)
    ab, ca = _mix0_fwd(z, wa, w0["ln_a_g"], w0["ln_a_b"], wb, name="mix0_mid")
    x1 = _mm(ab, w0["w_out"], add=x, name="mix0_out", tm=1024, tn=1024)
    w1 = weights("ffn0", x1)
    x2, ffn0 = _ffn_layer_fwd(x1, w0["norm_ffn"][0:1], w1["w_up"], wc[0], w1["w_down"], 0)
    w2 = weights("ffn1", x2)
    p = _pool_fwd(x2, w0["norm_mix_odd"], name="pool_mid")
    x3, yu = _pool_mm_fwd(p, w2["w_pool"], w0["pool_scale"], x2, name="pool_mm")
    x4, ffn1 = _ffn_layer_fwd(x3, w0["norm_ffn"][1:2], w2["w_up"], wc[1], w2["w_down"], 1)
    loss, d4, d4b, g_norm_final = _final_loss(x4, w0["norm_final"], target, name="final_loss")

    d3, d3b, g_nf1, g_wc1 = _ffn_layer_bwd(
        d4, d4b, x3, w0["norm_ffn"][1:2], w2["w_up"], wc[1], w2["w_down"], ffn1, 1, grads_ready)
    dyc, dp, g_scale = _pool_mm_bwd(d3, w2["w_pool"], w0["pool_scale"], yu, name="pool_mm_bwd")
    g_pool = _pool_dw(p, dyc, name="pool_dw")
    d2, d2b, g_nmo = _pool_bwd(dp, x2, w0["norm_mix_odd"], d3, name="pool_midbwd")
    d1, d1b, g_nf0, g_wc0 = _ffn_layer_bwd(
        d2, d2b, x1, w0["norm_ffn"][0:1], w1["w_up"], wc[0], w1["w_down"], ffn0, 0, grads_ready)
    dab = _mm(d1b, w0["w_out"], tb=True, name="mix0_dab", tm=1024, tn=1024)
    g_out = _mm(ab, d1b, ta=True, out_dtype=BF16, name="mix0_dwout", tm=1024, tn=512)
    dz, g_wa, g_wb, g_lg, g_lb = _mix0_bwd(z, ca, dab, wa, w0["ln_a_g"], w0["ln_a_b"], wb, name="mix0_midbwd")
    g_in = _mm(dz, h0, ta=True, out_dtype=BF16, name="mix0_dwin", tm=512, tn=1024)
    sent_mix = grads_ready("mix0", {"w_in": g_in, "w_out": g_out, "w_pool": g_pool})
    dx, g_nme = _mm_rms_bwd(dz, w0["w_in_t"], x, w0["norm_mix_even"], d1, sent_mix, name="mix0_dh")

    small = {
        "norm_mix_even": g_nme, "conv_a": g_wa[:A_TAPS], "ln_a_g": g_lg, "ln_a_b": g_lb, "conv_b": g_wb[:3],
        "norm_mix_odd": g_nmo, "pool_scale": g_scale, "norm_ffn": jnp.concatenate([g_nf0, g_nf1], axis=0),
        "conv_ffn": [g_wc0, g_wc1], "norm_final": g_norm_final,
    }
    return loss[0, 0], dx, small


def _my_pos():
    return lax.axis_index("x"), lax.axis_index("y"), lax.axis_index("c")


def _flip(pos, r):
    x, y, c = pos
    return (1 - x if r & 4 else x, 1 - y if r & 2 else y, 1 - c if r & 1 else c)


def _dev_index(pos):
    return 4 * pos[0] + 2 * pos[1] + pos[2]


_HBM = pl.BlockSpec(memory_space=pltpu.HBM)
_SEM = pl.BlockSpec(memory_space=pltpu.SEMAPHORE)
_EFFECT = pltpu.SideEffectType.DATAFLOW_SIDE_EFFECTING


def _exchange_copy(ins, lands, send_sems, recv_sems, scatter, pos, a, r, receiving):
    me = _dev_index(pos)
    peer = _flip(pos, r)
    dest = _dev_index(pos) if receiving else _dev_index(peer)
    src = ins[a].at[dest] if scatter[a] else ins[a]
    slot = _dev_index(peer) if receiving else me
    return pltpu.make_async_remote_copy(
        src_ref=src, dst_ref=lands[a].at[slot], send_sem=send_sems.at[a * (N_DEV - 1) + r - 1],
        recv_sem=recv_sems.at[a * (N_DEV - 1) + r - 1],
        device_id=peer, device_id_type=pl.DeviceIdType.MESH)


ALL_PEERS = tuple(range(1, N_DEV))
CHIP_PEERS = (1, 2, 4, 6)
FORWARDED = (2, 4, 6)


def _exchange_start(arrays, scatter, after, *, name, peers=ALL_PEERS):
    n = len(arrays)
    me = _dev_index(_my_pos())
    lands = []
    for arr, sc in zip(arrays, scatter):
        own = lax.dynamic_index_in_dim(arr, me, 0, keepdims=True) if sc else arr[None]
        shape = arr.shape if sc else (N_DEV,) + arr.shape
        lands.append(lax.dynamic_update_slice(lax.empty(shape, arr.dtype), own, (me,) + (0,) * (len(shape) - 1)))

    def body(*refs):
        ins, lnd = refs[:n], refs[n:2 * n]
        send_sems, recv_sems = refs[2 * n + 1], refs[2 * n + 2]
        token = refs[-1]
        pos = _my_pos()
        for a in range(n):
            for r in peers:
                _exchange_copy(ins, lnd, send_sems, recv_sems, scatter, pos, a, r, receiving=False).start()
        token[...] = jnp.zeros_like(token)

    bufs = [pltpu.with_memory_space_constraint(t, pltpu.HBM) for t in list(arrays) + lands]
    sems = pltpu.SemaphoreType.DMA((n * (N_DEV - 1),))
    res = pl.pallas_call(
        body, name=name,
        out_shape=(sems, sems, *[pltpu.HBM(t.shape, t.dtype) for t in bufs], jax.ShapeDtypeStruct((8, 128), F32)),
        in_specs=[_HBM] * (2 * n) + [pl.BlockSpec(memory_space=pl.ANY)],
        out_specs=(_SEM, _SEM, *[_HBM] * (2 * n), pl.BlockSpec(memory_space=pltpu.VMEM)),
        input_output_aliases={i: 2 + i for i in range(2 * n)},
        compiler_params=pltpu.CompilerParams(has_side_effects=_EFFECT),
    )(*bufs, after)
    return res[0], res[1], list(res[2:2 + n]), list(res[2 + n:2 + 2 * n]), res[-1]


def _exchange_wait(started, scatter, after, *, name, peers=ALL_PEERS):
    send_sems, recv_sems, arrays, lands, _ = started
    n = len(arrays)

    def body(*refs):
        ins, lnd = refs[:n], refs[n:2 * n]
        send_sems, recv_sems = refs[2 * n], refs[2 * n + 1]
        pos = _my_pos()
        for a in range(n):
            for r in peers:
                _exchange_copy(ins, lnd, send_sems, recv_sems, scatter, pos, a, r, receiving=False).wait_send()
                _exchange_copy(ins, lnd, send_sems, recv_sems, scatter, pos, a, r, receiving=True).wait_recv()

    bufs = list(arrays) + list(lands)
    res = pl.pallas_call(
        body, name=name,
        out_shape=tuple(pltpu.HBM(t.shape, t.dtype) for t in bufs),
        in_specs=[_HBM] * (2 * n) + [_SEM, _SEM, pl.BlockSpec(memory_space=pl.ANY)],
        out_specs=tuple([_HBM] * (2 * n)),
        input_output_aliases={i: i for i in range(2 * n)},
        compiler_params=pltpu.CompilerParams(has_side_effects=_EFFECT),
    )(*bufs, send_sems, recv_sems, after)
    return list(res[n:])


def _forward_copy(lands, send_sems, recv_sems, pos, a, q, receiving):
    slot = _dev_index(_flip(pos, q ^ 1 if receiving else q))
    idx = a * len(FORWARDED) + FORWARDED.index(q)
    return pltpu.make_async_remote_copy(
        src_ref=lands[a].at[slot], dst_ref=lands[a].at[slot], send_sem=send_sems.at[idx], recv_sem=recv_sems.at[idx],
        device_id=_flip(pos, 1), device_id_type=pl.DeviceIdType.MESH)


def _forward_start(lands, after, *, name):
    n = len(lands)

    def body(*refs):
        lnd = refs[:n]
        send_sems, recv_sems = refs[n + 1], refs[n + 2]
        token = refs[-1]
        pos = _my_pos()
        for a in range(n):
            for q in FORWARDED:
                _forward_copy(lnd, send_sems, recv_sems, pos, a, q, receiving=False).start()
        token[...] = jnp.zeros_like(token)

    sems = pltpu.SemaphoreType.DMA((n * len(FORWARDED),))
    res = pl.pallas_call(
        body, name=name,
        out_shape=(sems, sems, *[pltpu.HBM(t.shape, t.dtype) for t in lands], jax.ShapeDtypeStruct((8, 128), F32)),
        in_specs=[_HBM] * n + [pl.BlockSpec(memory_space=pl.ANY)],
        out_specs=(_SEM, _SEM, *[_HBM] * n, pl.BlockSpec(memory_space=pltpu.VMEM)),
        input_output_aliases={i: 2 + i for i in range(n)},
        compiler_params=pltpu.CompilerParams(has_side_effects=_EFFECT),
    )(*lands, after)
    return res[0], res[1], list(res[2:2 + n]), res[-1]


def _forward_wait(forwarded, after, *, name):
    send_sems, recv_sems, lands, _ = forwarded
    n = len(lands)

    def body(*refs):
        lnd = refs[:n]
        send_sems, recv_sems = refs[n], refs[n + 1]
        pos = _my_pos()
        for a in range(n):
            for q in FORWARDED:
                _forward_copy(lnd, send_sems, recv_sems, pos, a, q, receiving=False).wait_send()
                _forward_copy(lnd, send_sems, recv_sems, pos, a, q, receiving=True).wait_recv()

    res = pl.pallas_call(
        body, name=name,
        out_shape=tuple(pltpu.HBM(t.shape, t.dtype) for t in lands),
        in_specs=[_HBM] * n + [_SEM, _SEM, pl.BlockSpec(memory_space=pl.ANY)],
        out_specs=tuple([_HBM] * n),
        input_output_aliases={i: i for i in range(n)},
        compiler_params=pltpu.CompilerParams(has_side_effects=_EFFECT),
    )(*lands, send_sems, recv_sems, after)
    return list(res)


def _adamw(parts, w, m, v, *, name, tr):
    nl, r, c = w.shape
    assert len(parts) == nl and r % tr == 0
    n_i = r // tr

    def body(*refs):
        p_refs = refs[:nl]
        w_ref, m_ref, v_ref, g_ref, d_ref, mo_ref, vo_ref = refs[nl:]

        def update(p_ref):
            g = p_ref[0].astype(F32)
            for k in range(1, N_DEV):
                g = g + p_ref[k].astype(F32)
            mn = ADAM_B1 * m_ref[...] + (1.0 - ADAM_B1) * g
            vn = ADAM_B2 * v_ref[...] + (1.0 - ADAM_B2) * (g * g)
            m_hat = mn / (1.0 - ADAM_B1 ** ADAM_STEP)
            v_hat = vn / (1.0 - ADAM_B2 ** ADAM_STEP)
            g_ref[...] = g
            d_ref[...] = -ADAM_LR * (m_hat / (jnp.sqrt(v_hat) + ADAM_EPS) + ADAM_WD * w_ref[...])
            mo_ref[...] = mn
            vo_ref[...] = vn

        if nl == 1:
            update(p_refs[0])
        else:
            for layer in range(nl):
                pl.when(pl.program_id(0) == layer)(lambda p_ref=p_refs[layer]: update(p_ref))

    def parts_spec(layer):
        def index(l, i):
            return (0, jnp.where(l < layer, 0, jnp.where(l > layer, n_i - 1, i)), 0)
        return pl.BlockSpec((N_DEV, tr, c), index)

    blk = pl.BlockSpec((None, tr, c), lambda l, i: (l, i, 0))
    return pl.pallas_call(
        body, name=name, grid=(nl, n_i),
        in_specs=[parts_spec(layer) for layer in range(nl)] + [blk, blk, blk],
        out_specs=[blk] * 4, out_shape=[jax.ShapeDtypeStruct((nl, r, c), F32)] * 4,
        compiler_params=_params(("arbitrary", "arbitrary")),
    )(*parts, w, m, v)


def _pack(parts, lead=()):
    flat = jnp.concatenate([p.reshape(lead + (-1,)) for p in parts], axis=-1)
    n = flat.shape[-1]
    rows = -(-n // (8 * 128)) * 8
    flat = jnp.pad(flat, [(0, 0)] * len(lead) + [(0, rows * 128 - n)])
    return flat.reshape(lead + (rows, 128))


def _unpack(slab, shapes):
    flat = slab.reshape(-1)
    out, off = [], 0
    for shp in shapes:
        size = 1
        for dim in shp:
            size *= dim
        out.append(flat[off:off + size].reshape(shp))
        off += size
    return out


def _to_dev_major(g, axis):
    shp = g.shape
    g = g.reshape(shp[:axis] + (N_DEV, shp[axis] // N_DEV) + shp[axis + 1:])
    return jnp.moveaxis(g, axis, 0)


def _from_dev_major(g, axis):
    g = jnp.moveaxis(g, 0, axis)
    shp = g.shape
    return g.reshape(shp[:axis] + (shp[axis] * shp[axis + 1],) + shp[axis + 2:])


SMALL_SHARDED = ("conv_a", "conv_b", "norm_mix_odd", "pool_scale", "conv_ffn_w")
SMALL_REPLICATED = ("norm_mix_even", "ln_a_g", "ln_a_b", "norm_ffn", "norm_final")
BIG = {"w_in": ("w_in", 0, 320), "w_out": ("w_out", 0, 128), "w_pool": ("w_pool", 0, 128),
       "w_up0": ("w_up", 0, 352), "w_up1": ("w_up", 1, 352),
       "w_down0": ("w_down", 0, 352), "w_down1": ("w_down", 1, 352)}
COLUMN_SHARDED = ("w_in", "w_up")


def kernel(x, norm_mix_even, w_in, conv_a, ln_a_g, ln_a_b, conv_b, w_out, norm_mix_odd, w_pool, pool_scale, norm_ffn, w_up, conv_ffn_w, w_down, norm_final, loss_target, m_norm_mix_even, m_w_in, m_conv_a, m_ln_a_g, m_ln_a_b, m_conv_b, m_w_out, m_norm_mix_odd, m_w_pool, m_pool_scale, m_norm_ffn, m_w_up, m_conv_ffn_w, m_w_down, m_norm_final, v_norm_mix_even, v_w_in, v_conv_a, v_ln_a_g, v_ln_a_b, v_conv_b, v_w_out, v_norm_mix_odd, v_w_pool, v_pool_scale, v_norm_ffn, v_w_up, v_conv_ffn_w, v_w_down, v_norm_final):
    names = ("norm_mix_even", "w_in", "conv_a", "ln_a_g", "ln_a_b", "conv_b", "w_out", "norm_mix_odd", "w_pool",
             "pool_scale", "norm_ffn", "w_up", "conv_ffn_w", "w_down", "norm_final")
    wts = dict(zip(names, (norm_mix_even, w_in, conv_a, ln_a_g, ln_a_b, conv_b, w_out, norm_mix_odd, w_pool,
                           pool_scale, norm_ffn, w_up, conv_ffn_w, w_down, norm_final)))
    mom = dict(zip(names, (m_norm_mix_even, m_w_in, m_conv_a, m_ln_a_g, m_ln_a_b, m_conv_b, m_w_out, m_norm_mix_odd,
                           m_w_pool, m_pool_scale, m_norm_ffn, m_w_up, m_conv_ffn_w, m_w_down, m_norm_final)))
    var = dict(zip(names, (v_norm_mix_even, v_w_in, v_conv_a, v_ln_a_g, v_ln_a_b, v_conv_b, v_w_out, v_norm_mix_odd,
                           v_w_pool, v_pool_scale, v_norm_ffn, v_w_up, v_conv_ffn_w, v_w_down, v_norm_final)))
    d = x.shape[-1]

    def shard3d(t, param):
        a = t[param]
        if param in COLUMN_SHARDED:
            return jnp.swapaxes(a, 1, 2)
        return a.reshape(a.shape[0], -1, a.shape[-1])

    def unshard3d(a, param):
        if param in COLUMN_SHARDED:
            return jnp.swapaxes(a, 1, 2)
        return a.reshape(wts[param].shape)

    def shard2d(t, key):
        param, layer, _ = BIG[key]
        return shard3d(t, param)[layer]

    small_w = _pack([wts[k] for k in SMALL_SHARDED])
    bf = {k: shard2d(wts, k).astype(BF16) for k in BIG}
    gather_groups = {"mix0": ("w_in", "w_out", "small"), "ffn0": ("w_up0", "w_down0"),
                     "ffn1": ("w_pool", "w_up1", "w_down1")}
    order = list(gather_groups)
    started = {}

    def start_gather(grp, after):
        arrs = [small_w if k == "small" else bf[k] for k in gather_groups[grp]]
        started[grp] = _exchange_start(arrs, [False] * len(arrs), after, name=f"gather_{grp}_start", peers=CHIP_PEERS)

    start_gather(order[0], small_w)

    def weights(grp, after):
        keys = gather_groups[grp]
        lands = _exchange_wait(started[grp], [False] * len(keys), after, name=f"gather_{grp}_wait", peers=CHIP_PEERS)
        forwarded = _forward_start(lands, lands[0], name=f"gather_{grp}_forward")
        if grp != order[-1]:
            start_gather(order[order.index(grp) + 1], forwarded[-1])
            behind = started[order[order.index(grp) + 1]][-1]
        else:
            behind = forwarded[-1]
        gw = dict(zip(keys, _forward_wait(forwarded, behind, name=f"gather_{grp}_forward_wait")))
        if grp == "ffn0":
            return {"w_up": gw["w_up0"], "w_down": gw["w_down0"].reshape(N_PAIR, -1, d)}
        if grp == "ffn1":
            return {"w_up": gw["w_up1"], "w_down": gw["w_down1"].reshape(N_PAIR, -1, d),
                    "w_pool": _from_dev_major(gw["w_pool"].reshape(N_DEV, len(POOL_WINDOWS), -1, POOL_GROUP), 1)}
        per_dev = gw["small"].reshape(N_DEV, -1)
        sizes = [wts[k].size for k in SMALL_SHARDED]
        offs = [sum(sizes[:i]) for i in range(len(sizes))]
        small_full = {k: per_dev[:, o:o + n_].reshape((N_DEV,) + wts[k].shape)
                      for k, o, n_ in zip(SMALL_SHARDED, offs, sizes)}
        return {
            "norm_mix_even": norm_mix_even, "ln_a_g": ln_a_g, "ln_a_b": ln_a_b, "norm_ffn": norm_ffn,
            "norm_final": norm_final[None],
            "w_in_t": gw["w_in"].reshape(-1, d),
            "w_out": gw["w_out"].reshape(-1, d),
            "conv_a": _from_dev_major(small_full["conv_a"][:, 0], 1),
            "conv_b": _from_dev_major(small_full["conv_b"][:, 0], 1),
            "norm_mix_odd": _from_dev_major(small_full["norm_mix_odd"], 1),
            "pool_scale": _from_dev_major(small_full["pool_scale"], 1),
            "conv_ffn": [small_full["conv_ffn_w"][:, l] for l in range(2)],
        }

    def dev_major(k, g):
        if k == "w_pool":
            return _to_dev_major(g, 1).reshape(N_DEV, -1, POOL_GROUP)
        return g.reshape(N_DEV, -1, g.shape[-1])

    sent = {}

    def grads_ready(grp, grads):
        keys = tuple(grads)
        parts = [dev_major(k, grads[k]) for k in keys]
        sent[grp] = (keys, _exchange_start(parts, [True] * len(keys), parts[0], name=f"grads_{grp}_start"))
        return sent[grp][1][-1]

    loss, dx, g = _local_step(x[0], loss_target[0], weights, grads_ready)

    small_parts = _pack([
        _to_dev_major(g["conv_a"], 1), _to_dev_major(g["conv_b"], 1), _to_dev_major(g["norm_mix_odd"], 1),
        _to_dev_major(g["pool_scale"], 1), jnp.stack(g["conv_ffn"], axis=1)], lead=(N_DEV,))
    repl_parts = _pack([g[k] for k in SMALL_REPLICATED])
    sent["small"] = (("small", "replicated"),
                     _exchange_start([small_parts, repl_parts], [True, False], dx, name="grads_small_start"))

    landed, after = {}, sent["small"][1][-1]
    for grp in ("ffn1", "ffn0", "mix0"):
        keys, st = sent[grp]
        landed.update(zip(keys, _exchange_wait(st, [True] * len(keys), after, name=f"grads_{grp}_wait")))

    out = {}
    for param, keys in (("w_up", ("w_up0", "w_up1")), ("w_down", ("w_down0", "w_down1")), ("w_in", ("w_in",)),
                        ("w_out", ("w_out",)), ("w_pool", ("w_pool",))):
        res = _adamw([landed[k] for k in keys], shard3d(wts, param), shard3d(mom, param), shard3d(var, param),
                     name=f"adamw_{param}", tr=BIG[keys[0]][2])
        out[param] = [unshard3d(t, param) for t in res]
    keys, st = sent["small"]
    landed.update(zip(keys, _exchange_wait(st, [True, False], out["w_pool"][1], name="grads_small_wait")))

    def slab(t):
        return t[None]

    res = _adamw([landed["small"]], slab(small_w), slab(_pack([mom[k] for k in SMALL_SHARDED])),
                 slab(_pack([var[k] for k in SMALL_SHARDED])), name="adamw_small", tr=small_w.shape[0])
    unpacked = [_unpack(t, [wts[k].shape for k in SMALL_SHARDED]) for t in res]
    for j, k in enumerate(SMALL_SHARDED):
        out[k] = [u[j] for u in unpacked]
    repl_w = _pack([wts[k] for k in SMALL_REPLICATED])
    res = _adamw([landed["replicated"]], slab(repl_w), slab(_pack([mom[k] for k in SMALL_REPLICATED])),
                 slab(_pack([var[k] for k in SMALL_REPLICATED])), name="adamw_replicated", tr=repl_w.shape[0])
    unpacked = [_unpack(t, [wts[k].shape for k in SMALL_REPLICATED]) for t in res]
    for j, k in enumerate(SMALL_REPLICATED):
        out[k] = [u[j] for u in unpacked]

    loss = lax.psum(loss, ("x", "y", "c"))
    return (loss, dx[None], *[out[k][0] for k in names], *[out[k][1] for k in names],
            *[out[k][2] for k in names], *[out[k][3] for k in names])
```

```python
import jax
import jax.numpy as jnp
from jax import lax
from jax.experimental import pallas as pl
from jax.experimental.pallas import tpu as pltpu

F32 = jnp.float32
BF16 = jnp.bfloat16

RMS_EPS = 1e-6
LN_EPS = 1e-5
ADAM_LR = 0.001
ADAM_B1 = 0.9
ADAM_B2 = 0.999
ADAM_EPS = 1e-08
ADAM_WD = 0.01
ADAM_STEP = 10

N_DEV = 8
N_PAIR = N_DEV // 2
A_WIDTH = 512
A_TAPS = 31
POOL_WINDOWS = (2, 4, 8, 16)
POOL_GROUP = 256
HALO_A = 32
HALO_S = 16
VMEM_LIMIT = 56 * 1024 * 1024


def _params(sem, vmem=VMEM_LIMIT):
    return pltpu.CompilerParams(dimension_semantics=sem, vmem_limit_bytes=vmem)


def _sigmoid(x):
    return 0.5 * jnp.tanh(0.5 * x) + 0.5


def _prev_blk(i, ts, hb):
    return jnp.maximum(i * (ts // hb) - 1, 0)


def _next_blk(i, ts, hb, s):
    return jnp.minimum((i + 1) * (ts // hb), s // hb - 1)


def _mm(a, b, *, name, ta=False, tb=False, add=None, out_dtype=F32, tm=512, tn=512, tk=None):
    m, k = (a.shape[1], a.shape[0]) if ta else a.shape
    n = b.shape[0] if tb else b.shape[1]
    tk = k if tk is None else tk
    tm, tn, tk = min(tm, m), min(tn, n), min(tk, k)
    assert m % tm == 0 and n % tn == 0 and k % tk == 0, (name, m, n, k, tm, tn, tk)
    nk = k // tk
    dims = (((0,) if ta else (1,), (1,) if tb else (0,)), ((), ()))
    n_in = 2 + (add is not None)

    def body(*refs):
        a_ref, b_ref = refs[0], refs[1]
        add_ref = refs[2] if add is not None else None
        o_ref = refs[n_in]
        part = lax.dot_general(a_ref[...].astype(BF16), b_ref[...].astype(BF16), dims, preferred_element_type=F32)

        def finish(r):
            if add_ref is not None:
                r = r + add_ref[...]
            o_ref[...] = r.astype(out_dtype)

        if nk == 1:
            finish(part)
            return
        acc_ref = refs[-1]
        kk = pl.program_id(2)

        @pl.when(kk == 0)
        def _():
            acc_ref[...] = part

        @pl.when(kk > 0)
        def _():
            acc_ref[...] += part

        @pl.when(kk == nk - 1)
        def _():
            finish(acc_ref[...])

    a_spec = pl.BlockSpec((tk, tm), lambda i, j, kk: (kk, i)) if ta else pl.BlockSpec((tm, tk), lambda i, j, kk: (i, kk))
    b_spec = pl.BlockSpec((tn, tk), lambda i, j, kk: (j, kk)) if tb else pl.BlockSpec((tk, tn), lambda i, j, kk: (kk, j))
    in_specs = [a_spec, b_spec]
    args = [a, b]
    if add is not None:
        in_specs.append(pl.BlockSpec((tm, tn), lambda i, j, kk: (i, j)))
        args.append(add)
    return pl.pallas_call(
        body, name=name, grid=(m // tm, n // tn, nk),
        in_specs=in_specs, out_specs=pl.BlockSpec((tm, tn), lambda i, j, kk: (i, j)),
        out_shape=jax.ShapeDtypeStruct((m, n), out_dtype),
        scratch_shapes=[pltpu.VMEM((tm, tn), F32)] if nk > 1 else [],
        compiler_params=_params(("parallel", "parallel", "arbitrary")),
    )(*args)


def _rms_rows(xv, gv):
    return xv * lax.rsqrt(jnp.mean(xv * xv, axis=-1, keepdims=True) + RMS_EPS) * gv


_NT = (((1,), (1,)), ((), ()))
_TN = (((0,), (0,)), ((), ()))


def _rms_mm(x, g, wt, *, name, out_dtype, tm=1024, tn=512):
    s, d = x.shape
    n = wt.shape[0]
    tm = min(tm, s)
    assert s % tm == 0 and n % tn == 0

    def body(x_ref, g_ref, w_ref, h_ref, z_ref, hs_ref):
        @pl.when(pl.program_id(1) == 0)
        def _():
            hv = _rms_rows(x_ref[...], g_ref[...]).astype(BF16)
            hs_ref[...] = hv
            h_ref[...] = hv

        z_ref[...] = lax.dot_general(hs_ref[...], w_ref[...], _NT, preferred_element_type=F32).astype(out_dtype)

    return pl.pallas_call(
        body, name=name, grid=(s // tm, n // tn),
        in_specs=[pl.BlockSpec((tm, d), lambda i, j: (i, 0)),
                  pl.BlockSpec((1, d), lambda i, j: (0, 0)),
                  pl.BlockSpec((tn, d), lambda i, j: (j, 0))],
        out_specs=[pl.BlockSpec((tm, d), lambda i, j: (i, 0)),
                   pl.BlockSpec((tm, tn), lambda i, j: (i, j))],
        out_shape=[jax.ShapeDtypeStruct((s, d), BF16), jax.ShapeDtypeStruct((s, n), out_dtype)],
        scratch_shapes=[pltpu.VMEM((tm, d), BF16)],
        compiler_params=_params(("parallel", "arbitrary")),
    )(x, g, wt)


def _rms_bwd_rows(xv, gv, dh):
    r = lax.rsqrt(jnp.mean(xv * xv, axis=-1, keepdims=True) + RMS_EPS)
    xh = xv * r
    dn = dh * gv
    dx = r * (dn - xh * jnp.mean(dn * xh, axis=-1, keepdims=True))
    return dx, dh * xh


def _mm_rms_bwd(a, b, x, g, dres, dep, *, name, tm=512):
    s, k = a.shape
    d = b.shape[1]
    tm = min(tm, s)

    def body(a_ref, b_ref, x_ref, g_ref, dres_ref, dep_ref, dx_ref, dg_ref):
        @pl.when(pl.program_id(0) == 0)
        def _():
            dg_ref[...] = jnp.zeros_like(dg_ref)

        dh = jnp.dot(a_ref[...], b_ref[...], preferred_element_type=F32)
        dx, dgr = _rms_bwd_rows(x_ref[...], g_ref[...], dh)
        dx_ref[...] = dres_ref[...] + dx
        dg_ref[...] += jnp.sum(dgr, axis=0, keepdims=True)

    row = pl.BlockSpec((tm, d), lambda i: (i, 0))
    vec = pl.BlockSpec((1, d), lambda i: (0, 0))
    return pl.pallas_call(
        body, name=name, grid=(s // tm,),
        in_specs=[pl.BlockSpec((tm, k), lambda i: (i, 0)), pl.BlockSpec((k, d), lambda i: (0, 0)), row, vec, row,
                  pl.BlockSpec(memory_space=pl.ANY)],
        out_specs=[row, vec],
        out_shape=[jax.ShapeDtypeStruct((s, d), F32), jax.ShapeDtypeStruct((1, d), F32)],
        compiler_params=_params(("arbitrary",)),
    )(a, b, x, g, dres, dep)


def _final_loss(x, g, target, *, name, ts=256):
    s, d = x.shape
    ts = min(ts, s)

    def body(x_ref, g_ref, t_ref, loss_ref, dx_ref, dxb_ref, dg_ref):
        @pl.when(pl.program_id(0) == 0)
        def _():
            loss_ref[...] = jnp.zeros_like(loss_ref)
            dg_ref[...] = jnp.zeros_like(dg_ref)

        xv = x_ref[...]
        gv = g_ref[...]
        r = lax.rsqrt(jnp.mean(xv * xv, axis=-1, keepdims=True) + RMS_EPS)
        xh = xv * r
        err = xh * gv - t_ref[...]
        loss_ref[...] += 0.5 * jnp.sum(jnp.mean(err * err, axis=-1, keepdims=True), axis=0, keepdims=True)
        dy = err * (1.0 / d)
        dn = dy * gv
        dx = r * (dn - xh * jnp.mean(dn * xh, axis=-1, keepdims=True))
        dx_ref[...] = dx
        dxb_ref[...] = dx.astype(BF16)
        dg_ref[...] += jnp.sum(dy * xh, axis=0, keepdims=True)

    row = pl.BlockSpec((ts, d), lambda i: (i, 0))
    vec = pl.BlockSpec((1, d), lambda i: (0, 0))
    return pl.pallas_call(
        body, name=name, grid=(s // ts,),
        in_specs=[row, vec, row],
        out_specs=[pl.BlockSpec((1, 1), lambda i: (0, 0)), row, row, vec],
        out_shape=[jax.ShapeDtypeStruct((1, 1), F32), jax.ShapeDtypeStruct((s, d), F32),
                   jax.ShapeDtypeStruct((s, d), BF16), jax.ShapeDtypeStruct((1, d), F32)],
        compiler_params=_params(("arbitrary",)),
    )(x, g, target)


def _conv_taps(ext_ref, w_ref, n_taps, base, r0, rows, reverse=False):
    acc = None
    for k in range(n_taps):
        off = r0 + (base - k if reverse else base + k)
        term = w_ref[k:k + 1, :] * ext_ref[pl.ds(off, rows), :]
        acc = term if acc is None else acc + term
    return acc


def _shift_copies(src_ref, sh_ref, rows):
    for b in range(1, 8):
        sh_ref[b, 0:rows, :] = src_ref[pl.ds(b, rows), :]


def _shifted(src_ref, sh_ref, start, rows, off):
    a, b = divmod(off, 8)
    ref = src_ref if b == 0 else sh_ref.at[b]
    return ref[pl.ds(start + 8 * a, rows), :]


def _mix0_fwd(z, conv_a, ln_g, ln_b, conv_b, *, name, ts=256, rc=32):
    s = z.shape[0]
    c = A_WIDTH
    hb = HALO_A

    def body(z_ref, zp_ref, wa_ref, lg_ref, lb_ref, wb_ref, ab_ref, ca_ref, exta, extb, sha):
        keep = jnp.where(pl.program_id(0) > 0, 1.0, 0.0)
        zp = zp_ref[...]
        exta[0:hb, :] = zp[:, 0:c] * _sigmoid(zp[:, c:2 * c]) * keep
        extb[0:hb, :] = zp[:, 3 * c:4 * c] * zp[:, 4 * c:5 * c] * keep
        exta[hb:hb + ts, :] = z_ref[:, 0:c] * _sigmoid(z_ref[:, c:2 * c])
        extb[hb:hb + ts, :] = z_ref[:, 3 * c:4 * c] * z_ref[:, 4 * c:5 * c]
        _shift_copies(exta, sha, hb + ts - 8)
        lg = lg_ref[...]
        lb = lb_ref[...]
        for q in range(ts // rc):
            r0 = q * rc
            ca = None
            for k in range(A_TAPS):
                term = wa_ref[k:k + 1, :] * _shifted(exta, sha, r0, rc, hb - (A_TAPS - 1) + k)
                ca = term if ca is None else ca + term
            ca_ref[r0:r0 + rc, :] = ca
            mu = jnp.mean(ca, axis=-1, keepdims=True)
            xc = ca - mu
            rs = lax.rsqrt(jnp.mean(xc * xc, axis=-1, keepdims=True) + LN_EPS)
            l = xc * rs * lg + lb
            ab_ref[r0:r0 + rc, 0:c] = (l * _sigmoid(l)).astype(BF16)
            cbc = _conv_taps(extb, wb_ref, 3, hb - 2, r0, rc)
            ab_ref[r0:r0 + rc, c:2 * c] = (z_ref[r0:r0 + rc, 2 * c:3 * c] * cbc).astype(BF16)

    return pl.pallas_call(
        body, name=name, grid=(s // ts,),
        in_specs=[pl.BlockSpec((ts, 5 * c), lambda i: (i, 0)),
                  pl.BlockSpec((hb, 5 * c), lambda i: (_prev_blk(i, ts, hb), 0)),
                  pl.BlockSpec((32, c), lambda i: (0, 0)),
                  pl.BlockSpec((1, c), lambda i: (0, 0)),
                  pl.BlockSpec((1, c), lambda i: (0, 0)),
                  pl.BlockSpec((8, c), lambda i: (0, 0))],
        out_specs=[pl.BlockSpec((ts, 2 * c), lambda i: (i, 0)),
                   pl.BlockSpec((ts, c), lambda i: (i, 0))],
        out_shape=[jax.ShapeDtypeStruct((s, 2 * c), BF16), jax.ShapeDtypeStruct((s, c), F32)],
        scratch_shapes=[pltpu.VMEM((hb + ts, c), F32), pltpu.VMEM((hb + ts, c), F32),
                        pltpu.VMEM((8, hb + ts - 8, c), F32)],
        compiler_params=_params(("parallel",)),
    )(z, z, conv_a, ln_g, ln_b, conv_b)


def _mix0_bwd(z, ca, dab, conv_a, ln_g, ln_b, conv_b, *, name, ts=256, rc=32):
    s = z.shape[0]
    c = A_WIDTH
    hb = HALO_A
    ta = A_TAPS

    def body(z_ref, zp_ref, zn_ref, ca_ref, can_ref, d_ref, dn_ref, wa_ref, lg_ref, lb_ref, wb_ref,
             dz_ref, dwa_ref, dwb_ref, dlg_ref, dlb_ref, exta, extb, extdca, extdcb, shd):
        i = pl.program_id(0)
        keep_p = jnp.where(i > 0, 1.0, 0.0)
        keep_n = jnp.where(i < s // ts - 1, 1.0, 0.0)

        @pl.when(i == 0)
        def _():
            dwa_ref[...] = jnp.zeros_like(dwa_ref)
            dwb_ref[...] = jnp.zeros_like(dwb_ref)
            dlg_ref[...] = jnp.zeros_like(dlg_ref)
            dlb_ref[...] = jnp.zeros_like(dlb_ref)

        lg = lg_ref[...]
        lb = lb_ref[...]
        zp = zp_ref[...]
        exta[0:hb, :] = zp[:, 0:c] * _sigmoid(zp[:, c:2 * c]) * keep_p
        extb[0:hb, :] = zp[:, 3 * c:4 * c] * zp[:, 4 * c:5 * c] * keep_p
        exta[hb:hb + ts, :] = z_ref[:, 0:c] * _sigmoid(z_ref[:, c:2 * c])
        extb[hb:hb + ts, :] = z_ref[:, 3 * c:4 * c] * z_ref[:, 4 * c:5 * c]

        def ln_bwd(cav, dav):
            mu = jnp.mean(cav, axis=-1, keepdims=True)
            xc = cav - mu
            rs = lax.rsqrt(jnp.mean(xc * xc, axis=-1, keepdims=True) + LN_EPS)
            nv = xc * rs
            l = nv * lg + lb
            sg = _sigmoid(l)
            dl = dav * (sg * (1.0 + l * (1.0 - sg)))
            dnv = dl * lg
            dca = rs * (dnv - jnp.mean(dnv, axis=-1, keepdims=True)
                        - nv * jnp.mean(dnv * nv, axis=-1, keepdims=True))
            return dca, dl, nv

        dlg_acc = jnp.zeros((1, c), F32)
        dlb_acc = jnp.zeros((1, c), F32)
        for q in range(ts // rc):
            r0 = q * rc
            dca, dl, nv = ln_bwd(ca_ref[r0:r0 + rc, :], d_ref[r0:r0 + rc, 0:c])
            extdca[r0:r0 + rc, :] = dca
            dlg_acc = dlg_acc + jnp.sum(dl * nv, axis=0, keepdims=True)
            dlb_acc = dlb_acc + jnp.sum(dl, axis=0, keepdims=True)
            extdcb[r0:r0 + rc, :] = d_ref[r0:r0 + rc, c:2 * c] * z_ref[r0:r0 + rc, 2 * c:3 * c]
        dca_n, _, _ = ln_bwd(can_ref[...], dn_ref[:, 0:c])
        extdca[ts:ts + hb, :] = dca_n * keep_n
        extdcb[ts:ts + hb, :] = dn_ref[:, c:2 * c] * zn_ref[:, 2 * c:3 * c] * keep_n
        dlg_ref[...] += dlg_acc
        dlb_ref[...] += dlb_acc
        _shift_copies(extdca, shd, ts + hb - 8)

        for q in range(ts // rc):
            r0 = q * rc
            zr = z_ref[r0:r0 + rc, :]
            dga = None
            for k in range(ta):
                term = wa_ref[k:k + 1, :] * _shifted(extdca, shd, r0, rc, ta - 1 - k)
                dga = term if dga is None else dga + term
            sg = _sigmoid(zr[:, c:2 * c])
            dz_ref[r0:r0 + rc, 0:c] = (dga * sg).astype(BF16)
            dz_ref[r0:r0 + rc, c:2 * c] = (dga * zr[:, 0:c] * sg * (1.0 - sg)).astype(BF16)
            cbc = _conv_taps(extb, wb_ref, 3, hb - 2, r0, rc)
            dz_ref[r0:r0 + rc, 2 * c:3 * c] = (d_ref[r0:r0 + rc, c:2 * c] * cbc).astype(BF16)
            dcb = _conv_taps(extdcb, wb_ref, 3, 2, r0, rc, reverse=True)
            dz_ref[r0:r0 + rc, 3 * c:4 * c] = (dcb * zr[:, 4 * c:5 * c]).astype(BF16)
            dz_ref[r0:r0 + rc, 4 * c:5 * c] = (dcb * zr[:, 3 * c:4 * c]).astype(BF16)

        for k in range(ta):
            part = None
            for q in range(ts // rc):
                r0 = q * rc
                p = exta[hb + r0:hb + r0 + rc, :] * _shifted(extdca, shd, r0, rc, ta - 1 - k)
                for r in range(0, rc, 8):
                    part = p[r:r + 8, :] if part is None else part + p[r:r + 8, :]
            dwa_ref[k:k + 1, :] += jnp.sum(part, axis=0, keepdims=True)
        dcb_t = extdcb[0:ts, :]
        for k in range(3):
            dwb_ref[k:k + 1, :] += jnp.sum(dcb_t * extb[pl.ds(hb - 2 + k, ts), :], axis=0, keepdims=True)

    def tile(w):
        return pl.BlockSpec((ts, w), lambda i: (i, 0))

    def prev(w):
        return pl.BlockSpec((hb, w), lambda i: (_prev_blk(i, ts, hb), 0))

    def nxt(w):
        return pl.BlockSpec((hb, w), lambda i: (_next_blk(i, ts, hb, s), 0))

    def const(r, w):
        return pl.BlockSpec((r, w), lambda i: (0, 0))

    return pl.pallas_call(
        body, name=name, grid=(s // ts,),
        in_specs=[tile(5 * c), prev(5 * c), nxt(5 * c), tile(c), nxt(c), tile(2 * c), nxt(2 * c),
                  const(32, c), const(1, c), const(1, c), const(8, c)],
        out_specs=[tile(5 * c), const(32, c), const(8, c), const(1, c), const(1, c)],
        out_shape=[jax.ShapeDtypeStruct((s, 5 * c), BF16), jax.ShapeDtypeStruct((32, c), F32),
                   jax.ShapeDtypeStruct((8, c), F32), jax.ShapeDtypeStruct((1, c), F32),
                   jax.ShapeDtypeStruct((1, c), F32)],
        scratch_shapes=[pltpu.VMEM((hb + ts, c), F32), pltpu.VMEM((hb + ts, c), F32),
                        pltpu.VMEM((ts + hb, c), F32), pltpu.VMEM((ts + hb, c), F32),
                        pltpu.VMEM((8, ts + hb - 8, c), F32)],
        compiler_params=_params(("arbitrary",)),
    )(z, z, z, ca, ca, dab, dab, conv_a, ln_g, ln_b, conv_b)


def _ffn_up(x, g, w8, *, name, tm=1024):
    s, d = x.shape
    nb, c, _ = w8.shape
    tm = min(tm, s)

    def body(x_ref, g_ref, w_ref, h_ref, u_ref, hs_ref):
        @pl.when(pl.program_id(1) == 0)
        def _():
            hv = _rms_rows(x_ref[...], g_ref[...]).astype(BF16)
            hs_ref[...] = hv
            h_ref[...] = hv

        u_ref[...] = lax.dot_general(hs_ref[...], w_ref[...], _NT, preferred_element_type=F32).astype(BF16)

    return pl.pallas_call(
        body, name=name, grid=(s // tm, nb),
        in_specs=[pl.BlockSpec((tm, d), lambda i, k: (i, 0)),
                  pl.BlockSpec((1, d), lambda i, k: (0, 0)),
                  pl.BlockSpec((None, c, d), lambda i, k: (k, 0, 0))],
        out_specs=[pl.BlockSpec((tm, d), lambda i, k: (i, 0)),
                   pl.BlockSpec((None, tm, c), lambda i, k: (k, i, 0))],
        out_shape=[jax.ShapeDtypeStruct((s, d), BF16), jax.ShapeDtypeStruct((nb, s, c), BF16)],
        scratch_shapes=[pltpu.VMEM((tm, d), BF16)],
        compiler_params=_params(("parallel", "arbitrary")),
    )(x, g, w8)


def _ffn_mid(up8, wc8, *, name, ts=512, rc=32):
    nb, s, c = up8.shape
    hb = HALO_S
    ts = min(ts, s)

    def body(g_ref, gp_ref, v_ref, vp_ref, wg_ref, wv_ref, act_ref, ug_ref, uv_ref, extg, extv):
        keep = jnp.where(pl.program_id(0) > 0, 1.0, 0.0)
        extg[0:hb, :] = gp_ref[...].astype(F32) * keep
        extv[0:hb, :] = vp_ref[...].astype(F32) * keep
        extg[hb:hb + ts, :] = g_ref[...].astype(F32)
        extv[hb:hb + ts, :] = v_ref[...].astype(F32)
        for q in range(ts // rc):
            r0 = q * rc
            gg = _conv_taps(extg, wg_ref, 3, hb - 2, r0, rc)
            vv = _conv_taps(extv, wv_ref, 3, hb - 2, r0, rc)
            ug_ref[r0:r0 + rc, :] = gg.astype(BF16)
            uv_ref[r0:r0 + rc, :] = vv.astype(BF16)
            act_ref[r0:r0 + rc, :] = (gg * _sigmoid(gg) * vv).astype(BF16)

    def blk(rows, off, halo):
        if halo:
            return pl.BlockSpec((None, rows, c), lambda i, j: (j + off, _prev_blk(i, ts, hb), 0))
        return pl.BlockSpec((None, rows, c), lambda i, j: (j + off, i, 0))

    def taps(off):
        return pl.BlockSpec((None, 8, c), lambda i, j: (j + off, 0, 0))

    tile = pl.BlockSpec((None, ts, c), lambda i, j: (j, i, 0))
    out = jax.ShapeDtypeStruct((N_PAIR, s, c), BF16)
    return pl.pallas_call(
        body, name=name, grid=(s // ts, N_PAIR),
        in_specs=[blk(ts, 0, False), blk(hb, 0, True), blk(ts, N_PAIR, False), blk(hb, N_PAIR, True),
                  taps(0), taps(N_PAIR)],
        out_specs=[tile, tile, tile], out_shape=[out, out, out],
        scratch_shapes=[pltpu.VMEM((hb + ts, c), F32), pltpu.VMEM((hb + ts, c), F32)],
        compiler_params=_params(("parallel", "parallel")),
    )(up8, up8, up8, up8, wc8, wc8)


def _ffn_down(act4, wd4, x, *, name, tm=512):
    npair, s, c = act4.shape
    d = x.shape[1]
    tm = min(tm, s)

    def body(a_ref, w_ref, x_ref, o_ref):
        acc = x_ref[...]
        for j in range(npair):
            acc = acc + jnp.dot(a_ref[j], w_ref[j], preferred_element_type=F32)
        o_ref[...] = acc

    return pl.pallas_call(
        body, name=name, grid=(s // tm,),
        in_specs=[pl.BlockSpec((npair, tm, c), lambda i: (0, i, 0)),
                  pl.BlockSpec((npair, c, d), lambda i: (0, 0, 0)),
                  pl.BlockSpec((tm, d), lambda i: (i, 0))],
        out_specs=pl.BlockSpec((tm, d), lambda i: (i, 0)),
        out_shape=jax.ShapeDtypeStruct((s, d), F32),
        compiler_params=_params(("parallel",)),
    )(act4, wd4, x)


def _ffn_dact(db, wd4, *, name, tm=1024):
    s, d = db.shape
    npair, c, _ = wd4.shape
    tm = min(tm, s)

    def body(d_ref, w_ref, o_ref):
        o_ref[...] = lax.dot_general(d_ref[...], w_ref[...], (((1,), (1,)), ((), ())),
                                     preferred_element_type=F32).astype(BF16)

    return pl.pallas_call(
        body, name=name, grid=(s // tm, npair),
        in_specs=[pl.BlockSpec((tm, d), lambda i, j: (i, 0)),
                  pl.BlockSpec((None, c, d), lambda i, j: (j, 0, 0))],
        out_specs=pl.BlockSpec((None, tm, c), lambda i, j: (j, i, 0)),
        out_shape=jax.ShapeDtypeStruct((npair, s, c), BF16),
        compiler_params=_params(("parallel", "parallel")),
    )(db, wd4)


def _ffn_dwdown(act4, db, *, name):
    npair, s, c = act4.shape
    d = db.shape[1]

    def body(a_ref, d_ref, o_ref):
        o_ref[...] = lax.dot_general(a_ref[...], d_ref[...], (((0,), (0,)), ((), ())),
                                     preferred_element_type=F32).astype(BF16)

    return pl.pallas_call(
        body, name=name, grid=(npair,),
        in_specs=[pl.BlockSpec((None, s, c), lambda j: (j, 0, 0)),
                  pl.BlockSpec((s, d), lambda j: (0, 0))],
        out_specs=pl.BlockSpec((None, c, d), lambda j: (j, 0, 0)),
        out_shape=jax.ShapeDtypeStruct((npair, c, d), BF16),
        compiler_params=_params(("parallel",)),
    )(act4, db)


def _ffn_midbwd(up8, ug4, uv4, dact4, wc8, *, name, ts=512, rc=32):
    nb, s, c = up8.shape
    hb = HALO_S
    ts = min(ts, s)
    n_i = s // ts

    def body(pg_ref, pv_ref, ug_ref, ugn_ref, uv_ref, uvn_ref, d_ref, dn_ref, wg_ref, wv_ref,
             dg_ref, dv_ref, dwg_ref, dwv_ref, extdg, extdv):
        i = pl.program_id(1)
        keep_n = jnp.where(i < n_i - 1, 1.0, 0.0)

        @pl.when(i == 0)
        def _():
            dwg_ref[...] = jnp.zeros_like(dwg_ref)
            dwv_ref[...] = jnp.zeros_like(dwv_ref)

        def du_rows(r0, rows, gg, vv, dav):
            sg = _sigmoid(gg)
            extdg[r0:r0 + rows, :] = dav * vv * (sg * (1.0 + gg * (1.0 - sg)))
            extdv[r0:r0 + rows, :] = dav * (gg * sg)

        for q in range(ts // rc):
            rows = slice(q * rc, q * rc + rc)
            du_rows(q * rc, rc, ug_ref[rows, :].astype(F32), uv_ref[rows, :].astype(F32),
                    d_ref[rows, :].astype(F32))
        du_rows(ts, 8, ugn_ref[0:8, :].astype(F32), uvn_ref[0:8, :].astype(F32),
                dn_ref[0:8, :].astype(F32) * keep_n)

        def fold(p):
            acc = p[0:8, :]
            for r in range(8, rc, 8):
                acc = acc + p[r:r + 8, :]
            return acc

        for extd, p_ref, w_ref, out_ref, dw_ref in ((extdg, pg_ref, wg_ref, dg_ref, dwg_ref),
                                                    (extdv, pv_ref, wv_ref, dv_ref, dwv_ref)):
            part = [None, None, None]
            for q in range(ts // rc):
                r0 = q * rc
                pre = p_ref[r0:r0 + rc, :].astype(F32)
                dup = None
                for k in range(3):
                    sh = extd[pl.ds(r0 + 2 - k, rc), :]
                    term = w_ref[k:k + 1, :] * sh
                    dup = term if dup is None else dup + term
                    prod = fold(pre * sh)
                    part[k] = prod if part[k] is None else part[k] + prod
                out_ref[r0:r0 + rc, :] = dup.astype(BF16)
            for k in range(3):
                dw_ref[k:k + 1, :] += jnp.sum(part[k], axis=0, keepdims=True)

    def blk(off, nxt):
        if nxt:
            return pl.BlockSpec((None, hb, c), lambda j, i: (j + off, _next_blk(i, ts, hb, s), 0))
        return pl.BlockSpec((None, ts, c), lambda j, i: (j + off, i, 0))

    def taps(off):
        return pl.BlockSpec((None, 8, c), lambda j, i: (j + off, 0, 0))

    tile = blk(0, False)
    acc = pl.BlockSpec((None, 8, c), lambda j, i: (j, 0, 0))
    return pl.pallas_call(
        body, name=name, grid=(N_PAIR, n_i),
        in_specs=[tile, blk(N_PAIR, False), tile, blk(0, True), tile, blk(0, True), tile, blk(0, True),
                  taps(0), taps(N_PAIR)],
        out_specs=[tile, tile, acc, acc],
        out_shape=[jax.ShapeDtypeStruct((N_PAIR, s, c), BF16), jax.ShapeDtypeStruct((N_PAIR, s, c), BF16),
                   jax.ShapeDtypeStruct((N_PAIR, 8, c), F32), jax.ShapeDtypeStruct((N_PAIR, 8, c), F32)],
        scratch_shapes=[pltpu.VMEM((ts + 8, c), F32), pltpu.VMEM((ts + 8, c), F32)],
        compiler_params=_params(("parallel", "arbitrary")),
    )(up8, up8, ug4, ug4, uv4, uv4, dact4, dact4, wc8, wc8)


def _ffn_dh(dupg, dupv, w8, x, g, dres, dep, *, name, tm=256):
    npair, s, c = dupg.shape
    d = x.shape[1]
    tm = min(tm, s)

    def body(dg_ref, dv_ref, w_ref, x_ref, g_ref, dres_ref, dep_ref, dx_ref, dxb_ref, dgain_ref):
        @pl.when(pl.program_id(0) == 0)
        def _():
            dgain_ref[...] = jnp.zeros_like(dgain_ref)

        dh = None
        for j in range(npair):
            for src, k in ((dg_ref, j), (dv_ref, j + npair)):
                part = jnp.dot(src[j], w_ref[k], preferred_element_type=F32)
                dh = part if dh is None else dh + part
        dx, dgr = _rms_bwd_rows(x_ref[...], g_ref[...], dh)
        dx = dres_ref[...] + dx
        dx_ref[...] = dx
        dxb_ref[...] = dx.astype(BF16)
        dgain_ref[...] += jnp.sum(dgr, axis=0, keepdims=True)

    row = pl.BlockSpec((tm, d), lambda i: (i, 0))
    vec = pl.BlockSpec((1, d), lambda i: (0, 0))
    dup = pl.BlockSpec((npair, tm, c), lambda i: (0, i, 0))
    return pl.pallas_call(
        body, name=name, grid=(s // tm,),
        in_specs=[dup, dup, pl.BlockSpec((2 * npair, c, d), lambda i: (0, 0, 0)), row, vec, row,
                  pl.BlockSpec(memory_space=pl.ANY)],
        out_specs=[row, row, vec],
        out_shape=[jax.ShapeDtypeStruct((s, d), F32), jax.ShapeDtypeStruct((s, d), BF16),
                   jax.ShapeDtypeStruct((1, d), F32)],
        compiler_params=_params(("arbitrary",)),
    )(dupg, dupv, w8, x, g, dres, dep)


def _ffn_dwup(h, dupg, dupv, *, name, tm=512):
    npair, s, c = dupg.shape
    d = h.shape[1]

    def body(h_ref, dg_ref, dv_ref, o_ref):
        k = pl.program_id(1)

        @pl.when(k < npair)
        def _():
            o_ref[...] = lax.dot_general(dg_ref[...], h_ref[...], _TN, preferred_element_type=F32).astype(BF16)

        @pl.when(k >= npair)
        def _():
            o_ref[...] = lax.dot_general(dv_ref[...], h_ref[...], _TN, preferred_element_type=F32).astype(BF16)

    return pl.pallas_call(
        body, name=name, grid=(d // tm, 2 * npair),
        in_specs=[pl.BlockSpec((s, tm), lambda m, k: (0, m)),
                  pl.BlockSpec((None, s, c), lambda m, k: (jnp.minimum(k, npair - 1), 0, 0)),
                  pl.BlockSpec((None, s, c), lambda m, k: (jnp.maximum(k - npair, 0), 0, 0))],
        out_specs=pl.BlockSpec((None, c, tm), lambda m, k: (k, 0, m)),
        out_shape=jax.ShapeDtypeStruct((2 * npair, c, d), BF16),
        compiler_params=_params(("parallel", "arbitrary")),
    )(h, dupg, dupv)


def _pool_counts(i, ts, rows, window):
    t = lax.broadcasted_iota(jnp.int32, (rows, 1), 0) + i * ts + 1
    return jnp.minimum(t, window).astype(F32)


def _pool_fwd(x, g, *, name, ts=256):
    s, d = x.shape
    hb = HALO_A
    pg = POOL_GROUP
    ts = min(ts, s)

    def body(x_ref, xp_ref, g_ref, p_ref, ext, sa, sb):
        i = pl.program_id(0)
        keep = jnp.where(i > 0, 1.0, 0.0)
        gv = g_ref[...]
        ext[0:hb, :] = _rms_rows(xp_ref[...], gv) * keep
        ext[hb:hb + ts, :] = _rms_rows(x_ref[...], gv)
        rows = hb + ts - 8
        sa[0:8, :] = jnp.zeros((8, d), F32)
        sb[0:8, :] = jnp.zeros((8, d), F32)
        for gi, w in enumerate(POOL_WINDOWS):
            cols = slice(gi * pg, (gi + 1) * pg)
            cur, nxt, k = ext, sa, 1
            while k < w:
                nxt[8:8 + rows, cols] = cur[8:8 + rows, cols] + cur[pl.ds(8 - k, rows), cols]
                cur, nxt, k = nxt, (sb if nxt is sa else sa), 2 * k
            h = ext[hb:hb + ts, cols]
            p_ref[:, cols] = (cur[hb:hb + ts, cols] / _pool_counts(i, ts, ts, w) - h).astype(BF16)

    return pl.pallas_call(
        body, name=name, grid=(s // ts,),
        in_specs=[pl.BlockSpec((ts, d), lambda i: (i, 0)),
                  pl.BlockSpec((hb, d), lambda i: (_prev_blk(i, ts, hb), 0)),
                  pl.BlockSpec((1, d), lambda i: (0, 0))],
        out_specs=pl.BlockSpec((ts, d), lambda i: (i, 0)),
        out_shape=jax.ShapeDtypeStruct((s, d), BF16),
        scratch_shapes=[pltpu.VMEM((hb + ts, d), F32)] * 3,
        compiler_params=_params(("parallel",)),
    )(x, x, g)


def _pool_mm_fwd(p, w_pool, scale, x, *, name, ts=1024):
    s, d = x.shape
    pg = POOL_GROUP
    ts = min(ts, s)

    def body(p_ref, w_ref, s_ref, x_ref, o_ref, yu_ref):
        yu = jnp.dot(p_ref[...], w_ref[...], preferred_element_type=F32)
        yu_ref[...] = yu.astype(BF16)
        o_ref[...] = x_ref[...] + yu * s_ref[...]

    blk = pl.BlockSpec((ts, pg), lambda i, gi: (i, gi))
    return pl.pallas_call(
        body, name=name, grid=(s // ts, d // pg),
        in_specs=[blk, pl.BlockSpec((None, pg, pg), lambda i, gi: (gi, 0, 0)),
                  pl.BlockSpec((1, pg), lambda i, gi: (0, gi)), blk],
        out_specs=[blk, blk],
        out_shape=[jax.ShapeDtypeStruct((s, d), F32), jax.ShapeDtypeStruct((s, d), BF16)],
        compiler_params=_params(("parallel", "parallel")),
    )(p, w_pool, scale, x)


def _pool_mm_bwd(dres, w_pool, scale, yu, *, name, ts=1024):
    s, d = dres.shape
    pg = POOL_GROUP
    ts = min(ts, s)

    def body(d_ref, w_ref, s_ref, yu_ref, dyc_ref, dp_ref, ds_ref):
        @pl.when(pl.program_id(1) == 0)
        def _():
            ds_ref[...] = jnp.zeros_like(ds_ref)

        dv = d_ref[...]
        dyc = (dv * s_ref[...]).astype(BF16)
        dyc_ref[...] = dyc
        dp_ref[...] = lax.dot_general(dyc, w_ref[...], (((1,), (1,)), ((), ())), preferred_element_type=F32)
        ds_ref[...] += jnp.sum(dv * yu_ref[...].astype(F32), axis=0, keepdims=True)

    blk = pl.BlockSpec((ts, pg), lambda gi, i: (i, gi))
    vec = pl.BlockSpec((1, pg), lambda gi, i: (0, gi))
    return pl.pallas_call(
        body, name=name, grid=(d // pg, s // ts),
        in_specs=[blk, pl.BlockSpec((None, pg, pg), lambda gi, i: (gi, 0, 0)), vec, blk],
        out_specs=[blk, blk, vec],
        out_shape=[jax.ShapeDtypeStruct((s, d), BF16), jax.ShapeDtypeStruct((s, d), F32),
                   jax.ShapeDtypeStruct((1, d), F32)],
        compiler_params=_params(("parallel", "arbitrary")),
    )(dres, w_pool, scale, yu)


def _pool_dw(p, dyc, *, name):
    s, d = p.shape
    pg = POOL_GROUP

    def body(p_ref, d_ref, o_ref):
        o_ref[...] = lax.dot_general(p_ref[...], d_ref[...], (((0,), (0,)), ((), ())),
                                     preferred_element_type=F32).astype(BF16)

    blk = pl.BlockSpec((s, pg), lambda gi: (0, gi))
    return pl.pallas_call(
        body, name=name, grid=(d // pg,),
        in_specs=[blk, blk], out_specs=pl.BlockSpec((None, pg, pg), lambda gi: (gi, 0, 0)),
        out_shape=jax.ShapeDtypeStruct((d // pg, pg, pg), BF16),
        compiler_params=_params(("parallel",)),
    )(p, dyc)


def _pool_bwd(dp, x, g, dres, *, name, ts=256):
    s, d = x.shape
    hb = HALO_A
    pg = POOL_GROUP
    ts = min(ts, s)
    n_i = s // ts

    def body(dp_ref, dpn_ref, x_ref, g_ref, dres_ref, dx_ref, dxb_ref, dg_ref, ext, dh, sa, sb):
        i = pl.program_id(0)
        keep_n = jnp.where(i < n_i - 1, 1.0, 0.0)

        @pl.when(i == 0)
        def _():
            dg_ref[...] = jnp.zeros_like(dg_ref)

        for gi, w in enumerate(POOL_WINDOWS):
            cols = slice(gi * pg, (gi + 1) * pg)
            ext[0:ts, cols] = dp_ref[:, cols] / _pool_counts(i, ts, ts, w)
            ext[ts:ts + hb, cols] = dpn_ref[:, cols] / _pool_counts(i + 1, ts, hb, w) * keep_n
        rows = ts + hb - 8
        sa[rows:rows + 8, :] = jnp.zeros((8, d), F32)
        sb[rows:rows + 8, :] = jnp.zeros((8, d), F32)
        for gi, w in enumerate(POOL_WINDOWS):
            cols = slice(gi * pg, (gi + 1) * pg)
            cur, nxt, k = ext, sa, 1
            while k < w:
                nxt[0:rows, cols] = cur[0:rows, cols] + cur[pl.ds(k, rows), cols]
                cur, nxt, k = nxt, (sb if nxt is sa else sa), 2 * k
            dh[:, cols] = cur[0:ts, cols] - dp_ref[:, cols]
        dx, dgr = _rms_bwd_rows(x_ref[...], g_ref[...], dh[...])
        dx = dres_ref[...] + dx
        dx_ref[...] = dx
        dxb_ref[...] = dx.astype(BF16)
        dg_ref[...] += jnp.sum(dgr, axis=0, keepdims=True)

    row = pl.BlockSpec((ts, d), lambda i: (i, 0))
    vec = pl.BlockSpec((1, d), lambda i: (0, 0))
    return pl.pallas_call(
        body, name=name, grid=(n_i,),
        in_specs=[row, pl.BlockSpec((hb, d), lambda i: (_next_blk(i, ts, hb, s), 0)), row, vec, row],
        out_specs=[row, row, vec],
        out_shape=[jax.ShapeDtypeStruct((s, d), F32), jax.ShapeDtypeStruct((s, d), BF16),
                   jax.ShapeDtypeStruct((1, d), F32)],
        scratch_shapes=[pltpu.VMEM((ts + hb, d), F32), pltpu.VMEM((ts, d), F32),
                        pltpu.VMEM((ts + hb, d), F32), pltpu.VMEM((ts + hb, d), F32)],
        compiler_params=_params(("arbitrary",)),
    )(dp, dp, x, g, dres)


def _pad_rows(w, rows):
    pad = [(0, 0)] * (w.ndim - 2) + [(0, rows - w.shape[-2]), (0, 0)]
    return jnp.pad(w, pad)


def _ffn_layer_fwd(x, nf, w8, wc8, wd4, tag):
    h, up8 = _ffn_up(x, nf, w8, name=f"ffn{tag}_up")
    act4, ug4, uv4 = _ffn_mid(up8, wc8, name=f"ffn{tag}_mid")
    x_out = _ffn_down(act4, wd4, x, name=f"ffn{tag}_down")
    return x_out, (h, up8, ug4, uv4, act4)


def _ffn_layer_bwd(d, db, x, nf, w8, wc8, wd4, saved, tag, grads_ready):
    h, up8, ug4, uv4, act4 = saved
    dact4 = _ffn_dact(db, wd4, name=f"ffn{tag}_dact")
    dwd4 = _ffn_dwdown(act4, db, name=f"ffn{tag}_dwdown")
    dupg, dupv, dwg, dwv = _ffn_midbwd(up8, ug4, uv4, dact4, wc8, name=f"ffn{tag}_midbwd")
    dw8 = _ffn_dwup(h, dupg, dupv, name=f"ffn{tag}_dwup")
    sent = grads_ready(f"ffn{tag}", {f"w_up{tag}": dw8, f"w_down{tag}": dwd4})
    dx, dxb, dnf = _ffn_dh(dupg, dupv, w8, x, nf, d, sent, name=f"ffn{tag}_dh")
    dwc8 = jnp.concatenate([dwg, dwv], axis=0)[:, :3]
    return dx, dxb, dnf, dwc8


def _local_step(x, target, weights, grads_ready):
    w0 = weights("mix0", x)
    wa = _pad_rows(w0["conv_a"], 32)
    wb = _pad_rows(w0["conv_b"], 8)
    wc = [_pad_rows(w0["conv_ffn"][l], 8) for l in range(2)]
    h0, z = _rms_mm(x, w0["norm_mix_even"], w0["w_in_t"], name="mix0_in", out_dtype=F32, tn=1280)
    ab, ca = _mix0_fwd(z, wa, w0["ln_a_g"], w0["ln_a_b"], wb, name="mix0_mid")
    x1 = _mm(ab, w0["w_out"], add=x, name="mix0_out", tm=1024, tn=1024)
    w1 = weights("ffn0", x1)
    x2, ffn0 = _ffn_layer_fwd(x1, w0["norm_ffn"][0:1], w1["w_up"], wc[0], w1["w_down"], 0)
    w2 = weights("ffn1", x2)
    p = _pool_fwd(x2, w0["norm_mix_odd"], name="pool_mid")
    x3, yu = _pool_mm_fwd(p, w2["w_pool"], w0["pool_scale"], x2, name="pool_mm")
    x4, ffn1 = _ffn_layer_fwd(x3, w0["norm_ffn"][1:2], w2["w_up"], wc[1], w2["w_down"], 1)
    loss, d4, d4b, g_norm_final = _final_loss(x4, w0["norm_final"], target, name="final_loss")

    d3, d3b, g_nf1, g_wc1 = _ffn_layer_bwd(
        d4, d4b, x3, w0["norm_ffn"][1:2], w2["w_up"], wc[1], w2["w_down"], ffn1, 1, grads_ready)
    dyc, dp, g_scale = _pool_mm_bwd(d3, w2["w_pool"], w0["pool_scale"], yu, name="pool_mm_bwd")
    g_pool = _pool_dw(p, dyc, name="pool_dw")
    d2, d2b, g_nmo = _pool_bwd(dp, x2, w0["norm_mix_odd"], d3, name="pool_midbwd")
    d1, d1b, g_nf0, g_wc0 = _ffn_layer_bwd(
        d2, d2b, x1, w0["norm_ffn"][0:1], w1["w_up"], wc[0], w1["w_down"], ffn0, 0, grads_ready)
    dab = _mm(d1b, w0["w_out"], tb=True, name="mix0_dab", tm=1024, tn=1024)
    g_out = _mm(ab, d1b, ta=True, out_dtype=BF16, name="mix0_dwout", tm=1024, tn=512)
    dz, g_wa, g_wb, g_lg, g_lb = _mix0_bwd(z, ca, dab, wa, w0["ln_a_g"], w0["ln_a_b"], wb, name="mix0_midbwd")
    g_in = _mm(dz, h0, ta=True, out_dtype=BF16, name="mix0_dwin", tm=512, tn=1024)
    sent_mix = grads_ready("mix0", {"w_in": g_in, "w_out": g_out, "w_pool": g_pool})
    dx, g_nme = _mm_rms_bwd(dz, w0["w_in_t"], x, w0["norm_mix_even"], d1, sent_mix, name="mix0_dh")

    small = {
        "norm_mix_even": g_nme, "conv_a": g_wa[:A_TAPS], "ln_a_g": g_lg, "ln_a_b": g_lb, "conv_b": g_wb[:3],
        "norm_mix_odd": g_nmo, "pool_scale": g_scale, "norm_ffn": jnp.concatenate([g_nf0, g_nf1], axis=0),
        "conv_ffn": [g_wc0, g_wc1], "norm_final": g_norm_final,
    }
    return loss[0, 0], dx, small


def _my_pos():
    return lax.axis_index("x"), lax.axis_index("y"), lax.axis_index("c")


def _flip(pos, r):
    x, y, c = pos
    return (1 - x if r & 4 else x, 1 - y if r & 2 else y, 1 - c if r & 1 else c)


def _dev_index(pos):
    return 4 * pos[0] + 2 * pos[1] + pos[2]


_HBM = pl.BlockSpec(memory_space=pltpu.HBM)
_SEM = pl.BlockSpec(memory_space=pltpu.SEMAPHORE)
_EFFECT = pltpu.SideEffectType.DATAFLOW_SIDE_EFFECTING


def _exchange_copy(ins, lands, send_sems, recv_sems, scatter, pos, a, r, receiving):
    me = _dev_index(pos)
    peer = _flip(pos, r)
    dest = _dev_index(pos) if receiving else _dev_index(peer)
    src = ins[a].at[dest] if scatter[a] else ins[a]
    slot = _dev_index(peer) if receiving else me
    return pltpu.make_async_remote_copy(
        src_ref=src, dst_ref=lands[a].at[slot], send_sem=send_sems.at[a * (N_DEV - 1) + r - 1],
        recv_sem=recv_sems.at[a * (N_DEV - 1) + r - 1],
        device_id=peer, device_id_type=pl.DeviceIdType.MESH)


ALL_PEERS = tuple(range(1, N_DEV))
CHIP_PEERS = (1, 2, 4, 6)
FORWARDED = (2, 4, 6)


def _exchange_start(arrays, scatter, after, *, name, peers=ALL_PEERS):
    n = len(arrays)
    me = _dev_index(_my_pos())
    lands = []
    for arr, sc in zip(arrays, scatter):
        own = lax.dynamic_index_in_dim(arr, me, 0, keepdims=True) if sc else arr[None]
        shape = arr.shape if sc else (N_DEV,) + arr.shape
        lands.append(lax.dynamic_update_slice(lax.empty(shape, arr.dtype), own, (me,) + (0,) * (len(shape) - 1)))

    def body(*refs):
        ins, lnd = refs[:n], refs[n:2 * n]
        send_sems, recv_sems = refs[2 * n + 1], refs[2 * n + 2]
        token = refs[-1]
        pos = _my_pos()
        for a in range(n):
            for r in peers:
                _exchange_copy(ins, lnd, send_sems, recv_sems, scatter, pos, a, r, receiving=False).start()
        token[...] = jnp.zeros_like(token)

    bufs = [pltpu.with_memory_space_constraint(t, pltpu.HBM) for t in list(arrays) + lands]
    sems = pltpu.SemaphoreType.DMA((n * (N_DEV - 1),))
    res = pl.pallas_call(
        body, name=name,
        out_shape=(sems, sems, *[pltpu.HBM(t.shape, t.dtype) for t in bufs], jax.ShapeDtypeStruct((8, 128), F32)),
        in_specs=[_HBM] * (2 * n) + [pl.BlockSpec(memory_space=pl.ANY)],
        out_specs=(_SEM, _SEM, *[_HBM] * (2 * n), pl.BlockSpec(memory_space=pltpu.VMEM)),
        input_output_aliases={i: 2 + i for i in range(2 * n)},
        compiler_params=pltpu.CompilerParams(has_side_effects=_EFFECT),
    )(*bufs, after)
    return res[0], res[1], list(res[2:2 + n]), list(res[2 + n:2 + 2 * n]), res[-1]


def _exchange_wait(started, scatter, after, *, name, peers=ALL_PEERS):
    send_sems, recv_sems, arrays, lands, _ = started
    n = len(arrays)

    def body(*refs):
        ins, lnd = refs[:n], refs[n:2 * n]
        send_sems, recv_sems = refs[2 * n], refs[2 * n + 1]
        pos = _my_pos()
        for a in range(n):
            for r in peers:
                _exchange_copy(ins, lnd, send_sems, recv_sems, scatter, pos, a, r, receiving=False).wait_send()
                _exchange_copy(ins, lnd, send_sems, recv_sems, scatter, pos, a, r, receiving=True).wait_recv()

    bufs = list(arrays) + list(lands)
    res = pl.pallas_call(
        body, name=name,
        out_shape=tuple(pltpu.HBM(t.shape, t.dtype) for t in bufs),
        in_specs=[_HBM] * (2 * n) + [_SEM, _SEM, pl.BlockSpec(memory_space=pl.ANY)],
        out_specs=tuple([_HBM] * (2 * n)),
        input_output_aliases={i: i for i in range(2 * n)},
        compiler_params=pltpu.CompilerParams(has_side_effects=_EFFECT),
    )(*bufs, send_sems, recv_sems, after)
    return list(res[n:])


def _forward_copy(lands, send_sems, recv_sems, pos, a, q, receiving):
    slot = _dev_index(_flip(pos, q ^ 1 if receiving else q))
    idx = a * len(FORWARDED) + FORWARDED.index(q)
    return pltpu.make_async_remote_copy(
        src_ref=lands[a].at[slot], dst_ref=lands[a].at[slot], send_sem=send_sems.at[idx], recv_sem=recv_sems.at[idx],
        device_id=_flip(pos, 1), device_id_type=pl.DeviceIdType.MESH)


def _forward_start(lands, after, *, name):
    n = len(lands)

    def body(*refs):
        lnd = refs[:n]
        send_sems, recv_sems = refs[n + 1], refs[n + 2]
        token = refs[-1]
        pos = _my_pos()
        for a in range(n):
            for q in FORWARDED:
                _forward_copy(lnd, send_sems, recv_sems, pos, a, q, receiving=False).start()
        token[...] = jnp.zeros_like(token)

    sems = pltpu.SemaphoreType.DMA((n * len(FORWARDED),))
    res = pl.pallas_call(
        body, name=name,
        out_shape=(sems, sems, *[pltpu.HBM(t.shape, t.dtype) for t in lands], jax.ShapeDtypeStruct((8, 128), F32)),
        in_specs=[_HBM] * n + [pl.BlockSpec(memory_space=pl.ANY)],
        out_specs=(_SEM, _SEM, *[_HBM] * n, pl.BlockSpec(memory_space=pltpu.VMEM)),
        input_output_aliases={i: 2 + i for i in range(n)},
        compiler_params=pltpu.CompilerParams(has_side_effects=_EFFECT),
    )(*lands, after)
    return res[0], res[1], list(res[2:2 + n]), res[-1]


def _forward_wait(forwarded, after, *, name):
    send_sems, recv_sems, lands, _ = forwarded
    n = len(lands)

    def body(*refs):
        lnd = refs[:n]
        send_sems, recv_sems = refs[n], refs[n + 1]
        pos = _my_pos()
        for a in range(n):
            for q in FORWARDED:
                _forward_copy(lnd, send_sems, recv_sems, pos, a, q, receiving=False).wait_send()
                _forward_copy(lnd, send_sems, recv_sems, pos, a, q, receiving=True).wait_recv()

    res = pl.pallas_call(
        body, name=name,
        out_shape=tuple(pltpu.HBM(t.shape, t.dtype) for t in lands),
        in_specs=[_HBM] * n + [_SEM, _SEM, pl.BlockSpec(memory_space=pl.ANY)],
        out_specs=tuple([_HBM] * n),
        input_output_aliases={i: i for i in range(n)},
        compiler_params=pltpu.CompilerParams(has_side_effects=_EFFECT),
    )(*lands, send_sems, recv_sems, after)
    return list(res)


def _adamw(parts, w, m, v, *, name, tr):
    nl, r, c = w.shape
    assert len(parts) == nl and r % tr == 0
    n_i = r // tr

    def body(*refs):
        p_refs = refs[:nl]
        w_ref, m_ref, v_ref, g_ref, d_ref, mo_ref, vo_ref = refs[nl:]

        def update(p_ref):
            g = p_ref[0].astype(F32)
            for k in range(1, N_DEV):
                g = g + p_ref[k].astype(F32)
            mn = ADAM_B1 * m_ref[...] + (1.0 - ADAM_B1) * g
            vn = ADAM_B2 * v_ref[...] + (1.0 - ADAM_B2) * (g * g)
            m_hat = mn / (1.0 - ADAM_B1 ** ADAM_STEP)
            v_hat = vn / (1.0 - ADAM_B2 ** ADAM_STEP)
            g_ref[...] = g
            d_ref[...] = -ADAM_LR * (m_hat / (jnp.sqrt(v_hat) + ADAM_EPS) + ADAM_WD * w_ref[...])
            mo_ref[...] = mn
            vo_ref[...] = vn

        if nl == 1:
            update(p_refs[0])
        else:
            for layer in range(nl):
                pl.when(pl.program_id(0) == layer)(lambda p_ref=p_refs[layer]: update(p_ref))

    def parts_spec(layer):
        def index(l, i):
            return (0, jnp.where(l < layer, 0, jnp.where(l > layer, n_i - 1, i)), 0)
        return pl.BlockSpec((N_DEV, tr, c), index)

    blk = pl.BlockSpec((None, tr, c), lambda l, i: (l, i, 0))
    return pl.pallas_call(
        body, name=name, grid=(nl, n_i),
        in_specs=[parts_spec(layer) for layer in range(nl)] + [blk, blk, blk],
        out_specs=[blk] * 4, out_shape=[jax.ShapeDtypeStruct((nl, r, c), F32)] * 4,
        compiler_params=_params(("arbitrary", "arbitrary")),
    )(*parts, w, m, v)


def _pack(parts, lead=()):
    flat = jnp.concatenate([p.reshape(lead + (-1,)) for p in parts], axis=-1)
    n = flat.shape[-1]
    rows = -(-n // (8 * 128)) * 8
    flat = jnp.pad(flat, [(0, 0)] * len(lead) + [(0, rows * 128 - n)])
    return flat.reshape(lead + (rows, 128))


def _unpack(slab, shapes):
    flat = slab.reshape(-1)
    out, off = [], 0
    for shp in shapes:
        size = 1
        for dim in shp:
            size *= dim
        out.append(flat[off:off + size].reshape(shp))
        off += size
    return out


def _to_dev_major(g, axis):
    shp = g.shape
    g = g.reshape(shp[:axis] + (N_DEV, shp[axis] // N_DEV) + shp[axis + 1:])
    return jnp.moveaxis(g, axis, 0)


def _from_dev_major(g, axis):
    g = jnp.moveaxis(g, 0, axis)
    shp = g.shape
    return g.reshape(shp[:axis] + (shp[axis] * shp[axis + 1],) + shp[axis + 2:])


SMALL_SHARDED = ("conv_a", "conv_b", "norm_mix_odd", "pool_scale", "conv_ffn_w")
SMALL_REPLICATED = ("norm_mix_even", "ln_a_g", "ln_a_b", "norm_ffn", "norm_final")
BIG = {"w_in": ("w_in", 0, 320), "w_out": ("w_out", 0, 128), "w_pool": ("w_pool", 0, 128),
       "w_up0": ("w_up", 0, 352), "w_up1": ("w_up", 1, 352),
       "w_down0": ("w_down", 0, 352), "w_down1": ("w_down", 1, 352)}
COLUMN_SHARDED = ("w_in", "w_up")


def kernel(x, norm_mix_even, w_in, conv_a, ln_a_g, ln_a_b, conv_b, w_out, norm_mix_odd, w_pool, pool_scale, norm_ffn, w_up, conv_ffn_w, w_down, norm_final, loss_target, m_norm_mix_even, m_w_in, m_conv_a, m_ln_a_g, m_ln_a_b, m_conv_b, m_w_out, m_norm_mix_odd, m_w_pool, m_pool_scale, m_norm_ffn, m_w_up, m_conv_ffn_w, m_w_down, m_norm_final, v_norm_mix_even, v_w_in, v_conv_a, v_ln_a_g, v_ln_a_b, v_conv_b, v_w_out, v_norm_mix_odd, v_w_pool, v_pool_scale, v_norm_ffn, v_w_up, v_conv_ffn_w, v_w_down, v_norm_final):
    names = ("norm_mix_even", "w_in", "conv_a", "ln_a_g", "ln_a_b", "conv_b", "w_out", "norm_mix_odd", "w_pool",
             "pool_scale", "norm_ffn", "w_up", "conv_ffn_w", "w_down", "norm_final")
    wts = dict(zip(names, (norm_mix_even, w_in, conv_a, ln_a_g, ln_a_b, conv_b, w_out, norm_mix_odd, w_pool,
                           pool_scale, norm_ffn, w_up, conv_ffn_w, w_down, norm_final)))
    mom = dict(zip(names, (m_norm_mix_even, m_w_in, m_conv_a, m_ln_a_g, m_ln_a_b, m_conv_b, m_w_out, m_norm_mix_odd,
                           m_w_pool, m_pool_scale, m_norm_ffn, m_w_up, m_conv_ffn_w, m_w_down, m_norm_final)))
    var = dict(zip(names, (v_norm_mix_even, v_w_in, v_conv_a, v_ln_a_g, v_ln_a_b, v_conv_b, v_w_out, v_norm_mix_odd,
                           v_w_pool, v_pool_scale, v_norm_ffn, v_w_up, v_conv_ffn_w, v_w_down, v_norm_final)))
    d = x.shape[-1]

    def shard3d(t, param):
        a = t[param]
        if param in COLUMN_SHARDED:
            return jnp.swapaxes(a, 1, 2)
        return a.reshape(a.shape[0], -1, a.shape[-1])

    def unshard3d(a, param):
        if param in COLUMN_SHARDED:
            return jnp.swapaxes(a, 1, 2)
        return a.reshape(wts[param].shape)

    def shard2d(t, key):
        param, layer, _ = BIG[key]
        return shard3d(t, param)[layer]

    small_w = _pack([wts[k] for k in SMALL_SHARDED])
    bf = {k: shard2d(wts, k).astype(BF16) for k in BIG}
    gather_groups = {"mix0": ("w_in", "w_out", "small"), "ffn0": ("w_up0", "w_down0"),
                     "ffn1": ("w_pool", "w_up1", "w_down1")}
    order = list(gather_groups)
    started = {}

    def start_gather(grp, after):
        arrs = [small_w if k == "small" else bf[k] for k in gather_groups[grp]]
        started[grp] = _exchange_start(arrs, [False] * len(arrs), after, name=f"gather_{grp}_start", peers=CHIP_PEERS)

    start_gather(order[0], small_w)

    def weights(grp, after):
        keys = gather_groups[grp]
        lands = _exchange_wait(started[grp], [False] * len(keys), after, name=f"gather_{grp}_wait", peers=CHIP_PEERS)
        forwarded = _forward_start(lands, small_w, name=f"gather_{grp}_forward")
        if grp != order[-1]:
            start_gather(order[order.index(grp) + 1], forwarded[-1])
            behind = started[order[order.index(grp) + 1]][-1]
        else:
            behind = forwarded[-1]
        gw = dict(zip(keys, _forward_wait(forwarded, behind, name=f"gather_{grp}_forward_wait")))
        if grp == "ffn0":
            return {"w_up": gw["w_up0"], "w_down": gw["w_down0"].reshape(N_PAIR, -1, d)}
        if grp == "ffn1":
            return {"w_up": gw["w_up1"], "w_down": gw["w_down1"].reshape(N_PAIR, -1, d),
                    "w_pool": _from_dev_major(gw["w_pool"].reshape(N_DEV, len(POOL_WINDOWS), -1, POOL_GROUP), 1)}
        per_dev = gw["small"].reshape(N_DEV, -1)
        sizes = [wts[k].size for k in SMALL_SHARDED]
        offs = [sum(sizes[:i]) for i in range(len(sizes))]
        small_full = {k: per_dev[:, o:o + n_].reshape((N_DEV,) + wts[k].shape)
                      for k, o, n_ in zip(SMALL_SHARDED, offs, sizes)}
        return {
            "norm_mix_even": norm_mix_even, "ln_a_g": ln_a_g, "ln_a_b": ln_a_b, "norm_ffn": norm_ffn,
            "norm_final": norm_final[None],
            "w_in_t": gw["w_in"].reshape(-1, d),
            "w_out": gw["w_out"].reshape(-1, d),
            "conv_a": _from_dev_major(small_full["conv_a"][:, 0], 1),
            "conv_b": _from_dev_major(small_full["conv_b"][:, 0], 1),
            "norm_mix_odd": _from_dev_major(small_full["norm_mix_odd"], 1),
            "pool_scale": _from_dev_major(small_full["pool_scale"], 1),
            "conv_ffn": [small_full["conv_ffn_w"][:, l] for l in range(2)],
        }

    def dev_major(k, g):
        if k == "w_pool":
            return _to_dev_major(g, 1).reshape(N_DEV, -1, POOL_GROUP)
        return g.reshape(N_DEV, -1, g.shape[-1])

    sent = {}

    def grads_ready(grp, grads):
        keys = tuple(grads)
        parts = [dev_major(k, grads[k]) for k in keys]
        sent[grp] = (keys, _exchange_start(parts, [True] * len(keys), small_w, name=f"grads_{grp}_start"))
        return sent[grp][1][-1]

    loss, dx, g = _local_step(x[0], loss_target[0], weights, grads_ready)

    small_parts = _pack([
        _to_dev_major(g["conv_a"], 1), _to_dev_major(g["conv_b"], 1), _to_dev_major(g["norm_mix_odd"], 1),
        _to_dev_major(g["pool_scale"], 1), jnp.stack(g["conv_ffn"], axis=1)], lead=(N_DEV,))
    repl_parts = _pack([g[k] for k in SMALL_REPLICATED])
    sent["small"] = (("small", "replicated"),
                     _exchange_start([small_parts, repl_parts], [True, False], dx, name="grads_small_start"))

    landed, after = {}, sent["small"][1][-1]
    for grp in ("ffn1", "ffn0", "mix0"):
        keys, st = sent[grp]
        landed.update(zip(keys, _exchange_wait(st, [True] * len(keys), after, name=f"grads_{grp}_wait")))

    out = {}
    for param, keys in (("w_up", ("w_up0", "w_up1")), ("w_down", ("w_down0", "w_down1")), ("w_in", ("w_in",)),
                        ("w_out", ("w_out",)), ("w_pool", ("w_pool",))):
        res = _adamw([landed[k] for k in keys], shard3d(wts, param), shard3d(mom, param), shard3d(var, param),
                     name=f"adamw_{param}", tr=BIG[keys[0]][2])
        out[param] = [unshard3d(t, param) for t in res]
    keys, st = sent["small"]
    landed.update(zip(keys, _exchange_wait(st, [True, False], out["w_pool"][1], name="grads_small_wait")))

    def slab(t):
        return t[None]

    res = _adamw([landed["small"]], slab(small_w), slab(_pack([mom[k] for k in SMALL_SHARDED])),
                 slab(_pack([var[k] for k in SMALL_SHARDED])), name="adamw_small", tr=small_w.shape[0])
    unpacked = [_unpack(t, [wts[k].shape for k in SMALL_SHARDED]) for t in res]
    for j, k in enumerate(SMALL_SHARDED):
        out[k] = [u[j] for u in unpacked]
    repl_w = _pack([wts[k] for k in SMALL_REPLICATED])
    res = _adamw([landed["replicated"]], slab(repl_w), slab(_pack([mom[k] for k in SMALL_REPLICATED])),
                 slab(_pack([var[k] for k in SMALL_REPLICATED])), name="adamw_replicated", tr=repl_w.shape[0])
    unpacked = [_unpack(t, [wts[k].shape for k in SMALL_REPLICATED]) for t in res]
    for j, k in enumerate(SMALL_REPLICATED):
        out[k] = [u[j] for u in unpacked]

    loss = lax.psum(loss, ("x", "y", "c"))
    return (loss, dx[None], *[out[k][0] for k in names], *[out[k][1] for k in names],
            *[out[k][2] for k in names], *[out[k][3] for k in names])
```

```python
import jax
import jax.numpy as jnp
from jax import lax
from jax.experimental import pallas as pl
from jax.experimental.pallas import tpu as pltpu

F32 = jnp.float32
BF16 = jnp.bfloat16

RMS_EPS = 1e-6
LN_EPS = 1e-5
ADAM_LR = 0.001
ADAM_B1 = 0.9
ADAM_B2 = 0.999
ADAM_EPS = 1e-08
ADAM_WD = 0.01
ADAM_STEP = 10

N_DEV = 8
N_PAIR = N_DEV // 2
A_WIDTH = 512
A_TAPS = 31
POOL_WINDOWS = (2, 4, 8, 16)
POOL_GROUP = 256
HALO_A = 32
HALO_S = 16
VMEM_LIMIT = 56 * 1024 * 1024


def _params(sem, vmem=VMEM_LIMIT):
    return pltpu.CompilerParams(dimension_semantics=sem, vmem_limit_bytes=vmem)


def _sigmoid(x):
    return 0.5 * jnp.tanh(0.5 * x) + 0.5


def _prev_blk(i, ts, hb):
    return jnp.maximum(i * (ts // hb) - 1, 0)


def _next_blk(i, ts, hb, s):
    return jnp.minimum((i + 1) * (ts // hb), s // hb - 1)


def _mm(a, b, *, name, ta=False, tb=False, add=None, out_dtype=F32, tm=512, tn=512, tk=None):
    m, k = (a.shape[1], a.shape[0]) if ta else a.shape
    n = b.shape[0] if tb else b.shape[1]
    tk = k if tk is None else tk
    tm, tn, tk = min(tm, m), min(tn, n), min(tk, k)
    assert m % tm == 0 and n % tn == 0 and k % tk == 0, (name, m, n, k, tm, tn, tk)
    nk = k // tk
    dims = (((0,) if ta else (1,), (1,) if tb else (0,)), ((), ()))
    n_in = 2 + (add is not None)

    def body(*refs):
        a_ref, b_ref = refs[0], refs[1]
        add_ref = refs[2] if add is not None else None
        o_ref = refs[n_in]
        part = lax.dot_general(a_ref[...].astype(BF16), b_ref[...].astype(BF16), dims, preferred_element_type=F32)

        def finish(r):
            if add_ref is not None:
                r = r + add_ref[...]
            o_ref[...] = r.astype(out_dtype)

        if nk == 1:
            finish(part)
            return
        acc_ref = refs[-1]
        kk = pl.program_id(2)

        @pl.when(kk == 0)
        def _():
            acc_ref[...] = part

        @pl.when(kk > 0)
        def _():
            acc_ref[...] += part

        @pl.when(kk == nk - 1)
        def _():
            finish(acc_ref[...])

    a_spec = pl.BlockSpec((tk, tm), lambda i, j, kk: (kk, i)) if ta else pl.BlockSpec((tm, tk), lambda i, j, kk: (i, kk))
    b_spec = pl.BlockSpec((tn, tk), lambda i, j, kk: (j, kk)) if tb else pl.BlockSpec((tk, tn), lambda i, j, kk: (kk, j))
    in_specs = [a_spec, b_spec]
    args = [a, b]
    if add is not None:
        in_specs.append(pl.BlockSpec((tm, tn), lambda i, j, kk: (i, j)))
        args.append(add)
    return pl.pallas_call(
        body, name=name, grid=(m // tm, n // tn, nk),
        in_specs=in_specs, out_specs=pl.BlockSpec((tm, tn), lambda i, j, kk: (i, j)),
        out_shape=jax.ShapeDtypeStruct((m, n), out_dtype),
        scratch_shapes=[pltpu.VMEM((tm, tn), F32)] if nk > 1 else [],
        compiler_params=_params(("parallel", "parallel", "arbitrary")),
    )(*args)


def _rms_rows(xv, gv):
    return xv * lax.rsqrt(jnp.mean(xv * xv, axis=-1, keepdims=True) + RMS_EPS) * gv


_NT = (((1,), (1,)), ((), ()))
_TN = (((0,), (0,)), ((), ()))


def _rms_mm(x, g, wt, *, name, out_dtype, tm=1024, tn=512):
    s, d = x.shape
    n = wt.shape[0]
    tm = min(tm, s)
    assert s % tm == 0 and n % tn == 0

    def body(x_ref, g_ref, w_ref, h_ref, z_ref, hs_ref):
        @pl.when(pl.program_id(1) == 0)
        def _():
            hv = _rms_rows(x_ref[...], g_ref[...]).astype(BF16)
            hs_ref[...] = hv
            h_ref[...] = hv

        z_ref[...] = lax.dot_general(hs_ref[...], w_ref[...], _NT, preferred_element_type=F32).astype(out_dtype)

    return pl.pallas_call(
        body, name=name, grid=(s // tm, n // tn),
        in_specs=[pl.BlockSpec((tm, d), lambda i, j: (i, 0)),
                  pl.BlockSpec((1, d), lambda i, j: (0, 0)),
                  pl.BlockSpec((tn, d), lambda i, j: (j, 0))],
        out_specs=[pl.BlockSpec((tm, d), lambda i, j: (i, 0)),
                   pl.BlockSpec((tm, tn), lambda i, j: (i, j))],
        out_shape=[jax.ShapeDtypeStruct((s, d), BF16), jax.ShapeDtypeStruct((s, n), out_dtype)],
        scratch_shapes=[pltpu.VMEM((tm, d), BF16)],
        compiler_params=_params(("parallel", "arbitrary")),
    )(x, g, wt)


def _rms_bwd_rows(xv, gv, dh):
    r = lax.rsqrt(jnp.mean(xv * xv, axis=-1, keepdims=True) + RMS_EPS)
    xh = xv * r
    dn = dh * gv
    dx = r * (dn - xh * jnp.mean(dn * xh, axis=-1, keepdims=True))
    return dx, dh * xh


def _mm_rms_bwd(a, b, x, g, dres, dep, *, name, tm=512):
    s, k = a.shape
    d = b.shape[1]
    tm = min(tm, s)

    def body(a_ref, b_ref, x_ref, g_ref, dres_ref, dep_ref, dx_ref, dg_ref):
        @pl.when(pl.program_id(0) == 0)
        def _():
            dg_ref[...] = jnp.zeros_like(dg_ref)

        dh = jnp.dot(a_ref[...], b_ref[...], preferred_element_type=F32)
        dx, dgr = _rms_bwd_rows(x_ref[...], g_ref[...], dh)
        dx_ref[...] = dres_ref[...] + dx
        dg_ref[...] += jnp.sum(dgr, axis=0, keepdims=True)

    row = pl.BlockSpec((tm, d), lambda i: (i, 0))
    vec = pl.BlockSpec((1, d), lambda i: (0, 0))
    return pl.pallas_call(
        body, name=name, grid=(s // tm,),
        in_specs=[pl.BlockSpec((tm, k), lambda i: (i, 0)), pl.BlockSpec((k, d), lambda i: (0, 0)), row, vec, row,
                  pl.BlockSpec(memory_space=pl.ANY)],
        out_specs=[row, vec],
        out_shape=[jax.ShapeDtypeStruct((s, d), F32), jax.ShapeDtypeStruct((1, d), F32)],
        compiler_params=_params(("arbitrary",)),
    )(a, b, x, g, dres, dep)


def _final_loss(x, g, target, *, name, ts=256):
    s, d = x.shape
    ts = min(ts, s)

    def body(x_ref, g_ref, t_ref, loss_ref, dx_ref, dxb_ref, dg_ref):
        @pl.when(pl.program_id(0) == 0)
        def _():
            loss_ref[...] = jnp.zeros_like(loss_ref)
            dg_ref[...] = jnp.zeros_like(dg_ref)

        xv = x_ref[...]
        gv = g_ref[...]
        r = lax.rsqrt(jnp.mean(xv * xv, axis=-1, keepdims=True) + RMS_EPS)
        xh = xv * r
        err = xh * gv - t_ref[...]
        loss_ref[...] += 0.5 * jnp.sum(jnp.mean(err * err, axis=-1, keepdims=True), axis=0, keepdims=True)
        dy = err * (1.0 / d)
        dn = dy * gv
        dx = r * (dn - xh * jnp.mean(dn * xh, axis=-1, keepdims=True))
        dx_ref[...] = dx
        dxb_ref[...] = dx.astype(BF16)
        dg_ref[...] += jnp.sum(dy * xh, axis=0, keepdims=True)

    row = pl.BlockSpec((ts, d), lambda i: (i, 0))
    vec = pl.BlockSpec((1, d), lambda i: (0, 0))
    return pl.pallas_call(
        body, name=name, grid=(s // ts,),
        in_specs=[row, vec, row],
        out_specs=[pl.BlockSpec((1, 1), lambda i: (0, 0)), row, row, vec],
        out_shape=[jax.ShapeDtypeStruct((1, 1), F32), jax.ShapeDtypeStruct((s, d), F32),
                   jax.ShapeDtypeStruct((s, d), BF16), jax.ShapeDtypeStruct((1, d), F32)],
        compiler_params=_params(("arbitrary",)),
    )(x, g, target)


def _conv_taps(ext_ref, w_ref, n_taps, base, r0, rows, reverse=False):
    acc = None
    for k in range(n_taps):
        off = r0 + (base - k if reverse else base + k)
        term = w_ref[k:k + 1, :] * ext_ref[pl.ds(off, rows), :]
        acc = term if acc is None else acc + term
    return acc


def _shift_copies(src_ref, sh_ref, rows):
    for b in range(1, 8):
        sh_ref[b, 0:rows, :] = src_ref[pl.ds(b, rows), :]


def _shifted(src_ref, sh_ref, start, rows, off):
    a, b = divmod(off, 8)
    ref = src_ref if b == 0 else sh_ref.at[b]
    return ref[pl.ds(start + 8 * a, rows), :]


def _mix0_fwd(z, conv_a, ln_g, ln_b, conv_b, *, name, ts=256, rc=32):
    s = z.shape[0]
    c = A_WIDTH
    hb = HALO_A

    def body(z_ref, zp_ref, wa_ref, lg_ref, lb_ref, wb_ref, ab_ref, ca_ref, exta, extb, sha):
        keep = jnp.where(pl.program_id(0) > 0, 1.0, 0.0)
        zp = zp_ref[...]
        exta[0:hb, :] = zp[:, 0:c] * _sigmoid(zp[:, c:2 * c]) * keep
        extb[0:hb, :] = zp[:, 3 * c:4 * c] * zp[:, 4 * c:5 * c] * keep
        exta[hb:hb + ts, :] = z_ref[:, 0:c] * _sigmoid(z_ref[:, c:2 * c])
        extb[hb:hb + ts, :] = z_ref[:, 3 * c:4 * c] * z_ref[:, 4 * c:5 * c]
        _shift_copies(exta, sha, hb + ts - 8)
        lg = lg_ref[...]
        lb = lb_ref[...]
        for q in range(ts // rc):
            r0 = q * rc
            ca = None
            for k in range(A_TAPS):
                term = wa_ref[k:k + 1, :] * _shifted(exta, sha, r0, rc, hb - (A_TAPS - 1) + k)
                ca = term if ca is None else ca + term
            ca_ref[r0:r0 + rc, :] = ca
            mu = jnp.mean(ca, axis=-1, keepdims=True)
            xc = ca - mu
            rs = lax.rsqrt(jnp.mean(xc * xc, axis=-1, keepdims=True) + LN_EPS)
            l = xc * rs * lg + lb
            ab_ref[r0:r0 + rc, 0:c] = (l * _sigmoid(l)).astype(BF16)
            cbc = _conv_taps(extb, wb_ref, 3, hb - 2, r0, rc)
            ab_ref[r0:r0 + rc, c:2 * c] = (z_ref[r0:r0 + rc, 2 * c:3 * c] * cbc).astype(BF16)

    return pl.pallas_call(
        body, name=name, grid=(s // ts,),
        in_specs=[pl.BlockSpec((ts, 5 * c), lambda i: (i, 0)),
                  pl.BlockSpec((hb, 5 * c), lambda i: (_prev_blk(i, ts, hb), 0)),
                  pl.BlockSpec((32, c), lambda i: (0, 0)),
                  pl.BlockSpec((1, c), lambda i: (0, 0)),
                  pl.BlockSpec((1, c), lambda i: (0, 0)),
                  pl.BlockSpec((8, c), lambda i: (0, 0))],
        out_specs=[pl.BlockSpec((ts, 2 * c), lambda i: (i, 0)),
                   pl.BlockSpec((ts, c), lambda i: (i, 0))],
        out_shape=[jax.ShapeDtypeStruct((s, 2 * c), BF16), jax.ShapeDtypeStruct((s, c), F32)],
        scratch_shapes=[pltpu.VMEM((hb + ts, c), F32), pltpu.VMEM((hb + ts, c), F32),
                        pltpu.VMEM((8, hb + ts - 8, c), F32)],
        compiler_params=_params(("parallel",)),
    )(z, z, conv_a, ln_g, ln_b, conv_b)


def _mix0_bwd(z, ca, dab, conv_a, ln_g, ln_b, conv_b, *, name, ts=256, rc=32):
    s = z.shape[0]
    c = A_WIDTH
    hb = HALO_A
    ta = A_TAPS

    def body(z_ref, zp_ref, zn_ref, ca_ref, can_ref, d_ref, dn_ref, wa_ref, lg_ref, lb_ref, wb_ref,
             dz_ref, dwa_ref, dwb_ref, dlg_ref, dlb_ref, exta, extb, extdca, extdcb, shd):
        i = pl.program_id(0)
        keep_p = jnp.where(i > 0, 1.0, 0.0)
        keep_n = jnp.where(i < s // ts - 1, 1.0, 0.0)

        @pl.when(i == 0)
        def _():
            dwa_ref[...] = jnp.zeros_like(dwa_ref)
            dwb_ref[...] = jnp.zeros_like(dwb_ref)
            dlg_ref[...] = jnp.zeros_like(dlg_ref)
            dlb_ref[...] = jnp.zeros_like(dlb_ref)

        lg = lg_ref[...]
        lb = lb_ref[...]
        zp = zp_ref[...]
        exta[0:hb, :] = zp[:, 0:c] * _sigmoid(zp[:, c:2 * c]) * keep_p
        extb[0:hb, :] = zp[:, 3 * c:4 * c] * zp[:, 4 * c:5 * c] * keep_p
        exta[hb:hb + ts, :] = z_ref[:, 0:c] * _sigmoid(z_ref[:, c:2 * c])
        extb[hb:hb + ts, :] = z_ref[:, 3 * c:4 * c] * z_ref[:, 4 * c:5 * c]

        def ln_bwd(cav, dav):
            mu = jnp.mean(cav, axis=-1, keepdims=True)
            xc = cav - mu
            rs = lax.rsqrt(jnp.mean(xc * xc, axis=-1, keepdims=True) + LN_EPS)
            nv = xc * rs
            l = nv * lg + lb
            sg = _sigmoid(l)
            dl = dav * (sg * (1.0 + l * (1.0 - sg)))
            dnv = dl * lg
            dca = rs * (dnv - jnp.mean(dnv, axis=-1, keepdims=True)
                        - nv * jnp.mean(dnv * nv, axis=-1, keepdims=True))
            return dca, dl, nv

        dlg_acc = jnp.zeros((1, c), F32)
        dlb_acc = jnp.zeros((1, c), F32)
        for q in range(ts // rc):
            r0 = q * rc
            dca, dl, nv = ln_bwd(ca_ref[r0:r0 + rc, :], d_ref[r0:r0 + rc, 0:c])
            extdca[r0:r0 + rc, :] = dca
            dlg_acc = dlg_acc + jnp.sum(dl * nv, axis=0, keepdims=True)
            dlb_acc = dlb_acc + jnp.sum(dl, axis=0, keepdims=True)
            extdcb[r0:r0 + rc, :] = d_ref[r0:r0 + rc, c:2 * c] * z_ref[r0:r0 + rc, 2 * c:3 * c]
        dca_n, _, _ = ln_bwd(can_ref[...], dn_ref[:, 0:c])
        extdca[ts:ts + hb, :] = dca_n * keep_n
        extdcb[ts:ts + hb, :] = dn_ref[:, c:2 * c] * zn_ref[:, 2 * c:3 * c] * keep_n
        dlg_ref[...] += dlg_acc
        dlb_ref[...] += dlb_acc
        _shift_copies(extdca, shd, ts + hb - 8)

        for q in range(ts // rc):
            r0 = q * rc
            zr = z_ref[r0:r0 + rc, :]
            dga = None
            for k in range(ta):
                term = wa_ref[k:k + 1, :] * _shifted(extdca, shd, r0, rc, ta - 1 - k)
                dga = term if dga is None else dga + term
            sg = _sigmoid(zr[:, c:2 * c])
            dz_ref[r0:r0 + rc, 0:c] = (dga * sg).astype(BF16)
            dz_ref[r0:r0 + rc, c:2 * c] = (dga * zr[:, 0:c] * sg * (1.0 - sg)).astype(BF16)
            cbc = _conv_taps(extb, wb_ref, 3, hb - 2, r0, rc)
            dz_ref[r0:r0 + rc, 2 * c:3 * c] = (d_ref[r0:r0 + rc, c:2 * c] * cbc).astype(BF16)
            dcb = _conv_taps(extdcb, wb_ref, 3, 2, r0, rc, reverse=True)
            dz_ref[r0:r0 + rc, 3 * c:4 * c] = (dcb * zr[:, 4 * c:5 * c]).astype(BF16)
            dz_ref[r0:r0 + rc, 4 * c:5 * c] = (dcb * zr[:, 3 * c:4 * c]).astype(BF16)

        for k in range(ta):
            part = None
            for q in range(ts // rc):
                r0 = q * rc
                p = exta[hb + r0:hb + r0 + rc, :] * _shifted(extdca, shd, r0, rc, ta - 1 - k)
                for r in range(0, rc, 8):
                    part = p[r:r + 8, :] if part is None else part + p[r:r + 8, :]
            dwa_ref[k:k + 1, :] += jnp.sum(part, axis=0, keepdims=True)
        dcb_t = extdcb[0:ts, :]
        for k in range(3):
            dwb_ref[k:k + 1, :] += jnp.sum(dcb_t * extb[pl.ds(hb - 2 + k, ts), :], axis=0, keepdims=True)

    def tile(w):
        return pl.BlockSpec((ts, w), lambda i: (i, 0))

    def prev(w):
        return pl.BlockSpec((hb, w), lambda i: (_prev_blk(i, ts, hb), 0))

    def nxt(w):
        return pl.BlockSpec((hb, w), lambda i: (_next_blk(i, ts, hb, s), 0))

    def const(r, w):
        return pl.BlockSpec((r, w), lambda i: (0, 0))

    return pl.pallas_call(
        body, name=name, grid=(s // ts,),
        in_specs=[tile(5 * c), prev(5 * c), nxt(5 * c), tile(c), nxt(c), tile(2 * c), nxt(2 * c),
                  const(32, c), const(1, c), const(1, c), const(8, c)],
        out_specs=[tile(5 * c), const(32, c), const(8, c), const(1, c), const(1, c)],
        out_shape=[jax.ShapeDtypeStruct((s, 5 * c), BF16), jax.ShapeDtypeStruct((32, c), F32),
                   jax.ShapeDtypeStruct((8, c), F32), jax.ShapeDtypeStruct((1, c), F32),
                   jax.ShapeDtypeStruct((1, c), F32)],
        scratch_shapes=[pltpu.VMEM((hb + ts, c), F32), pltpu.VMEM((hb + ts, c), F32),
                        pltpu.VMEM((ts + hb, c), F32), pltpu.VMEM((ts + hb, c), F32),
                        pltpu.VMEM((8, ts + hb - 8, c), F32)],
        compiler_params=_params(("arbitrary",)),
    )(z, z, z, ca, ca, dab, dab, conv_a, ln_g, ln_b, conv_b)


def _ffn_up(x, g, w8, *, name, tm=2048):
    s, d = x.shape
    nb, c, _ = w8.shape
    tm = min(tm, s)

    def body(x_ref, g_ref, w_ref, h_ref, u_ref, hs_ref):
        @pl.when(pl.program_id(1) == 0)
        def _():
            hv = _rms_rows(x_ref[...], g_ref[...]).astype(BF16)
            hs_ref[...] = hv
            h_ref[...] = hv

        u_ref[...] = lax.dot_general(hs_ref[...], w_ref[...], _NT, preferred_element_type=F32).astype(BF16)

    return pl.pallas_call(
        body, name=name, grid=(s // tm, nb),
        in_specs=[pl.BlockSpec((tm, d), lambda i, k: (i, 0)),
                  pl.BlockSpec((1, d), lambda i, k: (0, 0)),
                  pl.BlockSpec((None, c, d), lambda i, k: (k, 0, 0))],
        out_specs=[pl.BlockSpec((tm, d), lambda i, k: (i, 0)),
                   pl.BlockSpec((None, tm, c), lambda i, k: (k, i, 0))],
        out_shape=[jax.ShapeDtypeStruct((s, d), BF16), jax.ShapeDtypeStruct((nb, s, c), BF16)],
        scratch_shapes=[pltpu.VMEM((tm, d), BF16)],
        compiler_params=_params(("parallel", "arbitrary")),
    )(x, g, w8)


def _ffn_mid(up8, wc8, *, name, ts=512, rc=32):
    nb, s, c = up8.shape
    hb = HALO_S
    ts = min(ts, s)

    def body(g_ref, gp_ref, v_ref, vp_ref, wg_ref, wv_ref, act_ref, ug_ref, uv_ref, extg, extv):
        keep = jnp.where(pl.program_id(0) > 0, 1.0, 0.0)
        extg[0:hb, :] = gp_ref[...].astype(F32) * keep
        extv[0:hb, :] = vp_ref[...].astype(F32) * keep
        extg[hb:hb + ts, :] = g_ref[...].astype(F32)
        extv[hb:hb + ts, :] = v_ref[...].astype(F32)
        for q in range(ts // rc):
            r0 = q * rc
            gg = _conv_taps(extg, wg_ref, 3, hb - 2, r0, rc)
            vv = _conv_taps(extv, wv_ref, 3, hb - 2, r0, rc)
            ug_ref[r0:r0 + rc, :] = gg.astype(BF16)
            uv_ref[r0:r0 + rc, :] = vv.astype(BF16)
            act_ref[r0:r0 + rc, :] = (gg * _sigmoid(gg) * vv).astype(BF16)

    def blk(rows, off, halo):
        if halo:
            return pl.BlockSpec((None, rows, c), lambda i, j: (j + off, _prev_blk(i, ts, hb), 0))
        return pl.BlockSpec((None, rows, c), lambda i, j: (j + off, i, 0))

    def taps(off):
        return pl.BlockSpec((None, 8, c), lambda i, j: (j + off, 0, 0))

    tile = pl.BlockSpec((None, ts, c), lambda i, j: (j, i, 0))
    out = jax.ShapeDtypeStruct((N_PAIR, s, c), BF16)
    return pl.pallas_call(
        body, name=name, grid=(s // ts, N_PAIR),
        in_specs=[blk(ts, 0, False), blk(hb, 0, True), blk(ts, N_PAIR, False), blk(hb, N_PAIR, True),
                  taps(0), taps(N_PAIR)],
        out_specs=[tile, tile, tile], out_shape=[out, out, out],
        scratch_shapes=[pltpu.VMEM((hb + ts, c), F32), pltpu.VMEM((hb + ts, c), F32)],
        compiler_params=_params(("parallel", "parallel")),
    )(up8, up8, up8, up8, wc8, wc8)


def _ffn_down(act4, wd4, x, *, name, tm=512):
    npair, s, c = act4.shape
    d = x.shape[1]
    tm = min(tm, s)

    def body(a_ref, w_ref, x_ref, o_ref):
        acc = x_ref[...]
        for j in range(npair):
            acc = acc + jnp.dot(a_ref[j], w_ref[j], preferred_element_type=F32)
        o_ref[...] = acc

    return pl.pallas_call(
        body, name=name, grid=(s // tm,),
        in_specs=[pl.BlockSpec((npair, tm, c), lambda i: (0, i, 0)),
                  pl.BlockSpec((npair, c, d), lambda i: (0, 0, 0)),
                  pl.BlockSpec((tm, d), lambda i: (i, 0))],
        out_specs=pl.BlockSpec((tm, d), lambda i: (i, 0)),
        out_shape=jax.ShapeDtypeStruct((s, d), F32),
        compiler_params=_params(("parallel",)),
    )(act4, wd4, x)


def _ffn_dact(db, wd4, *, name, tm=1024):
    s, d = db.shape
    npair, c, _ = wd4.shape
    tm = min(tm, s)

    def body(d_ref, w_ref, o_ref):
        o_ref[...] = lax.dot_general(d_ref[...], w_ref[...], (((1,), (1,)), ((), ())),
                                     preferred_element_type=F32).astype(BF16)

    return pl.pallas_call(
        body, name=name, grid=(s // tm, npair),
        in_specs=[pl.BlockSpec((tm, d), lambda i, j: (i, 0)),
                  pl.BlockSpec((None, c, d), lambda i, j: (j, 0, 0))],
        out_specs=pl.BlockSpec((None, tm, c), lambda i, j: (j, i, 0)),
        out_shape=jax.ShapeDtypeStruct((npair, s, c), BF16),
        compiler_params=_params(("parallel", "parallel")),
    )(db, wd4)


def _ffn_dwdown(act4, db, *, name):
    npair, s, c = act4.shape
    d = db.shape[1]

    def body(a_ref, d_ref, o_ref):
        o_ref[...] = lax.dot_general(a_ref[...], d_ref[...], (((0,), (0,)), ((), ())),
                                     preferred_element_type=F32).astype(BF16)

    return pl.pallas_call(
        body, name=name, grid=(npair,),
        in_specs=[pl.BlockSpec((None, s, c), lambda j: (j, 0, 0)),
                  pl.BlockSpec((s, d), lambda j: (0, 0))],
        out_specs=pl.BlockSpec((None, c, d), lambda j: (j, 0, 0)),
        out_shape=jax.ShapeDtypeStruct((npair, c, d), BF16),
        compiler_params=_params(("parallel",)),
    )(act4, db)


def _ffn_midbwd(up8, ug4, uv4, dact4, wc8, *, name, ts=512, rc=32):
    nb, s, c = up8.shape
    hb = HALO_S
    ts = min(ts, s)
    n_i = s // ts

    def body(pg_ref, pv_ref, ug_ref, ugn_ref, uv_ref, uvn_ref, d_ref, dn_ref, wg_ref, wv_ref,
             dg_ref, dv_ref, dwg_ref, dwv_ref, extdg, extdv):
        i = pl.program_id(1)
        keep_n = jnp.where(i < n_i - 1, 1.0, 0.0)

        @pl.when(i == 0)
        def _():
            dwg_ref[...] = jnp.zeros_like(dwg_ref)
            dwv_ref[...] = jnp.zeros_like(dwv_ref)

        def du_rows(r0, rows, gg, vv, dav):
            sg = _sigmoid(gg)
            extdg[r0:r0 + rows, :] = dav * vv * (sg * (1.0 + gg * (1.0 - sg)))
            extdv[r0:r0 + rows, :] = dav * (gg * sg)

        for q in range(ts // rc):
            rows = slice(q * rc, q * rc + rc)
            du_rows(q * rc, rc, ug_ref[rows, :].astype(F32), uv_ref[rows, :].astype(F32),
                    d_ref[rows, :].astype(F32))
        du_rows(ts, 8, ugn_ref[0:8, :].astype(F32), uvn_ref[0:8, :].astype(F32),
                dn_ref[0:8, :].astype(F32) * keep_n)

        def fold(p):
            acc = p[0:8, :]
            for r in range(8, rc, 8):
                acc = acc + p[r:r + 8, :]
            return acc

        for extd, p_ref, w_ref, out_ref, dw_ref in ((extdg, pg_ref, wg_ref, dg_ref, dwg_ref),
                                                    (extdv, pv_ref, wv_ref, dv_ref, dwv_ref)):
            part = [None, None, None]
            for q in range(ts // rc):
                r0 = q * rc
                pre = p_ref[r0:r0 + rc, :].astype(F32)
                dup = None
                for k in range(3):
                    sh = extd[pl.ds(r0 + 2 - k, rc), :]
                    term = w_ref[k:k + 1, :] * sh
                    dup = term if dup is None else dup + term
                    prod = fold(pre * sh)
                    part[k] = prod if part[k] is None else part[k] + prod
                out_ref[r0:r0 + rc, :] = dup.astype(BF16)
            for k in range(3):
                dw_ref[k:k + 1, :] += jnp.sum(part[k], axis=0, keepdims=True)

    def blk(off, nxt):
        if nxt:
            return pl.BlockSpec((None, hb, c), lambda j, i: (j + off, _next_blk(i, ts, hb, s), 0))
        return pl.BlockSpec((None, ts, c), lambda j, i: (j + off, i, 0))

    def taps(off):
        return pl.BlockSpec((None, 8, c), lambda j, i: (j + off, 0, 0))

    tile = blk(0, False)
    acc = pl.BlockSpec((None, 8, c), lambda j, i: (j, 0, 0))
    return pl.pallas_call(
        body, name=name, grid=(N_PAIR, n_i),
        in_specs=[tile, blk(N_PAIR, False), tile, blk(0, True), tile, blk(0, True), tile, blk(0, True),
                  taps(0), taps(N_PAIR)],
        out_specs=[tile, tile, acc, acc],
        out_shape=[jax.ShapeDtypeStruct((N_PAIR, s, c), BF16), jax.ShapeDtypeStruct((N_PAIR, s, c), BF16),
                   jax.ShapeDtypeStruct((N_PAIR, 8, c), F32), jax.ShapeDtypeStruct((N_PAIR, 8, c), F32)],
        scratch_shapes=[pltpu.VMEM((ts + 8, c), F32), pltpu.VMEM((ts + 8, c), F32)],
        compiler_params=_params(("parallel", "arbitrary")),
    )(up8, up8, ug4, ug4, uv4, uv4, dact4, dact4, wc8, wc8)


def _ffn_dh(dupg, dupv, w8, x, g, dres, dep, *, name, tm=512):
    npair, s, c = dupg.shape
    d = x.shape[1]
    tm = min(tm, s)

    def body(dg_ref, dv_ref, w_ref, x_ref, g_ref, dres_ref, dep_ref, dx_ref, dxb_ref, dgain_ref):
        @pl.when(pl.program_id(0) == 0)
        def _():
            dgain_ref[...] = jnp.zeros_like(dgain_ref)

        dh = None
        for j in range(npair):
            for src, k in ((dg_ref, j), (dv_ref, j + npair)):
                part = jnp.dot(src[j], w_ref[k], preferred_element_type=F32)
                dh = part if dh is None else dh + part
        dx, dgr = _rms_bwd_rows(x_ref[...], g_ref[...], dh)
        dx = dres_ref[...] + dx
        dx_ref[...] = dx
        dxb_ref[...] = dx.astype(BF16)
        dgain_ref[...] += jnp.sum(dgr, axis=0, keepdims=True)

    row = pl.BlockSpec((tm, d), lambda i: (i, 0))
    vec = pl.BlockSpec((1, d), lambda i: (0, 0))
    dup = pl.BlockSpec((npair, tm, c), lambda i: (0, i, 0))
    return pl.pallas_call(
        body, name=name, grid=(s // tm,),
        in_specs=[dup, dup, pl.BlockSpec((2 * npair, c, d), lambda i: (0, 0, 0)), row, vec, row,
                  pl.BlockSpec(memory_space=pl.ANY)],
        out_specs=[row, row, vec],
        out_shape=[jax.ShapeDtypeStruct((s, d), F32), jax.ShapeDtypeStruct((s, d), BF16),
                   jax.ShapeDtypeStruct((1, d), F32)],
        compiler_params=_params(("arbitrary",)),
    )(dupg, dupv, w8, x, g, dres, dep)


def _ffn_dwup(h, dupg, dupv, *, name, tm=512):
    npair, s, c = dupg.shape
    d = h.shape[1]

    def body(h_ref, dg_ref, dv_ref, o_ref):
        k = pl.program_id(1)

        @pl.when(k < npair)
        def _():
            o_ref[...] = lax.dot_general(dg_ref[...], h_ref[...], _TN, preferred_element_type=F32).astype(BF16)

        @pl.when(k >= npair)
        def _():
            o_ref[...] = lax.dot_general(dv_ref[...], h_ref[...], _TN, preferred_element_type=F32).astype(BF16)

    return pl.pallas_call(
        body, name=name, grid=(d // tm, 2 * npair),
        in_specs=[pl.BlockSpec((s, tm), lambda m, k: (0, m)),
                  pl.BlockSpec((None, s, c), lambda m, k: (jnp.minimum(k, npair - 1), 0, 0)),
                  pl.BlockSpec((None, s, c), lambda m, k: (jnp.maximum(k - npair, 0), 0, 0))],
        out_specs=pl.BlockSpec((None, c, tm), lambda m, k: (k, 0, m)),
        out_shape=jax.ShapeDtypeStruct((2 * npair, c, d), BF16),
        compiler_params=_params(("parallel", "arbitrary")),
    )(h, dupg, dupv)


def _pool_counts(i, ts, rows, window):
    t = lax.broadcasted_iota(jnp.int32, (rows, 1), 0) + i * ts + 1
    return jnp.minimum(t, window).astype(F32)


def _pool_fwd(x, g, *, name, ts=256):
    s, d = x.shape
    hb = HALO_A
    pg = POOL_GROUP
    ts = min(ts, s)

    def body(x_ref, xp_ref, g_ref, p_ref, ext, sa, sb):
        i = pl.program_id(0)
        keep = jnp.where(i > 0, 1.0, 0.0)
        gv = g_ref[...]
        ext[0:hb, :] = _rms_rows(xp_ref[...], gv) * keep
        ext[hb:hb + ts, :] = _rms_rows(x_ref[...], gv)
        rows = hb + ts - 8
        sa[0:8, :] = jnp.zeros((8, d), F32)
        sb[0:8, :] = jnp.zeros((8, d), F32)
        for gi, w in enumerate(POOL_WINDOWS):
            cols = slice(gi * pg, (gi + 1) * pg)
            cur, nxt, k = ext, sa, 1
            while k < w:
                nxt[8:8 + rows, cols] = cur[8:8 + rows, cols] + cur[pl.ds(8 - k, rows), cols]
                cur, nxt, k = nxt, (sb if nxt is sa else sa), 2 * k
            h = ext[hb:hb + ts, cols]
            p_ref[:, cols] = (cur[hb:hb + ts, cols] / _pool_counts(i, ts, ts, w) - h).astype(BF16)

    return pl.pallas_call(
        body, name=name, grid=(s // ts,),
        in_specs=[pl.BlockSpec((ts, d), lambda i: (i, 0)),
                  pl.BlockSpec((hb, d), lambda i: (_prev_blk(i, ts, hb), 0)),
                  pl.BlockSpec((1, d), lambda i: (0, 0))],
        out_specs=pl.BlockSpec((ts, d), lambda i: (i, 0)),
        out_shape=jax.ShapeDtypeStruct((s, d), BF16),
        scratch_shapes=[pltpu.VMEM((hb + ts, d), F32)] * 3,
        compiler_params=_params(("parallel",)),
    )(x, x, g)


def _pool_mm_fwd(p, w_pool, scale, x, *, name, ts=1024):
    s, d = x.shape
    pg = POOL_GROUP
    ts = min(ts, s)

    def body(p_ref, w_ref, s_ref, x_ref, o_ref, yu_ref):
        yu = jnp.dot(p_ref[...], w_ref[...], preferred_element_type=F32)
        yu_ref[...] = yu.astype(BF16)
        o_ref[...] = x_ref[...] + yu * s_ref[...]

    blk = pl.BlockSpec((ts, pg), lambda i, gi: (i, gi))
    return pl.pallas_call(
        body, name=name, grid=(s // ts, d // pg),
        in_specs=[blk, pl.BlockSpec((None, pg, pg), lambda i, gi: (gi, 0, 0)),
                  pl.BlockSpec((1, pg), lambda i, gi: (0, gi)), blk],
        out_specs=[blk, blk],
        out_shape=[jax.ShapeDtypeStruct((s, d), F32), jax.ShapeDtypeStruct((s, d), BF16)],
        compiler_params=_params(("parallel", "parallel")),
    )(p, w_pool, scale, x)


def _pool_mm_bwd(dres, w_pool, scale, yu, *, name, ts=1024):
    s, d = dres.shape
    pg = POOL_GROUP
    ts = min(ts, s)

    def body(d_ref, w_ref, s_ref, yu_ref, dyc_ref, dp_ref, ds_ref):
        @pl.when(pl.program_id(1) == 0)
        def _():
            ds_ref[...] = jnp.zeros_like(ds_ref)

        dv = d_ref[...]
        dyc = (dv * s_ref[...]).astype(BF16)
        dyc_ref[...] = dyc
        dp_ref[...] = lax.dot_general(dyc, w_ref[...], (((1,), (1,)), ((), ())), preferred_element_type=F32)
        ds_ref[...] += jnp.sum(dv * yu_ref[...].astype(F32), axis=0, keepdims=True)

    blk = pl.BlockSpec((ts, pg), lambda gi, i: (i, gi))
    vec = pl.BlockSpec((1, pg), lambda gi, i: (0, gi))
    return pl.pallas_call(
        body, name=name, grid=(d // pg, s // ts),
        in_specs=[blk, pl.BlockSpec((None, pg, pg), lambda gi, i: (gi, 0, 0)), vec, blk],
        out_specs=[blk, blk, vec],
        out_shape=[jax.ShapeDtypeStruct((s, d), BF16), jax.ShapeDtypeStruct((s, d), F32),
                   jax.ShapeDtypeStruct((1, d), F32)],
        compiler_params=_params(("parallel", "arbitrary")),
    )(dres, w_pool, scale, yu)


def _pool_dw(p, dyc, *, name):
    s, d = p.shape
    pg = POOL_GROUP

    def body(p_ref, d_ref, o_ref):
        o_ref[...] = lax.dot_general(p_ref[...], d_ref[...], (((0,), (0,)), ((), ())),
                                     preferred_element_type=F32).astype(BF16)

    blk = pl.BlockSpec((s, pg), lambda gi: (0, gi))
    return pl.pallas_call(
        body, name=name, grid=(d // pg,),
        in_specs=[blk, blk], out_specs=pl.BlockSpec((None, pg, pg), lambda gi: (gi, 0, 0)),
        out_shape=jax.ShapeDtypeStruct((d // pg, pg, pg), BF16),
        compiler_params=_params(("parallel",)),
    )(p, dyc)


def _pool_bwd(dp, x, g, dres, *, name, ts=256):
    s, d = x.shape
    hb = HALO_A
    pg = POOL_GROUP
    ts = min(ts, s)
    n_i = s // ts

    def body(dp_ref, dpn_ref, x_ref, g_ref, dres_ref, dx_ref, dxb_ref, dg_ref, ext, dh, sa, sb):
        i = pl.program_id(0)
        keep_n = jnp.where(i < n_i - 1, 1.0, 0.0)

        @pl.when(i == 0)
        def _():
            dg_ref[...] = jnp.zeros_like(dg_ref)

        for gi, w in enumerate(POOL_WINDOWS):
            cols = slice(gi * pg, (gi + 1) * pg)
            ext[0:ts, cols] = dp_ref[:, cols] / _pool_counts(i, ts, ts, w)
            ext[ts:ts + hb, cols] = dpn_ref[:, cols] / _pool_counts(i + 1, ts, hb, w) * keep_n
        rows = ts + hb - 8
        sa[rows:rows + 8, :] = jnp.zeros((8, d), F32)
        sb[rows:rows + 8, :] = jnp.zeros((8, d), F32)
        for gi, w in enumerate(POOL_WINDOWS):
            cols = slice(gi * pg, (gi + 1) * pg)
            cur, nxt, k = ext, sa, 1
            while k < w:
                nxt[0:rows, cols] = cur[0:rows, cols] + cur[pl.ds(k, rows), cols]
                cur, nxt, k = nxt, (sb if nxt is sa else sa), 2 * k
            dh[:, cols] = cur[0:ts, cols] - dp_ref[:, cols]
        dx, dgr = _rms_bwd_rows(x_ref[...], g_ref[...], dh[...])
        dx = dres_ref[...] + dx
        dx_ref[...] = dx
        dxb_ref[...] = dx.astype(BF16)
        dg_ref[...] += jnp.sum(dgr, axis=0, keepdims=True)

    row = pl.BlockSpec((ts, d), lambda i: (i, 0))
    vec = pl.BlockSpec((1, d), lambda i: (0, 0))
    return pl.pallas_call(
        body, name=name, grid=(n_i,),
        in_specs=[row, pl.BlockSpec((hb, d), lambda i: (_next_blk(i, ts, hb, s), 0)), row, vec, row],
        out_specs=[row, row, vec],
        out_shape=[jax.ShapeDtypeStruct((s, d), F32), jax.ShapeDtypeStruct((s, d), BF16),
                   jax.ShapeDtypeStruct((1, d), F32)],
        scratch_shapes=[pltpu.VMEM((ts + hb, d), F32), pltpu.VMEM((ts, d), F32),
                        pltpu.VMEM((ts + hb, d), F32), pltpu.VMEM((ts + hb, d), F32)],
        compiler_params=_params(("arbitrary",)),
    )(dp, dp, x, g, dres)


def _pad_rows(w, rows):
    pad = [(0, 0)] * (w.ndim - 2) + [(0, rows - w.shape[-2]), (0, 0)]
    return jnp.pad(w, pad)


def _ffn_layer_fwd(x, nf, w8, wc8, wd4, tag):
    h, up8 = _ffn_up(x, nf, w8, name=f"ffn{tag}_up")
    act4, ug4, uv4 = _ffn_mid(up8, wc8, name=f"ffn{tag}_mid")
    x_out = _ffn_down(act4, wd4, x, name=f"ffn{tag}_down")
    return x_out, (h, up8, ug4, uv4, act4)


def _ffn_layer_bwd(d, db, x, nf, w8, wc8, wd4, saved, tag, grads_ready):
    h, up8, ug4, uv4, act4 = saved
    dact4 = _ffn_dact(db, wd4, name=f"ffn{tag}_dact")
    dwd4 = _ffn_dwdown(act4, db, name=f"ffn{tag}_dwdown")
    dupg, dupv, dwg, dwv = _ffn_midbwd(up8, ug4, uv4, dact4, wc8, name=f"ffn{tag}_midbwd")
    dw8 = _ffn_dwup(h, dupg, dupv, name=f"ffn{tag}_dwup")
    sent = grads_ready(f"ffn{tag}", {f"w_up{tag}": dw8, f"w_down{tag}": dwd4})
    dx, dxb, dnf = _ffn_dh(dupg, dupv, w8, x, nf, d, sent, name=f"ffn{tag}_dh")
    dwc8 = jnp.concatenate([dwg, dwv], axis=0)[:, :3]
    return dx, dxb, dnf, dwc8


def _local_step(x, target, weights, grads_ready):
    w0 = weights("mix0", x)
    wa = _pad_rows(w0["conv_a"], 32)
    wb = _pad_rows(w0["conv_b"], 8)
    wc = [_pad_rows(w0["conv_ffn"][l], 8) for l in range(2)]
    h0, z = _rms_mm(x, w0["norm_mix_even"], w0["w_in_t"], name="mix0_in", out_dtype=F32, tn=1280)
    ab, ca = _mix0_fwd(z, wa, w0["ln_a_g"], w0["ln_a_b"], wb, name="mix0_mid")
    x1 = _mm(ab, w0["w_out"], add=x, name="mix0_out", tm=1024, tn=1024)
    w1 = weights("ffn0", x1)
    x2, ffn0 = _ffn_layer_fwd(x1, w0["norm_ffn"][0:1], w1["w_up"], wc[0], w1["w_down"], 0)
    w2 = weights("ffn1", x2)
    p = _pool_fwd(x2, w0["norm_mix_odd"], name="pool_mid")
    x3, yu = _pool_mm_fwd(p, w2["w_pool"], w0["pool_scale"], x2, name="pool_mm")
    x4, ffn1 = _ffn_layer_fwd(x3, w0["norm_ffn"][1:2], w2["w_up"], wc[1], w2["w_down"], 1)
    loss, d4, d4b, g_norm_final = _final_loss(x4, w0["norm_final"], target, name="final_loss")

    d3, d3b, g_nf1, g_wc1 = _ffn_layer_bwd(
        d4, d4b, x3, w0["norm_ffn"][1:2], w2["w_up"], wc[1], w2["w_down"], ffn1, 1, grads_ready)
    dyc, dp, g_scale = _pool_mm_bwd(d3, w2["w_pool"], w0["pool_scale"], yu, name="pool_mm_bwd")
    g_pool = _pool_dw(p, dyc, name="pool_dw")
    d2, d2b, g_nmo = _pool_bwd(dp, x2, w0["norm_mix_odd"], d3, name="pool_midbwd")
    d1, d1b, g_nf0, g_wc0 = _ffn_layer_bwd(
        d2, d2b, x1, w0["norm_ffn"][0:1], w1["w_up"], wc[0], w1["w_down"], ffn0, 0, grads_ready)
    dab = _mm(d1b, w0["w_out"], tb=True, name="mix0_dab", tm=1024, tn=1024)
    g_out = _mm(ab, d1b, ta=True, out_dtype=BF16, name="mix0_dwout", tm=1024, tn=512)
    dz, g_wa, g_wb, g_lg, g_lb = _mix0_bwd(z, ca, dab, wa, w0["ln_a_g"], w0["ln_a_b"], wb, name="mix0_midbwd")
    g_in = _mm(dz, h0, ta=True, out_dtype=BF16, name="mix0_dwin", tm=512, tn=1024)
    sent_mix = grads_ready("mix0", {"w_in": g_in, "w_out": g_out, "w_pool": g_pool})
    dx, g_nme = _mm_rms_bwd(dz, w0["w_in_t"], x, w0["norm_mix_even"], d1, sent_mix, name="mix0_dh")

    small = {
        "norm_mix_even": g_nme, "conv_a": g_wa[:A_TAPS], "ln_a_g": g_lg, "ln_a_b": g_lb, "conv_b": g_wb[:3],
        "norm_mix_odd": g_nmo, "pool_scale": g_scale, "norm_ffn": jnp.concatenate([g_nf0, g_nf1], axis=0),
        "conv_ffn": [g_wc0, g_wc1], "norm_final": g_norm_final,
    }
    return loss[0, 0], dx, small


def _my_pos():
    return lax.axis_index("x"), lax.axis_index("y"), lax.axis_index("c")


def _flip(pos, r):
    x, y, c = pos
    return (1 - x if r & 4 else x, 1 - y if r & 2 else y, 1 - c if r & 1 else c)


def _dev_index(pos):
    return 4 * pos[0] + 2 * pos[1] + pos[2]


_HBM = pl.BlockSpec(memory_space=pltpu.HBM)
_SEM = pl.BlockSpec(memory_space=pltpu.SEMAPHORE)
_EFFECT = pltpu.SideEffectType.DATAFLOW_SIDE_EFFECTING


def _exchange_copy(ins, lands, send_sems, recv_sems, scatter, pos, a, r, receiving):
    me = _dev_index(pos)
    peer = _flip(pos, r)
    dest = _dev_index(pos) if receiving else _dev_index(peer)
    src = ins[a].at[dest] if scatter[a] else ins[a]
    slot = _dev_index(peer) if receiving else me
    return pltpu.make_async_remote_copy(
        src_ref=src, dst_ref=lands[a].at[slot], send_sem=send_sems.at[a * (N_DEV - 1) + r - 1],
        recv_sem=recv_sems.at[a * (N_DEV - 1) + r - 1],
        device_id=peer, device_id_type=pl.DeviceIdType.MESH)


ALL_PEERS = tuple(range(1, N_DEV))
CHIP_PEERS = (1, 2, 4, 6)
FORWARDED = (2, 4, 6)


def _exchange_start(arrays, scatter, after, *, name, peers=ALL_PEERS):
    n = len(arrays)
    me = _dev_index(_my_pos())
    lands = []
    for arr, sc in zip(arrays, scatter):
        own = lax.dynamic_index_in_dim(arr, me, 0, keepdims=True) if sc else arr[None]
        shape = arr.shape if sc else (N_DEV,) + arr.shape
        lands.append(lax.dynamic_update_slice(lax.empty(shape, arr.dtype), own, (me,) + (0,) * (len(shape) - 1)))

    def body(*refs):
        ins, lnd = refs[:n], refs[n:2 * n]
        send_sems, recv_sems = refs[2 * n + 1], refs[2 * n + 2]
        token = refs[-1]
        pos = _my_pos()
        for a in range(n):
            for r in peers:
                _exchange_copy(ins, lnd, send_sems, recv_sems, scatter, pos, a, r, receiving=False).start()
        token[...] = jnp.zeros_like(token)

    bufs = [pltpu.with_memory_space_constraint(t, pltpu.HBM) for t in list(arrays) + lands]
    sems = pltpu.SemaphoreType.DMA((n * (N_DEV - 1),))
    res = pl.pallas_call(
        body, name=name,
        out_shape=(sems, sems, *[pltpu.HBM(t.shape, t.dtype) for t in bufs], jax.ShapeDtypeStruct((8, 128), F32)),
        in_specs=[_HBM] * (2 * n) + [pl.BlockSpec(memory_space=pl.ANY)],
        out_specs=(_SEM, _SEM, *[_HBM] * (2 * n), pl.BlockSpec(memory_space=pltpu.VMEM)),
        input_output_aliases={i: 2 + i for i in range(2 * n)},
        compiler_params=pltpu.CompilerParams(has_side_effects=_EFFECT),
    )(*bufs, after)
    return res[0], res[1], list(res[2:2 + n]), list(res[2 + n:2 + 2 * n]), res[-1]


def _exchange_wait(started, scatter, after, *, name, peers=ALL_PEERS):
    send_sems, recv_sems, arrays, lands, _ = started
    n = len(arrays)

    def body(*refs):
        ins, lnd = refs[:n], refs[n:2 * n]
        send_sems, recv_sems = refs[2 * n], refs[2 * n + 1]
        pos = _my_pos()
        for a in range(n):
            for r in peers:
                _exchange_copy(ins, lnd, send_sems, recv_sems, scatter, pos, a, r, receiving=False).wait_send()
                _exchange_copy(ins, lnd, send_sems, recv_sems, scatter, pos, a, r, receiving=True).wait_recv()

    bufs = list(arrays) + list(lands)
    res = pl.pallas_call(
        body, name=name,
        out_shape=tuple(pltpu.HBM(t.shape, t.dtype) for t in bufs),
        in_specs=[_HBM] * (2 * n) + [_SEM, _SEM, pl.BlockSpec(memory_space=pl.ANY)],
        out_specs=tuple([_HBM] * (2 * n)),
        input_output_aliases={i: i for i in range(2 * n)},
        compiler_params=pltpu.CompilerParams(has_side_effects=_EFFECT),
    )(*bufs, send_sems, recv_sems, after)
    return list(res[n:])


def _forward_copy(lands, send_sems, recv_sems, pos, a, q, receiving):
    slot = _dev_index(_flip(pos, q ^ 1 if receiving else q))
    idx = a * len(FORWARDED) + FORWARDED.index(q)
    return pltpu.make_async_remote_copy(
        src_ref=lands[a].at[slot], dst_ref=lands[a].at[slot], send_sem=send_sems.at[idx], recv_sem=recv_sems.at[idx],
        device_id=_flip(pos, 1), device_id_type=pl.DeviceIdType.MESH)


def _forward_start(lands, after, *, name):
    n = len(lands)

    def body(*refs):
        lnd = refs[:n]
        send_sems, recv_sems = refs[n + 1], refs[n + 2]
        token = refs[-1]
        pos = _my_pos()
        for a in range(n):
            for q in FORWARDED:
                _forward_copy(lnd, send_sems, recv_sems, pos, a, q, receiving=False).start()
        token[...] = jnp.zeros_like(token)

    sems = pltpu.SemaphoreType.DMA((n * len(FORWARDED),))
    res = pl.pallas_call(
        body, name=name,
        out_shape=(sems, sems, *[pltpu.HBM(t.shape, t.dtype) for t in lands], jax.ShapeDtypeStruct((8, 128), F32)),
        in_specs=[_HBM] * n + [pl.BlockSpec(memory_space=pl.ANY)],
        out_specs=(_SEM, _SEM, *[_HBM] * n, pl.BlockSpec(memory_space=pltpu.VMEM)),
        input_output_aliases={i: 2 + i for i in range(n)},
        compiler_params=pltpu.CompilerParams(has_side_effects=_EFFECT),
    )(*lands, after)
    return res[0], res[1], list(res[2:2 + n]), res[-1]


def _forward_wait(forwarded, after, *, name):
    send_sems, recv_sems, lands, _ = forwarded
    n = len(lands)

    def body(*refs):
        lnd = refs[:n]
        send_sems, recv_sems = refs[n], refs[n + 1]
        pos = _my_pos()
        for a in range(n):
            for q in FORWARDED:
                _forward_copy(lnd, send_sems, recv_sems, pos, a, q, receiving=False).wait_send()
                _forward_copy(lnd, send_sems, recv_sems, pos, a, q, receiving=True).wait_recv()

    res = pl.pallas_call(
        body, name=name,
        out_shape=tuple(pltpu.HBM(t.shape, t.dtype) for t in lands),
        in_specs=[_HBM] * n + [_SEM, _SEM, pl.BlockSpec(memory_space=pl.ANY)],
        out_specs=tuple([_HBM] * n),
        input_output_aliases={i: i for i in range(n)},
        compiler_params=pltpu.CompilerParams(has_side_effects=_EFFECT),
    )(*lands, send_sems, recv_sems, after)
    return list(res)


def _adamw(parts, w, m, v, *, name, tr):
    nl, r, c = w.shape
    assert len(parts) == nl and r % tr == 0
    n_i = r // tr

    def body(*refs):
        p_refs = refs[:nl]
        w_ref, m_ref, v_ref, g_ref, d_ref, mo_ref, vo_ref = refs[nl:]

        def update(p_ref):
            g = p_ref[0].astype(F32)
            for k in range(1, N_DEV):
                g = g + p_ref[k].astype(F32)
            mn = ADAM_B1 * m_ref[...] + (1.0 - ADAM_B1) * g
            vn = ADAM_B2 * v_ref[...] + (1.0 - ADAM_B2) * (g * g)
            m_hat = mn / (1.0 - ADAM_B1 ** ADAM_STEP)
            v_hat = vn / (1.0 - ADAM_B2 ** ADAM_STEP)
            g_ref[...] = g
            d_ref[...] = -ADAM_LR * (m_hat / (jnp.sqrt(v_hat) + ADAM_EPS) + ADAM_WD * w_ref[...])
            mo_ref[...] = mn
            vo_ref[...] = vn

        if nl == 1:
            update(p_refs[0])
        else:
            for layer in range(nl):
                pl.when(pl.program_id(0) == layer)(lambda p_ref=p_refs[layer]: update(p_ref))

    def parts_spec(layer):
        def index(l, i):
            return (0, jnp.where(l < layer, 0, jnp.where(l > layer, n_i - 1, i)), 0)
        return pl.BlockSpec((N_DEV, tr, c), index)

    blk = pl.BlockSpec((None, tr, c), lambda l, i: (l, i, 0))
    return pl.pallas_call(
        body, name=name, grid=(nl, n_i),
        in_specs=[parts_spec(layer) for layer in range(nl)] + [blk, blk, blk],
        out_specs=[blk] * 4, out_shape=[jax.ShapeDtypeStruct((nl, r, c), F32)] * 4,
        compiler_params=_params(("arbitrary", "arbitrary")),
    )(*parts, w, m, v)


def _pack(parts, lead=()):
    flat = jnp.concatenate([p.reshape(lead + (-1,)) for p in parts], axis=-1)
    n = flat.shape[-1]
    rows = -(-n // (8 * 128)) * 8
    flat = jnp.pad(flat, [(0, 0)] * len(lead) + [(0, rows * 128 - n)])
    return flat.reshape(lead + (rows, 128))


def _unpack(slab, shapes):
    flat = slab.reshape(-1)
    out, off = [], 0
    for shp in shapes:
        size = 1
        for dim in shp:
            size *= dim
        out.append(flat[off:off + size].reshape(shp))
        off += size
    return out


def _to_dev_major(g, axis):
    shp = g.shape
    g = g.reshape(shp[:axis] + (N_DEV, shp[axis] // N_DEV) + shp[axis + 1:])
    return jnp.moveaxis(g, axis, 0)


def _from_dev_major(g, axis):
    g = jnp.moveaxis(g, 0, axis)
    shp = g.shape
    return g.reshape(shp[:axis] + (shp[axis] * shp[axis + 1],) + shp[axis + 2:])


SMALL_SHARDED = ("conv_a", "conv_b", "norm_mix_odd", "pool_scale", "conv_ffn_w")
SMALL_REPLICATED = ("norm_mix_even", "ln_a_g", "ln_a_b", "norm_ffn", "norm_final")
BIG = {"w_in": ("w_in", 0, 320), "w_out": ("w_out", 0, 128), "w_pool": ("w_pool", 0, 128),
       "w_up0": ("w_up", 0, 352), "w_up1": ("w_up", 1, 352),
       "w_down0": ("w_down", 0, 352), "w_down1": ("w_down", 1, 352)}
COLUMN_SHARDED = ("w_in", "w_up")


def kernel(x, norm_mix_even, w_in, conv_a, ln_a_g, ln_a_b, conv_b, w_out, norm_mix_odd, w_pool, pool_scale, norm_ffn, w_up, conv_ffn_w, w_down, norm_final, loss_target, m_norm_mix_even, m_w_in, m_conv_a, m_ln_a_g, m_ln_a_b, m_conv_b, m_w_out, m_norm_mix_odd, m_w_pool, m_pool_scale, m_norm_ffn, m_w_up, m_conv_ffn_w, m_w_down, m_norm_final, v_norm_mix_even, v_w_in, v_conv_a, v_ln_a_g, v_ln_a_b, v_conv_b, v_w_out, v_norm_mix_odd, v_w_pool, v_pool_scale, v_norm_ffn, v_w_up, v_conv_ffn_w, v_w_down, v_norm_final):
    names = ("norm_mix_even", "w_in", "conv_a", "ln_a_g", "ln_a_b", "conv_b", "w_out", "norm_mix_odd", "w_pool",
             "pool_scale", "norm_ffn", "w_up", "conv_ffn_w", "w_down", "norm_final")
    wts = dict(zip(names, (norm_mix_even, w_in, conv_a, ln_a_g, ln_a_b, conv_b, w_out, norm_mix_odd, w_pool,
                           pool_scale, norm_ffn, w_up, conv_ffn_w, w_down, norm_final)))
    mom = dict(zip(names, (m_norm_mix_even, m_w_in, m_conv_a, m_ln_a_g, m_ln_a_b, m_conv_b, m_w_out, m_norm_mix_odd,
                           m_w_pool, m_pool_scale, m_norm_ffn, m_w_up, m_conv_ffn_w, m_w_down, m_norm_final)))
    var = dict(zip(names, (v_norm_mix_even, v_w_in, v_conv_a, v_ln_a_g, v_ln_a_b, v_conv_b, v_w_out, v_norm_mix_odd,
                           v_w_pool, v_pool_scale, v_norm_ffn, v_w_up, v_conv_ffn_w, v_w_down, v_norm_final)))
    d = x.shape[-1]

    def shard3d(t, param):
        a = t[param]
        if param in COLUMN_SHARDED:
            return jnp.swapaxes(a, 1, 2)
        return a.reshape(a.shape[0], -1, a.shape[-1])

    def unshard3d(a, param):
        if param in COLUMN_SHARDED:
            return jnp.swapaxes(a, 1, 2)
        return a.reshape(wts[param].shape)

    def shard2d(t, key):
        param, layer, _ = BIG[key]
        return shard3d(t, param)[layer]

    small_w = _pack([wts[k] for k in SMALL_SHARDED])
    bf = {k: shard2d(wts, k).astype(BF16) for k in BIG}
    gather_groups = {"mix0": ("w_in", "w_out", "small"), "ffn0": ("w_up0", "w_down0"),
                     "ffn1": ("w_pool", "w_up1", "w_down1")}
    order = list(gather_groups)
    started = {}

    def start_gather(grp, after):
        arrs = [small_w if k == "small" else bf[k] for k in gather_groups[grp]]
        started[grp] = _exchange_start(arrs, [False] * len(arrs), after, name=f"gather_{grp}_start", peers=CHIP_PEERS)

    start_gather(order[0], small_w)

    def weights(grp, after):
        keys = gather_groups[grp]
        lands = _exchange_wait(started[grp], [False] * len(keys), after, name=f"gather_{grp}_wait", peers=CHIP_PEERS)
        forwarded = _forward_start(lands, small_w, name=f"gather_{grp}_forward")
        if grp != order[-1]:
            start_gather(order[order.index(grp) + 1], forwarded[-1])
            behind = started[order[order.index(grp) + 1]][-1]
        else:
            behind = forwarded[-1]
        gw = dict(zip(keys, _forward_wait(forwarded, behind, name=f"gather_{grp}_forward_wait")))
        if grp == "ffn0":
            return {"w_up": gw["w_up0"], "w_down": gw["w_down0"].reshape(N_PAIR, -1, d)}
        if grp == "ffn1":
            return {"w_up": gw["w_up1"], "w_down": gw["w_down1"].reshape(N_PAIR, -1, d),
                    "w_pool": _from_dev_major(gw["w_pool"].reshape(N_DEV, len(POOL_WINDOWS), -1, POOL_GROUP), 1)}
        per_dev = gw["small"].reshape(N_DEV, -1)
        sizes = [wts[k].size for k in SMALL_SHARDED]
        offs = [sum(sizes[:i]) for i in range(len(sizes))]
        small_full = {k: per_dev[:, o:o + n_].reshape((N_DEV,) + wts[k].shape)
                      for k, o, n_ in zip(SMALL_SHARDED, offs, sizes)}
        return {
            "norm_mix_even": norm_mix_even, "ln_a_g": ln_a_g, "ln_a_b": ln_a_b, "norm_ffn": norm_ffn,
            "norm_final": norm_final[None],
            "w_in_t": gw["w_in"].reshape(-1, d),
            "w_out": gw["w_out"].reshape(-1, d),
            "conv_a": _from_dev_major(small_full["conv_a"][:, 0], 1),
            "conv_b": _from_dev_major(small_full["conv_b"][:, 0], 1),
            "norm_mix_odd": _from_dev_major(small_full["norm_mix_odd"], 1),
            "pool_scale": _from_dev_major(small_full["pool_scale"], 1),
            "conv_ffn": [small_full["conv_ffn_w"][:, l] for l in range(2)],
        }

    def dev_major(k, g):
        if k == "w_pool":
            return _to_dev_major(g, 1).reshape(N_DEV, -1, POOL_GROUP)
        return g.reshape(N_DEV, -1, g.shape[-1])

    sent = {}

    def grads_ready(grp, grads):
        keys = tuple(grads)
        parts = [dev_major(k, grads[k]) for k in keys]
        sent[grp] = (keys, _exchange_start(parts, [True] * len(keys), small_w, name=f"grads_{grp}_start"))
        return sent[grp][1][-1]

    loss, dx, g = _local_step(x[0], loss_target[0], weights, grads_ready)

    small_parts = _pack([
        _to_dev_major(g["conv_a"], 1), _to_dev_major(g["conv_b"], 1), _to_dev_major(g["norm_mix_odd"], 1),
        _to_dev_major(g["pool_scale"], 1), jnp.stack(g["conv_ffn"], axis=1)], lead=(N_DEV,))
    repl_parts = _pack([g[k] for k in SMALL_REPLICATED])
    sent["small"] = (("small", "replicated"),
                     _exchange_start([small_parts, repl_parts], [True, False], dx, name="grads_small_start"))

    landed, out = {}, {}

    def wait_grads(grp, after, scat=None):
        keys, st = sent[grp]
        scat = [True] * len(keys) if scat is None else scat
        landed.update(zip(keys, _exchange_wait(st, scat, after, name=f"grads_{grp}_wait")))

    def update(param, keys):
        res = _adamw([landed[k] for k in keys], shard3d(wts, param), shard3d(mom, param), shard3d(var, param),
                     name=f"adamw_{param}", tr=BIG[keys[0]][2])
        out[param] = [unshard3d(t, param) for t in res]

    wait_grads("ffn1", sent["small"][1][-1])
    wait_grads("ffn0", landed["w_up1"])
    update("w_up", ("w_up0", "w_up1"))
    update("w_down", ("w_down0", "w_down1"))
    wait_grads("mix0", out["w_down"][1])
    update("w_in", ("w_in",))
    update("w_out", ("w_out",))
    update("w_pool", ("w_pool",))
    wait_grads("small", out["w_pool"][1], [True, False])

    def slab(t):
        return t[None]

    res = _adamw([landed["small"]], slab(small_w), slab(_pack([mom[k] for k in SMALL_SHARDED])),
                 slab(_pack([var[k] for k in SMALL_SHARDED])), name="adamw_small", tr=small_w.shape[0])
    unpacked = [_unpack(t, [wts[k].shape for k in SMALL_SHARDED]) for t in res]
    for j, k in enumerate(SMALL_SHARDED):
        out[k] = [u[j] for u in unpacked]
    repl_w = _pack([wts[k] for k in SMALL_REPLICATED])
    res = _adamw([landed["replicated"]], slab(repl_w), slab(_pack([mom[k] for k in SMALL_REPLICATED])),
                 slab(_pack([var[k] for k in SMALL_REPLICATED])), name="adamw_replicated", tr=repl_w.shape[0])
    unpacked = [_unpack(t, [wts[k].shape for k in SMALL_REPLICATED]) for t in res]
    for j, k in enumerate(SMALL_REPLICATED):
        out[k] = [u[j] for u in unpacked]

    loss = lax.psum(loss, ("x", "y", "c"))
    return (loss, dx[None], *[out[k][0] for k in names], *[out[k][1] for k in names],
            *[out[k][2] for k in names], *[out[k][3] for k in names])
```

```python
import jax
import jax.numpy as jnp
from jax import lax
from jax.experimental import pallas as pl
from jax.experimental.pallas import tpu as pltpu

F32 = jnp.float32
BF16 = jnp.bfloat16

RMS_EPS = 1e-6
LN_EPS = 1e-5
ADAM_LR = 0.001
ADAM_B1 = 0.9
ADAM_B2 = 0.999
ADAM_EPS = 1e-08
ADAM_WD = 0.01
ADAM_STEP = 10

N_DEV = 8
N_PAIR = N_DEV // 2
A_WIDTH = 512
A_TAPS = 31
POOL_WINDOWS = (2, 4, 8, 16)
POOL_GROUP = 256
HALO_A = 32
HALO_S = 16
VMEM_LIMIT = 56 * 1024 * 1024


def _params(sem, vmem=VMEM_LIMIT):
    return pltpu.CompilerParams(dimension_semantics=sem, vmem_limit_bytes=vmem)


def _sigmoid(x):
    return 0.5 * jnp.tanh(0.5 * x) + 0.5


def _prev_blk(i, ts, hb):
    return jnp.maximum(i * (ts // hb) - 1, 0)


def _next_blk(i, ts, hb, s):
    return jnp.minimum((i + 1) * (ts // hb), s // hb - 1)


def _mm(a, b, *, name, ta=False, tb=False, add=None, out_dtype=F32, tm=512, tn=512, tk=None):
    m, k = (a.shape[1], a.shape[0]) if ta else a.shape
    n = b.shape[0] if tb else b.shape[1]
    tk = k if tk is None else tk
    tm, tn, tk = min(tm, m), min(tn, n), min(tk, k)
    assert m % tm == 0 and n % tn == 0 and k % tk == 0, (name, m, n, k, tm, tn, tk)
    nk = k // tk
    dims = (((0,) if ta else (1,), (1,) if tb else (0,)), ((), ()))
    n_in = 2 + (add is not None)

    def body(*refs):
        a_ref, b_ref = refs[0], refs[1]
        add_ref = refs[2] if add is not None else None
        o_ref = refs[n_in]
        part = lax.dot_general(a_ref[...].astype(BF16), b_ref[...].astype(BF16), dims, preferred_element_type=F32)

        def finish(r):
            if add_ref is not None:
                r = r + add_ref[...]
            o_ref[...] = r.astype(out_dtype)

        if nk == 1:
            finish(part)
            return
        acc_ref = refs[-1]
        kk = pl.program_id(2)

        @pl.when(kk == 0)
        def _():
            acc_ref[...] = part

        @pl.when(kk > 0)
        def _():
            acc_ref[...] += part

        @pl.when(kk == nk - 1)
        def _():
            finish(acc_ref[...])

    a_spec = pl.BlockSpec((tk, tm), lambda i, j, kk: (kk, i)) if ta else pl.BlockSpec((tm, tk), lambda i, j, kk: (i, kk))
    b_spec = pl.BlockSpec((tn, tk), lambda i, j, kk: (j, kk)) if tb else pl.BlockSpec((tk, tn), lambda i, j, kk: (kk, j))
    in_specs = [a_spec, b_spec]
    args = [a, b]
    if add is not None:
        in_specs.append(pl.BlockSpec((tm, tn), lambda i, j, kk: (i, j)))
        args.append(add)
    return pl.pallas_call(
        body, name=name, grid=(m // tm, n // tn, nk),
        in_specs=in_specs, out_specs=pl.BlockSpec((tm, tn), lambda i, j, kk: (i, j)),
        out_shape=jax.ShapeDtypeStruct((m, n), out_dtype),
        scratch_shapes=[pltpu.VMEM((tm, tn), F32)] if nk > 1 else [],
        compiler_params=_params(("parallel", "parallel", "arbitrary")),
    )(*args)


def _rms_rows(xv, gv):
    return xv * lax.rsqrt(jnp.mean(xv * xv, axis=-1, keepdims=True) + RMS_EPS) * gv


_NT = (((1,), (1,)), ((), ()))
_TN = (((0,), (0,)), ((), ()))


def _rms_mm(x, g, wt, *, name, out_dtype, tm=1024, tn=512):
    s, d = x.shape
    n = wt.shape[0]
    tm = min(tm, s)
    assert s % tm == 0 and n % tn == 0

    def body(x_ref, g_ref, w_ref, h_ref, z_ref, hs_ref):
        @pl.when(pl.program_id(1) == 0)
        def _():
            hv = _rms_rows(x_ref[...], g_ref[...]).astype(BF16)
            hs_ref[...] = hv
            h_ref[...] = hv

        z_ref[...] = lax.dot_general(hs_ref[...], w_ref[...], _NT, preferred_element_type=F32).astype(out_dtype)

    return pl.pallas_call(
        body, name=name, grid=(s // tm, n // tn),
        in_specs=[pl.BlockSpec((tm, d), lambda i, j: (i, 0)),
                  pl.BlockSpec((1, d), lambda i, j: (0, 0)),
                  pl.BlockSpec((tn, d), lambda i, j: (j, 0))],
        out_specs=[pl.BlockSpec((tm, d), lambda i, j: (i, 0)),
                   pl.BlockSpec((tm, tn), lambda i, j: (i, j))],
        out_shape=[jax.ShapeDtypeStruct((s, d), BF16), jax.ShapeDtypeStruct((s, n), out_dtype)],
        scratch_shapes=[pltpu.VMEM((tm, d), BF16)],
        compiler_params=_params(("parallel", "arbitrary")),
    )(x, g, wt)


def _rms_bwd_rows(xv, gv, dh):
    r = lax.rsqrt(jnp.mean(xv * xv, axis=-1, keepdims=True) + RMS_EPS)
    xh = xv * r
    dn = dh * gv
    dx = r * (dn - xh * jnp.mean(dn * xh, axis=-1, keepdims=True))
    return dx, dh * xh


def _mm_rms_bwd(a, b, x, g, dres, dep, *, name, tm=512):
    s, k = a.shape
    d = b.shape[1]
    tm = min(tm, s)

    def body(a_ref, b_ref, x_ref, g_ref, dres_ref, dep_ref, dx_ref, dg_ref):
        @pl.when(pl.program_id(0) == 0)
        def _():
            dg_ref[...] = jnp.zeros_like(dg_ref)

        dh = jnp.dot(a_ref[...], b_ref[...], preferred_element_type=F32)
        dx, dgr = _rms_bwd_rows(x_ref[...], g_ref[...], dh)
        dx_ref[...] = dres_ref[...] + dx
        dg_ref[...] += jnp.sum(dgr, axis=0, keepdims=True)

    row = pl.BlockSpec((tm, d), lambda i: (i, 0))
    vec = pl.BlockSpec((1, d), lambda i: (0, 0))
    return pl.pallas_call(
        body, name=name, grid=(s // tm,),
        in_specs=[pl.BlockSpec((tm, k), lambda i: (i, 0)), pl.BlockSpec((k, d), lambda i: (0, 0)), row, vec, row,
                  pl.BlockSpec(memory_space=pl.ANY)],
        out_specs=[row, vec],
        out_shape=[jax.ShapeDtypeStruct((s, d), F32), jax.ShapeDtypeStruct((1, d), F32)],
        compiler_params=_params(("arbitrary",)),
    )(a, b, x, g, dres, dep)


def _final_loss(x, g, target, *, name, ts=256):
    s, d = x.shape
    ts = min(ts, s)

    def body(x_ref, g_ref, t_ref, loss_ref, dx_ref, dxb_ref, dg_ref):
        @pl.when(pl.program_id(0) == 0)
        def _():
            loss_ref[...] = jnp.zeros_like(loss_ref)
            dg_ref[...] = jnp.zeros_like(dg_ref)

        xv = x_ref[...]
        gv = g_ref[...]
        r = lax.rsqrt(jnp.mean(xv * xv, axis=-1, keepdims=True) + RMS_EPS)
        xh = xv * r
        err = xh * gv - t_ref[...]
        loss_ref[...] += 0.5 * jnp.sum(jnp.mean(err * err, axis=-1, keepdims=True), axis=0, keepdims=True)
        dy = err * (1.0 / d)
        dn = dy * gv
        dx = r * (dn - xh * jnp.mean(dn * xh, axis=-1, keepdims=True))
        dx_ref[...] = dx
        dxb_ref[...] = dx.astype(BF16)
        dg_ref[...] += jnp.sum(dy * xh, axis=0, keepdims=True)

    row = pl.BlockSpec((ts, d), lambda i: (i, 0))
    vec = pl.BlockSpec((1, d), lambda i: (0, 0))
    return pl.pallas_call(
        body, name=name, grid=(s // ts,),
        in_specs=[row, vec, row],
        out_specs=[pl.BlockSpec((1, 1), lambda i: (0, 0)), row, row, vec],
        out_shape=[jax.ShapeDtypeStruct((1, 1), F32), jax.ShapeDtypeStruct((s, d), F32),
                   jax.ShapeDtypeStruct((s, d), BF16), jax.ShapeDtypeStruct((1, d), F32)],
        compiler_params=_params(("arbitrary",)),
    )(x, g, target)


def _conv_taps(ext_ref, w_ref, n_taps, base, r0, rows, reverse=False):
    acc = None
    for k in range(n_taps):
        off = r0 + (base - k if reverse else base + k)
        term = w_ref[k:k + 1, :] * ext_ref[pl.ds(off, rows), :]
        acc = term if acc is None else acc + term
    return acc


def _shift_copies(src_ref, sh_ref, rows):
    for b in range(1, 8):
        sh_ref[b, 0:rows, :] = src_ref[pl.ds(b, rows), :]


def _shifted(src_ref, sh_ref, start, rows, off):
    a, b = divmod(off, 8)
    ref = src_ref if b == 0 else sh_ref.at[b]
    return ref[pl.ds(start + 8 * a, rows), :]


def _mix0_fwd(z, conv_a, ln_g, ln_b, conv_b, *, name, ts=256, rc=32):
    s = z.shape[0]
    c = A_WIDTH
    hb = HALO_A

    def body(z_ref, zp_ref, wa_ref, lg_ref, lb_ref, wb_ref, ab_ref, ca_ref, exta, extb, sha):
        keep = jnp.where(pl.program_id(0) > 0, 1.0, 0.0)
        zp = zp_ref[...]
        exta[0:hb, :] = zp[:, 0:c] * _sigmoid(zp[:, c:2 * c]) * keep
        extb[0:hb, :] = zp[:, 3 * c:4 * c] * zp[:, 4 * c:5 * c] * keep
        exta[hb:hb + ts, :] = z_ref[:, 0:c] * _sigmoid(z_ref[:, c:2 * c])
        extb[hb:hb + ts, :] = z_ref[:, 3 * c:4 * c] * z_ref[:, 4 * c:5 * c]
        _shift_copies(exta, sha, hb + ts - 8)
        lg = lg_ref[...]
        lb = lb_ref[...]
        for q in range(ts // rc):
            r0 = q * rc
            ca = None
            for k in range(A_TAPS):
                term = wa_ref[k:k + 1, :] * _shifted(exta, sha, r0, rc, hb - (A_TAPS - 1) + k)
                ca = term if ca is None else ca + term
            ca_ref[r0:r0 + rc, :] = ca
            mu = jnp.mean(ca, axis=-1, keepdims=True)
            xc = ca - mu
            rs = lax.rsqrt(jnp.mean(xc * xc, axis=-1, keepdims=True) + LN_EPS)
            l = xc * rs * lg + lb
            ab_ref[r0:r0 + rc, 0:c] = (l * _sigmoid(l)).astype(BF16)
            cbc = _conv_taps(extb, wb_ref, 3, hb - 2, r0, rc)
            ab_ref[r0:r0 + rc, c:2 * c] = (z_ref[r0:r0 + rc, 2 * c:3 * c] * cbc).astype(BF16)

    return pl.pallas_call(
        body, name=name, grid=(s // ts,),
        in_specs=[pl.BlockSpec((ts, 5 * c), lambda i: (i, 0)),
                  pl.BlockSpec((hb, 5 * c), lambda i: (_prev_blk(i, ts, hb), 0)),
                  pl.BlockSpec((32, c), lambda i: (0, 0)),
                  pl.BlockSpec((1, c), lambda i: (0, 0)),
                  pl.BlockSpec((1, c), lambda i: (0, 0)),
                  pl.BlockSpec((8, c), lambda i: (0, 0))],
        out_specs=[pl.BlockSpec((ts, 2 * c), lambda i: (i, 0)),
                   pl.BlockSpec((ts, c), lambda i: (i, 0))],
        out_shape=[jax.ShapeDtypeStruct((s, 2 * c), BF16), jax.ShapeDtypeStruct((s, c), F32)],
        scratch_shapes=[pltpu.VMEM((hb + ts, c), F32), pltpu.VMEM((hb + ts, c), F32),
                        pltpu.VMEM((8, hb + ts - 8, c), F32)],
        compiler_params=_params(("parallel",)),
    )(z, z, conv_a, ln_g, ln_b, conv_b)


def _mix0_bwd(z, ca, dab, conv_a, ln_g, ln_b, conv_b, *, name, ts=256, rc=32):
    s = z.shape[0]
    c = A_WIDTH
    hb = HALO_A
    ta = A_TAPS

    def body(z_ref, zp_ref, zn_ref, ca_ref, can_ref, d_ref, dn_ref, wa_ref, lg_ref, lb_ref, wb_ref,
             dz_ref, dwa_ref, dwb_ref, dlg_ref, dlb_ref, exta, extb, extdca, extdcb, shd):
        i = pl.program_id(0)
        keep_p = jnp.where(i > 0, 1.0, 0.0)
        keep_n = jnp.where(i < s // ts - 1, 1.0, 0.0)

        @pl.when(i == 0)
        def _():
            dwa_ref[...] = jnp.zeros_like(dwa_ref)
            dwb_ref[...] = jnp.zeros_like(dwb_ref)
            dlg_ref[...] = jnp.zeros_like(dlg_ref)
            dlb_ref[...] = jnp.zeros_like(dlb_ref)

        lg = lg_ref[...]
        lb = lb_ref[...]
        zp = zp_ref[...]
        exta[0:hb, :] = zp[:, 0:c] * _sigmoid(zp[:, c:2 * c]) * keep_p
        extb[0:hb, :] = zp[:, 3 * c:4 * c] * zp[:, 4 * c:5 * c] * keep_p
        exta[hb:hb + ts, :] = z_ref[:, 0:c] * _sigmoid(z_ref[:, c:2 * c])
        extb[hb:hb + ts, :] = z_ref[:, 3 * c:4 * c] * z_ref[:, 4 * c:5 * c]

        def ln_bwd(cav, dav):
            mu = jnp.mean(cav, axis=-1, keepdims=True)
            xc = cav - mu
            rs = lax.rsqrt(jnp.mean(xc * xc, axis=-1, keepdims=True) + LN_EPS)
            nv = xc * rs
            l = nv * lg + lb
            sg = _sigmoid(l)
            dl = dav * (sg * (1.0 + l * (1.0 - sg)))
            dnv = dl * lg
            dca = rs * (dnv - jnp.mean(dnv, axis=-1, keepdims=True)
                        - nv * jnp.mean(dnv * nv, axis=-1, keepdims=True))
            return dca, dl, nv

        dlg_acc = jnp.zeros((1, c), F32)
        dlb_acc = jnp.zeros((1, c), F32)
        for q in range(ts // rc):
            r0 = q * rc
            dca, dl, nv = ln_bwd(ca_ref[r0:r0 + rc, :], d_ref[r0:r0 + rc, 0:c])
            extdca[r0:r0 + rc, :] = dca
            dlg_acc = dlg_acc + jnp.sum(dl * nv, axis=0, keepdims=True)
            dlb_acc = dlb_acc + jnp.sum(dl, axis=0, keepdims=True)
            extdcb[r0:r0 + rc, :] = d_ref[r0:r0 + rc, c:2 * c] * z_ref[r0:r0 + rc, 2 * c:3 * c]
        dca_n, _, _ = ln_bwd(can_ref[...], dn_ref[:, 0:c])
        extdca[ts:ts + hb, :] = dca_n * keep_n
        extdcb[ts:ts + hb, :] = dn_ref[:, c:2 * c] * zn_ref[:, 2 * c:3 * c] * keep_n
        dlg_ref[...] += dlg_acc
        dlb_ref[...] += dlb_acc
        _shift_copies(extdca, shd, ts + hb - 8)

        for q in range(ts // rc):
            r0 = q * rc
            zr = z_ref[r0:r0 + rc, :]
            dga = None
            for k in range(ta):
                term = wa_ref[k:k + 1, :] * _shifted(extdca, shd, r0, rc, ta - 1 - k)
                dga = term if dga is None else dga + term
            sg = _sigmoid(zr[:, c:2 * c])
            dz_ref[r0:r0 + rc, 0:c] = (dga * sg).astype(BF16)
            dz_ref[r0:r0 + rc, c:2 * c] = (dga * zr[:, 0:c] * sg * (1.0 - sg)).astype(BF16)
            cbc = _conv_taps(extb, wb_ref, 3, hb - 2, r0, rc)
            dz_ref[r0:r0 + rc, 2 * c:3 * c] = (d_ref[r0:r0 + rc, c:2 * c] * cbc).astype(BF16)
            dcb = _conv_taps(extdcb, wb_ref, 3, 2, r0, rc, reverse=True)
            dz_ref[r0:r0 + rc, 3 * c:4 * c] = (dcb * zr[:, 4 * c:5 * c]).astype(BF16)
            dz_ref[r0:r0 + rc, 4 * c:5 * c] = (dcb * zr[:, 3 * c:4 * c]).astype(BF16)

        for k in range(ta):
            part = None
            for q in range(ts // rc):
                r0 = q * rc
                p = exta[hb + r0:hb + r0 + rc, :] * _shifted(extdca, shd, r0, rc, ta - 1 - k)
                for r in range(0, rc, 8):
                    part = p[r:r + 8, :] if part is None else part + p[r:r + 8, :]
            dwa_ref[k:k + 1, :] += jnp.sum(part, axis=0, keepdims=True)
        dcb_t = extdcb[0:ts, :]
        for k in range(3):
            dwb_ref[k:k + 1, :] += jnp.sum(dcb_t * extb[pl.ds(hb - 2 + k, ts), :], axis=0, keepdims=True)

    def tile(w):
        return pl.BlockSpec((ts, w), lambda i: (i, 0))

    def prev(w):
        return pl.BlockSpec((hb, w), lambda i: (_prev_blk(i, ts, hb), 0))

    def nxt(w):
        return pl.BlockSpec((hb, w), lambda i: (_next_blk(i, ts, hb, s), 0))

    def const(r, w):
        return pl.BlockSpec((r, w), lambda i: (0, 0))

    return pl.pallas_call(
        body, name=name, grid=(s // ts,),
        in_specs=[tile(5 * c), prev(5 * c), nxt(5 * c), tile(c), nxt(c), tile(2 * c), nxt(2 * c),
                  const(32, c), const(1, c), const(1, c), const(8, c)],
        out_specs=[tile(5 * c), const(32, c), const(8, c), const(1, c), const(1, c)],
        out_shape=[jax.ShapeDtypeStruct((s, 5 * c), BF16), jax.ShapeDtypeStruct((32, c), F32),
                   jax.ShapeDtypeStruct((8, c), F32), jax.ShapeDtypeStruct((1, c), F32),
                   jax.ShapeDtypeStruct((1, c), F32)],
        scratch_shapes=[pltpu.VMEM((hb + ts, c), F32), pltpu.VMEM((hb + ts, c), F32),
                        pltpu.VMEM((ts + hb, c), F32), pltpu.VMEM((ts + hb, c), F32),
                        pltpu.VMEM((8, ts + hb - 8, c), F32)],
        compiler_params=_params(("arbitrary",)),
    )(z, z, z, ca, ca, dab, dab, conv_a, ln_g, ln_b, conv_b)


def _ffn_up(x, g, w8, *, name, tm=2048):
    s, d = x.shape
    nb, c, _ = w8.shape
    tm = min(tm, s)

    def body(x_ref, g_ref, w_ref, h_ref, u_ref, hs_ref):
        @pl.when(pl.program_id(1) == 0)
        def _():
            hv = _rms_rows(x_ref[...], g_ref[...]).astype(BF16)
            hs_ref[...] = hv
            h_ref[...] = hv

        u_ref[...] = lax.dot_general(hs_ref[...], w_ref[...], _NT, preferred_element_type=F32).astype(BF16)

    return pl.pallas_call(
        body, name=name, grid=(s // tm, nb),
        in_specs=[pl.BlockSpec((tm, d), lambda i, k: (i, 0)),
                  pl.BlockSpec((1, d), lambda i, k: (0, 0)),
                  pl.BlockSpec((None, c, d), lambda i, k: (k, 0, 0))],
        out_specs=[pl.BlockSpec((tm, d), lambda i, k: (i, 0)),
                   pl.BlockSpec((None, tm, c), lambda i, k: (k, i, 0))],
        out_shape=[jax.ShapeDtypeStruct((s, d), BF16), jax.ShapeDtypeStruct((nb, s, c), BF16)],
        scratch_shapes=[pltpu.VMEM((tm, d), BF16)],
        compiler_params=_params(("parallel", "arbitrary")),
    )(x, g, w8)


def _ffn_mid(up8, wc8, *, name, ts=512, rc=32):
    nb, s, c = up8.shape
    hb = HALO_S
    ts = min(ts, s)

    def body(g_ref, gp_ref, v_ref, vp_ref, wg_ref, wv_ref, act_ref, ug_ref, uv_ref, extg, extv):
        keep = jnp.where(pl.program_id(0) > 0, 1.0, 0.0)
        extg[0:hb, :] = gp_ref[...].astype(F32) * keep
        extv[0:hb, :] = vp_ref[...].astype(F32) * keep
        extg[hb:hb + ts, :] = g_ref[...].astype(F32)
        extv[hb:hb + ts, :] = v_ref[...].astype(F32)
        for q in range(ts // rc):
            r0 = q * rc
            gg = _conv_taps(extg, wg_ref, 3, hb - 2, r0, rc)
            vv = _conv_taps(extv, wv_ref, 3, hb - 2, r0, rc)
            ug_ref[r0:r0 + rc, :] = gg.astype(BF16)
            uv_ref[r0:r0 + rc, :] = vv.astype(BF16)
            act_ref[r0:r0 + rc, :] = (gg * _sigmoid(gg) * vv).astype(BF16)

    def blk(rows, off, halo):
        if halo:
            return pl.BlockSpec((None, rows, c), lambda i, j: (j + off, _prev_blk(i, ts, hb), 0))
        return pl.BlockSpec((None, rows, c), lambda i, j: (j + off, i, 0))

    def taps(off):
        return pl.BlockSpec((None, 8, c), lambda i, j: (j + off, 0, 0))

    tile = pl.BlockSpec((None, ts, c), lambda i, j: (j, i, 0))
    out = jax.ShapeDtypeStruct((N_PAIR, s, c), BF16)
    return pl.pallas_call(
        body, name=name, grid=(s // ts, N_PAIR),
        in_specs=[blk(ts, 0, False), blk(hb, 0, True), blk(ts, N_PAIR, False), blk(hb, N_PAIR, True),
                  taps(0), taps(N_PAIR)],
        out_specs=[tile, tile, tile], out_shape=[out, out, out],
        scratch_shapes=[pltpu.VMEM((hb + ts, c), F32), pltpu.VMEM((hb + ts, c), F32)],
        compiler_params=_params(("parallel", "parallel")),
    )(up8, up8, up8, up8, wc8, wc8)


def _ffn_down(act4, wd4, x, *, name, tm=512):
    npair, s, c = act4.shape
    d = x.shape[1]
    tm = min(tm, s)

    def body(a_ref, w_ref, x_ref, o_ref):
        acc = x_ref[...]
        for j in range(npair):
            acc = acc + jnp.dot(a_ref[j], w_ref[j], preferred_element_type=F32)
        o_ref[...] = acc

    return pl.pallas_call(
        body, name=name, grid=(s // tm,),
        in_specs=[pl.BlockSpec((npair, tm, c), lambda i: (0, i, 0)),
                  pl.BlockSpec((npair, c, d), lambda i: (0, 0, 0)),
                  pl.BlockSpec((tm, d), lambda i: (i, 0))],
        out_specs=pl.BlockSpec((tm, d), lambda i: (i, 0)),
        out_shape=jax.ShapeDtypeStruct((s, d), F32),
        compiler_params=_params(("parallel",)),
    )(act4, wd4, x)


def _ffn_dact(db, wd4, *, name, tm=1024):
    s, d = db.shape
    npair, c, _ = wd4.shape
    tm = min(tm, s)

    def body(d_ref, w_ref, o_ref):
        o_ref[...] = lax.dot_general(d_ref[...], w_ref[...], (((1,), (1,)), ((), ())),
                                     preferred_element_type=F32).astype(BF16)

    return pl.pallas_call(
        body, name=name, grid=(s // tm, npair),
        in_specs=[pl.BlockSpec((tm, d), lambda i, j: (i, 0)),
                  pl.BlockSpec((None, c, d), lambda i, j: (j, 0, 0))],
        out_specs=pl.BlockSpec((None, tm, c), lambda i, j: (j, i, 0)),
        out_shape=jax.ShapeDtypeStruct((npair, s, c), BF16),
        compiler_params=_params(("parallel", "parallel")),
    )(db, wd4)


def _ffn_dwdown(act4, db, *, name):
    npair, s, c = act4.shape
    d = db.shape[1]

    def body(a_ref, d_ref, o_ref):
        o_ref[...] = lax.dot_general(a_ref[...], d_ref[...], (((0,), (0,)), ((), ())),
                                     preferred_element_type=F32).astype(BF16)

    return pl.pallas_call(
        body, name=name, grid=(npair,),
        in_specs=[pl.BlockSpec((None, s, c), lambda j: (j, 0, 0)),
                  pl.BlockSpec((s, d), lambda j: (0, 0))],
        out_specs=pl.BlockSpec((None, c, d), lambda j: (j, 0, 0)),
        out_shape=jax.ShapeDtypeStruct((npair, c, d), BF16),
        compiler_params=_params(("parallel",)),
    )(act4, db)


def _ffn_midbwd(up8, ug4, uv4, dact4, wc8, *, name, ts=512, rc=32):
    nb, s, c = up8.shape
    hb = HALO_S
    ts = min(ts, s)
    n_i = s // ts

    def body(pg_ref, pv_ref, ug_ref, ugn_ref, uv_ref, uvn_ref, d_ref, dn_ref, wg_ref, wv_ref,
             dg_ref, dv_ref, dwg_ref, dwv_ref, extdg, extdv):
        i = pl.program_id(1)
        keep_n = jnp.where(i < n_i - 1, 1.0, 0.0)

        @pl.when(i == 0)
        def _():
            dwg_ref[...] = jnp.zeros_like(dwg_ref)
            dwv_ref[...] = jnp.zeros_like(dwv_ref)

        def du_rows(r0, rows, gg, vv, dav):
            sg = _sigmoid(gg)
            extdg[r0:r0 + rows, :] = dav * vv * (sg * (1.0 + gg * (1.0 - sg)))
            extdv[r0:r0 + rows, :] = dav * (gg * sg)

        for q in range(ts // rc):
            rows = slice(q * rc, q * rc + rc)
            du_rows(q * rc, rc, ug_ref[rows, :].astype(F32), uv_ref[rows, :].astype(F32),
                    d_ref[rows, :].astype(F32))
        du_rows(ts, 8, ugn_ref[0:8, :].astype(F32), uvn_ref[0:8, :].astype(F32),
                dn_ref[0:8, :].astype(F32) * keep_n)

        def fold(p):
            acc = p[0:8, :]
            for r in range(8, rc, 8):
                acc = acc + p[r:r + 8, :]
            return acc

        for extd, p_ref, w_ref, out_ref, dw_ref in ((extdg, pg_ref, wg_ref, dg_ref, dwg_ref),
                                                    (extdv, pv_ref, wv_ref, dv_ref, dwv_ref)):
            part = [None, None, None]
            for q in range(ts // rc):
                r0 = q * rc
                pre = p_ref[r0:r0 + rc, :].astype(F32)
                dup = None
                for k in range(3):
                    sh = extd[pl.ds(r0 + 2 - k, rc), :]
                    term = w_ref[k:k + 1, :] * sh
                    dup = term if dup is None else dup + term
                    prod = fold(pre * sh)
                    part[k] = prod if part[k] is None else part[k] + prod
                out_ref[r0:r0 + rc, :] = dup.astype(BF16)
            for k in range(3):
                dw_ref[k:k + 1, :] += jnp.sum(part[k], axis=0, keepdims=True)

    def blk(off, nxt):
        if nxt:
            return pl.BlockSpec((None, hb, c), lambda j, i: (j + off, _next_blk(i, ts, hb, s), 0))
        return pl.BlockSpec((None, ts, c), lambda j, i: (j + off, i, 0))

    def taps(off):
        return pl.BlockSpec((None, 8, c), lambda j, i: (j + off, 0, 0))

    tile = blk(0, False)
    acc = pl.BlockSpec((None, 8, c), lambda j, i: (j, 0, 0))
    return pl.pallas_call(
        body, name=name, grid=(N_PAIR, n_i),
        in_specs=[tile, blk(N_PAIR, False), tile, blk(0, True), tile, blk(0, True), tile, blk(0, True),
                  taps(0), taps(N_PAIR)],
        out_specs=[tile, tile, acc, acc],
        out_shape=[jax.ShapeDtypeStruct((N_PAIR, s, c), BF16), jax.ShapeDtypeStruct((N_PAIR, s, c), BF16),
                   jax.ShapeDtypeStruct((N_PAIR, 8, c), F32), jax.ShapeDtypeStruct((N_PAIR, 8, c), F32)],
        scratch_shapes=[pltpu.VMEM((ts + 8, c), F32), pltpu.VMEM((ts + 8, c), F32)],
        compiler_params=_params(("parallel", "arbitrary")),
    )(up8, up8, ug4, ug4, uv4, uv4, dact4, dact4, wc8, wc8)


def _ffn_dh(dupg, dupv, w8, x, g, dres, dep, *, name, tm=512):
    npair, s, c = dupg.shape
    d = x.shape[1]
    tm = min(tm, s)

    def body(dg_ref, dv_ref, w_ref, x_ref, g_ref, dres_ref, dep_ref, dx_ref, dxb_ref, dgain_ref):
        @pl.when(pl.program_id(0) == 0)
        def _():
            dgain_ref[...] = jnp.zeros_like(dgain_ref)

        dh = None
        for j in range(npair):
            for src, k in ((dg_ref, j), (dv_ref, j + npair)):
                part = jnp.dot(src[j], w_ref[k], preferred_element_type=F32)
                dh = part if dh is None else dh + part
        dx, dgr = _rms_bwd_rows(x_ref[...], g_ref[...], dh)
        dx = dres_ref[...] + dx
        dx_ref[...] = dx
        dxb_ref[...] = dx.astype(BF16)
        dgain_ref[...] += jnp.sum(dgr, axis=0, keepdims=True)

    row = pl.BlockSpec((tm, d), lambda i: (i, 0))
    vec = pl.BlockSpec((1, d), lambda i: (0, 0))
    dup = pl.BlockSpec((npair, tm, c), lambda i: (0, i, 0))
    return pl.pallas_call(
        body, name=name, grid=(s // tm,),
        in_specs=[dup, dup, pl.BlockSpec((2 * npair, c, d), lambda i: (0, 0, 0)), row, vec, row,
                  pl.BlockSpec(memory_space=pl.ANY)],
        out_specs=[row, row, vec],
        out_shape=[jax.ShapeDtypeStruct((s, d), F32), jax.ShapeDtypeStruct((s, d), BF16),
                   jax.ShapeDtypeStruct((1, d), F32)],
        compiler_params=_params(("arbitrary",)),
    )(dupg, dupv, w8, x, g, dres, dep)


def _ffn_dwup(h, dupg, dupv, *, name, tm=512):
    npair, s, c = dupg.shape
    d = h.shape[1]

    def body(h_ref, dg_ref, dv_ref, o_ref):
        k = pl.program_id(1)

        @pl.when(k < npair)
        def _():
            o_ref[...] = lax.dot_general(dg_ref[...], h_ref[...], _TN, preferred_element_type=F32).astype(BF16)

        @pl.when(k >= npair)
        def _():
            o_ref[...] = lax.dot_general(dv_ref[...], h_ref[...], _TN, preferred_element_type=F32).astype(BF16)

    return pl.pallas_call(
        body, name=name, grid=(d // tm, 2 * npair),
        in_specs=[pl.BlockSpec((s, tm), lambda m, k: (0, m)),
                  pl.BlockSpec((None, s, c), lambda m, k: (jnp.minimum(k, npair - 1), 0, 0)),
                  pl.BlockSpec((None, s, c), lambda m, k: (jnp.maximum(k - npair, 0), 0, 0))],
        out_specs=pl.BlockSpec((None, c, tm), lambda m, k: (k, 0, m)),
        out_shape=jax.ShapeDtypeStruct((2 * npair, c, d), BF16),
        compiler_params=_params(("parallel", "arbitrary")),
    )(h, dupg, dupv)


def _pool_counts(i, ts, rows, window):
    t = lax.broadcasted_iota(jnp.int32, (rows, 1), 0) + i * ts + 1
    return jnp.minimum(t, window).astype(F32)


def _pool_fwd(x, g, *, name, ts=256):
    s, d = x.shape
    hb = HALO_A
    pg = POOL_GROUP
    ts = min(ts, s)

    def body(x_ref, xp_ref, g_ref, p_ref, ext, sa, sb):
        i = pl.program_id(0)
        keep = jnp.where(i > 0, 1.0, 0.0)
        gv = g_ref[...]
        ext[0:hb, :] = _rms_rows(xp_ref[...], gv) * keep
        ext[hb:hb + ts, :] = _rms_rows(x_ref[...], gv)
        rows = hb + ts - 8
        sa[0:8, :] = jnp.zeros((8, d), F32)
        sb[0:8, :] = jnp.zeros((8, d), F32)
        for gi, w in enumerate(POOL_WINDOWS):
            cols = slice(gi * pg, (gi + 1) * pg)
            cur, nxt, k = ext, sa, 1
            while k < w:
                nxt[8:8 + rows, cols] = cur[8:8 + rows, cols] + cur[pl.ds(8 - k, rows), cols]
                cur, nxt, k = nxt, (sb if nxt is sa else sa), 2 * k
            h = ext[hb:hb + ts, cols]
            p_ref[:, cols] = (cur[hb:hb + ts, cols] / _pool_counts(i, ts, ts, w) - h).astype(BF16)

    return pl.pallas_call(
        body, name=name, grid=(s // ts,),
        in_specs=[pl.BlockSpec((ts, d), lambda i: (i, 0)),
                  pl.BlockSpec((hb, d), lambda i: (_prev_blk(i, ts, hb), 0)),
                  pl.BlockSpec((1, d), lambda i: (0, 0))],
        out_specs=pl.BlockSpec((ts, d), lambda i: (i, 0)),
        out_shape=jax.ShapeDtypeStruct((s, d), BF16),
        scratch_shapes=[pltpu.VMEM((hb + ts, d), F32)] * 3,
        compiler_params=_params(("parallel",)),
    )(x, x, g)


def _pool_mm_fwd(p, w_pool, scale, x, *, name, ts=1024):
    s, d = x.shape
    pg = POOL_GROUP
    ts = min(ts, s)

    def body(p_ref, w_ref, s_ref, x_ref, o_ref, yu_ref):
        yu = jnp.dot(p_ref[...], w_ref[...], preferred_element_type=F32)
        yu_ref[...] = yu.astype(BF16)
        o_ref[...] = x_ref[...] + yu * s_ref[...]

    blk = pl.BlockSpec((ts, pg), lambda i, gi: (i, gi))
    return pl.pallas_call(
        body, name=name, grid=(s // ts, d // pg),
        in_specs=[blk, pl.BlockSpec((None, pg, pg), lambda i, gi: (gi, 0, 0)),
                  pl.BlockSpec((1, pg), lambda i, gi: (0, gi)), blk],
        out_specs=[blk, blk],
        out_shape=[jax.ShapeDtypeStruct((s, d), F32), jax.ShapeDtypeStruct((s, d), BF16)],
        compiler_params=_params(("parallel", "parallel")),
    )(p, w_pool, scale, x)


def _pool_mm_bwd(dres, w_pool, scale, yu, *, name, ts=1024):
    s, d = dres.shape
    pg = POOL_GROUP
    ts = min(ts, s)

    def body(d_ref, w_ref, s_ref, yu_ref, dyc_ref, dp_ref, ds_ref):
        @pl.when(pl.program_id(1) == 0)
        def _():
            ds_ref[...] = jnp.zeros_like(ds_ref)

        dv = d_ref[...]
        dyc = (dv * s_ref[...]).astype(BF16)
        dyc_ref[...] = dyc
        dp_ref[...] = lax.dot_general(dyc, w_ref[...], (((1,), (1,)), ((), ())), preferred_element_type=F32)
        ds_ref[...] += jnp.sum(dv * yu_ref[...].astype(F32), axis=0, keepdims=True)

    blk = pl.BlockSpec((ts, pg), lambda gi, i: (i, gi))
    vec = pl.BlockSpec((1, pg), lambda gi, i: (0, gi))
    return pl.pallas_call(
        body, name=name, grid=(d // pg, s // ts),
        in_specs=[blk, pl.BlockSpec((None, pg, pg), lambda gi, i: (gi, 0, 0)), vec, blk],
        out_specs=[blk, blk, vec],
        out_shape=[jax.ShapeDtypeStruct((s, d), BF16), jax.ShapeDtypeStruct((s, d), F32),
                   jax.ShapeDtypeStruct((1, d), F32)],
        compiler_params=_params(("parallel", "arbitrary")),
    )(dres, w_pool, scale, yu)


def _pool_dw(p, dyc, *, name):
    s, d = p.shape
    pg = POOL_GROUP

    def body(p_ref, d_ref, o_ref):
        o_ref[...] = lax.dot_general(p_ref[...], d_ref[...], (((0,), (0,)), ((), ())),
                                     preferred_element_type=F32).astype(BF16)

    blk = pl.BlockSpec((s, pg), lambda gi: (0, gi))
    return pl.pallas_call(
        body, name=name, grid=(d // pg,),
        in_specs=[blk, blk], out_specs=pl.BlockSpec((None, pg, pg), lambda gi: (gi, 0, 0)),
        out_shape=jax.ShapeDtypeStruct((d // pg, pg, pg), BF16),
        compiler_params=_params(("parallel",)),
    )(p, dyc)


def _pool_bwd(dp, x, g, dres, *, name, ts=256):
    s, d = x.shape
    hb = HALO_A
    pg = POOL_GROUP
    ts = min(ts, s)
    n_i = s // ts

    def body(dp_ref, dpn_ref, x_ref, g_ref, dres_ref, dx_ref, dxb_ref, dg_ref, ext, dh, sa, sb):
        i = pl.program_id(0)
        keep_n = jnp.where(i < n_i - 1, 1.0, 0.0)

        @pl.when(i == 0)
        def _():
            dg_ref[...] = jnp.zeros_like(dg_ref)

        for gi, w in enumerate(POOL_WINDOWS):
            cols = slice(gi * pg, (gi + 1) * pg)
            ext[0:ts, cols] = dp_ref[:, cols] / _pool_counts(i, ts, ts, w)
            ext[ts:ts + hb, cols] = dpn_ref[:, cols] / _pool_counts(i + 1, ts, hb, w) * keep_n
        rows = ts + hb - 8
        sa[rows:rows + 8, :] = jnp.zeros((8, d), F32)
        sb[rows:rows + 8, :] = jnp.zeros((8, d), F32)
        for gi, w in enumerate(POOL_WINDOWS):
            cols = slice(gi * pg, (gi + 1) * pg)
            cur, nxt, k = ext, sa, 1
            while k < w:
                nxt[0:rows, cols] = cur[0:rows, cols] + cur[pl.ds(k, rows), cols]
                cur, nxt, k = nxt, (sb if nxt is sa else sa), 2 * k
            dh[:, cols] = cur[0:ts, cols] - dp_ref[:, cols]
        dx, dgr = _rms_bwd_rows(x_ref[...], g_ref[...], dh[...])
        dx = dres_ref[...] + dx
        dx_ref[...] = dx
        dxb_ref[...] = dx.astype(BF16)
        dg_ref[...] += jnp.sum(dgr, axis=0, keepdims=True)

    row = pl.BlockSpec((ts, d), lambda i: (i, 0))
    vec = pl.BlockSpec((1, d), lambda i: (0, 0))
    return pl.pallas_call(
        body, name=name, grid=(n_i,),
        in_specs=[row, pl.BlockSpec((hb, d), lambda i: (_next_blk(i, ts, hb, s), 0)), row, vec, row],
        out_specs=[row, row, vec],
        out_shape=[jax.ShapeDtypeStruct((s, d), F32), jax.ShapeDtypeStruct((s, d), BF16),
                   jax.ShapeDtypeStruct((1, d), F32)],
        scratch_shapes=[pltpu.VMEM((ts + hb, d), F32), pltpu.VMEM((ts, d), F32),
                        pltpu.VMEM((ts + hb, d), F32), pltpu.VMEM((ts + hb, d), F32)],
        compiler_params=_params(("arbitrary",)),
    )(dp, dp, x, g, dres)


def _pad_rows(w, rows):
    pad = [(0, 0)] * (w.ndim - 2) + [(0, rows - w.shape[-2]), (0, 0)]
    return jnp.pad(w, pad)


def _ffn_layer_fwd(x, nf, w8, wc8, wd4, tag):
    h, up8 = _ffn_up(x, nf, w8, name=f"ffn{tag}_up")
    act4, ug4, uv4 = _ffn_mid(up8, wc8, name=f"ffn{tag}_mid")
    x_out = _ffn_down(act4, wd4, x, name=f"ffn{tag}_down")
    return x_out, (h, up8, ug4, uv4, act4)


def _ffn_layer_bwd(d, db, x, nf, w8, wc8, wd4, saved, tag, grads_ready):
    h, up8, ug4, uv4, act4 = saved
    dact4 = _ffn_dact(db, wd4, name=f"ffn{tag}_dact")
    dwd4 = _ffn_dwdown(act4, db, name=f"ffn{tag}_dwdown")
    dupg, dupv, dwg, dwv = _ffn_midbwd(up8, ug4, uv4, dact4, wc8, name=f"ffn{tag}_midbwd")
    dw8 = _ffn_dwup(h, dupg, dupv, name=f"ffn{tag}_dwup")
    sent = grads_ready(f"ffn{tag}", {f"w_up{tag}": dw8, f"w_down{tag}": dwd4})
    dx, dxb, dnf = _ffn_dh(dupg, dupv, w8, x, nf, d, sent, name=f"ffn{tag}_dh")
    dwc8 = jnp.concatenate([dwg, dwv], axis=0)[:, :3]
    return dx, dxb, dnf, dwc8


def _local_step(x, target, weights, grads_ready):
    w0 = weights("mix0", x)
    wa = _pad_rows(w0["conv_a"], 32)
    wb = _pad_rows(w0["conv_b"], 8)
    wc = [_pad_rows(w0["conv_ffn"][l], 8) for l in range(2)]
    h0, z = _rms_mm(x, w0["norm_mix_even"], w0["w_in_t"], name="mix0_in", out_dtype=F32, tn=1280)
    ab, ca = _mix0_fwd(z, wa, w0["ln_a_g"], w0["ln_a_b"], wb, name="mix0_mid")
    x1 = _mm(ab, w0["w_out"], add=x, name="mix0_out", tm=1024, tn=1024)
    w1 = weights("ffn0", x1)
    x2, ffn0 = _ffn_layer_fwd(x1, w0["norm_ffn"][0:1], w1["w_up"], wc[0], w1["w_down"], 0)
    w2 = weights("ffn1", x2)
    p = _pool_fwd(x2, w0["norm_mix_odd"], name="pool_mid")
    x3, yu = _pool_mm_fwd(p, w2["w_pool"], w0["pool_scale"], x2, name="pool_mm")
    x4, ffn1 = _ffn_layer_fwd(x3, w0["norm_ffn"][1:2], w2["w_up"], wc[1], w2["w_down"], 1)
    loss, d4, d4b, g_norm_final = _final_loss(x4, w0["norm_final"], target, name="final_loss")

    d3, d3b, g_nf1, g_wc1 = _ffn_layer_bwd(
        d4, d4b, x3, w0["norm_ffn"][1:2], w2["w_up"], wc[1], w2["w_down"], ffn1, 1, grads_ready)
    dyc, dp, g_scale = _pool_mm_bwd(d3, w2["w_pool"], w0["pool_scale"], yu, name="pool_mm_bwd")
    g_pool = _pool_dw(p, dyc, name="pool_dw")
    d2, d2b, g_nmo = _pool_bwd(dp, x2, w0["norm_mix_odd"], d3, name="pool_midbwd")
    d1, d1b, g_nf0, g_wc0 = _ffn_layer_bwd(
        d2, d2b, x1, w0["norm_ffn"][0:1], w1["w_up"], wc[0], w1["w_down"], ffn0, 0, grads_ready)
    dab = _mm(d1b, w0["w_out"], tb=True, name="mix0_dab", tm=1024, tn=1024)
    g_out = _mm(ab, d1b, ta=True, out_dtype=BF16, name="mix0_dwout", tm=1024, tn=512)
    dz, g_wa, g_wb, g_lg, g_lb = _mix0_bwd(z, ca, dab, wa, w0["ln_a_g"], w0["ln_a_b"], wb, name="mix0_midbwd")
    g_in = _mm(dz, h0, ta=True, out_dtype=BF16, name="mix0_dwin", tm=512, tn=1024)
    sent_mix = grads_ready("mix0", {"w_in": g_in, "w_out": g_out, "w_pool": g_pool})
    dx, g_nme = _mm_rms_bwd(dz, w0["w_in_t"], x, w0["norm_mix_even"], d1, sent_mix, name="mix0_dh")

    small = {
        "norm_mix_even": g_nme, "conv_a": g_wa[:A_TAPS], "ln_a_g": g_lg, "ln_a_b": g_lb, "conv_b": g_wb[:3],
        "norm_mix_odd": g_nmo, "pool_scale": g_scale, "norm_ffn": jnp.concatenate([g_nf0, g_nf1], axis=0),
        "conv_ffn": [g_wc0, g_wc1], "norm_final": g_norm_final,
    }
    return loss[0, 0], dx, small


def _my_pos():
    return lax.axis_index("x"), lax.axis_index("y"), lax.axis_index("c")


def _flip(pos, r):
    x, y, c = pos
    return (1 - x if r & 4 else x, 1 - y if r & 2 else y, 1 - c if r & 1 else c)


def _dev_index(pos):
    return 4 * pos[0] + 2 * pos[1] + pos[2]


_HBM = pl.BlockSpec(memory_space=pltpu.HBM)
_SEM = pl.BlockSpec(memory_space=pltpu.SEMAPHORE)
_EFFECT = pltpu.SideEffectType.DATAFLOW_SIDE_EFFECTING


def _exchange_copy(ins, lands, send_sems, recv_sems, scatter, pos, a, r, receiving):
    me = _dev_index(pos)
    peer = _flip(pos, r)
    dest = _dev_index(pos) if receiving else _dev_index(peer)
    src = ins[a].at[dest] if scatter[a] else ins[a]
    slot = _dev_index(peer) if receiving else me
    return pltpu.make_async_remote_copy(
        src_ref=src, dst_ref=lands[a].at[slot], send_sem=send_sems.at[a * (N_DEV - 1) + r - 1],
        recv_sem=recv_sems.at[a * (N_DEV - 1) + r - 1],
        device_id=peer, device_id_type=pl.DeviceIdType.MESH)


ALL_PEERS = tuple(range(1, N_DEV))
CHIP_PEERS = (1, 2, 4, 6)
FORWARDED = (2, 4, 6)


def _exchange_start(arrays, scatter, after, *, name, peers=ALL_PEERS):
    n = len(arrays)
    me = _dev_index(_my_pos())
    lands = []
    for arr, sc in zip(arrays, scatter):
        own = lax.dynamic_index_in_dim(arr, me, 0, keepdims=True) if sc else arr[None]
        shape = arr.shape if sc else (N_DEV,) + arr.shape
        lands.append(lax.dynamic_update_slice(lax.empty(shape, arr.dtype), own, (me,) + (0,) * (len(shape) - 1)))

    def body(*refs):
        ins, lnd = refs[:n], refs[n:2 * n]
        send_sems, recv_sems = refs[2 * n + 1], refs[2 * n + 2]
        token = refs[-1]
        pos = _my_pos()
        for a in range(n):
            for r in peers:
                _exchange_copy(ins, lnd, send_sems, recv_sems, scatter, pos, a, r, receiving=False).start()
        token[...] = jnp.zeros_like(token)

    bufs = [pltpu.with_memory_space_constraint(t, pltpu.HBM) for t in list(arrays) + lands]
    sems = pltpu.SemaphoreType.DMA((n * (N_DEV - 1),))
    res = pl.pallas_call(
        body, name=name,
        out_shape=(sems, sems, *[pltpu.HBM(t.shape, t.dtype) for t in bufs], jax.ShapeDtypeStruct((8, 128), F32)),
        in_specs=[_HBM] * (2 * n) + [pl.BlockSpec(memory_space=pl.ANY)],
        out_specs=(_SEM, _SEM, *[_HBM] * (2 * n), pl.BlockSpec(memory_space=pltpu.VMEM)),
        input_output_aliases={i: 2 + i for i in range(2 * n)},
        compiler_params=pltpu.CompilerParams(has_side_effects=_EFFECT),
    )(*bufs, after)
    return res[0], res[1], list(res[2:2 + n]), list(res[2 + n:2 + 2 * n]), res[-1]


def _exchange_wait(started, scatter, after, *, name, peers=ALL_PEERS):
    send_sems, recv_sems, arrays, lands, _ = started
    n = len(arrays)

    def body(*refs):
        ins, lnd = refs[:n], refs[n:2 * n]
        send_sems, recv_sems = refs[2 * n], refs[2 * n + 1]
        pos = _my_pos()
        for a in range(n):
            for r in peers:
                _exchange_copy(ins, lnd, send_sems, recv_sems, scatter, pos, a, r, receiving=False).wait_send()
                _exchange_copy(ins, lnd, send_sems, recv_sems, scatter, pos, a, r, receiving=True).wait_recv()

    bufs = list(arrays) + list(lands)
    after = list(after) if isinstance(after, (list, tuple)) else [after]
    res = pl.pallas_call(
        body, name=name,
        out_shape=tuple(pltpu.HBM(t.shape, t.dtype) for t in bufs),
        in_specs=[_HBM] * (2 * n) + [_SEM, _SEM] + [pl.BlockSpec(memory_space=pl.ANY)] * len(after),
        out_specs=tuple([_HBM] * (2 * n)),
        input_output_aliases={i: i for i in range(2 * n)},
        compiler_params=pltpu.CompilerParams(has_side_effects=_EFFECT),
    )(*bufs, send_sems, recv_sems, *after)
    return list(res[n:])


def _forward_copy(lands, send_sems, recv_sems, pos, a, q, receiving):
    slot = _dev_index(_flip(pos, q ^ 1 if receiving else q))
    idx = a * len(FORWARDED) + FORWARDED.index(q)
    return pltpu.make_async_remote_copy(
        src_ref=lands[a].at[slot], dst_ref=lands[a].at[slot], send_sem=send_sems.at[idx], recv_sem=recv_sems.at[idx],
        device_id=_flip(pos, 1), device_id_type=pl.DeviceIdType.MESH)


def _forward_start(lands, after, *, name):
    n = len(lands)

    def body(*refs):
        lnd = refs[:n]
        send_sems, recv_sems = refs[n + 1], refs[n + 2]
        token = refs[-1]
        pos = _my_pos()
        for a in range(n):
            for q in FORWARDED:
                _forward_copy(lnd, send_sems, recv_sems, pos, a, q, receiving=False).start()
        token[...] = jnp.zeros_like(token)

    sems = pltpu.SemaphoreType.DMA((n * len(FORWARDED),))
    res = pl.pallas_call(
        body, name=name,
        out_shape=(sems, sems, *[pltpu.HBM(t.shape, t.dtype) for t in lands], jax.ShapeDtypeStruct((8, 128), F32)),
        in_specs=[_HBM] * n + [pl.BlockSpec(memory_space=pl.ANY)],
        out_specs=(_SEM, _SEM, *[_HBM] * n, pl.BlockSpec(memory_space=pltpu.VMEM)),
        input_output_aliases={i: 2 + i for i in range(n)},
        compiler_params=pltpu.CompilerParams(has_side_effects=_EFFECT),
    )(*lands, after)
    return res[0], res[1], list(res[2:2 + n]), res[-1]


def _forward_wait(forwarded, after, *, name):
    send_sems, recv_sems, lands, _ = forwarded
    n = len(lands)

    def body(*refs):
        lnd = refs[:n]
        send_sems, recv_sems = refs[n], refs[n + 1]
        pos = _my_pos()
        for a in range(n):
            for q in FORWARDED:
                _forward_copy(lnd, send_sems, recv_sems, pos, a, q, receiving=False).wait_send()
                _forward_copy(lnd, send_sems, recv_sems, pos, a, q, receiving=True).wait_recv()

    res = pl.pallas_call(
        body, name=name,
        out_shape=tuple(pltpu.HBM(t.shape, t.dtype) for t in lands),
        in_specs=[_HBM] * n + [_SEM, _SEM, pl.BlockSpec(memory_space=pl.ANY)],
        out_specs=tuple([_HBM] * n),
        input_output_aliases={i: i for i in range(n)},
        compiler_params=pltpu.CompilerParams(has_side_effects=_EFFECT),
    )(*lands, send_sems, recv_sems, after)
    return list(res)


def _adamw_update(p_ref, w_ref, m_ref, v_ref, g_ref, d_ref, mo_ref, vo_ref):
    g = p_ref[0].astype(F32)
    for k in range(1, N_DEV):
        g = g + p_ref[k].astype(F32)
    mn = ADAM_B1 * m_ref[...] + (1.0 - ADAM_B1) * g
    vn = ADAM_B2 * v_ref[...] + (1.0 - ADAM_B2) * (g * g)
    m_hat = mn / (1.0 - ADAM_B1 ** ADAM_STEP)
    v_hat = vn / (1.0 - ADAM_B2 ** ADAM_STEP)
    g_ref[...] = g
    d_ref[...] = -ADAM_LR * (m_hat / (jnp.sqrt(v_hat) + ADAM_EPS) + ADAM_WD * w_ref[...])
    mo_ref[...] = mn
    vo_ref[...] = vn


def _adamw_small(parts, ws, ms, vs, *, name):
    n = len(ws)

    def body(*refs):
        ins, outs = refs[:4 * n], refs[4 * n:]
        for j in range(n):
            _adamw_update(ins[j], ins[n + j], ins[2 * n + j], ins[3 * n + j], *outs[4 * j:4 * j + 4])

    return pl.pallas_call(
        body, name=name,
        out_shape=[jax.ShapeDtypeStruct(w.shape, F32) for w in ws for _ in range(4)],
    )(*parts, *ws, *ms, *vs)


def _adamw(parts, w, m, v, *, name, tr):
    nl, r, c = w.shape
    assert len(parts) == nl and r % tr == 0
    n_i = r // tr

    def body(*refs):
        p_refs = refs[:nl]
        w_ref, m_ref, v_ref = refs[nl:nl + 3]

        def update(p_ref):
            _adamw_update(p_ref, w_ref, m_ref, v_ref, *refs[nl + 3:])

        if nl == 1:
            update(p_refs[0])
        else:
            for layer in range(nl):
                pl.when(pl.program_id(0) == layer)(lambda p_ref=p_refs[layer]: update(p_ref))

    def parts_spec(layer):
        def index(l, i):
            return (0, jnp.where(l < layer, 0, jnp.where(l > layer, n_i - 1, i)), 0)
        return pl.BlockSpec((N_DEV, tr, c), index)

    blk = pl.BlockSpec((None, tr, c), lambda l, i: (l, i, 0))
    return pl.pallas_call(
        body, name=name, grid=(nl, n_i),
        in_specs=[parts_spec(layer) for layer in range(nl)] + [blk, blk, blk],
        out_specs=[blk] * 4, out_shape=[jax.ShapeDtypeStruct((nl, r, c), F32)] * 4,
        compiler_params=_params(("arbitrary", "arbitrary")),
    )(*parts, w, m, v)


def _pack(parts, lead=()):
    flat = jnp.concatenate([p.reshape(lead + (-1,)) for p in parts], axis=-1)
    n = flat.shape[-1]
    rows = -(-n // (8 * 128)) * 8
    flat = jnp.pad(flat, [(0, 0)] * len(lead) + [(0, rows * 128 - n)])
    return flat.reshape(lead + (rows, 128))


def _to_dev_major(g, axis):
    shp = g.shape
    g = g.reshape(shp[:axis] + (N_DEV, shp[axis] // N_DEV) + shp[axis + 1:])
    return jnp.moveaxis(g, axis, 0)


def _from_dev_major(g, axis):
    g = jnp.moveaxis(g, 0, axis)
    shp = g.shape
    return g.reshape(shp[:axis] + (shp[axis] * shp[axis + 1],) + shp[axis + 2:])


SMALL_SHARDED = ("conv_a", "conv_b", "norm_mix_odd", "pool_scale", "conv_ffn_w")
SMALL_REPLICATED = ("norm_mix_even", "ln_a_g", "ln_a_b", "norm_ffn", "norm_final")
BIG = {"w_in": ("w_in", 0, 320), "w_out": ("w_out", 0, 128), "w_pool": ("w_pool", 0, 128),
       "w_up0": ("w_up", 0, 352), "w_up1": ("w_up", 1, 352),
       "w_down0": ("w_down", 0, 352), "w_down1": ("w_down", 1, 352)}
COLUMN_SHARDED = ("w_in", "w_up")


def kernel(x, norm_mix_even, w_in, conv_a, ln_a_g, ln_a_b, conv_b, w_out, norm_mix_odd, w_pool, pool_scale, norm_ffn, w_up, conv_ffn_w, w_down, norm_final, loss_target, m_norm_mix_even, m_w_in, m_conv_a, m_ln_a_g, m_ln_a_b, m_conv_b, m_w_out, m_norm_mix_odd, m_w_pool, m_pool_scale, m_norm_ffn, m_w_up, m_conv_ffn_w, m_w_down, m_norm_final, v_norm_mix_even, v_w_in, v_conv_a, v_ln_a_g, v_ln_a_b, v_conv_b, v_w_out, v_norm_mix_odd, v_w_pool, v_pool_scale, v_norm_ffn, v_w_up, v_conv_ffn_w, v_w_down, v_norm_final):
    names = ("norm_mix_even", "w_in", "conv_a", "ln_a_g", "ln_a_b", "conv_b", "w_out", "norm_mix_odd", "w_pool",
             "pool_scale", "norm_ffn", "w_up", "conv_ffn_w", "w_down", "norm_final")
    wts = dict(zip(names, (norm_mix_even, w_in, conv_a, ln_a_g, ln_a_b, conv_b, w_out, norm_mix_odd, w_pool,
                           pool_scale, norm_ffn, w_up, conv_ffn_w, w_down, norm_final)))
    mom = dict(zip(names, (m_norm_mix_even, m_w_in, m_conv_a, m_ln_a_g, m_ln_a_b, m_conv_b, m_w_out, m_norm_mix_odd,
                           m_w_pool, m_pool_scale, m_norm_ffn, m_w_up, m_conv_ffn_w, m_w_down, m_norm_final)))
    var = dict(zip(names, (v_norm_mix_even, v_w_in, v_conv_a, v_ln_a_g, v_ln_a_b, v_conv_b, v_w_out, v_norm_mix_odd,
                           v_w_pool, v_pool_scale, v_norm_ffn, v_w_up, v_conv_ffn_w, v_w_down, v_norm_final)))
    d = x.shape[-1]

    def shard3d(t, param):
        a = t[param]
        if param in COLUMN_SHARDED:
            return jnp.swapaxes(a, 1, 2)
        return a.reshape(a.shape[0], -1, a.shape[-1])

    def unshard3d(a, param):
        if param in COLUMN_SHARDED:
            return jnp.swapaxes(a, 1, 2)
        return a.reshape(wts[param].shape)

    def shard2d(t, key):
        param, layer, _ = BIG[key]
        return shard3d(t, param)[layer]

    small_w = _pack([wts[k] for k in SMALL_SHARDED])
    bf = {k: shard2d(wts, k).astype(BF16) for k in BIG}
    gather_groups = {"mix0": ("w_in", "w_out", "small"), "ffn0": ("w_up0", "w_down0"),
                     "ffn1": ("w_pool", "w_up1", "w_down1")}
    order = list(gather_groups)
    started = {}

    def start_gather(grp, after):
        arrs = [small_w if k == "small" else bf[k] for k in gather_groups[grp]]
        started[grp] = _exchange_start(arrs, [False] * len(arrs), after, name=f"gather_{grp}_start", peers=CHIP_PEERS)

    start_gather(order[0], small_w)

    def weights(grp, after):
        keys = gather_groups[grp]
        lands = _exchange_wait(started[grp], [False] * len(keys), after, name=f"gather_{grp}_wait", peers=CHIP_PEERS)
        forwarded = _forward_start(lands, small_w, name=f"gather_{grp}_forward")
        if grp != order[-1]:
            start_gather(order[order.index(grp) + 1], forwarded[-1])
            behind = started[order[order.index(grp) + 1]][-1]
        else:
            behind = forwarded[-1]
        gw = dict(zip(keys, _forward_wait(forwarded, behind, name=f"gather_{grp}_forward_wait")))
        if grp == "ffn0":
            return {"w_up": gw["w_up0"], "w_down": gw["w_down0"].reshape(N_PAIR, -1, d)}
        if grp == "ffn1":
            return {"w_up": gw["w_up1"], "w_down": gw["w_down1"].reshape(N_PAIR, -1, d),
                    "w_pool": _from_dev_major(gw["w_pool"].reshape(N_DEV, len(POOL_WINDOWS), -1, POOL_GROUP), 1)}
        per_dev = gw["small"].reshape(N_DEV, -1)
        sizes = [wts[k].size for k in SMALL_SHARDED]
        offs = [sum(sizes[:i]) for i in range(len(sizes))]
        small_full = {k: per_dev[:, o:o + n_].reshape((N_DEV,) + wts[k].shape)
                      for k, o, n_ in zip(SMALL_SHARDED, offs, sizes)}
        return {
            "norm_mix_even": norm_mix_even, "ln_a_g": ln_a_g, "ln_a_b": ln_a_b, "norm_ffn": norm_ffn,
            "norm_final": norm_final[None],
            "w_in_t": gw["w_in"].reshape(-1, d),
            "w_out": gw["w_out"].reshape(-1, d),
            "conv_a": _from_dev_major(small_full["conv_a"][:, 0], 1),
            "conv_b": _from_dev_major(small_full["conv_b"][:, 0], 1),
            "norm_mix_odd": _from_dev_major(small_full["norm_mix_odd"], 1),
            "pool_scale": _from_dev_major(small_full["pool_scale"], 1),
            "conv_ffn": [small_full["conv_ffn_w"][:, l] for l in range(2)],
        }

    def dev_major(k, g):
        if k == "w_pool":
            return _to_dev_major(g, 1).reshape(N_DEV, -1, POOL_GROUP)
        return g.reshape(N_DEV, -1, g.shape[-1])

    sent = {}

    def grads_ready(grp, grads):
        keys = tuple(grads)
        parts = [dev_major(k, grads[k]) for k in keys]
        sent[grp] = (keys, _exchange_start(parts, [True] * len(keys), small_w, name=f"grads_{grp}_start"))
        return sent[grp][1][-1]

    loss, dx, g = _local_step(x[0], loss_target[0], weights, grads_ready)

    def small2d(a):
        return a.reshape(-1, a.shape[-1])

    small_keys = SMALL_SHARDED + SMALL_REPLICATED
    small_parts = [_to_dev_major(g["conv_a"], 1), _to_dev_major(g["conv_b"], 1),
                   _to_dev_major(g["norm_mix_odd"], 1), _to_dev_major(g["pool_scale"], 1),
                   jnp.stack(g["conv_ffn"], axis=1).reshape(N_DEV, -1, w_up.shape[-1])]
    small_parts += [g[k] for k in SMALL_REPLICATED]
    small_scatter = [True] * len(SMALL_SHARDED) + [False] * len(SMALL_REPLICATED)
    sent["small"] = (small_keys, _exchange_start(small_parts, small_scatter, dx, name="grads_small_start"))

    landed, out = {}, {}

    def wait_grads(grp, after, scat=None):
        keys, st = sent[grp]
        scat = [True] * len(keys) if scat is None else scat
        landed.update(zip(keys, _exchange_wait(st, scat, after, name=f"grads_{grp}_wait")))

    def update(param, keys):
        res = _adamw([landed[k] for k in keys], shard3d(wts, param), shard3d(mom, param), shard3d(var, param),
                     name=f"adamw_{param}", tr=BIG[keys[0]][2])
        out[param] = [unshard3d(t, param) for t in res]

    wait_grads("ffn1", sent["small"][1][-1])
    wait_grads("ffn0", landed["w_up1"])
    update("w_up", ("w_up0", "w_up1"))
    update("w_down", ("w_down0", "w_down1"))
    wait_grads("mix0", [out["w_up"][1], out["w_down"][1]])
    update("w_in", ("w_in",))
    update("w_out", ("w_out",))
    update("w_pool", ("w_pool",))
    wait_grads("small", out["w_pool"][1], small_scatter)
    res = _adamw_small([landed[k] for k in small_keys], [small2d(wts[k]) for k in small_keys],
                       [small2d(mom[k]) for k in small_keys], [small2d(var[k]) for k in small_keys],
                       name="adamw_small")
    for j, k in enumerate(small_keys):
        out[k] = [t.reshape(wts[k].shape) for t in res[4 * j:4 * j + 4]]

    loss = lax.psum(loss, ("x", "y", "c"))
    return (loss, dx[None], *[out[k][0] for k in names], *[out[k][1] for k in names],
            *[out[k][2] for k in names], *[out[k][3] for k in names])
```

```python
import jax
import jax.numpy as jnp
from jax import lax
from jax.experimental import pallas as pl
from jax.experimental.pallas import tpu as pltpu

F32 = jnp.float32
BF16 = jnp.bfloat16

RMS_EPS = 1e-6
LN_EPS = 1e-5
ADAM_LR = 0.001
ADAM_B1 = 0.9
ADAM_B2 = 0.999
ADAM_EPS = 1e-08
ADAM_WD = 0.01
ADAM_STEP = 10

N_DEV = 8
N_PAIR = N_DEV // 2
A_WIDTH = 512
A_TAPS = 31
POOL_WINDOWS = (2, 4, 8, 16)
POOL_GROUP = 256
HALO_A = 32
HALO_S = 16
VMEM_LIMIT = 56 * 1024 * 1024


def _params(sem, vmem=VMEM_LIMIT):
    return pltpu.CompilerParams(dimension_semantics=sem, vmem_limit_bytes=vmem)


def _sigmoid(x):
    return 0.5 * jnp.tanh(0.5 * x) + 0.5


def _prev_blk(i, ts, hb):
    return jnp.maximum(i * (ts // hb) - 1, 0)


def _next_blk(i, ts, hb, s):
    return jnp.minimum((i + 1) * (ts // hb), s // hb - 1)


def _mm(a, b, *, name, ta=False, tb=False, add=None, out_dtype=F32, tm=512, tn=512, tk=None):
    m, k = (a.shape[1], a.shape[0]) if ta else a.shape
    n = b.shape[0] if tb else b.shape[1]
    tk = k if tk is None else tk
    tm, tn, tk = min(tm, m), min(tn, n), min(tk, k)
    assert m % tm == 0 and n % tn == 0 and k % tk == 0, (name, m, n, k, tm, tn, tk)
    nk = k // tk
    dims = (((0,) if ta else (1,), (1,) if tb else (0,)), ((), ()))
    n_in = 2 + (add is not None)

    def body(*refs):
        a_ref, b_ref = refs[0], refs[1]
        add_ref = refs[2] if add is not None else None
        o_ref = refs[n_in]
        part = lax.dot_general(a_ref[...].astype(BF16), b_ref[...].astype(BF16), dims, preferred_element_type=F32)

        def finish(r):
            if add_ref is not None:
                r = r + add_ref[...]
            o_ref[...] = r.astype(out_dtype)

        if nk == 1:
            finish(part)
            return
        acc_ref = refs[-1]
        kk = pl.program_id(2)

        @pl.when(kk == 0)
        def _():
            acc_ref[...] = part

        @pl.when(kk > 0)
        def _():
            acc_ref[...] += part

        @pl.when(kk == nk - 1)
        def _():
            finish(acc_ref[...])

    a_spec = pl.BlockSpec((tk, tm), lambda i, j, kk: (kk, i)) if ta else pl.BlockSpec((tm, tk), lambda i, j, kk: (i, kk))
    b_spec = pl.BlockSpec((tn, tk), lambda i, j, kk: (j, kk)) if tb else pl.BlockSpec((tk, tn), lambda i, j, kk: (kk, j))
    in_specs = [a_spec, b_spec]
    args = [a, b]
    if add is not None:
        in_specs.append(pl.BlockSpec((tm, tn), lambda i, j, kk: (i, j)))
        args.append(add)
    return pl.pallas_call(
        body, name=name, grid=(m // tm, n // tn, nk),
        in_specs=in_specs, out_specs=pl.BlockSpec((tm, tn), lambda i, j, kk: (i, j)),
        out_shape=jax.ShapeDtypeStruct((m, n), out_dtype),
        scratch_shapes=[pltpu.VMEM((tm, tn), F32)] if nk > 1 else [],
        compiler_params=_params(("parallel", "parallel", "arbitrary")),
    )(*args)


def _rms_rows(xv, gv):
    return xv * lax.rsqrt(jnp.mean(xv * xv, axis=-1, keepdims=True) + RMS_EPS) * gv


_NT = (((1,), (1,)), ((), ()))
_TN = (((0,), (0,)), ((), ()))


def _rms_mm(x, g, wt, *, name, out_dtype, tm=1024, tn=512):
    s, d = x.shape
    n = wt.shape[0]
    tm = min(tm, s)
    assert s % tm == 0 and n % tn == 0

    def body(x_ref, g_ref, w_ref, h_ref, z_ref, hs_ref):
        @pl.when(pl.program_id(1) == 0)
        def _():
            hv = _rms_rows(x_ref[...], g_ref[...]).astype(BF16)
            hs_ref[...] = hv
            h_ref[...] = hv

        z_ref[...] = lax.dot_general(hs_ref[...], w_ref[...], _NT, preferred_element_type=F32).astype(out_dtype)

    return pl.pallas_call(
        body, name=name, grid=(s // tm, n // tn),
        in_specs=[pl.BlockSpec((tm, d), lambda i, j: (i, 0)),
                  pl.BlockSpec((1, d), lambda i, j: (0, 0)),
                  pl.BlockSpec((tn, d), lambda i, j: (j, 0))],
        out_specs=[pl.BlockSpec((tm, d), lambda i, j: (i, 0)),
                   pl.BlockSpec((tm, tn), lambda i, j: (i, j))],
        out_shape=[jax.ShapeDtypeStruct((s, d), BF16), jax.ShapeDtypeStruct((s, n), out_dtype)],
        scratch_shapes=[pltpu.VMEM((tm, d), BF16)],
        compiler_params=_params(("parallel", "arbitrary")),
    )(x, g, wt)


def _rms_bwd_rows(xv, gv, dh):
    r = lax.rsqrt(jnp.mean(xv * xv, axis=-1, keepdims=True) + RMS_EPS)
    xh = xv * r
    dn = dh * gv
    dx = r * (dn - xh * jnp.mean(dn * xh, axis=-1, keepdims=True))
    return dx, dh * xh


def _mm_rms_bwd(a, b, x, g, dres, dep, *, name, tm=512):
    s, k = a.shape
    d = b.shape[1]
    tm = min(tm, s)

    def body(a_ref, b_ref, x_ref, g_ref, dres_ref, dep_ref, dx_ref, dg_ref):
        @pl.when(pl.program_id(0) == 0)
        def _():
            dg_ref[...] = jnp.zeros_like(dg_ref)

        dh = jnp.dot(a_ref[...], b_ref[...], preferred_element_type=F32)
        dx, dgr = _rms_bwd_rows(x_ref[...], g_ref[...], dh)
        dx_ref[...] = dres_ref[...] + dx
        dg_ref[...] += jnp.sum(dgr, axis=0, keepdims=True)

    row = pl.BlockSpec((tm, d), lambda i: (i, 0))
    vec = pl.BlockSpec((1, d), lambda i: (0, 0))
    return pl.pallas_call(
        body, name=name, grid=(s // tm,),
        in_specs=[pl.BlockSpec((tm, k), lambda i: (i, 0)), pl.BlockSpec((k, d), lambda i: (0, 0)), row, vec, row,
                  pl.BlockSpec(memory_space=pl.ANY)],
        out_specs=[row, vec],
        out_shape=[jax.ShapeDtypeStruct((s, d), F32), jax.ShapeDtypeStruct((1, d), F32)],
        compiler_params=_params(("arbitrary",)),
    )(a, b, x, g, dres, dep)


def _ffn_down_loss(act4, wd4, x, g, target, *, name, tm=512):
    npair, s, c = act4.shape
    d = x.shape[1]
    tm = min(tm, s)

    def body(a_ref, w_ref, x_ref, g_ref, t_ref, loss_ref, dx_ref, dxb_ref, dg_ref):
        @pl.when(pl.program_id(0) == 0)
        def _():
            loss_ref[...] = jnp.zeros_like(loss_ref)
            dg_ref[...] = jnp.zeros_like(dg_ref)

        xv = x_ref[...]
        for j in range(npair):
            xv = xv + jnp.dot(a_ref[j], w_ref[j], preferred_element_type=F32)
        gv = g_ref[...]
        r = lax.rsqrt(jnp.mean(xv * xv, axis=-1, keepdims=True) + RMS_EPS)
        xh = xv * r
        err = xh * gv - t_ref[...]
        loss_ref[...] += 0.5 * jnp.sum(jnp.mean(err * err, axis=-1, keepdims=True), axis=0, keepdims=True)
        dy = err * (1.0 / d)
        dn = dy * gv
        dx = r * (dn - xh * jnp.mean(dn * xh, axis=-1, keepdims=True))
        dx_ref[...] = dx
        dxb_ref[...] = dx.astype(BF16)
        dg_ref[...] += jnp.sum(dy * xh, axis=0, keepdims=True)

    row = pl.BlockSpec((tm, d), lambda i: (i, 0))
    vec = pl.BlockSpec((1, d), lambda i: (0, 0))
    return pl.pallas_call(
        body, name=name, grid=(s // tm,),
        in_specs=[pl.BlockSpec((npair, tm, c), lambda i: (0, i, 0)),
                  pl.BlockSpec((npair, c, d), lambda i: (0, 0, 0)), row, vec, row],
        out_specs=[pl.BlockSpec((1, 1), lambda i: (0, 0)), row, row, vec],
        out_shape=[jax.ShapeDtypeStruct((1, 1), F32), jax.ShapeDtypeStruct((s, d), F32),
                   jax.ShapeDtypeStruct((s, d), BF16), jax.ShapeDtypeStruct((1, d), F32)],
        compiler_params=_params(("arbitrary",)),
    )(act4, wd4, x, g, target)


def _conv_taps(ext_ref, w_ref, n_taps, base, r0, rows, reverse=False):
    acc = None
    for k in range(n_taps):
        off = r0 + (base - k if reverse else base + k)
        term = w_ref[k:k + 1, :] * ext_ref[pl.ds(off, rows), :]
        acc = term if acc is None else acc + term
    return acc


def _shift_copies(src_ref, sh_ref, rows):
    for b in range(1, 8):
        sh_ref[b, 0:rows, :] = src_ref[pl.ds(b, rows), :]


def _shifted(src_ref, sh_ref, start, rows, off):
    a, b = divmod(off, 8)
    ref = src_ref if b == 0 else sh_ref.at[b]
    return ref[pl.ds(start + 8 * a, rows), :]


def _mix0_fwd(z, conv_a, ln_g, ln_b, conv_b, *, name, ts=256, rc=32):
    s = z.shape[0]
    c = A_WIDTH
    hb = HALO_A

    def body(z_ref, zp_ref, wa_ref, lg_ref, lb_ref, wb_ref, ab_ref, ca_ref, exta, extb, sha):
        keep = jnp.where(pl.program_id(0) > 0, 1.0, 0.0)
        zp = zp_ref[...]
        exta[0:hb, :] = zp[:, 0:c] * _sigmoid(zp[:, c:2 * c]) * keep
        extb[0:hb, :] = zp[:, 3 * c:4 * c] * zp[:, 4 * c:5 * c] * keep
        exta[hb:hb + ts, :] = z_ref[:, 0:c] * _sigmoid(z_ref[:, c:2 * c])
        extb[hb:hb + ts, :] = z_ref[:, 3 * c:4 * c] * z_ref[:, 4 * c:5 * c]
        _shift_copies(exta, sha, hb + ts - 8)
        lg = lg_ref[...]
        lb = lb_ref[...]
        for q in range(ts // rc):
            r0 = q * rc
            ca = None
            for k in range(A_TAPS):
                term = wa_ref[k:k + 1, :] * _shifted(exta, sha, r0, rc, hb - (A_TAPS - 1) + k)
                ca = term if ca is None else ca + term
            ca_ref[r0:r0 + rc, :] = ca
            mu = jnp.mean(ca, axis=-1, keepdims=True)
            xc = ca - mu
            rs = lax.rsqrt(jnp.mean(xc * xc, axis=-1, keepdims=True) + LN_EPS)
            l = xc * rs * lg + lb
            ab_ref[r0:r0 + rc, 0:c] = (l * _sigmoid(l)).astype(BF16)
            cbc = _conv_taps(extb, wb_ref, 3, hb - 2, r0, rc)
            ab_ref[r0:r0 + rc, c:2 * c] = (z_ref[r0:r0 + rc, 2 * c:3 * c] * cbc).astype(BF16)

    return pl.pallas_call(
        body, name=name, grid=(s // ts,),
        in_specs=[pl.BlockSpec((ts, 5 * c), lambda i: (i, 0)),
                  pl.BlockSpec((hb, 5 * c), lambda i: (_prev_blk(i, ts, hb), 0)),
                  pl.BlockSpec((32, c), lambda i: (0, 0)),
                  pl.BlockSpec((1, c), lambda i: (0, 0)),
                  pl.BlockSpec((1, c), lambda i: (0, 0)),
                  pl.BlockSpec((8, c), lambda i: (0, 0))],
        out_specs=[pl.BlockSpec((ts, 2 * c), lambda i: (i, 0)),
                   pl.BlockSpec((ts, c), lambda i: (i, 0))],
        out_shape=[jax.ShapeDtypeStruct((s, 2 * c), BF16), jax.ShapeDtypeStruct((s, c), F32)],
        scratch_shapes=[pltpu.VMEM((hb + ts, c), F32), pltpu.VMEM((hb + ts, c), F32),
                        pltpu.VMEM((8, hb + ts - 8, c), F32)],
        compiler_params=_params(("parallel",)),
    )(z, z, conv_a, ln_g, ln_b, conv_b)


def _mix0_bwd(z, ca, dab, conv_a, ln_g, ln_b, conv_b, *, name, ts=256, rc=32):
    s = z.shape[0]
    c = A_WIDTH
    hb = HALO_A
    ta = A_TAPS

    def body(z_ref, zp_ref, zn_ref, ca_ref, can_ref, d_ref, dn_ref, wa_ref, lg_ref, lb_ref, wb_ref,
             dz_ref, dwa_ref, dwb_ref, dlg_ref, dlb_ref, exta, extb, extdca, extdcb, shd):
        i = pl.program_id(0)
        keep_p = jnp.where(i > 0, 1.0, 0.0)
        keep_n = jnp.where(i < s // ts - 1, 1.0, 0.0)

        @pl.when(i == 0)
        def _():
            dwa_ref[...] = jnp.zeros_like(dwa_ref)
            dwb_ref[...] = jnp.zeros_like(dwb_ref)
            dlg_ref[...] = jnp.zeros_like(dlg_ref)
            dlb_ref[...] = jnp.zeros_like(dlb_ref)

        lg = lg_ref[...]
        lb = lb_ref[...]
        zp = zp_ref[...]
        exta[0:hb, :] = zp[:, 0:c] * _sigmoid(zp[:, c:2 * c]) * keep_p
        extb[0:hb, :] = zp[:, 3 * c:4 * c] * zp[:, 4 * c:5 * c] * keep_p
        exta[hb:hb + ts, :] = z_ref[:, 0:c] * _sigmoid(z_ref[:, c:2 * c])
        extb[hb:hb + ts, :] = z_ref[:, 3 * c:4 * c] * z_ref[:, 4 * c:5 * c]

        def ln_bwd(cav, dav):
            mu = jnp.mean(cav, axis=-1, keepdims=True)
            xc = cav - mu
            rs = lax.rsqrt(jnp.mean(xc * xc, axis=-1, keepdims=True) + LN_EPS)
            nv = xc * rs
            l = nv * lg + lb
            sg = _sigmoid(l)
            dl = dav * (sg * (1.0 + l * (1.0 - sg)))
            dnv = dl * lg
            dca = rs * (dnv - jnp.mean(dnv, axis=-1, keepdims=True)
                        - nv * jnp.mean(dnv * nv, axis=-1, keepdims=True))
            return dca, dl, nv

        dlg_acc = jnp.zeros((1, c), F32)
        dlb_acc = jnp.zeros((1, c), F32)
        for q in range(ts // rc):
            r0 = q * rc
            dca, dl, nv = ln_bwd(ca_ref[r0:r0 + rc, :], d_ref[r0:r0 + rc, 0:c])
            extdca[r0:r0 + rc, :] = dca
            dlg_acc = dlg_acc + jnp.sum(dl * nv, axis=0, keepdims=True)
            dlb_acc = dlb_acc + jnp.sum(dl, axis=0, keepdims=True)
            extdcb[r0:r0 + rc, :] = d_ref[r0:r0 + rc, c:2 * c] * z_ref[r0:r0 + rc, 2 * c:3 * c]
        dca_n, _, _ = ln_bwd(can_ref[...], dn_ref[:, 0:c])
        extdca[ts:ts + hb, :] = dca_n * keep_n
        extdcb[ts:ts + hb, :] = dn_ref[:, c:2 * c] * zn_ref[:, 2 * c:3 * c] * keep_n
        dlg_ref[...] += dlg_acc
        dlb_ref[...] += dlb_acc
        _shift_copies(extdca, shd, ts + hb - 8)

        for q in range(ts // rc):
            r0 = q * rc
            zr = z_ref[r0:r0 + rc, :]
            dga = None
            for k in range(ta):
                term = wa_ref[k:k + 1, :] * _shifted(extdca, shd, r0, rc, ta - 1 - k)
                dga = term if dga is None else dga + term
            sg = _sigmoid(zr[:, c:2 * c])
            dz_ref[r0:r0 + rc, 0:c] = (dga * sg).astype(BF16)
            dz_ref[r0:r0 + rc, c:2 * c] = (dga * zr[:, 0:c] * sg * (1.0 - sg)).astype(BF16)
            cbc = _conv_taps(extb, wb_ref, 3, hb - 2, r0, rc)
            dz_ref[r0:r0 + rc, 2 * c:3 * c] = (d_ref[r0:r0 + rc, c:2 * c] * cbc).astype(BF16)
            dcb = _conv_taps(extdcb, wb_ref, 3, 2, r0, rc, reverse=True)
            dz_ref[r0:r0 + rc, 3 * c:4 * c] = (dcb * zr[:, 4 * c:5 * c]).astype(BF16)
            dz_ref[r0:r0 + rc, 4 * c:5 * c] = (dcb * zr[:, 3 * c:4 * c]).astype(BF16)

        for k in range(ta):
            part = None
            for q in range(ts // rc):
                r0 = q * rc
                p = exta[hb + r0:hb + r0 + rc, :] * _shifted(extdca, shd, r0, rc, ta - 1 - k)
                for r in range(0, rc, 8):
                    part = p[r:r + 8, :] if part is None else part + p[r:r + 8, :]
            dwa_ref[k:k + 1, :] += jnp.sum(part, axis=0, keepdims=True)
        dcb_t = extdcb[0:ts, :]
        for k in range(3):
            dwb_ref[k:k + 1, :] += jnp.sum(dcb_t * extb[pl.ds(hb - 2 + k, ts), :], axis=0, keepdims=True)

    def tile(w):
        return pl.BlockSpec((ts, w), lambda i: (i, 0))

    def prev(w):
        return pl.BlockSpec((hb, w), lambda i: (_prev_blk(i, ts, hb), 0))

    def nxt(w):
        return pl.BlockSpec((hb, w), lambda i: (_next_blk(i, ts, hb, s), 0))

    def const(r, w):
        return pl.BlockSpec((r, w), lambda i: (0, 0))

    return pl.pallas_call(
        body, name=name, grid=(s // ts,),
        in_specs=[tile(5 * c), prev(5 * c), nxt(5 * c), tile(c), nxt(c), tile(2 * c), nxt(2 * c),
                  const(32, c), const(1, c), const(1, c), const(8, c)],
        out_specs=[tile(5 * c), const(32, c), const(8, c), const(1, c), const(1, c)],
        out_shape=[jax.ShapeDtypeStruct((s, 5 * c), BF16), jax.ShapeDtypeStruct((32, c), F32),
                   jax.ShapeDtypeStruct((8, c), F32), jax.ShapeDtypeStruct((1, c), F32),
                   jax.ShapeDtypeStruct((1, c), F32)],
        scratch_shapes=[pltpu.VMEM((hb + ts, c), F32), pltpu.VMEM((hb + ts, c), F32),
                        pltpu.VMEM((ts + hb, c), F32), pltpu.VMEM((ts + hb, c), F32),
                        pltpu.VMEM((8, ts + hb - 8, c), F32)],
        compiler_params=_params(("arbitrary",)),
    )(z, z, z, ca, ca, dab, dab, conv_a, ln_g, ln_b, conv_b)


def _ffn_up(x, g, w8, *, name, tm=2048):
    s, d = x.shape
    nb, c, _ = w8.shape
    tm = min(tm, s)

    def body(x_ref, g_ref, w_ref, h_ref, u_ref, hs_ref):
        @pl.when(pl.program_id(1) == 0)
        def _():
            hv = _rms_rows(x_ref[...], g_ref[...]).astype(BF16)
            hs_ref[...] = hv
            h_ref[...] = hv

        u_ref[...] = lax.dot_general(hs_ref[...], w_ref[...], _NT, preferred_element_type=F32).astype(BF16)

    return pl.pallas_call(
        body, name=name, grid=(s // tm, nb),
        in_specs=[pl.BlockSpec((tm, d), lambda i, k: (i, 0)),
                  pl.BlockSpec((1, d), lambda i, k: (0, 0)),
                  pl.BlockSpec((None, c, d), lambda i, k: (k, 0, 0))],
        out_specs=[pl.BlockSpec((tm, d), lambda i, k: (i, 0)),
                   pl.BlockSpec((None, tm, c), lambda i, k: (k, i, 0))],
        out_shape=[jax.ShapeDtypeStruct((s, d), BF16), jax.ShapeDtypeStruct((nb, s, c), BF16)],
        scratch_shapes=[pltpu.VMEM((tm, d), BF16)],
        compiler_params=_params(("parallel", "arbitrary")),
    )(x, g, w8)


def _ffn_mid(up8, wc8, *, name, ts=512, rc=32):
    nb, s, c = up8.shape
    hb = HALO_S
    ts = min(ts, s)

    def body(g_ref, gp_ref, v_ref, vp_ref, wg_ref, wv_ref, act_ref, ug_ref, uv_ref, extg, extv):
        keep = jnp.where(pl.program_id(0) > 0, 1.0, 0.0)
        extg[0:hb, :] = gp_ref[...].astype(F32) * keep
        extv[0:hb, :] = vp_ref[...].astype(F32) * keep
        extg[hb:hb + ts, :] = g_ref[...].astype(F32)
        extv[hb:hb + ts, :] = v_ref[...].astype(F32)
        for q in range(ts // rc):
            r0 = q * rc
            gg = _conv_taps(extg, wg_ref, 3, hb - 2, r0, rc)
            vv = _conv_taps(extv, wv_ref, 3, hb - 2, r0, rc)
            ug_ref[r0:r0 + rc, :] = gg.astype(BF16)
            uv_ref[r0:r0 + rc, :] = vv.astype(BF16)
            act_ref[r0:r0 + rc, :] = (gg * _sigmoid(gg) * vv).astype(BF16)

    def blk(rows, off, halo):
        if halo:
            return pl.BlockSpec((None, rows, c), lambda i, j: (j + off, _prev_blk(i, ts, hb), 0))
        return pl.BlockSpec((None, rows, c), lambda i, j: (j + off, i, 0))

    def taps(off):
        return pl.BlockSpec((None, 8, c), lambda i, j: (j + off, 0, 0))

    tile = pl.BlockSpec((None, ts, c), lambda i, j: (j, i, 0))
    out = jax.ShapeDtypeStruct((N_PAIR, s, c), BF16)
    return pl.pallas_call(
        body, name=name, grid=(s // ts, N_PAIR),
        in_specs=[blk(ts, 0, False), blk(hb, 0, True), blk(ts, N_PAIR, False), blk(hb, N_PAIR, True),
                  taps(0), taps(N_PAIR)],
        out_specs=[tile, tile, tile], out_shape=[out, out, out],
        scratch_shapes=[pltpu.VMEM((hb + ts, c), F32), pltpu.VMEM((hb + ts, c), F32)],
        compiler_params=_params(("parallel", "parallel")),
    )(up8, up8, up8, up8, wc8, wc8)


def _ffn_down(act4, wd4, x, *, name, tm=512):
    npair, s, c = act4.shape
    d = x.shape[1]
    tm = min(tm, s)

    def body(a_ref, w_ref, x_ref, o_ref):
        acc = x_ref[...]
        for j in range(npair):
            acc = acc + jnp.dot(a_ref[j], w_ref[j], preferred_element_type=F32)
        o_ref[...] = acc

    return pl.pallas_call(
        body, name=name, grid=(s // tm,),
        in_specs=[pl.BlockSpec((npair, tm, c), lambda i: (0, i, 0)),
                  pl.BlockSpec((npair, c, d), lambda i: (0, 0, 0)),
                  pl.BlockSpec((tm, d), lambda i: (i, 0))],
        out_specs=pl.BlockSpec((tm, d), lambda i: (i, 0)),
        out_shape=jax.ShapeDtypeStruct((s, d), F32),
        compiler_params=_params(("parallel",)),
    )(act4, wd4, x)


def _ffn_dact(db, wd4, *, name, tm=1024):
    s, d = db.shape
    npair, c, _ = wd4.shape
    tm = min(tm, s)

    def body(d_ref, w_ref, o_ref):
        o_ref[...] = lax.dot_general(d_ref[...], w_ref[...], (((1,), (1,)), ((), ())),
                                     preferred_element_type=F32).astype(BF16)

    return pl.pallas_call(
        body, name=name, grid=(s // tm, npair),
        in_specs=[pl.BlockSpec((tm, d), lambda i, j: (i, 0)),
                  pl.BlockSpec((None, c, d), lambda i, j: (j, 0, 0))],
        out_specs=pl.BlockSpec((None, tm, c), lambda i, j: (j, i, 0)),
        out_shape=jax.ShapeDtypeStruct((npair, s, c), BF16),
        compiler_params=_params(("parallel", "parallel")),
    )(db, wd4)


def _ffn_dwdown(act4, db, *, name):
    npair, s, c = act4.shape
    d = db.shape[1]

    def body(a_ref, d_ref, o_ref):
        o_ref[...] = lax.dot_general(a_ref[...], d_ref[...], (((0,), (0,)), ((), ())),
                                     preferred_element_type=F32).astype(BF16)

    return pl.pallas_call(
        body, name=name, grid=(npair,),
        in_specs=[pl.BlockSpec((None, s, c), lambda j: (j, 0, 0)),
                  pl.BlockSpec((s, d), lambda j: (0, 0))],
        out_specs=pl.BlockSpec((None, c, d), lambda j: (j, 0, 0)),
        out_shape=jax.ShapeDtypeStruct((npair, c, d), BF16),
        compiler_params=_params(("parallel",)),
    )(act4, db)


def _ffn_midbwd(up8, ug4, uv4, dact4, wc8, *, name, ts=512, rc=32):
    nb, s, c = up8.shape
    hb = HALO_S
    ts = min(ts, s)
    n_i = s // ts

    def body(pg_ref, pv_ref, ug_ref, ugn_ref, uv_ref, uvn_ref, d_ref, dn_ref, wg_ref, wv_ref,
             dg_ref, dv_ref, dwg_ref, dwv_ref, extdg, extdv):
        i = pl.program_id(1)
        keep_n = jnp.where(i < n_i - 1, 1.0, 0.0)

        @pl.when(i == 0)
        def _():
            dwg_ref[...] = jnp.zeros_like(dwg_ref)
            dwv_ref[...] = jnp.zeros_like(dwv_ref)

        def du_rows(r0, rows, gg, vv, dav):
            sg = _sigmoid(gg)
            extdg[r0:r0 + rows, :] = dav * vv * (sg * (1.0 + gg * (1.0 - sg)))
            extdv[r0:r0 + rows, :] = dav * (gg * sg)

        for q in range(ts // rc):
            rows = slice(q * rc, q * rc + rc)
            du_rows(q * rc, rc, ug_ref[rows, :].astype(F32), uv_ref[rows, :].astype(F32),
                    d_ref[rows, :].astype(F32))
        du_rows(ts, 8, ugn_ref[0:8, :].astype(F32), uvn_ref[0:8, :].astype(F32),
                dn_ref[0:8, :].astype(F32) * keep_n)

        def fold(p):
            acc = p[0:8, :]
            for r in range(8, rc, 8):
                acc = acc + p[r:r + 8, :]
            return acc

        for extd, p_ref, w_ref, out_ref, dw_ref in ((extdg, pg_ref, wg_ref, dg_ref, dwg_ref),
                                                    (extdv, pv_ref, wv_ref, dv_ref, dwv_ref)):
            part = [None, None, None]
            for q in range(ts // rc):
                r0 = q * rc
                pre = p_ref[r0:r0 + rc, :].astype(F32)
                dup = None
                for k in range(3):
                    sh = extd[pl.ds(r0 + 2 - k, rc), :]
                    term = w_ref[k:k + 1, :] * sh
                    dup = term if dup is None else dup + term
                    prod = fold(pre * sh)
                    part[k] = prod if part[k] is None else part[k] + prod
                out_ref[r0:r0 + rc, :] = dup.astype(BF16)
            for k in range(3):
                dw_ref[k:k + 1, :] += jnp.sum(part[k], axis=0, keepdims=True)

    def blk(off, nxt):
        if nxt:
            return pl.BlockSpec((None, hb, c), lambda j, i: (j + off, _next_blk(i, ts, hb, s), 0))
        return pl.BlockSpec((None, ts, c), lambda j, i: (j + off, i, 0))

    def taps(off):
        return pl.BlockSpec((None, 8, c), lambda j, i: (j + off, 0, 0))

    tile = blk(0, False)
    acc = pl.BlockSpec((None, 8, c), lambda j, i: (j, 0, 0))
    return pl.pallas_call(
        body, name=name, grid=(N_PAIR, n_i),
        in_specs=[tile, blk(N_PAIR, False), tile, blk(0, True), tile, blk(0, True), tile, blk(0, True),
                  taps(0), taps(N_PAIR)],
        out_specs=[tile, tile, acc, acc],
        out_shape=[jax.ShapeDtypeStruct((N_PAIR, s, c), BF16), jax.ShapeDtypeStruct((N_PAIR, s, c), BF16),
                   jax.ShapeDtypeStruct((N_PAIR, 8, c), F32), jax.ShapeDtypeStruct((N_PAIR, 8, c), F32)],
        scratch_shapes=[pltpu.VMEM((ts + 8, c), F32), pltpu.VMEM((ts + 8, c), F32)],
        compiler_params=_params(("parallel", "arbitrary")),
    )(up8, up8, ug4, ug4, uv4, uv4, dact4, dact4, wc8, wc8)


def _ffn_dh(dupg, dupv, w8, x, g, dres, dep, *, name, tm=512):
    npair, s, c = dupg.shape
    d = x.shape[1]
    tm = min(tm, s)

    def body(dg_ref, dv_ref, w_ref, x_ref, g_ref, dres_ref, dep_ref, dx_ref, dxb_ref, dgain_ref):
        @pl.when(pl.program_id(0) == 0)
        def _():
            dgain_ref[...] = jnp.zeros_like(dgain_ref)

        dh = None
        for j in range(npair):
            for src, k in ((dg_ref, j), (dv_ref, j + npair)):
                part = jnp.dot(src[j], w_ref[k], preferred_element_type=F32)
                dh = part if dh is None else dh + part
        dx, dgr = _rms_bwd_rows(x_ref[...], g_ref[...], dh)
        dx = dres_ref[...] + dx
        dx_ref[...] = dx
        dxb_ref[...] = dx.astype(BF16)
        dgain_ref[...] += jnp.sum(dgr, axis=0, keepdims=True)

    row = pl.BlockSpec((tm, d), lambda i: (i, 0))
    vec = pl.BlockSpec((1, d), lambda i: (0, 0))
    dup = pl.BlockSpec((npair, tm, c), lambda i: (0, i, 0))
    return pl.pallas_call(
        body, name=name, grid=(s // tm,),
        in_specs=[dup, dup, pl.BlockSpec((2 * npair, c, d), lambda i: (0, 0, 0)), row, vec, row,
                  pl.BlockSpec(memory_space=pl.ANY)],
        out_specs=[row, row, vec],
        out_shape=[jax.ShapeDtypeStruct((s, d), F32), jax.ShapeDtypeStruct((s, d), BF16),
                   jax.ShapeDtypeStruct((1, d), F32)],
        compiler_params=_params(("arbitrary",)),
    )(dupg, dupv, w8, x, g, dres, dep)


def _ffn_dwup(h, dupg, dupv, *, name, tm=512):
    npair, s, c = dupg.shape
    d = h.shape[1]

    def body(h_ref, dg_ref, dv_ref, o_ref):
        k = pl.program_id(1)

        @pl.when(k < npair)
        def _():
            o_ref[...] = lax.dot_general(dg_ref[...], h_ref[...], _TN, preferred_element_type=F32).astype(BF16)

        @pl.when(k >= npair)
        def _():
            o_ref[...] = lax.dot_general(dv_ref[...], h_ref[...], _TN, preferred_element_type=F32).astype(BF16)

    return pl.pallas_call(
        body, name=name, grid=(d // tm, 2 * npair),
        in_specs=[pl.BlockSpec((s, tm), lambda m, k: (0, m)),
                  pl.BlockSpec((None, s, c), lambda m, k: (jnp.minimum(k, npair - 1), 0, 0)),
                  pl.BlockSpec((None, s, c), lambda m, k: (jnp.maximum(k - npair, 0), 0, 0))],
        out_specs=pl.BlockSpec((None, c, tm), lambda m, k: (k, 0, m)),
        out_shape=jax.ShapeDtypeStruct((2 * npair, c, d), BF16),
        compiler_params=_params(("parallel", "arbitrary")),
    )(h, dupg, dupv)


def _pool_counts(i, ts, rows, window):
    t = lax.broadcasted_iota(jnp.int32, (rows, 1), 0) + i * ts + 1
    return jnp.minimum(t, window).astype(F32)


def _pool_fwd(x, g, *, name, ts=256):
    s, d = x.shape
    hb = HALO_A
    pg = POOL_GROUP
    ts = min(ts, s)

    def body(x_ref, xp_ref, g_ref, p_ref, ext, sa, sb):
        i = pl.program_id(0)
        keep = jnp.where(i > 0, 1.0, 0.0)
        gv = g_ref[...]
        ext[0:hb, :] = _rms_rows(xp_ref[...], gv) * keep
        ext[hb:hb + ts, :] = _rms_rows(x_ref[...], gv)
        rows = hb + ts - 8
        sa[0:8, :] = jnp.zeros((8, d), F32)
        sb[0:8, :] = jnp.zeros((8, d), F32)
        for gi, w in enumerate(POOL_WINDOWS):
            cols = slice(gi * pg, (gi + 1) * pg)
            cur, nxt, k = ext, sa, 1
            while k < w:
                nxt[8:8 + rows, cols] = cur[8:8 + rows, cols] + cur[pl.ds(8 - k, rows), cols]
                cur, nxt, k = nxt, (sb if nxt is sa else sa), 2 * k
            h = ext[hb:hb + ts, cols]
            p_ref[:, cols] = (cur[hb:hb + ts, cols] / _pool_counts(i, ts, ts, w) - h).astype(BF16)

    return pl.pallas_call(
        body, name=name, grid=(s // ts,),
        in_specs=[pl.BlockSpec((ts, d), lambda i: (i, 0)),
                  pl.BlockSpec((hb, d), lambda i: (_prev_blk(i, ts, hb), 0)),
                  pl.BlockSpec((1, d), lambda i: (0, 0))],
        out_specs=pl.BlockSpec((ts, d), lambda i: (i, 0)),
        out_shape=jax.ShapeDtypeStruct((s, d), BF16),
        scratch_shapes=[pltpu.VMEM((hb + ts, d), F32)] * 3,
        compiler_params=_params(("parallel",)),
    )(x, x, g)


def _pool_mm_fwd(p, w_pool, scale, x, *, name, ts=1024):
    s, d = x.shape
    pg = POOL_GROUP
    ts = min(ts, s)

    def body(p_ref, w_ref, s_ref, x_ref, o_ref, yu_ref):
        yu = jnp.dot(p_ref[...], w_ref[...], preferred_element_type=F32)
        yu_ref[...] = yu.astype(BF16)
        o_ref[...] = x_ref[...] + yu * s_ref[...]

    blk = pl.BlockSpec((ts, pg), lambda i, gi: (i, gi))
    return pl.pallas_call(
        body, name=name, grid=(s // ts, d // pg),
        in_specs=[blk, pl.BlockSpec((None, pg, pg), lambda i, gi: (gi, 0, 0)),
                  pl.BlockSpec((1, pg), lambda i, gi: (0, gi)), blk],
        out_specs=[blk, blk],
        out_shape=[jax.ShapeDtypeStruct((s, d), F32), jax.ShapeDtypeStruct((s, d), BF16)],
        compiler_params=_params(("parallel", "parallel")),
    )(p, w_pool, scale, x)


def _pool_mm_bwd(dres, w_pool, scale, yu, *, name, ts=1024):
    s, d = dres.shape
    pg = POOL_GROUP
    ts = min(ts, s)

    def body(d_ref, w_ref, s_ref, yu_ref, dyc_ref, dp_ref, ds_ref):
        @pl.when(pl.program_id(1) == 0)
        def _():
            ds_ref[...] = jnp.zeros_like(ds_ref)

        dv = d_ref[...]
        dyc = (dv * s_ref[...]).astype(BF16)
        dyc_ref[...] = dyc
        dp_ref[...] = lax.dot_general(dyc, w_ref[...], (((1,), (1,)), ((), ())), preferred_element_type=F32)
        ds_ref[...] += jnp.sum(dv * yu_ref[...].astype(F32), axis=0, keepdims=True)

    blk = pl.BlockSpec((ts, pg), lambda gi, i: (i, gi))
    vec = pl.BlockSpec((1, pg), lambda gi, i: (0, gi))
    return pl.pallas_call(
        body, name=name, grid=(d // pg, s // ts),
        in_specs=[blk, pl.BlockSpec((None, pg, pg), lambda gi, i: (gi, 0, 0)), vec, blk],
        out_specs=[blk, blk, vec],
        out_shape=[jax.ShapeDtypeStruct((s, d), BF16), jax.ShapeDtypeStruct((s, d), F32),
                   jax.ShapeDtypeStruct((1, d), F32)],
        compiler_params=_params(("parallel", "arbitrary")),
    )(dres, w_pool, scale, yu)


def _pool_dw(p, dyc, *, name):
    s, d = p.shape
    pg = POOL_GROUP

    def body(p_ref, d_ref, o_ref):
        o_ref[...] = lax.dot_general(p_ref[...], d_ref[...], (((0,), (0,)), ((), ())),
                                     preferred_element_type=F32).astype(BF16)

    blk = pl.BlockSpec((s, pg), lambda gi: (0, gi))
    return pl.pallas_call(
        body, name=name, grid=(d // pg,),
        in_specs=[blk, blk], out_specs=pl.BlockSpec((None, pg, pg), lambda gi: (gi, 0, 0)),
        out_shape=jax.ShapeDtypeStruct((d // pg, pg, pg), BF16),
        compiler_params=_params(("parallel",)),
    )(p, dyc)


def _pool_bwd(dp, x, g, dres, *, name, ts=256):
    s, d = x.shape
    hb = HALO_A
    pg = POOL_GROUP
    ts = min(ts, s)
    n_i = s // ts

    def body(dp_ref, dpn_ref, x_ref, g_ref, dres_ref, dx_ref, dxb_ref, dg_ref, ext, dh, sa, sb):
        i = pl.program_id(0)
        keep_n = jnp.where(i < n_i - 1, 1.0, 0.0)

        @pl.when(i == 0)
        def _():
            dg_ref[...] = jnp.zeros_like(dg_ref)

        for gi, w in enumerate(POOL_WINDOWS):
            cols = slice(gi * pg, (gi + 1) * pg)
            ext[0:ts, cols] = dp_ref[:, cols] / _pool_counts(i, ts, ts, w)
            ext[ts:ts + hb, cols] = dpn_ref[:, cols] / _pool_counts(i + 1, ts, hb, w) * keep_n
        rows = ts + hb - 8
        sa[rows:rows + 8, :] = jnp.zeros((8, d), F32)
        sb[rows:rows + 8, :] = jnp.zeros((8, d), F32)
        for gi, w in enumerate(POOL_WINDOWS):
            cols = slice(gi * pg, (gi + 1) * pg)
            cur, nxt, k = ext, sa, 1
            while k < w:
                nxt[0:rows, cols] = cur[0:rows, cols] + cur[pl.ds(k, rows), cols]
                cur, nxt, k = nxt, (sb if nxt is sa else sa), 2 * k
            dh[:, cols] = cur[0:ts, cols] - dp_ref[:, cols]
        dx, dgr = _rms_bwd_rows(x_ref[...], g_ref[...], dh[...])
        dx = dres_ref[...] + dx
        dx_ref[...] = dx
        dxb_ref[...] = dx.astype(BF16)
        dg_ref[...] += jnp.sum(dgr, axis=0, keepdims=True)

    row = pl.BlockSpec((ts, d), lambda i: (i, 0))
    vec = pl.BlockSpec((1, d), lambda i: (0, 0))
    return pl.pallas_call(
        body, name=name, grid=(n_i,),
        in_specs=[row, pl.BlockSpec((hb, d), lambda i: (_next_blk(i, ts, hb, s), 0)), row, vec, row],
        out_specs=[row, row, vec],
        out_shape=[jax.ShapeDtypeStruct((s, d), F32), jax.ShapeDtypeStruct((s, d), BF16),
                   jax.ShapeDtypeStruct((1, d), F32)],
        scratch_shapes=[pltpu.VMEM((ts + hb, d), F32), pltpu.VMEM((ts, d), F32),
                        pltpu.VMEM((ts + hb, d), F32), pltpu.VMEM((ts + hb, d), F32)],
        compiler_params=_params(("arbitrary",)),
    )(dp, dp, x, g, dres)


def _pad_rows(w, rows):
    pad = [(0, 0)] * (w.ndim - 2) + [(0, rows - w.shape[-2]), (0, 0)]
    return jnp.pad(w, pad)


def _ffn_layer_fwd(x, nf, w8, wc8, wd4, tag, head=None):
    h, up8 = _ffn_up(x, nf, w8, name=f"ffn{tag}_up")
    act4, ug4, uv4 = _ffn_mid(up8, wc8, name=f"ffn{tag}_mid")
    if head is None:
        out = _ffn_down(act4, wd4, x, name=f"ffn{tag}_down")
    else:
        out = _ffn_down_loss(act4, wd4, x, *head, name=f"ffn{tag}_down_loss")
    return out, (h, up8, ug4, uv4, act4)


def _ffn_layer_bwd(d, db, x, nf, w8, wc8, wd4, saved, tag, grads_ready):
    h, up8, ug4, uv4, act4 = saved
    dact4 = _ffn_dact(db, wd4, name=f"ffn{tag}_dact")
    dwd4 = _ffn_dwdown(act4, db, name=f"ffn{tag}_dwdown")
    dupg, dupv, dwg, dwv = _ffn_midbwd(up8, ug4, uv4, dact4, wc8, name=f"ffn{tag}_midbwd")
    dw8 = _ffn_dwup(h, dupg, dupv, name=f"ffn{tag}_dwup")
    sent = grads_ready(f"ffn{tag}", {f"w_up{tag}": dw8, f"w_down{tag}": dwd4})
    dx, dxb, dnf = _ffn_dh(dupg, dupv, w8, x, nf, d, sent, name=f"ffn{tag}_dh")
    dwc8 = jnp.concatenate([dwg, dwv], axis=0)[:, :3]
    return dx, dxb, dnf, dwc8


def _local_step(x, target, weights, grads_ready):
    w0 = weights("mix0", x)
    wa = _pad_rows(w0["conv_a"], 32)
    wb = _pad_rows(w0["conv_b"], 8)
    wc = [_pad_rows(w0["conv_ffn"][l], 8) for l in range(2)]
    h0, z = _rms_mm(x, w0["norm_mix_even"], w0["w_in_t"], name="mix0_in", out_dtype=F32, tn=2560)
    ab, ca = _mix0_fwd(z, wa, w0["ln_a_g"], w0["ln_a_b"], wb, name="mix0_mid")
    x1 = _mm(ab, w0["w_out"], add=x, name="mix0_out", tm=1024, tn=1024)
    w1 = weights("ffn0", x1)
    x2, ffn0 = _ffn_layer_fwd(x1, w0["norm_ffn"][0:1], w1["w_up"], wc[0], w1["w_down"], 0)
    w2 = weights("ffn1", x2)
    p = _pool_fwd(x2, w0["norm_mix_odd"], name="pool_mid")
    x3, yu = _pool_mm_fwd(p, w2["w_pool"], w0["pool_scale"], x2, name="pool_mm")
    (loss, d4, d4b, g_norm_final), ffn1 = _ffn_layer_fwd(
        x3, w0["norm_ffn"][1:2], w2["w_up"], wc[1], w2["w_down"], 1, head=(w0["norm_final"], target))

    d3, d3b, g_nf1, g_wc1 = _ffn_layer_bwd(
        d4, d4b, x3, w0["norm_ffn"][1:2], w2["w_up"], wc[1], w2["w_down"], ffn1, 1, grads_ready)
    dyc, dp, g_scale = _pool_mm_bwd(d3, w2["w_pool"], w0["pool_scale"], yu, name="pool_mm_bwd")
    g_pool = _pool_dw(p, dyc, name="pool_dw")
    d2, d2b, g_nmo = _pool_bwd(dp, x2, w0["norm_mix_odd"], d3, name="pool_midbwd")
    d1, d1b, g_nf0, g_wc0 = _ffn_layer_bwd(
        d2, d2b, x1, w0["norm_ffn"][0:1], w1["w_up"], wc[0], w1["w_down"], ffn0, 0, grads_ready)
    dab = _mm(d1b, w0["w_out"], tb=True, name="mix0_dab", tm=1024, tn=1024)
    g_out = _mm(ab, d1b, ta=True, out_dtype=BF16, name="mix0_dwout", tm=1024, tn=512)
    dz, g_wa, g_wb, g_lg, g_lb = _mix0_bwd(z, ca, dab, wa, w0["ln_a_g"], w0["ln_a_b"], wb, name="mix0_midbwd")
    g_in = _mm(dz, h0, ta=True, out_dtype=BF16, name="mix0_dwin", tm=512, tn=1024)
    sent_mix = grads_ready("mix0", {"w_in": g_in, "w_out": g_out, "w_pool": g_pool})
    dx, g_nme = _mm_rms_bwd(dz, w0["w_in_t"], x, w0["norm_mix_even"], d1, sent_mix, name="mix0_dh")

    small = {
        "norm_mix_even": g_nme, "conv_a": g_wa[:A_TAPS], "ln_a_g": g_lg, "ln_a_b": g_lb, "conv_b": g_wb[:3],
        "norm_mix_odd": g_nmo, "pool_scale": g_scale, "norm_ffn": jnp.concatenate([g_nf0, g_nf1], axis=0),
        "conv_ffn": [g_wc0, g_wc1], "norm_final": g_norm_final,
    }
    return loss[0, 0], dx, small


def _my_pos():
    return lax.axis_index("x"), lax.axis_index("y"), lax.axis_index("c")


def _flip(pos, r):
    x, y, c = pos
    return (1 - x if r & 4 else x, 1 - y if r & 2 else y, 1 - c if r & 1 else c)


def _dev_index(pos):
    return 4 * pos[0] + 2 * pos[1] + pos[2]


_HBM = pl.BlockSpec(memory_space=pltpu.HBM)
_SEM = pl.BlockSpec(memory_space=pltpu.SEMAPHORE)
_EFFECT = pltpu.SideEffectType.DATAFLOW_SIDE_EFFECTING


def _exchange_copy(ins, lands, send_sems, recv_sems, scatter, pos, a, r, receiving):
    me = _dev_index(pos)
    peer = _flip(pos, r)
    dest = _dev_index(pos) if receiving else _dev_index(peer)
    src = ins[a].at[dest] if scatter[a] else ins[a]
    slot = _dev_index(peer) if receiving else me
    return pltpu.make_async_remote_copy(
        src_ref=src, dst_ref=lands[a].at[slot], send_sem=send_sems.at[a * (N_DEV - 1) + r - 1],
        recv_sem=recv_sems.at[a * (N_DEV - 1) + r - 1],
        device_id=peer, device_id_type=pl.DeviceIdType.MESH)


ALL_PEERS = tuple(range(1, N_DEV))
CHIP_PEERS = (1, 2, 4, 6)
FORWARDED = (2, 4, 6)


def _exchange_start(arrays, scatter, after, *, name, peers=ALL_PEERS):
    n = len(arrays)
    me = _dev_index(_my_pos())
    lands = []
    for arr, sc in zip(arrays, scatter):
        own = lax.dynamic_index_in_dim(arr, me, 0, keepdims=True) if sc else arr[None]
        shape = arr.shape if sc else (N_DEV,) + arr.shape
        lands.append(lax.dynamic_update_slice(lax.empty(shape, arr.dtype), own, (me,) + (0,) * (len(shape) - 1)))

    def body(*refs):
        ins, lnd = refs[:n], refs[n:2 * n]
        send_sems, recv_sems = refs[2 * n + 1], refs[2 * n + 2]
        token = refs[-1]
        pos = _my_pos()
        for a in range(n):
            for r in peers:
                _exchange_copy(ins, lnd, send_sems, recv_sems, scatter, pos, a, r, receiving=False).start()
        token[...] = jnp.zeros_like(token)

    bufs = [pltpu.with_memory_space_constraint(t, pltpu.HBM) for t in list(arrays) + lands]
    sems = pltpu.SemaphoreType.DMA((n * (N_DEV - 1),))
    res = pl.pallas_call(
        body, name=name,
        out_shape=(sems, sems, *[pltpu.HBM(t.shape, t.dtype) for t in bufs], jax.ShapeDtypeStruct((8, 128), F32)),
        in_specs=[_HBM] * (2 * n) + [pl.BlockSpec(memory_space=pl.ANY)],
        out_specs=(_SEM, _SEM, *[_HBM] * (2 * n), pl.BlockSpec(memory_space=pltpu.VMEM)),
        input_output_aliases={i: 2 + i for i in range(2 * n)},
        compiler_params=pltpu.CompilerParams(has_side_effects=_EFFECT),
    )(*bufs, after)
    return res[0], res[1], list(res[2:2 + n]), list(res[2 + n:2 + 2 * n]), res[-1]


def _exchange_wait(started, scatter, after, *, name, peers=ALL_PEERS):
    send_sems, recv_sems, arrays, lands, _ = started
    n = len(arrays)

    def body(*refs):
        ins, lnd = refs[:n], refs[n:2 * n]
        send_sems, recv_sems = refs[2 * n], refs[2 * n + 1]
        pos = _my_pos()
        for a in range(n):
            for r in peers:
                _exchange_copy(ins, lnd, send_sems, recv_sems, scatter, pos, a, r, receiving=False).wait_send()
                _exchange_copy(ins, lnd, send_sems, recv_sems, scatter, pos, a, r, receiving=True).wait_recv()

    bufs = list(arrays) + list(lands)
    after = list(after) if isinstance(after, (list, tuple)) else [after]
    res = pl.pallas_call(
        body, name=name,
        out_shape=tuple(pltpu.HBM(t.shape, t.dtype) for t in bufs),
        in_specs=[_HBM] * (2 * n) + [_SEM, _SEM] + [pl.BlockSpec(memory_space=pl.ANY)] * len(after),
        out_specs=tuple([_HBM] * (2 * n)),
        input_output_aliases={i: i for i in range(2 * n)},
        compiler_params=pltpu.CompilerParams(has_side_effects=_EFFECT),
    )(*bufs, send_sems, recv_sems, *after)
    return list(res[n:])


def _forward_copy(lands, send_sems, recv_sems, pos, a, q, receiving):
    slot = _dev_index(_flip(pos, q ^ 1 if receiving else q))
    idx = a * len(FORWARDED) + FORWARDED.index(q)
    return pltpu.make_async_remote_copy(
        src_ref=lands[a].at[slot], dst_ref=lands[a].at[slot], send_sem=send_sems.at[idx], recv_sem=recv_sems.at[idx],
        device_id=_flip(pos, 1), device_id_type=pl.DeviceIdType.MESH)


def _forward_start(lands, after, *, name):
    n = len(lands)

    def body(*refs):
        lnd = refs[:n]
        send_sems, recv_sems = refs[n + 1], refs[n + 2]
        token = refs[-1]
        pos = _my_pos()
        for a in range(n):
            for q in FORWARDED:
                _forward_copy(lnd, send_sems, recv_sems, pos, a, q, receiving=False).start()
        token[...] = jnp.zeros_like(token)

    sems = pltpu.SemaphoreType.DMA((n * len(FORWARDED),))
    res = pl.pallas_call(
        body, name=name,
        out_shape=(sems, sems, *[pltpu.HBM(t.shape, t.dtype) for t in lands], jax.ShapeDtypeStruct((8, 128), F32)),
        in_specs=[_HBM] * n + [pl.BlockSpec(memory_space=pl.ANY)],
        out_specs=(_SEM, _SEM, *[_HBM] * n, pl.BlockSpec(memory_space=pltpu.VMEM)),
        input_output_aliases={i: 2 + i for i in range(n)},
        compiler_params=pltpu.CompilerParams(has_side_effects=_EFFECT),
    )(*lands, after)
    return res[0], res[1], list(res[2:2 + n]), res[-1]


def _forward_wait(forwarded, after, *, name):
    send_sems, recv_sems, lands, _ = forwarded
    n = len(lands)

    def body(*refs):
        lnd = refs[:n]
        send_sems, recv_sems = refs[n], refs[n + 1]
        pos = _my_pos()
        for a in range(n):
            for q in FORWARDED:
                _forward_copy(lnd, send_sems, recv_sems, pos, a, q, receiving=False).wait_send()
                _forward_copy(lnd, send_sems, recv_sems, pos, a, q, receiving=True).wait_recv()

    res = pl.pallas_call(
        body, name=name,
        out_shape=tuple(pltpu.HBM(t.shape, t.dtype) for t in lands),
        in_specs=[_HBM] * n + [_SEM, _SEM, pl.BlockSpec(memory_space=pl.ANY)],
        out_specs=tuple([_HBM] * n),
        input_output_aliases={i: i for i in range(n)},
        compiler_params=pltpu.CompilerParams(has_side_effects=_EFFECT),
    )(*lands, send_sems, recv_sems, after)
    return list(res)


def _adamw_update(p_ref, w_ref, m_ref, v_ref, g_ref, d_ref, mo_ref, vo_ref):
    g = p_ref[0].astype(F32)
    for k in range(1, N_DEV):
        g = g + p_ref[k].astype(F32)
    mn = ADAM_B1 * m_ref[...] + (1.0 - ADAM_B1) * g
    vn = ADAM_B2 * v_ref[...] + (1.0 - ADAM_B2) * (g * g)
    m_hat = mn / (1.0 - ADAM_B1 ** ADAM_STEP)
    v_hat = vn / (1.0 - ADAM_B2 ** ADAM_STEP)
    g_ref[...] = g
    d_ref[...] = -ADAM_LR * (m_hat / (jnp.sqrt(v_hat) + ADAM_EPS) + ADAM_WD * w_ref[...])
    mo_ref[...] = mn
    vo_ref[...] = vn


def _adamw_small(parts, ws, ms, vs, *, name):
    n = len(ws)

    def body(*refs):
        ins, outs = refs[:4 * n], refs[4 * n:]
        for j in range(n):
            _adamw_update(ins[j], ins[n + j], ins[2 * n + j], ins[3 * n + j], *outs[4 * j:4 * j + 4])

    return pl.pallas_call(
        body, name=name,
        out_shape=[jax.ShapeDtypeStruct(w.shape, F32) for w in ws for _ in range(4)],
    )(*parts, *ws, *ms, *vs)


def _adamw(parts, w, m, v, *, name, tr):
    nl, r, c = w.shape
    assert len(parts) == nl and r % tr == 0
    n_i = r // tr

    def body(*refs):
        p_refs = refs[:nl]
        w_ref, m_ref, v_ref = refs[nl:nl + 3]

        def update(p_ref):
            _adamw_update(p_ref, w_ref, m_ref, v_ref, *refs[nl + 3:])

        if nl == 1:
            update(p_refs[0])
        else:
            for layer in range(nl):
                pl.when(pl.program_id(0) == layer)(lambda p_ref=p_refs[layer]: update(p_ref))

    def parts_spec(layer):
        def index(l, i):
            return (0, jnp.where(l < layer, 0, jnp.where(l > layer, n_i - 1, i)), 0)
        return pl.BlockSpec((N_DEV, tr, c), index)

    blk = pl.BlockSpec((None, tr, c), lambda l, i: (l, i, 0))
    return pl.pallas_call(
        body, name=name, grid=(nl, n_i),
        in_specs=[parts_spec(layer) for layer in range(nl)] + [blk, blk, blk],
        out_specs=[blk] * 4, out_shape=[jax.ShapeDtypeStruct((nl, r, c), F32)] * 4,
        compiler_params=_params(("arbitrary", "arbitrary")),
    )(*parts, w, m, v)


def _pack(parts, lead=()):
    flat = jnp.concatenate([p.reshape(lead + (-1,)) for p in parts], axis=-1)
    n = flat.shape[-1]
    rows = -(-n // (8 * 128)) * 8
    flat = jnp.pad(flat, [(0, 0)] * len(lead) + [(0, rows * 128 - n)])
    return flat.reshape(lead + (rows, 128))


def _to_dev_major(g, axis):
    shp = g.shape
    g = g.reshape(shp[:axis] + (N_DEV, shp[axis] // N_DEV) + shp[axis + 1:])
    return jnp.moveaxis(g, axis, 0)


def _from_dev_major(g, axis):
    g = jnp.moveaxis(g, 0, axis)
    shp = g.shape
    return g.reshape(shp[:axis] + (shp[axis] * shp[axis + 1],) + shp[axis + 2:])


SMALL_SHARDED = ("conv_a", "conv_b", "norm_mix_odd", "pool_scale", "conv_ffn_w")
SMALL_REPLICATED = ("norm_mix_even", "ln_a_g", "ln_a_b", "norm_ffn", "norm_final")
BIG = {"w_in": ("w_in", 0, 320), "w_out": ("w_out", 0, 128), "w_pool": ("w_pool", 0, 128),
       "w_up0": ("w_up", 0, 352), "w_up1": ("w_up", 1, 352),
       "w_down0": ("w_down", 0, 352), "w_down1": ("w_down", 1, 352)}
COLUMN_SHARDED = ("w_in", "w_up")


def kernel(x, norm_mix_even, w_in, conv_a, ln_a_g, ln_a_b, conv_b, w_out, norm_mix_odd, w_pool, pool_scale, norm_ffn, w_up, conv_ffn_w, w_down, norm_final, loss_target, m_norm_mix_even, m_w_in, m_conv_a, m_ln_a_g, m_ln_a_b, m_conv_b, m_w_out, m_norm_mix_odd, m_w_pool, m_pool_scale, m_norm_ffn, m_w_up, m_conv_ffn_w, m_w_down, m_norm_final, v_norm_mix_even, v_w_in, v_conv_a, v_ln_a_g, v_ln_a_b, v_conv_b, v_w_out, v_norm_mix_odd, v_w_pool, v_pool_scale, v_norm_ffn, v_w_up, v_conv_ffn_w, v_w_down, v_norm_final):
    names = ("norm_mix_even", "w_in", "conv_a", "ln_a_g", "ln_a_b", "conv_b", "w_out", "norm_mix_odd", "w_pool",
             "pool_scale", "norm_ffn", "w_up", "conv_ffn_w", "w_down", "norm_final")
    wts = dict(zip(names, (norm_mix_even, w_in, conv_a, ln_a_g, ln_a_b, conv_b, w_out, norm_mix_odd, w_pool,
                           pool_scale, norm_ffn, w_up, conv_ffn_w, w_down, norm_final)))
    mom = dict(zip(names, (m_norm_mix_even, m_w_in, m_conv_a, m_ln_a_g, m_ln_a_b, m_conv_b, m_w_out, m_norm_mix_odd,
                           m_w_pool, m_pool_scale, m_norm_ffn, m_w_up, m_conv_ffn_w, m_w_down, m_norm_final)))
    var = dict(zip(names, (v_norm_mix_even, v_w_in, v_conv_a, v_ln_a_g, v_ln_a_b, v_conv_b, v_w_out, v_norm_mix_odd,
                           v_w_pool, v_pool_scale, v_norm_ffn, v_w_up, v_conv_ffn_w, v_w_down, v_norm_final)))
    d = x.shape[-1]

    def shard3d(t, param):
        a = t[param]
        if param in COLUMN_SHARDED:
            return jnp.swapaxes(a, 1, 2)
        return a.reshape(a.shape[0], -1, a.shape[-1])

    def unshard3d(a, param):
        if param in COLUMN_SHARDED:
            return jnp.swapaxes(a, 1, 2)
        return a.reshape(wts[param].shape)

    def shard2d(t, key):
        param, layer, _ = BIG[key]
        return shard3d(t, param)[layer]

    small_w = _pack([wts[k] for k in SMALL_SHARDED])
    bf = {k: shard2d(wts, k).astype(BF16) for k in BIG}
    gather_groups = {"mix0": ("w_in", "w_out", "small"), "ffn0": ("w_up0", "w_down0"),
                     "ffn1": ("w_pool", "w_up1", "w_down1")}
    order = list(gather_groups)
    started = {}

    def start_gather(grp, after):
        arrs = [small_w if k == "small" else bf[k] for k in gather_groups[grp]]
        started[grp] = _exchange_start(arrs, [False] * len(arrs), after, name=f"gather_{grp}_start", peers=CHIP_PEERS)

    start_gather(order[0], small_w)

    def weights(grp, after):
        keys = gather_groups[grp]
        lands = _exchange_wait(started[grp], [False] * len(keys), after, name=f"gather_{grp}_wait", peers=CHIP_PEERS)
        forwarded = _forward_start(lands, small_w, name=f"gather_{grp}_forward")
        if grp != order[-1]:
            start_gather(order[order.index(grp) + 1], forwarded[-1])
            behind = started[order[order.index(grp) + 1]][-1]
        else:
            behind = forwarded[-1]
        gw = dict(zip(keys, _forward_wait(forwarded, behind, name=f"gather_{grp}_forward_wait")))
        if grp == "ffn0":
            return {"w_up": gw["w_up0"], "w_down": gw["w_down0"].reshape(N_PAIR, -1, d)}
        if grp == "ffn1":
            return {"w_up": gw["w_up1"], "w_down": gw["w_down1"].reshape(N_PAIR, -1, d),
                    "w_pool": _from_dev_major(gw["w_pool"].reshape(N_DEV, len(POOL_WINDOWS), -1, POOL_GROUP), 1)}
        per_dev = gw["small"].reshape(N_DEV, -1)
        sizes = [wts[k].size for k in SMALL_SHARDED]
        offs = [sum(sizes[:i]) for i in range(len(sizes))]
        small_full = {k: per_dev[:, o:o + n_].reshape((N_DEV,) + wts[k].shape)
                      for k, o, n_ in zip(SMALL_SHARDED, offs, sizes)}
        return {
            "norm_mix_even": norm_mix_even, "ln_a_g": ln_a_g, "ln_a_b": ln_a_b, "norm_ffn": norm_ffn,
            "norm_final": norm_final[None],
            "w_in_t": gw["w_in"].reshape(-1, d),
            "w_out": gw["w_out"].reshape(-1, d),
            "conv_a": _from_dev_major(small_full["conv_a"][:, 0], 1),
            "conv_b": _from_dev_major(small_full["conv_b"][:, 0], 1),
            "norm_mix_odd": _from_dev_major(small_full["norm_mix_odd"], 1),
            "pool_scale": _from_dev_major(small_full["pool_scale"], 1),
            "conv_ffn": [small_full["conv_ffn_w"][:, l] for l in range(2)],
        }

    def dev_major(k, g):
        if k == "w_pool":
            return _to_dev_major(g, 1).reshape(N_DEV, -1, POOL_GROUP)
        return g.reshape(N_DEV, -1, g.shape[-1])

    sent = {}

    def grads_ready(grp, grads):
        keys = tuple(grads)
        parts = [dev_major(k, grads[k]) for k in keys]
        sent[grp] = (keys, _exchange_start(parts, [True] * len(keys), small_w, name=f"grads_{grp}_start"))
        return sent[grp][1][-1]

    loss, dx, g = _local_step(x[0], loss_target[0], weights, grads_ready)

    def small2d(a):
        return a.reshape(-1, a.shape[-1])

    small_keys = SMALL_SHARDED + SMALL_REPLICATED
    small_parts = [_to_dev_major(g["conv_a"], 1), _to_dev_major(g["conv_b"], 1),
                   _to_dev_major(g["norm_mix_odd"], 1), _to_dev_major(g["pool_scale"], 1),
                   jnp.stack(g["conv_ffn"], axis=1).reshape(N_DEV, -1, w_up.shape[-1])]
    small_parts += [g[k] for k in SMALL_REPLICATED]
    small_scatter = [True] * len(SMALL_SHARDED) + [False] * len(SMALL_REPLICATED)
    small_parts.append(jnp.full((1, 128), loss, F32))
    small_scatter.append(False)
    sent["small"] = (small_keys + ("loss",),
                     _exchange_start(small_parts, small_scatter, dx, name="grads_small_start"))

    landed, out, raw = {}, {}, {}

    def wait_grads(grp, after, scat=None):
        keys, st = sent[grp]
        scat = [True] * len(keys) if scat is None else scat
        landed.update(zip(keys, _exchange_wait(st, scat, after, name=f"grads_{grp}_wait")))

    def update(param, keys):
        raw[param] = _adamw([landed[k] for k in keys], shard3d(wts, param), shard3d(mom, param),
                            shard3d(var, param), name=f"adamw_{param}", tr=BIG[keys[0]][2])
        out[param] = [unshard3d(t, param) for t in raw[param]]

    wait_grads("ffn1", sent["small"][1][-1])
    wait_grads("ffn0", landed["w_up1"])
    update("w_up", ("w_up0", "w_up1"))
    update("w_down", ("w_down0", "w_down1"))
    wait_grads("mix0", [raw["w_up"][1], raw["w_down"][1]])
    update("w_in", ("w_in",))
    update("w_out", ("w_out",))
    update("w_pool", ("w_pool",))
    wait_grads("small", raw["w_pool"][1], small_scatter)
    res = _adamw_small([landed[k] for k in small_keys], [small2d(wts[k]) for k in small_keys],
                       [small2d(mom[k]) for k in small_keys], [small2d(var[k]) for k in small_keys],
                       name="adamw_small")
    for j, k in enumerate(small_keys):
        out[k] = [t.reshape(wts[k].shape) for t in res[4 * j:4 * j + 4]]

    loss = jnp.sum(landed["loss"][:, 0, 0])
    return (loss, dx[None], *[out[k][0] for k in names], *[out[k][1] for k in names],
            *[out[k][2] for k in names], *[out[k][3] for k in names])
```

```python
import jax
import jax.numpy as jnp
from jax import lax
from jax.experimental import pallas as pl
from jax.experimental.pallas import tpu as pltpu

F32 = jnp.float32
BF16 = jnp.bfloat16

RMS_EPS = 1e-6
LN_EPS = 1e-5
ADAM_LR = 0.001
ADAM_B1 = 0.9
ADAM_B2 = 0.999
ADAM_EPS = 1e-08
ADAM_WD = 0.01
ADAM_STEP = 10

N_DEV = 8
N_PAIR = N_DEV // 2
A_WIDTH = 512
A_TAPS = 31
POOL_WINDOWS = (2, 4, 8, 16)
POOL_GROUP = 256
HALO_A = 32
HALO_S = 16
VMEM_LIMIT = 56 * 1024 * 1024


def _params(sem, vmem=VMEM_LIMIT):
    return pltpu.CompilerParams(dimension_semantics=sem, vmem_limit_bytes=vmem)


def _sigmoid(x):
    return 0.5 * jnp.tanh(0.5 * x) + 0.5


def _prev_blk(i, ts, hb):
    return jnp.maximum(i * (ts // hb) - 1, 0)


def _next_blk(i, ts, hb, s):
    return jnp.minimum((i + 1) * (ts // hb), s // hb - 1)


def _mm(a, b, *, name, ta=False, tb=False, add=None, out_dtype=F32, tm=512, tn=512, tk=None):
    m, k = (a.shape[1], a.shape[0]) if ta else a.shape
    n = b.shape[0] if tb else b.shape[1]
    tk = k if tk is None else tk
    tm, tn, tk = min(tm, m), min(tn, n), min(tk, k)
    assert m % tm == 0 and n % tn == 0 and k % tk == 0, (name, m, n, k, tm, tn, tk)
    nk = k // tk
    dims = (((0,) if ta else (1,), (1,) if tb else (0,)), ((), ()))
    n_in = 2 + (add is not None)

    def body(*refs):
        a_ref, b_ref = refs[0], refs[1]
        add_ref = refs[2] if add is not None else None
        o_ref = refs[n_in]
        part = lax.dot_general(a_ref[...].astype(BF16), b_ref[...].astype(BF16), dims, preferred_element_type=F32)

        def finish(r):
            if add_ref is not None:
                r = r + add_ref[...]
            o_ref[...] = r.astype(out_dtype)

        if nk == 1:
            finish(part)
            return
        acc_ref = refs[-1]
        kk = pl.program_id(2)

        @pl.when(kk == 0)
        def _():
            acc_ref[...] = part

        @pl.when(kk > 0)
        def _():
            acc_ref[...] += part

        @pl.when(kk == nk - 1)
        def _():
            finish(acc_ref[...])

    a_spec = pl.BlockSpec((tk, tm), lambda i, j, kk: (kk, i)) if ta else pl.BlockSpec((tm, tk), lambda i, j, kk: (i, kk))
    b_spec = pl.BlockSpec((tn, tk), lambda i, j, kk: (j, kk)) if tb else pl.BlockSpec((tk, tn), lambda i, j, kk: (kk, j))
    in_specs = [a_spec, b_spec]
    args = [a, b]
    if add is not None:
        in_specs.append(pl.BlockSpec((tm, tn), lambda i, j, kk: (i, j)))
        args.append(add)
    return pl.pallas_call(
        body, name=name, grid=(m // tm, n // tn, nk),
        in_specs=in_specs, out_specs=pl.BlockSpec((tm, tn), lambda i, j, kk: (i, j)),
        out_shape=jax.ShapeDtypeStruct((m, n), out_dtype),
        scratch_shapes=[pltpu.VMEM((tm, tn), F32)] if nk > 1 else [],
        compiler_params=_params(("parallel", "parallel", "arbitrary")),
    )(*args)


def _rms_rows(xv, gv):
    return xv * lax.rsqrt(jnp.mean(xv * xv, axis=-1, keepdims=True) + RMS_EPS) * gv


_NT = (((1,), (1,)), ((), ()))
_TN = (((0,), (0,)), ((), ()))


def _rms_mm(x, g, wt, *, name, out_dtype, tm=1024, tn=512):
    s, d = x.shape
    n = wt.shape[0]
    tm = min(tm, s)
    assert s % tm == 0 and n % tn == 0

    def body(x_ref, g_ref, w_ref, h_ref, z_ref, hs_ref):
        @pl.when(pl.program_id(1) == 0)
        def _():
            hv = _rms_rows(x_ref[...], g_ref[...]).astype(BF16)
            hs_ref[...] = hv
            h_ref[...] = hv

        z_ref[...] = lax.dot_general(hs_ref[...], w_ref[...], _NT, preferred_element_type=F32).astype(out_dtype)

    return pl.pallas_call(
        body, name=name, grid=(s // tm, n // tn),
        in_specs=[pl.BlockSpec((tm, d), lambda i, j: (i, 0)),
                  pl.BlockSpec((1, d), lambda i, j: (0, 0)),
                  pl.BlockSpec((tn, d), lambda i, j: (j, 0))],
        out_specs=[pl.BlockSpec((tm, d), lambda i, j: (i, 0)),
                   pl.BlockSpec((tm, tn), lambda i, j: (i, j))],
        out_shape=[jax.ShapeDtypeStruct((s, d), BF16), jax.ShapeDtypeStruct((s, n), out_dtype)],
        scratch_shapes=[pltpu.VMEM((tm, d), BF16)],
        compiler_params=_params(("parallel", "arbitrary")),
    )(x, g, wt)


def _rms_bwd_rows(xv, gv, dh):
    r = lax.rsqrt(jnp.mean(xv * xv, axis=-1, keepdims=True) + RMS_EPS)
    xh = xv * r
    dn = dh * gv
    dx = r * (dn - xh * jnp.mean(dn * xh, axis=-1, keepdims=True))
    return dx, dh * xh


def _mm_rms_bwd(a, b, x, g, dres, dep, *, name, tm=512):
    s, k = a.shape
    d = b.shape[1]
    tm = min(tm, s)

    def body(a_ref, b_ref, x_ref, g_ref, dres_ref, dep_ref, dx_ref, dg_ref):
        @pl.when(pl.program_id(0) == 0)
        def _():
            dg_ref[...] = jnp.zeros_like(dg_ref)

        dh = jnp.dot(a_ref[...], b_ref[...], preferred_element_type=F32)
        dx, dgr = _rms_bwd_rows(x_ref[...], g_ref[...], dh)
        dx_ref[...] = dres_ref[...] + dx
        dg_ref[...] += jnp.sum(dgr, axis=0, keepdims=True)

    row = pl.BlockSpec((tm, d), lambda i: (i, 0))
    vec = pl.BlockSpec((1, d), lambda i: (0, 0))
    return pl.pallas_call(
        body, name=name, grid=(s // tm,),
        in_specs=[pl.BlockSpec((tm, k), lambda i: (i, 0)), pl.BlockSpec((k, d), lambda i: (0, 0)), row, vec, row,
                  pl.BlockSpec(memory_space=pl.ANY)],
        out_specs=[row, vec],
        out_shape=[jax.ShapeDtypeStruct((s, d), F32), jax.ShapeDtypeStruct((1, d), F32)],
        compiler_params=_params(("arbitrary",)),
    )(a, b, x, g, dres, dep)


def _conv_taps(ext_ref, w_ref, n_taps, base, r0, rows, reverse=False):
    acc = None
    for k in range(n_taps):
        off = r0 + (base - k if reverse else base + k)
        term = w_ref[k:k + 1, :] * ext_ref[pl.ds(off, rows), :]
        acc = term if acc is None else acc + term
    return acc


def _shift_copies(src_ref, sh_ref, rows):
    for b in range(1, 8):
        sh_ref[b, 0:rows, :] = src_ref[pl.ds(b, rows), :]


def _shifted(src_ref, sh_ref, start, rows, off):
    a, b = divmod(off, 8)
    ref = src_ref if b == 0 else sh_ref.at[b]
    return ref[pl.ds(start + 8 * a, rows), :]


def _mix0_fwd(z, conv_a, ln_g, ln_b, conv_b, *, name, ts=256, rc=32):
    s = z.shape[0]
    c = A_WIDTH
    hb = HALO_A

    def body(z_ref, zp_ref, wa_ref, lg_ref, lb_ref, wb_ref, ab_ref, ca_ref, exta, extb, sha):
        keep = jnp.where(pl.program_id(0) > 0, 1.0, 0.0)
        zp = zp_ref[...]
        exta[0:hb, :] = zp[:, 0:c] * _sigmoid(zp[:, c:2 * c]) * keep
        extb[0:hb, :] = zp[:, 3 * c:4 * c] * zp[:, 4 * c:5 * c] * keep
        exta[hb:hb + ts, :] = z_ref[:, 0:c] * _sigmoid(z_ref[:, c:2 * c])
        extb[hb:hb + ts, :] = z_ref[:, 3 * c:4 * c] * z_ref[:, 4 * c:5 * c]
        _shift_copies(exta, sha, hb + ts - 8)
        lg = lg_ref[...]
        lb = lb_ref[...]
        for q in range(ts // rc):
            r0 = q * rc
            ca = None
            for k in range(A_TAPS):
                term = wa_ref[k:k + 1, :] * _shifted(exta, sha, r0, rc, hb - (A_TAPS - 1) + k)
                ca = term if ca is None else ca + term
            ca_ref[r0:r0 + rc, :] = ca
            mu = jnp.mean(ca, axis=-1, keepdims=True)
            xc = ca - mu
            rs = lax.rsqrt(jnp.mean(xc * xc, axis=-1, keepdims=True) + LN_EPS)
            l = xc * rs * lg + lb
            ab_ref[r0:r0 + rc, 0:c] = (l * _sigmoid(l)).astype(BF16)
            cbc = _conv_taps(extb, wb_ref, 3, hb - 2, r0, rc)
            ab_ref[r0:r0 + rc, c:2 * c] = (z_ref[r0:r0 + rc, 2 * c:3 * c] * cbc).astype(BF16)

    return pl.pallas_call(
        body, name=name, grid=(s // ts,),
        in_specs=[pl.BlockSpec((ts, 5 * c), lambda i: (i, 0)),
                  pl.BlockSpec((hb, 5 * c), lambda i: (_prev_blk(i, ts, hb), 0)),
                  pl.BlockSpec((32, c), lambda i: (0, 0)),
                  pl.BlockSpec((1, c), lambda i: (0, 0)),
                  pl.BlockSpec((1, c), lambda i: (0, 0)),
                  pl.BlockSpec((8, c), lambda i: (0, 0))],
        out_specs=[pl.BlockSpec((ts, 2 * c), lambda i: (i, 0)),
                   pl.BlockSpec((ts, c), lambda i: (i, 0))],
        out_shape=[jax.ShapeDtypeStruct((s, 2 * c), BF16), jax.ShapeDtypeStruct((s, c), F32)],
        scratch_shapes=[pltpu.VMEM((hb + ts, c), F32), pltpu.VMEM((hb + ts, c), F32),
                        pltpu.VMEM((8, hb + ts - 8, c), F32)],
        compiler_params=_params(("parallel",)),
    )(z, z, conv_a, ln_g, ln_b, conv_b)


def _mix0_bwd(z, ca, dab, conv_a, ln_g, ln_b, conv_b, *, name, ts=256, rc=32):
    s = z.shape[0]
    c = A_WIDTH
    hb = HALO_A
    ta = A_TAPS

    def body(z_ref, zp_ref, zn_ref, ca_ref, can_ref, d_ref, dn_ref, wa_ref, lg_ref, lb_ref, wb_ref,
             dz_ref, dwa_ref, dwb_ref, dlg_ref, dlb_ref, exta, extb, extdca, extdcb, shd):
        i = pl.program_id(0)
        keep_p = jnp.where(i > 0, 1.0, 0.0)
        keep_n = jnp.where(i < s // ts - 1, 1.0, 0.0)

        @pl.when(i == 0)
        def _():
            dwa_ref[...] = jnp.zeros_like(dwa_ref)
            dwb_ref[...] = jnp.zeros_like(dwb_ref)
            dlg_ref[...] = jnp.zeros_like(dlg_ref)
            dlb_ref[...] = jnp.zeros_like(dlb_ref)

        lg = lg_ref[...]
        lb = lb_ref[...]
        zp = zp_ref[...]
        exta[0:hb, :] = zp[:, 0:c] * _sigmoid(zp[:, c:2 * c]) * keep_p
        extb[0:hb, :] = zp[:, 3 * c:4 * c] * zp[:, 4 * c:5 * c] * keep_p
        exta[hb:hb + ts, :] = z_ref[:, 0:c] * _sigmoid(z_ref[:, c:2 * c])
        extb[hb:hb + ts, :] = z_ref[:, 3 * c:4 * c] * z_ref[:, 4 * c:5 * c]

        def ln_bwd(cav, dav):
            mu = jnp.mean(cav, axis=-1, keepdims=True)
            xc = cav - mu
            rs = lax.rsqrt(jnp.mean(xc * xc, axis=-1, keepdims=True) + LN_EPS)
            nv = xc * rs
            l = nv * lg + lb
            sg = _sigmoid(l)
            dl = dav * (sg * (1.0 + l * (1.0 - sg)))
            dnv = dl * lg
            dca = rs * (dnv - jnp.mean(dnv, axis=-1, keepdims=True)
                        - nv * jnp.mean(dnv * nv, axis=-1, keepdims=True))
            return dca, dl, nv

        dlg_acc = jnp.zeros((1, c), F32)
        dlb_acc = jnp.zeros((1, c), F32)
        for q in range(ts // rc):
            r0 = q * rc
            dca, dl, nv = ln_bwd(ca_ref[r0:r0 + rc, :], d_ref[r0:r0 + rc, 0:c])
            extdca[r0:r0 + rc, :] = dca
            dlg_acc = dlg_acc + jnp.sum(dl * nv, axis=0, keepdims=True)
            dlb_acc = dlb_acc + jnp.sum(dl, axis=0, keepdims=True)
            extdcb[r0:r0 + rc, :] = d_ref[r0:r0 + rc, c:2 * c] * z_ref[r0:r0 + rc, 2 * c:3 * c]
        dca_n, _, _ = ln_bwd(can_ref[...], dn_ref[:, 0:c])
        extdca[ts:ts + hb, :] = dca_n * keep_n
        extdcb[ts:ts + hb, :] = dn_ref[:, c:2 * c] * zn_ref[:, 2 * c:3 * c] * keep_n
        dlg_ref[...] += dlg_acc
        dlb_ref[...] += dlb_acc
        _shift_copies(extdca, shd, ts + hb - 8)

        for q in range(ts // rc):
            r0 = q * rc
            zr = z_ref[r0:r0 + rc, :]
            dga = None
            for k in range(ta):
                term = wa_ref[k:k + 1, :] * _shifted(extdca, shd, r0, rc, ta - 1 - k)
                dga = term if dga is None else dga + term
            sg = _sigmoid(zr[:, c:2 * c])
            dz_ref[r0:r0 + rc, 0:c] = (dga * sg).astype(BF16)
            dz_ref[r0:r0 + rc, c:2 * c] = (dga * zr[:, 0:c] * sg * (1.0 - sg)).astype(BF16)
            cbc = _conv_taps(extb, wb_ref, 3, hb - 2, r0, rc)
            dz_ref[r0:r0 + rc, 2 * c:3 * c] = (d_ref[r0:r0 + rc, c:2 * c] * cbc).astype(BF16)
            dcb = _conv_taps(extdcb, wb_ref, 3, 2, r0, rc, reverse=True)
            dz_ref[r0:r0 + rc, 3 * c:4 * c] = (dcb * zr[:, 4 * c:5 * c]).astype(BF16)
            dz_ref[r0:r0 + rc, 4 * c:5 * c] = (dcb * zr[:, 3 * c:4 * c]).astype(BF16)

        for k in range(ta):
            part = None
            for q in range(ts // rc):
                r0 = q * rc
                p = exta[hb + r0:hb + r0 + rc, :] * _shifted(extdca, shd, r0, rc, ta - 1 - k)
                for r in range(0, rc, 8):
                    part = p[r:r + 8, :] if part is None else part + p[r:r + 8, :]
            dwa_ref[k:k + 1, :] += jnp.sum(part, axis=0, keepdims=True)
        dcb_t = extdcb[0:ts, :]
        for k in range(3):
            dwb_ref[k:k + 1, :] += jnp.sum(dcb_t * extb[pl.ds(hb - 2 + k, ts), :], axis=0, keepdims=True)

    def tile(w):
        return pl.BlockSpec((ts, w), lambda i: (i, 0))

    def prev(w):
        return pl.BlockSpec((hb, w), lambda i: (_prev_blk(i, ts, hb), 0))

    def nxt(w):
        return pl.BlockSpec((hb, w), lambda i: (_next_blk(i, ts, hb, s), 0))

    def const(r, w):
        return pl.BlockSpec((r, w), lambda i: (0, 0))

    return pl.pallas_call(
        body, name=name, grid=(s // ts,),
        in_specs=[tile(5 * c), prev(5 * c), nxt(5 * c), tile(c), nxt(c), tile(2 * c), nxt(2 * c),
                  const(32, c), const(1, c), const(1, c), const(8, c)],
        out_specs=[tile(5 * c), const(32, c), const(8, c), const(1, c), const(1, c)],
        out_shape=[jax.ShapeDtypeStruct((s, 5 * c), BF16), jax.ShapeDtypeStruct((32, c), F32),
                   jax.ShapeDtypeStruct((8, c), F32), jax.ShapeDtypeStruct((1, c), F32),
                   jax.ShapeDtypeStruct((1, c), F32)],
        scratch_shapes=[pltpu.VMEM((hb + ts, c), F32), pltpu.VMEM((hb + ts, c), F32),
                        pltpu.VMEM((ts + hb, c), F32), pltpu.VMEM((ts + hb, c), F32),
                        pltpu.VMEM((8, ts + hb - 8, c), F32)],
        compiler_params=_params(("arbitrary",)),
    )(z, z, z, ca, ca, dab, dab, conv_a, ln_g, ln_b, conv_b)


def _ffn_up(x, g, w8, *, name, tm=2048):
    s, d = x.shape
    nb, c, _ = w8.shape
    tm = min(tm, s)

    def body(x_ref, g_ref, w_ref, h_ref, u_ref, hs_ref):
        @pl.when(pl.program_id(1) == 0)
        def _():
            hv = _rms_rows(x_ref[...], g_ref[...]).astype(BF16)
            hs_ref[...] = hv
            h_ref[...] = hv

        u_ref[...] = lax.dot_general(hs_ref[...], w_ref[...], _NT, preferred_element_type=F32).astype(BF16)

    return pl.pallas_call(
        body, name=name, grid=(s // tm, nb),
        in_specs=[pl.BlockSpec((tm, d), lambda i, k: (i, 0)),
                  pl.BlockSpec((1, d), lambda i, k: (0, 0)),
                  pl.BlockSpec((None, c, d), lambda i, k: (k, 0, 0))],
        out_specs=[pl.BlockSpec((tm, d), lambda i, k: (i, 0)),
                   pl.BlockSpec((None, tm, c), lambda i, k: (k, i, 0))],
        out_shape=[jax.ShapeDtypeStruct((s, d), BF16), jax.ShapeDtypeStruct((nb, s, c), BF16)],
        scratch_shapes=[pltpu.VMEM((tm, d), BF16)],
        compiler_params=_params(("parallel", "arbitrary")),
    )(x, g, w8)


def _ffn_mid_down(up8, wc8, wd4, x, head=None, *, name, ts=256, rc=32):
    nb, s, c = up8.shape
    d = x.shape[1]
    hb = HALO_S
    ts = min(ts, s)

    def body(*refs):
        u_ref, up_ref, wc_ref, wd_ref, x_ref = refs[:5]
        n_out = 4 if head is None else 7
        outs = refs[-1 - n_out:-1]
        act_ref, ug_ref, uv_ref = outs[-3:]
        ext = refs[-1]
        keep = jnp.where(pl.program_id(0) > 0, 1.0, 0.0)
        acc = x_ref[...]
        for j in range(N_PAIR):
            eg, ev = ext.at[2 * (j % 2)], ext.at[2 * (j % 2) + 1]
            eg[0:hb, :] = up_ref[j].astype(F32) * keep
            ev[0:hb, :] = up_ref[j + N_PAIR].astype(F32) * keep
            eg[hb:hb + ts, :] = u_ref[j].astype(F32)
            ev[hb:hb + ts, :] = u_ref[j + N_PAIR].astype(F32)
            for q in range(ts // rc):
                r0 = q * rc
                gg = _conv_taps(eg, wc_ref.at[j], 3, hb - 2, r0, rc)
                vv = _conv_taps(ev, wc_ref.at[j + N_PAIR], 3, hb - 2, r0, rc)
                ug_ref[j, r0:r0 + rc, :] = gg.astype(BF16)
                uv_ref[j, r0:r0 + rc, :] = vv.astype(BF16)
                act_ref[j, r0:r0 + rc, :] = (gg * _sigmoid(gg) * vv).astype(BF16)
            acc = acc + jnp.dot(act_ref[j], wd_ref[j], preferred_element_type=F32)
        if head is None:
            outs[0][...] = acc
            return
        g_ref, t_ref = refs[5], refs[6]
        loss_ref, dx_ref, dxb_ref, dg_ref = outs[:4]

        @pl.when(pl.program_id(0) == 0)
        def _():
            loss_ref[...] = jnp.zeros_like(loss_ref)
            dg_ref[...] = jnp.zeros_like(dg_ref)

        gv = g_ref[...]
        r = lax.rsqrt(jnp.mean(acc * acc, axis=-1, keepdims=True) + RMS_EPS)
        xh = acc * r
        err = xh * gv - t_ref[...]
        loss_ref[...] += 0.5 * jnp.sum(jnp.mean(err * err, axis=-1, keepdims=True), axis=0, keepdims=True)
        dy = err * (1.0 / d)
        dn = dy * gv
        dx = r * (dn - xh * jnp.mean(dn * xh, axis=-1, keepdims=True))
        dx_ref[...] = dx
        dxb_ref[...] = dx.astype(BF16)
        dg_ref[...] += jnp.sum(dy * xh, axis=0, keepdims=True)

    row = pl.BlockSpec((ts, d), lambda i: (i, 0))
    vec = pl.BlockSpec((1, d), lambda i: (0, 0))
    tile = pl.BlockSpec((N_PAIR, ts, c), lambda i: (0, i, 0))
    half = jax.ShapeDtypeStruct((N_PAIR, s, c), BF16)
    in_specs = [pl.BlockSpec((nb, ts, c), lambda i: (0, i, 0)),
                pl.BlockSpec((nb, hb, c), lambda i: (0, _prev_blk(i, ts, hb), 0)),
                pl.BlockSpec((nb, 8, c), lambda i: (0, 0, 0)),
                pl.BlockSpec((N_PAIR, c, d), lambda i: (0, 0, 0)), row]
    args = [up8, up8, wc8, wd4, x]
    if head is None:
        out_specs = [row, tile, tile, tile]
        out_shape = [jax.ShapeDtypeStruct((s, d), F32), half, half, half]
    else:
        in_specs += [vec, row]
        args += list(head)
        out_specs = [pl.BlockSpec((1, 1), lambda i: (0, 0)), row, row, vec, tile, tile, tile]
        out_shape = [jax.ShapeDtypeStruct((1, 1), F32), jax.ShapeDtypeStruct((s, d), F32),
                     jax.ShapeDtypeStruct((s, d), BF16), jax.ShapeDtypeStruct((1, d), F32), half, half, half]
    return pl.pallas_call(
        body, name=name, grid=(s // ts,), in_specs=in_specs, out_specs=out_specs, out_shape=out_shape,
        scratch_shapes=[pltpu.VMEM((4, hb + ts, c), F32)],
        compiler_params=_params(("parallel",) if head is None else ("arbitrary",)),
    )(*args)


def _ffn_dwdown(act4, db, *, name):
    npair, s, c = act4.shape
    d = db.shape[1]

    def body(a_ref, d_ref, o_ref):
        o_ref[...] = lax.dot_general(a_ref[...], d_ref[...], (((0,), (0,)), ((), ())),
                                     preferred_element_type=F32).astype(BF16)

    return pl.pallas_call(
        body, name=name, grid=(npair,),
        in_specs=[pl.BlockSpec((None, s, c), lambda j: (j, 0, 0)),
                  pl.BlockSpec((s, d), lambda j: (0, 0))],
        out_specs=pl.BlockSpec((None, c, d), lambda j: (j, 0, 0)),
        out_shape=jax.ShapeDtypeStruct((npair, c, d), BF16),
        compiler_params=_params(("parallel",)),
    )(act4, db)


def _ffn_midbwd(db, wd4, up8, ug4, uv4, wc8, *, name, ts=256, rc=32):
    nb, s, c = up8.shape
    d = db.shape[1]
    hb = HALO_S
    ts = min(ts, s)
    n_i = s // ts

    def body(db_ref, dbn_ref, wd_ref, p_ref, ug_ref, ugn_ref, uv_ref, uvn_ref, wc_ref,
             dg_ref, dv_ref, dwg_ref, dwv_ref, dact, extd):
        i = pl.program_id(0)
        keep_n = jnp.where(i < n_i - 1, 1.0, 0.0)

        @pl.when(i == 0)
        def _():
            dwg_ref[...] = jnp.zeros_like(dwg_ref)
            dwv_ref[...] = jnp.zeros_like(dwv_ref)

        def fold(p):
            acc = p[0:8, :]
            for r in range(8, rc, 8):
                acc = acc + p[r:r + 8, :]
            return acc

        for j in range(N_PAIR):
            da = dact.at[j % 2]
            extdg, extdv = extd.at[2 * (j % 2)], extd.at[2 * (j % 2) + 1]
            da[0:ts, :] = lax.dot_general(db_ref[...], wd_ref[j], _NT, preferred_element_type=F32)
            da[ts:ts + hb, :] = lax.dot_general(dbn_ref[...], wd_ref[j], _NT, preferred_element_type=F32) * keep_n

            def du_rows(r0, rows, gg, vv, dav):
                sg = _sigmoid(gg)
                extdg[r0:r0 + rows, :] = dav * vv * (sg * (1.0 + gg * (1.0 - sg)))
                extdv[r0:r0 + rows, :] = dav * (gg * sg)

            for q in range(ts // rc):
                rows = slice(q * rc, q * rc + rc)
                du_rows(q * rc, rc, ug_ref[j, rows, :].astype(F32), uv_ref[j, rows, :].astype(F32), da[rows, :])
            du_rows(ts, 8, ugn_ref[j, 0:8, :].astype(F32), uvn_ref[j, 0:8, :].astype(F32), da[ts:ts + 8, :])

            for extdu, k_blk, out_ref, dw_ref in ((extdg, j, dg_ref, dwg_ref), (extdv, j + N_PAIR, dv_ref, dwv_ref)):
                part = [None, None, None]
                for q in range(ts // rc):
                    r0 = q * rc
                    pre = p_ref[k_blk, r0:r0 + rc, :].astype(F32)
                    dup = None
                    for k in range(3):
                        sh = extdu[pl.ds(r0 + 2 - k, rc), :]
                        term = wc_ref[k_blk, k:k + 1, :] * sh
                        dup = term if dup is None else dup + term
                        prod = fold(pre * sh)
                        part[k] = prod if part[k] is None else part[k] + prod
                    out_ref[j, r0:r0 + rc, :] = dup.astype(BF16)
                for k in range(3):
                    dw_ref[j, k:k + 1, :] += jnp.sum(part[k], axis=0, keepdims=True)

    tile = pl.BlockSpec((N_PAIR, ts, c), lambda i: (0, i, 0))
    halo = pl.BlockSpec((N_PAIR, hb, c), lambda i: (0, _next_blk(i, ts, hb, s), 0))
    acc = pl.BlockSpec((N_PAIR, 8, c), lambda i: (0, 0, 0))
    half = jax.ShapeDtypeStruct((N_PAIR, s, c), BF16)
    taps = jax.ShapeDtypeStruct((N_PAIR, 8, c), F32)
    return pl.pallas_call(
        body, name=name, grid=(n_i,),
        in_specs=[pl.BlockSpec((ts, d), lambda i: (i, 0)),
                  pl.BlockSpec((hb, d), lambda i: (_next_blk(i, ts, hb, s), 0)),
                  pl.BlockSpec((N_PAIR, c, d), lambda i: (0, 0, 0)),
                  pl.BlockSpec((nb, ts, c), lambda i: (0, i, 0)),
                  tile, halo, tile, halo,
                  pl.BlockSpec((nb, 8, c), lambda i: (0, 0, 0))],
        out_specs=[tile, tile, acc, acc], out_shape=[half, half, taps, taps],
        scratch_shapes=[pltpu.VMEM((2, ts + hb, c), F32), pltpu.VMEM((4, ts + 8, c), F32)],
        compiler_params=_params(("arbitrary",)),
    )(db, db, wd4, up8, ug4, ug4, uv4, uv4, wc8)


def _ffn_dh(dupg, dupv, w8, x, g, dres, dep, *, name, tm=512):
    npair, s, c = dupg.shape
    d = x.shape[1]
    tm = min(tm, s)

    def body(dg_ref, dv_ref, w_ref, x_ref, g_ref, dres_ref, dep_ref, dx_ref, dxb_ref, dgain_ref):
        @pl.when(pl.program_id(0) == 0)
        def _():
            dgain_ref[...] = jnp.zeros_like(dgain_ref)

        dh = None
        for j in range(npair):
            for src, k in ((dg_ref, j), (dv_ref, j + npair)):
                part = jnp.dot(src[j], w_ref[k], preferred_element_type=F32)
                dh = part if dh is None else dh + part
        dx, dgr = _rms_bwd_rows(x_ref[...], g_ref[...], dh)
        dx = dres_ref[...] + dx
        dx_ref[...] = dx
        dxb_ref[...] = dx.astype(BF16)
        dgain_ref[...] += jnp.sum(dgr, axis=0, keepdims=True)

    row = pl.BlockSpec((tm, d), lambda i: (i, 0))
    vec = pl.BlockSpec((1, d), lambda i: (0, 0))
    dup = pl.BlockSpec((npair, tm, c), lambda i: (0, i, 0))
    return pl.pallas_call(
        body, name=name, grid=(s // tm,),
        in_specs=[dup, dup, pl.BlockSpec((2 * npair, c, d), lambda i: (0, 0, 0)), row, vec, row,
                  pl.BlockSpec(memory_space=pl.ANY)],
        out_specs=[row, row, vec],
        out_shape=[jax.ShapeDtypeStruct((s, d), F32), jax.ShapeDtypeStruct((s, d), BF16),
                   jax.ShapeDtypeStruct((1, d), F32)],
        compiler_params=_params(("arbitrary",)),
    )(dupg, dupv, w8, x, g, dres, dep)


def _ffn_dwup(h, dupg, dupv, *, name, tm=512):
    npair, s, c = dupg.shape
    d = h.shape[1]

    def body(h_ref, dg_ref, dv_ref, o_ref):
        k = pl.program_id(1)

        @pl.when(k < npair)
        def _():
            o_ref[...] = lax.dot_general(dg_ref[...], h_ref[...], _TN, preferred_element_type=F32).astype(BF16)

        @pl.when(k >= npair)
        def _():
            o_ref[...] = lax.dot_general(dv_ref[...], h_ref[...], _TN, preferred_element_type=F32).astype(BF16)

    return pl.pallas_call(
        body, name=name, grid=(d // tm, 2 * npair),
        in_specs=[pl.BlockSpec((s, tm), lambda m, k: (0, m)),
                  pl.BlockSpec((None, s, c), lambda m, k: (jnp.minimum(k, npair - 1), 0, 0)),
                  pl.BlockSpec((None, s, c), lambda m, k: (jnp.maximum(k - npair, 0), 0, 0))],
        out_specs=pl.BlockSpec((None, c, tm), lambda m, k: (k, 0, m)),
        out_shape=jax.ShapeDtypeStruct((2 * npair, c, d), BF16),
        compiler_params=_params(("parallel", "arbitrary")),
    )(h, dupg, dupv)


def _pool_counts(i, ts, rows, window):
    t = lax.broadcasted_iota(jnp.int32, (rows, 1), 0) + i * ts + 1
    return jnp.minimum(t, window).astype(F32)


def _pool_fwd(x, g, *, name, ts=256):
    s, d = x.shape
    hb = HALO_A
    pg = POOL_GROUP
    ts = min(ts, s)

    def body(x_ref, xp_ref, g_ref, p_ref, ext, sa, sb):
        i = pl.program_id(0)
        keep = jnp.where(i > 0, 1.0, 0.0)
        gv = g_ref[...]
        ext[0:hb, :] = _rms_rows(xp_ref[...], gv) * keep
        ext[hb:hb + ts, :] = _rms_rows(x_ref[...], gv)
        rows = hb + ts - 8
        sa[0:8, :] = jnp.zeros((8, d), F32)
        sb[0:8, :] = jnp.zeros((8, d), F32)
        for gi, w in enumerate(POOL_WINDOWS):
            cols = slice(gi * pg, (gi + 1) * pg)
            cur, nxt, k = ext, sa, 1
            while k < w:
                nxt[8:8 + rows, cols] = cur[8:8 + rows, cols] + cur[pl.ds(8 - k, rows), cols]
                cur, nxt, k = nxt, (sb if nxt is sa else sa), 2 * k
            h = ext[hb:hb + ts, cols]
            p_ref[:, cols] = (cur[hb:hb + ts, cols] / _pool_counts(i, ts, ts, w) - h).astype(BF16)

    return pl.pallas_call(
        body, name=name, grid=(s // ts,),
        in_specs=[pl.BlockSpec((ts, d), lambda i: (i, 0)),
                  pl.BlockSpec((hb, d), lambda i: (_prev_blk(i, ts, hb), 0)),
                  pl.BlockSpec((1, d), lambda i: (0, 0))],
        out_specs=pl.BlockSpec((ts, d), lambda i: (i, 0)),
        out_shape=jax.ShapeDtypeStruct((s, d), BF16),
        scratch_shapes=[pltpu.VMEM((hb + ts, d), F32)] * 3,
        compiler_params=_params(("parallel",)),
    )(x, x, g)


def _pool_mm_fwd(p, w_pool, scale, x, *, name, ts=1024):
    s, d = x.shape
    pg = POOL_GROUP
    ts = min(ts, s)

    def body(p_ref, w_ref, s_ref, x_ref, o_ref, yu_ref):
        yu = jnp.dot(p_ref[...], w_ref[...], preferred_element_type=F32)
        yu_ref[...] = yu.astype(BF16)
        o_ref[...] = x_ref[...] + yu * s_ref[...]

    blk = pl.BlockSpec((ts, pg), lambda i, gi: (i, gi))
    return pl.pallas_call(
        body, name=name, grid=(s // ts, d // pg),
        in_specs=[blk, pl.BlockSpec((None, pg, pg), lambda i, gi: (gi, 0, 0)),
                  pl.BlockSpec((1, pg), lambda i, gi: (0, gi)), blk],
        out_specs=[blk, blk],
        out_shape=[jax.ShapeDtypeStruct((s, d), F32), jax.ShapeDtypeStruct((s, d), BF16)],
        compiler_params=_params(("parallel", "parallel")),
    )(p, w_pool, scale, x)


def _pool_mm_bwd(dres, w_pool, scale, yu, *, name, ts=1024):
    s, d = dres.shape
    pg = POOL_GROUP
    ts = min(ts, s)

    def body(d_ref, w_ref, s_ref, yu_ref, dyc_ref, dp_ref, ds_ref):
        @pl.when(pl.program_id(1) == 0)
        def _():
            ds_ref[...] = jnp.zeros_like(ds_ref)

        dv = d_ref[...]
        dyc = (dv * s_ref[...]).astype(BF16)
        dyc_ref[...] = dyc
        dp_ref[...] = lax.dot_general(dyc, w_ref[...], (((1,), (1,)), ((), ())), preferred_element_type=F32)
        ds_ref[...] += jnp.sum(dv * yu_ref[...].astype(F32), axis=0, keepdims=True)

    blk = pl.BlockSpec((ts, pg), lambda gi, i: (i, gi))
    vec = pl.BlockSpec((1, pg), lambda gi, i: (0, gi))
    return pl.pallas_call(
        body, name=name, grid=(d // pg, s // ts),
        in_specs=[blk, pl.BlockSpec((None, pg, pg), lambda gi, i: (gi, 0, 0)), vec, blk],
        out_specs=[blk, blk, vec],
        out_shape=[jax.ShapeDtypeStruct((s, d), BF16), jax.ShapeDtypeStruct((s, d), F32),
                   jax.ShapeDtypeStruct((1, d), F32)],
        compiler_params=_params(("parallel", "arbitrary")),
    )(dres, w_pool, scale, yu)


def _pool_dw(p, dyc, *, name):
    s, d = p.shape
    pg = POOL_GROUP

    def body(p_ref, d_ref, o_ref):
        o_ref[...] = lax.dot_general(p_ref[...], d_ref[...], (((0,), (0,)), ((), ())),
                                     preferred_element_type=F32).astype(BF16)

    blk = pl.BlockSpec((s, pg), lambda gi: (0, gi))
    return pl.pallas_call(
        body, name=name, grid=(d // pg,),
        in_specs=[blk, blk], out_specs=pl.BlockSpec((None, pg, pg), lambda gi: (gi, 0, 0)),
        out_shape=jax.ShapeDtypeStruct((d // pg, pg, pg), BF16),
        compiler_params=_params(("parallel",)),
    )(p, dyc)


def _pool_bwd(dp, x, g, dres, *, name, ts=256):
    s, d = x.shape
    hb = HALO_A
    pg = POOL_GROUP
    ts = min(ts, s)
    n_i = s // ts

    def body(dp_ref, dpn_ref, x_ref, g_ref, dres_ref, dx_ref, dxb_ref, dg_ref, ext, dh, sa, sb):
        i = pl.program_id(0)
        keep_n = jnp.where(i < n_i - 1, 1.0, 0.0)

        @pl.when(i == 0)
        def _():
            dg_ref[...] = jnp.zeros_like(dg_ref)

        for gi, w in enumerate(POOL_WINDOWS):
            cols = slice(gi * pg, (gi + 1) * pg)
            ext[0:ts, cols] = dp_ref[:, cols] / _pool_counts(i, ts, ts, w)
            ext[ts:ts + hb, cols] = dpn_ref[:, cols] / _pool_counts(i + 1, ts, hb, w) * keep_n
        rows = ts + hb - 8
        sa[rows:rows + 8, :] = jnp.zeros((8, d), F32)
        sb[rows:rows + 8, :] = jnp.zeros((8, d), F32)
        for gi, w in enumerate(POOL_WINDOWS):
            cols = slice(gi * pg, (gi + 1) * pg)
            cur, nxt, k = ext, sa, 1
            while k < w:
                nxt[0:rows, cols] = cur[0:rows, cols] + cur[pl.ds(k, rows), cols]
                cur, nxt, k = nxt, (sb if nxt is sa else sa), 2 * k
            dh[:, cols] = cur[0:ts, cols] - dp_ref[:, cols]
        dx, dgr = _rms_bwd_rows(x_ref[...], g_ref[...], dh[...])
        dx = dres_ref[...] + dx
        dx_ref[...] = dx
        dxb_ref[...] = dx.astype(BF16)
        dg_ref[...] += jnp.sum(dgr, axis=0, keepdims=True)

    row = pl.BlockSpec((ts, d), lambda i: (i, 0))
    vec = pl.BlockSpec((1, d), lambda i: (0, 0))
    return pl.pallas_call(
        body, name=name, grid=(n_i,),
        in_specs=[row, pl.BlockSpec((hb, d), lambda i: (_next_blk(i, ts, hb, s), 0)), row, vec, row],
        out_specs=[row, row, vec],
        out_shape=[jax.ShapeDtypeStruct((s, d), F32), jax.ShapeDtypeStruct((s, d), BF16),
                   jax.ShapeDtypeStruct((1, d), F32)],
        scratch_shapes=[pltpu.VMEM((ts + hb, d), F32), pltpu.VMEM((ts, d), F32),
                        pltpu.VMEM((ts + hb, d), F32), pltpu.VMEM((ts + hb, d), F32)],
        compiler_params=_params(("arbitrary",)),
    )(dp, dp, x, g, dres)


def _pad_rows(w, rows):
    pad = [(0, 0)] * (w.ndim - 2) + [(0, rows - w.shape[-2]), (0, 0)]
    return jnp.pad(w, pad)


def _ffn_layer_fwd(x, nf, w8, wc8, wd4, tag, head=None):
    h, up8 = _ffn_up(x, nf, w8, name=f"ffn{tag}_up")
    res = _ffn_mid_down(up8, wc8, wd4, x, head, name=f"ffn{tag}_mid_down")
    act4, ug4, uv4 = res[-3:]
    out = res[0] if head is None else tuple(res[:4])
    return out, (h, up8, ug4, uv4, act4)


def _ffn_layer_bwd(d, db, x, nf, w8, wc8, wd4, saved, tag, grads_ready):
    h, up8, ug4, uv4, act4 = saved
    dwd4 = _ffn_dwdown(act4, db, name=f"ffn{tag}_dwdown")
    dupg, dupv, dwg, dwv = _ffn_midbwd(db, wd4, up8, ug4, uv4, wc8, name=f"ffn{tag}_midbwd")
    dw8 = _ffn_dwup(h, dupg, dupv, name=f"ffn{tag}_dwup")
    sent = grads_ready(f"ffn{tag}", {f"w_up{tag}": dw8, f"w_down{tag}": dwd4})
    dx, dxb, dnf = _ffn_dh(dupg, dupv, w8, x, nf, d, sent, name=f"ffn{tag}_dh")
    dwc8 = jnp.concatenate([dwg, dwv], axis=0)[:, :3]
    return dx, dxb, dnf, dwc8


def _local_step(x, target, weights, grads_ready):
    w0 = weights("mix0", x)
    wa = _pad_rows(w0["conv_a"], 32)
    wb = _pad_rows(w0["conv_b"], 8)
    wc = [_pad_rows(w0["conv_ffn"][l], 8) for l in range(2)]
    h0, z = _rms_mm(x, w0["norm_mix_even"], w0["w_in_t"], name="mix0_in", out_dtype=F32, tn=2560)
    ab, ca = _mix0_fwd(z, wa, w0["ln_a_g"], w0["ln_a_b"], wb, name="mix0_mid")
    x1 = _mm(ab, w0["w_out"], add=x, name="mix0_out", tm=1024, tn=1024)
    w1 = weights("ffn0", x1)
    x2, ffn0 = _ffn_layer_fwd(x1, w0["norm_ffn"][0:1], w1["w_up"], wc[0], w1["w_down"], 0)
    w2 = weights("ffn1", x2)
    p = _pool_fwd(x2, w0["norm_mix_odd"], name="pool_mid")
    x3, yu = _pool_mm_fwd(p, w2["w_pool"], w0["pool_scale"], x2, name="pool_mm")
    (loss, d4, d4b, g_norm_final), ffn1 = _ffn_layer_fwd(
        x3, w0["norm_ffn"][1:2], w2["w_up"], wc[1], w2["w_down"], 1, head=(w0["norm_final"], target))

    d3, d3b, g_nf1, g_wc1 = _ffn_layer_bwd(
        d4, d4b, x3, w0["norm_ffn"][1:2], w2["w_up"], wc[1], w2["w_down"], ffn1, 1, grads_ready)
    dyc, dp, g_scale = _pool_mm_bwd(d3, w2["w_pool"], w0["pool_scale"], yu, name="pool_mm_bwd")
    g_pool = _pool_dw(p, dyc, name="pool_dw")
    d2, d2b, g_nmo = _pool_bwd(dp, x2, w0["norm_mix_odd"], d3, name="pool_midbwd")
    d1, d1b, g_nf0, g_wc0 = _ffn_layer_bwd(
        d2, d2b, x1, w0["norm_ffn"][0:1], w1["w_up"], wc[0], w1["w_down"], ffn0, 0, grads_ready)
    dab = _mm(d1b, w0["w_out"], tb=True, name="mix0_dab", tm=1024, tn=1024)
    g_out = _mm(ab, d1b, ta=True, out_dtype=BF16, name="mix0_dwout", tm=1024, tn=512)
    dz, g_wa, g_wb, g_lg, g_lb = _mix0_bwd(z, ca, dab, wa, w0["ln_a_g"], w0["ln_a_b"], wb, name="mix0_midbwd")
    g_in = _mm(dz, h0, ta=True, out_dtype=BF16, name="mix0_dwin", tm=512, tn=1024)
    sent_mix = grads_ready("mix0", {"w_in": g_in, "w_out": g_out, "w_pool": g_pool})
    dx, g_nme = _mm_rms_bwd(dz, w0["w_in_t"], x, w0["norm_mix_even"], d1, sent_mix, name="mix0_dh")

    small = {
        "norm_mix_even": g_nme, "conv_a": g_wa[:A_TAPS], "ln_a_g": g_lg, "ln_a_b": g_lb, "conv_b": g_wb[:3],
        "norm_mix_odd": g_nmo, "pool_scale": g_scale, "norm_ffn": jnp.concatenate([g_nf0, g_nf1], axis=0),
        "conv_ffn": [g_wc0, g_wc1], "norm_final": g_norm_final,
    }
    return loss[0, 0], dx, small


def _my_pos():
    return lax.axis_index("x"), lax.axis_index("y"), lax.axis_index("c")


def _flip(pos, r):
    x, y, c = pos
    return (1 - x if r & 4 else x, 1 - y if r & 2 else y, 1 - c if r & 1 else c)


def _dev_index(pos):
    return 4 * pos[0] + 2 * pos[1] + pos[2]


_HBM = pl.BlockSpec(memory_space=pltpu.HBM)
_SEM = pl.BlockSpec(memory_space=pltpu.SEMAPHORE)
_EFFECT = pltpu.SideEffectType.DATAFLOW_SIDE_EFFECTING


def _exchange_copy(ins, lands, send_sems, recv_sems, scatter, pos, a, r, receiving):
    me = _dev_index(pos)
    peer = _flip(pos, r)
    dest = _dev_index(pos) if receiving else _dev_index(peer)
    src = ins[a].at[dest] if scatter[a] else ins[a]
    slot = _dev_index(peer) if receiving else me
    return pltpu.make_async_remote_copy(
        src_ref=src, dst_ref=lands[a].at[slot], send_sem=send_sems.at[a * (N_DEV - 1) + r - 1],
        recv_sem=recv_sems.at[a * (N_DEV - 1) + r - 1],
        device_id=peer, device_id_type=pl.DeviceIdType.MESH)


ALL_PEERS = tuple(range(1, N_DEV))
CHIP_PEERS = (1, 2, 4, 6)
FORWARDED = (2, 4, 6)


def _exchange_start(arrays, scatter, after, *, name, peers=ALL_PEERS):
    n = len(arrays)
    me = _dev_index(_my_pos())
    lands = []
    for arr, sc in zip(arrays, scatter):
        own = lax.dynamic_index_in_dim(arr, me, 0, keepdims=True) if sc else arr[None]
        shape = arr.shape if sc else (N_DEV,) + arr.shape
        lands.append(lax.dynamic_update_slice(lax.empty(shape, arr.dtype), own, (me,) + (0,) * (len(shape) - 1)))

    def body(*refs):
        ins, lnd = refs[:n], refs[n:2 * n]
        send_sems, recv_sems = refs[2 * n + 1], refs[2 * n + 2]
        token = refs[-1]
        pos = _my_pos()
        for a in range(n):
            for r in peers:
                _exchange_copy(ins, lnd, send_sems, recv_sems, scatter, pos, a, r, receiving=False).start()
        token[...] = jnp.zeros_like(token)

    bufs = [pltpu.with_memory_space_constraint(t, pltpu.HBM) for t in list(arrays) + lands]
    sems = pltpu.SemaphoreType.DMA((n * (N_DEV - 1),))
    res = pl.pallas_call(
        body, name=name,
        out_shape=(sems, sems, *[pltpu.HBM(t.shape, t.dtype) for t in bufs], jax.ShapeDtypeStruct((8, 128), F32)),
        in_specs=[_HBM] * (2 * n) + [pl.BlockSpec(memory_space=pl.ANY)],
        out_specs=(_SEM, _SEM, *[_HBM] * (2 * n), pl.BlockSpec(memory_space=pltpu.VMEM)),
        input_output_aliases={i: 2 + i for i in range(2 * n)},
        compiler_params=pltpu.CompilerParams(has_side_effects=_EFFECT),
    )(*bufs, after)
    return res[0], res[1], list(res[2:2 + n]), list(res[2 + n:2 + 2 * n]), res[-1]


def _exchange_wait(started, scatter, after, *, name, peers=ALL_PEERS):
    send_sems, recv_sems, arrays, lands, _ = started
    n = len(arrays)

    def body(*refs):
        ins, lnd = refs[:n], refs[n:2 * n]
        send_sems, recv_sems = refs[2 * n], refs[2 * n + 1]
        pos = _my_pos()
        for a in range(n):
            for r in peers:
                _exchange_copy(ins, lnd, send_sems, recv_sems, scatter, pos, a, r, receiving=False).wait_send()
                _exchange_copy(ins, lnd, send_sems, recv_sems, scatter, pos, a, r, receiving=True).wait_recv()

    bufs = list(arrays) + list(lands)
    after = list(after) if isinstance(after, (list, tuple)) else [after]
    res = pl.pallas_call(
        body, name=name,
        out_shape=tuple(pltpu.HBM(t.shape, t.dtype) for t in bufs),
        in_specs=[_HBM] * (2 * n) + [_SEM, _SEM] + [pl.BlockSpec(memory_space=pl.ANY)] * len(after),
        out_specs=tuple([_HBM] * (2 * n)),
        input_output_aliases={i: i for i in range(2 * n)},
        compiler_params=pltpu.CompilerParams(has_side_effects=_EFFECT),
    )(*bufs, send_sems, recv_sems, *after)
    return list(res[n:])


def _forward_copy(lands, send_sems, recv_sems, pos, a, q, receiving):
    slot = _dev_index(_flip(pos, q ^ 1 if receiving else q))
    idx = a * len(FORWARDED) + FORWARDED.index(q)
    return pltpu.make_async_remote_copy(
        src_ref=lands[a].at[slot], dst_ref=lands[a].at[slot], send_sem=send_sems.at[idx], recv_sem=recv_sems.at[idx],
        device_id=_flip(pos, 1), device_id_type=pl.DeviceIdType.MESH)


def _forward_start(lands, after, *, name):
    n = len(lands)

    def body(*refs):
        lnd = refs[:n]
        send_sems, recv_sems = refs[n + 1], refs[n + 2]
        token = refs[-1]
        pos = _my_pos()
        for a in range(n):
            for q in FORWARDED:
                _forward_copy(lnd, send_sems, recv_sems, pos, a, q, receiving=False).start()
        token[...] = jnp.zeros_like(token)

    sems = pltpu.SemaphoreType.DMA((n * len(FORWARDED),))
    res = pl.pallas_call(
        body, name=name,
        out_shape=(sems, sems, *[pltpu.HBM(t.shape, t.dtype) for t in lands], jax.ShapeDtypeStruct((8, 128), F32)),
        in_specs=[_HBM] * n + [pl.BlockSpec(memory_space=pl.ANY)],
        out_specs=(_SEM, _SEM, *[_HBM] * n, pl.BlockSpec(memory_space=pltpu.VMEM)),
        input_output_aliases={i: 2 + i for i in range(n)},
        compiler_params=pltpu.CompilerParams(has_side_effects=_EFFECT),
    )(*lands, after)
    return res[0], res[1], list(res[2:2 + n]), res[-1]


def _forward_wait(forwarded, after, *, name):
    send_sems, recv_sems, lands, _ = forwarded
    n = len(lands)

    def body(*refs):
        lnd = refs[:n]
        send_sems, recv_sems = refs[n], refs[n + 1]
        pos = _my_pos()
        for a in range(n):
            for q in FORWARDED:
                _forward_copy(lnd, send_sems, recv_sems, pos, a, q, receiving=False).wait_send()
                _forward_copy(lnd, send_sems, recv_sems, pos, a, q, receiving=True).wait_recv()

    res = pl.pallas_call(
        body, name=name,
        out_shape=tuple(pltpu.HBM(t.shape, t.dtype) for t in lands),
        in_specs=[_HBM] * n + [_SEM, _SEM, pl.BlockSpec(memory_space=pl.ANY)],
        out_specs=tuple([_HBM] * n),
        input_output_aliases={i: i for i in range(n)},
        compiler_params=pltpu.CompilerParams(has_side_effects=_EFFECT),
    )(*lands, send_sems, recv_sems, after)
    return list(res)


def _adamw_update(p_ref, w_ref, m_ref, v_ref, g_ref, d_ref, mo_ref, vo_ref):
    g = p_ref[0].astype(F32)
    for k in range(1, N_DEV):
        g = g + p_ref[k].astype(F32)
    mn = ADAM_B1 * m_ref[...] + (1.0 - ADAM_B1) * g
    vn = ADAM_B2 * v_ref[...] + (1.0 - ADAM_B2) * (g * g)
    m_hat = mn / (1.0 - ADAM_B1 ** ADAM_STEP)
    v_hat = vn / (1.0 - ADAM_B2 ** ADAM_STEP)
    g_ref[...] = g
    d_ref[...] = -ADAM_LR * (m_hat / (jnp.sqrt(v_hat) + ADAM_EPS) + ADAM_WD * w_ref[...])
    mo_ref[...] = mn
    vo_ref[...] = vn


def _adamw_small(parts, ws, ms, vs, *, name):
    n = len(ws)

    def body(*refs):
        ins, outs = refs[:4 * n], refs[4 * n:]
        for j in range(n):
            _adamw_update(ins[j], ins[n + j], ins[2 * n + j], ins[3 * n + j], *outs[4 * j:4 * j + 4])

    return pl.pallas_call(
        body, name=name,
        out_shape=[jax.ShapeDtypeStruct(w.shape, F32) for w in ws for _ in range(4)],
    )(*parts, *ws, *ms, *vs)


def _adamw(parts, w, m, v, *, name, tr):
    nl, r, c = w.shape
    assert len(parts) == nl and r % tr == 0
    n_i = r // tr

    def body(*refs):
        p_refs = refs[:nl]
        w_ref, m_ref, v_ref = refs[nl:nl + 3]

        def update(p_ref):
            _adamw_update(p_ref, w_ref, m_ref, v_ref, *refs[nl + 3:])

        if nl == 1:
            update(p_refs[0])
        else:
            for layer in range(nl):
                pl.when(pl.program_id(0) == layer)(lambda p_ref=p_refs[layer]: update(p_ref))

    def parts_spec(layer):
        def index(l, i):
            return (0, jnp.where(l < layer, 0, jnp.where(l > layer, n_i - 1, i)), 0)
        return pl.BlockSpec((N_DEV, tr, c), index)

    blk = pl.BlockSpec((None, tr, c), lambda l, i: (l, i, 0))
    return pl.pallas_call(
        body, name=name, grid=(nl, n_i),
        in_specs=[parts_spec(layer) for layer in range(nl)] + [blk, blk, blk],
        out_specs=[blk] * 4, out_shape=[jax.ShapeDtypeStruct((nl, r, c), F32)] * 4,
        compiler_params=_params(("arbitrary", "arbitrary")),
    )(*parts, w, m, v)


def _pack(parts, lead=()):
    flat = jnp.concatenate([p.reshape(lead + (-1,)) for p in parts], axis=-1)
    n = flat.shape[-1]
    rows = -(-n // (8 * 128)) * 8
    flat = jnp.pad(flat, [(0, 0)] * len(lead) + [(0, rows * 128 - n)])
    return flat.reshape(lead + (rows, 128))


def _to_dev_major(g, axis):
    shp = g.shape
    g = g.reshape(shp[:axis] + (N_DEV, shp[axis] // N_DEV) + shp[axis + 1:])
    return jnp.moveaxis(g, axis, 0)


def _from_dev_major(g, axis):
    g = jnp.moveaxis(g, 0, axis)
    shp = g.shape
    return g.reshape(shp[:axis] + (shp[axis] * shp[axis + 1],) + shp[axis + 2:])


SMALL_SHARDED = ("conv_a", "conv_b", "norm_mix_odd", "pool_scale", "conv_ffn_w")
SMALL_REPLICATED = ("norm_mix_even", "ln_a_g", "ln_a_b", "norm_ffn", "norm_final")
BIG = {"w_in": ("w_in", 0, 320), "w_out": ("w_out", 0, 128), "w_pool": ("w_pool", 0, 128),
       "w_up0": ("w_up", 0, 352), "w_up1": ("w_up", 1, 352),
       "w_down0": ("w_down", 0, 352), "w_down1": ("w_down", 1, 352)}
COLUMN_SHARDED = ("w_in", "w_up")


def kernel(x, norm_mix_even, w_in, conv_a, ln_a_g, ln_a_b, conv_b, w_out, norm_mix_odd, w_pool, pool_scale, norm_ffn, w_up, conv_ffn_w, w_down, norm_final, loss_target, m_norm_mix_even, m_w_in, m_conv_a, m_ln_a_g, m_ln_a_b, m_conv_b, m_w_out, m_norm_mix_odd, m_w_pool, m_pool_scale, m_norm_ffn, m_w_up, m_conv_ffn_w, m_w_down, m_norm_final, v_norm_mix_even, v_w_in, v_conv_a, v_ln_a_g, v_ln_a_b, v_conv_b, v_w_out, v_norm_mix_odd, v_w_pool, v_pool_scale, v_norm_ffn, v_w_up, v_conv_ffn_w, v_w_down, v_norm_final):
    names = ("norm_mix_even", "w_in", "conv_a", "ln_a_g", "ln_a_b", "conv_b", "w_out", "norm_mix_odd", "w_pool",
             "pool_scale", "norm_ffn", "w_up", "conv_ffn_w", "w_down", "norm_final")
    wts = dict(zip(names, (norm_mix_even, w_in, conv_a, ln_a_g, ln_a_b, conv_b, w_out, norm_mix_odd, w_pool,
                           pool_scale, norm_ffn, w_up, conv_ffn_w, w_down, norm_final)))
    mom = dict(zip(names, (m_norm_mix_even, m_w_in, m_conv_a, m_ln_a_g, m_ln_a_b, m_conv_b, m_w_out, m_norm_mix_odd,
                           m_w_pool, m_pool_scale, m_norm_ffn, m_w_up, m_conv_ffn_w, m_w_down, m_norm_final)))
    var = dict(zip(names, (v_norm_mix_even, v_w_in, v_conv_a, v_ln_a_g, v_ln_a_b, v_conv_b, v_w_out, v_norm_mix_odd,
                           v_w_pool, v_pool_scale, v_norm_ffn, v_w_up, v_conv_ffn_w, v_w_down, v_norm_final)))
    d = x.shape[-1]

    def shard3d(t, param):
        a = t[param]
        if param in COLUMN_SHARDED:
            return jnp.swapaxes(a, 1, 2)
        return a.reshape(a.shape[0], -1, a.shape[-1])

    def unshard3d(a, param):
        if param in COLUMN_SHARDED:
            return jnp.swapaxes(a, 1, 2)
        return a.reshape(wts[param].shape)

    def shard2d(t, key):
        param, layer, _ = BIG[key]
        return shard3d(t, param)[layer]

    small_w = _pack([wts[k] for k in SMALL_SHARDED])
    bf = {k: shard2d(wts, k).astype(BF16) for k in BIG}
    gather_groups = {"mix0": ("w_in", "w_out", "small"), "ffn0": ("w_up0", "w_down0"),
                     "ffn1": ("w_pool", "w_up1", "w_down1")}
    order = list(gather_groups)
    started = {}

    def start_gather(grp, after):
        arrs = [small_w if k == "small" else bf[k] for k in gather_groups[grp]]
        started[grp] = _exchange_start(arrs, [False] * len(arrs), after, name=f"gather_{grp}_start", peers=CHIP_PEERS)

    start_gather(order[0], small_w)

    def weights(grp, after):
        keys = gather_groups[grp]
        lands = _exchange_wait(started[grp], [False] * len(keys), after, name=f"gather_{grp}_wait", peers=CHIP_PEERS)
        forwarded = _forward_start(lands, small_w, name=f"gather_{grp}_forward")
        if grp != order[-1]:
            start_gather(order[order.index(grp) + 1], forwarded[-1])
            behind = started[order[order.index(grp) + 1]][-1]
        else:
            behind = forwarded[-1]
        gw = dict(zip(keys, _forward_wait(forwarded, behind, name=f"gather_{grp}_forward_wait")))
        if grp == "ffn0":
            return {"w_up": gw["w_up0"], "w_down": gw["w_down0"].reshape(N_PAIR, -1, d)}
        if grp == "ffn1":
            return {"w_up": gw["w_up1"], "w_down": gw["w_down1"].reshape(N_PAIR, -1, d),
                    "w_pool": _from_dev_major(gw["w_pool"].reshape(N_DEV, len(POOL_WINDOWS), -1, POOL_GROUP), 1)}
        per_dev = gw["small"].reshape(N_DEV, -1)
        sizes = [wts[k].size for k in SMALL_SHARDED]
        offs = [sum(sizes[:i]) for i in range(len(sizes))]
        small_full = {k: per_dev[:, o:o + n_].reshape((N_DEV,) + wts[k].shape)
                      for k, o, n_ in zip(SMALL_SHARDED, offs, sizes)}
        return {
            "norm_mix_even": norm_mix_even, "ln_a_g": ln_a_g, "ln_a_b": ln_a_b, "norm_ffn": norm_ffn,
            "norm_final": norm_final[None],
            "w_in_t": gw["w_in"].reshape(-1, d),
            "w_out": gw["w_out"].reshape(-1, d),
            "conv_a": _from_dev_major(small_full["conv_a"][:, 0], 1),
            "conv_b": _from_dev_major(small_full["conv_b"][:, 0], 1),
            "norm_mix_odd": _from_dev_major(small_full["norm_mix_odd"], 1),
            "pool_scale": _from_dev_major(small_full["pool_scale"], 1),
            "conv_ffn": [small_full["conv_ffn_w"][:, l] for l in range(2)],
        }

    def dev_major(k, g):
        if k == "w_pool":
            return _to_dev_major(g, 1).reshape(N_DEV, -1, POOL_GROUP)
        return g.reshape(N_DEV, -1, g.shape[-1])

    sent = {}

    def grads_ready(grp, grads):
        keys = tuple(grads)
        parts = [dev_major(k, grads[k]) for k in keys]
        sent[grp] = (keys, _exchange_start(parts, [True] * len(keys), small_w, name=f"grads_{grp}_start"))
        return sent[grp][1][-1]

    loss, dx, g = _local_step(x[0], loss_target[0], weights, grads_ready)

    def small2d(a):
        return a.reshape(-1, a.shape[-1])

    small_keys = SMALL_SHARDED + SMALL_REPLICATED
    small_parts = [_to_dev_major(g["conv_a"], 1), _to_dev_major(g["conv_b"], 1),
                   _to_dev_major(g["norm_mix_odd"], 1), _to_dev_major(g["pool_scale"], 1),
                   jnp.stack(g["conv_ffn"], axis=1).reshape(N_DEV, -1, w_up.shape[-1])]
    small_parts += [g[k] for k in SMALL_REPLICATED]
    small_scatter = [True] * len(SMALL_SHARDED) + [False] * len(SMALL_REPLICATED)
    small_parts.append(jnp.full((1, 128), loss, F32))
    small_scatter.append(False)
    sent["small"] = (small_keys + ("loss",),
                     _exchange_start(small_parts, small_scatter, dx, name="grads_small_start"))

    landed, out, raw = {}, {}, {}

    def wait_grads(grp, after, scat=None):
        keys, st = sent[grp]
        scat = [True] * len(keys) if scat is None else scat
        landed.update(zip(keys, _exchange_wait(st, scat, after, name=f"grads_{grp}_wait")))

    def update(param, keys):
        raw[param] = _adamw([landed[k] for k in keys], shard3d(wts, param), shard3d(mom, param),
                            shard3d(var, param), name=f"adamw_{param}", tr=BIG[keys[0]][2])
        out[param] = [unshard3d(t, param) for t in raw[param]]

    wait_grads("ffn1", sent["small"][1][-1])
    wait_grads("ffn0", landed["w_up1"])
    update("w_up", ("w_up0", "w_up1"))
    update("w_down", ("w_down0", "w_down1"))
    wait_grads("mix0", [raw["w_up"][1], raw["w_down"][1]])
    update("w_in", ("w_in",))
    update("w_out", ("w_out",))
    update("w_pool", ("w_pool",))
    wait_grads("small", raw["w_pool"][1], small_scatter)
    res = _adamw_small([landed[k] for k in small_keys], [small2d(wts[k]) for k in small_keys],
                       [small2d(mom[k]) for k in small_keys], [small2d(var[k]) for k in small_keys],
                       name="adamw_small")
    for j, k in enumerate(small_keys):
        out[k] = [t.reshape(wts[k].shape) for t in res[4 * j:4 * j + 4]]

    loss = jnp.sum(landed["loss"][:, 0, 0])
    return (loss, dx[None], *[out[k][0] for k in names], *[out[k][1] for k in names],
            *[out[k][2] for k in names], *[out[k][3] for k in names])
```

```python
import jax
import jax.numpy as jnp
from jax import lax
from jax.experimental import pallas as pl
from jax.experimental.pallas import tpu as pltpu

F32 = jnp.float32
BF16 = jnp.bfloat16

RMS_EPS = 1e-6
LN_EPS = 1e-5
ADAM_LR = 0.001
ADAM_B1 = 0.9
ADAM_B2 = 0.999
ADAM_EPS = 1e-08
ADAM_WD = 0.01
ADAM_STEP = 10

N_DEV = 8
N_PAIR = N_DEV // 2
A_WIDTH = 512
A_TAPS = 31
POOL_WINDOWS = (2, 4, 8, 16)
POOL_GROUP = 256
HALO_A = 32
HALO_S = 16
VMEM_LIMIT = 56 * 1024 * 1024


def _params(sem, vmem=VMEM_LIMIT):
    return pltpu.CompilerParams(dimension_semantics=sem, vmem_limit_bytes=vmem)


def _sigmoid(x):
    return 0.5 * jnp.tanh(0.5 * x) + 0.5


def _prev_blk(i, ts, hb):
    return jnp.maximum(i * (ts // hb) - 1, 0)


def _next_blk(i, ts, hb, s):
    return jnp.minimum((i + 1) * (ts // hb), s // hb - 1)


def _mm(a, b, *, name, ta=False, tb=False, add=None, out_dtype=F32, tm=512, tn=512, tk=None):
    m, k = (a.shape[1], a.shape[0]) if ta else a.shape
    n = b.shape[0] if tb else b.shape[1]
    tk = k if tk is None else tk
    tm, tn, tk = min(tm, m), min(tn, n), min(tk, k)
    assert m % tm == 0 and n % tn == 0 and k % tk == 0, (name, m, n, k, tm, tn, tk)
    nk = k // tk
    dims = (((0,) if ta else (1,), (1,) if tb else (0,)), ((), ()))
    n_in = 2 + (add is not None)

    def body(*refs):
        a_ref, b_ref = refs[0], refs[1]
        add_ref = refs[2] if add is not None else None
        o_ref = refs[n_in]
        part = lax.dot_general(a_ref[...].astype(BF16), b_ref[...].astype(BF16), dims, preferred_element_type=F32)

        def finish(r):
            if add_ref is not None:
                r = r + add_ref[...]
            o_ref[...] = r.astype(out_dtype)

        if nk == 1:
            finish(part)
            return
        acc_ref = refs[-1]
        kk = pl.program_id(2)

        @pl.when(kk == 0)
        def _():
            acc_ref[...] = part

        @pl.when(kk > 0)
        def _():
            acc_ref[...] += part

        @pl.when(kk == nk - 1)
        def _():
            finish(acc_ref[...])

    a_spec = pl.BlockSpec((tk, tm), lambda i, j, kk: (kk, i)) if ta else pl.BlockSpec((tm, tk), lambda i, j, kk: (i, kk))
    b_spec = pl.BlockSpec((tn, tk), lambda i, j, kk: (j, kk)) if tb else pl.BlockSpec((tk, tn), lambda i, j, kk: (kk, j))
    in_specs = [a_spec, b_spec]
    args = [a, b]
    if add is not None:
        in_specs.append(pl.BlockSpec((tm, tn), lambda i, j, kk: (i, j)))
        args.append(add)
    return pl.pallas_call(
        body, name=name, grid=(m // tm, n // tn, nk),
        in_specs=in_specs, out_specs=pl.BlockSpec((tm, tn), lambda i, j, kk: (i, j)),
        out_shape=jax.ShapeDtypeStruct((m, n), out_dtype),
        scratch_shapes=[pltpu.VMEM((tm, tn), F32)] if nk > 1 else [],
        compiler_params=_params(("parallel", "parallel", "arbitrary")),
    )(*args)


def _rms_rows(xv, gv):
    return xv * lax.rsqrt(jnp.mean(xv * xv, axis=-1, keepdims=True) + RMS_EPS) * gv


_NT = (((1,), (1,)), ((), ()))
_TN = (((0,), (0,)), ((), ()))


def _rms_mm(x, g, wt, *, name, out_dtype, tm=1024, tn=512):
    s, d = x.shape
    n = wt.shape[0]
    tm = min(tm, s)
    assert s % tm == 0 and n % tn == 0

    def body(x_ref, g_ref, w_ref, h_ref, z_ref, hs_ref):
        @pl.when(pl.program_id(1) == 0)
        def _():
            hv = _rms_rows(x_ref[...], g_ref[...]).astype(BF16)
            hs_ref[...] = hv
            h_ref[...] = hv

        z_ref[...] = lax.dot_general(hs_ref[...], w_ref[...], _NT, preferred_element_type=F32).astype(out_dtype)

    return pl.pallas_call(
        body, name=name, grid=(s // tm, n // tn),
        in_specs=[pl.BlockSpec((tm, d), lambda i, j: (i, 0)),
                  pl.BlockSpec((1, d), lambda i, j: (0, 0)),
                  pl.BlockSpec((tn, d), lambda i, j: (j, 0))],
        out_specs=[pl.BlockSpec((tm, d), lambda i, j: (i, 0)),
                   pl.BlockSpec((tm, tn), lambda i, j: (i, j))],
        out_shape=[jax.ShapeDtypeStruct((s, d), BF16), jax.ShapeDtypeStruct((s, n), out_dtype)],
        scratch_shapes=[pltpu.VMEM((tm, d), BF16)],
        compiler_params=_params(("parallel", "arbitrary")),
    )(x, g, wt)


def _rms_bwd_rows(xv, gv, dh):
    r = lax.rsqrt(jnp.mean(xv * xv, axis=-1, keepdims=True) + RMS_EPS)
    xh = xv * r
    dn = dh * gv
    dx = r * (dn - xh * jnp.mean(dn * xh, axis=-1, keepdims=True))
    return dx, dh * xh


def _mm_rms_bwd(a, b, x, g, dres, dep, *, name, tm=512):
    s, k = a.shape
    d = b.shape[1]
    tm = min(tm, s)

    def body(a_ref, b_ref, x_ref, g_ref, dres_ref, dep_ref, dx_ref, dg_ref):
        @pl.when(pl.program_id(0) == 0)
        def _():
            dg_ref[...] = jnp.zeros_like(dg_ref)

        dh = jnp.dot(a_ref[...], b_ref[...], preferred_element_type=F32)
        dx, dgr = _rms_bwd_rows(x_ref[...], g_ref[...], dh)
        dx_ref[...] = dres_ref[...] + dx
        dg_ref[...] += jnp.sum(dgr, axis=0, keepdims=True)

    row = pl.BlockSpec((tm, d), lambda i: (i, 0))
    vec = pl.BlockSpec((1, d), lambda i: (0, 0))
    return pl.pallas_call(
        body, name=name, grid=(s // tm,),
        in_specs=[pl.BlockSpec((tm, k), lambda i: (i, 0)), pl.BlockSpec((k, d), lambda i: (0, 0)), row, vec, row,
                  pl.BlockSpec(memory_space=pl.ANY)],
        out_specs=[row, vec],
        out_shape=[jax.ShapeDtypeStruct((s, d), F32), jax.ShapeDtypeStruct((1, d), F32)],
        compiler_params=_params(("arbitrary",)),
    )(a, b, x, g, dres, dep)


def _conv_taps(ext_ref, w_ref, n_taps, base, r0, rows, reverse=False):
    acc = None
    for k in range(n_taps):
        off = r0 + (base - k if reverse else base + k)
        term = w_ref[k:k + 1, :] * ext_ref[pl.ds(off, rows), :]
        acc = term if acc is None else acc + term
    return acc


def _shift_copies(src_ref, sh_ref, rows):
    for b in range(1, 8):
        sh_ref[b, 0:rows, :] = src_ref[pl.ds(b, rows), :]


def _shifted(src_ref, sh_ref, start, rows, off):
    a, b = divmod(off, 8)
    ref = src_ref if b == 0 else sh_ref.at[b]
    return ref[pl.ds(start + 8 * a, rows), :]


def _mix0_fwd(z, conv_a, ln_g, ln_b, conv_b, *, name, ts=256, rc=32):
    s = z.shape[0]
    c = A_WIDTH
    hb = HALO_A

    def body(z_ref, zp_ref, wa_ref, lg_ref, lb_ref, wb_ref, ab_ref, ca_ref, exta, extb, sha):
        keep = jnp.where(pl.program_id(0) > 0, 1.0, 0.0)
        zp = zp_ref[...]
        exta[0:hb, :] = zp[:, 0:c] * _sigmoid(zp[:, c:2 * c]) * keep
        extb[0:hb, :] = zp[:, 3 * c:4 * c] * zp[:, 4 * c:5 * c] * keep
        exta[hb:hb + ts, :] = z_ref[:, 0:c] * _sigmoid(z_ref[:, c:2 * c])
        extb[hb:hb + ts, :] = z_ref[:, 3 * c:4 * c] * z_ref[:, 4 * c:5 * c]
        _shift_copies(exta, sha, hb + ts - 8)
        lg = lg_ref[...]
        lb = lb_ref[...]
        for q in range(ts // rc):
            r0 = q * rc
            ca = None
            for k in range(A_TAPS):
                term = wa_ref[k:k + 1, :] * _shifted(exta, sha, r0, rc, hb - (A_TAPS - 1) + k)
                ca = term if ca is None else ca + term
            ca_ref[r0:r0 + rc, :] = ca
            mu = jnp.mean(ca, axis=-1, keepdims=True)
            xc = ca - mu
            rs = lax.rsqrt(jnp.mean(xc * xc, axis=-1, keepdims=True) + LN_EPS)
            l = xc * rs * lg + lb
            ab_ref[r0:r0 + rc, 0:c] = (l * _sigmoid(l)).astype(BF16)
            cbc = _conv_taps(extb, wb_ref, 3, hb - 2, r0, rc)
            ab_ref[r0:r0 + rc, c:2 * c] = (z_ref[r0:r0 + rc, 2 * c:3 * c] * cbc).astype(BF16)

    return pl.pallas_call(
        body, name=name, grid=(s // ts,),
        in_specs=[pl.BlockSpec((ts, 5 * c), lambda i: (i, 0)),
                  pl.BlockSpec((hb, 5 * c), lambda i: (_prev_blk(i, ts, hb), 0)),
                  pl.BlockSpec((32, c), lambda i: (0, 0)),
                  pl.BlockSpec((1, c), lambda i: (0, 0)),
                  pl.BlockSpec((1, c), lambda i: (0, 0)),
                  pl.BlockSpec((8, c), lambda i: (0, 0))],
        out_specs=[pl.BlockSpec((ts, 2 * c), lambda i: (i, 0)),
                   pl.BlockSpec((ts, c), lambda i: (i, 0))],
        out_shape=[jax.ShapeDtypeStruct((s, 2 * c), BF16), jax.ShapeDtypeStruct((s, c), F32)],
        scratch_shapes=[pltpu.VMEM((hb + ts, c), F32), pltpu.VMEM((hb + ts, c), F32),
                        pltpu.VMEM((8, hb + ts - 8, c), F32)],
        compiler_params=_params(("parallel",)),
    )(z, z, conv_a, ln_g, ln_b, conv_b)


def _mix0_bwd(z, ca, dab, conv_a, ln_g, ln_b, conv_b, *, name, ts=256, rc=32):
    s = z.shape[0]
    c = A_WIDTH
    hb = HALO_A
    ta = A_TAPS

    def body(z_ref, zp_ref, zn_ref, ca_ref, can_ref, d_ref, dn_ref, wa_ref, lg_ref, lb_ref, wb_ref,
             dz_ref, dwa_ref, dwb_ref, dlg_ref, dlb_ref, exta, extb, extdca, extdcb, shd):
        i = pl.program_id(0)
        keep_p = jnp.where(i > 0, 1.0, 0.0)
        keep_n = jnp.where(i < s // ts - 1, 1.0, 0.0)

        @pl.when(i == 0)
        def _():
            dwa_ref[...] = jnp.zeros_like(dwa_ref)
            dwb_ref[...] = jnp.zeros_like(dwb_ref)
            dlg_ref[...] = jnp.zeros_like(dlg_ref)
            dlb_ref[...] = jnp.zeros_like(dlb_ref)

        lg = lg_ref[...]
        lb = lb_ref[...]
        zp = zp_ref[...]
        exta[0:hb, :] = zp[:, 0:c] * _sigmoid(zp[:, c:2 * c]) * keep_p
        extb[0:hb, :] = zp[:, 3 * c:4 * c] * zp[:, 4 * c:5 * c] * keep_p
        exta[hb:hb + ts, :] = z_ref[:, 0:c] * _sigmoid(z_ref[:, c:2 * c])
        extb[hb:hb + ts, :] = z_ref[:, 3 * c:4 * c] * z_ref[:, 4 * c:5 * c]

        def ln_bwd(cav, dav):
            mu = jnp.mean(cav, axis=-1, keepdims=True)
            xc = cav - mu
            rs = lax.rsqrt(jnp.mean(xc * xc, axis=-1, keepdims=True) + LN_EPS)
            nv = xc * rs
            l = nv * lg + lb
            sg = _sigmoid(l)
            dl = dav * (sg * (1.0 + l * (1.0 - sg)))
            dnv = dl * lg
            dca = rs * (dnv - jnp.mean(dnv, axis=-1, keepdims=True)
                        - nv * jnp.mean(dnv * nv, axis=-1, keepdims=True))
            return dca, dl, nv

        dlg_acc = jnp.zeros((1, c), F32)
        dlb_acc = jnp.zeros((1, c), F32)
        for q in range(ts // rc):
            r0 = q * rc
            dca, dl, nv = ln_bwd(ca_ref[r0:r0 + rc, :], d_ref[r0:r0 + rc, 0:c])
            extdca[r0:r0 + rc, :] = dca
            dlg_acc = dlg_acc + jnp.sum(dl * nv, axis=0, keepdims=True)
            dlb_acc = dlb_acc + jnp.sum(dl, axis=0, keepdims=True)
            extdcb[r0:r0 + rc, :] = d_ref[r0:r0 + rc, c:2 * c] * z_ref[r0:r0 + rc, 2 * c:3 * c]
        dca_n, _, _ = ln_bwd(can_ref[...], dn_ref[:, 0:c])
        extdca[ts:ts + hb, :] = dca_n * keep_n
        extdcb[ts:ts + hb, :] = dn_ref[:, c:2 * c] * zn_ref[:, 2 * c:3 * c] * keep_n
        dlg_ref[...] += dlg_acc
        dlb_ref[...] += dlb_acc
        _shift_copies(extdca, shd, ts + hb - 8)

        for q in range(ts // rc):
            r0 = q * rc
            zr = z_ref[r0:r0 + rc, :]
            dga = None
            for k in range(ta):
                term = wa_ref[k:k + 1, :] * _shifted(extdca, shd, r0, rc, ta - 1 - k)
                dga = term if dga is None else dga + term
            sg = _sigmoid(zr[:, c:2 * c])
            dz_ref[r0:r0 + rc, 0:c] = (dga * sg).astype(BF16)
            dz_ref[r0:r0 + rc, c:2 * c] = (dga * zr[:, 0:c] * sg * (1.0 - sg)).astype(BF16)
            cbc = _conv_taps(extb, wb_ref, 3, hb - 2, r0, rc)
            dz_ref[r0:r0 + rc, 2 * c:3 * c] = (d_ref[r0:r0 + rc, c:2 * c] * cbc).astype(BF16)
            dcb = _conv_taps(extdcb, wb_ref, 3, 2, r0, rc, reverse=True)
            dz_ref[r0:r0 + rc, 3 * c:4 * c] = (dcb * zr[:, 4 * c:5 * c]).astype(BF16)
            dz_ref[r0:r0 + rc, 4 * c:5 * c] = (dcb * zr[:, 3 * c:4 * c]).astype(BF16)

        for k in range(ta):
            part = None
            for q in range(ts // rc):
                r0 = q * rc
                p = exta[hb + r0:hb + r0 + rc, :] * _shifted(extdca, shd, r0, rc, ta - 1 - k)
                for r in range(0, rc, 8):
                    part = p[r:r + 8, :] if part is None else part + p[r:r + 8, :]
            dwa_ref[k:k + 1, :] += jnp.sum(part, axis=0, keepdims=True)
        dcb_t = extdcb[0:ts, :]
        for k in range(3):
            dwb_ref[k:k + 1, :] += jnp.sum(dcb_t * extb[pl.ds(hb - 2 + k, ts), :], axis=0, keepdims=True)

    def tile(w):
        return pl.BlockSpec((ts, w), lambda i: (i, 0))

    def prev(w):
        return pl.BlockSpec((hb, w), lambda i: (_prev_blk(i, ts, hb), 0))

    def nxt(w):
        return pl.BlockSpec((hb, w), lambda i: (_next_blk(i, ts, hb, s), 0))

    def const(r, w):
        return pl.BlockSpec((r, w), lambda i: (0, 0))

    return pl.pallas_call(
        body, name=name, grid=(s // ts,),
        in_specs=[tile(5 * c), prev(5 * c), nxt(5 * c), tile(c), nxt(c), tile(2 * c), nxt(2 * c),
                  const(32, c), const(1, c), const(1, c), const(8, c)],
        out_specs=[tile(5 * c), const(32, c), const(8, c), const(1, c), const(1, c)],
        out_shape=[jax.ShapeDtypeStruct((s, 5 * c), BF16), jax.ShapeDtypeStruct((32, c), F32),
                   jax.ShapeDtypeStruct((8, c), F32), jax.ShapeDtypeStruct((1, c), F32),
                   jax.ShapeDtypeStruct((1, c), F32)],
        scratch_shapes=[pltpu.VMEM((hb + ts, c), F32), pltpu.VMEM((hb + ts, c), F32),
                        pltpu.VMEM((ts + hb, c), F32), pltpu.VMEM((ts + hb, c), F32),
                        pltpu.VMEM((8, ts + hb - 8, c), F32)],
        compiler_params=_params(("arbitrary",)),
    )(z, z, z, ca, ca, dab, dab, conv_a, ln_g, ln_b, conv_b)


def _ffn_up(x, g, w8, *, name, tm=2048):
    s, d = x.shape
    nb, c, _ = w8.shape
    tm = min(tm, s)

    def body(x_ref, g_ref, w_ref, h_ref, u_ref, hs_ref):
        @pl.when(pl.program_id(1) == 0)
        def _():
            hv = _rms_rows(x_ref[...], g_ref[...]).astype(BF16)
            hs_ref[...] = hv
            h_ref[...] = hv

        u_ref[...] = lax.dot_general(hs_ref[...], w_ref[...], _NT, preferred_element_type=F32).astype(BF16)

    return pl.pallas_call(
        body, name=name, grid=(s // tm, nb),
        in_specs=[pl.BlockSpec((tm, d), lambda i, k: (i, 0)),
                  pl.BlockSpec((1, d), lambda i, k: (0, 0)),
                  pl.BlockSpec((None, c, d), lambda i, k: (k, 0, 0))],
        out_specs=[pl.BlockSpec((tm, d), lambda i, k: (i, 0)),
                   pl.BlockSpec((None, tm, c), lambda i, k: (k, i, 0))],
        out_shape=[jax.ShapeDtypeStruct((s, d), BF16), jax.ShapeDtypeStruct((nb, s, c), BF16)],
        scratch_shapes=[pltpu.VMEM((tm, d), BF16)],
        compiler_params=_params(("parallel", "arbitrary")),
    )(x, g, w8)


def _ffn_mid_down(up8, wc8, wd4, x, head=None, *, name, ts=256, rc=32):
    nb, s, c = up8.shape
    d = x.shape[1]
    hb = HALO_S
    ts = min(ts, s)

    def body(*refs):
        u_ref, up_ref, wc_ref, wd_ref, x_ref = refs[:5]
        n_out = 4 if head is None else 7
        outs = refs[-1 - n_out:-1]
        act_ref, ug_ref, uv_ref = outs[-3:]
        ext = refs[-1]
        keep = jnp.where(pl.program_id(0) > 0, 1.0, 0.0)
        acc = x_ref[...]
        for j in range(N_PAIR):
            eg, ev = ext.at[2 * (j % 2)], ext.at[2 * (j % 2) + 1]
            eg[0:hb, :] = up_ref[j].astype(F32) * keep
            ev[0:hb, :] = up_ref[j + N_PAIR].astype(F32) * keep
            eg[hb:hb + ts, :] = u_ref[j].astype(F32)
            ev[hb:hb + ts, :] = u_ref[j + N_PAIR].astype(F32)
            for q in range(ts // rc):
                r0 = q * rc
                gg = _conv_taps(eg, wc_ref.at[j], 3, hb - 2, r0, rc)
                vv = _conv_taps(ev, wc_ref.at[j + N_PAIR], 3, hb - 2, r0, rc)
                ug_ref[j, r0:r0 + rc, :] = gg.astype(BF16)
                uv_ref[j, r0:r0 + rc, :] = vv.astype(BF16)
                act_ref[j, r0:r0 + rc, :] = (gg * _sigmoid(gg) * vv).astype(BF16)
            acc = acc + jnp.dot(act_ref[j], wd_ref[j], preferred_element_type=F32)
        if head is None:
            outs[0][...] = acc
            return
        g_ref, t_ref = refs[5], refs[6]
        loss_ref, dx_ref, dxb_ref, dg_ref = outs[:4]

        @pl.when(pl.program_id(0) == 0)
        def _():
            loss_ref[...] = jnp.zeros_like(loss_ref)
            dg_ref[...] = jnp.zeros_like(dg_ref)

        gv = g_ref[...]
        r = lax.rsqrt(jnp.mean(acc * acc, axis=-1, keepdims=True) + RMS_EPS)
        xh = acc * r
        err = xh * gv - t_ref[...]
        loss_ref[...] += 0.5 * jnp.sum(jnp.mean(err * err, axis=-1, keepdims=True), axis=0, keepdims=True)
        dy = err * (1.0 / d)
        dn = dy * gv
        dx = r * (dn - xh * jnp.mean(dn * xh, axis=-1, keepdims=True))
        dx_ref[...] = dx
        dxb_ref[...] = dx.astype(BF16)
        dg_ref[...] += jnp.sum(dy * xh, axis=0, keepdims=True)

    row = pl.BlockSpec((ts, d), lambda i: (i, 0))
    vec = pl.BlockSpec((1, d), lambda i: (0, 0))
    tile = pl.BlockSpec((N_PAIR, ts, c), lambda i: (0, i, 0))
    half = jax.ShapeDtypeStruct((N_PAIR, s, c), BF16)
    in_specs = [pl.BlockSpec((nb, ts, c), lambda i: (0, i, 0)),
                pl.BlockSpec((nb, hb, c), lambda i: (0, _prev_blk(i, ts, hb), 0)),
                pl.BlockSpec((nb, 8, c), lambda i: (0, 0, 0)),
                pl.BlockSpec((N_PAIR, c, d), lambda i: (0, 0, 0)), row]
    args = [up8, up8, wc8, wd4, x]
    if head is None:
        out_specs = [row, tile, tile, tile]
        out_shape = [jax.ShapeDtypeStruct((s, d), F32), half, half, half]
    else:
        in_specs += [vec, row]
        args += list(head)
        out_specs = [pl.BlockSpec((1, 1), lambda i: (0, 0)), row, row, vec, tile, tile, tile]
        out_shape = [jax.ShapeDtypeStruct((1, 1), F32), jax.ShapeDtypeStruct((s, d), F32),
                     jax.ShapeDtypeStruct((s, d), BF16), jax.ShapeDtypeStruct((1, d), F32), half, half, half]
    return pl.pallas_call(
        body, name=name, grid=(s // ts,), in_specs=in_specs, out_specs=out_specs, out_shape=out_shape,
        scratch_shapes=[pltpu.VMEM((4, hb + ts, c), F32)],
        compiler_params=_params(("parallel",) if head is None else ("arbitrary",)),
    )(*args)


def _ffn_dwdown(act4, db, *, name):
    npair, s, c = act4.shape
    d = db.shape[1]

    def body(a_ref, d_ref, o_ref):
        o_ref[...] = lax.dot_general(a_ref[...], d_ref[...], (((0,), (0,)), ((), ())),
                                     preferred_element_type=F32).astype(BF16)

    return pl.pallas_call(
        body, name=name, grid=(npair,),
        in_specs=[pl.BlockSpec((None, s, c), lambda j: (j, 0, 0)),
                  pl.BlockSpec((s, d), lambda j: (0, 0))],
        out_specs=pl.BlockSpec((None, c, d), lambda j: (j, 0, 0)),
        out_shape=jax.ShapeDtypeStruct((npair, c, d), BF16),
        compiler_params=_params(("parallel",)),
    )(act4, db)


def _ffn_dact(db, wd4, *, name, tm=1024):
    s, d = db.shape
    npair, c, _ = wd4.shape
    tm = min(tm, s)

    def body(d_ref, w_ref, o_ref):
        o_ref[...] = lax.dot_general(d_ref[...], w_ref[...], _NT, preferred_element_type=F32).astype(BF16)

    return pl.pallas_call(
        body, name=name, grid=(s // tm, npair),
        in_specs=[pl.BlockSpec((tm, d), lambda i, j: (i, 0)),
                  pl.BlockSpec((None, c, d), lambda i, j: (j, 0, 0))],
        out_specs=pl.BlockSpec((None, tm, c), lambda i, j: (j, i, 0)),
        out_shape=jax.ShapeDtypeStruct((npair, s, c), BF16),
        compiler_params=_params(("parallel", "parallel")),
    )(db, wd4)


def _ffn_midbwd(up8, ug4, uv4, dact4, wc8, *, name, ts=512, rc=32):
    nb, s, c = up8.shape
    hb = HALO_S
    ts = min(ts, s)
    n_i = s // ts

    def body(pg_ref, pv_ref, ug_ref, ugn_ref, uv_ref, uvn_ref, d_ref, dn_ref, wg_ref, wv_ref,
             dg_ref, dv_ref, dwg_ref, dwv_ref, extdg, extdv):
        i = pl.program_id(1)
        keep_n = jnp.where(i < n_i - 1, 1.0, 0.0)

        @pl.when(i == 0)
        def _():
            dwg_ref[...] = jnp.zeros_like(dwg_ref)
            dwv_ref[...] = jnp.zeros_like(dwv_ref)

        def du_rows(r0, rows, gg, vv, dav):
            sg = _sigmoid(gg)
            extdg[r0:r0 + rows, :] = dav * vv * (sg * (1.0 + gg * (1.0 - sg)))
            extdv[r0:r0 + rows, :] = dav * (gg * sg)

        for q in range(ts // rc):
            rows = slice(q * rc, q * rc + rc)
            du_rows(q * rc, rc, ug_ref[rows, :].astype(F32), uv_ref[rows, :].astype(F32),
                    d_ref[rows, :].astype(F32))
        du_rows(ts, 8, ugn_ref[0:8, :].astype(F32), uvn_ref[0:8, :].astype(F32),
                dn_ref[0:8, :].astype(F32) * keep_n)

        def fold(p):
            acc = p[0:8, :]
            for r in range(8, rc, 8):
                acc = acc + p[r:r + 8, :]
            return acc

        for extd, p_ref, w_ref, out_ref, dw_ref in ((extdg, pg_ref, wg_ref, dg_ref, dwg_ref),
                                                    (extdv, pv_ref, wv_ref, dv_ref, dwv_ref)):
            part = [None, None, None]
            for q in range(ts // rc):
                r0 = q * rc
                pre = p_ref[r0:r0 + rc, :].astype(F32)
                dup = None
                for k in range(3):
                    sh = extd[pl.ds(r0 + 2 - k, rc), :]
                    term = w_ref[k:k + 1, :] * sh
                    dup = term if dup is None else dup + term
                    prod = fold(pre * sh)
                    part[k] = prod if part[k] is None else part[k] + prod
                out_ref[r0:r0 + rc, :] = dup.astype(BF16)
            for k in range(3):
                dw_ref[k:k + 1, :] += jnp.sum(part[k], axis=0, keepdims=True)

    def blk(off, nxt):
        if nxt:
            return pl.BlockSpec((None, hb, c), lambda j, i: (j + off, _next_blk(i, ts, hb, s), 0))
        return pl.BlockSpec((None, ts, c), lambda j, i: (j + off, i, 0))

    def taps(off):
        return pl.BlockSpec((None, 8, c), lambda j, i: (j + off, 0, 0))

    tile = blk(0, False)
    acc = pl.BlockSpec((None, 8, c), lambda j, i: (j, 0, 0))
    return pl.pallas_call(
        body, name=name, grid=(N_PAIR, n_i),
        in_specs=[tile, blk(N_PAIR, False), tile, blk(0, True), tile, blk(0, True), tile, blk(0, True),
                  taps(0), taps(N_PAIR)],
        out_specs=[tile, tile, acc, acc],
        out_shape=[jax.ShapeDtypeStruct((N_PAIR, s, c), BF16), jax.ShapeDtypeStruct((N_PAIR, s, c), BF16),
                   jax.ShapeDtypeStruct((N_PAIR, 8, c), F32), jax.ShapeDtypeStruct((N_PAIR, 8, c), F32)],
        scratch_shapes=[pltpu.VMEM((ts + 8, c), F32), pltpu.VMEM((ts + 8, c), F32)],
        compiler_params=_params(("parallel", "arbitrary")),
    )(up8, up8, ug4, ug4, uv4, uv4, dact4, dact4, wc8, wc8)


def _ffn_dh(dupg, dupv, w8, x, g, dres, dep, *, name, tm=512):
    npair, s, c = dupg.shape
    d = x.shape[1]
    tm = min(tm, s)

    def body(dg_ref, dv_ref, w_ref, x_ref, g_ref, dres_ref, dep_ref, dx_ref, dxb_ref, dgain_ref):
        @pl.when(pl.program_id(0) == 0)
        def _():
            dgain_ref[...] = jnp.zeros_like(dgain_ref)

        dh = None
        for j in range(npair):
            for src, k in ((dg_ref, j), (dv_ref, j + npair)):
                part = jnp.dot(src[j], w_ref[k], preferred_element_type=F32)
                dh = part if dh is None else dh + part
        dx, dgr = _rms_bwd_rows(x_ref[...], g_ref[...], dh)
        dx = dres_ref[...] + dx
        dx_ref[...] = dx
        dxb_ref[...] = dx.astype(BF16)
        dgain_ref[...] += jnp.sum(dgr, axis=0, keepdims=True)

    row = pl.BlockSpec((tm, d), lambda i: (i, 0))
    vec = pl.BlockSpec((1, d), lambda i: (0, 0))
    dup = pl.BlockSpec((npair, tm, c), lambda i: (0, i, 0))
    return pl.pallas_call(
        body, name=name, grid=(s // tm,),
        in_specs=[dup, dup, pl.BlockSpec((2 * npair, c, d), lambda i: (0, 0, 0)), row, vec, row,
                  pl.BlockSpec(memory_space=pl.ANY)],
        out_specs=[row, row, vec],
        out_shape=[jax.ShapeDtypeStruct((s, d), F32), jax.ShapeDtypeStruct((s, d), BF16),
                   jax.ShapeDtypeStruct((1, d), F32)],
        compiler_params=_params(("arbitrary",)),
    )(dupg, dupv, w8, x, g, dres, dep)


def _ffn_dwup(h, dupg, dupv, *, name, tm=512):
    npair, s, c = dupg.shape
    d = h.shape[1]

    def body(h_ref, dg_ref, dv_ref, o_ref):
        k = pl.program_id(1)

        @pl.when(k < npair)
        def _():
            o_ref[...] = lax.dot_general(dg_ref[...], h_ref[...], _TN, preferred_element_type=F32).astype(BF16)

        @pl.when(k >= npair)
        def _():
            o_ref[...] = lax.dot_general(dv_ref[...], h_ref[...], _TN, preferred_element_type=F32).astype(BF16)

    return pl.pallas_call(
        body, name=name, grid=(d // tm, 2 * npair),
        in_specs=[pl.BlockSpec((s, tm), lambda m, k: (0, m)),
                  pl.BlockSpec((None, s, c), lambda m, k: (jnp.minimum(k, npair - 1), 0, 0)),
                  pl.BlockSpec((None, s, c), lambda m, k: (jnp.maximum(k - npair, 0), 0, 0))],
        out_specs=pl.BlockSpec((None, c, tm), lambda m, k: (k, 0, m)),
        out_shape=jax.ShapeDtypeStruct((2 * npair, c, d), BF16),
        compiler_params=_params(("parallel", "arbitrary")),
    )(h, dupg, dupv)


def _pool_counts(i, ts, rows, window):
    t = lax.broadcasted_iota(jnp.int32, (rows, 1), 0) + i * ts + 1
    return jnp.minimum(t, window).astype(F32)


def _pool_fwd(x, g, *, name, ts=256):
    s, d = x.shape
    hb = HALO_A
    pg = POOL_GROUP
    ts = min(ts, s)

    def body(x_ref, xp_ref, g_ref, p_ref, ext, sa, sb):
        i = pl.program_id(0)
        keep = jnp.where(i > 0, 1.0, 0.0)
        gv = g_ref[...]
        ext[0:hb, :] = _rms_rows(xp_ref[...], gv) * keep
        ext[hb:hb + ts, :] = _rms_rows(x_ref[...], gv)
        rows = hb + ts - 8
        sa[0:8, :] = jnp.zeros((8, d), F32)
        sb[0:8, :] = jnp.zeros((8, d), F32)
        for gi, w in enumerate(POOL_WINDOWS):
            cols = slice(gi * pg, (gi + 1) * pg)
            cur, nxt, k = ext, sa, 1
            while k < w:
                nxt[8:8 + rows, cols] = cur[8:8 + rows, cols] + cur[pl.ds(8 - k, rows), cols]
                cur, nxt, k = nxt, (sb if nxt is sa else sa), 2 * k
            h = ext[hb:hb + ts, cols]
            p_ref[:, cols] = (cur[hb:hb + ts, cols] / _pool_counts(i, ts, ts, w) - h).astype(BF16)

    return pl.pallas_call(
        body, name=name, grid=(s // ts,),
        in_specs=[pl.BlockSpec((ts, d), lambda i: (i, 0)),
                  pl.BlockSpec((hb, d), lambda i: (_prev_blk(i, ts, hb), 0)),
                  pl.BlockSpec((1, d), lambda i: (0, 0))],
        out_specs=pl.BlockSpec((ts, d), lambda i: (i, 0)),
        out_shape=jax.ShapeDtypeStruct((s, d), BF16),
        scratch_shapes=[pltpu.VMEM((hb + ts, d), F32)] * 3,
        compiler_params=_params(("parallel",)),
    )(x, x, g)


def _pool_mm_fwd(p, w_pool, scale, x, *, name, ts=1024):
    s, d = x.shape
    pg = POOL_GROUP
    ts = min(ts, s)

    def body(p_ref, w_ref, s_ref, x_ref, o_ref, yu_ref):
        yu = jnp.dot(p_ref[...], w_ref[...], preferred_element_type=F32)
        yu_ref[...] = yu.astype(BF16)
        o_ref[...] = x_ref[...] + yu * s_ref[...]

    blk = pl.BlockSpec((ts, pg), lambda i, gi: (i, gi))
    return pl.pallas_call(
        body, name=name, grid=(s // ts, d // pg),
        in_specs=[blk, pl.BlockSpec((None, pg, pg), lambda i, gi: (gi, 0, 0)),
                  pl.BlockSpec((1, pg), lambda i, gi: (0, gi)), blk],
        out_specs=[blk, blk],
        out_shape=[jax.ShapeDtypeStruct((s, d), F32), jax.ShapeDtypeStruct((s, d), BF16)],
        compiler_params=_params(("parallel", "parallel")),
    )(p, w_pool, scale, x)


def _pool_mm_bwd(dres, w_pool, scale, yu, *, name, ts=1024):
    s, d = dres.shape
    pg = POOL_GROUP
    ts = min(ts, s)

    def body(d_ref, w_ref, s_ref, yu_ref, dyc_ref, dp_ref, ds_ref):
        @pl.when(pl.program_id(1) == 0)
        def _():
            ds_ref[...] = jnp.zeros_like(ds_ref)

        dv = d_ref[...]
        dyc = (dv * s_ref[...]).astype(BF16)
        dyc_ref[...] = dyc
        dp_ref[...] = lax.dot_general(dyc, w_ref[...], (((1,), (1,)), ((), ())), preferred_element_type=F32)
        ds_ref[...] += jnp.sum(dv * yu_ref[...].astype(F32), axis=0, keepdims=True)

    blk = pl.BlockSpec((ts, pg), lambda gi, i: (i, gi))
    vec = pl.BlockSpec((1, pg), lambda gi, i: (0, gi))
    return pl.pallas_call(
        body, name=name, grid=(d // pg, s // ts),
        in_specs=[blk, pl.BlockSpec((None, pg, pg), lambda gi, i: (gi, 0, 0)), vec, blk],
        out_specs=[blk, blk, vec],
        out_shape=[jax.ShapeDtypeStruct((s, d), BF16), jax.ShapeDtypeStruct((s, d), F32),
                   jax.ShapeDtypeStruct((1, d), F32)],
        compiler_params=_params(("parallel", "arbitrary")),
    )(dres, w_pool, scale, yu)


def _pool_dw(p, dyc, *, name):
    s, d = p.shape
    pg = POOL_GROUP

    def body(p_ref, d_ref, o_ref):
        o_ref[...] = lax.dot_general(p_ref[...], d_ref[...], (((0,), (0,)), ((), ())),
                                     preferred_element_type=F32).astype(BF16)

    blk = pl.BlockSpec((s, pg), lambda gi: (0, gi))
    return pl.pallas_call(
        body, name=name, grid=(d // pg,),
        in_specs=[blk, blk], out_specs=pl.BlockSpec((None, pg, pg), lambda gi: (gi, 0, 0)),
        out_shape=jax.ShapeDtypeStruct((d // pg, pg, pg), BF16),
        compiler_params=_params(("parallel",)),
    )(p, dyc)


def _pool_bwd(dp, x, g, dres, *, name, ts=256):
    s, d = x.shape
    hb = HALO_A
    pg = POOL_GROUP
    ts = min(ts, s)
    n_i = s // ts

    def body(dp_ref, dpn_ref, x_ref, g_ref, dres_ref, dx_ref, dxb_ref, dg_ref, ext, dh, sa, sb):
        i = pl.program_id(0)
        keep_n = jnp.where(i < n_i - 1, 1.0, 0.0)

        @pl.when(i == 0)
        def _():
            dg_ref[...] = jnp.zeros_like(dg_ref)

        for gi, w in enumerate(POOL_WINDOWS):
            cols = slice(gi * pg, (gi + 1) * pg)
            ext[0:ts, cols] = dp_ref[:, cols] / _pool_counts(i, ts, ts, w)
            ext[ts:ts + hb, cols] = dpn_ref[:, cols] / _pool_counts(i + 1, ts, hb, w) * keep_n
        rows = ts + hb - 8
        sa[rows:rows + 8, :] = jnp.zeros((8, d), F32)
        sb[rows:rows + 8, :] = jnp.zeros((8, d), F32)
        for gi, w in enumerate(POOL_WINDOWS):
            cols = slice(gi * pg, (gi + 1) * pg)
            cur, nxt, k = ext, sa, 1
            while k < w:
                nxt[0:rows, cols] = cur[0:rows, cols] + cur[pl.ds(k, rows), cols]
                cur, nxt, k = nxt, (sb if nxt is sa else sa), 2 * k
            dh[:, cols] = cur[0:ts, cols] - dp_ref[:, cols]
        dx, dgr = _rms_bwd_rows(x_ref[...], g_ref[...], dh[...])
        dx = dres_ref[...] + dx
        dx_ref[...] = dx
        dxb_ref[...] = dx.astype(BF16)
        dg_ref[...] += jnp.sum(dgr, axis=0, keepdims=True)

    row = pl.BlockSpec((ts, d), lambda i: (i, 0))
    vec = pl.BlockSpec((1, d), lambda i: (0, 0))
    return pl.pallas_call(
        body, name=name, grid=(n_i,),
        in_specs=[row, pl.BlockSpec((hb, d), lambda i: (_next_blk(i, ts, hb, s), 0)), row, vec, row],
        out_specs=[row, row, vec],
        out_shape=[jax.ShapeDtypeStruct((s, d), F32), jax.ShapeDtypeStruct((s, d), BF16),
                   jax.ShapeDtypeStruct((1, d), F32)],
        scratch_shapes=[pltpu.VMEM((ts + hb, d), F32), pltpu.VMEM((ts, d), F32),
                        pltpu.VMEM((ts + hb, d), F32), pltpu.VMEM((ts + hb, d), F32)],
        compiler_params=_params(("arbitrary",)),
    )(dp, dp, x, g, dres)


def _pad_rows(w, rows):
    pad = [(0, 0)] * (w.ndim - 2) + [(0, rows - w.shape[-2]), (0, 0)]
    return jnp.pad(w, pad)


def _ffn_layer_fwd(x, nf, w8, wc8, wd4, tag, head=None):
    h, up8 = _ffn_up(x, nf, w8, name=f"ffn{tag}_up")
    res = _ffn_mid_down(up8, wc8, wd4, x, head, name=f"ffn{tag}_mid_down")
    act4, ug4, uv4 = res[-3:]
    out = res[0] if head is None else tuple(res[:4])
    return out, (h, up8, ug4, uv4, act4)


def _ffn_layer_bwd(d, db, x, nf, w8, wc8, wd4, saved, tag, grads_ready):
    h, up8, ug4, uv4, act4 = saved
    dact4 = _ffn_dact(db, wd4, name=f"ffn{tag}_dact")
    dwd4 = _ffn_dwdown(act4, db, name=f"ffn{tag}_dwdown")
    dupg, dupv, dwg, dwv = _ffn_midbwd(up8, ug4, uv4, dact4, wc8, name=f"ffn{tag}_midbwd")
    dw8 = _ffn_dwup(h, dupg, dupv, name=f"ffn{tag}_dwup")
    sent = grads_ready(f"ffn{tag}", {f"w_up{tag}": dw8, f"w_down{tag}": dwd4})
    dx, dxb, dnf = _ffn_dh(dupg, dupv, w8, x, nf, d, sent, name=f"ffn{tag}_dh")
    dwc8 = jnp.concatenate([dwg, dwv], axis=0)[:, :3]
    return dx, dxb, dnf, dwc8


def _local_step(x, target, weights, grads_ready):
    w0 = weights("mix0", x)
    wa = _pad_rows(w0["conv_a"], 32)
    wb = _pad_rows(w0["conv_b"], 8)
    wc = [_pad_rows(w0["conv_ffn"][l], 8) for l in range(2)]
    h0, z = _rms_mm(x, w0["norm_mix_even"], w0["w_in_t"], name="mix0_in", out_dtype=F32, tn=2560)
    ab, ca = _mix0_fwd(z, wa, w0["ln_a_g"], w0["ln_a_b"], wb, name="mix0_mid")
    x1 = _mm(ab, w0["w_out"], add=x, name="mix0_out", tm=1024, tn=1024)
    w1 = weights("ffn0", x1)
    x2, ffn0 = _ffn_layer_fwd(x1, w0["norm_ffn"][0:1], w1["w_up"], wc[0], w1["w_down"], 0)
    w2 = weights("ffn1", x2)
    p = _pool_fwd(x2, w0["norm_mix_odd"], name="pool_mid")
    x3, yu = _pool_mm_fwd(p, w2["w_pool"], w0["pool_scale"], x2, name="pool_mm")
    (loss, d4, d4b, g_norm_final), ffn1 = _ffn_layer_fwd(
        x3, w0["norm_ffn"][1:2], w2["w_up"], wc[1], w2["w_down"], 1, head=(w0["norm_final"], target))

    d3, d3b, g_nf1, g_wc1 = _ffn_layer_bwd(
        d4, d4b, x3, w0["norm_ffn"][1:2], w2["w_up"], wc[1], w2["w_down"], ffn1, 1, grads_ready)
    dyc, dp, g_scale = _pool_mm_bwd(d3, w2["w_pool"], w0["pool_scale"], yu, name="pool_mm_bwd")
    g_pool = _pool_dw(p, dyc, name="pool_dw")
    d2, d2b, g_nmo = _pool_bwd(dp, x2, w0["norm_mix_odd"], d3, name="pool_midbwd")
    d1, d1b, g_nf0, g_wc0 = _ffn_layer_bwd(
        d2, d2b, x1, w0["norm_ffn"][0:1], w1["w_up"], wc[0], w1["w_down"], ffn0, 0, grads_ready)
    dab = _mm(d1b, w0["w_out"], tb=True, name="mix0_dab", tm=1024, tn=1024)
    g_out = _mm(ab, d1b, ta=True, out_dtype=BF16, name="mix0_dwout", tm=1024, tn=512)
    dz, g_wa, g_wb, g_lg, g_lb = _mix0_bwd(z, ca, dab, wa, w0["ln_a_g"], w0["ln_a_b"], wb, name="mix0_midbwd")
    g_in = _mm(dz, h0, ta=True, out_dtype=BF16, name="mix0_dwin", tm=512, tn=1024)
    sent_mix = grads_ready("mix0", {"w_in": g_in, "w_out": g_out, "w_pool": g_pool})
    dx, g_nme = _mm_rms_bwd(dz, w0["w_in_t"], x, w0["norm_mix_even"], d1, sent_mix, name="mix0_dh")

    small = {
        "norm_mix_even": g_nme, "conv_a": g_wa[:A_TAPS], "ln_a_g": g_lg, "ln_a_b": g_lb, "conv_b": g_wb[:3],
        "norm_mix_odd": g_nmo, "pool_scale": g_scale, "norm_ffn": jnp.concatenate([g_nf0, g_nf1], axis=0),
        "conv_ffn": [g_wc0, g_wc1], "norm_final": g_norm_final,
    }
    return loss[0, 0], dx, small


def _my_pos():
    return lax.axis_index("x"), lax.axis_index("y"), lax.axis_index("c")


def _flip(pos, r):
    x, y, c = pos
    return (1 - x if r & 4 else x, 1 - y if r & 2 else y, 1 - c if r & 1 else c)


def _dev_index(pos):
    return 4 * pos[0] + 2 * pos[1] + pos[2]


_HBM = pl.BlockSpec(memory_space=pltpu.HBM)
_SEM = pl.BlockSpec(memory_space=pltpu.SEMAPHORE)
_EFFECT = pltpu.SideEffectType.DATAFLOW_SIDE_EFFECTING


def _exchange_copy(ins, lands, send_sems, recv_sems, scatter, pos, a, r, receiving):
    me = _dev_index(pos)
    peer = _flip(pos, r)
    dest = _dev_index(pos) if receiving else _dev_index(peer)
    src = ins[a].at[dest] if scatter[a] else ins[a]
    slot = _dev_index(peer) if receiving else me
    return pltpu.make_async_remote_copy(
        src_ref=src, dst_ref=lands[a].at[slot], send_sem=send_sems.at[a * (N_DEV - 1) + r - 1],
        recv_sem=recv_sems.at[a * (N_DEV - 1) + r - 1],
        device_id=peer, device_id_type=pl.DeviceIdType.MESH)


ALL_PEERS = tuple(range(1, N_DEV))
CHIP_PEERS = (1, 2, 4, 6)
FORWARDED = (2, 4, 6)


def _exchange_start(arrays, scatter, after, *, name, peers=ALL_PEERS):
    n = len(arrays)
    me = _dev_index(_my_pos())
    lands = []
    for arr, sc in zip(arrays, scatter):
        own = lax.dynamic_index_in_dim(arr, me, 0, keepdims=True) if sc else arr[None]
        shape = arr.shape if sc else (N_DEV,) + arr.shape
        lands.append(lax.dynamic_update_slice(lax.empty(shape, arr.dtype), own, (me,) + (0,) * (len(shape) - 1)))

    def body(*refs):
        ins, lnd = refs[:n], refs[n:2 * n]
        send_sems, recv_sems = refs[2 * n + 1], refs[2 * n + 2]
        token = refs[-1]
        pos = _my_pos()
        for a in range(n):
            for r in peers:
                _exchange_copy(ins, lnd, send_sems, recv_sems, scatter, pos, a, r, receiving=False).start()
        token[...] = jnp.zeros_like(token)

    bufs = [pltpu.with_memory_space_constraint(t, pltpu.HBM) for t in list(arrays) + lands]
    sems = pltpu.SemaphoreType.DMA((n * (N_DEV - 1),))
    res = pl.pallas_call(
        body, name=name,
        out_shape=(sems, sems, *[pltpu.HBM(t.shape, t.dtype) for t in bufs], jax.ShapeDtypeStruct((8, 128), F32)),
        in_specs=[_HBM] * (2 * n) + [pl.BlockSpec(memory_space=pl.ANY)],
        out_specs=(_SEM, _SEM, *[_HBM] * (2 * n), pl.BlockSpec(memory_space=pltpu.VMEM)),
        input_output_aliases={i: 2 + i for i in range(2 * n)},
        compiler_params=pltpu.CompilerParams(has_side_effects=_EFFECT),
    )(*bufs, after)
    return res[0], res[1], list(res[2:2 + n]), list(res[2 + n:2 + 2 * n]), res[-1]


def _exchange_wait(started, scatter, after, *, name, peers=ALL_PEERS):
    send_sems, recv_sems, arrays, lands, _ = started
    n = len(arrays)

    def body(*refs):
        ins, lnd = refs[:n], refs[n:2 * n]
        send_sems, recv_sems = refs[2 * n], refs[2 * n + 1]
        pos = _my_pos()
        for a in range(n):
            for r in peers:
                _exchange_copy(ins, lnd, send_sems, recv_sems, scatter, pos, a, r, receiving=False).wait_send()
                _exchange_copy(ins, lnd, send_sems, recv_sems, scatter, pos, a, r, receiving=True).wait_recv()

    bufs = list(arrays) + list(lands)
    after = list(after) if isinstance(after, (list, tuple)) else [after]
    res = pl.pallas_call(
        body, name=name,
        out_shape=tuple(pltpu.HBM(t.shape, t.dtype) for t in bufs),
        in_specs=[_HBM] * (2 * n) + [_SEM, _SEM] + [pl.BlockSpec(memory_space=pl.ANY)] * len(after),
        out_specs=tuple([_HBM] * (2 * n)),
        input_output_aliases={i: i for i in range(2 * n)},
        compiler_params=pltpu.CompilerParams(has_side_effects=_EFFECT),
    )(*bufs, send_sems, recv_sems, *after)
    return list(res[n:])


def _forward_copy(lands, send_sems, recv_sems, pos, a, q, receiving):
    slot = _dev_index(_flip(pos, q ^ 1 if receiving else q))
    idx = a * len(FORWARDED) + FORWARDED.index(q)
    return pltpu.make_async_remote_copy(
        src_ref=lands[a].at[slot], dst_ref=lands[a].at[slot], send_sem=send_sems.at[idx], recv_sem=recv_sems.at[idx],
        device_id=_flip(pos, 1), device_id_type=pl.DeviceIdType.MESH)


def _forward_start(lands, after, *, name):
    n = len(lands)

    def body(*refs):
        lnd = refs[:n]
        send_sems, recv_sems = refs[n + 1], refs[n + 2]
        token = refs[-1]
        pos = _my_pos()
        for a in range(n):
            for q in FORWARDED:
                _forward_copy(lnd, send_sems, recv_sems, pos, a, q, receiving=False).start()
        token[...] = jnp.zeros_like(token)

    sems = pltpu.SemaphoreType.DMA((n * len(FORWARDED),))
    res = pl.pallas_call(
        body, name=name,
        out_shape=(sems, sems, *[pltpu.HBM(t.shape, t.dtype) for t in lands], jax.ShapeDtypeStruct((8, 128), F32)),
        in_specs=[_HBM] * n + [pl.BlockSpec(memory_space=pl.ANY)],
        out_specs=(_SEM, _SEM, *[_HBM] * n, pl.BlockSpec(memory_space=pltpu.VMEM)),
        input_output_aliases={i: 2 + i for i in range(n)},
        compiler_params=pltpu.CompilerParams(has_side_effects=_EFFECT),
    )(*lands, after)
    return res[0], res[1], list(res[2:2 + n]), res[-1]


def _forward_wait(forwarded, after, *, name):
    send_sems, recv_sems, lands, _ = forwarded
    n = len(lands)

    def body(*refs):
        lnd = refs[:n]
        send_sems, recv_sems = refs[n], refs[n + 1]
        pos = _my_pos()
        for a in range(n):
            for q in FORWARDED:
                _forward_copy(lnd, send_sems, recv_sems, pos, a, q, receiving=False).wait_send()
                _forward_copy(lnd, send_sems, recv_sems, pos, a, q, receiving=True).wait_recv()

    res = pl.pallas_call(
        body, name=name,
        out_shape=tuple(pltpu.HBM(t.shape, t.dtype) for t in lands),
        in_specs=[_HBM] * n + [_SEM, _SEM, pl.BlockSpec(memory_space=pl.ANY)],
        out_specs=tuple([_HBM] * n),
        input_output_aliases={i: i for i in range(n)},
        compiler_params=pltpu.CompilerParams(has_side_effects=_EFFECT),
    )(*lands, send_sems, recv_sems, after)
    return list(res)


def _adamw_update(p_ref, w_ref, m_ref, v_ref, g_ref, d_ref, mo_ref, vo_ref):
    g = p_ref[0].astype(F32)
    for k in range(1, N_DEV):
        g = g + p_ref[k].astype(F32)
    mn = ADAM_B1 * m_ref[...] + (1.0 - ADAM_B1) * g
    vn = ADAM_B2 * v_ref[...] + (1.0 - ADAM_B2) * (g * g)
    m_hat = mn / (1.0 - ADAM_B1 ** ADAM_STEP)
    v_hat = vn / (1.0 - ADAM_B2 ** ADAM_STEP)
    g_ref[...] = g
    d_ref[...] = -ADAM_LR * (m_hat / (jnp.sqrt(v_hat) + ADAM_EPS) + ADAM_WD * w_ref[...])
    mo_ref[...] = mn
    vo_ref[...] = vn


def _adamw_small(parts, ws, ms, vs, *, name):
    n = len(ws)

    def body(*refs):
        ins, outs = refs[:4 * n], refs[4 * n:]
        for j in range(n):
            _adamw_update(ins[j], ins[n + j], ins[2 * n + j], ins[3 * n + j], *outs[4 * j:4 * j + 4])

    return pl.pallas_call(
        body, name=name,
        out_shape=[jax.ShapeDtypeStruct(w.shape, F32) for w in ws for _ in range(4)],
    )(*parts, *ws, *ms, *vs)


def _adamw(parts, w, m, v, *, name, tr):
    nl, r, c = w.shape
    assert len(parts) == nl and r % tr == 0
    n_i = r // tr

    def body(*refs):
        p_refs = refs[:nl]
        w_ref, m_ref, v_ref = refs[nl:nl + 3]

        def update(p_ref):
            _adamw_update(p_ref, w_ref, m_ref, v_ref, *refs[nl + 3:])

        if nl == 1:
            update(p_refs[0])
        else:
            for layer in range(nl):
                pl.when(pl.program_id(0) == layer)(lambda p_ref=p_refs[layer]: update(p_ref))

    def parts_spec(layer):
        def index(l, i):
            return (0, jnp.where(l < layer, 0, jnp.where(l > layer, n_i - 1, i)), 0)
        return pl.BlockSpec((N_DEV, tr, c), index)

    blk = pl.BlockSpec((None, tr, c), lambda l, i: (l, i, 0))
    return pl.pallas_call(
        body, name=name, grid=(nl, n_i),
        in_specs=[parts_spec(layer) for layer in range(nl)] + [blk, blk, blk],
        out_specs=[blk] * 4, out_shape=[jax.ShapeDtypeStruct((nl, r, c), F32)] * 4,
        compiler_params=_params(("arbitrary", "arbitrary")),
    )(*parts, w, m, v)


def _pack(parts, lead=()):
    flat = jnp.concatenate([p.reshape(lead + (-1,)) for p in parts], axis=-1)
    n = flat.shape[-1]
    rows = -(-n // (8 * 128)) * 8
    flat = jnp.pad(flat, [(0, 0)] * len(lead) + [(0, rows * 128 - n)])
    return flat.reshape(lead + (rows, 128))


def _to_dev_major(g, axis):
    shp = g.shape
    g = g.reshape(shp[:axis] + (N_DEV, shp[axis] // N_DEV) + shp[axis + 1:])
    return jnp.moveaxis(g, axis, 0)


def _from_dev_major(g, axis):
    g = jnp.moveaxis(g, 0, axis)
    shp = g.shape
    return g.reshape(shp[:axis] + (shp[axis] * shp[axis + 1],) + shp[axis + 2:])


SMALL_SHARDED = ("conv_a", "conv_b", "norm_mix_odd", "pool_scale", "conv_ffn_w")
SMALL_REPLICATED = ("norm_mix_even", "ln_a_g", "ln_a_b", "norm_ffn", "norm_final")
BIG = {"w_in": ("w_in", 0, 320), "w_out": ("w_out", 0, 128), "w_pool": ("w_pool", 0, 128),
       "w_up0": ("w_up", 0, 352), "w_up1": ("w_up", 1, 352),
       "w_down0": ("w_down", 0, 352), "w_down1": ("w_down", 1, 352)}
COLUMN_SHARDED = ("w_in", "w_up")


def kernel(x, norm_mix_even, w_in, conv_a, ln_a_g, ln_a_b, conv_b, w_out, norm_mix_odd, w_pool, pool_scale, norm_ffn, w_up, conv_ffn_w, w_down, norm_final, loss_target, m_norm_mix_even, m_w_in, m_conv_a, m_ln_a_g, m_ln_a_b, m_conv_b, m_w_out, m_norm_mix_odd, m_w_pool, m_pool_scale, m_norm_ffn, m_w_up, m_conv_ffn_w, m_w_down, m_norm_final, v_norm_mix_even, v_w_in, v_conv_a, v_ln_a_g, v_ln_a_b, v_conv_b, v_w_out, v_norm_mix_odd, v_w_pool, v_pool_scale, v_norm_ffn, v_w_up, v_conv_ffn_w, v_w_down, v_norm_final):
    names = ("norm_mix_even", "w_in", "conv_a", "ln_a_g", "ln_a_b", "conv_b", "w_out", "norm_mix_odd", "w_pool",
             "pool_scale", "norm_ffn", "w_up", "conv_ffn_w", "w_down", "norm_final")
    wts = dict(zip(names, (norm_mix_even, w_in, conv_a, ln_a_g, ln_a_b, conv_b, w_out, norm_mix_odd, w_pool,
                           pool_scale, norm_ffn, w_up, conv_ffn_w, w_down, norm_final)))
    mom = dict(zip(names, (m_norm_mix_even, m_w_in, m_conv_a, m_ln_a_g, m_ln_a_b, m_conv_b, m_w_out, m_norm_mix_odd,
                           m_w_pool, m_pool_scale, m_norm_ffn, m_w_up, m_conv_ffn_w, m_w_down, m_norm_final)))
    var = dict(zip(names, (v_norm_mix_even, v_w_in, v_conv_a, v_ln_a_g, v_ln_a_b, v_conv_b, v_w_out, v_norm_mix_odd,
                           v_w_pool, v_pool_scale, v_norm_ffn, v_w_up, v_conv_ffn_w, v_w_down, v_norm_final)))
    d = x.shape[-1]

    def shard3d(t, param):
        a = t[param]
        if param in COLUMN_SHARDED:
            return jnp.swapaxes(a, 1, 2)
        return a.reshape(a.shape[0], -1, a.shape[-1])

    def unshard3d(a, param):
        if param in COLUMN_SHARDED:
            return jnp.swapaxes(a, 1, 2)
        return a.reshape(wts[param].shape)

    def shard2d(t, key):
        param, layer, _ = BIG[key]
        return shard3d(t, param)[layer]

    small_w = _pack([wts[k] for k in SMALL_SHARDED])
    bf = {k: shard2d(wts, k).astype(BF16) for k in BIG}
    gather_groups = {"mix0": ("w_in", "w_out", "small"), "ffn0": ("w_up0", "w_down0"),
                     "ffn1": ("w_pool", "w_up1", "w_down1")}
    order = list(gather_groups)
    started = {}

    def start_gather(grp, after):
        arrs = [small_w if k == "small" else bf[k] for k in gather_groups[grp]]
        started[grp] = _exchange_start(arrs, [False] * len(arrs), after, name=f"gather_{grp}_start", peers=CHIP_PEERS)

    start_gather(order[0], small_w)

    def weights(grp, after):
        keys = gather_groups[grp]
        lands = _exchange_wait(started[grp], [False] * len(keys), after, name=f"gather_{grp}_wait", peers=CHIP_PEERS)
        forwarded = _forward_start(lands, small_w, name=f"gather_{grp}_forward")
        if grp != order[-1]:
            start_gather(order[order.index(grp) + 1], forwarded[-1])
            behind = started[order[order.index(grp) + 1]][-1]
        else:
            behind = forwarded[-1]
        gw = dict(zip(keys, _forward_wait(forwarded, behind, name=f"gather_{grp}_forward_wait")))
        if grp == "ffn0":
            return {"w_up": gw["w_up0"], "w_down": gw["w_down0"].reshape(N_PAIR, -1, d)}
        if grp == "ffn1":
            return {"w_up": gw["w_up1"], "w_down": gw["w_down1"].reshape(N_PAIR, -1, d),
                    "w_pool": _from_dev_major(gw["w_pool"].reshape(N_DEV, len(POOL_WINDOWS), -1, POOL_GROUP), 1)}
        per_dev = gw["small"].reshape(N_DEV, -1)
        sizes = [wts[k].size for k in SMALL_SHARDED]
        offs = [sum(sizes[:i]) for i in range(len(sizes))]
        small_full = {k: per_dev[:, o:o + n_].reshape((N_DEV,) + wts[k].shape)
                      for k, o, n_ in zip(SMALL_SHARDED, offs, sizes)}
        return {
            "norm_mix_even": norm_mix_even, "ln_a_g": ln_a_g, "ln_a_b": ln_a_b, "norm_ffn": norm_ffn,
            "norm_final": norm_final[None],
            "w_in_t": gw["w_in"].reshape(-1, d),
            "w_out": gw["w_out"].reshape(-1, d),
            "conv_a": _from_dev_major(small_full["conv_a"][:, 0], 1),
            "conv_b": _from_dev_major(small_full["conv_b"][:, 0], 1),
            "norm_mix_odd": _from_dev_major(small_full["norm_mix_odd"], 1),
            "pool_scale": _from_dev_major(small_full["pool_scale"], 1),
            "conv_ffn": [small_full["conv_ffn_w"][:, l] for l in range(2)],
        }

    def dev_major(k, g):
        if k == "w_pool":
            return _to_dev_major(g, 1).reshape(N_DEV, -1, POOL_GROUP)
        return g.reshape(N_DEV, -1, g.shape[-1])

    sent = {}

    def grads_ready(grp, grads):
        keys = tuple(grads)
        parts = [dev_major(k, grads[k]) for k in keys]
        sent[grp] = (keys, _exchange_start(parts, [True] * len(keys), small_w, name=f"grads_{grp}_start"))
        return sent[grp][1][-1]

    loss, dx, g = _local_step(x[0], loss_target[0], weights, grads_ready)

    def small2d(a):
        return a.reshape(-1, a.shape[-1])

    small_keys = SMALL_SHARDED + SMALL_REPLICATED
    small_parts = [_to_dev_major(g["conv_a"], 1), _to_dev_major(g["conv_b"], 1),
                   _to_dev_major(g["norm_mix_odd"], 1), _to_dev_major(g["pool_scale"], 1),
                   jnp.stack(g["conv_ffn"], axis=1).reshape(N_DEV, -1, w_up.shape[-1])]
    small_parts += [g[k] for k in SMALL_REPLICATED]
    small_scatter = [True] * len(SMALL_SHARDED) + [False] * len(SMALL_REPLICATED)
    small_parts.append(jnp.full((1, 128), loss, F32))
    small_scatter.append(False)
    sent["small"] = (small_keys + ("loss",),
                     _exchange_start(small_parts, small_scatter, dx, name="grads_small_start"))

    landed, out, raw = {}, {}, {}

    def wait_grads(grp, after, scat=None):
        keys, st = sent[grp]
        scat = [True] * len(keys) if scat is None else scat
        landed.update(zip(keys, _exchange_wait(st, scat, after, name=f"grads_{grp}_wait")))

    def update(param, keys):
        raw[param] = _adamw([landed[k] for k in keys], shard3d(wts, param), shard3d(mom, param),
                            shard3d(var, param), name=f"adamw_{param}", tr=BIG[keys[0]][2])
        out[param] = [unshard3d(t, param) for t in raw[param]]

    wait_grads("ffn1", sent["small"][1][-1])
    wait_grads("ffn0", landed["w_up1"])
    update("w_up", ("w_up0", "w_up1"))
    update("w_down", ("w_down0", "w_down1"))
    wait_grads("mix0", [raw["w_up"][1], raw["w_down"][1]])
    update("w_in", ("w_in",))
    update("w_out", ("w_out",))
    update("w_pool", ("w_pool",))
    wait_grads("small", raw["w_pool"][1], small_scatter)
    res = _adamw_small([landed[k] for k in small_keys], [small2d(wts[k]) for k in small_keys],
                       [small2d(mom[k]) for k in small_keys], [small2d(var[k]) for k in small_keys],
                       name="adamw_small")
    for j, k in enumerate(small_keys):
        out[k] = [t.reshape(wts[k].shape) for t in res[4 * j:4 * j + 4]]

    loss = jnp.sum(landed["loss"][:, 0, 0])
    return (loss, dx[None], *[out[k][0] for k in names], *[out[k][1] for k in names],
            *[out[k][2] for k in names], *[out[k][3] for k in names])
```

```python
import jax
import jax.numpy as jnp
from jax import lax
from jax.experimental import pallas as pl
from jax.experimental.pallas import tpu as pltpu

F32 = jnp.float32
BF16 = jnp.bfloat16

RMS_EPS = 1e-6
LN_EPS = 1e-5
ADAM_LR = 0.001
ADAM_B1 = 0.9
ADAM_B2 = 0.999
ADAM_EPS = 1e-08
ADAM_WD = 0.01
ADAM_STEP = 10

N_DEV = 8
N_PAIR = N_DEV // 2
A_WIDTH = 512
A_TAPS = 31
POOL_WINDOWS = (2, 4, 8, 16)
POOL_GROUP = 256
HALO_A = 32
HALO_S = 16
VMEM_LIMIT = 56 * 1024 * 1024


def _params(sem, vmem=VMEM_LIMIT):
    return pltpu.CompilerParams(dimension_semantics=sem, vmem_limit_bytes=vmem)


def _sigmoid(x):
    return 0.5 * jnp.tanh(0.5 * x) + 0.5


def _prev_blk(i, ts, hb):
    return jnp.maximum(i * (ts // hb) - 1, 0)


def _next_blk(i, ts, hb, s):
    return jnp.minimum((i + 1) * (ts // hb), s // hb - 1)


def _mm(a, b, *, name, ta=False, tb=False, add=None, out_dtype=F32, tm=512, tn=512, tk=None):
    m, k = (a.shape[1], a.shape[0]) if ta else a.shape
    n = b.shape[0] if tb else b.shape[1]
    tk = k if tk is None else tk
    tm, tn, tk = min(tm, m), min(tn, n), min(tk, k)
    assert m % tm == 0 and n % tn == 0 and k % tk == 0, (name, m, n, k, tm, tn, tk)
    nk = k // tk
    dims = (((0,) if ta else (1,), (1,) if tb else (0,)), ((), ()))
    n_in = 2 + (add is not None)

    def body(*refs):
        a_ref, b_ref = refs[0], refs[1]
        add_ref = refs[2] if add is not None else None
        o_ref = refs[n_in]
        part = lax.dot_general(a_ref[...].astype(BF16), b_ref[...].astype(BF16), dims, preferred_element_type=F32)

        def finish(r):
            if add_ref is not None:
                r = r + add_ref[...]
            o_ref[...] = r.astype(out_dtype)

        if nk == 1:
            finish(part)
            return
        acc_ref = refs[-1]
        kk = pl.program_id(2)

        @pl.when(kk == 0)
        def _():
            acc_ref[...] = part

        @pl.when(kk > 0)
        def _():
            acc_ref[...] += part

        @pl.when(kk == nk - 1)
        def _():
            finish(acc_ref[...])

    a_spec = pl.BlockSpec((tk, tm), lambda i, j, kk: (kk, i)) if ta else pl.BlockSpec((tm, tk), lambda i, j, kk: (i, kk))
    b_spec = pl.BlockSpec((tn, tk), lambda i, j, kk: (j, kk)) if tb else pl.BlockSpec((tk, tn), lambda i, j, kk: (kk, j))
    in_specs = [a_spec, b_spec]
    args = [a, b]
    if add is not None:
        in_specs.append(pl.BlockSpec((tm, tn), lambda i, j, kk: (i, j)))
        args.append(add)
    return pl.pallas_call(
        body, name=name, grid=(m // tm, n // tn, nk),
        in_specs=in_specs, out_specs=pl.BlockSpec((tm, tn), lambda i, j, kk: (i, j)),
        out_shape=jax.ShapeDtypeStruct((m, n), out_dtype),
        scratch_shapes=[pltpu.VMEM((tm, tn), F32)] if nk > 1 else [],
        compiler_params=_params(("parallel", "parallel", "arbitrary")),
    )(*args)


def _rms_rows(xv, gv):
    return xv * lax.rsqrt(jnp.mean(xv * xv, axis=-1, keepdims=True) + RMS_EPS) * gv


_NT = (((1,), (1,)), ((), ()))
_TN = (((0,), (0,)), ((), ()))


def _rms_mm(x, g, wt, *, name, out_dtype, tm=1024, tn=512):
    s, d = x.shape
    n = wt.shape[0]
    tm = min(tm, s)
    assert s % tm == 0 and n % tn == 0

    def body(x_ref, g_ref, w_ref, h_ref, z_ref, hs_ref):
        @pl.when(pl.program_id(1) == 0)
        def _():
            hv = _rms_rows(x_ref[...], g_ref[...]).astype(BF16)
            hs_ref[...] = hv
            h_ref[...] = hv

        z_ref[...] = lax.dot_general(hs_ref[...], w_ref[...], _NT, preferred_element_type=F32).astype(out_dtype)

    return pl.pallas_call(
        body, name=name, grid=(s // tm, n // tn),
        in_specs=[pl.BlockSpec((tm, d), lambda i, j: (i, 0)),
                  pl.BlockSpec((1, d), lambda i, j: (0, 0)),
                  pl.BlockSpec((tn, d), lambda i, j: (j, 0))],
        out_specs=[pl.BlockSpec((tm, d), lambda i, j: (i, 0)),
                   pl.BlockSpec((tm, tn), lambda i, j: (i, j))],
        out_shape=[jax.ShapeDtypeStruct((s, d), BF16), jax.ShapeDtypeStruct((s, n), out_dtype)],
        scratch_shapes=[pltpu.VMEM((tm, d), BF16)],
        compiler_params=_params(("parallel", "arbitrary")),
    )(x, g, wt)


def _rms_bwd_rows(xv, gv, dh):
    r = lax.rsqrt(jnp.mean(xv * xv, axis=-1, keepdims=True) + RMS_EPS)
    xh = xv * r
    dn = dh * gv
    dx = r * (dn - xh * jnp.mean(dn * xh, axis=-1, keepdims=True))
    return dx, dh * xh


def _mm_rms_bwd(a, b, x, g, dres, dep, *, name, tm=512):
    s, k = a.shape
    d = b.shape[1]
    tm = min(tm, s)

    def body(a_ref, b_ref, x_ref, g_ref, dres_ref, dep_ref, dx_ref, dg_ref):
        @pl.when(pl.program_id(0) == 0)
        def _():
            dg_ref[...] = jnp.zeros_like(dg_ref)

        dh = jnp.dot(a_ref[...], b_ref[...], preferred_element_type=F32)
        dx, dgr = _rms_bwd_rows(x_ref[...], g_ref[...], dh)
        dx_ref[...] = dres_ref[...] + dx
        dg_ref[...] += jnp.sum(dgr, axis=0, keepdims=True)

    row = pl.BlockSpec((tm, d), lambda i: (i, 0))
    vec = pl.BlockSpec((1, d), lambda i: (0, 0))
    return pl.pallas_call(
        body, name=name, grid=(s // tm,),
        in_specs=[pl.BlockSpec((tm, k), lambda i: (i, 0)), pl.BlockSpec((k, d), lambda i: (0, 0)), row, vec, row,
                  pl.BlockSpec(memory_space=pl.ANY)],
        out_specs=[row, vec],
        out_shape=[jax.ShapeDtypeStruct((s, d), F32), jax.ShapeDtypeStruct((1, d), F32)],
        compiler_params=_params(("arbitrary",)),
    )(a, b, x, g, dres, dep)


def _conv_taps(ext_ref, w_ref, n_taps, base, r0, rows, reverse=False):
    acc = None
    for k in range(n_taps):
        off = r0 + (base - k if reverse else base + k)
        term = w_ref[k:k + 1, :] * ext_ref[pl.ds(off, rows), :]
        acc = term if acc is None else acc + term
    return acc


def _shift_copies(src_ref, sh_ref, rows):
    for b in range(1, 8):
        sh_ref[b, 0:rows, :] = src_ref[pl.ds(b, rows), :]


def _shifted(src_ref, sh_ref, start, rows, off):
    a, b = divmod(off, 8)
    ref = src_ref if b == 0 else sh_ref.at[b]
    return ref[pl.ds(start + 8 * a, rows), :]


def _mix0_fwd(z, conv_a, ln_g, ln_b, conv_b, *, name, ts=256, rc=32):
    s = z.shape[0]
    c = A_WIDTH
    hb = HALO_A

    def body(z_ref, zp_ref, wa_ref, lg_ref, lb_ref, wb_ref, ab_ref, ca_ref, exta, extb, sha):
        keep = jnp.where(pl.program_id(0) > 0, 1.0, 0.0)
        zp = zp_ref[...]
        exta[0:hb, :] = zp[:, 0:c] * _sigmoid(zp[:, c:2 * c]) * keep
        extb[0:hb, :] = zp[:, 3 * c:4 * c] * zp[:, 4 * c:5 * c] * keep
        exta[hb:hb + ts, :] = z_ref[:, 0:c] * _sigmoid(z_ref[:, c:2 * c])
        extb[hb:hb + ts, :] = z_ref[:, 3 * c:4 * c] * z_ref[:, 4 * c:5 * c]
        _shift_copies(exta, sha, hb + ts - 8)
        lg = lg_ref[...]
        lb = lb_ref[...]
        for q in range(ts // rc):
            r0 = q * rc
            ca = None
            for k in range(A_TAPS):
                term = wa_ref[k:k + 1, :] * _shifted(exta, sha, r0, rc, hb - (A_TAPS - 1) + k)
                ca = term if ca is None else ca + term
            ca_ref[r0:r0 + rc, :] = ca
            mu = jnp.mean(ca, axis=-1, keepdims=True)
            xc = ca - mu
            rs = lax.rsqrt(jnp.mean(xc * xc, axis=-1, keepdims=True) + LN_EPS)
            l = xc * rs * lg + lb
            ab_ref[r0:r0 + rc, 0:c] = (l * _sigmoid(l)).astype(BF16)
            cbc = _conv_taps(extb, wb_ref, 3, hb - 2, r0, rc)
            ab_ref[r0:r0 + rc, c:2 * c] = (z_ref[r0:r0 + rc, 2 * c:3 * c] * cbc).astype(BF16)

    return pl.pallas_call(
        body, name=name, grid=(s // ts,),
        in_specs=[pl.BlockSpec((ts, 5 * c), lambda i: (i, 0)),
                  pl.BlockSpec((hb, 5 * c), lambda i: (_prev_blk(i, ts, hb), 0)),
                  pl.BlockSpec((32, c), lambda i: (0, 0)),
                  pl.BlockSpec((1, c), lambda i: (0, 0)),
                  pl.BlockSpec((1, c), lambda i: (0, 0)),
                  pl.BlockSpec((8, c), lambda i: (0, 0))],
        out_specs=[pl.BlockSpec((ts, 2 * c), lambda i: (i, 0)),
                   pl.BlockSpec((ts, c), lambda i: (i, 0))],
        out_shape=[jax.ShapeDtypeStruct((s, 2 * c), BF16), jax.ShapeDtypeStruct((s, c), F32)],
        scratch_shapes=[pltpu.VMEM((hb + ts, c), F32), pltpu.VMEM((hb + ts, c), F32),
                        pltpu.VMEM((8, hb + ts - 8, c), F32)],
        compiler_params=_params(("parallel",)),
    )(z, z, conv_a, ln_g, ln_b, conv_b)


def _mix0_bwd(z, ca, dab, conv_a, ln_g, ln_b, conv_b, *, name, ts=256, rc=32):
    s = z.shape[0]
    c = A_WIDTH
    hb = HALO_A
    ta = A_TAPS

    def body(z_ref, zp_ref, zn_ref, ca_ref, can_ref, d_ref, dn_ref, wa_ref, lg_ref, lb_ref, wb_ref,
             dz_ref, dwa_ref, dwb_ref, dlg_ref, dlb_ref, exta, extb, extdca, extdcb, shd):
        i = pl.program_id(0)
        keep_p = jnp.where(i > 0, 1.0, 0.0)
        keep_n = jnp.where(i < s // ts - 1, 1.0, 0.0)

        @pl.when(i == 0)
        def _():
            dwa_ref[...] = jnp.zeros_like(dwa_ref)
            dwb_ref[...] = jnp.zeros_like(dwb_ref)
            dlg_ref[...] = jnp.zeros_like(dlg_ref)
            dlb_ref[...] = jnp.zeros_like(dlb_ref)

        lg = lg_ref[...]
        lb = lb_ref[...]
        zp = zp_ref[...]
        exta[0:hb, :] = zp[:, 0:c] * _sigmoid(zp[:, c:2 * c]) * keep_p
        extb[0:hb, :] = zp[:, 3 * c:4 * c] * zp[:, 4 * c:5 * c] * keep_p
        exta[hb:hb + ts, :] = z_ref[:, 0:c] * _sigmoid(z_ref[:, c:2 * c])
        extb[hb:hb + ts, :] = z_ref[:, 3 * c:4 * c] * z_ref[:, 4 * c:5 * c]

        def ln_bwd(cav, dav):
            mu = jnp.mean(cav, axis=-1, keepdims=True)
            xc = cav - mu
            rs = lax.rsqrt(jnp.mean(xc * xc, axis=-1, keepdims=True) + LN_EPS)
            nv = xc * rs
            l = nv * lg + lb
            sg = _sigmoid(l)
            dl = dav * (sg * (1.0 + l * (1.0 - sg)))
            dnv = dl * lg
            dca = rs * (dnv - jnp.mean(dnv, axis=-1, keepdims=True)
                        - nv * jnp.mean(dnv * nv, axis=-1, keepdims=True))
            return dca, dl, nv

        dlg_acc = jnp.zeros((1, c), F32)
        dlb_acc = jnp.zeros((1, c), F32)
        for q in range(ts // rc):
            r0 = q * rc
            dca, dl, nv = ln_bwd(ca_ref[r0:r0 + rc, :], d_ref[r0:r0 + rc, 0:c])
            extdca[r0:r0 + rc, :] = dca
            dlg_acc = dlg_acc + jnp.sum(dl * nv, axis=0, keepdims=True)
            dlb_acc = dlb_acc + jnp.sum(dl, axis=0, keepdims=True)
            extdcb[r0:r0 + rc, :] = d_ref[r0:r0 + rc, c:2 * c] * z_ref[r0:r0 + rc, 2 * c:3 * c]
        dca_n, _, _ = ln_bwd(can_ref[...], dn_ref[:, 0:c])
        extdca[ts:ts + hb, :] = dca_n * keep_n
        extdcb[ts:ts + hb, :] = dn_ref[:, c:2 * c] * zn_ref[:, 2 * c:3 * c] * keep_n
        dlg_ref[...] += dlg_acc
        dlb_ref[...] += dlb_acc
        _shift_copies(extdca, shd, ts + hb - 8)

        for q in range(ts // rc):
            r0 = q * rc
            zr = z_ref[r0:r0 + rc, :]
            dga = None
            for k in range(ta):
                term = wa_ref[k:k + 1, :] * _shifted(extdca, shd, r0, rc, ta - 1 - k)
                dga = term if dga is None else dga + term
            sg = _sigmoid(zr[:, c:2 * c])
            dz_ref[r0:r0 + rc, 0:c] = (dga * sg).astype(BF16)
            dz_ref[r0:r0 + rc, c:2 * c] = (dga * zr[:, 0:c] * sg * (1.0 - sg)).astype(BF16)
            cbc = _conv_taps(extb, wb_ref, 3, hb - 2, r0, rc)
            dz_ref[r0:r0 + rc, 2 * c:3 * c] = (d_ref[r0:r0 + rc, c:2 * c] * cbc).astype(BF16)
            dcb = _conv_taps(extdcb, wb_ref, 3, 2, r0, rc, reverse=True)
            dz_ref[r0:r0 + rc, 3 * c:4 * c] = (dcb * zr[:, 4 * c:5 * c]).astype(BF16)
            dz_ref[r0:r0 + rc, 4 * c:5 * c] = (dcb * zr[:, 3 * c:4 * c]).astype(BF16)

        for k in range(ta):
            part = None
            for q in range(ts // rc):
                r0 = q * rc
                p = exta[hb + r0:hb + r0 + rc, :] * _shifted(extdca, shd, r0, rc, ta - 1 - k)
                for r in range(0, rc, 8):
                    part = p[r:r + 8, :] if part is None else part + p[r:r + 8, :]
            dwa_ref[k:k + 1, :] += jnp.sum(part, axis=0, keepdims=True)
        dcb_t = extdcb[0:ts, :]
        for k in range(3):
            dwb_ref[k:k + 1, :] += jnp.sum(dcb_t * extb[pl.ds(hb - 2 + k, ts), :], axis=0, keepdims=True)

    def tile(w):
        return pl.BlockSpec((ts, w), lambda i: (i, 0))

    def prev(w):
        return pl.BlockSpec((hb, w), lambda i: (_prev_blk(i, ts, hb), 0))

    def nxt(w):
        return pl.BlockSpec((hb, w), lambda i: (_next_blk(i, ts, hb, s), 0))

    def const(r, w):
        return pl.BlockSpec((r, w), lambda i: (0, 0))

    return pl.pallas_call(
        body, name=name, grid=(s // ts,),
        in_specs=[tile(5 * c), prev(5 * c), nxt(5 * c), tile(c), nxt(c), tile(2 * c), nxt(2 * c),
                  const(32, c), const(1, c), const(1, c), const(8, c)],
        out_specs=[tile(5 * c), const(32, c), const(8, c), const(1, c), const(1, c)],
        out_shape=[jax.ShapeDtypeStruct((s, 5 * c), BF16), jax.ShapeDtypeStruct((32, c), F32),
                   jax.ShapeDtypeStruct((8, c), F32), jax.ShapeDtypeStruct((1, c), F32),
                   jax.ShapeDtypeStruct((1, c), F32)],
        scratch_shapes=[pltpu.VMEM((hb + ts, c), F32), pltpu.VMEM((hb + ts, c), F32),
                        pltpu.VMEM((ts + hb, c), F32), pltpu.VMEM((ts + hb, c), F32),
                        pltpu.VMEM((8, ts + hb - 8, c), F32)],
        compiler_params=_params(("arbitrary",)),
    )(z, z, z, ca, ca, dab, dab, conv_a, ln_g, ln_b, conv_b)


def _ffn_up(x, g, w8, *, name, tm=2048):
    s, d = x.shape
    nb, c, _ = w8.shape
    tm = min(tm, s)

    def body(x_ref, g_ref, w_ref, h_ref, u_ref, hs_ref):
        @pl.when(pl.program_id(1) == 0)
        def _():
            hv = _rms_rows(x_ref[...], g_ref[...]).astype(BF16)
            hs_ref[...] = hv
            h_ref[...] = hv

        u_ref[...] = lax.dot_general(hs_ref[...], w_ref[...], _NT, preferred_element_type=F32).astype(BF16)

    return pl.pallas_call(
        body, name=name, grid=(s // tm, nb),
        in_specs=[pl.BlockSpec((tm, d), lambda i, k: (i, 0)),
                  pl.BlockSpec((1, d), lambda i, k: (0, 0)),
                  pl.BlockSpec((None, c, d), lambda i, k: (k, 0, 0))],
        out_specs=[pl.BlockSpec((tm, d), lambda i, k: (i, 0)),
                   pl.BlockSpec((None, tm, c), lambda i, k: (k, i, 0))],
        out_shape=[jax.ShapeDtypeStruct((s, d), BF16), jax.ShapeDtypeStruct((nb, s, c), BF16)],
        scratch_shapes=[pltpu.VMEM((tm, d), BF16)],
        compiler_params=_params(("parallel", "arbitrary")),
    )(x, g, w8)


def _ffn_mid_down(up8, wc8, wd4, x, head=None, *, name, ts=256, rc=32):
    nb, s, c = up8.shape
    d = x.shape[1]
    hb = HALO_S
    ts = min(ts, s)

    def body(*refs):
        u_ref, up_ref, wc_ref, wd_ref, x_ref = refs[:5]
        n_out = 4 if head is None else 7
        outs = refs[-1 - n_out:-1]
        act_ref, ug_ref, uv_ref = outs[-3:]
        ext = refs[-1]
        keep = jnp.where(pl.program_id(0) > 0, 1.0, 0.0)
        acc = x_ref[...]
        for j in range(N_PAIR):
            eg, ev = ext.at[2 * (j % 2)], ext.at[2 * (j % 2) + 1]
            eg[0:hb, :] = up_ref[j].astype(F32) * keep
            ev[0:hb, :] = up_ref[j + N_PAIR].astype(F32) * keep
            eg[hb:hb + ts, :] = u_ref[j].astype(F32)
            ev[hb:hb + ts, :] = u_ref[j + N_PAIR].astype(F32)
            for q in range(ts // rc):
                r0 = q * rc
                gg = _conv_taps(eg, wc_ref.at[j], 3, hb - 2, r0, rc)
                vv = _conv_taps(ev, wc_ref.at[j + N_PAIR], 3, hb - 2, r0, rc)
                ug_ref[j, r0:r0 + rc, :] = gg.astype(BF16)
                uv_ref[j, r0:r0 + rc, :] = vv.astype(BF16)
                act_ref[j, r0:r0 + rc, :] = (gg * _sigmoid(gg) * vv).astype(BF16)
            acc = acc + jnp.dot(act_ref[j], wd_ref[j], preferred_element_type=F32)
        if head is None:
            outs[0][...] = acc
            return
        g_ref, t_ref = refs[5], refs[6]
        loss_ref, dx_ref, dxb_ref, dg_ref = outs[:4]

        @pl.when(pl.program_id(0) == 0)
        def _():
            loss_ref[...] = jnp.zeros_like(loss_ref)
            dg_ref[...] = jnp.zeros_like(dg_ref)

        gv = g_ref[...]
        r = lax.rsqrt(jnp.mean(acc * acc, axis=-1, keepdims=True) + RMS_EPS)
        xh = acc * r
        err = xh * gv - t_ref[...]
        loss_ref[...] += 0.5 * jnp.sum(jnp.mean(err * err, axis=-1, keepdims=True), axis=0, keepdims=True)
        dy = err * (1.0 / d)
        dn = dy * gv
        dx = r * (dn - xh * jnp.mean(dn * xh, axis=-1, keepdims=True))
        dx_ref[...] = dx
        dxb_ref[...] = dx.astype(BF16)
        dg_ref[...] += jnp.sum(dy * xh, axis=0, keepdims=True)

    row = pl.BlockSpec((ts, d), lambda i: (i, 0))
    vec = pl.BlockSpec((1, d), lambda i: (0, 0))
    tile = pl.BlockSpec((N_PAIR, ts, c), lambda i: (0, i, 0))
    half = jax.ShapeDtypeStruct((N_PAIR, s, c), BF16)
    in_specs = [pl.BlockSpec((nb, ts, c), lambda i: (0, i, 0)),
                pl.BlockSpec((nb, hb, c), lambda i: (0, _prev_blk(i, ts, hb), 0)),
                pl.BlockSpec((nb, 8, c), lambda i: (0, 0, 0)),
                pl.BlockSpec((N_PAIR, c, d), lambda i: (0, 0, 0)), row]
    args = [up8, up8, wc8, wd4, x]
    if head is None:
        out_specs = [row, tile, tile, tile]
        out_shape = [jax.ShapeDtypeStruct((s, d), F32), half, half, half]
    else:
        in_specs += [vec, row]
        args += list(head)
        out_specs = [pl.BlockSpec((1, 1), lambda i: (0, 0)), row, row, vec, tile, tile, tile]
        out_shape = [jax.ShapeDtypeStruct((1, 1), F32), jax.ShapeDtypeStruct((s, d), F32),
                     jax.ShapeDtypeStruct((s, d), BF16), jax.ShapeDtypeStruct((1, d), F32), half, half, half]
    return pl.pallas_call(
        body, name=name, grid=(s // ts,), in_specs=in_specs, out_specs=out_specs, out_shape=out_shape,
        scratch_shapes=[pltpu.VMEM((4, hb + ts, c), F32)],
        compiler_params=_params(("parallel",) if head is None else ("arbitrary",)),
    )(*args)


def _ffn_dwdown(act4, db, *, name):
    npair, s, c = act4.shape
    d = db.shape[1]

    def body(a_ref, d_ref, o_ref):
        o_ref[...] = lax.dot_general(a_ref[...], d_ref[...], (((0,), (0,)), ((), ())),
                                     preferred_element_type=F32).astype(BF16)

    return pl.pallas_call(
        body, name=name, grid=(npair,),
        in_specs=[pl.BlockSpec((None, s, c), lambda j: (j, 0, 0)),
                  pl.BlockSpec((s, d), lambda j: (0, 0))],
        out_specs=pl.BlockSpec((None, c, d), lambda j: (j, 0, 0)),
        out_shape=jax.ShapeDtypeStruct((npair, c, d), BF16),
        compiler_params=_params(("parallel",)),
    )(act4, db)


def _ffn_dact(db, wd4, *, name, tm=1024):
    s, d = db.shape
    npair, c, _ = wd4.shape
    tm = min(tm, s)

    def body(d_ref, w_ref, o_ref):
        o_ref[...] = lax.dot_general(d_ref[...], w_ref[...], _NT, preferred_element_type=F32).astype(BF16)

    return pl.pallas_call(
        body, name=name, grid=(s // tm, npair),
        in_specs=[pl.BlockSpec((tm, d), lambda i, j: (i, 0)),
                  pl.BlockSpec((None, c, d), lambda i, j: (j, 0, 0))],
        out_specs=pl.BlockSpec((None, tm, c), lambda i, j: (j, i, 0)),
        out_shape=jax.ShapeDtypeStruct((npair, s, c), BF16),
        compiler_params=_params(("parallel", "parallel")),
    )(db, wd4)


def _ffn_midbwd(up8, ug4, uv4, dact4, wc8, *, name, ts=512, rc=32):
    nb, s, c = up8.shape
    hb = HALO_S
    ts = min(ts, s)
    n_i = s // ts

    def body(pg_ref, pv_ref, ug_ref, ugn_ref, uv_ref, uvn_ref, d_ref, dn_ref, wg_ref, wv_ref,
             dg_ref, dv_ref, dwg_ref, dwv_ref, extdg, extdv):
        i = pl.program_id(1)
        keep_n = jnp.where(i < n_i - 1, 1.0, 0.0)

        @pl.when(i == 0)
        def _():
            dwg_ref[...] = jnp.zeros_like(dwg_ref)
            dwv_ref[...] = jnp.zeros_like(dwv_ref)

        def du_rows(r0, rows, gg, vv, dav):
            sg = _sigmoid(gg)
            extdg[r0:r0 + rows, :] = dav * vv * (sg * (1.0 + gg * (1.0 - sg)))
            extdv[r0:r0 + rows, :] = dav * (gg * sg)

        for q in range(ts // rc):
            rows = slice(q * rc, q * rc + rc)
            du_rows(q * rc, rc, ug_ref[rows, :].astype(F32), uv_ref[rows, :].astype(F32),
                    d_ref[rows, :].astype(F32))
        du_rows(ts, 8, ugn_ref[0:8, :].astype(F32), uvn_ref[0:8, :].astype(F32),
                dn_ref[0:8, :].astype(F32) * keep_n)

        def fold(p):
            acc = p[0:8, :]
            for r in range(8, rc, 8):
                acc = acc + p[r:r + 8, :]
            return acc

        for extd, p_ref, w_ref, out_ref, dw_ref in ((extdg, pg_ref, wg_ref, dg_ref, dwg_ref),
                                                    (extdv, pv_ref, wv_ref, dv_ref, dwv_ref)):
            part = [None, None, None]
            for q in range(ts // rc):
                r0 = q * rc
                pre = p_ref[r0:r0 + rc, :].astype(F32)
                dup = None
                for k in range(3):
                    sh = extd[pl.ds(r0 + 2 - k, rc), :]
                    term = w_ref[k:k + 1, :] * sh
                    dup = term if dup is None else dup + term
                    prod = fold(pre * sh)
                    part[k] = prod if part[k] is None else part[k] + prod
                out_ref[r0:r0 + rc, :] = dup.astype(BF16)
            for k in range(3):
                dw_ref[k:k + 1, :] += jnp.sum(part[k], axis=0, keepdims=True)

    def blk(off, nxt):
        if nxt:
            return pl.BlockSpec((None, hb, c), lambda j, i: (j + off, _next_blk(i, ts, hb, s), 0))
        return pl.BlockSpec((None, ts, c), lambda j, i: (j + off, i, 0))

    def taps(off):
        return pl.BlockSpec((None, 8, c), lambda j, i: (j + off, 0, 0))

    tile = blk(0, False)
    acc = pl.BlockSpec((None, 8, c), lambda j, i: (j, 0, 0))
    return pl.pallas_call(
        body, name=name, grid=(N_PAIR, n_i),
        in_specs=[tile, blk(N_PAIR, False), tile, blk(0, True), tile, blk(0, True), tile, blk(0, True),
                  taps(0), taps(N_PAIR)],
        out_specs=[tile, tile, acc, acc],
        out_shape=[jax.ShapeDtypeStruct((N_PAIR, s, c), BF16), jax.ShapeDtypeStruct((N_PAIR, s, c), BF16),
                   jax.ShapeDtypeStruct((N_PAIR, 8, c), F32), jax.ShapeDtypeStruct((N_PAIR, 8, c), F32)],
        scratch_shapes=[pltpu.VMEM((ts + 8, c), F32), pltpu.VMEM((ts + 8, c), F32)],
        compiler_params=_params(("parallel", "arbitrary")),
    )(up8, up8, ug4, ug4, uv4, uv4, dact4, dact4, wc8, wc8)


def _ffn_dh(dupg, dupv, w8, x, g, dres, dep, *, name, tm=512):
    npair, s, c = dupg.shape
    d = x.shape[1]
    tm = min(tm, s)

    def body(dg_ref, dv_ref, w_ref, x_ref, g_ref, dres_ref, dep_ref, dx_ref, dxb_ref, dgain_ref):
        @pl.when(pl.program_id(0) == 0)
        def _():
            dgain_ref[...] = jnp.zeros_like(dgain_ref)

        dh = None
        for j in range(npair):
            for src, k in ((dg_ref, j), (dv_ref, j + npair)):
                part = jnp.dot(src[j], w_ref[k], preferred_element_type=F32)
                dh = part if dh is None else dh + part
        dx, dgr = _rms_bwd_rows(x_ref[...], g_ref[...], dh)
        dx = dres_ref[...] + dx
        dx_ref[...] = dx
        dxb_ref[...] = dx.astype(BF16)
        dgain_ref[...] += jnp.sum(dgr, axis=0, keepdims=True)

    row = pl.BlockSpec((tm, d), lambda i: (i, 0))
    vec = pl.BlockSpec((1, d), lambda i: (0, 0))
    dup = pl.BlockSpec((npair, tm, c), lambda i: (0, i, 0))
    return pl.pallas_call(
        body, name=name, grid=(s // tm,),
        in_specs=[dup, dup, pl.BlockSpec((2 * npair, c, d), lambda i: (0, 0, 0)), row, vec, row,
                  pl.BlockSpec(memory_space=pl.ANY)],
        out_specs=[row, row, vec],
        out_shape=[jax.ShapeDtypeStruct((s, d), F32), jax.ShapeDtypeStruct((s, d), BF16),
                   jax.ShapeDtypeStruct((1, d), F32)],
        compiler_params=_params(("arbitrary",)),
    )(dupg, dupv, w8, x, g, dres, dep)


def _ffn_dwup(h, dupg, dupv, *, name, tm=512):
    npair, s, c = dupg.shape
    d = h.shape[1]

    def body(h_ref, dg_ref, dv_ref, o_ref):
        k = pl.program_id(1)

        @pl.when(k < npair)
        def _():
            o_ref[...] = lax.dot_general(dg_ref[...], h_ref[...], _TN, preferred_element_type=F32).astype(BF16)

        @pl.when(k >= npair)
        def _():
            o_ref[...] = lax.dot_general(dv_ref[...], h_ref[...], _TN, preferred_element_type=F32).astype(BF16)

    return pl.pallas_call(
        body, name=name, grid=(d // tm, 2 * npair),
        in_specs=[pl.BlockSpec((s, tm), lambda m, k: (0, m)),
                  pl.BlockSpec((None, s, c), lambda m, k: (jnp.minimum(k, npair - 1), 0, 0)),
                  pl.BlockSpec((None, s, c), lambda m, k: (jnp.maximum(k - npair, 0), 0, 0))],
        out_specs=pl.BlockSpec((None, c, tm), lambda m, k: (k, 0, m)),
        out_shape=jax.ShapeDtypeStruct((2 * npair, c, d), BF16),
        compiler_params=_params(("parallel", "arbitrary")),
    )(h, dupg, dupv)


def _pool_counts(i, ts, rows, window):
    t = lax.broadcasted_iota(jnp.int32, (rows, 1), 0) + i * ts + 1
    return jnp.minimum(t, window).astype(F32)


def _pool_fwd(x, g, w_pool, scale, *, name, ts=256):
    s, d = x.shape
    hb = HALO_A
    pg = POOL_GROUP
    ts = min(ts, s)

    def body(x_ref, xp_ref, g_ref, w_ref, s_ref, o_ref, p_ref, yu_ref, ext, sa, sb):
        i = pl.program_id(0)
        keep = jnp.where(i > 0, 1.0, 0.0)
        gv = g_ref[...]
        ext[0:hb, :] = _rms_rows(xp_ref[...], gv) * keep
        ext[hb:hb + ts, :] = _rms_rows(x_ref[...], gv)
        rows = hb + ts - 8
        sa[0:8, :] = jnp.zeros((8, d), F32)
        sb[0:8, :] = jnp.zeros((8, d), F32)
        for gi, w in enumerate(POOL_WINDOWS):
            cols = slice(gi * pg, (gi + 1) * pg)
            cur, nxt, k = ext, sa, 1
            while k < w:
                nxt[8:8 + rows, cols] = cur[8:8 + rows, cols] + cur[pl.ds(8 - k, rows), cols]
                cur, nxt, k = nxt, (sb if nxt is sa else sa), 2 * k
            h = ext[hb:hb + ts, cols]
            pv = (cur[hb:hb + ts, cols] / _pool_counts(i, ts, ts, w) - h).astype(BF16)
            p_ref[:, cols] = pv
            yu = jnp.dot(pv, w_ref[gi], preferred_element_type=F32)
            yu_ref[:, cols] = yu.astype(BF16)
            o_ref[:, cols] = x_ref[:, cols] + yu * s_ref[:, cols]

    row = pl.BlockSpec((ts, d), lambda i: (i, 0))
    vec = pl.BlockSpec((1, d), lambda i: (0, 0))
    return pl.pallas_call(
        body, name=name, grid=(s // ts,),
        in_specs=[row, pl.BlockSpec((hb, d), lambda i: (_prev_blk(i, ts, hb), 0)), vec,
                  pl.BlockSpec((d // pg, pg, pg), lambda i: (0, 0, 0)), vec],
        out_specs=[row, row, row],
        out_shape=[jax.ShapeDtypeStruct((s, d), F32), jax.ShapeDtypeStruct((s, d), BF16),
                   jax.ShapeDtypeStruct((s, d), BF16)],
        scratch_shapes=[pltpu.VMEM((hb + ts, d), F32)] * 3,
        compiler_params=_params(("parallel",)),
    )(x, x, g, w_pool, scale)


def _pool_dw(p, dyc, *, name):
    s, d = p.shape
    pg = POOL_GROUP

    def body(p_ref, d_ref, o_ref):
        o_ref[...] = lax.dot_general(p_ref[...], d_ref[...], (((0,), (0,)), ((), ())),
                                     preferred_element_type=F32).astype(BF16)

    blk = pl.BlockSpec((s, pg), lambda gi: (0, gi))
    return pl.pallas_call(
        body, name=name, grid=(d // pg,),
        in_specs=[blk, blk], out_specs=pl.BlockSpec((None, pg, pg), lambda gi: (gi, 0, 0)),
        out_shape=jax.ShapeDtypeStruct((d // pg, pg, pg), BF16),
        compiler_params=_params(("parallel",)),
    )(p, dyc)


def _pool_bwd(dres, w_pool, scale, yu, x, g, *, name, ts=256):
    s, d = x.shape
    hb = HALO_A
    pg = POOL_GROUP
    ts = min(ts, s)
    n_i = s // ts

    def body(dres_ref, dresn_ref, w_ref, s_ref, yu_ref, x_ref, g_ref,
             dx_ref, dxb_ref, dg_ref, dyc_ref, ds_ref, ext, dh, sa, sb, dp_s):
        i = pl.program_id(0)
        keep_n = jnp.where(i < n_i - 1, 1.0, 0.0)

        @pl.when(i == 0)
        def _():
            dg_ref[...] = jnp.zeros_like(dg_ref)
            ds_ref[...] = jnp.zeros_like(ds_ref)

        ds_ref[...] += jnp.sum(dres_ref[...] * yu_ref[...].astype(F32), axis=0, keepdims=True)
        for gi, w in enumerate(POOL_WINDOWS):
            cols = slice(gi * pg, (gi + 1) * pg)
            dyc = (dres_ref[:, cols] * s_ref[:, cols]).astype(BF16)
            dyc_ref[:, cols] = dyc
            dp = lax.dot_general(dyc, w_ref[gi], _NT, preferred_element_type=F32)
            dp_s[:, cols] = dp
            ext[0:ts, cols] = dp / _pool_counts(i, ts, ts, w)
            dycn = (dresn_ref[:, cols] * s_ref[:, cols]).astype(BF16)
            dpn = lax.dot_general(dycn, w_ref[gi], _NT, preferred_element_type=F32)
            ext[ts:ts + hb, cols] = dpn / _pool_counts(i + 1, ts, hb, w) * keep_n
        rows = ts + hb - 8
        sa[rows:rows + 8, :] = jnp.zeros((8, d), F32)
        sb[rows:rows + 8, :] = jnp.zeros((8, d), F32)
        for gi, w in enumerate(POOL_WINDOWS):
            cols = slice(gi * pg, (gi + 1) * pg)
            cur, nxt, k = ext, sa, 1
            while k < w:
                nxt[0:rows, cols] = cur[0:rows, cols] + cur[pl.ds(k, rows), cols]
                cur, nxt, k = nxt, (sb if nxt is sa else sa), 2 * k
            dh[:, cols] = cur[0:ts, cols] - dp_s[:, cols]
        dx, dgr = _rms_bwd_rows(x_ref[...], g_ref[...], dh[...])
        dx = dres_ref[...] + dx
        dx_ref[...] = dx
        dxb_ref[...] = dx.astype(BF16)
        dg_ref[...] += jnp.sum(dgr, axis=0, keepdims=True)

    row = pl.BlockSpec((ts, d), lambda i: (i, 0))
    vec = pl.BlockSpec((1, d), lambda i: (0, 0))
    return pl.pallas_call(
        body, name=name, grid=(n_i,),
        in_specs=[row, pl.BlockSpec((hb, d), lambda i: (_next_blk(i, ts, hb, s), 0)),
                  pl.BlockSpec((d // pg, pg, pg), lambda i: (0, 0, 0)), vec, row, row, vec],
        out_specs=[row, row, vec, row, vec],
        out_shape=[jax.ShapeDtypeStruct((s, d), F32), jax.ShapeDtypeStruct((s, d), BF16),
                   jax.ShapeDtypeStruct((1, d), F32), jax.ShapeDtypeStruct((s, d), BF16),
                   jax.ShapeDtypeStruct((1, d), F32)],
        scratch_shapes=[pltpu.VMEM((ts + hb, d), F32), pltpu.VMEM((ts, d), F32),
                        pltpu.VMEM((ts + hb, d), F32), pltpu.VMEM((ts + hb, d), F32),
                        pltpu.VMEM((ts, d), F32)],
        compiler_params=_params(("arbitrary",)),
    )(dres, dres, w_pool, scale, yu, x, g)


def _pad_rows(w, rows):
    pad = [(0, 0)] * (w.ndim - 2) + [(0, rows - w.shape[-2]), (0, 0)]
    return jnp.pad(w, pad)


def _ffn_layer_fwd(x, nf, w8, wc8, wd4, tag, head=None):
    h, up8 = _ffn_up(x, nf, w8, name=f"ffn{tag}_up")
    res = _ffn_mid_down(up8, wc8, wd4, x, head, name=f"ffn{tag}_mid_down")
    act4, ug4, uv4 = res[-3:]
    out = res[0] if head is None else tuple(res[:4])
    return out, (h, up8, ug4, uv4, act4)


def _ffn_layer_bwd(d, db, x, nf, w8, wc8, wd4, saved, tag, grads_ready):
    h, up8, ug4, uv4, act4 = saved
    dact4 = _ffn_dact(db, wd4, name=f"ffn{tag}_dact")
    dwd4 = _ffn_dwdown(act4, db, name=f"ffn{tag}_dwdown")
    dupg, dupv, dwg, dwv = _ffn_midbwd(up8, ug4, uv4, dact4, wc8, name=f"ffn{tag}_midbwd")
    dw8 = _ffn_dwup(h, dupg, dupv, name=f"ffn{tag}_dwup")
    sent = grads_ready(f"ffn{tag}", {f"w_up{tag}": dw8, f"w_down{tag}": dwd4})
    dx, dxb, dnf = _ffn_dh(dupg, dupv, w8, x, nf, d, sent, name=f"ffn{tag}_dh")
    dwc8 = jnp.concatenate([dwg, dwv], axis=0)[:, :3]
    return dx, dxb, dnf, dwc8


def _local_step(x, target, weights, grads_ready):
    w0 = weights("mix0", x)
    wa = _pad_rows(w0["conv_a"], 32)
    wb = _pad_rows(w0["conv_b"], 8)
    wc = [_pad_rows(w0["conv_ffn"][l], 8) for l in range(2)]
    h0, z = _rms_mm(x, w0["norm_mix_even"], w0["w_in_t"], name="mix0_in", out_dtype=F32, tn=2560)
    ab, ca = _mix0_fwd(z, wa, w0["ln_a_g"], w0["ln_a_b"], wb, name="mix0_mid")
    x1 = _mm(ab, w0["w_out"], add=x, name="mix0_out", tm=1024, tn=1024)
    w1 = weights("ffn0", x1)
    x2, ffn0 = _ffn_layer_fwd(x1, w0["norm_ffn"][0:1], w1["w_up"], wc[0], w1["w_down"], 0)
    w2 = weights("ffn1", x2)
    x3, p, yu = _pool_fwd(x2, w0["norm_mix_odd"], w2["w_pool"], w0["pool_scale"], name="pool_fwd")
    (loss, d4, d4b, g_norm_final), ffn1 = _ffn_layer_fwd(
        x3, w0["norm_ffn"][1:2], w2["w_up"], wc[1], w2["w_down"], 1, head=(w0["norm_final"], target))

    d3, d3b, g_nf1, g_wc1 = _ffn_layer_bwd(
        d4, d4b, x3, w0["norm_ffn"][1:2], w2["w_up"], wc[1], w2["w_down"], ffn1, 1, grads_ready)
    d2, d2b, g_nmo, dyc, g_scale = _pool_bwd(d3, w2["w_pool"], w0["pool_scale"], yu, x2, w0["norm_mix_odd"],
                                             name="pool_bwd")
    g_pool = _pool_dw(p, dyc, name="pool_dw")
    d1, d1b, g_nf0, g_wc0 = _ffn_layer_bwd(
        d2, d2b, x1, w0["norm_ffn"][0:1], w1["w_up"], wc[0], w1["w_down"], ffn0, 0, grads_ready)
    dab = _mm(d1b, w0["w_out"], tb=True, name="mix0_dab", tm=1024, tn=1024)
    g_out = _mm(ab, d1b, ta=True, out_dtype=BF16, name="mix0_dwout", tm=1024, tn=512)
    dz, g_wa, g_wb, g_lg, g_lb = _mix0_bwd(z, ca, dab, wa, w0["ln_a_g"], w0["ln_a_b"], wb, name="mix0_midbwd")
    g_in = _mm(dz, h0, ta=True, out_dtype=BF16, name="mix0_dwin", tm=512, tn=1024)
    sent_mix = grads_ready("mix0", {"w_in": g_in, "w_out": g_out, "w_pool": g_pool})
    dx, g_nme = _mm_rms_bwd(dz, w0["w_in_t"], x, w0["norm_mix_even"], d1, sent_mix, name="mix0_dh")

    small = {
        "norm_mix_even": g_nme, "conv_a": g_wa[:A_TAPS], "ln_a_g": g_lg, "ln_a_b": g_lb, "conv_b": g_wb[:3],
        "norm_mix_odd": g_nmo, "pool_scale": g_scale, "norm_ffn": jnp.concatenate([g_nf0, g_nf1], axis=0),
        "conv_ffn": [g_wc0, g_wc1], "norm_final": g_norm_final,
    }
    return loss[0, 0], dx, small


def _my_pos():
    return lax.axis_index("x"), lax.axis_index("y"), lax.axis_index("c")


def _flip(pos, r):
    x, y, c = pos
    return (1 - x if r & 4 else x, 1 - y if r & 2 else y, 1 - c if r & 1 else c)


def _dev_index(pos):
    return 4 * pos[0] + 2 * pos[1] + pos[2]


_HBM = pl.BlockSpec(memory_space=pltpu.HBM)
_SEM = pl.BlockSpec(memory_space=pltpu.SEMAPHORE)
_EFFECT = pltpu.SideEffectType.DATAFLOW_SIDE_EFFECTING


def _exchange_copy(ins, lands, send_sems, recv_sems, scatter, pos, a, r, receiving):
    me = _dev_index(pos)
    peer = _flip(pos, r)
    dest = _dev_index(pos) if receiving else _dev_index(peer)
    src = ins[a].at[dest] if scatter[a] else ins[a]
    slot = _dev_index(peer) if receiving else me
    return pltpu.make_async_remote_copy(
        src_ref=src, dst_ref=lands[a].at[slot], send_sem=send_sems.at[a * (N_DEV - 1) + r - 1],
        recv_sem=recv_sems.at[a * (N_DEV - 1) + r - 1],
        device_id=peer, device_id_type=pl.DeviceIdType.MESH)


ALL_PEERS = tuple(range(1, N_DEV))
CHIP_PEERS = (1, 2, 4, 6)
FORWARDED = (2, 4, 6)


def _exchange_start(arrays, scatter, after, *, name, peers=ALL_PEERS):
    n = len(arrays)
    me = _dev_index(_my_pos())
    lands = []
    for arr, sc in zip(arrays, scatter):
        own = lax.dynamic_index_in_dim(arr, me, 0, keepdims=True) if sc else arr[None]
        shape = arr.shape if sc else (N_DEV,) + arr.shape
        lands.append(lax.dynamic_update_slice(lax.empty(shape, arr.dtype), own, (me,) + (0,) * (len(shape) - 1)))

    def body(*refs):
        ins, lnd = refs[:n], refs[n:2 * n]
        send_sems, recv_sems = refs[2 * n + 1], refs[2 * n + 2]
        token = refs[-1]
        pos = _my_pos()
        for a in range(n):
            for r in peers:
                _exchange_copy(ins, lnd, send_sems, recv_sems, scatter, pos, a, r, receiving=False).start()
        token[...] = jnp.zeros_like(token)

    bufs = [pltpu.with_memory_space_constraint(t, pltpu.HBM) for t in list(arrays) + lands]
    sems = pltpu.SemaphoreType.DMA((n * (N_DEV - 1),))
    res = pl.pallas_call(
        body, name=name,
        out_shape=(sems, sems, *[pltpu.HBM(t.shape, t.dtype) for t in bufs], jax.ShapeDtypeStruct((8, 128), F32)),
        in_specs=[_HBM] * (2 * n) + [pl.BlockSpec(memory_space=pl.ANY)],
        out_specs=(_SEM, _SEM, *[_HBM] * (2 * n), pl.BlockSpec(memory_space=pltpu.VMEM)),
        input_output_aliases={i: 2 + i for i in range(2 * n)},
        compiler_params=pltpu.CompilerParams(has_side_effects=_EFFECT),
    )(*bufs, after)
    return res[0], res[1], list(res[2:2 + n]), list(res[2 + n:2 + 2 * n]), res[-1]


def _exchange_wait(started, scatter, after, *, name, peers=ALL_PEERS):
    send_sems, recv_sems, arrays, lands, _ = started
    n = len(arrays)

    def body(*refs):
        ins, lnd = refs[:n], refs[n:2 * n]
        send_sems, recv_sems = refs[2 * n], refs[2 * n + 1]
        pos = _my_pos()
        for a in range(n):
            for r in peers:
                _exchange_copy(ins, lnd, send_sems, recv_sems, scatter, pos, a, r, receiving=False).wait_send()
                _exchange_copy(ins, lnd, send_sems, recv_sems, scatter, pos, a, r, receiving=True).wait_recv()

    bufs = list(arrays) + list(lands)
    after = list(after) if isinstance(after, (list, tuple)) else [after]
    res = pl.pallas_call(
        body, name=name,
        out_shape=tuple(pltpu.HBM(t.shape, t.dtype) for t in bufs),
        in_specs=[_HBM] * (2 * n) + [_SEM, _SEM] + [pl.BlockSpec(memory_space=pl.ANY)] * len(after),
        out_specs=tuple([_HBM] * (2 * n)),
        input_output_aliases={i: i for i in range(2 * n)},
        compiler_params=pltpu.CompilerParams(has_side_effects=_EFFECT),
    )(*bufs, send_sems, recv_sems, *after)
    return list(res[n:])


def _forward_copy(lands, send_sems, recv_sems, pos, a, q, receiving):
    slot = _dev_index(_flip(pos, q ^ 1 if receiving else q))
    idx = a * len(FORWARDED) + FORWARDED.index(q)
    return pltpu.make_async_remote_copy(
        src_ref=lands[a].at[slot], dst_ref=lands[a].at[slot], send_sem=send_sems.at[idx], recv_sem=recv_sems.at[idx],
        device_id=_flip(pos, 1), device_id_type=pl.DeviceIdType.MESH)


def _forward_start(lands, after, *, name):
    n = len(lands)

    def body(*refs):
        lnd = refs[:n]
        send_sems, recv_sems = refs[n + 1], refs[n + 2]
        token = refs[-1]
        pos = _my_pos()
        for a in range(n):
            for q in FORWARDED:
                _forward_copy(lnd, send_sems, recv_sems, pos, a, q, receiving=False).start()
        token[...] = jnp.zeros_like(token)

    sems = pltpu.SemaphoreType.DMA((n * len(FORWARDED),))
    res = pl.pallas_call(
        body, name=name,
        out_shape=(sems, sems, *[pltpu.HBM(t.shape, t.dtype) for t in lands], jax.ShapeDtypeStruct((8, 128), F32)),
        in_specs=[_HBM] * n + [pl.BlockSpec(memory_space=pl.ANY)],
        out_specs=(_SEM, _SEM, *[_HBM] * n, pl.BlockSpec(memory_space=pltpu.VMEM)),
        input_output_aliases={i: 2 + i for i in range(n)},
        compiler_params=pltpu.CompilerParams(has_side_effects=_EFFECT),
    )(*lands, after)
    return res[0], res[1], list(res[2:2 + n]), res[-1]


def _forward_wait(forwarded, after, *, name):
    send_sems, recv_sems, lands, _ = forwarded
    n = len(lands)

    def body(*refs):
        lnd = refs[:n]
        send_sems, recv_sems = refs[n], refs[n + 1]
        pos = _my_pos()
        for a in range(n):
            for q in FORWARDED:
                _forward_copy(lnd, send_sems, recv_sems, pos, a, q, receiving=False).wait_send()
                _forward_copy(lnd, send_sems, recv_sems, pos, a, q, receiving=True).wait_recv()

    res = pl.pallas_call(
        body, name=name,
        out_shape=tuple(pltpu.HBM(t.shape, t.dtype) for t in lands),
        in_specs=[_HBM] * n + [_SEM, _SEM, pl.BlockSpec(memory_space=pl.ANY)],
        out_specs=tuple([_HBM] * n),
        input_output_aliases={i: i for i in range(n)},
        compiler_params=pltpu.CompilerParams(has_side_effects=_EFFECT),
    )(*lands, send_sems, recv_sems, after)
    return list(res)


def _adamw_update(p_ref, w_ref, m_ref, v_ref, g_ref, d_ref, mo_ref, vo_ref):
    g = p_ref[0].astype(F32)
    for k in range(1, N_DEV):
        g = g + p_ref[k].astype(F32)
    mn = ADAM_B1 * m_ref[...] + (1.0 - ADAM_B1) * g
    vn = ADAM_B2 * v_ref[...] + (1.0 - ADAM_B2) * (g * g)
    m_hat = mn / (1.0 - ADAM_B1 ** ADAM_STEP)
    v_hat = vn / (1.0 - ADAM_B2 ** ADAM_STEP)
    g_ref[...] = g
    d_ref[...] = -ADAM_LR * (m_hat / (jnp.sqrt(v_hat) + ADAM_EPS) + ADAM_WD * w_ref[...])
    mo_ref[...] = mn
    vo_ref[...] = vn


def _adamw_small(parts, ws, ms, vs, *, name):
    n = len(ws)

    def body(*refs):
        ins, outs = refs[:4 * n], refs[4 * n:]
        for j in range(n):
            _adamw_update(ins[j], ins[n + j], ins[2 * n + j], ins[3 * n + j], *outs[4 * j:4 * j + 4])

    return pl.pallas_call(
        body, name=name,
        out_shape=[jax.ShapeDtypeStruct(w.shape, F32) for w in ws for _ in range(4)],
    )(*parts, *ws, *ms, *vs)


def _adamw(parts, w, m, v, *, name, tr):
    nl, r, c = w.shape
    assert len(parts) == nl and r % tr == 0
    n_i = r // tr

    def body(*refs):
        p_refs = refs[:nl]
        w_ref, m_ref, v_ref = refs[nl:nl + 3]

        def update(p_ref):
            _adamw_update(p_ref, w_ref, m_ref, v_ref, *refs[nl + 3:])

        if nl == 1:
            update(p_refs[0])
        else:
            for layer in range(nl):
                pl.when(pl.program_id(0) == layer)(lambda p_ref=p_refs[layer]: update(p_ref))

    def parts_spec(layer):
        def index(l, i):
            return (0, jnp.where(l < layer, 0, jnp.where(l > layer, n_i - 1, i)), 0)
        return pl.BlockSpec((N_DEV, tr, c), index)

    blk = pl.BlockSpec((None, tr, c), lambda l, i: (l, i, 0))
    return pl.pallas_call(
        body, name=name, grid=(nl, n_i),
        in_specs=[parts_spec(layer) for layer in range(nl)] + [blk, blk, blk],
        out_specs=[blk] * 4, out_shape=[jax.ShapeDtypeStruct((nl, r, c), F32)] * 4,
        compiler_params=_params(("arbitrary", "arbitrary")),
    )(*parts, w, m, v)


def _pack(parts, lead=()):
    flat = jnp.concatenate([p.reshape(lead + (-1,)) for p in parts], axis=-1)
    n = flat.shape[-1]
    rows = -(-n // (8 * 128)) * 8
    flat = jnp.pad(flat, [(0, 0)] * len(lead) + [(0, rows * 128 - n)])
    return flat.reshape(lead + (rows, 128))


def _to_dev_major(g, axis):
    shp = g.shape
    g = g.reshape(shp[:axis] + (N_DEV, shp[axis] // N_DEV) + shp[axis + 1:])
    return jnp.moveaxis(g, axis, 0)


def _from_dev_major(g, axis):
    g = jnp.moveaxis(g, 0, axis)
    shp = g.shape
    return g.reshape(shp[:axis] + (shp[axis] * shp[axis + 1],) + shp[axis + 2:])


SMALL_SHARDED = ("conv_a", "conv_b", "norm_mix_odd", "pool_scale", "conv_ffn_w")
SMALL_REPLICATED = ("norm_mix_even", "ln_a_g", "ln_a_b", "norm_ffn", "norm_final")
BIG = {"w_in": ("w_in", 0, 320), "w_out": ("w_out", 0, 128), "w_pool": ("w_pool", 0, 128),
       "w_up0": ("w_up", 0, 352), "w_up1": ("w_up", 1, 352),
       "w_down0": ("w_down", 0, 352), "w_down1": ("w_down", 1, 352)}
COLUMN_SHARDED = ("w_in", "w_up")


def kernel(x, norm_mix_even, w_in, conv_a, ln_a_g, ln_a_b, conv_b, w_out, norm_mix_odd, w_pool, pool_scale, norm_ffn, w_up, conv_ffn_w, w_down, norm_final, loss_target, m_norm_mix_even, m_w_in, m_conv_a, m_ln_a_g, m_ln_a_b, m_conv_b, m_w_out, m_norm_mix_odd, m_w_pool, m_pool_scale, m_norm_ffn, m_w_up, m_conv_ffn_w, m_w_down, m_norm_final, v_norm_mix_even, v_w_in, v_conv_a, v_ln_a_g, v_ln_a_b, v_conv_b, v_w_out, v_norm_mix_odd, v_w_pool, v_pool_scale, v_norm_ffn, v_w_up, v_conv_ffn_w, v_w_down, v_norm_final):
    names = ("norm_mix_even", "w_in", "conv_a", "ln_a_g", "ln_a_b", "conv_b", "w_out", "norm_mix_odd", "w_pool",
             "pool_scale", "norm_ffn", "w_up", "conv_ffn_w", "w_down", "norm_final")
    wts = dict(zip(names, (norm_mix_even, w_in, conv_a, ln_a_g, ln_a_b, conv_b, w_out, norm_mix_odd, w_pool,
                           pool_scale, norm_ffn, w_up, conv_ffn_w, w_down, norm_final)))
    mom = dict(zip(names, (m_norm_mix_even, m_w_in, m_conv_a, m_ln_a_g, m_ln_a_b, m_conv_b, m_w_out, m_norm_mix_odd,
                           m_w_pool, m_pool_scale, m_norm_ffn, m_w_up, m_conv_ffn_w, m_w_down, m_norm_final)))
    var = dict(zip(names, (v_norm_mix_even, v_w_in, v_conv_a, v_ln_a_g, v_ln_a_b, v_conv_b, v_w_out, v_norm_mix_odd,
                           v_w_pool, v_pool_scale, v_norm_ffn, v_w_up, v_conv_ffn_w, v_w_down, v_norm_final)))
    d = x.shape[-1]

    def shard3d(t, param):
        a = t[param]
        if param in COLUMN_SHARDED:
            return jnp.swapaxes(a, 1, 2)
        return a.reshape(a.shape[0], -1, a.shape[-1])

    def unshard3d(a, param):
        if param in COLUMN_SHARDED:
            return jnp.swapaxes(a, 1, 2)
        return a.reshape(wts[param].shape)

    def shard2d(t, key):
        param, layer, _ = BIG[key]
        return shard3d(t, param)[layer]

    small_w = _pack([wts[k] for k in SMALL_SHARDED])
    bf = {k: shard2d(wts, k).astype(BF16) for k in BIG}
    gather_groups = {"mix0": ("w_in", "w_out", "small"), "ffn0": ("w_up0", "w_down0"),
                     "ffn1": ("w_pool", "w_up1", "w_down1")}
    order = list(gather_groups)
    started = {}

    def start_gather(grp, after):
        arrs = [small_w if k == "small" else bf[k] for k in gather_groups[grp]]
        started[grp] = _exchange_start(arrs, [False] * len(arrs), after, name=f"gather_{grp}_start", peers=CHIP_PEERS)

    start_gather(order[0], small_w)

    def weights(grp, after):
        keys = gather_groups[grp]
        lands = _exchange_wait(started[grp], [False] * len(keys), after, name=f"gather_{grp}_wait", peers=CHIP_PEERS)
        forwarded = _forward_start(lands, small_w, name=f"gather_{grp}_forward")
        if grp != order[-1]:
            start_gather(order[order.index(grp) + 1], forwarded[-1])
            behind = started[order[order.index(grp) + 1]][-1]
        else:
            behind = forwarded[-1]
        gw = dict(zip(keys, _forward_wait(forwarded, behind, name=f"gather_{grp}_forward_wait")))
        if grp == "ffn0":
            return {"w_up": gw["w_up0"], "w_down": gw["w_down0"].reshape(N_PAIR, -1, d)}
        if grp == "ffn1":
            return {"w_up": gw["w_up1"], "w_down": gw["w_down1"].reshape(N_PAIR, -1, d),
                    "w_pool": _from_dev_major(gw["w_pool"].reshape(N_DEV, len(POOL_WINDOWS), -1, POOL_GROUP), 1)}
        per_dev = gw["small"].reshape(N_DEV, -1)
        sizes = [wts[k].size for k in SMALL_SHARDED]
        offs = [sum(sizes[:i]) for i in range(len(sizes))]
        small_full = {k: per_dev[:, o:o + n_].reshape((N_DEV,) + wts[k].shape)
                      for k, o, n_ in zip(SMALL_SHARDED, offs, sizes)}
        return {
            "norm_mix_even": norm_mix_even, "ln_a_g": ln_a_g, "ln_a_b": ln_a_b, "norm_ffn": norm_ffn,
            "norm_final": norm_final[None],
            "w_in_t": gw["w_in"].reshape(-1, d),
            "w_out": gw["w_out"].reshape(-1, d),
            "conv_a": _from_dev_major(small_full["conv_a"][:, 0], 1),
            "conv_b": _from_dev_major(small_full["conv_b"][:, 0], 1),
            "norm_mix_odd": _from_dev_major(small_full["norm_mix_odd"], 1),
            "pool_scale": _from_dev_major(small_full["pool_scale"], 1),
            "conv_ffn": [small_full["conv_ffn_w"][:, l] for l in range(2)],
        }

    def dev_major(k, g):
        if k == "w_pool":
            return _to_dev_major(g, 1).reshape(N_DEV, -1, POOL_GROUP)
        return g.reshape(N_DEV, -1, g.shape[-1])

    sent = {}

    def grads_ready(grp, grads):
        keys = tuple(grads)
        parts = [dev_major(k, grads[k]) for k in keys]
        sent[grp] = (keys, _exchange_start(parts, [True] * len(keys), small_w, name=f"grads_{grp}_start"))
        return sent[grp][1][-1]

    loss, dx, g = _local_step(x[0], loss_target[0], weights, grads_ready)

    def small2d(a):
        return a.reshape(-1, a.shape[-1])

    small_keys = SMALL_SHARDED + SMALL_REPLICATED
    small_parts = [_to_dev_major(g["conv_a"], 1), _to_dev_major(g["conv_b"], 1),
                   _to_dev_major(g["norm_mix_odd"], 1), _to_dev_major(g["pool_scale"], 1),
                   jnp.stack(g["conv_ffn"], axis=1).reshape(N_DEV, -1, w_up.shape[-1])]
    small_parts += [g[k] for k in SMALL_REPLICATED]
    small_scatter = [True] * len(SMALL_SHARDED) + [False] * len(SMALL_REPLICATED)
    small_parts.append(jnp.full((1, 128), loss, F32))
    small_scatter.append(False)
    sent["small"] = (small_keys + ("loss",),
                     _exchange_start(small_parts, small_scatter, dx, name="grads_small_start"))

    landed, out, raw = {}, {}, {}

    def wait_grads(grp, after, scat=None):
        keys, st = sent[grp]
        scat = [True] * len(keys) if scat is None else scat
        landed.update(zip(keys, _exchange_wait(st, scat, after, name=f"grads_{grp}_wait")))

    def update(param, keys):
        raw[param] = _adamw([landed[k] for k in keys], shard3d(wts, param), shard3d(mom, param),
                            shard3d(var, param), name=f"adamw_{param}", tr=BIG[keys[0]][2])
        out[param] = [unshard3d(t, param) for t in raw[param]]

    wait_grads("ffn1", sent["small"][1][-1])
    wait_grads("ffn0", landed["w_up1"])
    update("w_up", ("w_up0", "w_up1"))
    update("w_down", ("w_down0", "w_down1"))
    wait_grads("mix0", [raw["w_up"][1], raw["w_down"][1]])
    update("w_in", ("w_in",))
    update("w_out", ("w_out",))
    update("w_pool", ("w_pool",))
    wait_grads("small", raw["w_pool"][1], small_scatter)
    res = _adamw_small([landed[k] for k in small_keys], [small2d(wts[k]) for k in small_keys],
                       [small2d(mom[k]) for k in small_keys], [small2d(var[k]) for k in small_keys],
                       name="adamw_small")
    for j, k in enumerate(small_keys):
        out[k] = [t.reshape(wts[k].shape) for t in res[4 * j:4 * j + 4]]

    loss = jnp.sum(landed["loss"][:, 0, 0])
    return (loss, dx[None], *[out[k][0] for k in names], *[out[k][1] for k in names],
            *[out[k][2] for k in names], *[out[k][3] for k in names])
```

```python
import jax
import jax.numpy as jnp
from jax import lax
from jax.experimental import pallas as pl
from jax.experimental.pallas import tpu as pltpu

F32 = jnp.float32
BF16 = jnp.bfloat16

RMS_EPS = 1e-6
LN_EPS = 1e-5
ADAM_LR = 0.001
ADAM_B1 = 0.9
ADAM_B2 = 0.999
ADAM_EPS = 1e-08
ADAM_WD = 0.01
ADAM_STEP = 10

N_DEV = 8
N_PAIR = N_DEV // 2
A_WIDTH = 512
A_TAPS = 31
POOL_WINDOWS = (2, 4, 8, 16)
POOL_GROUP = 256
HALO_A = 32
HALO_S = 16
VMEM_LIMIT = 56 * 1024 * 1024


def _params(sem, vmem=VMEM_LIMIT):
    return pltpu.CompilerParams(dimension_semantics=sem, vmem_limit_bytes=vmem)


def _sigmoid(x):
    return 0.5 * jnp.tanh(0.5 * x) + 0.5


def _prev_blk(i, ts, hb):
    return jnp.maximum(i * (ts // hb) - 1, 0)


def _next_blk(i, ts, hb, s):
    return jnp.minimum((i + 1) * (ts // hb), s // hb - 1)


def _mm(a, b, *, name, ta=False, tb=False, add=None, out_dtype=F32, tm=512, tn=512, tk=None):
    m, k = (a.shape[1], a.shape[0]) if ta else a.shape
    n = b.shape[0] if tb else b.shape[1]
    tk = k if tk is None else tk
    tm, tn, tk = min(tm, m), min(tn, n), min(tk, k)
    assert m % tm == 0 and n % tn == 0 and k % tk == 0, (name, m, n, k, tm, tn, tk)
    nk = k // tk
    dims = (((0,) if ta else (1,), (1,) if tb else (0,)), ((), ()))
    n_in = 2 + (add is not None)

    def body(*refs):
        a_ref, b_ref = refs[0], refs[1]
        add_ref = refs[2] if add is not None else None
        o_ref = refs[n_in]
        part = lax.dot_general(a_ref[...].astype(BF16), b_ref[...].astype(BF16), dims, preferred_element_type=F32)

        def finish(r):
            if add_ref is not None:
                r = r + add_ref[...]
            o_ref[...] = r.astype(out_dtype)

        if nk == 1:
            finish(part)
            return
        acc_ref = refs[-1]
        kk = pl.program_id(2)

        @pl.when(kk == 0)
        def _():
            acc_ref[...] = part

        @pl.when(kk > 0)
        def _():
            acc_ref[...] += part

        @pl.when(kk == nk - 1)
        def _():
            finish(acc_ref[...])

    a_spec = pl.BlockSpec((tk, tm), lambda i, j, kk: (kk, i)) if ta else pl.BlockSpec((tm, tk), lambda i, j, kk: (i, kk))
    b_spec = pl.BlockSpec((tn, tk), lambda i, j, kk: (j, kk)) if tb else pl.BlockSpec((tk, tn), lambda i, j, kk: (kk, j))
    in_specs = [a_spec, b_spec]
    args = [a, b]
    if add is not None:
        in_specs.append(pl.BlockSpec((tm, tn), lambda i, j, kk: (i, j)))
        args.append(add)
    return pl.pallas_call(
        body, name=name, grid=(m // tm, n // tn, nk),
        in_specs=in_specs, out_specs=pl.BlockSpec((tm, tn), lambda i, j, kk: (i, j)),
        out_shape=jax.ShapeDtypeStruct((m, n), out_dtype),
        scratch_shapes=[pltpu.VMEM((tm, tn), F32)] if nk > 1 else [],
        compiler_params=_params(("parallel", "parallel", "arbitrary")),
    )(*args)


def _rms_rows(xv, gv):
    return xv * lax.rsqrt(jnp.mean(xv * xv, axis=-1, keepdims=True) + RMS_EPS) * gv


_NT = (((1,), (1,)), ((), ()))
_TN = (((0,), (0,)), ((), ()))


def _rms_mm(x, g, wt, *, name, out_dtype, tm=1024, tn=512):
    s, d = x.shape
    n = wt.shape[0]
    tm = min(tm, s)
    assert s % tm == 0 and n % tn == 0

    def body(x_ref, g_ref, w_ref, h_ref, z_ref, hs_ref):
        @pl.when(pl.program_id(1) == 0)
        def _():
            hv = _rms_rows(x_ref[...], g_ref[...]).astype(BF16)
            hs_ref[...] = hv
            h_ref[...] = hv

        z_ref[...] = lax.dot_general(hs_ref[...], w_ref[...], _NT, preferred_element_type=F32).astype(out_dtype)

    return pl.pallas_call(
        body, name=name, grid=(s // tm, n // tn),
        in_specs=[pl.BlockSpec((tm, d), lambda i, j: (i, 0)),
                  pl.BlockSpec((1, d), lambda i, j: (0, 0)),
                  pl.BlockSpec((tn, d), lambda i, j: (j, 0))],
        out_specs=[pl.BlockSpec((tm, d), lambda i, j: (i, 0)),
                   pl.BlockSpec((tm, tn), lambda i, j: (i, j))],
        out_shape=[jax.ShapeDtypeStruct((s, d), BF16), jax.ShapeDtypeStruct((s, n), out_dtype)],
        scratch_shapes=[pltpu.VMEM((tm, d), BF16)],
        compiler_params=_params(("parallel", "arbitrary")),
    )(x, g, wt)


def _rms_bwd_rows(xv, gv, dh):
    r = lax.rsqrt(jnp.mean(xv * xv, axis=-1, keepdims=True) + RMS_EPS)
    xh = xv * r
    dn = dh * gv
    dx = r * (dn - xh * jnp.mean(dn * xh, axis=-1, keepdims=True))
    return dx, dh * xh


def _mm_rms_bwd(a, b, x, g, dres, dep, *, name, tm=512):
    s, k = a.shape
    d = b.shape[1]
    tm = min(tm, s)

    def body(a_ref, b_ref, x_ref, g_ref, dres_ref, dep_ref, dx_ref, dg_ref):
        @pl.when(pl.program_id(0) == 0)
        def _():
            dg_ref[...] = jnp.zeros_like(dg_ref)

        dh = jnp.dot(a_ref[...], b_ref[...], preferred_element_type=F32)
        dx, dgr = _rms_bwd_rows(x_ref[...], g_ref[...], dh)
        dx_ref[...] = dres_ref[...] + dx
        dg_ref[...] += jnp.sum(dgr, axis=0, keepdims=True)

    row = pl.BlockSpec((tm, d), lambda i: (i, 0))
    vec = pl.BlockSpec((1, d), lambda i: (0, 0))
    return pl.pallas_call(
        body, name=name, grid=(s // tm,),
        in_specs=[pl.BlockSpec((tm, k), lambda i: (i, 0)), pl.BlockSpec((k, d), lambda i: (0, 0)), row, vec, row,
                  pl.BlockSpec(memory_space=pl.ANY)],
        out_specs=[row, vec],
        out_shape=[jax.ShapeDtypeStruct((s, d), F32), jax.ShapeDtypeStruct((1, d), F32)],
        compiler_params=_params(("arbitrary",)),
    )(a, b, x, g, dres, dep)


def _conv_taps(ext_ref, w_ref, n_taps, base, r0, rows, reverse=False):
    acc = None
    for k in range(n_taps):
        off = r0 + (base - k if reverse else base + k)
        term = w_ref[k:k + 1, :] * ext_ref[pl.ds(off, rows), :]
        acc = term if acc is None else acc + term
    return acc


def _shift_copies(src_ref, sh_ref, rows):
    for b in range(1, 8):
        sh_ref[b, 0:rows, :] = src_ref[pl.ds(b, rows), :]


def _shifted(src_ref, sh_ref, start, rows, off):
    a, b = divmod(off, 8)
    ref = src_ref if b == 0 else sh_ref.at[b]
    return ref[pl.ds(start + 8 * a, rows), :]


def _mix0_fwd(z, conv_a, ln_g, ln_b, conv_b, w_out, x, *, name, ts=256, rc=32):
    s = z.shape[0]
    c = A_WIDTH
    hb = HALO_A
    half = ts // 2

    def body(z_ref, zp_ref, wa_ref, lg_ref, lb_ref, wb_ref, wo_ref, x_ref, ab_ref, ca_ref, o_ref, exta, extb, sha):
        keep = jnp.where(pl.program_id(0) > 0, 1.0, 0.0)
        zp = zp_ref[...]
        exta[0:hb, :] = zp[:, 0:c] * _sigmoid(zp[:, c:2 * c]) * keep
        extb[0:hb, :] = zp[:, 3 * c:4 * c] * zp[:, 4 * c:5 * c] * keep
        exta[hb:hb + ts, :] = z_ref[:, 0:c] * _sigmoid(z_ref[:, c:2 * c])
        extb[hb:hb + ts, :] = z_ref[:, 3 * c:4 * c] * z_ref[:, 4 * c:5 * c]
        _shift_copies(exta, sha, hb + ts - 8)
        lg = lg_ref[...]
        lb = lb_ref[...]
        for q in range(ts // rc):
            r0 = q * rc
            ca = None
            for k in range(A_TAPS):
                term = wa_ref[k:k + 1, :] * _shifted(exta, sha, r0, rc, hb - (A_TAPS - 1) + k)
                ca = term if ca is None else ca + term
            ca_ref[r0:r0 + rc, :] = ca
            mu = jnp.mean(ca, axis=-1, keepdims=True)
            xc = ca - mu
            rs = lax.rsqrt(jnp.mean(xc * xc, axis=-1, keepdims=True) + LN_EPS)
            l = xc * rs * lg + lb
            ab_ref[r0:r0 + rc, 0:c] = (l * _sigmoid(l)).astype(BF16)
            cbc = _conv_taps(extb, wb_ref, 3, hb - 2, r0, rc)
            ab_ref[r0:r0 + rc, c:2 * c] = (z_ref[r0:r0 + rc, 2 * c:3 * c] * cbc).astype(BF16)
            if (r0 + rc) % half == 0:
                rows = slice(r0 + rc - half, r0 + rc)
                o_ref[rows, :] = x_ref[rows, :] + jnp.dot(ab_ref[rows, :], wo_ref[...], preferred_element_type=F32)

    d = x.shape[1]
    row = pl.BlockSpec((ts, d), lambda i: (i, 0))
    return pl.pallas_call(
        body, name=name, grid=(s // ts,),
        in_specs=[pl.BlockSpec((ts, 5 * c), lambda i: (i, 0)),
                  pl.BlockSpec((hb, 5 * c), lambda i: (_prev_blk(i, ts, hb), 0)),
                  pl.BlockSpec((32, c), lambda i: (0, 0)),
                  pl.BlockSpec((1, c), lambda i: (0, 0)),
                  pl.BlockSpec((1, c), lambda i: (0, 0)),
                  pl.BlockSpec((8, c), lambda i: (0, 0)),
                  pl.BlockSpec((2 * c, d), lambda i: (0, 0)), row],
        out_specs=[pl.BlockSpec((ts, 2 * c), lambda i: (i, 0)),
                   pl.BlockSpec((ts, c), lambda i: (i, 0)), row],
        out_shape=[jax.ShapeDtypeStruct((s, 2 * c), BF16), jax.ShapeDtypeStruct((s, c), F32),
                   jax.ShapeDtypeStruct((s, d), F32)],
        scratch_shapes=[pltpu.VMEM((hb + ts, c), F32), pltpu.VMEM((hb + ts, c), F32),
                        pltpu.VMEM((8, hb + ts - 8, c), F32)],
        compiler_params=_params(("parallel",)),
    )(z, z, conv_a, ln_g, ln_b, conv_b, w_out, x)


def _mix0_bwd(z, ca, dab, conv_a, ln_g, ln_b, conv_b, *, name, ts=256, rc=32):
    s = z.shape[0]
    c = A_WIDTH
    hb = HALO_A
    ta = A_TAPS

    def body(z_ref, zp_ref, zn_ref, ca_ref, can_ref, d_ref, dn_ref, wa_ref, lg_ref, lb_ref, wb_ref,
             dz_ref, dwa_ref, dwb_ref, dlg_ref, dlb_ref, exta, extb, extdca, extdcb, shd):
        i = pl.program_id(0)
        keep_p = jnp.where(i > 0, 1.0, 0.0)
        keep_n = jnp.where(i < s // ts - 1, 1.0, 0.0)

        @pl.when(i == 0)
        def _():
            dwa_ref[...] = jnp.zeros_like(dwa_ref)
            dwb_ref[...] = jnp.zeros_like(dwb_ref)
            dlg_ref[...] = jnp.zeros_like(dlg_ref)
            dlb_ref[...] = jnp.zeros_like(dlb_ref)

        lg = lg_ref[...]
        lb = lb_ref[...]
        zp = zp_ref[...]
        exta[0:hb, :] = zp[:, 0:c] * _sigmoid(zp[:, c:2 * c]) * keep_p
        extb[0:hb, :] = zp[:, 3 * c:4 * c] * zp[:, 4 * c:5 * c] * keep_p
        exta[hb:hb + ts, :] = z_ref[:, 0:c] * _sigmoid(z_ref[:, c:2 * c])
        extb[hb:hb + ts, :] = z_ref[:, 3 * c:4 * c] * z_ref[:, 4 * c:5 * c]

        def ln_bwd(cav, dav):
            mu = jnp.mean(cav, axis=-1, keepdims=True)
            xc = cav - mu
            rs = lax.rsqrt(jnp.mean(xc * xc, axis=-1, keepdims=True) + LN_EPS)
            nv = xc * rs
            l = nv * lg + lb
            sg = _sigmoid(l)
            dl = dav * (sg * (1.0 + l * (1.0 - sg)))
            dnv = dl * lg
            dca = rs * (dnv - jnp.mean(dnv, axis=-1, keepdims=True)
                        - nv * jnp.mean(dnv * nv, axis=-1, keepdims=True))
            return dca, dl, nv

        dlg_acc = jnp.zeros((1, c), F32)
        dlb_acc = jnp.zeros((1, c), F32)
        for q in range(ts // rc):
            r0 = q * rc
            dca, dl, nv = ln_bwd(ca_ref[r0:r0 + rc, :], d_ref[r0:r0 + rc, 0:c])
            extdca[r0:r0 + rc, :] = dca
            dlg_acc = dlg_acc + jnp.sum(dl * nv, axis=0, keepdims=True)
            dlb_acc = dlb_acc + jnp.sum(dl, axis=0, keepdims=True)
            extdcb[r0:r0 + rc, :] = d_ref[r0:r0 + rc, c:2 * c] * z_ref[r0:r0 + rc, 2 * c:3 * c]
        dca_n, _, _ = ln_bwd(can_ref[...], dn_ref[:, 0:c])
        extdca[ts:ts + hb, :] = dca_n * keep_n
        extdcb[ts:ts + hb, :] = dn_ref[:, c:2 * c] * zn_ref[:, 2 * c:3 * c] * keep_n
        dlg_ref[...] += dlg_acc
        dlb_ref[...] += dlb_acc
        _shift_copies(extdca, shd, ts + hb - 8)

        for q in range(ts // rc):
            r0 = q * rc
            zr = z_ref[r0:r0 + rc, :]
            dga = None
            for k in range(ta):
                term = wa_ref[k:k + 1, :] * _shifted(extdca, shd, r0, rc, ta - 1 - k)
                dga = term if dga is None else dga + term
            sg = _sigmoid(zr[:, c:2 * c])
            dz_ref[r0:r0 + rc, 0:c] = (dga * sg).astype(BF16)
            dz_ref[r0:r0 + rc, c:2 * c] = (dga * zr[:, 0:c] * sg * (1.0 - sg)).astype(BF16)
            cbc = _conv_taps(extb, wb_ref, 3, hb - 2, r0, rc)
            dz_ref[r0:r0 + rc, 2 * c:3 * c] = (d_ref[r0:r0 + rc, c:2 * c] * cbc).astype(BF16)
            dcb = _conv_taps(extdcb, wb_ref, 3, 2, r0, rc, reverse=True)
            dz_ref[r0:r0 + rc, 3 * c:4 * c] = (dcb * zr[:, 4 * c:5 * c]).astype(BF16)
            dz_ref[r0:r0 + rc, 4 * c:5 * c] = (dcb * zr[:, 3 * c:4 * c]).astype(BF16)

        for k in range(ta):
            part = None
            for q in range(ts // rc):
                r0 = q * rc
                p = exta[hb + r0:hb + r0 + rc, :] * _shifted(extdca, shd, r0, rc, ta - 1 - k)
                for r in range(0, rc, 8):
                    part = p[r:r + 8, :] if part is None else part + p[r:r + 8, :]
            dwa_ref[k:k + 1, :] += jnp.sum(part, axis=0, keepdims=True)
        dcb_t = extdcb[0:ts, :]
        for k in range(3):
            dwb_ref[k:k + 1, :] += jnp.sum(dcb_t * extb[pl.ds(hb - 2 + k, ts), :], axis=0, keepdims=True)

    def tile(w):
        return pl.BlockSpec((ts, w), lambda i: (i, 0))

    def prev(w):
        return pl.BlockSpec((hb, w), lambda i: (_prev_blk(i, ts, hb), 0))

    def nxt(w):
        return pl.BlockSpec((hb, w), lambda i: (_next_blk(i, ts, hb, s), 0))

    def const(r, w):
        return pl.BlockSpec((r, w), lambda i: (0, 0))

    return pl.pallas_call(
        body, name=name, grid=(s // ts,),
        in_specs=[tile(5 * c), prev(5 * c), nxt(5 * c), tile(c), nxt(c), tile(2 * c), nxt(2 * c),
                  const(32, c), const(1, c), const(1, c), const(8, c)],
        out_specs=[tile(5 * c), const(32, c), const(8, c), const(1, c), const(1, c)],
        out_shape=[jax.ShapeDtypeStruct((s, 5 * c), BF16), jax.ShapeDtypeStruct((32, c), F32),
                   jax.ShapeDtypeStruct((8, c), F32), jax.ShapeDtypeStruct((1, c), F32),
                   jax.ShapeDtypeStruct((1, c), F32)],
        scratch_shapes=[pltpu.VMEM((hb + ts, c), F32), pltpu.VMEM((hb + ts, c), F32),
                        pltpu.VMEM((ts + hb, c), F32), pltpu.VMEM((ts + hb, c), F32),
                        pltpu.VMEM((8, ts + hb - 8, c), F32)],
        compiler_params=_params(("arbitrary",)),
    )(z, z, z, ca, ca, dab, dab, conv_a, ln_g, ln_b, conv_b)


def _ffn_up(x, g, w8, *, name, tm=2048):
    s, d = x.shape
    nb, c, _ = w8.shape
    tm = min(tm, s)

    def body(x_ref, g_ref, w_ref, h_ref, u_ref, hs_ref):
        @pl.when(pl.program_id(1) == 0)
        def _():
            hv = _rms_rows(x_ref[...], g_ref[...]).astype(BF16)
            hs_ref[...] = hv
            h_ref[...] = hv

        u_ref[...] = lax.dot_general(hs_ref[...], w_ref[...], _NT, preferred_element_type=F32).astype(BF16)

    return pl.pallas_call(
        body, name=name, grid=(s // tm, nb),
        in_specs=[pl.BlockSpec((tm, d), lambda i, k: (i, 0)),
                  pl.BlockSpec((1, d), lambda i, k: (0, 0)),
                  pl.BlockSpec((None, c, d), lambda i, k: (k, 0, 0))],
        out_specs=[pl.BlockSpec((tm, d), lambda i, k: (i, 0)),
                   pl.BlockSpec((None, tm, c), lambda i, k: (k, i, 0))],
        out_shape=[jax.ShapeDtypeStruct((s, d), BF16), jax.ShapeDtypeStruct((nb, s, c), BF16)],
        scratch_shapes=[pltpu.VMEM((tm, d), BF16)],
        compiler_params=_params(("parallel", "arbitrary")),
    )(x, g, w8)


def _ffn_mid_down(up8, wc8, wd4, x, head=None, *, name, ts=256, rc=32):
    nb, s, c = up8.shape
    d = x.shape[1]
    hb = HALO_S
    ts = min(ts, s)

    def body(*refs):
        u_ref, up_ref, wc_ref, wd_ref, x_ref = refs[:5]
        n_out = 4 if head is None else 7
        outs = refs[-1 - n_out:-1]
        act_ref, ug_ref, uv_ref = outs[-3:]
        ext = refs[-1]
        keep = jnp.where(pl.program_id(0) > 0, 1.0, 0.0)
        acc = x_ref[...]
        for j in range(N_PAIR):
            eg, ev = ext.at[2 * (j % 2)], ext.at[2 * (j % 2) + 1]
            eg[0:hb, :] = up_ref[j].astype(F32) * keep
            ev[0:hb, :] = up_ref[j + N_PAIR].astype(F32) * keep
            eg[hb:hb + ts, :] = u_ref[j].astype(F32)
            ev[hb:hb + ts, :] = u_ref[j + N_PAIR].astype(F32)
            for q in range(ts // rc):
                r0 = q * rc
                gg = _conv_taps(eg, wc_ref.at[j], 3, hb - 2, r0, rc)
                vv = _conv_taps(ev, wc_ref.at[j + N_PAIR], 3, hb - 2, r0, rc)
                ug_ref[j, r0:r0 + rc, :] = gg.astype(BF16)
                uv_ref[j, r0:r0 + rc, :] = vv.astype(BF16)
                act_ref[j, r0:r0 + rc, :] = (gg * _sigmoid(gg) * vv).astype(BF16)
            acc = acc + jnp.dot(act_ref[j], wd_ref[j], preferred_element_type=F32)
        if head is None:
            outs[0][...] = acc
            return
        g_ref, t_ref = refs[5], refs[6]
        loss_ref, dx_ref, dxb_ref, dg_ref = outs[:4]

        @pl.when(pl.program_id(0) == 0)
        def _():
            loss_ref[...] = jnp.zeros_like(loss_ref)
            dg_ref[...] = jnp.zeros_like(dg_ref)

        gv = g_ref[...]
        r = lax.rsqrt(jnp.mean(acc * acc, axis=-1, keepdims=True) + RMS_EPS)
        xh = acc * r
        err = xh * gv - t_ref[...]
        loss_ref[...] += 0.5 * jnp.sum(jnp.mean(err * err, axis=-1, keepdims=True), axis=0, keepdims=True)
        dy = err * (1.0 / d)
        dn = dy * gv
        dx = r * (dn - xh * jnp.mean(dn * xh, axis=-1, keepdims=True))
        dx_ref[...] = dx
        dxb_ref[...] = dx.astype(BF16)
        dg_ref[...] += jnp.sum(dy * xh, axis=0, keepdims=True)

    row = pl.BlockSpec((ts, d), lambda i: (i, 0))
    vec = pl.BlockSpec((1, d), lambda i: (0, 0))
    tile = pl.BlockSpec((N_PAIR, ts, c), lambda i: (0, i, 0))
    half = jax.ShapeDtypeStruct((N_PAIR, s, c), BF16)
    in_specs = [pl.BlockSpec((nb, ts, c), lambda i: (0, i, 0)),
                pl.BlockSpec((nb, hb, c), lambda i: (0, _prev_blk(i, ts, hb), 0)),
                pl.BlockSpec((nb, 8, c), lambda i: (0, 0, 0)),
                pl.BlockSpec((N_PAIR, c, d), lambda i: (0, 0, 0)), row]
    args = [up8, up8, wc8, wd4, x]
    if head is None:
        out_specs = [row, tile, tile, tile]
        out_shape = [jax.ShapeDtypeStruct((s, d), F32), half, half, half]
    else:
        in_specs += [vec, row]
        args += list(head)
        out_specs = [pl.BlockSpec((1, 1), lambda i: (0, 0)), row, row, vec, tile, tile, tile]
        out_shape = [jax.ShapeDtypeStruct((1, 1), F32), jax.ShapeDtypeStruct((s, d), F32),
                     jax.ShapeDtypeStruct((s, d), BF16), jax.ShapeDtypeStruct((1, d), F32), half, half, half]
    return pl.pallas_call(
        body, name=name, grid=(s // ts,), in_specs=in_specs, out_specs=out_specs, out_shape=out_shape,
        scratch_shapes=[pltpu.VMEM((4, hb + ts, c), F32)],
        compiler_params=_params(("parallel",) if head is None else ("arbitrary",)),
    )(*args)


def _ffn_dwdown(act4, db, *, name):
    npair, s, c = act4.shape
    d = db.shape[1]

    def body(a_ref, d_ref, o_ref):
        o_ref[...] = lax.dot_general(a_ref[...], d_ref[...], (((0,), (0,)), ((), ())),
                                     preferred_element_type=F32).astype(BF16)

    return pl.pallas_call(
        body, name=name, grid=(npair,),
        in_specs=[pl.BlockSpec((None, s, c), lambda j: (j, 0, 0)),
                  pl.BlockSpec((s, d), lambda j: (0, 0))],
        out_specs=pl.BlockSpec((None, c, d), lambda j: (j, 0, 0)),
        out_shape=jax.ShapeDtypeStruct((npair, c, d), BF16),
        compiler_params=_params(("parallel",)),
    )(act4, db)


def _ffn_dact(db, wd4, *, name, tm=1024):
    s, d = db.shape
    npair, c, _ = wd4.shape
    tm = min(tm, s)

    def body(d_ref, w_ref, o_ref):
        o_ref[...] = lax.dot_general(d_ref[...], w_ref[...], _NT, preferred_element_type=F32).astype(BF16)

    return pl.pallas_call(
        body, name=name, grid=(s // tm, npair),
        in_specs=[pl.BlockSpec((tm, d), lambda i, j: (i, 0)),
                  pl.BlockSpec((None, c, d), lambda i, j: (j, 0, 0))],
        out_specs=pl.BlockSpec((None, tm, c), lambda i, j: (j, i, 0)),
        out_shape=jax.ShapeDtypeStruct((npair, s, c), BF16),
        compiler_params=_params(("parallel", "parallel")),
    )(db, wd4)


def _ffn_midbwd(up8, ug4, uv4, dact4, wc8, *, name, ts=512, rc=32):
    nb, s, c = up8.shape
    hb = HALO_S
    ts = min(ts, s)
    n_i = s // ts

    def body(pg_ref, pv_ref, ug_ref, ugn_ref, uv_ref, uvn_ref, d_ref, dn_ref, wg_ref, wv_ref,
             dg_ref, dv_ref, dwg_ref, dwv_ref, extdg, extdv):
        i = pl.program_id(1)
        keep_n = jnp.where(i < n_i - 1, 1.0, 0.0)

        @pl.when(i == 0)
        def _():
            dwg_ref[...] = jnp.zeros_like(dwg_ref)
            dwv_ref[...] = jnp.zeros_like(dwv_ref)

        def du_rows(r0, rows, gg, vv, dav):
            sg = _sigmoid(gg)
            extdg[r0:r0 + rows, :] = dav * vv * (sg * (1.0 + gg * (1.0 - sg)))
            extdv[r0:r0 + rows, :] = dav * (gg * sg)

        for q in range(ts // rc):
            rows = slice(q * rc, q * rc + rc)
            du_rows(q * rc, rc, ug_ref[rows, :].astype(F32), uv_ref[rows, :].astype(F32),
                    d_ref[rows, :].astype(F32))
        du_rows(ts, 8, ugn_ref[0:8, :].astype(F32), uvn_ref[0:8, :].astype(F32),
                dn_ref[0:8, :].astype(F32) * keep_n)

        def fold(p):
            acc = p[0:8, :]
            for r in range(8, rc, 8):
                acc = acc + p[r:r + 8, :]
            return acc

        for extd, p_ref, w_ref, out_ref, dw_ref in ((extdg, pg_ref, wg_ref, dg_ref, dwg_ref),
                                                    (extdv, pv_ref, wv_ref, dv_ref, dwv_ref)):
            part = [None, None, None]
            for q in range(ts // rc):
                r0 = q * rc
                pre = p_ref[r0:r0 + rc, :].astype(F32)
                dup = None
                for k in range(3):
                    sh = extd[pl.ds(r0 + 2 - k, rc), :]
                    term = w_ref[k:k + 1, :] * sh
                    dup = term if dup is None else dup + term
                    prod = fold(pre * sh)
                    part[k] = prod if part[k] is None else part[k] + prod
                out_ref[r0:r0 + rc, :] = dup.astype(BF16)
            for k in range(3):
                dw_ref[k:k + 1, :] += jnp.sum(part[k], axis=0, keepdims=True)

    def blk(off, nxt):
        if nxt:
            return pl.BlockSpec((None, hb, c), lambda j, i: (j + off, _next_blk(i, ts, hb, s), 0))
        return pl.BlockSpec((None, ts, c), lambda j, i: (j + off, i, 0))

    def taps(off):
        return pl.BlockSpec((None, 8, c), lambda j, i: (j + off, 0, 0))

    tile = blk(0, False)
    acc = pl.BlockSpec((None, 8, c), lambda j, i: (j, 0, 0))
    return pl.pallas_call(
        body, name=name, grid=(N_PAIR, n_i),
        in_specs=[tile, blk(N_PAIR, False), tile, blk(0, True), tile, blk(0, True), tile, blk(0, True),
                  taps(0), taps(N_PAIR)],
        out_specs=[tile, tile, acc, acc],
        out_shape=[jax.ShapeDtypeStruct((N_PAIR, s, c), BF16), jax.ShapeDtypeStruct((N_PAIR, s, c), BF16),
                   jax.ShapeDtypeStruct((N_PAIR, 8, c), F32), jax.ShapeDtypeStruct((N_PAIR, 8, c), F32)],
        scratch_shapes=[pltpu.VMEM((ts + 8, c), F32), pltpu.VMEM((ts + 8, c), F32)],
        compiler_params=_params(("parallel", "arbitrary")),
    )(up8, up8, ug4, ug4, uv4, uv4, dact4, dact4, wc8, wc8)


def _ffn_dh(dupg, dupv, w8, x, g, dres, dep, *, name, tm=512):
    npair, s, c = dupg.shape
    d = x.shape[1]
    tm = min(tm, s)

    def body(dg_ref, dv_ref, w_ref, x_ref, g_ref, dres_ref, dep_ref, dx_ref, dxb_ref, dgain_ref):
        @pl.when(pl.program_id(0) == 0)
        def _():
            dgain_ref[...] = jnp.zeros_like(dgain_ref)

        dh = None
        for j in range(npair):
            for src, k in ((dg_ref, j), (dv_ref, j + npair)):
                part = jnp.dot(src[j], w_ref[k], preferred_element_type=F32)
                dh = part if dh is None else dh + part
        dx, dgr = _rms_bwd_rows(x_ref[...], g_ref[...], dh)
        dx = dres_ref[...] + dx
        dx_ref[...] = dx
        dxb_ref[...] = dx.astype(BF16)
        dgain_ref[...] += jnp.sum(dgr, axis=0, keepdims=True)

    row = pl.BlockSpec((tm, d), lambda i: (i, 0))
    vec = pl.BlockSpec((1, d), lambda i: (0, 0))
    dup = pl.BlockSpec((npair, tm, c), lambda i: (0, i, 0))
    return pl.pallas_call(
        body, name=name, grid=(s // tm,),
        in_specs=[dup, dup, pl.BlockSpec((2 * npair, c, d), lambda i: (0, 0, 0)), row, vec, row,
                  pl.BlockSpec(memory_space=pl.ANY)],
        out_specs=[row, row, vec],
        out_shape=[jax.ShapeDtypeStruct((s, d), F32), jax.ShapeDtypeStruct((s, d), BF16),
                   jax.ShapeDtypeStruct((1, d), F32)],
        compiler_params=_params(("arbitrary",)),
    )(dupg, dupv, w8, x, g, dres, dep)


def _ffn_dwup(h, dupg, dupv, *, name, tm=512):
    npair, s, c = dupg.shape
    d = h.shape[1]

    def body(h_ref, dg_ref, dv_ref, o_ref):
        k = pl.program_id(1)

        @pl.when(k < npair)
        def _():
            o_ref[...] = lax.dot_general(dg_ref[...], h_ref[...], _TN, preferred_element_type=F32).astype(BF16)

        @pl.when(k >= npair)
        def _():
            o_ref[...] = lax.dot_general(dv_ref[...], h_ref[...], _TN, preferred_element_type=F32).astype(BF16)

    return pl.pallas_call(
        body, name=name, grid=(d // tm, 2 * npair),
        in_specs=[pl.BlockSpec((s, tm), lambda m, k: (0, m)),
                  pl.BlockSpec((None, s, c), lambda m, k: (jnp.minimum(k, npair - 1), 0, 0)),
                  pl.BlockSpec((None, s, c), lambda m, k: (jnp.maximum(k - npair, 0), 0, 0))],
        out_specs=pl.BlockSpec((None, c, tm), lambda m, k: (k, 0, m)),
        out_shape=jax.ShapeDtypeStruct((2 * npair, c, d), BF16),
        compiler_params=_params(("parallel", "arbitrary")),
    )(h, dupg, dupv)


def _pool_counts(i, ts, rows, window):
    t = lax.broadcasted_iota(jnp.int32, (rows, 1), 0) + i * ts + 1
    return jnp.minimum(t, window).astype(F32)


def _pool_fwd(x, g, w_pool, scale, *, name, ts=256):
    s, d = x.shape
    hb = HALO_A
    pg = POOL_GROUP
    ts = min(ts, s)

    def body(x_ref, xp_ref, g_ref, w_ref, s_ref, o_ref, p_ref, yu_ref, ext, sa, sb):
        i = pl.program_id(0)
        keep = jnp.where(i > 0, 1.0, 0.0)
        gv = g_ref[...]
        ext[0:hb, :] = _rms_rows(xp_ref[...], gv) * keep
        ext[hb:hb + ts, :] = _rms_rows(x_ref[...], gv)
        rows = hb + ts - 8
        sa[0:8, :] = jnp.zeros((8, d), F32)
        sb[0:8, :] = jnp.zeros((8, d), F32)
        for gi, w in enumerate(POOL_WINDOWS):
            cols = slice(gi * pg, (gi + 1) * pg)
            cur, nxt, k = ext, sa, 1
            while k < w:
                nxt[8:8 + rows, cols] = cur[8:8 + rows, cols] + cur[pl.ds(8 - k, rows), cols]
                cur, nxt, k = nxt, (sb if nxt is sa else sa), 2 * k
            h = ext[hb:hb + ts, cols]
            pv = (cur[hb:hb + ts, cols] / _pool_counts(i, ts, ts, w) - h).astype(BF16)
            p_ref[:, cols] = pv
            yu = jnp.dot(pv, w_ref[gi], preferred_element_type=F32)
            yu_ref[:, cols] = yu.astype(BF16)
            o_ref[:, cols] = x_ref[:, cols] + yu * s_ref[:, cols]

    row = pl.BlockSpec((ts, d), lambda i: (i, 0))
    vec = pl.BlockSpec((1, d), lambda i: (0, 0))
    return pl.pallas_call(
        body, name=name, grid=(s // ts,),
        in_specs=[row, pl.BlockSpec((hb, d), lambda i: (_prev_blk(i, ts, hb), 0)), vec,
                  pl.BlockSpec((d // pg, pg, pg), lambda i: (0, 0, 0)), vec],
        out_specs=[row, row, row],
        out_shape=[jax.ShapeDtypeStruct((s, d), F32), jax.ShapeDtypeStruct((s, d), BF16),
                   jax.ShapeDtypeStruct((s, d), BF16)],
        scratch_shapes=[pltpu.VMEM((hb + ts, d), F32)] * 3,
        compiler_params=_params(("parallel",)),
    )(x, x, g, w_pool, scale)


def _pool_dw(p, dyc, *, name):
    s, d = p.shape
    pg = POOL_GROUP

    def body(p_ref, d_ref, o_ref):
        o_ref[...] = lax.dot_general(p_ref[...], d_ref[...], (((0,), (0,)), ((), ())),
                                     preferred_element_type=F32).astype(BF16)

    blk = pl.BlockSpec((s, pg), lambda gi: (0, gi))
    return pl.pallas_call(
        body, name=name, grid=(d // pg,),
        in_specs=[blk, blk], out_specs=pl.BlockSpec((None, pg, pg), lambda gi: (gi, 0, 0)),
        out_shape=jax.ShapeDtypeStruct((d // pg, pg, pg), BF16),
        compiler_params=_params(("parallel",)),
    )(p, dyc)


def _pool_bwd(dres, w_pool, scale, yu, x, g, *, name, ts=256):
    s, d = x.shape
    hb = HALO_A
    pg = POOL_GROUP
    ts = min(ts, s)
    n_i = s // ts

    def body(dres_ref, dresn_ref, w_ref, s_ref, yu_ref, x_ref, g_ref,
             dx_ref, dxb_ref, dg_ref, dyc_ref, ds_ref, ext, dh, sa, sb, dp_s):
        i = pl.program_id(0)
        keep_n = jnp.where(i < n_i - 1, 1.0, 0.0)

        @pl.when(i == 0)
        def _():
            dg_ref[...] = jnp.zeros_like(dg_ref)
            ds_ref[...] = jnp.zeros_like(ds_ref)

        ds_ref[...] += jnp.sum(dres_ref[...] * yu_ref[...].astype(F32), axis=0, keepdims=True)
        for gi, w in enumerate(POOL_WINDOWS):
            cols = slice(gi * pg, (gi + 1) * pg)
            dyc = (dres_ref[:, cols] * s_ref[:, cols]).astype(BF16)
            dyc_ref[:, cols] = dyc
            dp = lax.dot_general(dyc, w_ref[gi], _NT, preferred_element_type=F32)
            dp_s[:, cols] = dp
            ext[0:ts, cols] = dp / _pool_counts(i, ts, ts, w)
            dycn = (dresn_ref[:, cols] * s_ref[:, cols]).astype(BF16)
            dpn = lax.dot_general(dycn, w_ref[gi], _NT, preferred_element_type=F32)
            ext[ts:ts + hb, cols] = dpn / _pool_counts(i + 1, ts, hb, w) * keep_n
        rows = ts + hb - 8
        sa[rows:rows + 8, :] = jnp.zeros((8, d), F32)
        sb[rows:rows + 8, :] = jnp.zeros((8, d), F32)
        for gi, w in enumerate(POOL_WINDOWS):
            cols = slice(gi * pg, (gi + 1) * pg)
            cur, nxt, k = ext, sa, 1
            while k < w:
                nxt[0:rows, cols] = cur[0:rows, cols] + cur[pl.ds(k, rows), cols]
                cur, nxt, k = nxt, (sb if nxt is sa else sa), 2 * k
            dh[:, cols] = cur[0:ts, cols] - dp_s[:, cols]
        dx, dgr = _rms_bwd_rows(x_ref[...], g_ref[...], dh[...])
        dx = dres_ref[...] + dx
        dx_ref[...] = dx
        dxb_ref[...] = dx.astype(BF16)
        dg_ref[...] += jnp.sum(dgr, axis=0, keepdims=True)

    row = pl.BlockSpec((ts, d), lambda i: (i, 0))
    vec = pl.BlockSpec((1, d), lambda i: (0, 0))
    return pl.pallas_call(
        body, name=name, grid=(n_i,),
        in_specs=[row, pl.BlockSpec((hb, d), lambda i: (_next_blk(i, ts, hb, s), 0)),
                  pl.BlockSpec((d // pg, pg, pg), lambda i: (0, 0, 0)), vec, row, row, vec],
        out_specs=[row, row, vec, row, vec],
        out_shape=[jax.ShapeDtypeStruct((s, d), F32), jax.ShapeDtypeStruct((s, d), BF16),
                   jax.ShapeDtypeStruct((1, d), F32), jax.ShapeDtypeStruct((s, d), BF16),
                   jax.ShapeDtypeStruct((1, d), F32)],
        scratch_shapes=[pltpu.VMEM((ts + hb, d), F32), pltpu.VMEM((ts, d), F32),
                        pltpu.VMEM((ts + hb, d), F32), pltpu.VMEM((ts + hb, d), F32),
                        pltpu.VMEM((ts, d), F32)],
        compiler_params=_params(("arbitrary",)),
    )(dres, dres, w_pool, scale, yu, x, g)


def _pad_rows(w, rows):
    pad = [(0, 0)] * (w.ndim - 2) + [(0, rows - w.shape[-2]), (0, 0)]
    return jnp.pad(w, pad)


def _ffn_layer_fwd(x, nf, w8, wc8, wd4, tag, head=None):
    h, up8 = _ffn_up(x, nf, w8, name=f"ffn{tag}_up")
    res = _ffn_mid_down(up8, wc8, wd4, x, head, name=f"ffn{tag}_mid_down")
    act4, ug4, uv4 = res[-3:]
    out = res[0] if head is None else tuple(res[:4])
    return out, (h, up8, ug4, uv4, act4)


def _ffn_layer_bwd(d, db, x, nf, w8, wc8, wd4, saved, tag, grads_ready):
    h, up8, ug4, uv4, act4 = saved
    dact4 = _ffn_dact(db, wd4, name=f"ffn{tag}_dact")
    dwd4 = _ffn_dwdown(act4, db, name=f"ffn{tag}_dwdown")
    dupg, dupv, dwg, dwv = _ffn_midbwd(up8, ug4, uv4, dact4, wc8, name=f"ffn{tag}_midbwd")
    dw8 = _ffn_dwup(h, dupg, dupv, name=f"ffn{tag}_dwup")
    sent = grads_ready(f"ffn{tag}", {f"w_up{tag}": dw8, f"w_down{tag}": dwd4})
    dx, dxb, dnf = _ffn_dh(dupg, dupv, w8, x, nf, d, sent, name=f"ffn{tag}_dh")
    dwc8 = jnp.concatenate([dwg, dwv], axis=0)[:, :3]
    return dx, dxb, dnf, dwc8


def _local_step(x, target, weights, grads_ready):
    w0 = weights("mix0", x)
    wa = _pad_rows(w0["conv_a"], 32)
    wb = _pad_rows(w0["conv_b"], 8)
    wc = [_pad_rows(w0["conv_ffn"][l], 8) for l in range(2)]
    h0, z = _rms_mm(x, w0["norm_mix_even"], w0["w_in_t"], name="mix0_in", out_dtype=F32, tn=2560)
    ab, ca, x1 = _mix0_fwd(z, wa, w0["ln_a_g"], w0["ln_a_b"], wb, w0["w_out"], x, name="mix0_mid")
    w1 = weights("ffn0", x1)
    x2, ffn0 = _ffn_layer_fwd(x1, w0["norm_ffn"][0:1], w1["w_up"], wc[0], w1["w_down"], 0)
    w2 = weights("ffn1", x2)
    x3, p, yu = _pool_fwd(x2, w0["norm_mix_odd"], w2["w_pool"], w0["pool_scale"], name="pool_fwd")
    (loss, d4, d4b, g_norm_final), ffn1 = _ffn_layer_fwd(
        x3, w0["norm_ffn"][1:2], w2["w_up"], wc[1], w2["w_down"], 1, head=(w0["norm_final"], target))

    d3, d3b, g_nf1, g_wc1 = _ffn_layer_bwd(
        d4, d4b, x3, w0["norm_ffn"][1:2], w2["w_up"], wc[1], w2["w_down"], ffn1, 1, grads_ready)
    d2, d2b, g_nmo, dyc, g_scale = _pool_bwd(d3, w2["w_pool"], w0["pool_scale"], yu, x2, w0["norm_mix_odd"],
                                             name="pool_bwd")
    g_pool = _pool_dw(p, dyc, name="pool_dw")
    d1, d1b, g_nf0, g_wc0 = _ffn_layer_bwd(
        d2, d2b, x1, w0["norm_ffn"][0:1], w1["w_up"], wc[0], w1["w_down"], ffn0, 0, grads_ready)
    dab = _mm(d1b, w0["w_out"], tb=True, name="mix0_dab", tm=1024, tn=1024)
    g_out = _mm(ab, d1b, ta=True, out_dtype=BF16, name="mix0_dwout", tm=1024, tn=512)
    dz, g_wa, g_wb, g_lg, g_lb = _mix0_bwd(z, ca, dab, wa, w0["ln_a_g"], w0["ln_a_b"], wb, name="mix0_midbwd")
    g_in = _mm(dz, h0, ta=True, out_dtype=BF16, name="mix0_dwin", tm=512, tn=1024)
    sent_mix = grads_ready("mix0", {"w_in": g_in, "w_out": g_out, "w_pool": g_pool})
    dx, g_nme = _mm_rms_bwd(dz, w0["w_in_t"], x, w0["norm_mix_even"], d1, sent_mix, name="mix0_dh")

    small = {
        "norm_mix_even": g_nme, "conv_a": g_wa[:A_TAPS], "ln_a_g": g_lg, "ln_a_b": g_lb, "conv_b": g_wb[:3],
        "norm_mix_odd": g_nmo, "pool_scale": g_scale, "norm_ffn": jnp.concatenate([g_nf0, g_nf1], axis=0),
        "conv_ffn": [g_wc0, g_wc1], "norm_final": g_norm_final,
    }
    return loss[0, 0], dx, small


def _my_pos():
    return lax.axis_index("x"), lax.axis_index("y"), lax.axis_index("c")


def _flip(pos, r):
    x, y, c = pos
    return (1 - x if r & 4 else x, 1 - y if r & 2 else y, 1 - c if r & 1 else c)


def _dev_index(pos):
    return 4 * pos[0] + 2 * pos[1] + pos[2]


_HBM = pl.BlockSpec(memory_space=pltpu.HBM)
_SEM = pl.BlockSpec(memory_space=pltpu.SEMAPHORE)
_EFFECT = pltpu.SideEffectType.DATAFLOW_SIDE_EFFECTING


def _exchange_copy(ins, lands, send_sems, recv_sems, scatter, pos, a, r, receiving):
    me = _dev_index(pos)
    peer = _flip(pos, r)
    dest = _dev_index(pos) if receiving else _dev_index(peer)
    src = ins[a].at[dest] if scatter[a] else ins[a]
    slot = _dev_index(peer) if receiving else me
    return pltpu.make_async_remote_copy(
        src_ref=src, dst_ref=lands[a].at[slot], send_sem=send_sems.at[a * (N_DEV - 1) + r - 1],
        recv_sem=recv_sems.at[a * (N_DEV - 1) + r - 1],
        device_id=peer, device_id_type=pl.DeviceIdType.MESH)


ALL_PEERS = tuple(range(1, N_DEV))
CHIP_PEERS = (1, 2, 4, 6)
FORWARDED = (2, 4, 6)


def _exchange_start(arrays, scatter, after, *, name, peers=ALL_PEERS):
    n = len(arrays)
    me = _dev_index(_my_pos())
    lands = []
    for arr, sc in zip(arrays, scatter):
        own = lax.dynamic_index_in_dim(arr, me, 0, keepdims=True) if sc else arr[None]
        shape = arr.shape if sc else (N_DEV,) + arr.shape
        lands.append(lax.dynamic_update_slice(lax.empty(shape, arr.dtype), own, (me,) + (0,) * (len(shape) - 1)))

    def body(*refs):
        ins, lnd = refs[:n], refs[n:2 * n]
        send_sems, recv_sems = refs[2 * n + 1], refs[2 * n + 2]
        token = refs[-1]
        pos = _my_pos()
        for a in range(n):
            for r in peers:
                _exchange_copy(ins, lnd, send_sems, recv_sems, scatter, pos, a, r, receiving=False).start()
        token[...] = jnp.zeros_like(token)

    bufs = [pltpu.with_memory_space_constraint(t, pltpu.HBM) for t in list(arrays) + lands]
    sems = pltpu.SemaphoreType.DMA((n * (N_DEV - 1),))
    res = pl.pallas_call(
        body, name=name,
        out_shape=(sems, sems, *[pltpu.HBM(t.shape, t.dtype) for t in bufs], jax.ShapeDtypeStruct((8, 128), F32)),
        in_specs=[_HBM] * (2 * n) + [pl.BlockSpec(memory_space=pl.ANY)],
        out_specs=(_SEM, _SEM, *[_HBM] * (2 * n), pl.BlockSpec(memory_space=pltpu.VMEM)),
        input_output_aliases={i: 2 + i for i in range(2 * n)},
        compiler_params=pltpu.CompilerParams(has_side_effects=_EFFECT),
    )(*bufs, after)
    return res[0], res[1], list(res[2:2 + n]), list(res[2 + n:2 + 2 * n]), res[-1]


def _exchange_wait(started, scatter, after, *, name, peers=ALL_PEERS):
    send_sems, recv_sems, arrays, lands, _ = started
    n = len(arrays)

    def body(*refs):
        ins, lnd = refs[:n], refs[n:2 * n]
        send_sems, recv_sems = refs[2 * n], refs[2 * n + 1]
        pos = _my_pos()
        for a in range(n):
            for r in peers:
                _exchange_copy(ins, lnd, send_sems, recv_sems, scatter, pos, a, r, receiving=False).wait_send()
                _exchange_copy(ins, lnd, send_sems, recv_sems, scatter, pos, a, r, receiving=True).wait_recv()

    bufs = list(arrays) + list(lands)
    after = list(after) if isinstance(after, (list, tuple)) else [after]
    res = pl.pallas_call(
        body, name=name,
        out_shape=tuple(pltpu.HBM(t.shape, t.dtype) for t in bufs),
        in_specs=[_HBM] * (2 * n) + [_SEM, _SEM] + [pl.BlockSpec(memory_space=pl.ANY)] * len(after),
        out_specs=tuple([_HBM] * (2 * n)),
        input_output_aliases={i: i for i in range(2 * n)},
        compiler_params=pltpu.CompilerParams(has_side_effects=_EFFECT),
    )(*bufs, send_sems, recv_sems, *after)
    return list(res[n:])


def _forward_copy(lands, send_sems, recv_sems, pos, a, q, receiving):
    slot = _dev_index(_flip(pos, q ^ 1 if receiving else q))
    idx = a * len(FORWARDED) + FORWARDED.index(q)
    return pltpu.make_async_remote_copy(
        src_ref=lands[a].at[slot], dst_ref=lands[a].at[slot], send_sem=send_sems.at[idx], recv_sem=recv_sems.at[idx],
        device_id=_flip(pos, 1), device_id_type=pl.DeviceIdType.MESH)


def _forward_start(lands, after, *, name):
    n = len(lands)

    def body(*refs):
        lnd = refs[:n]
        send_sems, recv_sems = refs[n + 1], refs[n + 2]
        token = refs[-1]
        pos = _my_pos()
        for a in range(n):
            for q in FORWARDED:
                _forward_copy(lnd, send_sems, recv_sems, pos, a, q, receiving=False).start()
        token[...] = jnp.zeros_like(token)

    sems = pltpu.SemaphoreType.DMA((n * len(FORWARDED),))
    res = pl.pallas_call(
        body, name=name,
        out_shape=(sems, sems, *[pltpu.HBM(t.shape, t.dtype) for t in lands], jax.ShapeDtypeStruct((8, 128), F32)),
        in_specs=[_HBM] * n + [pl.BlockSpec(memory_space=pl.ANY)],
        out_specs=(_SEM, _SEM, *[_HBM] * n, pl.BlockSpec(memory_space=pltpu.VMEM)),
        input_output_aliases={i: 2 + i for i in range(n)},
        compiler_params=pltpu.CompilerParams(has_side_effects=_EFFECT),
    )(*lands, after)
    return res[0], res[1], list(res[2:2 + n]), res[-1]


def _forward_wait(forwarded, after, *, name):
    send_sems, recv_sems, lands, _ = forwarded
    n = len(lands)

    def body(*refs):
        lnd = refs[:n]
        send_sems, recv_sems = refs[n], refs[n + 1]
        pos = _my_pos()
        for a in range(n):
            for q in FORWARDED:
                _forward_copy(lnd, send_sems, recv_sems, pos, a, q, receiving=False).wait_send()
                _forward_copy(lnd, send_sems, recv_sems, pos, a, q, receiving=True).wait_recv()

    res = pl.pallas_call(
        body, name=name,
        out_shape=tuple(pltpu.HBM(t.shape, t.dtype) for t in lands),
        in_specs=[_HBM] * n + [_SEM, _SEM, pl.BlockSpec(memory_space=pl.ANY)],
        out_specs=tuple([_HBM] * n),
        input_output_aliases={i: i for i in range(n)},
        compiler_params=pltpu.CompilerParams(has_side_effects=_EFFECT),
    )(*lands, send_sems, recv_sems, after)
    return list(res)


def _adamw_update(p_ref, w_ref, m_ref, v_ref, g_ref, d_ref, mo_ref, vo_ref):
    g = p_ref[0].astype(F32)
    for k in range(1, N_DEV):
        g = g + p_ref[k].astype(F32)
    mn = ADAM_B1 * m_ref[...] + (1.0 - ADAM_B1) * g
    vn = ADAM_B2 * v_ref[...] + (1.0 - ADAM_B2) * (g * g)
    m_hat = mn / (1.0 - ADAM_B1 ** ADAM_STEP)
    v_hat = vn / (1.0 - ADAM_B2 ** ADAM_STEP)
    g_ref[...] = g
    d_ref[...] = -ADAM_LR * (m_hat / (jnp.sqrt(v_hat) + ADAM_EPS) + ADAM_WD * w_ref[...])
    mo_ref[...] = mn
    vo_ref[...] = vn


def _adamw_small(parts, ws, ms, vs, *, name):
    n = len(ws)

    def body(*refs):
        ins, outs = refs[:4 * n], refs[4 * n:]
        for j in range(n):
            _adamw_update(ins[j], ins[n + j], ins[2 * n + j], ins[3 * n + j], *outs[4 * j:4 * j + 4])

    return pl.pallas_call(
        body, name=name,
        out_shape=[jax.ShapeDtypeStruct(w.shape, F32) for w in ws for _ in range(4)],
    )(*parts, *ws, *ms, *vs)


def _adamw(parts, w, m, v, *, name, tr):
    nl, r, c = w.shape
    assert len(parts) == nl and r % tr == 0
    n_i = r // tr

    def body(*refs):
        p_refs = refs[:nl]
        w_ref, m_ref, v_ref = refs[nl:nl + 3]

        def update(p_ref):
            _adamw_update(p_ref, w_ref, m_ref, v_ref, *refs[nl + 3:])

        if nl == 1:
            update(p_refs[0])
        else:
            for layer in range(nl):
                pl.when(pl.program_id(0) == layer)(lambda p_ref=p_refs[layer]: update(p_ref))

    def parts_spec(layer):
        def index(l, i):
            return (0, jnp.where(l < layer, 0, jnp.where(l > layer, n_i - 1, i)), 0)
        return pl.BlockSpec((N_DEV, tr, c), index)

    blk = pl.BlockSpec((None, tr, c), lambda l, i: (l, i, 0))
    return pl.pallas_call(
        body, name=name, grid=(nl, n_i),
        in_specs=[parts_spec(layer) for layer in range(nl)] + [blk, blk, blk],
        out_specs=[blk] * 4, out_shape=[jax.ShapeDtypeStruct((nl, r, c), F32)] * 4,
        compiler_params=_params(("arbitrary", "arbitrary")),
    )(*parts, w, m, v)


def _pack(parts, lead=()):
    flat = jnp.concatenate([p.reshape(lead + (-1,)) for p in parts], axis=-1)
    n = flat.shape[-1]
    rows = -(-n // (8 * 128)) * 8
    flat = jnp.pad(flat, [(0, 0)] * len(lead) + [(0, rows * 128 - n)])
    return flat.reshape(lead + (rows, 128))


def _to_dev_major(g, axis):
    shp = g.shape
    g = g.reshape(shp[:axis] + (N_DEV, shp[axis] // N_DEV) + shp[axis + 1:])
    return jnp.moveaxis(g, axis, 0)


def _from_dev_major(g, axis):
    g = jnp.moveaxis(g, 0, axis)
    shp = g.shape
    return g.reshape(shp[:axis] + (shp[axis] * shp[axis + 1],) + shp[axis + 2:])


SMALL_SHARDED = ("conv_a", "conv_b", "norm_mix_odd", "pool_scale", "conv_ffn_w")
SMALL_REPLICATED = ("norm_mix_even", "ln_a_g", "ln_a_b", "norm_ffn", "norm_final")
BIG = {"w_in": ("w_in", 0, 320), "w_out": ("w_out", 0, 128), "w_pool": ("w_pool", 0, 128),
       "w_up0": ("w_up", 0, 352), "w_up1": ("w_up", 1, 352),
       "w_down0": ("w_down", 0, 352), "w_down1": ("w_down", 1, 352)}
COLUMN_SHARDED = ("w_in", "w_up")


def kernel(x, norm_mix_even, w_in, conv_a, ln_a_g, ln_a_b, conv_b, w_out, norm_mix_odd, w_pool, pool_scale, norm_ffn, w_up, conv_ffn_w, w_down, norm_final, loss_target, m_norm_mix_even, m_w_in, m_conv_a, m_ln_a_g, m_ln_a_b, m_conv_b, m_w_out, m_norm_mix_odd, m_w_pool, m_pool_scale, m_norm_ffn, m_w_up, m_conv_ffn_w, m_w_down, m_norm_final, v_norm_mix_even, v_w_in, v_conv_a, v_ln_a_g, v_ln_a_b, v_conv_b, v_w_out, v_norm_mix_odd, v_w_pool, v_pool_scale, v_norm_ffn, v_w_up, v_conv_ffn_w, v_w_down, v_norm_final):
    names = ("norm_mix_even", "w_in", "conv_a", "ln_a_g", "ln_a_b", "conv_b", "w_out", "norm_mix_odd", "w_pool",
             "pool_scale", "norm_ffn", "w_up", "conv_ffn_w", "w_down", "norm_final")
    wts = dict(zip(names, (norm_mix_even, w_in, conv_a, ln_a_g, ln_a_b, conv_b, w_out, norm_mix_odd, w_pool,
                           pool_scale, norm_ffn, w_up, conv_ffn_w, w_down, norm_final)))
    mom = dict(zip(names, (m_norm_mix_even, m_w_in, m_conv_a, m_ln_a_g, m_ln_a_b, m_conv_b, m_w_out, m_norm_mix_odd,
                           m_w_pool, m_pool_scale, m_norm_ffn, m_w_up, m_conv_ffn_w, m_w_down, m_norm_final)))
    var = dict(zip(names, (v_norm_mix_even, v_w_in, v_conv_a, v_ln_a_g, v_ln_a_b, v_conv_b, v_w_out, v_norm_mix_odd,
                           v_w_pool, v_pool_scale, v_norm_ffn, v_w_up, v_conv_ffn_w, v_w_down, v_norm_final)))
    d = x.shape[-1]

    def shard3d(t, param):
        a = t[param]
        if param in COLUMN_SHARDED:
            return jnp.swapaxes(a, 1, 2)
        return a.reshape(a.shape[0], -1, a.shape[-1])

    def unshard3d(a, param):
        if param in COLUMN_SHARDED:
            return jnp.swapaxes(a, 1, 2)
        return a.reshape(wts[param].shape)

    def shard2d(t, key):
        param, layer, _ = BIG[key]
        return shard3d(t, param)[layer]

    small_w = _pack([wts[k] for k in SMALL_SHARDED])
    bf = {k: shard2d(wts, k).astype(BF16) for k in BIG}
    gather_groups = {"mix0": ("w_in", "w_out", "small"), "ffn0": ("w_up0", "w_down0"),
                     "ffn1": ("w_pool", "w_up1", "w_down1")}
    order = list(gather_groups)
    started = {}

    def start_gather(grp, after):
        arrs = [small_w if k == "small" else bf[k] for k in gather_groups[grp]]
        started[grp] = _exchange_start(arrs, [False] * len(arrs), after, name=f"gather_{grp}_start", peers=CHIP_PEERS)

    start_gather(order[0], small_w)

    def weights(grp, after):
        keys = gather_groups[grp]
        lands = _exchange_wait(started[grp], [False] * len(keys), after, name=f"gather_{grp}_wait", peers=CHIP_PEERS)
        forwarded = _forward_start(lands, small_w, name=f"gather_{grp}_forward")
        if grp != order[-1]:
            start_gather(order[order.index(grp) + 1], forwarded[-1])
            behind = started[order[order.index(grp) + 1]][-1]
        else:
            behind = forwarded[-1]
        gw = dict(zip(keys, _forward_wait(forwarded, behind, name=f"gather_{grp}_forward_wait")))
        if grp == "ffn0":
            return {"w_up": gw["w_up0"], "w_down": gw["w_down0"].reshape(N_PAIR, -1, d)}
        if grp == "ffn1":
            return {"w_up": gw["w_up1"], "w_down": gw["w_down1"].reshape(N_PAIR, -1, d),
                    "w_pool": _from_dev_major(gw["w_pool"].reshape(N_DEV, len(POOL_WINDOWS), -1, POOL_GROUP), 1)}
        per_dev = gw["small"].reshape(N_DEV, -1)
        sizes = [wts[k].size for k in SMALL_SHARDED]
        offs = [sum(sizes[:i]) for i in range(len(sizes))]
        small_full = {k: per_dev[:, o:o + n_].reshape((N_DEV,) + wts[k].shape)
                      for k, o, n_ in zip(SMALL_SHARDED, offs, sizes)}
        return {
            "norm_mix_even": norm_mix_even, "ln_a_g": ln_a_g, "ln_a_b": ln_a_b, "norm_ffn": norm_ffn,
            "norm_final": norm_final[None],
            "w_in_t": gw["w_in"].reshape(-1, d),
            "w_out": gw["w_out"].reshape(-1, d),
            "conv_a": _from_dev_major(small_full["conv_a"][:, 0], 1),
            "conv_b": _from_dev_major(small_full["conv_b"][:, 0], 1),
            "norm_mix_odd": _from_dev_major(small_full["norm_mix_odd"], 1),
            "pool_scale": _from_dev_major(small_full["pool_scale"], 1),
            "conv_ffn": [small_full["conv_ffn_w"][:, l] for l in range(2)],
        }

    def dev_major(k, g):
        if k == "w_pool":
            return _to_dev_major(g, 1).reshape(N_DEV, -1, POOL_GROUP)
        return g.reshape(N_DEV, -1, g.shape[-1])

    sent = {}

    def grads_ready(grp, grads):
        keys = tuple(grads)
        parts = [dev_major(k, grads[k]) for k in keys]
        sent[grp] = (keys, _exchange_start(parts, [True] * len(keys), small_w, name=f"grads_{grp}_start"))
        return sent[grp][1][-1]

    loss, dx, g = _local_step(x[0], loss_target[0], weights, grads_ready)

    def small2d(a):
        return a.reshape(-1, a.shape[-1])

    small_keys = SMALL_SHARDED + SMALL_REPLICATED
    small_parts = [_to_dev_major(g["conv_a"], 1), _to_dev_major(g["conv_b"], 1),
                   _to_dev_major(g["norm_mix_odd"], 1), _to_dev_major(g["pool_scale"], 1),
                   jnp.stack(g["conv_ffn"], axis=1).reshape(N_DEV, -1, w_up.shape[-1])]
    small_parts += [g[k] for k in SMALL_REPLICATED]
    small_scatter = [True] * len(SMALL_SHARDED) + [False] * len(SMALL_REPLICATED)
    small_parts.append(jnp.full((1, 128), loss, F32))
    small_scatter.append(False)
    sent["small"] = (small_keys + ("loss",),
                     _exchange_start(small_parts, small_scatter, dx, name="grads_small_start"))

    landed, out, raw = {}, {}, {}

    def wait_grads(grp, after, scat=None):
        keys, st = sent[grp]
        scat = [True] * len(keys) if scat is None else scat
        landed.update(zip(keys, _exchange_wait(st, scat, after, name=f"grads_{grp}_wait")))

    def update(param, keys):
        raw[param] = _adamw([landed[k] for k in keys], shard3d(wts, param), shard3d(mom, param),
                            shard3d(var, param), name=f"adamw_{param}", tr=BIG[keys[0]][2])
        out[param] = [unshard3d(t, param) for t in raw[param]]

    wait_grads("ffn1", sent["small"][1][-1])
    wait_grads("ffn0", landed["w_up1"])
    update("w_up", ("w_up0", "w_up1"))
    update("w_down", ("w_down0", "w_down1"))
    wait_grads("mix0", [raw["w_up"][1], raw["w_down"][1]])
    update("w_in", ("w_in",))
    update("w_out", ("w_out",))
    update("w_pool", ("w_pool",))
    wait_grads("small", raw["w_pool"][1], small_scatter)
    res = _adamw_small([landed[k] for k in small_keys], [small2d(wts[k]) for k in small_keys],
                       [small2d(mom[k]) for k in small_keys], [small2d(var[k]) for k in small_keys],
                       name="adamw_small")
    for j, k in enumerate(small_keys):
        out[k] = [t.reshape(wts[k].shape) for t in res[4 * j:4 * j + 4]]

    loss = jnp.sum(landed["loss"][:, 0, 0])
    return (loss, dx[None], *[out[k][0] for k in names], *[out[k][1] for k in names],
            *[out[k][2] for k in names], *[out[k][3] for k in names])
```

```python
import jax
import jax.numpy as jnp
from jax import lax
from jax.experimental import pallas as pl
from jax.experimental.pallas import tpu as pltpu

F32 = jnp.float32
BF16 = jnp.bfloat16

RMS_EPS = 1e-6
LN_EPS = 1e-5
ADAM_LR = 0.001
ADAM_B1 = 0.9
ADAM_B2 = 0.999
ADAM_EPS = 1e-08
ADAM_WD = 0.01
ADAM_STEP = 10

N_DEV = 8
N_PAIR = N_DEV // 2
A_WIDTH = 512
A_TAPS = 31
POOL_WINDOWS = (2, 4, 8, 16)
POOL_GROUP = 256
HALO_A = 32
HALO_S = 16
VMEM_LIMIT = 56 * 1024 * 1024


def _params(sem, vmem=VMEM_LIMIT):
    return pltpu.CompilerParams(dimension_semantics=sem, vmem_limit_bytes=vmem)


def _sigmoid(x):
    return 0.5 * jnp.tanh(0.5 * x) + 0.5


def _prev_blk(i, ts, hb):
    return jnp.maximum(i * (ts // hb) - 1, 0)


def _next_blk(i, ts, hb, s):
    return jnp.minimum((i + 1) * (ts // hb), s // hb - 1)


def _mm(a, b, *, name, ta=False, tb=False, out_dtype=F32, tm=512, tn=512):
    m, k = (a.shape[1], a.shape[0]) if ta else a.shape
    n = b.shape[0] if tb else b.shape[1]
    tm, tn = min(tm, m), min(tn, n)
    assert m % tm == 0 and n % tn == 0, (name, m, n, tm, tn)
    dims = (((0,) if ta else (1,), (1,) if tb else (0,)), ((), ()))

    def body(a_ref, b_ref, o_ref):
        o_ref[...] = lax.dot_general(a_ref[...], b_ref[...], dims, preferred_element_type=F32).astype(out_dtype)

    a_spec = pl.BlockSpec((k, tm), lambda i, j: (0, i)) if ta else pl.BlockSpec((tm, k), lambda i, j: (i, 0))
    b_spec = pl.BlockSpec((tn, k), lambda i, j: (j, 0)) if tb else pl.BlockSpec((k, tn), lambda i, j: (0, j))
    return pl.pallas_call(
        body, name=name, grid=(m // tm, n // tn),
        in_specs=[a_spec, b_spec], out_specs=pl.BlockSpec((tm, tn), lambda i, j: (i, j)),
        out_shape=jax.ShapeDtypeStruct((m, n), out_dtype),
        compiler_params=_params(("parallel", "parallel")),
    )(a, b)


def _rms_rows(xv, gv):
    return xv * lax.rsqrt(jnp.mean(xv * xv, axis=-1, keepdims=True) + RMS_EPS) * gv


_NT = (((1,), (1,)), ((), ()))
_TN = (((0,), (0,)), ((), ()))


def _rms_mm(x, g, wt, *, name, out_dtype, tm=1024, tn=512):
    s, d = x.shape
    n = wt.shape[0]
    tm = min(tm, s)
    assert s % tm == 0 and n % tn == 0

    def body(x_ref, g_ref, w_ref, h_ref, z_ref, hs_ref):
        @pl.when(pl.program_id(1) == 0)
        def _():
            hv = _rms_rows(x_ref[...], g_ref[...]).astype(BF16)
            hs_ref[...] = hv
            h_ref[...] = hv

        z_ref[...] = lax.dot_general(hs_ref[...], w_ref[...], _NT, preferred_element_type=F32).astype(out_dtype)

    return pl.pallas_call(
        body, name=name, grid=(s // tm, n // tn),
        in_specs=[pl.BlockSpec((tm, d), lambda i, j: (i, 0)),
                  pl.BlockSpec((1, d), lambda i, j: (0, 0)),
                  pl.BlockSpec((tn, d), lambda i, j: (j, 0))],
        out_specs=[pl.BlockSpec((tm, d), lambda i, j: (i, 0)),
                   pl.BlockSpec((tm, tn), lambda i, j: (i, j))],
        out_shape=[jax.ShapeDtypeStruct((s, d), BF16), jax.ShapeDtypeStruct((s, n), out_dtype)],
        scratch_shapes=[pltpu.VMEM((tm, d), BF16)],
        compiler_params=_params(("parallel", "arbitrary")),
    )(x, g, wt)


def _rms_bwd_rows(xv, gv, dh):
    r = lax.rsqrt(jnp.mean(xv * xv, axis=-1, keepdims=True) + RMS_EPS)
    xh = xv * r
    dn = dh * gv
    dx = r * (dn - xh * jnp.mean(dn * xh, axis=-1, keepdims=True))
    return dx, dh * xh


def _mm_rms_bwd(a, b, x, g, dres, dep, *, name, tm=512):
    s, k = a.shape
    d = b.shape[1]
    tm = min(tm, s)

    def body(a_ref, b_ref, x_ref, g_ref, dres_ref, dep_ref, dx_ref, dg_ref):
        @pl.when(pl.program_id(0) == 0)
        def _():
            dg_ref[...] = jnp.zeros_like(dg_ref)

        dh = jnp.dot(a_ref[...], b_ref[...], preferred_element_type=F32)
        dx, dgr = _rms_bwd_rows(x_ref[...], g_ref[...], dh)
        dx_ref[...] = dres_ref[...] + dx
        dg_ref[...] += jnp.sum(dgr, axis=0, keepdims=True)

    row = pl.BlockSpec((tm, d), lambda i: (i, 0))
    vec = pl.BlockSpec((1, d), lambda i: (0, 0))
    return pl.pallas_call(
        body, name=name, grid=(s // tm,),
        in_specs=[pl.BlockSpec((tm, k), lambda i: (i, 0)), pl.BlockSpec((k, d), lambda i: (0, 0)), row, vec, row,
                  pl.BlockSpec(memory_space=pl.ANY)],
        out_specs=[row, vec],
        out_shape=[jax.ShapeDtypeStruct((s, d), F32), jax.ShapeDtypeStruct((1, d), F32)],
        compiler_params=_params(("arbitrary",)),
    )(a, b, x, g, dres, dep)


def _conv_taps(ext_ref, w_ref, n_taps, base, r0, rows, reverse=False):
    acc = None
    for k in range(n_taps):
        off = r0 + (base - k if reverse else base + k)
        term = w_ref[k:k + 1, :] * ext_ref[pl.ds(off, rows), :]
        acc = term if acc is None else acc + term
    return acc


def _shift_copies(src_ref, sh_ref, rows):
    for b in range(1, 8):
        sh_ref[b, 0:rows, :] = src_ref[pl.ds(b, rows), :]


def _shifted(src_ref, sh_ref, start, rows, off):
    a, b = divmod(off, 8)
    ref = src_ref if b == 0 else sh_ref.at[b]
    return ref[pl.ds(start + 8 * a, rows), :]


def _mix0_fwd(z, conv_a, ln_g, ln_b, conv_b, w_out, x, *, name, ts=256, rc=32):
    s = z.shape[0]
    c = A_WIDTH
    hb = HALO_A
    half = ts // 2

    def body(z_ref, zp_ref, wa_ref, lg_ref, lb_ref, wb_ref, wo_ref, x_ref, ab_ref, ca_ref, o_ref, exta, extb, sha):
        keep = jnp.where(pl.program_id(0) > 0, 1.0, 0.0)
        zp = zp_ref[...]
        exta[0:hb, :] = zp[:, 0:c] * _sigmoid(zp[:, c:2 * c]) * keep
        extb[0:hb, :] = zp[:, 3 * c:4 * c] * zp[:, 4 * c:5 * c] * keep
        exta[hb:hb + ts, :] = z_ref[:, 0:c] * _sigmoid(z_ref[:, c:2 * c])
        extb[hb:hb + ts, :] = z_ref[:, 3 * c:4 * c] * z_ref[:, 4 * c:5 * c]
        _shift_copies(exta, sha, hb + ts - 8)
        lg = lg_ref[...]
        lb = lb_ref[...]
        for q in range(ts // rc):
            r0 = q * rc
            ca = None
            for k in range(A_TAPS):
                term = wa_ref[k:k + 1, :] * _shifted(exta, sha, r0, rc, hb - (A_TAPS - 1) + k)
                ca = term if ca is None else ca + term
            ca_ref[r0:r0 + rc, :] = ca
            mu = jnp.mean(ca, axis=-1, keepdims=True)
            xc = ca - mu
            rs = lax.rsqrt(jnp.mean(xc * xc, axis=-1, keepdims=True) + LN_EPS)
            l = xc * rs * lg + lb
            ab_ref[r0:r0 + rc, 0:c] = (l * _sigmoid(l)).astype(BF16)
            cbc = _conv_taps(extb, wb_ref, 3, hb - 2, r0, rc)
            ab_ref[r0:r0 + rc, c:2 * c] = (z_ref[r0:r0 + rc, 2 * c:3 * c] * cbc).astype(BF16)
            if (r0 + rc) % half == 0:
                rows = slice(r0 + rc - half, r0 + rc)
                o_ref[rows, :] = x_ref[rows, :] + jnp.dot(ab_ref[rows, :], wo_ref[...], preferred_element_type=F32)

    d = x.shape[1]
    row = pl.BlockSpec((ts, d), lambda i: (i, 0))
    return pl.pallas_call(
        body, name=name, grid=(s // ts,),
        in_specs=[pl.BlockSpec((ts, 5 * c), lambda i: (i, 0)),
                  pl.BlockSpec((hb, 5 * c), lambda i: (_prev_blk(i, ts, hb), 0)),
                  pl.BlockSpec((32, c), lambda i: (0, 0)),
                  pl.BlockSpec((1, c), lambda i: (0, 0)),
                  pl.BlockSpec((1, c), lambda i: (0, 0)),
                  pl.BlockSpec((8, c), lambda i: (0, 0)),
                  pl.BlockSpec((2 * c, d), lambda i: (0, 0)), row],
        out_specs=[pl.BlockSpec((ts, 2 * c), lambda i: (i, 0)),
                   pl.BlockSpec((ts, c), lambda i: (i, 0)), row],
        out_shape=[jax.ShapeDtypeStruct((s, 2 * c), BF16), jax.ShapeDtypeStruct((s, c), F32),
                   jax.ShapeDtypeStruct((s, d), F32)],
        scratch_shapes=[pltpu.VMEM((hb + ts, c), F32), pltpu.VMEM((hb + ts, c), F32),
                        pltpu.VMEM((8, hb + ts - 8, c), F32)],
        compiler_params=_params(("parallel",)),
    )(z, z, conv_a, ln_g, ln_b, conv_b, w_out, x)


def _mix0_bwd(z, ca, dab, conv_a, ln_g, ln_b, conv_b, *, name, ts=256, rc=32):
    s = z.shape[0]
    c = A_WIDTH
    hb = HALO_A
    ta = A_TAPS

    def body(z_ref, zp_ref, zn_ref, ca_ref, can_ref, d_ref, dn_ref, wa_ref, lg_ref, lb_ref, wb_ref,
             dz_ref, dwa_ref, dwb_ref, dlg_ref, dlb_ref, exta, extb, extdca, extdcb, shd):
        i = pl.program_id(0)
        keep_p = jnp.where(i > 0, 1.0, 0.0)
        keep_n = jnp.where(i < s // ts - 1, 1.0, 0.0)

        @pl.when(i == 0)
        def _():
            dwa_ref[...] = jnp.zeros_like(dwa_ref)
            dwb_ref[...] = jnp.zeros_like(dwb_ref)
            dlg_ref[...] = jnp.zeros_like(dlg_ref)
            dlb_ref[...] = jnp.zeros_like(dlb_ref)

        lg = lg_ref[...]
        lb = lb_ref[...]
        zp = zp_ref[...]
        exta[0:hb, :] = zp[:, 0:c] * _sigmoid(zp[:, c:2 * c]) * keep_p
        extb[0:hb, :] = zp[:, 3 * c:4 * c] * zp[:, 4 * c:5 * c] * keep_p
        exta[hb:hb + ts, :] = z_ref[:, 0:c] * _sigmoid(z_ref[:, c:2 * c])
        extb[hb:hb + ts, :] = z_ref[:, 3 * c:4 * c] * z_ref[:, 4 * c:5 * c]

        def ln_bwd(cav, dav):
            mu = jnp.mean(cav, axis=-1, keepdims=True)
            xc = cav - mu
            rs = lax.rsqrt(jnp.mean(xc * xc, axis=-1, keepdims=True) + LN_EPS)
            nv = xc * rs
            l = nv * lg + lb
            sg = _sigmoid(l)
            dl = dav * (sg * (1.0 + l * (1.0 - sg)))
            dnv = dl * lg
            dca = rs * (dnv - jnp.mean(dnv, axis=-1, keepdims=True)
                        - nv * jnp.mean(dnv * nv, axis=-1, keepdims=True))
            return dca, dl, nv

        dlg_acc = jnp.zeros((1, c), F32)
        dlb_acc = jnp.zeros((1, c), F32)
        for q in range(ts // rc):
            r0 = q * rc
            dca, dl, nv = ln_bwd(ca_ref[r0:r0 + rc, :], d_ref[r0:r0 + rc, 0:c])
            extdca[r0:r0 + rc, :] = dca
            dlg_acc = dlg_acc + jnp.sum(dl * nv, axis=0, keepdims=True)
            dlb_acc = dlb_acc + jnp.sum(dl, axis=0, keepdims=True)
            extdcb[r0:r0 + rc, :] = d_ref[r0:r0 + rc, c:2 * c] * z_ref[r0:r0 + rc, 2 * c:3 * c]
        dca_n, _, _ = ln_bwd(can_ref[...], dn_ref[:, 0:c])
        extdca[ts:ts + hb, :] = dca_n * keep_n
        extdcb[ts:ts + hb, :] = dn_ref[:, c:2 * c] * zn_ref[:, 2 * c:3 * c] * keep_n
        dlg_ref[...] += dlg_acc
        dlb_ref[...] += dlb_acc
        _shift_copies(extdca, shd, ts + hb - 8)

        for q in range(ts // rc):
            r0 = q * rc
            zr = z_ref[r0:r0 + rc, :]
            dga = None
            for k in range(ta):
                term = wa_ref[k:k + 1, :] * _shifted(extdca, shd, r0, rc, ta - 1 - k)
                dga = term if dga is None else dga + term
            sg = _sigmoid(zr[:, c:2 * c])
            dz_ref[r0:r0 + rc, 0:c] = (dga * sg).astype(BF16)
            dz_ref[r0:r0 + rc, c:2 * c] = (dga * zr[:, 0:c] * sg * (1.0 - sg)).astype(BF16)
            cbc = _conv_taps(extb, wb_ref, 3, hb - 2, r0, rc)
            dz_ref[r0:r0 + rc, 2 * c:3 * c] = (d_ref[r0:r0 + rc, c:2 * c] * cbc).astype(BF16)
            dcb = _conv_taps(extdcb, wb_ref, 3, 2, r0, rc, reverse=True)
            dz_ref[r0:r0 + rc, 3 * c:4 * c] = (dcb * zr[:, 4 * c:5 * c]).astype(BF16)
            dz_ref[r0:r0 + rc, 4 * c:5 * c] = (dcb * zr[:, 3 * c:4 * c]).astype(BF16)

        for k in range(ta):
            part = None
            for q in range(ts // rc):
                r0 = q * rc
                p = exta[hb + r0:hb + r0 + rc, :] * _shifted(extdca, shd, r0, rc, ta - 1 - k)
                for r in range(0, rc, 8):
                    part = p[r:r + 8, :] if part is None else part + p[r:r + 8, :]
            dwa_ref[k:k + 1, :] += jnp.sum(part, axis=0, keepdims=True)
        dcb_t = extdcb[0:ts, :]
        for k in range(3):
            dwb_ref[k:k + 1, :] += jnp.sum(dcb_t * extb[pl.ds(hb - 2 + k, ts), :], axis=0, keepdims=True)

    def tile(w):
        return pl.BlockSpec((ts, w), lambda i: (i, 0))

    def prev(w):
        return pl.BlockSpec((hb, w), lambda i: (_prev_blk(i, ts, hb), 0))

    def nxt(w):
        return pl.BlockSpec((hb, w), lambda i: (_next_blk(i, ts, hb, s), 0))

    def const(r, w):
        return pl.BlockSpec((r, w), lambda i: (0, 0))

    return pl.pallas_call(
        body, name=name, grid=(s // ts,),
        in_specs=[tile(5 * c), prev(5 * c), nxt(5 * c), tile(c), nxt(c), tile(2 * c), nxt(2 * c),
                  const(32, c), const(1, c), const(1, c), const(8, c)],
        out_specs=[tile(5 * c), const(32, c), const(8, c), const(1, c), const(1, c)],
        out_shape=[jax.ShapeDtypeStruct((s, 5 * c), BF16), jax.ShapeDtypeStruct((32, c), F32),
                   jax.ShapeDtypeStruct((8, c), F32), jax.ShapeDtypeStruct((1, c), F32),
                   jax.ShapeDtypeStruct((1, c), F32)],
        scratch_shapes=[pltpu.VMEM((hb + ts, c), F32), pltpu.VMEM((hb + ts, c), F32),
                        pltpu.VMEM((ts + hb, c), F32), pltpu.VMEM((ts + hb, c), F32),
                        pltpu.VMEM((8, ts + hb - 8, c), F32)],
        compiler_params=_params(("arbitrary",)),
    )(z, z, z, ca, ca, dab, dab, conv_a, ln_g, ln_b, conv_b)


def _ffn_up(x, g, w8, *, name, tm=2048):
    s, d = x.shape
    nb, c, _ = w8.shape
    tm = min(tm, s)

    def body(x_ref, g_ref, w_ref, h_ref, u_ref, hs_ref):
        @pl.when(pl.program_id(1) == 0)
        def _():
            hv = _rms_rows(x_ref[...], g_ref[...]).astype(BF16)
            hs_ref[...] = hv
            h_ref[...] = hv

        u_ref[...] = lax.dot_general(hs_ref[...], w_ref[...], _NT, preferred_element_type=F32).astype(BF16)

    return pl.pallas_call(
        body, name=name, grid=(s // tm, nb),
        in_specs=[pl.BlockSpec((tm, d), lambda i, k: (i, 0)),
                  pl.BlockSpec((1, d), lambda i, k: (0, 0)),
                  pl.BlockSpec((None, c, d), lambda i, k: (k, 0, 0))],
        out_specs=[pl.BlockSpec((tm, d), lambda i, k: (i, 0)),
                   pl.BlockSpec((None, tm, c), lambda i, k: (k, i, 0))],
        out_shape=[jax.ShapeDtypeStruct((s, d), BF16), jax.ShapeDtypeStruct((nb, s, c), BF16)],
        scratch_shapes=[pltpu.VMEM((tm, d), BF16)],
        compiler_params=_params(("parallel", "arbitrary")),
    )(x, g, w8)


def _ffn_mid_down(up8, wc8, wd4, x, head=None, *, name, ts=256, rc=32):
    nb, s, c = up8.shape
    d = x.shape[1]
    hb = HALO_S
    ts = min(ts, s)

    def body(*refs):
        u_ref, up_ref, wc_ref, wd_ref, x_ref = refs[:5]
        n_out = 4 if head is None else 7
        outs = refs[-1 - n_out:-1]
        act_ref, ug_ref, uv_ref = outs[-3:]
        ext = refs[-1]
        keep = jnp.where(pl.program_id(0) > 0, 1.0, 0.0)
        acc = x_ref[...]
        for j in range(N_PAIR):
            eg, ev = ext.at[2 * (j % 2)], ext.at[2 * (j % 2) + 1]
            eg[0:hb, :] = up_ref[j].astype(F32) * keep
            ev[0:hb, :] = up_ref[j + N_PAIR].astype(F32) * keep
            eg[hb:hb + ts, :] = u_ref[j].astype(F32)
            ev[hb:hb + ts, :] = u_ref[j + N_PAIR].astype(F32)
            for q in range(ts // rc):
                r0 = q * rc
                gg = _conv_taps(eg, wc_ref.at[j], 3, hb - 2, r0, rc)
                vv = _conv_taps(ev, wc_ref.at[j + N_PAIR], 3, hb - 2, r0, rc)
                ug_ref[j, r0:r0 + rc, :] = gg.astype(BF16)
                uv_ref[j, r0:r0 + rc, :] = vv.astype(BF16)
                act_ref[j, r0:r0 + rc, :] = (gg * _sigmoid(gg) * vv).astype(BF16)
            acc = acc + jnp.dot(act_ref[j], wd_ref[j], preferred_element_type=F32)
        if head is None:
            outs[0][...] = acc
            return
        g_ref, t_ref = refs[5], refs[6]
        loss_ref, dx_ref, dxb_ref, dg_ref = outs[:4]

        @pl.when(pl.program_id(0) == 0)
        def _():
            loss_ref[...] = jnp.zeros_like(loss_ref)
            dg_ref[...] = jnp.zeros_like(dg_ref)

        gv = g_ref[...]
        r = lax.rsqrt(jnp.mean(acc * acc, axis=-1, keepdims=True) + RMS_EPS)
        xh = acc * r
        err = xh * gv - t_ref[...]
        loss_ref[...] += 0.5 * jnp.sum(jnp.mean(err * err, axis=-1, keepdims=True), axis=0, keepdims=True)
        dy = err * (1.0 / d)
        dn = dy * gv
        dx = r * (dn - xh * jnp.mean(dn * xh, axis=-1, keepdims=True))
        dx_ref[...] = dx
        dxb_ref[...] = dx.astype(BF16)
        dg_ref[...] += jnp.sum(dy * xh, axis=0, keepdims=True)

    row = pl.BlockSpec((ts, d), lambda i: (i, 0))
    vec = pl.BlockSpec((1, d), lambda i: (0, 0))
    tile = pl.BlockSpec((N_PAIR, ts, c), lambda i: (0, i, 0))
    half = jax.ShapeDtypeStruct((N_PAIR, s, c), BF16)
    in_specs = [pl.BlockSpec((nb, ts, c), lambda i: (0, i, 0)),
                pl.BlockSpec((nb, hb, c), lambda i: (0, _prev_blk(i, ts, hb), 0)),
                pl.BlockSpec((nb, 8, c), lambda i: (0, 0, 0)),
                pl.BlockSpec((N_PAIR, c, d), lambda i: (0, 0, 0)), row]
    args = [up8, up8, wc8, wd4, x]
    if head is None:
        out_specs = [row, tile, tile, tile]
        out_shape = [jax.ShapeDtypeStruct((s, d), F32), half, half, half]
    else:
        in_specs += [vec, row]
        args += list(head)
        out_specs = [pl.BlockSpec((1, 1), lambda i: (0, 0)), row, row, vec, tile, tile, tile]
        out_shape = [jax.ShapeDtypeStruct((1, 1), F32), jax.ShapeDtypeStruct((s, d), F32),
                     jax.ShapeDtypeStruct((s, d), BF16), jax.ShapeDtypeStruct((1, d), F32), half, half, half]
    return pl.pallas_call(
        body, name=name, grid=(s // ts,), in_specs=in_specs, out_specs=out_specs, out_shape=out_shape,
        scratch_shapes=[pltpu.VMEM((4, hb + ts, c), F32)],
        compiler_params=_params(("parallel",) if head is None else ("arbitrary",)),
    )(*args)


def _ffn_dwdown(act4, db, *, name):
    npair, s, c = act4.shape
    d = db.shape[1]

    def body(a_ref, d_ref, o_ref):
        o_ref[...] = lax.dot_general(a_ref[...], d_ref[...], (((0,), (0,)), ((), ())),
                                     preferred_element_type=F32).astype(BF16)

    return pl.pallas_call(
        body, name=name, grid=(npair,),
        in_specs=[pl.BlockSpec((None, s, c), lambda j: (j, 0, 0)),
                  pl.BlockSpec((s, d), lambda j: (0, 0))],
        out_specs=pl.BlockSpec((None, c, d), lambda j: (j, 0, 0)),
        out_shape=jax.ShapeDtypeStruct((npair, c, d), BF16),
        compiler_params=_params(("parallel",)),
    )(act4, db)


def _ffn_dact(db, wd4, *, name, tm=2048):
    s, d = db.shape
    npair, c, _ = wd4.shape
    tm = min(tm, s)

    def body(d_ref, w_ref, o_ref):
        o_ref[...] = lax.dot_general(d_ref[...], w_ref[...], _NT, preferred_element_type=F32).astype(BF16)

    return pl.pallas_call(
        body, name=name, grid=(s // tm, npair),
        in_specs=[pl.BlockSpec((tm, d), lambda i, j: (i, 0)),
                  pl.BlockSpec((None, c, d), lambda i, j: (j, 0, 0))],
        out_specs=pl.BlockSpec((None, tm, c), lambda i, j: (j, i, 0)),
        out_shape=jax.ShapeDtypeStruct((npair, s, c), BF16),
        compiler_params=_params(("parallel", "parallel")),
    )(db, wd4)


def _ffn_midbwd(up8, ug4, uv4, dact4, wc8, *, name, ts=1024, rc=32):
    nb, s, c = up8.shape
    hb = HALO_S
    ts = min(ts, s)
    n_i = s // ts

    def body(pg_ref, pv_ref, ug_ref, ugn_ref, uv_ref, uvn_ref, d_ref, dn_ref, wg_ref, wv_ref,
             dg_ref, dv_ref, dwg_ref, dwv_ref, extdg, extdv):
        i = pl.program_id(1)
        keep_n = jnp.where(i < n_i - 1, 1.0, 0.0)

        @pl.when(i == 0)
        def _():
            dwg_ref[...] = jnp.zeros_like(dwg_ref)
            dwv_ref[...] = jnp.zeros_like(dwv_ref)

        def du_rows(r0, rows, gg, vv, dav):
            sg = _sigmoid(gg)
            extdg[r0:r0 + rows, :] = dav * vv * (sg * (1.0 + gg * (1.0 - sg)))
            extdv[r0:r0 + rows, :] = dav * (gg * sg)

        for q in range(ts // rc):
            rows = slice(q * rc, q * rc + rc)
            du_rows(q * rc, rc, ug_ref[rows, :].astype(F32), uv_ref[rows, :].astype(F32),
                    d_ref[rows, :].astype(F32))
        du_rows(ts, 8, ugn_ref[0:8, :].astype(F32), uvn_ref[0:8, :].astype(F32),
                dn_ref[0:8, :].astype(F32) * keep_n)

        def fold(p):
            acc = p[0:8, :]
            for r in range(8, rc, 8):
                acc = acc + p[r:r + 8, :]
            return acc

        for extd, p_ref, w_ref, out_ref, dw_ref in ((extdg, pg_ref, wg_ref, dg_ref, dwg_ref),
                                                    (extdv, pv_ref, wv_ref, dv_ref, dwv_ref)):
            part = [None, None, None]
            for q in range(ts // rc):
                r0 = q * rc
                pre = p_ref[r0:r0 + rc, :].astype(F32)
                dup = None
                for k in range(3):
                    sh = extd[pl.ds(r0 + 2 - k, rc), :]
                    term = w_ref[k:k + 1, :] * sh
                    dup = term if dup is None else dup + term
                    prod = fold(pre * sh)
                    part[k] = prod if part[k] is None else part[k] + prod
                out_ref[r0:r0 + rc, :] = dup.astype(BF16)
            for k in range(3):
                dw_ref[k:k + 1, :] += jnp.sum(part[k], axis=0, keepdims=True)

    def blk(off, nxt):
        if nxt:
            return pl.BlockSpec((None, hb, c), lambda j, i: (j + off, _next_blk(i, ts, hb, s), 0))
        return pl.BlockSpec((None, ts, c), lambda j, i: (j + off, i, 0))

    def taps(off):
        return pl.BlockSpec((None, 8, c), lambda j, i: (j + off, 0, 0))

    tile = blk(0, False)
    acc = pl.BlockSpec((None, 8, c), lambda j, i: (j, 0, 0))
    return pl.pallas_call(
        body, name=name, grid=(N_PAIR, n_i),
        in_specs=[tile, blk(N_PAIR, False), tile, blk(0, True), tile, blk(0, True), tile, blk(0, True),
                  taps(0), taps(N_PAIR)],
        out_specs=[tile, tile, acc, acc],
        out_shape=[jax.ShapeDtypeStruct((N_PAIR, s, c), BF16), jax.ShapeDtypeStruct((N_PAIR, s, c), BF16),
                   jax.ShapeDtypeStruct((N_PAIR, 8, c), F32), jax.ShapeDtypeStruct((N_PAIR, 8, c), F32)],
        scratch_shapes=[pltpu.VMEM((ts + 8, c), F32), pltpu.VMEM((ts + 8, c), F32)],
        compiler_params=_params(("parallel", "arbitrary")),
    )(up8, up8, ug4, ug4, uv4, uv4, dact4, dact4, wc8, wc8)


def _ffn_dh(dupg, dupv, w8, x, g, dres, dep, *, name, tm=512):
    npair, s, c = dupg.shape
    d = x.shape[1]
    tm = min(tm, s)

    def body(dg_ref, dv_ref, w_ref, x_ref, g_ref, dres_ref, dep_ref, dx_ref, dxb_ref, dgain_ref):
        @pl.when(pl.program_id(0) == 0)
        def _():
            dgain_ref[...] = jnp.zeros_like(dgain_ref)

        dh = None
        for j in range(npair):
            for src, k in ((dg_ref, j), (dv_ref, j + npair)):
                part = jnp.dot(src[j], w_ref[k], preferred_element_type=F32)
                dh = part if dh is None else dh + part
        dx, dgr = _rms_bwd_rows(x_ref[...], g_ref[...], dh)
        dx = dres_ref[...] + dx
        dx_ref[...] = dx
        dxb_ref[...] = dx.astype(BF16)
        dgain_ref[...] += jnp.sum(dgr, axis=0, keepdims=True)

    row = pl.BlockSpec((tm, d), lambda i: (i, 0))
    vec = pl.BlockSpec((1, d), lambda i: (0, 0))
    dup = pl.BlockSpec((npair, tm, c), lambda i: (0, i, 0))
    return pl.pallas_call(
        body, name=name, grid=(s // tm,),
        in_specs=[dup, dup, pl.BlockSpec((2 * npair, c, d), lambda i: (0, 0, 0)), row, vec, row,
                  pl.BlockSpec(memory_space=pl.ANY)],
        out_specs=[row, row, vec],
        out_shape=[jax.ShapeDtypeStruct((s, d), F32), jax.ShapeDtypeStruct((s, d), BF16),
                   jax.ShapeDtypeStruct((1, d), F32)],
        compiler_params=_params(("arbitrary",)),
    )(dupg, dupv, w8, x, g, dres, dep)


def _ffn_dwup(h, dupg, dupv, *, name, tm=1024):
    npair, s, c = dupg.shape
    d = h.shape[1]

    def body(h_ref, dg_ref, dv_ref, o_ref):
        k = pl.program_id(1)

        @pl.when(k < npair)
        def _():
            o_ref[...] = lax.dot_general(dg_ref[...], h_ref[...], _TN, preferred_element_type=F32).astype(BF16)

        @pl.when(k >= npair)
        def _():
            o_ref[...] = lax.dot_general(dv_ref[...], h_ref[...], _TN, preferred_element_type=F32).astype(BF16)

    return pl.pallas_call(
        body, name=name, grid=(d // tm, 2 * npair),
        in_specs=[pl.BlockSpec((s, tm), lambda m, k: (0, m)),
                  pl.BlockSpec((None, s, c), lambda m, k: (jnp.minimum(k, npair - 1), 0, 0)),
                  pl.BlockSpec((None, s, c), lambda m, k: (jnp.maximum(k - npair, 0), 0, 0))],
        out_specs=pl.BlockSpec((None, c, tm), lambda m, k: (k, 0, m)),
        out_shape=jax.ShapeDtypeStruct((2 * npair, c, d), BF16),
        compiler_params=_params(("parallel", "arbitrary")),
    )(h, dupg, dupv)


def _pool_counts(i, ts, rows, window):
    t = lax.broadcasted_iota(jnp.int32, (rows, 1), 0) + i * ts + 1
    return jnp.minimum(t, window).astype(F32)


def _pool_fwd(x, g, w_pool, scale, *, name, ts=256):
    s, d = x.shape
    hb = HALO_A
    pg = POOL_GROUP
    ts = min(ts, s)

    def body(x_ref, xp_ref, g_ref, w_ref, s_ref, o_ref, p_ref, yu_ref, ext, sa, sb):
        i = pl.program_id(0)
        keep = jnp.where(i > 0, 1.0, 0.0)
        gv = g_ref[...]
        ext[0:hb, :] = _rms_rows(xp_ref[...], gv) * keep
        ext[hb:hb + ts, :] = _rms_rows(x_ref[...], gv)
        rows = hb + ts - 8
        sa[0:8, :] = jnp.zeros((8, d), F32)
        sb[0:8, :] = jnp.zeros((8, d), F32)
        for gi, w in enumerate(POOL_WINDOWS):
            cols = slice(gi * pg, (gi + 1) * pg)
            cur, nxt, k = ext, sa, 1
            while k < w:
                nxt[8:8 + rows, cols] = cur[8:8 + rows, cols] + cur[pl.ds(8 - k, rows), cols]
                cur, nxt, k = nxt, (sb if nxt is sa else sa), 2 * k
            h = ext[hb:hb + ts, cols]
            pv = (cur[hb:hb + ts, cols] / _pool_counts(i, ts, ts, w) - h).astype(BF16)
            p_ref[:, cols] = pv
            yu = jnp.dot(pv, w_ref[gi], preferred_element_type=F32)
            yu_ref[:, cols] = yu.astype(BF16)
            o_ref[:, cols] = x_ref[:, cols] + yu * s_ref[:, cols]

    row = pl.BlockSpec((ts, d), lambda i: (i, 0))
    vec = pl.BlockSpec((1, d), lambda i: (0, 0))
    return pl.pallas_call(
        body, name=name, grid=(s // ts,),
        in_specs=[row, pl.BlockSpec((hb, d), lambda i: (_prev_blk(i, ts, hb), 0)), vec,
                  pl.BlockSpec((d // pg, pg, pg), lambda i: (0, 0, 0)), vec],
        out_specs=[row, row, row],
        out_shape=[jax.ShapeDtypeStruct((s, d), F32), jax.ShapeDtypeStruct((s, d), BF16),
                   jax.ShapeDtypeStruct((s, d), BF16)],
        scratch_shapes=[pltpu.VMEM((hb + ts, d), F32)] * 3,
        compiler_params=_params(("parallel",)),
    )(x, x, g, w_pool, scale)


def _pool_dw(p, dyc, *, name):
    s, d = p.shape
    pg = POOL_GROUP

    def body(p_ref, d_ref, o_ref):
        o_ref[...] = lax.dot_general(p_ref[...], d_ref[...], (((0,), (0,)), ((), ())),
                                     preferred_element_type=F32).astype(BF16)

    blk = pl.BlockSpec((s, pg), lambda gi: (0, gi))
    return pl.pallas_call(
        body, name=name, grid=(d // pg,),
        in_specs=[blk, blk], out_specs=pl.BlockSpec((None, pg, pg), lambda gi: (gi, 0, 0)),
        out_shape=jax.ShapeDtypeStruct((d // pg, pg, pg), BF16),
        compiler_params=_params(("parallel",)),
    )(p, dyc)


def _pool_bwd(dres, w_pool, scale, yu, x, g, *, name, ts=256):
    s, d = x.shape
    hb = HALO_A
    pg = POOL_GROUP
    ts = min(ts, s)
    n_i = s // ts

    def body(dres_ref, dresn_ref, w_ref, s_ref, yu_ref, x_ref, g_ref,
             dx_ref, dxb_ref, dg_ref, dyc_ref, ds_ref, ext, dh, sa, sb, dp_s):
        i = pl.program_id(0)
        keep_n = jnp.where(i < n_i - 1, 1.0, 0.0)

        @pl.when(i == 0)
        def _():
            dg_ref[...] = jnp.zeros_like(dg_ref)
            ds_ref[...] = jnp.zeros_like(ds_ref)

        ds_ref[...] += jnp.sum(dres_ref[...] * yu_ref[...].astype(F32), axis=0, keepdims=True)
        for gi, w in enumerate(POOL_WINDOWS):
            cols = slice(gi * pg, (gi + 1) * pg)
            dyc = (dres_ref[:, cols] * s_ref[:, cols]).astype(BF16)
            dyc_ref[:, cols] = dyc
            dp = lax.dot_general(dyc, w_ref[gi], _NT, preferred_element_type=F32)
            dp_s[:, cols] = dp
            ext[0:ts, cols] = dp / _pool_counts(i, ts, ts, w)
            dycn = (dresn_ref[:, cols] * s_ref[:, cols]).astype(BF16)
            dpn = lax.dot_general(dycn, w_ref[gi], _NT, preferred_element_type=F32)
            ext[ts:ts + hb, cols] = dpn / _pool_counts(i + 1, ts, hb, w) * keep_n
        rows = ts + hb - 8
        sa[rows:rows + 8, :] = jnp.zeros((8, d), F32)
        sb[rows:rows + 8, :] = jnp.zeros((8, d), F32)
        for gi, w in enumerate(POOL_WINDOWS):
            cols = slice(gi * pg, (gi + 1) * pg)
            cur, nxt, k = ext, sa, 1
            while k < w:
                nxt[0:rows, cols] = cur[0:rows, cols] + cur[pl.ds(k, rows), cols]
                cur, nxt, k = nxt, (sb if nxt is sa else sa), 2 * k
            dh[:, cols] = cur[0:ts, cols] - dp_s[:, cols]
        dx, dgr = _rms_bwd_rows(x_ref[...], g_ref[...], dh[...])
        dx = dres_ref[...] + dx
        dx_ref[...] = dx
        dxb_ref[...] = dx.astype(BF16)
        dg_ref[...] += jnp.sum(dgr, axis=0, keepdims=True)

    row = pl.BlockSpec((ts, d), lambda i: (i, 0))
    vec = pl.BlockSpec((1, d), lambda i: (0, 0))
    return pl.pallas_call(
        body, name=name, grid=(n_i,),
        in_specs=[row, pl.BlockSpec((hb, d), lambda i: (_next_blk(i, ts, hb, s), 0)),
                  pl.BlockSpec((d // pg, pg, pg), lambda i: (0, 0, 0)), vec, row, row, vec],
        out_specs=[row, row, vec, row, vec],
        out_shape=[jax.ShapeDtypeStruct((s, d), F32), jax.ShapeDtypeStruct((s, d), BF16),
                   jax.ShapeDtypeStruct((1, d), F32), jax.ShapeDtypeStruct((s, d), BF16),
                   jax.ShapeDtypeStruct((1, d), F32)],
        scratch_shapes=[pltpu.VMEM((ts + hb, d), F32), pltpu.VMEM((ts, d), F32),
                        pltpu.VMEM((ts + hb, d), F32), pltpu.VMEM((ts + hb, d), F32),
                        pltpu.VMEM((ts, d), F32)],
        compiler_params=_params(("arbitrary",)),
    )(dres, dres, w_pool, scale, yu, x, g)


def _pad_rows(w, rows):
    pad = [(0, 0)] * (w.ndim - 2) + [(0, rows - w.shape[-2]), (0, 0)]
    return jnp.pad(w, pad)


def _ffn_layer_fwd(x, nf, w8, wc8, wd4, tag, head=None):
    h, up8 = _ffn_up(x, nf, w8, name=f"ffn{tag}_up")
    res = _ffn_mid_down(up8, wc8, wd4, x, head, name=f"ffn{tag}_mid_down")
    act4, ug4, uv4 = res[-3:]
    out = res[0] if head is None else tuple(res[:4])
    return out, (h, up8, ug4, uv4, act4)


def _ffn_layer_bwd(d, db, x, nf, w8, wc8, wd4, saved, tag, grads_ready):
    h, up8, ug4, uv4, act4 = saved
    dact4 = _ffn_dact(db, wd4, name=f"ffn{tag}_dact")
    dwd4 = _ffn_dwdown(act4, db, name=f"ffn{tag}_dwdown")
    dupg, dupv, dwg, dwv = _ffn_midbwd(up8, ug4, uv4, dact4, wc8, name=f"ffn{tag}_midbwd")
    dw8 = _ffn_dwup(h, dupg, dupv, name=f"ffn{tag}_dwup")
    sent = grads_ready(f"ffn{tag}", {f"w_up{tag}": dw8, f"w_down{tag}": dwd4})
    dx, dxb, dnf = _ffn_dh(dupg, dupv, w8, x, nf, d, sent, name=f"ffn{tag}_dh")
    dwc8 = jnp.concatenate([dwg, dwv], axis=0)[:, :3]
    return dx, dxb, dnf, dwc8


def _local_step(x, target, weights, grads_ready):
    w0 = weights("mix0", x)
    wa = _pad_rows(w0["conv_a"], 32)
    wb = _pad_rows(w0["conv_b"], 8)
    wc = [_pad_rows(w0["conv_ffn"][l], 8) for l in range(2)]
    h0, z = _rms_mm(x, w0["norm_mix_even"], w0["w_in_t"], name="mix0_in", out_dtype=F32, tn=2560)
    ab, ca, x1 = _mix0_fwd(z, wa, w0["ln_a_g"], w0["ln_a_b"], wb, w0["w_out"], x, name="mix0_mid")
    w1 = weights("ffn0", x1)
    x2, ffn0 = _ffn_layer_fwd(x1, w0["norm_ffn"][0:1], w1["w_up"], wc[0], w1["w_down"], 0)
    w2 = weights("ffn1", x2)
    x3, p, yu = _pool_fwd(x2, w0["norm_mix_odd"], w2["w_pool"], w0["pool_scale"], name="pool_fwd")
    (loss, d4, d4b, g_norm_final), ffn1 = _ffn_layer_fwd(
        x3, w0["norm_ffn"][1:2], w2["w_up"], wc[1], w2["w_down"], 1, head=(w0["norm_final"], target))

    d3, d3b, g_nf1, g_wc1 = _ffn_layer_bwd(
        d4, d4b, x3, w0["norm_ffn"][1:2], w2["w_up"], wc[1], w2["w_down"], ffn1, 1, grads_ready)
    d2, d2b, g_nmo, dyc, g_scale = _pool_bwd(d3, w2["w_pool"], w0["pool_scale"], yu, x2, w0["norm_mix_odd"],
                                             name="pool_bwd")
    g_pool = _pool_dw(p, dyc, name="pool_dw")
    d1, d1b, g_nf0, g_wc0 = _ffn_layer_bwd(
        d2, d2b, x1, w0["norm_ffn"][0:1], w1["w_up"], wc[0], w1["w_down"], ffn0, 0, grads_ready)
    dab = _mm(d1b, w0["w_out"], tb=True, name="mix0_dab", tm=1024, tn=1024)
    g_out = _mm(ab, d1b, ta=True, out_dtype=BF16, name="mix0_dwout", tm=1024, tn=512)
    dz, g_wa, g_wb, g_lg, g_lb = _mix0_bwd(z, ca, dab, wa, w0["ln_a_g"], w0["ln_a_b"], wb, name="mix0_midbwd")
    g_in = _mm(dz, h0, ta=True, out_dtype=BF16, name="mix0_dwin", tm=512, tn=1024)
    sent_mix = grads_ready("mix0", {"w_in": g_in, "w_out": g_out, "w_pool": g_pool})
    dx, g_nme = _mm_rms_bwd(dz, w0["w_in_t"], x, w0["norm_mix_even"], d1, sent_mix, name="mix0_dh")

    small = {
        "norm_mix_even": g_nme, "conv_a": g_wa[:A_TAPS], "ln_a_g": g_lg, "ln_a_b": g_lb, "conv_b": g_wb[:3],
        "norm_mix_odd": g_nmo, "pool_scale": g_scale, "norm_ffn": jnp.concatenate([g_nf0, g_nf1], axis=0),
        "conv_ffn": [g_wc0, g_wc1], "norm_final": g_norm_final,
    }
    return loss[0, 0], dx, small


def _my_pos():
    return lax.axis_index("x"), lax.axis_index("y"), lax.axis_index("c")


def _flip(pos, r):
    x, y, c = pos
    return (1 - x if r & 4 else x, 1 - y if r & 2 else y, 1 - c if r & 1 else c)


def _dev_index(pos):
    return 4 * pos[0] + 2 * pos[1] + pos[2]


_HBM = pl.BlockSpec(memory_space=pltpu.HBM)
_SEM = pl.BlockSpec(memory_space=pltpu.SEMAPHORE)
_EFFECT = pltpu.SideEffectType.DATAFLOW_SIDE_EFFECTING


def _exchange_copy(ins, lands, send_sems, recv_sems, scatter, pos, a, r, receiving):
    me = _dev_index(pos)
    peer = _flip(pos, r)
    dest = _dev_index(pos) if receiving else _dev_index(peer)
    src = ins[a].at[dest] if scatter[a] else ins[a]
    slot = _dev_index(peer) if receiving else me
    return pltpu.make_async_remote_copy(
        src_ref=src, dst_ref=lands[a].at[slot], send_sem=send_sems.at[a * (N_DEV - 1) + r - 1],
        recv_sem=recv_sems.at[a * (N_DEV - 1) + r - 1],
        device_id=peer, device_id_type=pl.DeviceIdType.MESH)


ALL_PEERS = tuple(range(1, N_DEV))
CHIP_PEERS = (1, 2, 4, 6)
FORWARDED = (2, 4, 6)


def _exchange_start(arrays, scatter, after, *, name, peers=ALL_PEERS):
    n = len(arrays)
    me = _dev_index(_my_pos())
    lands = []
    for arr, sc in zip(arrays, scatter):
        own = lax.dynamic_index_in_dim(arr, me, 0, keepdims=True) if sc else arr[None]
        shape = arr.shape if sc else (N_DEV,) + arr.shape
        lands.append(lax.dynamic_update_slice(lax.empty(shape, arr.dtype), own, (me,) + (0,) * (len(shape) - 1)))

    def body(*refs):
        ins, lnd = refs[:n], refs[n:2 * n]
        send_sems, recv_sems = refs[2 * n + 1], refs[2 * n + 2]
        token = refs[-1]
        pos = _my_pos()
        for a in range(n):
            for r in peers:
                _exchange_copy(ins, lnd, send_sems, recv_sems, scatter, pos, a, r, receiving=False).start()
        token[...] = jnp.zeros_like(token)

    bufs = [pltpu.with_memory_space_constraint(t, pltpu.HBM) for t in list(arrays) + lands]
    sems = pltpu.SemaphoreType.DMA((n * (N_DEV - 1),))
    res = pl.pallas_call(
        body, name=name,
        out_shape=(sems, sems, *[pltpu.HBM(t.shape, t.dtype) for t in bufs], jax.ShapeDtypeStruct((8, 128), F32)),
        in_specs=[_HBM] * (2 * n) + [pl.BlockSpec(memory_space=pl.ANY)],
        out_specs=(_SEM, _SEM, *[_HBM] * (2 * n), pl.BlockSpec(memory_space=pltpu.VMEM)),
        input_output_aliases={i: 2 + i for i in range(2 * n)},
        compiler_params=pltpu.CompilerParams(has_side_effects=_EFFECT),
    )(*bufs, after)
    return res[0], res[1], list(res[2:2 + n]), list(res[2 + n:2 + 2 * n]), res[-1]


def _exchange_wait(started, scatter, after, *, name, peers=ALL_PEERS):
    send_sems, recv_sems, arrays, lands, _ = started
    n = len(arrays)

    def body(*refs):
        ins, lnd = refs[:n], refs[n:2 * n]
        send_sems, recv_sems = refs[2 * n], refs[2 * n + 1]
        pos = _my_pos()
        for a in range(n):
            for r in peers:
                _exchange_copy(ins, lnd, send_sems, recv_sems, scatter, pos, a, r, receiving=False).wait_send()
                _exchange_copy(ins, lnd, send_sems, recv_sems, scatter, pos, a, r, receiving=True).wait_recv()

    bufs = list(arrays) + list(lands)
    after = list(after) if isinstance(after, (list, tuple)) else [after]
    res = pl.pallas_call(
        body, name=name,
        out_shape=tuple(pltpu.HBM(t.shape, t.dtype) for t in bufs),
        in_specs=[_HBM] * (2 * n) + [_SEM, _SEM] + [pl.BlockSpec(memory_space=pl.ANY)] * len(after),
        out_specs=tuple([_HBM] * (2 * n)),
        input_output_aliases={i: i for i in range(2 * n)},
        compiler_params=pltpu.CompilerParams(has_side_effects=_EFFECT),
    )(*bufs, send_sems, recv_sems, *after)
    return list(res[n:])


def _forward_copy(lands, send_sems, recv_sems, pos, a, q, receiving):
    slot = _dev_index(_flip(pos, q ^ 1 if receiving else q))
    idx = a * len(FORWARDED) + FORWARDED.index(q)
    return pltpu.make_async_remote_copy(
        src_ref=lands[a].at[slot], dst_ref=lands[a].at[slot], send_sem=send_sems.at[idx], recv_sem=recv_sems.at[idx],
        device_id=_flip(pos, 1), device_id_type=pl.DeviceIdType.MESH)


def _forward_start(lands, after, *, name):
    n = len(lands)

    def body(*refs):
        lnd = refs[:n]
        send_sems, recv_sems = refs[n + 1], refs[n + 2]
        token = refs[-1]
        pos = _my_pos()
        for a in range(n):
            for q in FORWARDED:
                _forward_copy(lnd, send_sems, recv_sems, pos, a, q, receiving=False).start()
        token[...] = jnp.zeros_like(token)

    sems = pltpu.SemaphoreType.DMA((n * len(FORWARDED),))
    res = pl.pallas_call(
        body, name=name,
        out_shape=(sems, sems, *[pltpu.HBM(t.shape, t.dtype) for t in lands], jax.ShapeDtypeStruct((8, 128), F32)),
        in_specs=[_HBM] * n + [pl.BlockSpec(memory_space=pl.ANY)],
        out_specs=(_SEM, _SEM, *[_HBM] * n, pl.BlockSpec(memory_space=pltpu.VMEM)),
        input_output_aliases={i: 2 + i for i in range(n)},
        compiler_params=pltpu.CompilerParams(has_side_effects=_EFFECT),
    )(*lands, after)
    return res[0], res[1], list(res[2:2 + n]), res[-1]


def _forward_wait(forwarded, after, *, name):
    send_sems, recv_sems, lands, _ = forwarded
    n = len(lands)

    def body(*refs):
        lnd = refs[:n]
        send_sems, recv_sems = refs[n], refs[n + 1]
        pos = _my_pos()
        for a in range(n):
            for q in FORWARDED:
                _forward_copy(lnd, send_sems, recv_sems, pos, a, q, receiving=False).wait_send()
                _forward_copy(lnd, send_sems, recv_sems, pos, a, q, receiving=True).wait_recv()

    res = pl.pallas_call(
        body, name=name,
        out_shape=tuple(pltpu.HBM(t.shape, t.dtype) for t in lands),
        in_specs=[_HBM] * n + [_SEM, _SEM, pl.BlockSpec(memory_space=pl.ANY)],
        out_specs=tuple([_HBM] * n),
        input_output_aliases={i: i for i in range(n)},
        compiler_params=pltpu.CompilerParams(has_side_effects=_EFFECT),
    )(*lands, send_sems, recv_sems, after)
    return list(res)


def _adamw_update(p_ref, w_ref, m_ref, v_ref, g_ref, d_ref, mo_ref, vo_ref):
    g = p_ref[0].astype(F32)
    for k in range(1, N_DEV):
        g = g + p_ref[k].astype(F32)
    mn = ADAM_B1 * m_ref[...] + (1.0 - ADAM_B1) * g
    vn = ADAM_B2 * v_ref[...] + (1.0 - ADAM_B2) * (g * g)
    m_hat = mn / (1.0 - ADAM_B1 ** ADAM_STEP)
    v_hat = vn / (1.0 - ADAM_B2 ** ADAM_STEP)
    g_ref[...] = g
    d_ref[...] = -ADAM_LR * (m_hat / (jnp.sqrt(v_hat) + ADAM_EPS) + ADAM_WD * w_ref[...])
    mo_ref[...] = mn
    vo_ref[...] = vn


def _adamw_small(parts, ws, ms, vs, *, name):
    n = len(ws)

    def body(*refs):
        ins, outs = refs[:4 * n], refs[4 * n:]
        for j in range(n):
            _adamw_update(ins[j], ins[n + j], ins[2 * n + j], ins[3 * n + j], *outs[4 * j:4 * j + 4])

    return pl.pallas_call(
        body, name=name,
        out_shape=[jax.ShapeDtypeStruct(w.shape, F32) for w in ws for _ in range(4)],
    )(*parts, *ws, *ms, *vs)


def _adamw(parts, w, m, v, *, name, tr):
    nl, r, c = w.shape
    assert len(parts) == nl and r % tr == 0
    n_i = r // tr

    def body(*refs):
        p_refs = refs[:nl]
        w_ref, m_ref, v_ref = refs[nl:nl + 3]

        def update(p_ref):
            _adamw_update(p_ref, w_ref, m_ref, v_ref, *refs[nl + 3:])

        if nl == 1:
            update(p_refs[0])
        else:
            for layer in range(nl):
                pl.when(pl.program_id(0) == layer)(lambda p_ref=p_refs[layer]: update(p_ref))

    def parts_spec(layer):
        def index(l, i):
            return (0, jnp.where(l < layer, 0, jnp.where(l > layer, n_i - 1, i)), 0)
        return pl.BlockSpec((N_DEV, tr, c), index)

    blk = pl.BlockSpec((None, tr, c), lambda l, i: (l, i, 0))
    return pl.pallas_call(
        body, name=name, grid=(nl, n_i),
        in_specs=[parts_spec(layer) for layer in range(nl)] + [blk, blk, blk],
        out_specs=[blk] * 4, out_shape=[jax.ShapeDtypeStruct((nl, r, c), F32)] * 4,
        compiler_params=_params(("arbitrary", "arbitrary")),
    )(*parts, w, m, v)


def _pack(parts, lead=()):
    flat = jnp.concatenate([p.reshape(lead + (-1,)) for p in parts], axis=-1)
    n = flat.shape[-1]
    rows = -(-n // (8 * 128)) * 8
    flat = jnp.pad(flat, [(0, 0)] * len(lead) + [(0, rows * 128 - n)])
    return flat.reshape(lead + (rows, 128))


def _to_dev_major(g, axis):
    shp = g.shape
    g = g.reshape(shp[:axis] + (N_DEV, shp[axis] // N_DEV) + shp[axis + 1:])
    return jnp.moveaxis(g, axis, 0)


def _from_dev_major(g, axis):
    g = jnp.moveaxis(g, 0, axis)
    shp = g.shape
    return g.reshape(shp[:axis] + (shp[axis] * shp[axis + 1],) + shp[axis + 2:])


SMALL_SHARDED = ("conv_a", "conv_b", "norm_mix_odd", "pool_scale", "conv_ffn_w")
SMALL_REPLICATED = ("norm_mix_even", "ln_a_g", "ln_a_b", "norm_ffn", "norm_final")
BIG = {"w_in": ("w_in", 0, 320), "w_out": ("w_out", 0, 128), "w_pool": ("w_pool", 0, 128),
       "w_up0": ("w_up", 0, 352), "w_up1": ("w_up", 1, 352),
       "w_down0": ("w_down", 0, 352), "w_down1": ("w_down", 1, 352)}
COLUMN_SHARDED = ("w_in", "w_up")


def kernel(x, norm_mix_even, w_in, conv_a, ln_a_g, ln_a_b, conv_b, w_out, norm_mix_odd, w_pool, pool_scale, norm_ffn, w_up, conv_ffn_w, w_down, norm_final, loss_target, m_norm_mix_even, m_w_in, m_conv_a, m_ln_a_g, m_ln_a_b, m_conv_b, m_w_out, m_norm_mix_odd, m_w_pool, m_pool_scale, m_norm_ffn, m_w_up, m_conv_ffn_w, m_w_down, m_norm_final, v_norm_mix_even, v_w_in, v_conv_a, v_ln_a_g, v_ln_a_b, v_conv_b, v_w_out, v_norm_mix_odd, v_w_pool, v_pool_scale, v_norm_ffn, v_w_up, v_conv_ffn_w, v_w_down, v_norm_final):
    names = ("norm_mix_even", "w_in", "conv_a", "ln_a_g", "ln_a_b", "conv_b", "w_out", "norm_mix_odd", "w_pool",
             "pool_scale", "norm_ffn", "w_up", "conv_ffn_w", "w_down", "norm_final")
    wts = dict(zip(names, (norm_mix_even, w_in, conv_a, ln_a_g, ln_a_b, conv_b, w_out, norm_mix_odd, w_pool,
                           pool_scale, norm_ffn, w_up, conv_ffn_w, w_down, norm_final)))
    mom = dict(zip(names, (m_norm_mix_even, m_w_in, m_conv_a, m_ln_a_g, m_ln_a_b, m_conv_b, m_w_out, m_norm_mix_odd,
                           m_w_pool, m_pool_scale, m_norm_ffn, m_w_up, m_conv_ffn_w, m_w_down, m_norm_final)))
    var = dict(zip(names, (v_norm_mix_even, v_w_in, v_conv_a, v_ln_a_g, v_ln_a_b, v_conv_b, v_w_out, v_norm_mix_odd,
                           v_w_pool, v_pool_scale, v_norm_ffn, v_w_up, v_conv_ffn_w, v_w_down, v_norm_final)))
    d = x.shape[-1]

    def shard3d(t, param):
        a = t[param]
        if param in COLUMN_SHARDED:
            return jnp.swapaxes(a, 1, 2)
        return a.reshape(a.shape[0], -1, a.shape[-1])

    def unshard3d(a, param):
        if param in COLUMN_SHARDED:
            return jnp.swapaxes(a, 1, 2)
        return a.reshape(wts[param].shape)

    def shard2d(t, key):
        param, layer, _ = BIG[key]
        return shard3d(t, param)[layer]

    small_w = _pack([wts[k] for k in SMALL_SHARDED])
    bf = {k: shard2d(wts, k).astype(BF16) for k in BIG}
    gather_groups = {"mix0": ("w_in", "w_out", "small"), "ffn0": ("w_up0", "w_down0"),
                     "ffn1": ("w_pool", "w_up1", "w_down1")}
    order = list(gather_groups)
    started = {}

    def start_gather(grp, after):
        arrs = [small_w if k == "small" else bf[k] for k in gather_groups[grp]]
        started[grp] = _exchange_start(arrs, [False] * len(arrs), after, name=f"gather_{grp}_start", peers=CHIP_PEERS)

    start_gather(order[0], small_w)

    def weights(grp, after):
        keys = gather_groups[grp]
        lands = _exchange_wait(started[grp], [False] * len(keys), after, name=f"gather_{grp}_wait", peers=CHIP_PEERS)
        forwarded = _forward_start(lands, small_w, name=f"gather_{grp}_forward")
        if grp != order[-1]:
            start_gather(order[order.index(grp) + 1], forwarded[-1])
            behind = started[order[order.index(grp) + 1]][-1]
        else:
            behind = forwarded[-1]
        gw = dict(zip(keys, _forward_wait(forwarded, behind, name=f"gather_{grp}_forward_wait")))
        if grp == "ffn0":
            return {"w_up": gw["w_up0"], "w_down": gw["w_down0"].reshape(N_PAIR, -1, d)}
        if grp == "ffn1":
            return {"w_up": gw["w_up1"], "w_down": gw["w_down1"].reshape(N_PAIR, -1, d),
                    "w_pool": _from_dev_major(gw["w_pool"].reshape(N_DEV, len(POOL_WINDOWS), -1, POOL_GROUP), 1)}
        per_dev = gw["small"].reshape(N_DEV, -1)
        sizes = [wts[k].size for k in SMALL_SHARDED]
        offs = [sum(sizes[:i]) for i in range(len(sizes))]
        small_full = {k: per_dev[:, o:o + n_].reshape((N_DEV,) + wts[k].shape)
                      for k, o, n_ in zip(SMALL_SHARDED, offs, sizes)}
        return {
            "norm_mix_even": norm_mix_even, "ln_a_g": ln_a_g, "ln_a_b": ln_a_b, "norm_ffn": norm_ffn,
            "norm_final": norm_final[None],
            "w_in_t": gw["w_in"].reshape(-1, d),
            "w_out": gw["w_out"].reshape(-1, d),
            "conv_a": _from_dev_major(small_full["conv_a"][:, 0], 1),
            "conv_b": _from_dev_major(small_full["conv_b"][:, 0], 1),
            "norm_mix_odd": _from_dev_major(small_full["norm_mix_odd"], 1),
            "pool_scale": _from_dev_major(small_full["pool_scale"], 1),
            "conv_ffn": [small_full["conv_ffn_w"][:, l] for l in range(2)],
        }

    def dev_major(k, g):
        if k == "w_pool":
            return _to_dev_major(g, 1).reshape(N_DEV, -1, POOL_GROUP)
        return g.reshape(N_DEV, -1, g.shape[-1])

    sent = {}

    def grads_ready(grp, grads):
        keys = tuple(grads)
        parts = [dev_major(k, grads[k]) for k in keys]
        sent[grp] = (keys, _exchange_start(parts, [True] * len(keys), small_w, name=f"grads_{grp}_start"))
        return sent[grp][1][-1]

    loss, dx, g = _local_step(x[0], loss_target[0], weights, grads_ready)

    def small2d(a):
        return a.reshape(-1, a.shape[-1])

    small_keys = SMALL_SHARDED + SMALL_REPLICATED
    small_parts = [_to_dev_major(g["conv_a"], 1), _to_dev_major(g["conv_b"], 1),
                   _to_dev_major(g["norm_mix_odd"], 1), _to_dev_major(g["pool_scale"], 1),
                   jnp.stack(g["conv_ffn"], axis=1).reshape(N_DEV, -1, w_up.shape[-1])]
    small_parts += [g[k] for k in SMALL_REPLICATED]
    small_scatter = [True] * len(SMALL_SHARDED) + [False] * len(SMALL_REPLICATED)
    small_parts.append(jnp.full((1, 128), loss, F32))
    small_scatter.append(False)
    sent["small"] = (small_keys + ("loss",),
                     _exchange_start(small_parts, small_scatter, dx, name="grads_small_start"))

    landed, out, raw = {}, {}, {}

    def wait_grads(grp, after, scat=None):
        keys, st = sent[grp]
        scat = [True] * len(keys) if scat is None else scat
        landed.update(zip(keys, _exchange_wait(st, scat, after, name=f"grads_{grp}_wait")))

    def update(param, keys):
        raw[param] = _adamw([landed[k] for k in keys], shard3d(wts, param), shard3d(mom, param),
                            shard3d(var, param), name=f"adamw_{param}", tr=BIG[keys[0]][2])
        out[param] = [unshard3d(t, param) for t in raw[param]]

    wait_grads("ffn1", sent["small"][1][-1])
    wait_grads("ffn0", landed["w_up1"])
    update("w_up", ("w_up0", "w_up1"))
    update("w_down", ("w_down0", "w_down1"))
    wait_grads("mix0", [raw["w_up"][1], raw["w_down"][1]])
    update("w_in", ("w_in",))
    update("w_out", ("w_out",))
    update("w_pool", ("w_pool",))
    wait_grads("small", raw["w_pool"][1], small_scatter)
    res = _adamw_small([landed[k] for k in small_keys], [small2d(wts[k]) for k in small_keys],
                       [small2d(mom[k]) for k in small_keys], [small2d(var[k]) for k in small_keys],
                       name="adamw_small")
    for j, k in enumerate(small_keys):
        out[k] = [t.reshape(wts[k].shape) for t in res[4 * j:4 * j + 4]]

    loss = jnp.sum(landed["loss"][:, 0, 0])
    return (loss, dx[None], *[out[k][0] for k in names], *[out[k][1] for k in names],
            *[out[k][2] for k in names], *[out[k][3] for k in names])
```

```python
import jax
import jax.numpy as jnp
from jax import lax
from jax.experimental import pallas as pl
from jax.experimental.pallas import tpu as pltpu

F32 = jnp.float32
BF16 = jnp.bfloat16

RMS_EPS = 1e-6
LN_EPS = 1e-5
ADAM_LR = 0.001
ADAM_B1 = 0.9
ADAM_B2 = 0.999
ADAM_EPS = 1e-08
ADAM_WD = 0.01
ADAM_STEP = 10

N_DEV = 8
N_PAIR = N_DEV // 2
A_WIDTH = 512
A_TAPS = 31
POOL_WINDOWS = (2, 4, 8, 16)
POOL_GROUP = 256
HALO_A = 32
HALO_S = 16
VMEM_LIMIT = 56 * 1024 * 1024


def _params(sem, vmem=VMEM_LIMIT):
    return pltpu.CompilerParams(dimension_semantics=sem, vmem_limit_bytes=vmem)


def _sigmoid(x):
    return 0.5 * jnp.tanh(0.5 * x) + 0.5


def _prev_blk(i, ts, hb):
    return jnp.maximum(i * (ts // hb) - 1, 0)


def _next_blk(i, ts, hb, s):
    return jnp.minimum((i + 1) * (ts // hb), s // hb - 1)


def _mm(a, b, *, name, ta=False, tb=False, out_dtype=F32, tm=512, tn=512):
    m, k = (a.shape[1], a.shape[0]) if ta else a.shape
    n = b.shape[0] if tb else b.shape[1]
    tm, tn = min(tm, m), min(tn, n)
    assert m % tm == 0 and n % tn == 0, (name, m, n, tm, tn)
    dims = (((0,) if ta else (1,), (1,) if tb else (0,)), ((), ()))

    def body(a_ref, b_ref, o_ref):
        o_ref[...] = lax.dot_general(a_ref[...], b_ref[...], dims, preferred_element_type=F32).astype(out_dtype)

    a_spec = pl.BlockSpec((k, tm), lambda i, j: (0, i)) if ta else pl.BlockSpec((tm, k), lambda i, j: (i, 0))
    b_spec = pl.BlockSpec((tn, k), lambda i, j: (j, 0)) if tb else pl.BlockSpec((k, tn), lambda i, j: (0, j))
    return pl.pallas_call(
        body, name=name, grid=(m // tm, n // tn),
        in_specs=[a_spec, b_spec], out_specs=pl.BlockSpec((tm, tn), lambda i, j: (i, j)),
        out_shape=jax.ShapeDtypeStruct((m, n), out_dtype),
        compiler_params=_params(("parallel", "parallel")),
    )(a, b)


def _rms_rows(xv, gv):
    return xv * lax.rsqrt(jnp.mean(xv * xv, axis=-1, keepdims=True) + RMS_EPS) * gv


_NT = (((1,), (1,)), ((), ()))
_TN = (((0,), (0,)), ((), ()))


def _rms_mm(x, g, wt, *, name, out_dtype, tm=1024, tn=512):
    s, d = x.shape
    n = wt.shape[0]
    tm = min(tm, s)
    assert s % tm == 0 and n % tn == 0

    def body(x_ref, g_ref, w_ref, h_ref, z_ref, hs_ref):
        @pl.when(pl.program_id(1) == 0)
        def _():
            hv = _rms_rows(x_ref[...], g_ref[...]).astype(BF16)
            hs_ref[...] = hv
            h_ref[...] = hv

        z_ref[...] = lax.dot_general(hs_ref[...], w_ref[...], _NT, preferred_element_type=F32).astype(out_dtype)

    return pl.pallas_call(
        body, name=name, grid=(s // tm, n // tn),
        in_specs=[pl.BlockSpec((tm, d), lambda i, j: (i, 0)),
                  pl.BlockSpec((1, d), lambda i, j: (0, 0)),
                  pl.BlockSpec((tn, d), lambda i, j: (j, 0))],
        out_specs=[pl.BlockSpec((tm, d), lambda i, j: (i, 0)),
                   pl.BlockSpec((tm, tn), lambda i, j: (i, j))],
        out_shape=[jax.ShapeDtypeStruct((s, d), BF16), jax.ShapeDtypeStruct((s, n), out_dtype)],
        scratch_shapes=[pltpu.VMEM((tm, d), BF16)],
        compiler_params=_params(("parallel", "arbitrary")),
    )(x, g, wt)


def _rms_bwd_rows(xv, gv, dh):
    r = lax.rsqrt(jnp.mean(xv * xv, axis=-1, keepdims=True) + RMS_EPS)
    xh = xv * r
    dn = dh * gv
    dx = r * (dn - xh * jnp.mean(dn * xh, axis=-1, keepdims=True))
    return dx, dh * xh


def _mm_rms_bwd(a, b, x, g, dres, dep, *, name, tm=512):
    s, k = a.shape
    d = b.shape[1]
    tm = min(tm, s)

    def body(a_ref, b_ref, x_ref, g_ref, dres_ref, dep_ref, dx_ref, dg_ref):
        @pl.when(pl.program_id(0) == 0)
        def _():
            dg_ref[...] = jnp.zeros_like(dg_ref)

        dh = jnp.dot(a_ref[...], b_ref[...], preferred_element_type=F32)
        dx, dgr = _rms_bwd_rows(x_ref[...], g_ref[...], dh)
        dx_ref[...] = dres_ref[...] + dx
        dg_ref[...] += jnp.sum(dgr, axis=0, keepdims=True)

    row = pl.BlockSpec((tm, d), lambda i: (i, 0))
    vec = pl.BlockSpec((1, d), lambda i: (0, 0))
    return pl.pallas_call(
        body, name=name, grid=(s // tm,),
        in_specs=[pl.BlockSpec((tm, k), lambda i: (i, 0)), pl.BlockSpec((k, d), lambda i: (0, 0)), row, vec, row,
                  pl.BlockSpec(memory_space=pl.ANY)],
        out_specs=[row, vec],
        out_shape=[jax.ShapeDtypeStruct((s, d), F32), jax.ShapeDtypeStruct((1, d), F32)],
        compiler_params=_params(("arbitrary",)),
    )(a, b, x, g, dres, dep)


def _conv_taps(ext_ref, w_ref, n_taps, base, r0, rows, reverse=False):
    acc = None
    for k in range(n_taps):
        off = r0 + (base - k if reverse else base + k)
        term = w_ref[k:k + 1, :] * ext_ref[pl.ds(off, rows), :]
        acc = term if acc is None else acc + term
    return acc


def _shift_copies(src_ref, sh_ref, rows):
    for b in range(1, 8):
        sh_ref[b, 0:rows, :] = src_ref[pl.ds(b, rows), :]


def _shifted(src_ref, sh_ref, start, rows, off):
    a, b = divmod(off, 8)
    ref = src_ref if b == 0 else sh_ref.at[b]
    return ref[pl.ds(start + 8 * a, rows), :]


def _mix0_fwd(z, conv_a, ln_g, ln_b, conv_b, w_out, x, *, name, ts=512, rc=32):
    s = z.shape[0]
    c = A_WIDTH
    hb = HALO_A
    half = ts // 2

    def body(z_ref, zp_ref, wa_ref, lg_ref, lb_ref, wb_ref, wo_ref, x_ref, ab_ref, ca_ref, o_ref, exta, extb, sha):
        keep = jnp.where(pl.program_id(0) > 0, 1.0, 0.0)
        zp = zp_ref[...]
        exta[0:hb, :] = zp[:, 0:c] * _sigmoid(zp[:, c:2 * c]) * keep
        extb[0:hb, :] = zp[:, 3 * c:4 * c] * zp[:, 4 * c:5 * c] * keep
        exta[hb:hb + ts, :] = z_ref[:, 0:c] * _sigmoid(z_ref[:, c:2 * c])
        extb[hb:hb + ts, :] = z_ref[:, 3 * c:4 * c] * z_ref[:, 4 * c:5 * c]
        _shift_copies(exta, sha, hb + ts - 8)
        lg = lg_ref[...]
        lb = lb_ref[...]
        for q in range(ts // rc):
            r0 = q * rc
            ca = None
            for k in range(A_TAPS):
                term = wa_ref[k:k + 1, :] * _shifted(exta, sha, r0, rc, hb - (A_TAPS - 1) + k)
                ca = term if ca is None else ca + term
            ca_ref[r0:r0 + rc, :] = ca
            mu = jnp.mean(ca, axis=-1, keepdims=True)
            xc = ca - mu
            rs = lax.rsqrt(jnp.mean(xc * xc, axis=-1, keepdims=True) + LN_EPS)
            l = xc * rs * lg + lb
            ab_ref[r0:r0 + rc, 0:c] = (l * _sigmoid(l)).astype(BF16)
            cbc = _conv_taps(extb, wb_ref, 3, hb - 2, r0, rc)
            ab_ref[r0:r0 + rc, c:2 * c] = (z_ref[r0:r0 + rc, 2 * c:3 * c] * cbc).astype(BF16)
            if (r0 + rc) % half == 0:
                rows = slice(r0 + rc - half, r0 + rc)
                o_ref[rows, :] = x_ref[rows, :] + jnp.dot(ab_ref[rows, :], wo_ref[...], preferred_element_type=F32)

    d = x.shape[1]
    row = pl.BlockSpec((ts, d), lambda i: (i, 0))
    return pl.pallas_call(
        body, name=name, grid=(s // ts,),
        in_specs=[pl.BlockSpec((ts, 5 * c), lambda i: (i, 0)),
                  pl.BlockSpec((hb, 5 * c), lambda i: (_prev_blk(i, ts, hb), 0)),
                  pl.BlockSpec((32, c), lambda i: (0, 0)),
                  pl.BlockSpec((1, c), lambda i: (0, 0)),
                  pl.BlockSpec((1, c), lambda i: (0, 0)),
                  pl.BlockSpec((8, c), lambda i: (0, 0)),
                  pl.BlockSpec((2 * c, d), lambda i: (0, 0)), row],
        out_specs=[pl.BlockSpec((ts, 2 * c), lambda i: (i, 0)),
                   pl.BlockSpec((ts, c), lambda i: (i, 0)), row],
        out_shape=[jax.ShapeDtypeStruct((s, 2 * c), BF16), jax.ShapeDtypeStruct((s, c), F32),
                   jax.ShapeDtypeStruct((s, d), F32)],
        scratch_shapes=[pltpu.VMEM((hb + ts, c), F32), pltpu.VMEM((hb + ts, c), F32),
                        pltpu.VMEM((8, hb + ts - 8, c), F32)],
        compiler_params=_params(("parallel",)),
    )(z, z, conv_a, ln_g, ln_b, conv_b, w_out, x)


def _mix0_bwd(z, ca, dab, conv_a, ln_g, ln_b, conv_b, *, name, ts=512, rc=32):
    s = z.shape[0]
    c = A_WIDTH
    hb = HALO_A
    ta = A_TAPS

    def body(z_ref, zp_ref, zn_ref, ca_ref, can_ref, d_ref, dn_ref, wa_ref, lg_ref, lb_ref, wb_ref,
             dz_ref, dwa_ref, dwb_ref, dlg_ref, dlb_ref, exta, extb, extdca, extdcb, shd):
        i = pl.program_id(0)
        keep_p = jnp.where(i > 0, 1.0, 0.0)
        keep_n = jnp.where(i < s // ts - 1, 1.0, 0.0)

        @pl.when(i == 0)
        def _():
            dwa_ref[...] = jnp.zeros_like(dwa_ref)
            dwb_ref[...] = jnp.zeros_like(dwb_ref)
            dlg_ref[...] = jnp.zeros_like(dlg_ref)
            dlb_ref[...] = jnp.zeros_like(dlb_ref)

        lg = lg_ref[...]
        lb = lb_ref[...]
        zp = zp_ref[...]
        exta[0:hb, :] = zp[:, 0:c] * _sigmoid(zp[:, c:2 * c]) * keep_p
        extb[0:hb, :] = zp[:, 3 * c:4 * c] * zp[:, 4 * c:5 * c] * keep_p
        exta[hb:hb + ts, :] = z_ref[:, 0:c] * _sigmoid(z_ref[:, c:2 * c])
        extb[hb:hb + ts, :] = z_ref[:, 3 * c:4 * c] * z_ref[:, 4 * c:5 * c]

        def ln_bwd(cav, dav):
            mu = jnp.mean(cav, axis=-1, keepdims=True)
            xc = cav - mu
            rs = lax.rsqrt(jnp.mean(xc * xc, axis=-1, keepdims=True) + LN_EPS)
            nv = xc * rs
            l = nv * lg + lb
            sg = _sigmoid(l)
            dl = dav * (sg * (1.0 + l * (1.0 - sg)))
            dnv = dl * lg
            dca = rs * (dnv - jnp.mean(dnv, axis=-1, keepdims=True)
                        - nv * jnp.mean(dnv * nv, axis=-1, keepdims=True))
            return dca, dl, nv

        dlg_acc = jnp.zeros((1, c), F32)
        dlb_acc = jnp.zeros((1, c), F32)
        for q in range(ts // rc):
            r0 = q * rc
            dca, dl, nv = ln_bwd(ca_ref[r0:r0 + rc, :], d_ref[r0:r0 + rc, 0:c])
            extdca[r0:r0 + rc, :] = dca
            dlg_acc = dlg_acc + jnp.sum(dl * nv, axis=0, keepdims=True)
            dlb_acc = dlb_acc + jnp.sum(dl, axis=0, keepdims=True)
            extdcb[r0:r0 + rc, :] = d_ref[r0:r0 + rc, c:2 * c] * z_ref[r0:r0 + rc, 2 * c:3 * c]
        dca_n, _, _ = ln_bwd(can_ref[...], dn_ref[:, 0:c])
        extdca[ts:ts + hb, :] = dca_n * keep_n
        extdcb[ts:ts + hb, :] = dn_ref[:, c:2 * c] * zn_ref[:, 2 * c:3 * c] * keep_n
        dlg_ref[...] += dlg_acc
        dlb_ref[...] += dlb_acc
        _shift_copies(extdca, shd, ts + hb - 8)

        for q in range(ts // rc):
            r0 = q * rc
            zr = z_ref[r0:r0 + rc, :]
            dga = None
            for k in range(ta):
                term = wa_ref[k:k + 1, :] * _shifted(extdca, shd, r0, rc, ta - 1 - k)
                dga = term if dga is None else dga + term
            sg = _sigmoid(zr[:, c:2 * c])
            dz_ref[r0:r0 + rc, 0:c] = (dga * sg).astype(BF16)
            dz_ref[r0:r0 + rc, c:2 * c] = (dga * zr[:, 0:c] * sg * (1.0 - sg)).astype(BF16)
            cbc = _conv_taps(extb, wb_ref, 3, hb - 2, r0, rc)
            dz_ref[r0:r0 + rc, 2 * c:3 * c] = (d_ref[r0:r0 + rc, c:2 * c] * cbc).astype(BF16)
            dcb = _conv_taps(extdcb, wb_ref, 3, 2, r0, rc, reverse=True)
            dz_ref[r0:r0 + rc, 3 * c:4 * c] = (dcb * zr[:, 4 * c:5 * c]).astype(BF16)
            dz_ref[r0:r0 + rc, 4 * c:5 * c] = (dcb * zr[:, 3 * c:4 * c]).astype(BF16)

        for k in range(ta):
            part = None
            for q in range(ts // rc):
                r0 = q * rc
                p = exta[hb + r0:hb + r0 + rc, :] * _shifted(extdca, shd, r0, rc, ta - 1 - k)
                for r in range(0, rc, 8):
                    part = p[r:r + 8, :] if part is None else part + p[r:r + 8, :]
            dwa_ref[k:k + 1, :] += jnp.sum(part, axis=0, keepdims=True)
        dcb_t = extdcb[0:ts, :]
        for k in range(3):
            dwb_ref[k:k + 1, :] += jnp.sum(dcb_t * extb[pl.ds(hb - 2 + k, ts), :], axis=0, keepdims=True)

    def tile(w):
        return pl.BlockSpec((ts, w), lambda i: (i, 0))

    def prev(w):
        return pl.BlockSpec((hb, w), lambda i: (_prev_blk(i, ts, hb), 0))

    def nxt(w):
        return pl.BlockSpec((hb, w), lambda i: (_next_blk(i, ts, hb, s), 0))

    def const(r, w):
        return pl.BlockSpec((r, w), lambda i: (0, 0))

    return pl.pallas_call(
        body, name=name, grid=(s // ts,),
        in_specs=[tile(5 * c), prev(5 * c), nxt(5 * c), tile(c), nxt(c), tile(2 * c), nxt(2 * c),
                  const(32, c), const(1, c), const(1, c), const(8, c)],
        out_specs=[tile(5 * c), const(32, c), const(8, c), const(1, c), const(1, c)],
        out_shape=[jax.ShapeDtypeStruct((s, 5 * c), BF16), jax.ShapeDtypeStruct((32, c), F32),
                   jax.ShapeDtypeStruct((8, c), F32), jax.ShapeDtypeStruct((1, c), F32),
                   jax.ShapeDtypeStruct((1, c), F32)],
        scratch_shapes=[pltpu.VMEM((hb + ts, c), F32), pltpu.VMEM((hb + ts, c), F32),
                        pltpu.VMEM((ts + hb, c), F32), pltpu.VMEM((ts + hb, c), F32),
                        pltpu.VMEM((8, ts + hb - 8, c), F32)],
        compiler_params=_params(("arbitrary",)),
    )(z, z, z, ca, ca, dab, dab, conv_a, ln_g, ln_b, conv_b)


def _ffn_up(x, g, w8, *, name, tm=2048):
    s, d = x.shape
    nb, c, _ = w8.shape
    tm = min(tm, s)

    def body(x_ref, g_ref, w_ref, h_ref, u_ref, hs_ref):
        @pl.when(pl.program_id(1) == 0)
        def _():
            hv = _rms_rows(x_ref[...], g_ref[...]).astype(BF16)
            hs_ref[...] = hv
            h_ref[...] = hv

        u_ref[...] = lax.dot_general(hs_ref[...], w_ref[...], _NT, preferred_element_type=F32).astype(BF16)

    return pl.pallas_call(
        body, name=name, grid=(s // tm, nb),
        in_specs=[pl.BlockSpec((tm, d), lambda i, k: (i, 0)),
                  pl.BlockSpec((1, d), lambda i, k: (0, 0)),
                  pl.BlockSpec((None, c, d), lambda i, k: (k, 0, 0))],
        out_specs=[pl.BlockSpec((tm, d), lambda i, k: (i, 0)),
                   pl.BlockSpec((None, tm, c), lambda i, k: (k, i, 0))],
        out_shape=[jax.ShapeDtypeStruct((s, d), BF16), jax.ShapeDtypeStruct((nb, s, c), BF16)],
        scratch_shapes=[pltpu.VMEM((tm, d), BF16)],
        compiler_params=_params(("parallel", "arbitrary")),
    )(x, g, w8)


def _ffn_mid_down(up8, wc8, wd4, x, head=None, *, name, ts=256, rc=32):
    nb, s, c = up8.shape
    d = x.shape[1]
    hb = HALO_S
    ts = min(ts, s)

    def body(*refs):
        u_ref, up_ref, wc_ref, wd_ref, x_ref = refs[:5]
        n_out = 4 if head is None else 7
        outs = refs[-1 - n_out:-1]
        act_ref, ug_ref, uv_ref = outs[-3:]
        ext = refs[-1]
        keep = jnp.where(pl.program_id(0) > 0, 1.0, 0.0)
        acc = x_ref[...]
        for j in range(N_PAIR):
            eg, ev = ext.at[2 * (j % 2)], ext.at[2 * (j % 2) + 1]
            eg[0:hb, :] = up_ref[j].astype(F32) * keep
            ev[0:hb, :] = up_ref[j + N_PAIR].astype(F32) * keep
            eg[hb:hb + ts, :] = u_ref[j].astype(F32)
            ev[hb:hb + ts, :] = u_ref[j + N_PAIR].astype(F32)
            for q in range(ts // rc):
                r0 = q * rc
                gg = _conv_taps(eg, wc_ref.at[j], 3, hb - 2, r0, rc)
                vv = _conv_taps(ev, wc_ref.at[j + N_PAIR], 3, hb - 2, r0, rc)
                ug_ref[j, r0:r0 + rc, :] = gg.astype(BF16)
                uv_ref[j, r0:r0 + rc, :] = vv.astype(BF16)
                act_ref[j, r0:r0 + rc, :] = (gg * _sigmoid(gg) * vv).astype(BF16)
            acc = acc + jnp.dot(act_ref[j], wd_ref[j], preferred_element_type=F32)
        if head is None:
            outs[0][...] = acc
            return
        g_ref, t_ref = refs[5], refs[6]
        loss_ref, dx_ref, dxb_ref, dg_ref = outs[:4]

        @pl.when(pl.program_id(0) == 0)
        def _():
            loss_ref[...] = jnp.zeros_like(loss_ref)
            dg_ref[...] = jnp.zeros_like(dg_ref)

        gv = g_ref[...]
        r = lax.rsqrt(jnp.mean(acc * acc, axis=-1, keepdims=True) + RMS_EPS)
        xh = acc * r
        err = xh * gv - t_ref[...]
        loss_ref[...] += 0.5 * jnp.sum(jnp.mean(err * err, axis=-1, keepdims=True), axis=0, keepdims=True)
        dy = err * (1.0 / d)
        dn = dy * gv
        dx = r * (dn - xh * jnp.mean(dn * xh, axis=-1, keepdims=True))
        dx_ref[...] = dx
        dxb_ref[...] = dx.astype(BF16)
        dg_ref[...] += jnp.sum(dy * xh, axis=0, keepdims=True)

    row = pl.BlockSpec((ts, d), lambda i: (i, 0))
    vec = pl.BlockSpec((1, d), lambda i: (0, 0))
    tile = pl.BlockSpec((N_PAIR, ts, c), lambda i: (0, i, 0))
    half = jax.ShapeDtypeStruct((N_PAIR, s, c), BF16)
    in_specs = [pl.BlockSpec((nb, ts, c), lambda i: (0, i, 0)),
                pl.BlockSpec((nb, hb, c), lambda i: (0, _prev_blk(i, ts, hb), 0)),
                pl.BlockSpec((nb, 8, c), lambda i: (0, 0, 0)),
                pl.BlockSpec((N_PAIR, c, d), lambda i: (0, 0, 0)), row]
    args = [up8, up8, wc8, wd4, x]
    if head is None:
        out_specs = [row, tile, tile, tile]
        out_shape = [jax.ShapeDtypeStruct((s, d), F32), half, half, half]
    else:
        in_specs += [vec, row]
        args += list(head)
        out_specs = [pl.BlockSpec((1, 1), lambda i: (0, 0)), row, row, vec, tile, tile, tile]
        out_shape = [jax.ShapeDtypeStruct((1, 1), F32), jax.ShapeDtypeStruct((s, d), F32),
                     jax.ShapeDtypeStruct((s, d), BF16), jax.ShapeDtypeStruct((1, d), F32), half, half, half]
    return pl.pallas_call(
        body, name=name, grid=(s // ts,), in_specs=in_specs, out_specs=out_specs, out_shape=out_shape,
        scratch_shapes=[pltpu.VMEM((4, hb + ts, c), F32)],
        compiler_params=_params(("parallel",) if head is None else ("arbitrary",)),
    )(*args)


def _ffn_dwdown(act4, db, *, name):
    npair, s, c = act4.shape
    d = db.shape[1]

    def body(a_ref, d_ref, o_ref):
        o_ref[...] = lax.dot_general(a_ref[...], d_ref[...], (((0,), (0,)), ((), ())),
                                     preferred_element_type=F32).astype(BF16)

    return pl.pallas_call(
        body, name=name, grid=(npair,),
        in_specs=[pl.BlockSpec((None, s, c), lambda j: (j, 0, 0)),
                  pl.BlockSpec((s, d), lambda j: (0, 0))],
        out_specs=pl.BlockSpec((None, c, d), lambda j: (j, 0, 0)),
        out_shape=jax.ShapeDtypeStruct((npair, c, d), BF16),
        compiler_params=_params(("parallel",)),
    )(act4, db)


def _ffn_dact(db, wd4, *, name, tm=2048):
    s, d = db.shape
    npair, c, _ = wd4.shape
    tm = min(tm, s)

    def body(d_ref, w_ref, o_ref):
        o_ref[...] = lax.dot_general(d_ref[...], w_ref[...], _NT, preferred_element_type=F32).astype(BF16)

    return pl.pallas_call(
        body, name=name, grid=(s // tm, npair),
        in_specs=[pl.BlockSpec((tm, d), lambda i, j: (i, 0)),
                  pl.BlockSpec((None, c, d), lambda i, j: (j, 0, 0))],
        out_specs=pl.BlockSpec((None, tm, c), lambda i, j: (j, i, 0)),
        out_shape=jax.ShapeDtypeStruct((npair, s, c), BF16),
        compiler_params=_params(("parallel", "parallel")),
    )(db, wd4)


def _ffn_midbwd(up8, ug4, uv4, dact4, wc8, *, name, ts=1024, rc=32):
    nb, s, c = up8.shape
    hb = HALO_S
    ts = min(ts, s)
    n_i = s // ts

    def body(pg_ref, pv_ref, ug_ref, ugn_ref, uv_ref, uvn_ref, d_ref, dn_ref, wg_ref, wv_ref,
             dg_ref, dv_ref, dwg_ref, dwv_ref, extdg, extdv):
        i = pl.program_id(1)
        keep_n = jnp.where(i < n_i - 1, 1.0, 0.0)

        @pl.when(i == 0)
        def _():
            dwg_ref[...] = jnp.zeros_like(dwg_ref)
            dwv_ref[...] = jnp.zeros_like(dwv_ref)

        def du_rows(r0, rows, gg, vv, dav):
            sg = _sigmoid(gg)
            extdg[r0:r0 + rows, :] = dav * vv * (sg * (1.0 + gg * (1.0 - sg)))
            extdv[r0:r0 + rows, :] = dav * (gg * sg)

        for q in range(ts // rc):
            rows = slice(q * rc, q * rc + rc)
            du_rows(q * rc, rc, ug_ref[rows, :].astype(F32), uv_ref[rows, :].astype(F32),
                    d_ref[rows, :].astype(F32))
        du_rows(ts, 8, ugn_ref[0:8, :].astype(F32), uvn_ref[0:8, :].astype(F32),
                dn_ref[0:8, :].astype(F32) * keep_n)

        def fold(p):
            acc = p[0:8, :]
            for r in range(8, rc, 8):
                acc = acc + p[r:r + 8, :]
            return acc

        for extd, p_ref, w_ref, out_ref, dw_ref in ((extdg, pg_ref, wg_ref, dg_ref, dwg_ref),
                                                    (extdv, pv_ref, wv_ref, dv_ref, dwv_ref)):
            part = [None, None, None]
            for q in range(ts // rc):
                r0 = q * rc
                pre = p_ref[r0:r0 + rc, :].astype(F32)
                dup = None
                for k in range(3):
                    sh = extd[pl.ds(r0 + 2 - k, rc), :]
                    term = w_ref[k:k + 1, :] * sh
                    dup = term if dup is None else dup + term
                    prod = fold(pre * sh)
                    part[k] = prod if part[k] is None else part[k] + prod
                out_ref[r0:r0 + rc, :] = dup.astype(BF16)
            for k in range(3):
                dw_ref[k:k + 1, :] += jnp.sum(part[k], axis=0, keepdims=True)

    def blk(off, nxt):
        if nxt:
            return pl.BlockSpec((None, hb, c), lambda j, i: (j + off, _next_blk(i, ts, hb, s), 0))
        return pl.BlockSpec((None, ts, c), lambda j, i: (j + off, i, 0))

    def taps(off):
        return pl.BlockSpec((None, 8, c), lambda j, i: (j + off, 0, 0))

    tile = blk(0, False)
    acc = pl.BlockSpec((None, 8, c), lambda j, i: (j, 0, 0))
    return pl.pallas_call(
        body, name=name, grid=(N_PAIR, n_i),
        in_specs=[tile, blk(N_PAIR, False), tile, blk(0, True), tile, blk(0, True), tile, blk(0, True),
                  taps(0), taps(N_PAIR)],
        out_specs=[tile, tile, acc, acc],
        out_shape=[jax.ShapeDtypeStruct((N_PAIR, s, c), BF16), jax.ShapeDtypeStruct((N_PAIR, s, c), BF16),
                   jax.ShapeDtypeStruct((N_PAIR, 8, c), F32), jax.ShapeDtypeStruct((N_PAIR, 8, c), F32)],
        scratch_shapes=[pltpu.VMEM((ts + 8, c), F32), pltpu.VMEM((ts + 8, c), F32)],
        compiler_params=_params(("parallel", "arbitrary")),
    )(up8, up8, ug4, ug4, uv4, uv4, dact4, dact4, wc8, wc8)


def _ffn_dh(dupg, dupv, w8, x, g, dres, dep, *, name, tm=512):
    npair, s, c = dupg.shape
    d = x.shape[1]
    tm = min(tm, s)

    def body(dg_ref, dv_ref, w_ref, x_ref, g_ref, dres_ref, dep_ref, dx_ref, dxb_ref, dgain_ref):
        @pl.when(pl.program_id(0) == 0)
        def _():
            dgain_ref[...] = jnp.zeros_like(dgain_ref)

        dh = None
        for j in range(npair):
            for src, k in ((dg_ref, j), (dv_ref, j + npair)):
                part = jnp.dot(src[j], w_ref[k], preferred_element_type=F32)
                dh = part if dh is None else dh + part
        dx, dgr = _rms_bwd_rows(x_ref[...], g_ref[...], dh)
        dx = dres_ref[...] + dx
        dx_ref[...] = dx
        dxb_ref[...] = dx.astype(BF16)
        dgain_ref[...] += jnp.sum(dgr, axis=0, keepdims=True)

    row = pl.BlockSpec((tm, d), lambda i: (i, 0))
    vec = pl.BlockSpec((1, d), lambda i: (0, 0))
    dup = pl.BlockSpec((npair, tm, c), lambda i: (0, i, 0))
    return pl.pallas_call(
        body, name=name, grid=(s // tm,),
        in_specs=[dup, dup, pl.BlockSpec((2 * npair, c, d), lambda i: (0, 0, 0)), row, vec, row,
                  pl.BlockSpec(memory_space=pl.ANY)],
        out_specs=[row, row, vec],
        out_shape=[jax.ShapeDtypeStruct((s, d), F32), jax.ShapeDtypeStruct((s, d), BF16),
                   jax.ShapeDtypeStruct((1, d), F32)],
        compiler_params=_params(("arbitrary",)),
    )(dupg, dupv, w8, x, g, dres, dep)


def _ffn_dwup(h, dupg, dupv, *, name, tm=1024):
    npair, s, c = dupg.shape
    d = h.shape[1]

    def body(h_ref, dg_ref, dv_ref, o_ref):
        k = pl.program_id(1)

        @pl.when(k < npair)
        def _():
            o_ref[...] = lax.dot_general(dg_ref[...], h_ref[...], _TN, preferred_element_type=F32).astype(BF16)

        @pl.when(k >= npair)
        def _():
            o_ref[...] = lax.dot_general(dv_ref[...], h_ref[...], _TN, preferred_element_type=F32).astype(BF16)

    return pl.pallas_call(
        body, name=name, grid=(d // tm, 2 * npair),
        in_specs=[pl.BlockSpec((s, tm), lambda m, k: (0, m)),
                  pl.BlockSpec((None, s, c), lambda m, k: (jnp.minimum(k, npair - 1), 0, 0)),
                  pl.BlockSpec((None, s, c), lambda m, k: (jnp.maximum(k - npair, 0), 0, 0))],
        out_specs=pl.BlockSpec((None, c, tm), lambda m, k: (k, 0, m)),
        out_shape=jax.ShapeDtypeStruct((2 * npair, c, d), BF16),
        compiler_params=_params(("parallel", "arbitrary")),
    )(h, dupg, dupv)


def _pool_counts(i, ts, rows, window):
    t = lax.broadcasted_iota(jnp.int32, (rows, 1), 0) + i * ts + 1
    return jnp.minimum(t, window).astype(F32)


def _pool_fwd(x, g, w_pool, scale, *, name, ts=512):
    s, d = x.shape
    hb = HALO_A
    pg = POOL_GROUP
    ts = min(ts, s)

    def body(x_ref, xp_ref, g_ref, w_ref, s_ref, o_ref, p_ref, yu_ref, ext, sa, sb):
        i = pl.program_id(0)
        keep = jnp.where(i > 0, 1.0, 0.0)
        gv = g_ref[...]
        ext[0:hb, :] = _rms_rows(xp_ref[...], gv) * keep
        ext[hb:hb + ts, :] = _rms_rows(x_ref[...], gv)
        rows = hb + ts - 8
        sa[0:8, :] = jnp.zeros((8, d), F32)
        sb[0:8, :] = jnp.zeros((8, d), F32)
        for gi, w in enumerate(POOL_WINDOWS):
            cols = slice(gi * pg, (gi + 1) * pg)
            cur, nxt, k = ext, sa, 1
            while k < w:
                nxt[8:8 + rows, cols] = cur[8:8 + rows, cols] + cur[pl.ds(8 - k, rows), cols]
                cur, nxt, k = nxt, (sb if nxt is sa else sa), 2 * k
            h = ext[hb:hb + ts, cols]
            pv = (cur[hb:hb + ts, cols] / _pool_counts(i, ts, ts, w) - h).astype(BF16)
            p_ref[:, cols] = pv
            yu = jnp.dot(pv, w_ref[gi], preferred_element_type=F32)
            yu_ref[:, cols] = yu.astype(BF16)
            o_ref[:, cols] = x_ref[:, cols] + yu * s_ref[:, cols]

    row = pl.BlockSpec((ts, d), lambda i: (i, 0))
    vec = pl.BlockSpec((1, d), lambda i: (0, 0))
    return pl.pallas_call(
        body, name=name, grid=(s // ts,),
        in_specs=[row, pl.BlockSpec((hb, d), lambda i: (_prev_blk(i, ts, hb), 0)), vec,
                  pl.BlockSpec((d // pg, pg, pg), lambda i: (0, 0, 0)), vec],
        out_specs=[row, row, row],
        out_shape=[jax.ShapeDtypeStruct((s, d), F32), jax.ShapeDtypeStruct((s, d), BF16),
                   jax.ShapeDtypeStruct((s, d), BF16)],
        scratch_shapes=[pltpu.VMEM((hb + ts, d), F32)] * 3,
        compiler_params=_params(("parallel",)),
    )(x, x, g, w_pool, scale)


def _pool_dw(p, dyc, *, name):
    s, d = p.shape
    pg = POOL_GROUP

    def body(p_ref, d_ref, o_ref):
        o_ref[...] = lax.dot_general(p_ref[...], d_ref[...], (((0,), (0,)), ((), ())),
                                     preferred_element_type=F32).astype(BF16)

    blk = pl.BlockSpec((s, pg), lambda gi: (0, gi))
    return pl.pallas_call(
        body, name=name, grid=(d // pg,),
        in_specs=[blk, blk], out_specs=pl.BlockSpec((None, pg, pg), lambda gi: (gi, 0, 0)),
        out_shape=jax.ShapeDtypeStruct((d // pg, pg, pg), BF16),
        compiler_params=_params(("parallel",)),
    )(p, dyc)


def _pool_bwd(dres, w_pool, scale, yu, x, g, *, name, ts=512):
    s, d = x.shape
    hb = HALO_A
    pg = POOL_GROUP
    ts = min(ts, s)
    n_i = s // ts

    def body(dres_ref, dresn_ref, w_ref, s_ref, yu_ref, x_ref, g_ref,
             dx_ref, dxb_ref, dg_ref, dyc_ref, ds_ref, ext, dh, sa, sb, dp_s):
        i = pl.program_id(0)
        keep_n = jnp.where(i < n_i - 1, 1.0, 0.0)

        @pl.when(i == 0)
        def _():
            dg_ref[...] = jnp.zeros_like(dg_ref)
            ds_ref[...] = jnp.zeros_like(ds_ref)

        ds_ref[...] += jnp.sum(dres_ref[...] * yu_ref[...].astype(F32), axis=0, keepdims=True)
        for gi, w in enumerate(POOL_WINDOWS):
            cols = slice(gi * pg, (gi + 1) * pg)
            dyc = (dres_ref[:, cols] * s_ref[:, cols]).astype(BF16)
            dyc_ref[:, cols] = dyc
            dp = lax.dot_general(dyc, w_ref[gi], _NT, preferred_element_type=F32)
            dp_s[:, cols] = dp
            ext[0:ts, cols] = dp / _pool_counts(i, ts, ts, w)
            dycn = (dresn_ref[:, cols] * s_ref[:, cols]).astype(BF16)
            dpn = lax.dot_general(dycn, w_ref[gi], _NT, preferred_element_type=F32)
            ext[ts:ts + hb, cols] = dpn / _pool_counts(i + 1, ts, hb, w) * keep_n
        rows = ts + hb - 8
        sa[rows:rows + 8, :] = jnp.zeros((8, d), F32)
        sb[rows:rows + 8, :] = jnp.zeros((8, d), F32)
        for gi, w in enumerate(POOL_WINDOWS):
            cols = slice(gi * pg, (gi + 1) * pg)
            cur, nxt, k = ext, sa, 1
            while k < w:
                nxt[0:rows, cols] = cur[0:rows, cols] + cur[pl.ds(k, rows), cols]
                cur, nxt, k = nxt, (sb if nxt is sa else sa), 2 * k
            dh[:, cols] = cur[0:ts, cols] - dp_s[:, cols]
        dx, dgr = _rms_bwd_rows(x_ref[...], g_ref[...], dh[...])
        dx = dres_ref[...] + dx
        dx_ref[...] = dx
        dxb_ref[...] = dx.astype(BF16)
        dg_ref[...] += jnp.sum(dgr, axis=0, keepdims=True)

    row = pl.BlockSpec((ts, d), lambda i: (i, 0))
    vec = pl.BlockSpec((1, d), lambda i: (0, 0))
    return pl.pallas_call(
        body, name=name, grid=(n_i,),
        in_specs=[row, pl.BlockSpec((hb, d), lambda i: (_next_blk(i, ts, hb, s), 0)),
                  pl.BlockSpec((d // pg, pg, pg), lambda i: (0, 0, 0)), vec, row, row, vec],
        out_specs=[row, row, vec, row, vec],
        out_shape=[jax.ShapeDtypeStruct((s, d), F32), jax.ShapeDtypeStruct((s, d), BF16),
                   jax.ShapeDtypeStruct((1, d), F32), jax.ShapeDtypeStruct((s, d), BF16),
                   jax.ShapeDtypeStruct((1, d), F32)],
        scratch_shapes=[pltpu.VMEM((ts + hb, d), F32), pltpu.VMEM((ts, d), F32),
                        pltpu.VMEM((ts + hb, d), F32), pltpu.VMEM((ts + hb, d), F32),
                        pltpu.VMEM((ts, d), F32)],
        compiler_params=_params(("arbitrary",)),
    )(dres, dres, w_pool, scale, yu, x, g)


def _pad_rows(w, rows):
    pad = [(0, 0)] * (w.ndim - 2) + [(0, rows - w.shape[-2]), (0, 0)]
    return jnp.pad(w, pad)


def _ffn_layer_fwd(x, nf, w8, wc8, wd4, tag, head=None):
    h, up8 = _ffn_up(x, nf, w8, name=f"ffn{tag}_up")
    res = _ffn_mid_down(up8, wc8, wd4, x, head, name=f"ffn{tag}_mid_down")
    act4, ug4, uv4 = res[-3:]
    out = res[0] if head is None else tuple(res[:4])
    return out, (h, up8, ug4, uv4, act4)


def _ffn_layer_bwd(d, db, x, nf, w8, wc8, wd4, saved, tag, grads_ready):
    h, up8, ug4, uv4, act4 = saved
    dact4 = _ffn_dact(db, wd4, name=f"ffn{tag}_dact")
    dwd4 = _ffn_dwdown(act4, db, name=f"ffn{tag}_dwdown")
    dupg, dupv, dwg, dwv = _ffn_midbwd(up8, ug4, uv4, dact4, wc8, name=f"ffn{tag}_midbwd")
    dw8 = _ffn_dwup(h, dupg, dupv, name=f"ffn{tag}_dwup")
    sent = grads_ready(f"ffn{tag}", {f"w_up{tag}": dw8, f"w_down{tag}": dwd4})
    dx, dxb, dnf = _ffn_dh(dupg, dupv, w8, x, nf, d, sent, name=f"ffn{tag}_dh")
    dwc8 = jnp.concatenate([dwg, dwv], axis=0)[:, :3]
    return dx, dxb, dnf, dwc8


def _local_step(x, target, weights, grads_ready):
    w0 = weights("mix0", x)
    wa = _pad_rows(w0["conv_a"], 32)
    wb = _pad_rows(w0["conv_b"], 8)
    wc = [_pad_rows(w0["conv_ffn"][l], 8) for l in range(2)]
    h0, z = _rms_mm(x, w0["norm_mix_even"], w0["w_in_t"], name="mix0_in", out_dtype=F32, tn=2560)
    ab, ca, x1 = _mix0_fwd(z, wa, w0["ln_a_g"], w0["ln_a_b"], wb, w0["w_out"], x, name="mix0_mid")
    w1 = weights("ffn0", x1)
    x2, ffn0 = _ffn_layer_fwd(x1, w0["norm_ffn"][0:1], w1["w_up"], wc[0], w1["w_down"], 0)
    w2 = weights("ffn1", x2)
    x3, p, yu = _pool_fwd(x2, w0["norm_mix_odd"], w2["w_pool"], w0["pool_scale"], name="pool_fwd")
    (loss, d4, d4b, g_norm_final), ffn1 = _ffn_layer_fwd(
        x3, w0["norm_ffn"][1:2], w2["w_up"], wc[1], w2["w_down"], 1, head=(w0["norm_final"], target))

    d3, d3b, g_nf1, g_wc1 = _ffn_layer_bwd(
        d4, d4b, x3, w0["norm_ffn"][1:2], w2["w_up"], wc[1], w2["w_down"], ffn1, 1, grads_ready)
    d2, d2b, g_nmo, dyc, g_scale = _pool_bwd(d3, w2["w_pool"], w0["pool_scale"], yu, x2, w0["norm_mix_odd"],
                                             name="pool_bwd")
    g_pool = _pool_dw(p, dyc, name="pool_dw")
    d1, d1b, g_nf0, g_wc0 = _ffn_layer_bwd(
        d2, d2b, x1, w0["norm_ffn"][0:1], w1["w_up"], wc[0], w1["w_down"], ffn0, 0, grads_ready)
    dab = _mm(d1b, w0["w_out"], tb=True, name="mix0_dab", tm=1024, tn=1024)
    g_out = _mm(ab, d1b, ta=True, out_dtype=BF16, name="mix0_dwout", tm=1024, tn=512)
    dz, g_wa, g_wb, g_lg, g_lb = _mix0_bwd(z, ca, dab, wa, w0["ln_a_g"], w0["ln_a_b"], wb, name="mix0_midbwd")
    g_in = _mm(dz, h0, ta=True, out_dtype=BF16, name="mix0_dwin", tm=512, tn=1024)
    sent_mix = grads_ready("mix0", {"w_in": g_in, "w_out": g_out, "w_pool": g_pool})
    dx, g_nme = _mm_rms_bwd(dz, w0["w_in_t"], x, w0["norm_mix_even"], d1, sent_mix, name="mix0_dh")

    small = {
        "norm_mix_even": g_nme, "conv_a": g_wa[:A_TAPS], "ln_a_g": g_lg, "ln_a_b": g_lb, "conv_b": g_wb[:3],
        "norm_mix_odd": g_nmo, "pool_scale": g_scale, "norm_ffn": jnp.concatenate([g_nf0, g_nf1], axis=0),
        "conv_ffn": [g_wc0, g_wc1], "norm_final": g_norm_final,
    }
    return loss[0, 0], dx, small


def _my_pos():
    return lax.axis_index("x"), lax.axis_index("y"), lax.axis_index("c")


def _flip(pos, r):
    x, y, c = pos
    return (1 - x if r & 4 else x, 1 - y if r & 2 else y, 1 - c if r & 1 else c)


def _dev_index(pos):
    return 4 * pos[0] + 2 * pos[1] + pos[2]


_HBM = pl.BlockSpec(memory_space=pltpu.HBM)
_SEM = pl.BlockSpec(memory_space=pltpu.SEMAPHORE)
_EFFECT = pltpu.SideEffectType.DATAFLOW_SIDE_EFFECTING


def _exchange_copy(ins, lands, send_sems, recv_sems, scatter, pos, a, r, receiving):
    me = _dev_index(pos)
    peer = _flip(pos, r)
    dest = _dev_index(pos) if receiving else _dev_index(peer)
    src = ins[a].at[dest] if scatter[a] else ins[a]
    slot = _dev_index(peer) if receiving else me
    return pltpu.make_async_remote_copy(
        src_ref=src, dst_ref=lands[a].at[slot], send_sem=send_sems.at[a * (N_DEV - 1) + r - 1],
        recv_sem=recv_sems.at[a * (N_DEV - 1) + r - 1],
        device_id=peer, device_id_type=pl.DeviceIdType.MESH)


ALL_PEERS = tuple(range(1, N_DEV))
CHIP_PEERS = (1, 2, 4, 6)
FORWARDED = (2, 4, 6)


def _exchange_start(arrays, scatter, after, *, name, peers=ALL_PEERS):
    n = len(arrays)
    me = _dev_index(_my_pos())
    lands = []
    for arr, sc in zip(arrays, scatter):
        own = lax.dynamic_index_in_dim(arr, me, 0, keepdims=True) if sc else arr[None]
        shape = arr.shape if sc else (N_DEV,) + arr.shape
        lands.append(lax.dynamic_update_slice(lax.empty(shape, arr.dtype), own, (me,) + (0,) * (len(shape) - 1)))

    def body(*refs):
        ins, lnd = refs[:n], refs[n:2 * n]
        send_sems, recv_sems = refs[2 * n + 1], refs[2 * n + 2]
        token = refs[-1]
        pos = _my_pos()
        for a in range(n):
            for r in peers:
                _exchange_copy(ins, lnd, send_sems, recv_sems, scatter, pos, a, r, receiving=False).start()
        token[...] = jnp.zeros_like(token)

    bufs = [pltpu.with_memory_space_constraint(t, pltpu.HBM) for t in list(arrays) + lands]
    sems = pltpu.SemaphoreType.DMA((n * (N_DEV - 1),))
    res = pl.pallas_call(
        body, name=name,
        out_shape=(sems, sems, *[pltpu.HBM(t.shape, t.dtype) for t in bufs], jax.ShapeDtypeStruct((8, 128), F32)),
        in_specs=[_HBM] * (2 * n) + [pl.BlockSpec(memory_space=pl.ANY)],
        out_specs=(_SEM, _SEM, *[_HBM] * (2 * n), pl.BlockSpec(memory_space=pltpu.VMEM)),
        input_output_aliases={i: 2 + i for i in range(2 * n)},
        compiler_params=pltpu.CompilerParams(has_side_effects=_EFFECT),
    )(*bufs, after)
    return res[0], res[1], list(res[2:2 + n]), list(res[2 + n:2 + 2 * n]), res[-1]


def _exchange_wait(started, scatter, after, *, name, peers=ALL_PEERS):
    send_sems, recv_sems, arrays, lands, _ = started
    n = len(arrays)

    def body(*refs):
        ins, lnd = refs[:n], refs[n:2 * n]
        send_sems, recv_sems = refs[2 * n], refs[2 * n + 1]
        pos = _my_pos()
        for a in range(n):
            for r in peers:
                _exchange_copy(ins, lnd, send_sems, recv_sems, scatter, pos, a, r, receiving=False).wait_send()
                _exchange_copy(ins, lnd, send_sems, recv_sems, scatter, pos, a, r, receiving=True).wait_recv()

    bufs = list(arrays) + list(lands)
    after = list(after) if isinstance(after, (list, tuple)) else [after]
    res = pl.pallas_call(
        body, name=name,
        out_shape=tuple(pltpu.HBM(t.shape, t.dtype) for t in bufs),
        in_specs=[_HBM] * (2 * n) + [_SEM, _SEM] + [pl.BlockSpec(memory_space=pl.ANY)] * len(after),
        out_specs=tuple([_HBM] * (2 * n)),
        input_output_aliases={i: i for i in range(2 * n)},
        compiler_params=pltpu.CompilerParams(has_side_effects=_EFFECT),
    )(*bufs, send_sems, recv_sems, *after)
    return list(res[n:])


def _forward_copy(lands, send_sems, recv_sems, pos, a, q, receiving):
    slot = _dev_index(_flip(pos, q ^ 1 if receiving else q))
    idx = a * len(FORWARDED) + FORWARDED.index(q)
    return pltpu.make_async_remote_copy(
        src_ref=lands[a].at[slot], dst_ref=lands[a].at[slot], send_sem=send_sems.at[idx], recv_sem=recv_sems.at[idx],
        device_id=_flip(pos, 1), device_id_type=pl.DeviceIdType.MESH)


def _forward_start(lands, after, *, name):
    n = len(lands)

    def body(*refs):
        lnd = refs[:n]
        send_sems, recv_sems = refs[n + 1], refs[n + 2]
        token = refs[-1]
        pos = _my_pos()
        for a in range(n):
            for q in FORWARDED:
                _forward_copy(lnd, send_sems, recv_sems, pos, a, q, receiving=False).start()
        token[...] = jnp.zeros_like(token)

    sems = pltpu.SemaphoreType.DMA((n * len(FORWARDED),))
    res = pl.pallas_call(
        body, name=name,
        out_shape=(sems, sems, *[pltpu.HBM(t.shape, t.dtype) for t in lands], jax.ShapeDtypeStruct((8, 128), F32)),
        in_specs=[_HBM] * n + [pl.BlockSpec(memory_space=pl.ANY)],
        out_specs=(_SEM, _SEM, *[_HBM] * n, pl.BlockSpec(memory_space=pltpu.VMEM)),
        input_output_aliases={i: 2 + i for i in range(n)},
        compiler_params=pltpu.CompilerParams(has_side_effects=_EFFECT),
    )(*lands, after)
    return res[0], res[1], list(res[2:2 + n]), res[-1]


def _forward_wait(forwarded, after, *, name):
    send_sems, recv_sems, lands, _ = forwarded
    n = len(lands)

    def body(*refs):
        lnd = refs[:n]
        send_sems, recv_sems = refs[n], refs[n + 1]
        pos = _my_pos()
        for a in range(n):
            for q in FORWARDED:
                _forward_copy(lnd, send_sems, recv_sems, pos, a, q, receiving=False).wait_send()
                _forward_copy(lnd, send_sems, recv_sems, pos, a, q, receiving=True).wait_recv()

    res = pl.pallas_call(
        body, name=name,
        out_shape=tuple(pltpu.HBM(t.shape, t.dtype) for t in lands),
        in_specs=[_HBM] * n + [_SEM, _SEM, pl.BlockSpec(memory_space=pl.ANY)],
        out_specs=tuple([_HBM] * n),
        input_output_aliases={i: i for i in range(n)},
        compiler_params=pltpu.CompilerParams(has_side_effects=_EFFECT),
    )(*lands, send_sems, recv_sems, after)
    return list(res)


def _adamw_update(p_ref, w_ref, m_ref, v_ref, g_ref, d_ref, mo_ref, vo_ref):
    g = p_ref[0].astype(F32)
    for k in range(1, N_DEV):
        g = g + p_ref[k].astype(F32)
    mn = ADAM_B1 * m_ref[...] + (1.0 - ADAM_B1) * g
    vn = ADAM_B2 * v_ref[...] + (1.0 - ADAM_B2) * (g * g)
    m_hat = mn / (1.0 - ADAM_B1 ** ADAM_STEP)
    v_hat = vn / (1.0 - ADAM_B2 ** ADAM_STEP)
    g_ref[...] = g
    d_ref[...] = -ADAM_LR * (m_hat / (jnp.sqrt(v_hat) + ADAM_EPS) + ADAM_WD * w_ref[...])
    mo_ref[...] = mn
    vo_ref[...] = vn


def _adamw_small(parts, ws, ms, vs, *, name):
    n = len(ws)

    def body(*refs):
        ins, outs = refs[:4 * n], refs[4 * n:]
        for j in range(n):
            _adamw_update(ins[j], ins[n + j], ins[2 * n + j], ins[3 * n + j], *outs[4 * j:4 * j + 4])

    return pl.pallas_call(
        body, name=name,
        out_shape=[jax.ShapeDtypeStruct(w.shape, F32) for w in ws for _ in range(4)],
    )(*parts, *ws, *ms, *vs)


def _adamw(parts, w, m, v, *, name, tr):
    nl, r, c = w.shape
    assert len(parts) == nl and r % tr == 0
    n_i = r // tr

    def body(*refs):
        p_refs = refs[:nl]
        w_ref, m_ref, v_ref = refs[nl:nl + 3]

        def update(p_ref):
            _adamw_update(p_ref, w_ref, m_ref, v_ref, *refs[nl + 3:])

        if nl == 1:
            update(p_refs[0])
        else:
            for layer in range(nl):
                pl.when(pl.program_id(0) == layer)(lambda p_ref=p_refs[layer]: update(p_ref))

    def parts_spec(layer):
        def index(l, i):
            return (0, jnp.where(l < layer, 0, jnp.where(l > layer, n_i - 1, i)), 0)
        return pl.BlockSpec((N_DEV, tr, c), index)

    blk = pl.BlockSpec((None, tr, c), lambda l, i: (l, i, 0))
    return pl.pallas_call(
        body, name=name, grid=(nl, n_i),
        in_specs=[parts_spec(layer) for layer in range(nl)] + [blk, blk, blk],
        out_specs=[blk] * 4, out_shape=[jax.ShapeDtypeStruct((nl, r, c), F32)] * 4,
        compiler_params=_params(("arbitrary", "arbitrary")),
    )(*parts, w, m, v)


def _pack(parts, lead=()):
    flat = jnp.concatenate([p.reshape(lead + (-1,)) for p in parts], axis=-1)
    n = flat.shape[-1]
    rows = -(-n // (8 * 128)) * 8
    flat = jnp.pad(flat, [(0, 0)] * len(lead) + [(0, rows * 128 - n)])
    return flat.reshape(lead + (rows, 128))


def _to_dev_major(g, axis):
    shp = g.shape
    g = g.reshape(shp[:axis] + (N_DEV, shp[axis] // N_DEV) + shp[axis + 1:])
    return jnp.moveaxis(g, axis, 0)


def _from_dev_major(g, axis):
    g = jnp.moveaxis(g, 0, axis)
    shp = g.shape
    return g.reshape(shp[:axis] + (shp[axis] * shp[axis + 1],) + shp[axis + 2:])


SMALL_SHARDED = ("conv_a", "conv_b", "norm_mix_odd", "pool_scale", "conv_ffn_w")
SMALL_REPLICATED = ("norm_mix_even", "ln_a_g", "ln_a_b", "norm_ffn", "norm_final")
BIG = {"w_in": ("w_in", 0, 320), "w_out": ("w_out", 0, 128), "w_pool": ("w_pool", 0, 128),
       "w_up0": ("w_up", 0, 352), "w_up1": ("w_up", 1, 352),
       "w_down0": ("w_down", 0, 352), "w_down1": ("w_down", 1, 352)}
COLUMN_SHARDED = ("w_in", "w_up")


def kernel(x, norm_mix_even, w_in, conv_a, ln_a_g, ln_a_b, conv_b, w_out, norm_mix_odd, w_pool, pool_scale, norm_ffn, w_up, conv_ffn_w, w_down, norm_final, loss_target, m_norm_mix_even, m_w_in, m_conv_a, m_ln_a_g, m_ln_a_b, m_conv_b, m_w_out, m_norm_mix_odd, m_w_pool, m_pool_scale, m_norm_ffn, m_w_up, m_conv_ffn_w, m_w_down, m_norm_final, v_norm_mix_even, v_w_in, v_conv_a, v_ln_a_g, v_ln_a_b, v_conv_b, v_w_out, v_norm_mix_odd, v_w_pool, v_pool_scale, v_norm_ffn, v_w_up, v_conv_ffn_w, v_w_down, v_norm_final):
    names = ("norm_mix_even", "w_in", "conv_a", "ln_a_g", "ln_a_b", "conv_b", "w_out", "norm_mix_odd", "w_pool",
             "pool_scale", "norm_ffn", "w_up", "conv_ffn_w", "w_down", "norm_final")
    wts = dict(zip(names, (norm_mix_even, w_in, conv_a, ln_a_g, ln_a_b, conv_b, w_out, norm_mix_odd, w_pool,
                           pool_scale, norm_ffn, w_up, conv_ffn_w, w_down, norm_final)))
    mom = dict(zip(names, (m_norm_mix_even, m_w_in, m_conv_a, m_ln_a_g, m_ln_a_b, m_conv_b, m_w_out, m_norm_mix_odd,
                           m_w_pool, m_pool_scale, m_norm_ffn, m_w_up, m_conv_ffn_w, m_w_down, m_norm_final)))
    var = dict(zip(names, (v_norm_mix_even, v_w_in, v_conv_a, v_ln_a_g, v_ln_a_b, v_conv_b, v_w_out, v_norm_mix_odd,
                           v_w_pool, v_pool_scale, v_norm_ffn, v_w_up, v_conv_ffn_w, v_w_down, v_norm_final)))
    d = x.shape[-1]

    def shard3d(t, param):
        a = t[param]
        if param in COLUMN_SHARDED:
            return jnp.swapaxes(a, 1, 2)
        return a.reshape(a.shape[0], -1, a.shape[-1])

    def unshard3d(a, param):
        if param in COLUMN_SHARDED:
            return jnp.swapaxes(a, 1, 2)
        return a.reshape(wts[param].shape)

    def shard2d(t, key):
        param, layer, _ = BIG[key]
        return shard3d(t, param)[layer]

    small_w = _pack([wts[k] for k in SMALL_SHARDED])
    bf = {k: shard2d(wts, k).astype(BF16) for k in BIG}
    gather_groups = {"mix0": ("w_in", "w_out", "small"), "ffn0": ("w_up0", "w_down0"),
                     "ffn1": ("w_pool", "w_up1", "w_down1")}
    order = list(gather_groups)
    started = {}

    def start_gather(grp, after):
        arrs = [small_w if k == "small" else bf[k] for k in gather_groups[grp]]
        started[grp] = _exchange_start(arrs, [False] * len(arrs), after, name=f"gather_{grp}_start", peers=CHIP_PEERS)

    start_gather(order[0], small_w)

    def weights(grp, after):
        keys = gather_groups[grp]
        lands = _exchange_wait(started[grp], [False] * len(keys), after, name=f"gather_{grp}_wait", peers=CHIP_PEERS)
        forwarded = _forward_start(lands, small_w, name=f"gather_{grp}_forward")
        if grp != order[-1]:
            start_gather(order[order.index(grp) + 1], forwarded[-1])
            behind = started[order[order.index(grp) + 1]][-1]
        else:
            behind = forwarded[-1]
        gw = dict(zip(keys, _forward_wait(forwarded, behind, name=f"gather_{grp}_forward_wait")))
        if grp == "ffn0":
            return {"w_up": gw["w_up0"], "w_down": gw["w_down0"].reshape(N_PAIR, -1, d)}
        if grp == "ffn1":
            return {"w_up": gw["w_up1"], "w_down": gw["w_down1"].reshape(N_PAIR, -1, d),
                    "w_pool": _from_dev_major(gw["w_pool"].reshape(N_DEV, len(POOL_WINDOWS), -1, POOL_GROUP), 1)}
        per_dev = gw["small"].reshape(N_DEV, -1)
        sizes = [wts[k].size for k in SMALL_SHARDED]
        offs = [sum(sizes[:i]) for i in range(len(sizes))]
        small_full = {k: per_dev[:, o:o + n_].reshape((N_DEV,) + wts[k].shape)
                      for k, o, n_ in zip(SMALL_SHARDED, offs, sizes)}
        return {
            "norm_mix_even": norm_mix_even, "ln_a_g": ln_a_g, "ln_a_b": ln_a_b, "norm_ffn": norm_ffn,
            "norm_final": norm_final[None],
            "w_in_t": gw["w_in"].reshape(-1, d),
            "w_out": gw["w_out"].reshape(-1, d),
            "conv_a": _from_dev_major(small_full["conv_a"][:, 0], 1),
            "conv_b": _from_dev_major(small_full["conv_b"][:, 0], 1),
            "norm_mix_odd": _from_dev_major(small_full["norm_mix_odd"], 1),
            "pool_scale": _from_dev_major(small_full["pool_scale"], 1),
            "conv_ffn": [small_full["conv_ffn_w"][:, l] for l in range(2)],
        }

    def dev_major(k, g):
        if k == "w_pool":
            return _to_dev_major(g, 1).reshape(N_DEV, -1, POOL_GROUP)
        return g.reshape(N_DEV, -1, g.shape[-1])

    sent = {}

    def grads_ready(grp, grads):
        keys = tuple(grads)
        parts = [dev_major(k, grads[k]) for k in keys]
        sent[grp] = (keys, _exchange_start(parts, [True] * len(keys), small_w, name=f"grads_{grp}_start"))
        return sent[grp][1][-1]

    loss, dx, g = _local_step(x[0], loss_target[0], weights, grads_ready)

    def small2d(a):
        return a.reshape(-1, a.shape[-1])

    small_keys = SMALL_SHARDED + SMALL_REPLICATED
    small_parts = [_to_dev_major(g["conv_a"], 1), _to_dev_major(g["conv_b"], 1),
                   _to_dev_major(g["norm_mix_odd"], 1), _to_dev_major(g["pool_scale"], 1),
                   jnp.stack(g["conv_ffn"], axis=1).reshape(N_DEV, -1, w_up.shape[-1])]
    small_parts += [g[k] for k in SMALL_REPLICATED]
    small_scatter = [True] * len(SMALL_SHARDED) + [False] * len(SMALL_REPLICATED)
    small_parts.append(jnp.full((1, 128), loss, F32))
    small_scatter.append(False)
    sent["small"] = (small_keys + ("loss",),
                     _exchange_start(small_parts, small_scatter, dx, name="grads_small_start"))

    landed, out, raw = {}, {}, {}

    def wait_grads(grp, after, scat=None):
        keys, st = sent[grp]
        scat = [True] * len(keys) if scat is None else scat
        landed.update(zip(keys, _exchange_wait(st, scat, after, name=f"grads_{grp}_wait")))

    def update(param, keys):
        raw[param] = _adamw([landed[k] for k in keys], shard3d(wts, param), shard3d(mom, param),
                            shard3d(var, param), name=f"adamw_{param}", tr=BIG[keys[0]][2])
        out[param] = [unshard3d(t, param) for t in raw[param]]

    wait_grads("ffn1", sent["small"][1][-1])
    wait_grads("ffn0", landed["w_up1"])
    update("w_up", ("w_up0", "w_up1"))
    update("w_down", ("w_down0", "w_down1"))
    wait_grads("mix0", [raw["w_up"][1], raw["w_down"][1]])
    update("w_in", ("w_in",))
    update("w_out", ("w_out",))
    update("w_pool", ("w_pool",))
    wait_grads("small", raw["w_pool"][1], small_scatter)
    res = _adamw_small([landed[k] for k in small_keys], [small2d(wts[k]) for k in small_keys],
                       [small2d(mom[k]) for k in small_keys], [small2d(var[k]) for k in small_keys],
                       name="adamw_small")
    for j, k in enumerate(small_keys):
        out[k] = [t.reshape(wts[k].shape) for t in res[4 * j:4 * j + 4]]

    loss = jnp.sum(landed["loss"][:, 0, 0])
    return (loss, dx[None], *[out[k][0] for k in names], *[out[k][1] for k in names],
            *[out[k][2] for k in names], *[out[k][3] for k in names])
```

```python
import jax
import jax.numpy as jnp
from jax import lax
from jax.experimental import pallas as pl
from jax.experimental.pallas import tpu as pltpu

F32 = jnp.float32
BF16 = jnp.bfloat16

RMS_EPS = 1e-6
LN_EPS = 1e-5
ADAM_LR = 0.001
ADAM_B1 = 0.9
ADAM_B2 = 0.999
ADAM_EPS = 1e-08
ADAM_WD = 0.01
ADAM_STEP = 10

N_DEV = 8
N_PAIR = N_DEV // 2
A_WIDTH = 512
A_TAPS = 31
POOL_WINDOWS = (2, 4, 8, 16)
POOL_GROUP = 256
HALO_A = 32
HALO_S = 16
VMEM_LIMIT = 56 * 1024 * 1024


def _params(sem, vmem=VMEM_LIMIT):
    return pltpu.CompilerParams(dimension_semantics=sem, vmem_limit_bytes=vmem)


def _sigmoid(x):
    return 0.5 * jnp.tanh(0.5 * x) + 0.5


def _prev_blk(i, ts, hb):
    return jnp.maximum(i * (ts // hb) - 1, 0)


def _next_blk(i, ts, hb, s):
    return jnp.minimum((i + 1) * (ts // hb), s // hb - 1)


def _mm(a, b, *, name, ta=False, tb=False, out_dtype=F32, tm=512, tn=512):
    m, k = (a.shape[1], a.shape[0]) if ta else a.shape
    n = b.shape[0] if tb else b.shape[1]
    tm, tn = min(tm, m), min(tn, n)
    assert m % tm == 0 and n % tn == 0, (name, m, n, tm, tn)
    dims = (((0,) if ta else (1,), (1,) if tb else (0,)), ((), ()))

    def body(a_ref, b_ref, o_ref):
        o_ref[...] = lax.dot_general(a_ref[...], b_ref[...], dims, preferred_element_type=F32).astype(out_dtype)

    a_spec = pl.BlockSpec((k, tm), lambda i, j: (0, i)) if ta else pl.BlockSpec((tm, k), lambda i, j: (i, 0))
    b_spec = pl.BlockSpec((tn, k), lambda i, j: (j, 0)) if tb else pl.BlockSpec((k, tn), lambda i, j: (0, j))
    return pl.pallas_call(
        body, name=name, grid=(m // tm, n // tn),
        in_specs=[a_spec, b_spec], out_specs=pl.BlockSpec((tm, tn), lambda i, j: (i, j)),
        out_shape=jax.ShapeDtypeStruct((m, n), out_dtype),
        compiler_params=_params(("parallel", "parallel")),
    )(a, b)


def _rms_rows(xv, gv):
    return xv * lax.rsqrt(jnp.mean(xv * xv, axis=-1, keepdims=True) + RMS_EPS) * gv


_NT = (((1,), (1,)), ((), ()))
_TN = (((0,), (0,)), ((), ()))


def _rms_mm(x, g, wt, *, name, out_dtype, tm=1024, tn=512):
    s, d = x.shape
    n = wt.shape[0]
    tm = min(tm, s)
    assert s % tm == 0 and n % tn == 0

    def body(x_ref, g_ref, w_ref, h_ref, z_ref, hs_ref):
        @pl.when(pl.program_id(1) == 0)
        def _():
            hv = _rms_rows(x_ref[...], g_ref[...]).astype(BF16)
            hs_ref[...] = hv
            h_ref[...] = hv

        z_ref[...] = lax.dot_general(hs_ref[...], w_ref[...], _NT, preferred_element_type=F32).astype(out_dtype)

    return pl.pallas_call(
        body, name=name, grid=(s // tm, n // tn),
        in_specs=[pl.BlockSpec((tm, d), lambda i, j: (i, 0)),
                  pl.BlockSpec((1, d), lambda i, j: (0, 0)),
                  pl.BlockSpec((tn, d), lambda i, j: (j, 0))],
        out_specs=[pl.BlockSpec((tm, d), lambda i, j: (i, 0)),
                   pl.BlockSpec((tm, tn), lambda i, j: (i, j))],
        out_shape=[jax.ShapeDtypeStruct((s, d), BF16), jax.ShapeDtypeStruct((s, n), out_dtype)],
        scratch_shapes=[pltpu.VMEM((tm, d), BF16)],
        compiler_params=_params(("parallel", "arbitrary")),
    )(x, g, wt)


def _rms_bwd_rows(xv, gv, dh):
    r = lax.rsqrt(jnp.mean(xv * xv, axis=-1, keepdims=True) + RMS_EPS)
    xh = xv * r
    dn = dh * gv
    dx = r * (dn - xh * jnp.mean(dn * xh, axis=-1, keepdims=True))
    return dx, dh * xh


def _mm_rms_bwd(a, b, x, g, dres, dep, *, name, tm=512):
    s, k = a.shape
    d = b.shape[1]
    tm = min(tm, s)

    def body(a_ref, b_ref, x_ref, g_ref, dres_ref, dep_ref, dx_ref, dg_ref):
        @pl.when(pl.program_id(0) == 0)
        def _():
            dg_ref[...] = jnp.zeros_like(dg_ref)

        dh = jnp.dot(a_ref[...], b_ref[...], preferred_element_type=F32)
        dx, dgr = _rms_bwd_rows(x_ref[...], g_ref[...], dh)
        dx_ref[...] = dres_ref[...] + dx
        dg_ref[...] += jnp.sum(dgr, axis=0, keepdims=True)

    row = pl.BlockSpec((tm, d), lambda i: (i, 0))
    vec = pl.BlockSpec((1, d), lambda i: (0, 0))
    return pl.pallas_call(
        body, name=name, grid=(s // tm,),
        in_specs=[pl.BlockSpec((tm, k), lambda i: (i, 0)), pl.BlockSpec((k, d), lambda i: (0, 0)), row, vec, row,
                  pl.BlockSpec(memory_space=pl.ANY)],
        out_specs=[row, vec],
        out_shape=[jax.ShapeDtypeStruct((s, d), F32), jax.ShapeDtypeStruct((1, d), F32)],
        compiler_params=_params(("arbitrary",)),
    )(a, b, x, g, dres, dep)


def _conv_taps(ext_ref, w_ref, n_taps, base, r0, rows, reverse=False):
    acc = None
    for k in range(n_taps):
        off = r0 + (base - k if reverse else base + k)
        term = w_ref[k:k + 1, :] * ext_ref[pl.ds(off, rows), :]
        acc = term if acc is None else acc + term
    return acc


def _shift_copies(src_ref, sh_ref, rows):
    for b in range(1, 8):
        sh_ref[b, 0:rows, :] = src_ref[pl.ds(b, rows), :]


def _shifted(src_ref, sh_ref, start, rows, off):
    a, b = divmod(off, 8)
    ref = src_ref if b == 0 else sh_ref.at[b]
    return ref[pl.ds(start + 8 * a, rows), :]


def _mix0_fwd(z, conv_a, ln_g, ln_b, conv_b, w_out, x, *, name, ts=512, rc=32):
    s = z.shape[0]
    c = A_WIDTH
    hb = HALO_A
    half = ts // 2

    def body(z_ref, zp_ref, wa_ref, lg_ref, lb_ref, wb_ref, wo_ref, x_ref, ab_ref, ca_ref, o_ref, exta, extb, sha):
        keep = jnp.where(pl.program_id(0) > 0, 1.0, 0.0)
        zp = zp_ref[...]
        exta[0:hb, :] = zp[:, 0:c] * _sigmoid(zp[:, c:2 * c]) * keep
        extb[0:hb, :] = zp[:, 3 * c:4 * c] * zp[:, 4 * c:5 * c] * keep
        exta[hb:hb + ts, :] = z_ref[:, 0:c] * _sigmoid(z_ref[:, c:2 * c])
        extb[hb:hb + ts, :] = z_ref[:, 3 * c:4 * c] * z_ref[:, 4 * c:5 * c]
        _shift_copies(exta, sha, hb + ts - 8)
        lg = lg_ref[...]
        lb = lb_ref[...]
        for q in range(ts // rc):
            r0 = q * rc
            ca = None
            for k in range(A_TAPS):
                term = wa_ref[k:k + 1, :] * _shifted(exta, sha, r0, rc, hb - (A_TAPS - 1) + k)
                ca = term if ca is None else ca + term
            ca_ref[r0:r0 + rc, :] = ca
            mu = jnp.mean(ca, axis=-1, keepdims=True)
            xc = ca - mu
            rs = lax.rsqrt(jnp.mean(xc * xc, axis=-1, keepdims=True) + LN_EPS)
            l = xc * rs * lg + lb
            ab_ref[r0:r0 + rc, 0:c] = (l * _sigmoid(l)).astype(BF16)
            cbc = _conv_taps(extb, wb_ref, 3, hb - 2, r0, rc)
            ab_ref[r0:r0 + rc, c:2 * c] = (z_ref[r0:r0 + rc, 2 * c:3 * c] * cbc).astype(BF16)
            if (r0 + rc) % half == 0:
                rows = slice(r0 + rc - half, r0 + rc)
                o_ref[rows, :] = x_ref[rows, :] + jnp.dot(ab_ref[rows, :], wo_ref[...], preferred_element_type=F32)

    d = x.shape[1]
    row = pl.BlockSpec((ts, d), lambda i: (i, 0))
    return pl.pallas_call(
        body, name=name, grid=(s // ts,),
        in_specs=[pl.BlockSpec((ts, 5 * c), lambda i: (i, 0)),
                  pl.BlockSpec((hb, 5 * c), lambda i: (_prev_blk(i, ts, hb), 0)),
                  pl.BlockSpec((32, c), lambda i: (0, 0)),
                  pl.BlockSpec((1, c), lambda i: (0, 0)),
                  pl.BlockSpec((1, c), lambda i: (0, 0)),
                  pl.BlockSpec((8, c), lambda i: (0, 0)),
                  pl.BlockSpec((2 * c, d), lambda i: (0, 0)), row],
        out_specs=[pl.BlockSpec((ts, 2 * c), lambda i: (i, 0)),
                   pl.BlockSpec((ts, c), lambda i: (i, 0)), row],
        out_shape=[jax.ShapeDtypeStruct((s, 2 * c), BF16), jax.ShapeDtypeStruct((s, c), F32),
                   jax.ShapeDtypeStruct((s, d), F32)],
        scratch_shapes=[pltpu.VMEM((hb + ts, c), F32), pltpu.VMEM((hb + ts, c), F32),
                        pltpu.VMEM((8, hb + ts - 8, c), F32)],
        compiler_params=_params(("parallel",)),
    )(z, z, conv_a, ln_g, ln_b, conv_b, w_out, x)


def _mix0_bwd(z, ca, dab, conv_a, ln_g, ln_b, conv_b, *, name, ts=512, rc=32):
    s = z.shape[0]
    c = A_WIDTH
    hb = HALO_A
    ta = A_TAPS

    def body(z_ref, zp_ref, zn_ref, ca_ref, can_ref, d_ref, dn_ref, wa_ref, lg_ref, lb_ref, wb_ref,
             dz_ref, dwa_ref, dwb_ref, dlg_ref, dlb_ref, exta, extb, extdca, extdcb, shd):
        i = pl.program_id(0)
        keep_p = jnp.where(i > 0, 1.0, 0.0)
        keep_n = jnp.where(i < s // ts - 1, 1.0, 0.0)

        @pl.when(i == 0)
        def _():
            dwa_ref[...] = jnp.zeros_like(dwa_ref)
            dwb_ref[...] = jnp.zeros_like(dwb_ref)
            dlg_ref[...] = jnp.zeros_like(dlg_ref)
            dlb_ref[...] = jnp.zeros_like(dlb_ref)

        lg = lg_ref[...]
        lb = lb_ref[...]
        zp = zp_ref[...]
        exta[0:hb, :] = zp[:, 0:c] * _sigmoid(zp[:, c:2 * c]) * keep_p
        extb[0:hb, :] = zp[:, 3 * c:4 * c] * zp[:, 4 * c:5 * c] * keep_p
        exta[hb:hb + ts, :] = z_ref[:, 0:c] * _sigmoid(z_ref[:, c:2 * c])
        extb[hb:hb + ts, :] = z_ref[:, 3 * c:4 * c] * z_ref[:, 4 * c:5 * c]

        def ln_bwd(cav, dav):
            mu = jnp.mean(cav, axis=-1, keepdims=True)
            xc = cav - mu
            rs = lax.rsqrt(jnp.mean(xc * xc, axis=-1, keepdims=True) + LN_EPS)
            nv = xc * rs
            l = nv * lg + lb
            sg = _sigmoid(l)
            dl = dav * (sg * (1.0 + l * (1.0 - sg)))
            dnv = dl * lg
            dca = rs * (dnv - jnp.mean(dnv, axis=-1, keepdims=True)
                        - nv * jnp.mean(dnv * nv, axis=-1, keepdims=True))
            return dca, dl, nv

        dlg_acc = jnp.zeros((1, c), F32)
        dlb_acc = jnp.zeros((1, c), F32)
        for q in range(ts // rc):
            r0 = q * rc
            dca, dl, nv = ln_bwd(ca_ref[r0:r0 + rc, :], d_ref[r0:r0 + rc, 0:c])
            extdca[r0:r0 + rc, :] = dca
            dlg_acc = dlg_acc + jnp.sum(dl * nv, axis=0, keepdims=True)
            dlb_acc = dlb_acc + jnp.sum(dl, axis=0, keepdims=True)
            extdcb[r0:r0 + rc, :] = d_ref[r0:r0 + rc, c:2 * c] * z_ref[r0:r0 + rc, 2 * c:3 * c]
        dca_n, _, _ = ln_bwd(can_ref[...], dn_ref[:, 0:c])
        extdca[ts:ts + hb, :] = dca_n * keep_n
        extdcb[ts:ts + hb, :] = dn_ref[:, c:2 * c] * zn_ref[:, 2 * c:3 * c] * keep_n
        dlg_ref[...] += dlg_acc
        dlb_ref[...] += dlb_acc
        _shift_copies(extdca, shd, ts + hb - 8)

        for q in range(ts // rc):
            r0 = q * rc
            zr = z_ref[r0:r0 + rc, :]
            dga = None
            for k in range(ta):
                term = wa_ref[k:k + 1, :] * _shifted(extdca, shd, r0, rc, ta - 1 - k)
                dga = term if dga is None else dga + term
            sg = _sigmoid(zr[:, c:2 * c])
            dz_ref[r0:r0 + rc, 0:c] = (dga * sg).astype(BF16)
            dz_ref[r0:r0 + rc, c:2 * c] = (dga * zr[:, 0:c] * sg * (1.0 - sg)).astype(BF16)
            cbc = _conv_taps(extb, wb_ref, 3, hb - 2, r0, rc)
            dz_ref[r0:r0 + rc, 2 * c:3 * c] = (d_ref[r0:r0 + rc, c:2 * c] * cbc).astype(BF16)
            dcb = _conv_taps(extdcb, wb_ref, 3, 2, r0, rc, reverse=True)
            dz_ref[r0:r0 + rc, 3 * c:4 * c] = (dcb * zr[:, 4 * c:5 * c]).astype(BF16)
            dz_ref[r0:r0 + rc, 4 * c:5 * c] = (dcb * zr[:, 3 * c:4 * c]).astype(BF16)

        for k in range(ta):
            part = None
            for q in range(ts // rc):
                r0 = q * rc
                p = exta[hb + r0:hb + r0 + rc, :] * _shifted(extdca, shd, r0, rc, ta - 1 - k)
                for r in range(0, rc, 8):
                    part = p[r:r + 8, :] if part is None else part + p[r:r + 8, :]
            dwa_ref[k:k + 1, :] += jnp.sum(part, axis=0, keepdims=True)
        dcb_t = extdcb[0:ts, :]
        for k in range(3):
            dwb_ref[k:k + 1, :] += jnp.sum(dcb_t * extb[pl.ds(hb - 2 + k, ts), :], axis=0, keepdims=True)

    def tile(w):
        return pl.BlockSpec((ts, w), lambda i: (i, 0))

    def prev(w):
        return pl.BlockSpec((hb, w), lambda i: (_prev_blk(i, ts, hb), 0))

    def nxt(w):
        return pl.BlockSpec((hb, w), lambda i: (_next_blk(i, ts, hb, s), 0))

    def const(r, w):
        return pl.BlockSpec((r, w), lambda i: (0, 0))

    return pl.pallas_call(
        body, name=name, grid=(s // ts,),
        in_specs=[tile(5 * c), prev(5 * c), nxt(5 * c), tile(c), nxt(c), tile(2 * c), nxt(2 * c),
                  const(32, c), const(1, c), const(1, c), const(8, c)],
        out_specs=[tile(5 * c), const(32, c), const(8, c), const(1, c), const(1, c)],
        out_shape=[jax.ShapeDtypeStruct((s, 5 * c), BF16), jax.ShapeDtypeStruct((32, c), F32),
                   jax.ShapeDtypeStruct((8, c), F32), jax.ShapeDtypeStruct((1, c), F32),
                   jax.ShapeDtypeStruct((1, c), F32)],
        scratch_shapes=[pltpu.VMEM((hb + ts, c), F32), pltpu.VMEM((hb + ts, c), F32),
                        pltpu.VMEM((ts + hb, c), F32), pltpu.VMEM((ts + hb, c), F32),
                        pltpu.VMEM((8, ts + hb - 8, c), F32)],
        compiler_params=_params(("arbitrary",)),
    )(z, z, z, ca, ca, dab, dab, conv_a, ln_g, ln_b, conv_b)


def _ffn_up(x, g, w8, *, name, tm=2048):
    s, d = x.shape
    nb, c, _ = w8.shape
    tm = min(tm, s)

    def body(x_ref, g_ref, w_ref, h_ref, u_ref, hs_ref):
        @pl.when(pl.program_id(1) == 0)
        def _():
            hv = _rms_rows(x_ref[...], g_ref[...]).astype(BF16)
            hs_ref[...] = hv
            h_ref[...] = hv

        u_ref[...] = lax.dot_general(hs_ref[...], w_ref[...], _NT, preferred_element_type=F32).astype(BF16)

    return pl.pallas_call(
        body, name=name, grid=(s // tm, nb),
        in_specs=[pl.BlockSpec((tm, d), lambda i, k: (i, 0)),
                  pl.BlockSpec((1, d), lambda i, k: (0, 0)),
                  pl.BlockSpec((None, c, d), lambda i, k: (k, 0, 0))],
        out_specs=[pl.BlockSpec((tm, d), lambda i, k: (i, 0)),
                   pl.BlockSpec((None, tm, c), lambda i, k: (k, i, 0))],
        out_shape=[jax.ShapeDtypeStruct((s, d), BF16), jax.ShapeDtypeStruct((nb, s, c), BF16)],
        scratch_shapes=[pltpu.VMEM((tm, d), BF16)],
        compiler_params=_params(("parallel", "arbitrary")),
    )(x, g, w8)


def _ffn_mid_down(up8, wc8, wd4, x, head=None, *, name, ts=256, rc=32):
    nb, s, c = up8.shape
    d = x.shape[1]
    hb = HALO_S
    ts = min(ts, s)

    def body(*refs):
        u_ref, up_ref, wc_ref, wd_ref, x_ref = refs[:5]
        n_out = 4 if head is None else 7
        outs = refs[-1 - n_out:-1]
        act_ref, ug_ref, uv_ref = outs[-3:]
        ext = refs[-1]
        keep = jnp.where(pl.program_id(0) > 0, 1.0, 0.0)
        acc = x_ref[...]
        for j in range(N_PAIR):
            eg, ev = ext.at[2 * (j % 2)], ext.at[2 * (j % 2) + 1]
            eg[0:hb, :] = up_ref[j].astype(F32) * keep
            ev[0:hb, :] = up_ref[j + N_PAIR].astype(F32) * keep
            eg[hb:hb + ts, :] = u_ref[j].astype(F32)
            ev[hb:hb + ts, :] = u_ref[j + N_PAIR].astype(F32)
            for q in range(ts // rc):
                r0 = q * rc
                gg = _conv_taps(eg, wc_ref.at[j], 3, hb - 2, r0, rc)
                vv = _conv_taps(ev, wc_ref.at[j + N_PAIR], 3, hb - 2, r0, rc)
                ug_ref[j, r0:r0 + rc, :] = gg.astype(BF16)
                uv_ref[j, r0:r0 + rc, :] = vv.astype(BF16)
                hg = 0.5 * gg
                act_ref[j, r0:r0 + rc, :] = ((hg * jnp.tanh(hg) + hg) * vv).astype(BF16)
            acc = acc + jnp.dot(act_ref[j], wd_ref[j], preferred_element_type=F32)
        if head is None:
            outs[0][...] = acc
            return
        g_ref, t_ref = refs[5], refs[6]
        loss_ref, dx_ref, dxb_ref, dg_ref = outs[:4]

        @pl.when(pl.program_id(0) == 0)
        def _():
            loss_ref[...] = jnp.zeros_like(loss_ref)
            dg_ref[...] = jnp.zeros_like(dg_ref)

        gv = g_ref[...]
        r = lax.rsqrt(jnp.mean(acc * acc, axis=-1, keepdims=True) + RMS_EPS)
        xh = acc * r
        err = xh * gv - t_ref[...]
        loss_ref[...] += 0.5 * jnp.sum(jnp.mean(err * err, axis=-1, keepdims=True), axis=0, keepdims=True)
        dy = err * (1.0 / d)
        dn = dy * gv
        dx = r * (dn - xh * jnp.mean(dn * xh, axis=-1, keepdims=True))
        dx_ref[...] = dx
        dxb_ref[...] = dx.astype(BF16)
        dg_ref[...] += jnp.sum(dy * xh, axis=0, keepdims=True)

    row = pl.BlockSpec((ts, d), lambda i: (i, 0))
    vec = pl.BlockSpec((1, d), lambda i: (0, 0))
    tile = pl.BlockSpec((N_PAIR, ts, c), lambda i: (0, i, 0))
    half = jax.ShapeDtypeStruct((N_PAIR, s, c), BF16)
    in_specs = [pl.BlockSpec((nb, ts, c), lambda i: (0, i, 0)),
                pl.BlockSpec((nb, hb, c), lambda i: (0, _prev_blk(i, ts, hb), 0)),
                pl.BlockSpec((nb, 8, c), lambda i: (0, 0, 0)),
                pl.BlockSpec((N_PAIR, c, d), lambda i: (0, 0, 0)), row]
    args = [up8, up8, wc8, wd4, x]
    if head is None:
        out_specs = [row, tile, tile, tile]
        out_shape = [jax.ShapeDtypeStruct((s, d), F32), half, half, half]
    else:
        in_specs += [vec, row]
        args += list(head)
        out_specs = [pl.BlockSpec((1, 1), lambda i: (0, 0)), row, row, vec, tile, tile, tile]
        out_shape = [jax.ShapeDtypeStruct((1, 1), F32), jax.ShapeDtypeStruct((s, d), F32),
                     jax.ShapeDtypeStruct((s, d), BF16), jax.ShapeDtypeStruct((1, d), F32), half, half, half]
    return pl.pallas_call(
        body, name=name, grid=(s // ts,), in_specs=in_specs, out_specs=out_specs, out_shape=out_shape,
        scratch_shapes=[pltpu.VMEM((4, hb + ts, c), F32)],
        compiler_params=_params(("parallel",) if head is None else ("arbitrary",)),
    )(*args)


def _ffn_dwdown(act4, db, *, name):
    npair, s, c = act4.shape
    d = db.shape[1]

    def body(a_ref, d_ref, o_ref):
        o_ref[...] = lax.dot_general(a_ref[...], d_ref[...], (((0,), (0,)), ((), ())),
                                     preferred_element_type=F32).astype(BF16)

    return pl.pallas_call(
        body, name=name, grid=(npair,),
        in_specs=[pl.BlockSpec((None, s, c), lambda j: (j, 0, 0)),
                  pl.BlockSpec((s, d), lambda j: (0, 0))],
        out_specs=pl.BlockSpec((None, c, d), lambda j: (j, 0, 0)),
        out_shape=jax.ShapeDtypeStruct((npair, c, d), BF16),
        compiler_params=_params(("parallel",)),
    )(act4, db)


def _ffn_dact(db, wd4, *, name, tm=2048):
    s, d = db.shape
    npair, c, _ = wd4.shape
    tm = min(tm, s)

    def body(d_ref, w_ref, o_ref):
        o_ref[...] = lax.dot_general(d_ref[...], w_ref[...], _NT, preferred_element_type=F32).astype(BF16)

    return pl.pallas_call(
        body, name=name, grid=(s // tm, npair),
        in_specs=[pl.BlockSpec((tm, d), lambda i, j: (i, 0)),
                  pl.BlockSpec((None, c, d), lambda i, j: (j, 0, 0))],
        out_specs=pl.BlockSpec((None, tm, c), lambda i, j: (j, i, 0)),
        out_shape=jax.ShapeDtypeStruct((npair, s, c), BF16),
        compiler_params=_params(("parallel", "parallel")),
    )(db, wd4)


def _ffn_midbwd(up8, ug4, uv4, dact4, wc8, *, name, ts=1024, rc=32):
    nb, s, c = up8.shape
    hb = HALO_S
    ts = min(ts, s)
    n_i = s // ts

    def body(pg_ref, pv_ref, ug_ref, ugn_ref, uv_ref, uvn_ref, d_ref, dn_ref, wg_ref, wv_ref,
             dg_ref, dv_ref, dwg_ref, dwv_ref, extdg, extdv):
        i = pl.program_id(1)
        keep_n = jnp.where(i < n_i - 1, 1.0, 0.0)

        @pl.when(i == 0)
        def _():
            dwg_ref[...] = jnp.zeros_like(dwg_ref)
            dwv_ref[...] = jnp.zeros_like(dwv_ref)

        def du_rows(r0, rows, gg, vv, dav):
            sg = _sigmoid(gg)
            silu = gg * sg
            extdg[r0:r0 + rows, :] = dav * vv * (sg + silu - silu * sg)
            extdv[r0:r0 + rows, :] = dav * silu

        for q in range(ts // rc):
            rows = slice(q * rc, q * rc + rc)
            du_rows(q * rc, rc, ug_ref[rows, :].astype(F32), uv_ref[rows, :].astype(F32),
                    d_ref[rows, :].astype(F32))
        du_rows(ts, 8, ugn_ref[0:8, :].astype(F32), uvn_ref[0:8, :].astype(F32),
                dn_ref[0:8, :].astype(F32) * keep_n)

        def fold(p):
            acc = p[0:8, :]
            for r in range(8, rc, 8):
                acc = acc + p[r:r + 8, :]
            return acc

        for extd, p_ref, w_ref, out_ref, dw_ref in ((extdg, pg_ref, wg_ref, dg_ref, dwg_ref),
                                                    (extdv, pv_ref, wv_ref, dv_ref, dwv_ref)):
            part = [None, None, None]
            for q in range(ts // rc):
                r0 = q * rc
                pre = p_ref[r0:r0 + rc, :].astype(F32)
                dup = None
                for k in range(3):
                    sh = extd[pl.ds(r0 + 2 - k, rc), :]
                    term = w_ref[k:k + 1, :] * sh
                    dup = term if dup is None else dup + term
                    prod = fold(pre * sh)
                    part[k] = prod if part[k] is None else part[k] + prod
                out_ref[r0:r0 + rc, :] = dup.astype(BF16)
            for k in range(3):
                dw_ref[k:k + 1, :] += jnp.sum(part[k], axis=0, keepdims=True)

    def blk(off, nxt):
        if nxt:
            return pl.BlockSpec((None, hb, c), lambda j, i: (j + off, _next_blk(i, ts, hb, s), 0))
        return pl.BlockSpec((None, ts, c), lambda j, i: (j + off, i, 0))

    def taps(off):
        return pl.BlockSpec((None, 8, c), lambda j, i: (j + off, 0, 0))

    tile = blk(0, False)
    acc = pl.BlockSpec((None, 8, c), lambda j, i: (j, 0, 0))
    return pl.pallas_call(
        body, name=name, grid=(N_PAIR, n_i),
        in_specs=[tile, blk(N_PAIR, False), tile, blk(0, True), tile, blk(0, True), tile, blk(0, True),
                  taps(0), taps(N_PAIR)],
        out_specs=[tile, tile, acc, acc],
        out_shape=[jax.ShapeDtypeStruct((N_PAIR, s, c), BF16), jax.ShapeDtypeStruct((N_PAIR, s, c), BF16),
                   jax.ShapeDtypeStruct((N_PAIR, 8, c), F32), jax.ShapeDtypeStruct((N_PAIR, 8, c), F32)],
        scratch_shapes=[pltpu.VMEM((ts + 8, c), F32), pltpu.VMEM((ts + 8, c), F32)],
        compiler_params=_params(("parallel", "arbitrary")),
    )(up8, up8, ug4, ug4, uv4, uv4, dact4, dact4, wc8, wc8)


def _ffn_dh(dupg, dupv, w8, x, g, dres, dep, *, name, tm=512):
    npair, s, c = dupg.shape
    d = x.shape[1]
    tm = min(tm, s)

    def body(dg_ref, dv_ref, w_ref, x_ref, g_ref, dres_ref, dep_ref, dx_ref, dxb_ref, dgain_ref):
        @pl.when(pl.program_id(0) == 0)
        def _():
            dgain_ref[...] = jnp.zeros_like(dgain_ref)

        half = tm // 2
        for hh in range(2):
            rows = slice(hh * half, (hh + 1) * half)
            dh = None
            for j in range(npair):
                for src, k in ((dg_ref, j), (dv_ref, j + npair)):
                    part = jnp.dot(src[j, rows, :], w_ref[k], preferred_element_type=F32)
                    dh = part if dh is None else dh + part
            dx, dgr = _rms_bwd_rows(x_ref[rows, :], g_ref[...], dh)
            dx = dres_ref[rows, :] + dx
            dx_ref[rows, :] = dx
            dxb_ref[rows, :] = dx.astype(BF16)
            dgain_ref[...] += jnp.sum(dgr, axis=0, keepdims=True)

    row = pl.BlockSpec((tm, d), lambda i: (i, 0))
    vec = pl.BlockSpec((1, d), lambda i: (0, 0))
    dup = pl.BlockSpec((npair, tm, c), lambda i: (0, i, 0))
    return pl.pallas_call(
        body, name=name, grid=(s // tm,),
        in_specs=[dup, dup, pl.BlockSpec((2 * npair, c, d), lambda i: (0, 0, 0)), row, vec, row,
                  pl.BlockSpec(memory_space=pl.ANY)],
        out_specs=[row, row, vec],
        out_shape=[jax.ShapeDtypeStruct((s, d), F32), jax.ShapeDtypeStruct((s, d), BF16),
                   jax.ShapeDtypeStruct((1, d), F32)],
        compiler_params=_params(("arbitrary",)),
    )(dupg, dupv, w8, x, g, dres, dep)


def _ffn_dwup(h, dupg, dupv, *, name, tm=1024):
    npair, s, c = dupg.shape
    d = h.shape[1]

    def body(h_ref, dg_ref, dv_ref, o_ref):
        k = pl.program_id(1)

        @pl.when(k < npair)
        def _():
            o_ref[...] = lax.dot_general(dg_ref[...], h_ref[...], _TN, preferred_element_type=F32).astype(BF16)

        @pl.when(k >= npair)
        def _():
            o_ref[...] = lax.dot_general(dv_ref[...], h_ref[...], _TN, preferred_element_type=F32).astype(BF16)

    return pl.pallas_call(
        body, name=name, grid=(d // tm, 2 * npair),
        in_specs=[pl.BlockSpec((s, tm), lambda m, k: (0, m)),
                  pl.BlockSpec((None, s, c), lambda m, k: (jnp.minimum(k, npair - 1), 0, 0)),
                  pl.BlockSpec((None, s, c), lambda m, k: (jnp.maximum(k - npair, 0), 0, 0))],
        out_specs=pl.BlockSpec((None, c, tm), lambda m, k: (k, 0, m)),
        out_shape=jax.ShapeDtypeStruct((2 * npair, c, d), BF16),
        compiler_params=_params(("parallel", "arbitrary")),
    )(h, dupg, dupv)


def _pool_counts(i, ts, rows, window):
    t = lax.broadcasted_iota(jnp.int32, (rows, 1), 0) + i * ts + 1
    return jnp.minimum(t, window).astype(F32)


def _pool_fwd(x, g, w_pool, scale, *, name, ts=512):
    s, d = x.shape
    hb = HALO_A
    pg = POOL_GROUP
    ts = min(ts, s)

    def body(x_ref, xp_ref, g_ref, w_ref, s_ref, o_ref, p_ref, yu_ref, ext, sa, sb):
        i = pl.program_id(0)
        keep = jnp.where(i > 0, 1.0, 0.0)
        gv = g_ref[...]
        ext[0:hb, :] = _rms_rows(xp_ref[...], gv) * keep
        ext[hb:hb + ts, :] = _rms_rows(x_ref[...], gv)
        rows = hb + ts - 8
        sa[0:8, :] = jnp.zeros((8, d), F32)
        sb[0:8, :] = jnp.zeros((8, d), F32)
        for gi, w in enumerate(POOL_WINDOWS):
            cols = slice(gi * pg, (gi + 1) * pg)
            cur, nxt, k = ext, sa, 1
            while k < w:
                nxt[8:8 + rows, cols] = cur[8:8 + rows, cols] + cur[pl.ds(8 - k, rows), cols]
                cur, nxt, k = nxt, (sb if nxt is sa else sa), 2 * k
            h = ext[hb:hb + ts, cols]
            pv = (cur[hb:hb + ts, cols] / _pool_counts(i, ts, ts, w) - h).astype(BF16)
            p_ref[:, cols] = pv
            yu = jnp.dot(pv, w_ref[gi], preferred_element_type=F32)
            yu_ref[:, cols] = yu.astype(BF16)
            o_ref[:, cols] = x_ref[:, cols] + yu * s_ref[:, cols]

    row = pl.BlockSpec((ts, d), lambda i: (i, 0))
    vec = pl.BlockSpec((1, d), lambda i: (0, 0))
    return pl.pallas_call(
        body, name=name, grid=(s // ts,),
        in_specs=[row, pl.BlockSpec((hb, d), lambda i: (_prev_blk(i, ts, hb), 0)), vec,
                  pl.BlockSpec((d // pg, pg, pg), lambda i: (0, 0, 0)), vec],
        out_specs=[row, row, row],
        out_shape=[jax.ShapeDtypeStruct((s, d), F32), jax.ShapeDtypeStruct((s, d), BF16),
                   jax.ShapeDtypeStruct((s, d), BF16)],
        scratch_shapes=[pltpu.VMEM((hb + ts, d), F32)] * 3,
        compiler_params=_params(("parallel",)),
    )(x, x, g, w_pool, scale)


def _pool_dw(p, dyc, *, name):
    s, d = p.shape
    pg = POOL_GROUP

    def body(p_ref, d_ref, o_ref):
        o_ref[...] = lax.dot_general(p_ref[...], d_ref[...], (((0,), (0,)), ((), ())),
                                     preferred_element_type=F32).astype(BF16)

    blk = pl.BlockSpec((s, pg), lambda gi: (0, gi))
    return pl.pallas_call(
        body, name=name, grid=(d // pg,),
        in_specs=[blk, blk], out_specs=pl.BlockSpec((None, pg, pg), lambda gi: (gi, 0, 0)),
        out_shape=jax.ShapeDtypeStruct((d // pg, pg, pg), BF16),
        compiler_params=_params(("parallel",)),
    )(p, dyc)


def _pool_bwd(dres, w_pool, scale, yu, x, g, *, name, ts=512):
    s, d = x.shape
    hb = HALO_A
    pg = POOL_GROUP
    ts = min(ts, s)
    n_i = s // ts

    def body(dres_ref, dresn_ref, w_ref, s_ref, yu_ref, x_ref, g_ref,
             dx_ref, dxb_ref, dg_ref, dyc_ref, ds_ref, ext, dh, sa, sb, dp_s):
        i = pl.program_id(0)
        keep_n = jnp.where(i < n_i - 1, 1.0, 0.0)

        @pl.when(i == 0)
        def _():
            dg_ref[...] = jnp.zeros_like(dg_ref)
            ds_ref[...] = jnp.zeros_like(ds_ref)

        ds_ref[...] += jnp.sum(dres_ref[...] * yu_ref[...].astype(F32), axis=0, keepdims=True)
        for gi, w in enumerate(POOL_WINDOWS):
            cols = slice(gi * pg, (gi + 1) * pg)
            dyc = (dres_ref[:, cols] * s_ref[:, cols]).astype(BF16)
            dyc_ref[:, cols] = dyc
            dp = lax.dot_general(dyc, w_ref[gi], _NT, preferred_element_type=F32)
            dp_s[:, cols] = dp
            ext[0:ts, cols] = dp / _pool_counts(i, ts, ts, w)
            dycn = (dresn_ref[:, cols] * s_ref[:, cols]).astype(BF16)
            dpn = lax.dot_general(dycn, w_ref[gi], _NT, preferred_element_type=F32)
            ext[ts:ts + hb, cols] = dpn / _pool_counts(i + 1, ts, hb, w) * keep_n
        rows = ts + hb - 8
        sa[rows:rows + 8, :] = jnp.zeros((8, d), F32)
        sb[rows:rows + 8, :] = jnp.zeros((8, d), F32)
        for gi, w in enumerate(POOL_WINDOWS):
            cols = slice(gi * pg, (gi + 1) * pg)
            cur, nxt, k = ext, sa, 1
            while k < w:
                nxt[0:rows, cols] = cur[0:rows, cols] + cur[pl.ds(k, rows), cols]
                cur, nxt, k = nxt, (sb if nxt is sa else sa), 2 * k
            dh[:, cols] = cur[0:ts, cols] - dp_s[:, cols]
        dx, dgr = _rms_bwd_rows(x_ref[...], g_ref[...], dh[...])
        dx = dres_ref[...] + dx
        dx_ref[...] = dx
        dxb_ref[...] = dx.astype(BF16)
        dg_ref[...] += jnp.sum(dgr, axis=0, keepdims=True)

    row = pl.BlockSpec((ts, d), lambda i: (i, 0))
    vec = pl.BlockSpec((1, d), lambda i: (0, 0))
    return pl.pallas_call(
        body, name=name, grid=(n_i,),
        in_specs=[row, pl.BlockSpec((hb, d), lambda i: (_next_blk(i, ts, hb, s), 0)),
                  pl.BlockSpec((d // pg, pg, pg), lambda i: (0, 0, 0)), vec, row, row, vec],
        out_specs=[row, row, vec, row, vec],
        out_shape=[jax.ShapeDtypeStruct((s, d), F32), jax.ShapeDtypeStruct((s, d), BF16),
                   jax.ShapeDtypeStruct((1, d), F32), jax.ShapeDtypeStruct((s, d), BF16),
                   jax.ShapeDtypeStruct((1, d), F32)],
        scratch_shapes=[pltpu.VMEM((ts + hb, d), F32), pltpu.VMEM((ts, d), F32),
                        pltpu.VMEM((ts + hb, d), F32), pltpu.VMEM((ts + hb, d), F32),
                        pltpu.VMEM((ts, d), F32)],
        compiler_params=_params(("arbitrary",)),
    )(dres, dres, w_pool, scale, yu, x, g)


def _pad_rows(w, rows):
    pad = [(0, 0)] * (w.ndim - 2) + [(0, rows - w.shape[-2]), (0, 0)]
    return jnp.pad(w, pad)


def _ffn_layer_fwd(x, nf, w8, wc8, wd4, tag, head=None):
    h, up8 = _ffn_up(x, nf, w8, name=f"ffn{tag}_up")
    res = _ffn_mid_down(up8, wc8, wd4, x, head, name=f"ffn{tag}_mid_down")
    act4, ug4, uv4 = res[-3:]
    out = res[0] if head is None else tuple(res[:4])
    return out, (h, up8, ug4, uv4, act4)


def _ffn_layer_bwd(d, db, x, nf, w8, wc8, wd4, saved, tag, grads_ready):
    h, up8, ug4, uv4, act4 = saved
    dact4 = _ffn_dact(db, wd4, name=f"ffn{tag}_dact")
    dwd4 = _ffn_dwdown(act4, db, name=f"ffn{tag}_dwdown")
    dupg, dupv, dwg, dwv = _ffn_midbwd(up8, ug4, uv4, dact4, wc8, name=f"ffn{tag}_midbwd")
    dw8 = _ffn_dwup(h, dupg, dupv, name=f"ffn{tag}_dwup")
    sent = grads_ready(f"ffn{tag}", {f"w_up{tag}": dw8, f"w_down{tag}": dwd4})
    dx, dxb, dnf = _ffn_dh(dupg, dupv, w8, x, nf, d, sent, name=f"ffn{tag}_dh")
    dwc8 = jnp.concatenate([dwg, dwv], axis=0)[:, :3]
    return dx, dxb, dnf, dwc8


def _local_step(x, target, weights, grads_ready):
    w0 = weights("mix0", x)
    wa = _pad_rows(w0["conv_a"], 32)
    wb = _pad_rows(w0["conv_b"], 8)
    wc = [_pad_rows(w0["conv_ffn"][l], 8) for l in range(2)]
    h0, z = _rms_mm(x, w0["norm_mix_even"], w0["w_in_t"], name="mix0_in", out_dtype=F32, tn=2560)
    ab, ca, x1 = _mix0_fwd(z, wa, w0["ln_a_g"], w0["ln_a_b"], wb, w0["w_out"], x, name="mix0_mid")
    w1 = weights("ffn0", x1)
    x2, ffn0 = _ffn_layer_fwd(x1, w0["norm_ffn"][0:1], w1["w_up"], wc[0], w1["w_down"], 0)
    w2 = weights("ffn1", x2)
    x3, p, yu = _pool_fwd(x2, w0["norm_mix_odd"], w2["w_pool"], w0["pool_scale"], name="pool_fwd")
    (loss, d4, d4b, g_norm_final), ffn1 = _ffn_layer_fwd(
        x3, w0["norm_ffn"][1:2], w2["w_up"], wc[1], w2["w_down"], 1, head=(w0["norm_final"], target))

    d3, d3b, g_nf1, g_wc1 = _ffn_layer_bwd(
        d4, d4b, x3, w0["norm_ffn"][1:2], w2["w_up"], wc[1], w2["w_down"], ffn1, 1, grads_ready)
    d2, d2b, g_nmo, dyc, g_scale = _pool_bwd(d3, w2["w_pool"], w0["pool_scale"], yu, x2, w0["norm_mix_odd"],
                                             name="pool_bwd")
    g_pool = _pool_dw(p, dyc, name="pool_dw")
    d1, d1b, g_nf0, g_wc0 = _ffn_layer_bwd(
        d2, d2b, x1, w0["norm_ffn"][0:1], w1["w_up"], wc[0], w1["w_down"], ffn0, 0, grads_ready)
    dab = _mm(d1b, w0["w_out"], tb=True, name="mix0_dab", tm=1024, tn=1024)
    g_out = _mm(ab, d1b, ta=True, out_dtype=BF16, name="mix0_dwout", tm=1024, tn=512)
    dz, g_wa, g_wb, g_lg, g_lb = _mix0_bwd(z, ca, dab, wa, w0["ln_a_g"], w0["ln_a_b"], wb, name="mix0_midbwd")
    g_in = _mm(dz, h0, ta=True, out_dtype=BF16, name="mix0_dwin", tm=512, tn=1024)
    sent_mix = grads_ready("mix0", {"w_in": g_in, "w_out": g_out, "w_pool": g_pool})
    dx, g_nme = _mm_rms_bwd(dz, w0["w_in_t"], x, w0["norm_mix_even"], d1, sent_mix, name="mix0_dh")

    small = {
        "norm_mix_even": g_nme, "conv_a": g_wa[:A_TAPS], "ln_a_g": g_lg, "ln_a_b": g_lb, "conv_b": g_wb[:3],
        "norm_mix_odd": g_nmo, "pool_scale": g_scale, "norm_ffn": jnp.concatenate([g_nf0, g_nf1], axis=0),
        "conv_ffn": [g_wc0, g_wc1], "norm_final": g_norm_final,
    }
    return loss[0, 0], dx, small


def _my_pos():
    return lax.axis_index("x"), lax.axis_index("y"), lax.axis_index("c")


def _flip(pos, r):
    x, y, c = pos
    return (1 - x if r & 4 else x, 1 - y if r & 2 else y, 1 - c if r & 1 else c)


def _dev_index(pos):
    return 4 * pos[0] + 2 * pos[1] + pos[2]


_HBM = pl.BlockSpec(memory_space=pltpu.HBM)
_SEM = pl.BlockSpec(memory_space=pltpu.SEMAPHORE)
_EFFECT = pltpu.SideEffectType.DATAFLOW_SIDE_EFFECTING


def _exchange_copy(ins, lands, send_sems, recv_sems, scatter, pos, a, r, receiving):
    me = _dev_index(pos)
    peer = _flip(pos, r)
    dest = _dev_index(pos) if receiving else _dev_index(peer)
    src = ins[a].at[dest] if scatter[a] else ins[a]
    slot = _dev_index(peer) if receiving else me
    return pltpu.make_async_remote_copy(
        src_ref=src, dst_ref=lands[a].at[slot], send_sem=send_sems.at[a * (N_DEV - 1) + r - 1],
        recv_sem=recv_sems.at[a * (N_DEV - 1) + r - 1],
        device_id=peer, device_id_type=pl.DeviceIdType.MESH)


ALL_PEERS = tuple(range(1, N_DEV))
CHIP_PEERS = (1, 2, 4, 6)
FORWARDED = (2, 4, 6)


def _exchange_start(arrays, scatter, after, *, name, peers=ALL_PEERS):
    n = len(arrays)
    me = _dev_index(_my_pos())
    lands = []
    for arr, sc in zip(arrays, scatter):
        own = lax.dynamic_index_in_dim(arr, me, 0, keepdims=True) if sc else arr[None]
        shape = arr.shape if sc else (N_DEV,) + arr.shape
        lands.append(lax.dynamic_update_slice(lax.empty(shape, arr.dtype), own, (me,) + (0,) * (len(shape) - 1)))

    def body(*refs):
        ins, lnd = refs[:n], refs[n:2 * n]
        send_sems, recv_sems = refs[2 * n + 1], refs[2 * n + 2]
        token = refs[-1]
        pos = _my_pos()
        for a in range(n):
            for r in peers:
                _exchange_copy(ins, lnd, send_sems, recv_sems, scatter, pos, a, r, receiving=False).start()
        token[...] = jnp.zeros_like(token)

    bufs = [pltpu.with_memory_space_constraint(t, pltpu.HBM) for t in list(arrays) + lands]
    sems = pltpu.SemaphoreType.DMA((n * (N_DEV - 1),))
    res = pl.pallas_call(
        body, name=name,
        out_shape=(sems, sems, *[pltpu.HBM(t.shape, t.dtype) for t in bufs], jax.ShapeDtypeStruct((8, 128), F32)),
        in_specs=[_HBM] * (2 * n) + [pl.BlockSpec(memory_space=pl.ANY)],
        out_specs=(_SEM, _SEM, *[_HBM] * (2 * n), pl.BlockSpec(memory_space=pltpu.VMEM)),
        input_output_aliases={i: 2 + i for i in range(2 * n)},
        compiler_params=pltpu.CompilerParams(has_side_effects=_EFFECT),
    )(*bufs, after)
    return res[0], res[1], list(res[2:2 + n]), list(res[2 + n:2 + 2 * n]), res[-1]


def _exchange_wait(started, scatter, after, *, name, peers=ALL_PEERS):
    send_sems, recv_sems, arrays, lands, _ = started
    n = len(arrays)

    def body(*refs):
        ins, lnd = refs[:n], refs[n:2 * n]
        send_sems, recv_sems = refs[2 * n], refs[2 * n + 1]
        pos = _my_pos()
        for a in range(n):
            for r in peers:
                _exchange_copy(ins, lnd, send_sems, recv_sems, scatter, pos, a, r, receiving=False).wait_send()
                _exchange_copy(ins, lnd, send_sems, recv_sems, scatter, pos, a, r, receiving=True).wait_recv()

    bufs = list(arrays) + list(lands)
    after = list(after) if isinstance(after, (list, tuple)) else [after]
    res = pl.pallas_call(
        body, name=name,
        out_shape=tuple(pltpu.HBM(t.shape, t.dtype) for t in bufs),
        in_specs=[_HBM] * (2 * n) + [_SEM, _SEM] + [pl.BlockSpec(memory_space=pl.ANY)] * len(after),
        out_specs=tuple([_HBM] * (2 * n)),
        input_output_aliases={i: i for i in range(2 * n)},
        compiler_params=pltpu.CompilerParams(has_side_effects=_EFFECT),
    )(*bufs, send_sems, recv_sems, *after)
    return list(res[n:])


def _forward_copy(lands, send_sems, recv_sems, pos, a, q, receiving):
    slot = _dev_index(_flip(pos, q ^ 1 if receiving else q))
    idx = a * len(FORWARDED) + FORWARDED.index(q)
    return pltpu.make_async_remote_copy(
        src_ref=lands[a].at[slot], dst_ref=lands[a].at[slot], send_sem=send_sems.at[idx], recv_sem=recv_sems.at[idx],
        device_id=_flip(pos, 1), device_id_type=pl.DeviceIdType.MESH)


def _forward_start(lands, after, *, name):
    n = len(lands)

    def body(*refs):
        lnd = refs[:n]
        send_sems, recv_sems = refs[n + 1], refs[n + 2]
        token = refs[-1]
        pos = _my_pos()
        for a in range(n):
            for q in FORWARDED:
                _forward_copy(lnd, send_sems, recv_sems, pos, a, q, receiving=False).start()
        token[...] = jnp.zeros_like(token)

    sems = pltpu.SemaphoreType.DMA((n * len(FORWARDED),))
    res = pl.pallas_call(
        body, name=name,
        out_shape=(sems, sems, *[pltpu.HBM(t.shape, t.dtype) for t in lands], jax.ShapeDtypeStruct((8, 128), F32)),
        in_specs=[_HBM] * n + [pl.BlockSpec(memory_space=pl.ANY)],
        out_specs=(_SEM, _SEM, *[_HBM] * n, pl.BlockSpec(memory_space=pltpu.VMEM)),
        input_output_aliases={i: 2 + i for i in range(n)},
        compiler_params=pltpu.CompilerParams(has_side_effects=_EFFECT),
    )(*lands, after)
    return res[0], res[1], list(res[2:2 + n]), res[-1]


def _forward_wait(forwarded, after, *, name):
    send_sems, recv_sems, lands, _ = forwarded
    n = len(lands)

    def body(*refs):
        lnd = refs[:n]
        send_sems, recv_sems = refs[n], refs[n + 1]
        pos = _my_pos()
        for a in range(n):
            for q in FORWARDED:
                _forward_copy(lnd, send_sems, recv_sems, pos, a, q, receiving=False).wait_send()
                _forward_copy(lnd, send_sems, recv_sems, pos, a, q, receiving=True).wait_recv()

    res = pl.pallas_call(
        body, name=name,
        out_shape=tuple(pltpu.HBM(t.shape, t.dtype) for t in lands),
        in_specs=[_HBM] * n + [_SEM, _SEM, pl.BlockSpec(memory_space=pl.ANY)],
        out_specs=tuple([_HBM] * n),
        input_output_aliases={i: i for i in range(n)},
        compiler_params=pltpu.CompilerParams(has_side_effects=_EFFECT),
    )(*lands, send_sems, recv_sems, after)
    return list(res)


def _adamw_update(p_ref, w_ref, m_ref, v_ref, g_ref, d_ref, mo_ref, vo_ref):
    g = p_ref[0].astype(F32)
    for k in range(1, N_DEV):
        g = g + p_ref[k].astype(F32)
    mn = ADAM_B1 * m_ref[...] + (1.0 - ADAM_B1) * g
    vn = ADAM_B2 * v_ref[...] + (1.0 - ADAM_B2) * (g * g)
    m_hat = mn / (1.0 - ADAM_B1 ** ADAM_STEP)
    v_hat = vn / (1.0 - ADAM_B2 ** ADAM_STEP)
    g_ref[...] = g
    d_ref[...] = -ADAM_LR * (m_hat / (jnp.sqrt(v_hat) + ADAM_EPS) + ADAM_WD * w_ref[...])
    mo_ref[...] = mn
    vo_ref[...] = vn


def _adamw_small(parts, ws, ms, vs, *, name):
    n = len(ws)

    def body(*refs):
        ins, outs = refs[:4 * n], refs[4 * n:]
        for j in range(n):
            _adamw_update(ins[j], ins[n + j], ins[2 * n + j], ins[3 * n + j], *outs[4 * j:4 * j + 4])

    return pl.pallas_call(
        body, name=name,
        out_shape=[jax.ShapeDtypeStruct(w.shape, F32) for w in ws for _ in range(4)],
    )(*parts, *ws, *ms, *vs)


def _adamw(parts, w, m, v, *, name, tr):
    nl, r, c = w.shape
    assert len(parts) == nl and r % tr == 0
    n_i = r // tr

    def body(*refs):
        p_refs = refs[:nl]
        w_ref, m_ref, v_ref = refs[nl:nl + 3]

        def update(p_ref):
            _adamw_update(p_ref, w_ref, m_ref, v_ref, *refs[nl + 3:])

        if nl == 1:
            update(p_refs[0])
        else:
            for layer in range(nl):
                pl.when(pl.program_id(0) == layer)(lambda p_ref=p_refs[layer]: update(p_ref))

    def parts_spec(layer):
        def index(l, i):
            return (0, jnp.where(l < layer, 0, jnp.where(l > layer, n_i - 1, i)), 0)
        return pl.BlockSpec((N_DEV, tr, c), index)

    blk = pl.BlockSpec((None, tr, c), lambda l, i: (l, i, 0))
    return pl.pallas_call(
        body, name=name, grid=(nl, n_i),
        in_specs=[parts_spec(layer) for layer in range(nl)] + [blk, blk, blk],
        out_specs=[blk] * 4, out_shape=[jax.ShapeDtypeStruct((nl, r, c), F32)] * 4,
        compiler_params=_params(("arbitrary", "arbitrary")),
    )(*parts, w, m, v)


def _pack(parts, lead=()):
    flat = jnp.concatenate([p.reshape(lead + (-1,)) for p in parts], axis=-1)
    n = flat.shape[-1]
    rows = -(-n // (8 * 128)) * 8
    flat = jnp.pad(flat, [(0, 0)] * len(lead) + [(0, rows * 128 - n)])
    return flat.reshape(lead + (rows, 128))


def _to_dev_major(g, axis):
    shp = g.shape
    g = g.reshape(shp[:axis] + (N_DEV, shp[axis] // N_DEV) + shp[axis + 1:])
    return jnp.moveaxis(g, axis, 0)


def _from_dev_major(g, axis):
    g = jnp.moveaxis(g, 0, axis)
    shp = g.shape
    return g.reshape(shp[:axis] + (shp[axis] * shp[axis + 1],) + shp[axis + 2:])


SMALL_SHARDED = ("conv_a", "conv_b", "norm_mix_odd", "pool_scale", "conv_ffn_w")
SMALL_REPLICATED = ("norm_mix_even", "ln_a_g", "ln_a_b", "norm_ffn", "norm_final")
BIG = {"w_in": ("w_in", 0, 320), "w_out": ("w_out", 0, 128), "w_pool": ("w_pool", 0, 128),
       "w_up0": ("w_up", 0, 352), "w_up1": ("w_up", 1, 352),
       "w_down0": ("w_down", 0, 352), "w_down1": ("w_down", 1, 352)}
COLUMN_SHARDED = ("w_in", "w_up")


def kernel(x, norm_mix_even, w_in, conv_a, ln_a_g, ln_a_b, conv_b, w_out, norm_mix_odd, w_pool, pool_scale, norm_ffn, w_up, conv_ffn_w, w_down, norm_final, loss_target, m_norm_mix_even, m_w_in, m_conv_a, m_ln_a_g, m_ln_a_b, m_conv_b, m_w_out, m_norm_mix_odd, m_w_pool, m_pool_scale, m_norm_ffn, m_w_up, m_conv_ffn_w, m_w_down, m_norm_final, v_norm_mix_even, v_w_in, v_conv_a, v_ln_a_g, v_ln_a_b, v_conv_b, v_w_out, v_norm_mix_odd, v_w_pool, v_pool_scale, v_norm_ffn, v_w_up, v_conv_ffn_w, v_w_down, v_norm_final):
    names = ("norm_mix_even", "w_in", "conv_a", "ln_a_g", "ln_a_b", "conv_b", "w_out", "norm_mix_odd", "w_pool",
             "pool_scale", "norm_ffn", "w_up", "conv_ffn_w", "w_down", "norm_final")
    wts = dict(zip(names, (norm_mix_even, w_in, conv_a, ln_a_g, ln_a_b, conv_b, w_out, norm_mix_odd, w_pool,
                           pool_scale, norm_ffn, w_up, conv_ffn_w, w_down, norm_final)))
    mom = dict(zip(names, (m_norm_mix_even, m_w_in, m_conv_a, m_ln_a_g, m_ln_a_b, m_conv_b, m_w_out, m_norm_mix_odd,
                           m_w_pool, m_pool_scale, m_norm_ffn, m_w_up, m_conv_ffn_w, m_w_down, m_norm_final)))
    var = dict(zip(names, (v_norm_mix_even, v_w_in, v_conv_a, v_ln_a_g, v_ln_a_b, v_conv_b, v_w_out, v_norm_mix_odd,
                           v_w_pool, v_pool_scale, v_norm_ffn, v_w_up, v_conv_ffn_w, v_w_down, v_norm_final)))
    d = x.shape[-1]

    def shard3d(t, param):
        a = t[param]
        if param in COLUMN_SHARDED:
            return jnp.swapaxes(a, 1, 2)
        return a.reshape(a.shape[0], -1, a.shape[-1])

    def unshard3d(a, param):
        if param in COLUMN_SHARDED:
            return jnp.swapaxes(a, 1, 2)
        return a.reshape(wts[param].shape)

    def shard2d(t, key):
        param, layer, _ = BIG[key]
        return shard3d(t, param)[layer]

    small_w = _pack([wts[k] for k in SMALL_SHARDED])
    bf = {k: shard2d(wts, k).astype(BF16) for k in BIG}
    gather_groups = {"mix0": ("w_in", "w_out", "small"), "ffn0": ("w_up0", "w_down0"),
                     "ffn1": ("w_pool", "w_up1", "w_down1")}
    order = list(gather_groups)
    started = {}

    def start_gather(grp, after):
        arrs = [small_w if k == "small" else bf[k] for k in gather_groups[grp]]
        started[grp] = _exchange_start(arrs, [False] * len(arrs), after, name=f"gather_{grp}_start", peers=CHIP_PEERS)

    start_gather(order[0], small_w)

    def weights(grp, after):
        keys = gather_groups[grp]
        lands = _exchange_wait(started[grp], [False] * len(keys), after, name=f"gather_{grp}_wait", peers=CHIP_PEERS)
        forwarded = _forward_start(lands, small_w, name=f"gather_{grp}_forward")
        if grp != order[-1]:
            start_gather(order[order.index(grp) + 1], forwarded[-1])
            behind = started[order[order.index(grp) + 1]][-1]
        else:
            behind = forwarded[-1]
        gw = dict(zip(keys, _forward_wait(forwarded, behind, name=f"gather_{grp}_forward_wait")))
        if grp == "ffn0":
            return {"w_up": gw["w_up0"], "w_down": gw["w_down0"].reshape(N_PAIR, -1, d)}
        if grp == "ffn1":
            return {"w_up": gw["w_up1"], "w_down": gw["w_down1"].reshape(N_PAIR, -1, d),
                    "w_pool": _from_dev_major(gw["w_pool"].reshape(N_DEV, len(POOL_WINDOWS), -1, POOL_GROUP), 1)}
        per_dev = gw["small"].reshape(N_DEV, -1)
        sizes = [wts[k].size for k in SMALL_SHARDED]
        offs = [sum(sizes[:i]) for i in range(len(sizes))]
        small_full = {k: per_dev[:, o:o + n_].reshape((N_DEV,) + wts[k].shape)
                      for k, o, n_ in zip(SMALL_SHARDED, offs, sizes)}
        return {
            "norm_mix_even": norm_mix_even, "ln_a_g": ln_a_g, "ln_a_b": ln_a_b, "norm_ffn": norm_ffn,
            "norm_final": norm_final[None],
            "w_in_t": gw["w_in"].reshape(-1, d),
            "w_out": gw["w_out"].reshape(-1, d),
            "conv_a": _from_dev_major(small_full["conv_a"][:, 0], 1),
            "conv_b": _from_dev_major(small_full["conv_b"][:, 0], 1),
            "norm_mix_odd": _from_dev_major(small_full["norm_mix_odd"], 1),
            "pool_scale": _from_dev_major(small_full["pool_scale"], 1),
            "conv_ffn": [small_full["conv_ffn_w"][:, l] for l in range(2)],
        }

    def dev_major(k, g):
        if k == "w_pool":
            return _to_dev_major(g, 1).reshape(N_DEV, -1, POOL_GROUP)
        return g.reshape(N_DEV, -1, g.shape[-1])

    sent = {}

    def grads_ready(grp, grads):
        keys = tuple(grads)
        parts = [dev_major(k, grads[k]) for k in keys]
        sent[grp] = (keys, _exchange_start(parts, [True] * len(keys), small_w, name=f"grads_{grp}_start"))
        return sent[grp][1][-1]

    loss, dx, g = _local_step(x[0], loss_target[0], weights, grads_ready)

    def small2d(a):
        return a.reshape(-1, a.shape[-1])

    small_keys = SMALL_SHARDED + SMALL_REPLICATED
    small_parts = [_to_dev_major(g["conv_a"], 1), _to_dev_major(g["conv_b"], 1),
                   _to_dev_major(g["norm_mix_odd"], 1), _to_dev_major(g["pool_scale"], 1),
                   jnp.stack(g["conv_ffn"], axis=1).reshape(N_DEV, -1, w_up.shape[-1])]
    small_parts += [g[k] for k in SMALL_REPLICATED]
    small_scatter = [True] * len(SMALL_SHARDED) + [False] * len(SMALL_REPLICATED)
    small_parts.append(jnp.full((1, 128), loss, F32))
    small_scatter.append(False)
    sent["small"] = (small_keys + ("loss",),
                     _exchange_start(small_parts, small_scatter, dx, name="grads_small_start"))

    landed, out, raw = {}, {}, {}

    def wait_grads(grp, after, scat=None):
        keys, st = sent[grp]
        scat = [True] * len(keys) if scat is None else scat
        landed.update(zip(keys, _exchange_wait(st, scat, after, name=f"grads_{grp}_wait")))

    def update(param, keys):
        raw[param] = _adamw([landed[k] for k in keys], shard3d(wts, param), shard3d(mom, param),
                            shard3d(var, param), name=f"adamw_{param}", tr=BIG[keys[0]][2])
        out[param] = [unshard3d(t, param) for t in raw[param]]

    wait_grads("ffn1", sent["small"][1][-1])
    wait_grads("ffn0", landed["w_up1"])
    update("w_up", ("w_up0", "w_up1"))
    update("w_down", ("w_down0", "w_down1"))
    wait_grads("mix0", [raw["w_up"][1], raw["w_down"][1]])
    update("w_in", ("w_in",))
    update("w_out", ("w_out",))
    update("w_pool", ("w_pool",))
    wait_grads("small", raw["w_pool"][1], small_scatter)
    res = _adamw_small([landed[k] for k in small_keys], [small2d(wts[k]) for k in small_keys],
                       [small2d(mom[k]) for k in small_keys], [small2d(var[k]) for k in small_keys],
                       name="adamw_small")
    for j, k in enumerate(small_keys):
        out[k] = [t.reshape(wts[k].shape) for t in res[4 * j:4 * j + 4]]

    loss = jnp.sum(landed["loss"][:, 0, 0])
    return (loss, dx[None], *[out[k][0] for k in names], *[out[k][1] for k in names],
            *[out[k][2] for k in names], *[out[k][3] for k in names])
```

```python
import jax
import jax.numpy as jnp
from jax import lax
from jax.experimental import pallas as pl
from jax.experimental.pallas import tpu as pltpu

F32 = jnp.float32
BF16 = jnp.bfloat16

RMS_EPS = 1e-6
LN_EPS = 1e-5
ADAM_LR = 0.001
ADAM_B1 = 0.9
ADAM_B2 = 0.999
ADAM_EPS = 1e-08
ADAM_WD = 0.01
ADAM_STEP = 10

N_DEV = 8
N_PAIR = N_DEV // 2
A_WIDTH = 512
A_TAPS = 31
POOL_WINDOWS = (2, 4, 8, 16)
POOL_GROUP = 256
HALO_A = 32
HALO_S = 16
VMEM_LIMIT = 56 * 1024 * 1024


def _params(sem, vmem=VMEM_LIMIT):
    return pltpu.CompilerParams(dimension_semantics=sem, vmem_limit_bytes=vmem)


def _sigmoid(x):
    return 0.5 * jnp.tanh(0.5 * x) + 0.5


def _prev_blk(i, ts, hb):
    return jnp.maximum(i * (ts // hb) - 1, 0)


def _next_blk(i, ts, hb, s):
    return jnp.minimum((i + 1) * (ts // hb), s // hb - 1)


def _mm(a, b, *, name, ta=False, tb=False, out_dtype=F32, tm=512, tn=512):
    m, k = (a.shape[1], a.shape[0]) if ta else a.shape
    n = b.shape[0] if tb else b.shape[1]
    tm, tn = min(tm, m), min(tn, n)
    assert m % tm == 0 and n % tn == 0, (name, m, n, tm, tn)
    dims = (((0,) if ta else (1,), (1,) if tb else (0,)), ((), ()))

    def body(a_ref, b_ref, o_ref):
        o_ref[...] = lax.dot_general(a_ref[...], b_ref[...], dims, preferred_element_type=F32).astype(out_dtype)

    a_spec = pl.BlockSpec((k, tm), lambda i, j: (0, i)) if ta else pl.BlockSpec((tm, k), lambda i, j: (i, 0))
    b_spec = pl.BlockSpec((tn, k), lambda i, j: (j, 0)) if tb else pl.BlockSpec((k, tn), lambda i, j: (0, j))
    return pl.pallas_call(
        body, name=name, grid=(m // tm, n // tn),
        in_specs=[a_spec, b_spec], out_specs=pl.BlockSpec((tm, tn), lambda i, j: (i, j)),
        out_shape=jax.ShapeDtypeStruct((m, n), out_dtype),
        compiler_params=_params(("parallel", "parallel")),
    )(a, b)


def _rms_rows(xv, gv):
    return xv * lax.rsqrt(jnp.mean(xv * xv, axis=-1, keepdims=True) + RMS_EPS) * gv


_NT = (((1,), (1,)), ((), ()))
_TN = (((0,), (0,)), ((), ()))


def _rms_mm(x, g, wt, *, name, out_dtype, tm=1024, tn=512):
    s, d = x.shape
    n = wt.shape[0]
    tm = min(tm, s)
    assert s % tm == 0 and n % tn == 0

    def body(x_ref, g_ref, w_ref, h_ref, z_ref, hs_ref):
        @pl.when(pl.program_id(1) == 0)
        def _():
            hv = _rms_rows(x_ref[...], g_ref[...]).astype(BF16)
            hs_ref[...] = hv
            h_ref[...] = hv

        z_ref[...] = lax.dot_general(hs_ref[...], w_ref[...], _NT, preferred_element_type=F32).astype(out_dtype)

    return pl.pallas_call(
        body, name=name, grid=(s // tm, n // tn),
        in_specs=[pl.BlockSpec((tm, d), lambda i, j: (i, 0)),
                  pl.BlockSpec((1, d), lambda i, j: (0, 0)),
                  pl.BlockSpec((tn, d), lambda i, j: (j, 0))],
        out_specs=[pl.BlockSpec((tm, d), lambda i, j: (i, 0)),
                   pl.BlockSpec((tm, tn), lambda i, j: (i, j))],
        out_shape=[jax.ShapeDtypeStruct((s, d), BF16), jax.ShapeDtypeStruct((s, n), out_dtype)],
        scratch_shapes=[pltpu.VMEM((tm, d), BF16)],
        compiler_params=_params(("parallel", "arbitrary")),
    )(x, g, wt)


def _rms_bwd_rows(xv, gv, dh):
    r = lax.rsqrt(jnp.mean(xv * xv, axis=-1, keepdims=True) + RMS_EPS)
    xh = xv * r
    dn = dh * gv
    dx = r * (dn - xh * jnp.mean(dn * xh, axis=-1, keepdims=True))
    return dx, dh * xh


def _mm_rms_bwd(a, b, x, g, dres, dep, *, name, tm=512):
    s, k = a.shape
    d = b.shape[1]
    tm = min(tm, s)

    def body(a_ref, b_ref, x_ref, g_ref, dres_ref, dep_ref, dx_ref, dg_ref):
        @pl.when(pl.program_id(0) == 0)
        def _():
            dg_ref[...] = jnp.zeros_like(dg_ref)

        dh = jnp.dot(a_ref[...], b_ref[...], preferred_element_type=F32)
        dx, dgr = _rms_bwd_rows(x_ref[...], g_ref[...], dh)
        dx_ref[...] = dres_ref[...] + dx
        dg_ref[...] += jnp.sum(dgr, axis=0, keepdims=True)

    row = pl.BlockSpec((tm, d), lambda i: (i, 0))
    vec = pl.BlockSpec((1, d), lambda i: (0, 0))
    return pl.pallas_call(
        body, name=name, grid=(s // tm,),
        in_specs=[pl.BlockSpec((tm, k), lambda i: (i, 0)), pl.BlockSpec((k, d), lambda i: (0, 0)), row, vec, row,
                  pl.BlockSpec(memory_space=pl.ANY)],
        out_specs=[row, vec],
        out_shape=[jax.ShapeDtypeStruct((s, d), F32), jax.ShapeDtypeStruct((1, d), F32)],
        compiler_params=_params(("arbitrary",)),
    )(a, b, x, g, dres, dep)


def _conv_taps(ext_ref, w_ref, n_taps, base, r0, rows, reverse=False):
    acc = None
    for k in range(n_taps):
        off = r0 + (base - k if reverse else base + k)
        term = w_ref[k:k + 1, :] * ext_ref[pl.ds(off, rows), :]
        acc = term if acc is None else acc + term
    return acc


def _shift_copies(src_ref, sh_ref, rows):
    for b in range(1, 8):
        sh_ref[b, 0:rows, :] = src_ref[pl.ds(b, rows), :]


def _shifted(src_ref, sh_ref, start, rows, off):
    a, b = divmod(off, 8)
    ref = src_ref if b == 0 else sh_ref.at[b]
    return ref[pl.ds(start + 8 * a, rows), :]


def _mix0_fwd(z, conv_a, ln_g, ln_b, conv_b, w_out, x, *, name, ts=512, rc=32):
    s = z.shape[0]
    c = A_WIDTH
    hb = HALO_A
    half = ts // 2

    def body(z_ref, zp_ref, wa_ref, lg_ref, lb_ref, wb_ref, wo_ref, x_ref, ab_ref, ca_ref, o_ref, exta, extb, sha):
        keep = jnp.where(pl.program_id(0) > 0, 1.0, 0.0)
        zp = zp_ref[...]
        exta[0:hb, :] = zp[:, 0:c] * _sigmoid(zp[:, c:2 * c]) * keep
        extb[0:hb, :] = zp[:, 3 * c:4 * c] * zp[:, 4 * c:5 * c] * keep
        exta[hb:hb + ts, :] = z_ref[:, 0:c] * _sigmoid(z_ref[:, c:2 * c])
        extb[hb:hb + ts, :] = z_ref[:, 3 * c:4 * c] * z_ref[:, 4 * c:5 * c]
        _shift_copies(exta, sha, hb + ts - 8)
        lg = lg_ref[...]
        lb = lb_ref[...]
        for q in range(ts // rc):
            r0 = q * rc
            ca = None
            for k in range(A_TAPS):
                term = wa_ref[k:k + 1, :] * _shifted(exta, sha, r0, rc, hb - (A_TAPS - 1) + k)
                ca = term if ca is None else ca + term
            ca_ref[r0:r0 + rc, :] = ca
            mu = jnp.mean(ca, axis=-1, keepdims=True)
            xc = ca - mu
            rs = lax.rsqrt(jnp.mean(xc * xc, axis=-1, keepdims=True) + LN_EPS)
            l = xc * rs * lg + lb
            ab_ref[r0:r0 + rc, 0:c] = (l * _sigmoid(l)).astype(BF16)
            cbc = _conv_taps(extb, wb_ref, 3, hb - 2, r0, rc)
            ab_ref[r0:r0 + rc, c:2 * c] = (z_ref[r0:r0 + rc, 2 * c:3 * c] * cbc).astype(BF16)
            if (r0 + rc) % half == 0:
                rows = slice(r0 + rc - half, r0 + rc)
                o_ref[rows, :] = x_ref[rows, :] + jnp.dot(ab_ref[rows, :], wo_ref[...], preferred_element_type=F32)

    d = x.shape[1]
    row = pl.BlockSpec((ts, d), lambda i: (i, 0))
    return pl.pallas_call(
        body, name=name, grid=(s // ts,),
        in_specs=[pl.BlockSpec((ts, 5 * c), lambda i: (i, 0)),
                  pl.BlockSpec((hb, 5 * c), lambda i: (_prev_blk(i, ts, hb), 0)),
                  pl.BlockSpec((32, c), lambda i: (0, 0)),
                  pl.BlockSpec((1, c), lambda i: (0, 0)),
                  pl.BlockSpec((1, c), lambda i: (0, 0)),
                  pl.BlockSpec((8, c), lambda i: (0, 0)),
                  pl.BlockSpec((2 * c, d), lambda i: (0, 0)), row],
        out_specs=[pl.BlockSpec((ts, 2 * c), lambda i: (i, 0)),
                   pl.BlockSpec((ts, c), lambda i: (i, 0)), row],
        out_shape=[jax.ShapeDtypeStruct((s, 2 * c), BF16), jax.ShapeDtypeStruct((s, c), F32),
                   jax.ShapeDtypeStruct((s, d), F32)],
        scratch_shapes=[pltpu.VMEM((hb + ts, c), F32), pltpu.VMEM((hb + ts, c), F32),
                        pltpu.VMEM((8, hb + ts - 8, c), F32)],
        compiler_params=_params(("parallel",)),
    )(z, z, conv_a, ln_g, ln_b, conv_b, w_out, x)


def _mix0_bwd(z, ca, dab, conv_a, ln_g, ln_b, conv_b, *, name, ts=512, rc=32):
    s = z.shape[0]
    c = A_WIDTH
    hb = HALO_A
    ta = A_TAPS

    def body(z_ref, zp_ref, zn_ref, ca_ref, can_ref, d_ref, dn_ref, wa_ref, lg_ref, lb_ref, wb_ref,
             dz_ref, dwa_ref, dwb_ref, dlg_ref, dlb_ref, exta, extb, extdca, extdcb, shd):
        i = pl.program_id(0)
        keep_p = jnp.where(i > 0, 1.0, 0.0)
        keep_n = jnp.where(i < s // ts - 1, 1.0, 0.0)

        @pl.when(i == 0)
        def _():
            dwa_ref[...] = jnp.zeros_like(dwa_ref)
            dwb_ref[...] = jnp.zeros_like(dwb_ref)
            dlg_ref[...] = jnp.zeros_like(dlg_ref)
            dlb_ref[...] = jnp.zeros_like(dlb_ref)

        lg = lg_ref[...]
        lb = lb_ref[...]
        zp = zp_ref[...]
        exta[0:hb, :] = zp[:, 0:c] * _sigmoid(zp[:, c:2 * c]) * keep_p
        extb[0:hb, :] = zp[:, 3 * c:4 * c] * zp[:, 4 * c:5 * c] * keep_p
        exta[hb:hb + ts, :] = z_ref[:, 0:c] * _sigmoid(z_ref[:, c:2 * c])
        extb[hb:hb + ts, :] = z_ref[:, 3 * c:4 * c] * z_ref[:, 4 * c:5 * c]

        def ln_bwd(cav, dav):
            mu = jnp.mean(cav, axis=-1, keepdims=True)
            xc = cav - mu
            rs = lax.rsqrt(jnp.mean(xc * xc, axis=-1, keepdims=True) + LN_EPS)
            nv = xc * rs
            l = nv * lg + lb
            sg = _sigmoid(l)
            dl = dav * (sg * (1.0 + l * (1.0 - sg)))
            dnv = dl * lg
            dca = rs * (dnv - jnp.mean(dnv, axis=-1, keepdims=True)
                        - nv * jnp.mean(dnv * nv, axis=-1, keepdims=True))
            return dca, dl, nv

        dlg_acc = jnp.zeros((1, c), F32)
        dlb_acc = jnp.zeros((1, c), F32)
        for q in range(ts // rc):
            r0 = q * rc
            dca, dl, nv = ln_bwd(ca_ref[r0:r0 + rc, :], d_ref[r0:r0 + rc, 0:c])
            extdca[r0:r0 + rc, :] = dca
            dlg_acc = dlg_acc + jnp.sum(dl * nv, axis=0, keepdims=True)
            dlb_acc = dlb_acc + jnp.sum(dl, axis=0, keepdims=True)
            extdcb[r0:r0 + rc, :] = d_ref[r0:r0 + rc, c:2 * c] * z_ref[r0:r0 + rc, 2 * c:3 * c]
        dca_n, _, _ = ln_bwd(can_ref[...], dn_ref[:, 0:c])
        extdca[ts:ts + hb, :] = dca_n * keep_n
        extdcb[ts:ts + hb, :] = dn_ref[:, c:2 * c] * zn_ref[:, 2 * c:3 * c] * keep_n
        dlg_ref[...] += dlg_acc
        dlb_ref[...] += dlb_acc
        _shift_copies(extdca, shd, ts + hb - 8)

        for q in range(ts // rc):
            r0 = q * rc
            zr = z_ref[r0:r0 + rc, :]
            dga = None
            for k in range(ta):
                term = wa_ref[k:k + 1, :] * _shifted(extdca, shd, r0, rc, ta - 1 - k)
                dga = term if dga is None else dga + term
            sg = _sigmoid(zr[:, c:2 * c])
            dz_ref[r0:r0 + rc, 0:c] = (dga * sg).astype(BF16)
            dz_ref[r0:r0 + rc, c:2 * c] = (dga * zr[:, 0:c] * sg * (1.0 - sg)).astype(BF16)
            cbc = _conv_taps(extb, wb_ref, 3, hb - 2, r0, rc)
            dz_ref[r0:r0 + rc, 2 * c:3 * c] = (d_ref[r0:r0 + rc, c:2 * c] * cbc).astype(BF16)
            dcb = _conv_taps(extdcb, wb_ref, 3, 2, r0, rc, reverse=True)
            dz_ref[r0:r0 + rc, 3 * c:4 * c] = (dcb * zr[:, 4 * c:5 * c]).astype(BF16)
            dz_ref[r0:r0 + rc, 4 * c:5 * c] = (dcb * zr[:, 3 * c:4 * c]).astype(BF16)

        for k in range(ta):
            part = None
            for q in range(ts // rc):
                r0 = q * rc
                p = exta[hb + r0:hb + r0 + rc, :] * _shifted(extdca, shd, r0, rc, ta - 1 - k)
                for r in range(0, rc, 8):
                    part = p[r:r + 8, :] if part is None else part + p[r:r + 8, :]
            dwa_ref[k:k + 1, :] += jnp.sum(part, axis=0, keepdims=True)
        dcb_t = extdcb[0:ts, :]
        for k in range(3):
            dwb_ref[k:k + 1, :] += jnp.sum(dcb_t * extb[pl.ds(hb - 2 + k, ts), :], axis=0, keepdims=True)

    def tile(w):
        return pl.BlockSpec((ts, w), lambda i: (i, 0))

    def prev(w):
        return pl.BlockSpec((hb, w), lambda i: (_prev_blk(i, ts, hb), 0))

    def nxt(w):
        return pl.BlockSpec((hb, w), lambda i: (_next_blk(i, ts, hb, s), 0))

    def const(r, w):
        return pl.BlockSpec((r, w), lambda i: (0, 0))

    return pl.pallas_call(
        body, name=name, grid=(s // ts,),
        in_specs=[tile(5 * c), prev(5 * c), nxt(5 * c), tile(c), nxt(c), tile(2 * c), nxt(2 * c),
                  const(32, c), const(1, c), const(1, c), const(8, c)],
        out_specs=[tile(5 * c), const(32, c), const(8, c), const(1, c), const(1, c)],
        out_shape=[jax.ShapeDtypeStruct((s, 5 * c), BF16), jax.ShapeDtypeStruct((32, c), F32),
                   jax.ShapeDtypeStruct((8, c), F32), jax.ShapeDtypeStruct((1, c), F32),
                   jax.ShapeDtypeStruct((1, c), F32)],
        scratch_shapes=[pltpu.VMEM((hb + ts, c), F32), pltpu.VMEM((hb + ts, c), F32),
                        pltpu.VMEM((ts + hb, c), F32), pltpu.VMEM((ts + hb, c), F32),
                        pltpu.VMEM((8, ts + hb - 8, c), F32)],
        compiler_params=_params(("arbitrary",)),
    )(z, z, z, ca, ca, dab, dab, conv_a, ln_g, ln_b, conv_b)


def _ffn_up(x, g, w8, *, name, tm=1024):
    s, d = x.shape
    nb, c, _ = w8.shape
    tm = min(tm, s)
    grp = N_PAIR

    def body(x_ref, g_ref, w_ref, h_ref, u_ref, hs_ref):
        @pl.when(pl.program_id(1) == 0)
        def _():
            hv = _rms_rows(x_ref[...], g_ref[...]).astype(BF16)
            hs_ref[...] = hv
            h_ref[...] = hv

        res = lax.dot_general(hs_ref[...], w_ref[...].reshape(grp * c, d), _NT, preferred_element_type=F32)
        for b in range(grp):
            u_ref[b] = res[:, b * c:(b + 1) * c].astype(BF16)

    return pl.pallas_call(
        body, name=name, grid=(s // tm, nb // grp),
        in_specs=[pl.BlockSpec((tm, d), lambda i, k: (i, 0)),
                  pl.BlockSpec((1, d), lambda i, k: (0, 0)),
                  pl.BlockSpec((grp, c, d), lambda i, k: (k, 0, 0))],
        out_specs=[pl.BlockSpec((tm, d), lambda i, k: (i, 0)),
                   pl.BlockSpec((grp, tm, c), lambda i, k: (k, i, 0))],
        out_shape=[jax.ShapeDtypeStruct((s, d), BF16), jax.ShapeDtypeStruct((nb, s, c), BF16)],
        scratch_shapes=[pltpu.VMEM((tm, d), BF16)],
        compiler_params=_params(("parallel", "arbitrary")),
    )(x, g, w8)


def _ffn_mid_down(up8, wc8, wd4, x, head=None, *, name, ts=256, rc=32):
    nb, s, c = up8.shape
    d = x.shape[1]
    hb = HALO_S
    ts = min(ts, s)

    def body(*refs):
        u_ref, up_ref, wc_ref, wd_ref, x_ref = refs[:5]
        n_out = 4 if head is None else 7
        outs = refs[-1 - n_out:-1]
        act_ref, ug_ref, uv_ref = outs[-3:]
        ext = refs[-1]
        keep = jnp.where(pl.program_id(0) > 0, 1.0, 0.0)
        acc = x_ref[...]
        for j in range(N_PAIR):
            eg, ev = ext.at[2 * (j % 2)], ext.at[2 * (j % 2) + 1]
            eg[0:hb, :] = up_ref[j].astype(F32) * keep
            ev[0:hb, :] = up_ref[j + N_PAIR].astype(F32) * keep
            eg[hb:hb + ts, :] = u_ref[j].astype(F32)
            ev[hb:hb + ts, :] = u_ref[j + N_PAIR].astype(F32)
            for q in range(ts // rc):
                r0 = q * rc
                gg = _conv_taps(eg, wc_ref.at[j], 3, hb - 2, r0, rc)
                vv = _conv_taps(ev, wc_ref.at[j + N_PAIR], 3, hb - 2, r0, rc)
                ug_ref[j, r0:r0 + rc, :] = gg.astype(BF16)
                uv_ref[j, r0:r0 + rc, :] = vv.astype(BF16)
                act_ref[j, r0:r0 + rc, :] = (gg * _sigmoid(gg) * vv).astype(BF16)
            acc = acc + jnp.dot(act_ref[j], wd_ref[j], preferred_element_type=F32)
        if head is None:
            outs[0][...] = acc
            return
        g_ref, t_ref = refs[5], refs[6]
        loss_ref, dx_ref, dxb_ref, dg_ref = outs[:4]

        @pl.when(pl.program_id(0) == 0)
        def _():
            loss_ref[...] = jnp.zeros_like(loss_ref)
            dg_ref[...] = jnp.zeros_like(dg_ref)

        gv = g_ref[...]
        r = lax.rsqrt(jnp.mean(acc * acc, axis=-1, keepdims=True) + RMS_EPS)
        xh = acc * r
        err = xh * gv - t_ref[...]
        loss_ref[...] += 0.5 * jnp.sum(jnp.mean(err * err, axis=-1, keepdims=True), axis=0, keepdims=True)
        dy = err * (1.0 / d)
        dn = dy * gv
        dx = r * (dn - xh * jnp.mean(dn * xh, axis=-1, keepdims=True))
        dx_ref[...] = dx
        dxb_ref[...] = dx.astype(BF16)
        dg_ref[...] += jnp.sum(dy * xh, axis=0, keepdims=True)

    row = pl.BlockSpec((ts, d), lambda i: (i, 0))
    vec = pl.BlockSpec((1, d), lambda i: (0, 0))
    tile = pl.BlockSpec((N_PAIR, ts, c), lambda i: (0, i, 0))
    half = jax.ShapeDtypeStruct((N_PAIR, s, c), BF16)
    in_specs = [pl.BlockSpec((nb, ts, c), lambda i: (0, i, 0)),
                pl.BlockSpec((nb, hb, c), lambda i: (0, _prev_blk(i, ts, hb), 0)),
                pl.BlockSpec((nb, 8, c), lambda i: (0, 0, 0)),
                pl.BlockSpec((N_PAIR, c, d), lambda i: (0, 0, 0)), row]
    args = [up8, up8, wc8, wd4, x]
    if head is None:
        out_specs = [row, tile, tile, tile]
        out_shape = [jax.ShapeDtypeStruct((s, d), F32), half, half, half]
    else:
        in_specs += [vec, row]
        args += list(head)
        out_specs = [pl.BlockSpec((1, 1), lambda i: (0, 0)), row, row, vec, tile, tile, tile]
        out_shape = [jax.ShapeDtypeStruct((1, 1), F32), jax.ShapeDtypeStruct((s, d), F32),
                     jax.ShapeDtypeStruct((s, d), BF16), jax.ShapeDtypeStruct((1, d), F32), half, half, half]
    return pl.pallas_call(
        body, name=name, grid=(s // ts,), in_specs=in_specs, out_specs=out_specs, out_shape=out_shape,
        scratch_shapes=[pltpu.VMEM((4, hb + ts, c), F32)],
        compiler_params=_params(("parallel",) if head is None else ("arbitrary",)),
    )(*args)


def _ffn_dwdown(act4, db, *, name):
    npair, s, c = act4.shape
    d = db.shape[1]

    def body(a_ref, d_ref, o_ref):
        o_ref[...] = lax.dot_general(a_ref[...], d_ref[...], (((0,), (0,)), ((), ())),
                                     preferred_element_type=F32).astype(BF16)

    return pl.pallas_call(
        body, name=name, grid=(npair,),
        in_specs=[pl.BlockSpec((None, s, c), lambda j: (j, 0, 0)),
                  pl.BlockSpec((s, d), lambda j: (0, 0))],
        out_specs=pl.BlockSpec((None, c, d), lambda j: (j, 0, 0)),
        out_shape=jax.ShapeDtypeStruct((npair, c, d), BF16),
        compiler_params=_params(("parallel",)),
    )(act4, db)


def _ffn_dact(db, wd4, *, name, tm=1024):
    s, d = db.shape
    npair, c, _ = wd4.shape
    tm = min(tm, s)

    def body(d_ref, w_ref, o_ref):
        res = lax.dot_general(d_ref[...], w_ref[...].reshape(npair * c, d), _NT, preferred_element_type=F32)
        for j in range(npair):
            o_ref[j] = res[:, j * c:(j + 1) * c].astype(BF16)

    return pl.pallas_call(
        body, name=name, grid=(s // tm,),
        in_specs=[pl.BlockSpec((tm, d), lambda i: (i, 0)),
                  pl.BlockSpec((npair, c, d), lambda i: (0, 0, 0))],
        out_specs=pl.BlockSpec((npair, tm, c), lambda i: (0, i, 0)),
        out_shape=jax.ShapeDtypeStruct((npair, s, c), BF16),
        compiler_params=_params(("parallel",)),
    )(db, wd4)


def _ffn_midbwd(up8, ug4, uv4, dact4, wc8, *, name, ts=1024, rc=32):
    nb, s, c = up8.shape
    hb = HALO_S
    ts = min(ts, s)
    n_i = s // ts

    def body(pg_ref, pv_ref, ug_ref, ugn_ref, uv_ref, uvn_ref, d_ref, dn_ref, wg_ref, wv_ref,
             dg_ref, dv_ref, dwg_ref, dwv_ref, extdg, extdv):
        i = pl.program_id(1)
        keep_n = jnp.where(i < n_i - 1, 1.0, 0.0)

        @pl.when(i == 0)
        def _():
            dwg_ref[...] = jnp.zeros_like(dwg_ref)
            dwv_ref[...] = jnp.zeros_like(dwv_ref)

        def du_rows(r0, rows, gg, vv, dav):
            sg = _sigmoid(gg)
            extdg[r0:r0 + rows, :] = dav * vv * (sg * (1.0 + gg * (1.0 - sg)))
            extdv[r0:r0 + rows, :] = dav * (gg * sg)

        for q in range(ts // rc):
            rows = slice(q * rc, q * rc + rc)
            du_rows(q * rc, rc, ug_ref[rows, :].astype(F32), uv_ref[rows, :].astype(F32),
                    d_ref[rows, :].astype(F32))
        du_rows(ts, 8, ugn_ref[0:8, :].astype(F32), uvn_ref[0:8, :].astype(F32),
                dn_ref[0:8, :].astype(F32) * keep_n)

        def fold(p):
            acc = p[0:8, :]
            for r in range(8, rc, 8):
                acc = acc + p[r:r + 8, :]
            return acc

        for extd, p_ref, w_ref, out_ref, dw_ref in ((extdg, pg_ref, wg_ref, dg_ref, dwg_ref),
                                                    (extdv, pv_ref, wv_ref, dv_ref, dwv_ref)):
            part = [None, None, None]
            for q in range(ts // rc):
                r0 = q * rc
                pre = p_ref[r0:r0 + rc, :].astype(F32)
                dup = None
                for k in range(3):
                    sh = extd[pl.ds(r0 + 2 - k, rc), :]
                    term = w_ref[k:k + 1, :] * sh
                    dup = term if dup is None else dup + term
                    prod = fold(pre * sh)
                    part[k] = prod if part[k] is None else part[k] + prod
                out_ref[r0:r0 + rc, :] = dup.astype(BF16)
            for k in range(3):
                dw_ref[k:k + 1, :] += jnp.sum(part[k], axis=0, keepdims=True)

    def blk(off, nxt):
        if nxt:
            return pl.BlockSpec((None, hb, c), lambda j, i: (j + off, _next_blk(i, ts, hb, s), 0))
        return pl.BlockSpec((None, ts, c), lambda j, i: (j + off, i, 0))

    def taps(off):
        return pl.BlockSpec((None, 8, c), lambda j, i: (j + off, 0, 0))

    tile = blk(0, False)
    acc = pl.BlockSpec((None, 8, c), lambda j, i: (j, 0, 0))
    return pl.pallas_call(
        body, name=name, grid=(N_PAIR, n_i),
        in_specs=[tile, blk(N_PAIR, False), tile, blk(0, True), tile, blk(0, True), tile, blk(0, True),
                  taps(0), taps(N_PAIR)],
        out_specs=[tile, tile, acc, acc],
        out_shape=[jax.ShapeDtypeStruct((N_PAIR, s, c), BF16), jax.ShapeDtypeStruct((N_PAIR, s, c), BF16),
                   jax.ShapeDtypeStruct((N_PAIR, 8, c), F32), jax.ShapeDtypeStruct((N_PAIR, 8, c), F32)],
        scratch_shapes=[pltpu.VMEM((ts + 8, c), F32), pltpu.VMEM((ts + 8, c), F32)],
        compiler_params=_params(("parallel", "arbitrary")),
    )(up8, up8, ug4, ug4, uv4, uv4, dact4, dact4, wc8, wc8)


def _ffn_dh(dupg, dupv, w8, x, g, dres, dep, *, name, tm=512):
    npair, s, c = dupg.shape
    d = x.shape[1]
    tm = min(tm, s)

    def body(dg_ref, dv_ref, w_ref, x_ref, g_ref, dres_ref, dep_ref, dx_ref, dxb_ref, dgain_ref):
        @pl.when(pl.program_id(0) == 0)
        def _():
            dgain_ref[...] = jnp.zeros_like(dgain_ref)

        dh = None
        for j in range(npair):
            for src, k in ((dg_ref, j), (dv_ref, j + npair)):
                part = jnp.dot(src[j], w_ref[k], preferred_element_type=F32)
                dh = part if dh is None else dh + part
        dx, dgr = _rms_bwd_rows(x_ref[...], g_ref[...], dh)
        dx = dres_ref[...] + dx
        dx_ref[...] = dx
        dxb_ref[...] = dx.astype(BF16)
        dgain_ref[...] += jnp.sum(dgr, axis=0, keepdims=True)

    row = pl.BlockSpec((tm, d), lambda i: (i, 0))
    vec = pl.BlockSpec((1, d), lambda i: (0, 0))
    dup = pl.BlockSpec((npair, tm, c), lambda i: (0, i, 0))
    return pl.pallas_call(
        body, name=name, grid=(s // tm,),
        in_specs=[dup, dup, pl.BlockSpec((2 * npair, c, d), lambda i: (0, 0, 0)), row, vec, row,
                  pl.BlockSpec(memory_space=pl.ANY)],
        out_specs=[row, row, vec],
        out_shape=[jax.ShapeDtypeStruct((s, d), F32), jax.ShapeDtypeStruct((s, d), BF16),
                   jax.ShapeDtypeStruct((1, d), F32)],
        compiler_params=_params(("arbitrary",)),
    )(dupg, dupv, w8, x, g, dres, dep)


def _ffn_dwup(h, dupg, dupv, *, name, tm=1024):
    npair, s, c = dupg.shape
    d = h.shape[1]

    def body(h_ref, dg_ref, dv_ref, o_ref):
        k = pl.program_id(1)

        @pl.when(k < npair)
        def _():
            o_ref[...] = lax.dot_general(dg_ref[...], h_ref[...], _TN, preferred_element_type=F32).astype(BF16)

        @pl.when(k >= npair)
        def _():
            o_ref[...] = lax.dot_general(dv_ref[...], h_ref[...], _TN, preferred_element_type=F32).astype(BF16)

    return pl.pallas_call(
        body, name=name, grid=(d // tm, 2 * npair),
        in_specs=[pl.BlockSpec((s, tm), lambda m, k: (0, m)),
                  pl.BlockSpec((None, s, c), lambda m, k: (jnp.minimum(k, npair - 1), 0, 0)),
                  pl.BlockSpec((None, s, c), lambda m, k: (jnp.maximum(k - npair, 0), 0, 0))],
        out_specs=pl.BlockSpec((None, c, tm), lambda m, k: (k, 0, m)),
        out_shape=jax.ShapeDtypeStruct((2 * npair, c, d), BF16),
        compiler_params=_params(("parallel", "arbitrary")),
    )(h, dupg, dupv)


def _pool_counts(i, ts, rows, window):
    t = lax.broadcasted_iota(jnp.int32, (rows, 1), 0) + i * ts + 1
    return jnp.minimum(t, window).astype(F32)


def _pool_fwd(x, g, w_pool, scale, *, name, ts=512):
    s, d = x.shape
    hb = HALO_A
    pg = POOL_GROUP
    ts = min(ts, s)

    def body(x_ref, xp_ref, g_ref, w_ref, s_ref, o_ref, p_ref, yu_ref, ext, sa, sb):
        i = pl.program_id(0)
        keep = jnp.where(i > 0, 1.0, 0.0)
        gv = g_ref[...]
        ext[0:hb, :] = _rms_rows(xp_ref[...], gv) * keep
        ext[hb:hb + ts, :] = _rms_rows(x_ref[...], gv)
        rows = hb + ts - 8
        sa[0:8, :] = jnp.zeros((8, d), F32)
        sb[0:8, :] = jnp.zeros((8, d), F32)
        for gi, w in enumerate(POOL_WINDOWS):
            cols = slice(gi * pg, (gi + 1) * pg)
            cur, nxt, k = ext, sa, 1
            while k < w:
                nxt[8:8 + rows, cols] = cur[8:8 + rows, cols] + cur[pl.ds(8 - k, rows), cols]
                cur, nxt, k = nxt, (sb if nxt is sa else sa), 2 * k
            h = ext[hb:hb + ts, cols]
            pv = (cur[hb:hb + ts, cols] / _pool_counts(i, ts, ts, w) - h).astype(BF16)
            p_ref[:, cols] = pv
            yu = jnp.dot(pv, w_ref[gi], preferred_element_type=F32)
            yu_ref[:, cols] = yu.astype(BF16)
            o_ref[:, cols] = x_ref[:, cols] + yu * s_ref[:, cols]

    row = pl.BlockSpec((ts, d), lambda i: (i, 0))
    vec = pl.BlockSpec((1, d), lambda i: (0, 0))
    return pl.pallas_call(
        body, name=name, grid=(s // ts,),
        in_specs=[row, pl.BlockSpec((hb, d), lambda i: (_prev_blk(i, ts, hb), 0)), vec,
                  pl.BlockSpec((d // pg, pg, pg), lambda i: (0, 0, 0)), vec],
        out_specs=[row, row, row],
        out_shape=[jax.ShapeDtypeStruct((s, d), F32), jax.ShapeDtypeStruct((s, d), BF16),
                   jax.ShapeDtypeStruct((s, d), BF16)],
        scratch_shapes=[pltpu.VMEM((hb + ts, d), F32)] * 3,
        compiler_params=_params(("parallel",)),
    )(x, x, g, w_pool, scale)


def _pool_dw(p, dyc, *, name):
    s, d = p.shape
    pg = POOL_GROUP

    def body(p_ref, d_ref, o_ref):
        o_ref[...] = lax.dot_general(p_ref[...], d_ref[...], (((0,), (0,)), ((), ())),
                                     preferred_element_type=F32).astype(BF16)

    blk = pl.BlockSpec((s, pg), lambda gi: (0, gi))
    return pl.pallas_call(
        body, name=name, grid=(d // pg,),
        in_specs=[blk, blk], out_specs=pl.BlockSpec((None, pg, pg), lambda gi: (gi, 0, 0)),
        out_shape=jax.ShapeDtypeStruct((d // pg, pg, pg), BF16),
        compiler_params=_params(("parallel",)),
    )(p, dyc)


def _pool_bwd(dres, w_pool, scale, yu, x, g, *, name, ts=512):
    s, d = x.shape
    hb = HALO_A
    pg = POOL_GROUP
    ts = min(ts, s)
    n_i = s // ts

    def body(dres_ref, dresn_ref, w_ref, s_ref, yu_ref, x_ref, g_ref,
             dx_ref, dxb_ref, dg_ref, dyc_ref, ds_ref, ext, dh, sa, sb, dp_s):
        i = pl.program_id(0)
        keep_n = jnp.where(i < n_i - 1, 1.0, 0.0)

        @pl.when(i == 0)
        def _():
            dg_ref[...] = jnp.zeros_like(dg_ref)
            ds_ref[...] = jnp.zeros_like(ds_ref)

        ds_ref[...] += jnp.sum(dres_ref[...] * yu_ref[...].astype(F32), axis=0, keepdims=True)
        for gi, w in enumerate(POOL_WINDOWS):
            cols = slice(gi * pg, (gi + 1) * pg)
            dyc = (dres_ref[:, cols] * s_ref[:, cols]).astype(BF16)
            dyc_ref[:, cols] = dyc
            dp = lax.dot_general(dyc, w_ref[gi], _NT, preferred_element_type=F32)
            dp_s[:, cols] = dp
            ext[0:ts, cols] = dp / _pool_counts(i, ts, ts, w)
            dycn = (dresn_ref[:, cols] * s_ref[:, cols]).astype(BF16)
            dpn = lax.dot_general(dycn, w_ref[gi], _NT, preferred_element_type=F32)
            ext[ts:ts + hb, cols] = dpn / _pool_counts(i + 1, ts, hb, w) * keep_n
        rows = ts + hb - 8
        sa[rows:rows + 8, :] = jnp.zeros((8, d), F32)
        sb[rows:rows + 8, :] = jnp.zeros((8, d), F32)
        for gi, w in enumerate(POOL_WINDOWS):
            cols = slice(gi * pg, (gi + 1) * pg)
            cur, nxt, k = ext, sa, 1
            while k < w:
                nxt[0:rows, cols] = cur[0:rows, cols] + cur[pl.ds(k, rows), cols]
                cur, nxt, k = nxt, (sb if nxt is sa else sa), 2 * k
            dh[:, cols] = cur[0:ts, cols] - dp_s[:, cols]
        dx, dgr = _rms_bwd_rows(x_ref[...], g_ref[...], dh[...])
        dx = dres_ref[...] + dx
        dx_ref[...] = dx
        dxb_ref[...] = dx.astype(BF16)
        dg_ref[...] += jnp.sum(dgr, axis=0, keepdims=True)

    row = pl.BlockSpec((ts, d), lambda i: (i, 0))
    vec = pl.BlockSpec((1, d), lambda i: (0, 0))
    return pl.pallas_call(
        body, name=name, grid=(n_i,),
        in_specs=[row, pl.BlockSpec((hb, d), lambda i: (_next_blk(i, ts, hb, s), 0)),
                  pl.BlockSpec((d // pg, pg, pg), lambda i: (0, 0, 0)), vec, row, row, vec],
        out_specs=[row, row, vec, row, vec],
        out_shape=[jax.ShapeDtypeStruct((s, d), F32), jax.ShapeDtypeStruct((s, d), BF16),
                   jax.ShapeDtypeStruct((1, d), F32), jax.ShapeDtypeStruct((s, d), BF16),
                   jax.ShapeDtypeStruct((1, d), F32)],
        scratch_shapes=[pltpu.VMEM((ts + hb, d), F32), pltpu.VMEM((ts, d), F32),
                        pltpu.VMEM((ts + hb, d), F32), pltpu.VMEM((ts + hb, d), F32),
                        pltpu.VMEM((ts, d), F32)],
        compiler_params=_params(("arbitrary",)),
    )(dres, dres, w_pool, scale, yu, x, g)


def _pad_rows(w, rows):
    pad = [(0, 0)] * (w.ndim - 2) + [(0, rows - w.shape[-2]), (0, 0)]
    return jnp.pad(w, pad)


def _ffn_layer_fwd(x, nf, w8, wc8, wd4, tag, head=None):
    h, up8 = _ffn_up(x, nf, w8, name=f"ffn{tag}_up")
    res = _ffn_mid_down(up8, wc8, wd4, x, head, name=f"ffn{tag}_mid_down")
    act4, ug4, uv4 = res[-3:]
    out = res[0] if head is None else tuple(res[:4])
    return out, (h, up8, ug4, uv4, act4)


def _ffn_layer_bwd(d, db, x, nf, w8, wc8, wd4, saved, tag, grads_ready):
    h, up8, ug4, uv4, act4 = saved
    dact4 = _ffn_dact(db, wd4, name=f"ffn{tag}_dact")
    dwd4 = _ffn_dwdown(act4, db, name=f"ffn{tag}_dwdown")
    dupg, dupv, dwg, dwv = _ffn_midbwd(up8, ug4, uv4, dact4, wc8, name=f"ffn{tag}_midbwd")
    dw8 = _ffn_dwup(h, dupg, dupv, name=f"ffn{tag}_dwup")
    sent = grads_ready(f"ffn{tag}", {f"w_up{tag}": dw8, f"w_down{tag}": dwd4})
    dx, dxb, dnf = _ffn_dh(dupg, dupv, w8, x, nf, d, sent, name=f"ffn{tag}_dh")
    dwc8 = jnp.concatenate([dwg, dwv], axis=0)[:, :3]
    return dx, dxb, dnf, dwc8


def _local_step(x, target, weights, grads_ready):
    w0 = weights("mix0", x)
    wa = _pad_rows(w0["conv_a"], 32)
    wb = _pad_rows(w0["conv_b"], 8)
    wc = [_pad_rows(w0["conv_ffn"][l], 8) for l in range(2)]
    h0, z = _rms_mm(x, w0["norm_mix_even"], w0["w_in_t"], name="mix0_in", out_dtype=F32, tn=2560)
    ab, ca, x1 = _mix0_fwd(z, wa, w0["ln_a_g"], w0["ln_a_b"], wb, w0["w_out"], x, name="mix0_mid")
    w1 = weights("ffn0", x1)
    x2, ffn0 = _ffn_layer_fwd(x1, w0["norm_ffn"][0:1], w1["w_up"], wc[0], w1["w_down"], 0)
    w2 = weights("ffn1", x2)
    x3, p, yu = _pool_fwd(x2, w0["norm_mix_odd"], w2["w_pool"], w0["pool_scale"], name="pool_fwd")
    (loss, d4, d4b, g_norm_final), ffn1 = _ffn_layer_fwd(
        x3, w0["norm_ffn"][1:2], w2["w_up"], wc[1], w2["w_down"], 1, head=(w0["norm_final"], target))

    d3, d3b, g_nf1, g_wc1 = _ffn_layer_bwd(
        d4, d4b, x3, w0["norm_ffn"][1:2], w2["w_up"], wc[1], w2["w_down"], ffn1, 1, grads_ready)
    d2, d2b, g_nmo, dyc, g_scale = _pool_bwd(d3, w2["w_pool"], w0["pool_scale"], yu, x2, w0["norm_mix_odd"],
                                             name="pool_bwd")
    g_pool = _pool_dw(p, dyc, name="pool_dw")
    d1, d1b, g_nf0, g_wc0 = _ffn_layer_bwd(
        d2, d2b, x1, w0["norm_ffn"][0:1], w1["w_up"], wc[0], w1["w_down"], ffn0, 0, grads_ready)
    dab = _mm(d1b, w0["w_out"], tb=True, name="mix0_dab", tm=1024, tn=1024)
    g_out = _mm(ab, d1b, ta=True, out_dtype=BF16, name="mix0_dwout", tm=1024, tn=512)
    dz, g_wa, g_wb, g_lg, g_lb = _mix0_bwd(z, ca, dab, wa, w0["ln_a_g"], w0["ln_a_b"], wb, name="mix0_midbwd")
    g_in = _mm(dz, h0, ta=True, out_dtype=BF16, name="mix0_dwin", tm=512, tn=1024)
    sent_mix = grads_ready("mix0", {"w_in": g_in, "w_out": g_out, "w_pool": g_pool})
    dx, g_nme = _mm_rms_bwd(dz, w0["w_in_t"], x, w0["norm_mix_even"], d1, sent_mix, name="mix0_dh")

    small = {
        "norm_mix_even": g_nme, "conv_a": g_wa[:A_TAPS], "ln_a_g": g_lg, "ln_a_b": g_lb, "conv_b": g_wb[:3],
        "norm_mix_odd": g_nmo, "pool_scale": g_scale, "norm_ffn": jnp.concatenate([g_nf0, g_nf1], axis=0),
        "conv_ffn": [g_wc0, g_wc1], "norm_final": g_norm_final,
    }
    return loss[0, 0], dx, small


def _my_pos():
    return lax.axis_index("x"), lax.axis_index("y"), lax.axis_index("c")


def _flip(pos, r):
    x, y, c = pos
    return (1 - x if r & 4 else x, 1 - y if r & 2 else y, 1 - c if r & 1 else c)


def _dev_index(pos):
    return 4 * pos[0] + 2 * pos[1] + pos[2]


_HBM = pl.BlockSpec(memory_space=pltpu.HBM)
_SEM = pl.BlockSpec(memory_space=pltpu.SEMAPHORE)
_EFFECT = pltpu.SideEffectType.DATAFLOW_SIDE_EFFECTING


def _exchange_copy(ins, lands, send_sems, recv_sems, scatter, pos, a, r, receiving):
    me = _dev_index(pos)
    peer = _flip(pos, r)
    dest = _dev_index(pos) if receiving else _dev_index(peer)
    src = ins[a].at[dest] if scatter[a] else ins[a]
    slot = _dev_index(peer) if receiving else me
    return pltpu.make_async_remote_copy(
        src_ref=src, dst_ref=lands[a].at[slot], send_sem=send_sems.at[a * (N_DEV - 1) + r - 1],
        recv_sem=recv_sems.at[a * (N_DEV - 1) + r - 1],
        device_id=peer, device_id_type=pl.DeviceIdType.MESH)


ALL_PEERS = tuple(range(1, N_DEV))
CHIP_PEERS = (1, 2, 4, 6)
FORWARDED = (2, 4, 6)


def _exchange_start(arrays, scatter, after, *, name, peers=ALL_PEERS):
    n = len(arrays)
    me = _dev_index(_my_pos())
    lands = []
    for arr, sc in zip(arrays, scatter):
        own = lax.dynamic_index_in_dim(arr, me, 0, keepdims=True) if sc else arr[None]
        shape = arr.shape if sc else (N_DEV,) + arr.shape
        lands.append(lax.dynamic_update_slice(lax.empty(shape, arr.dtype), own, (me,) + (0,) * (len(shape) - 1)))

    def body(*refs):
        ins, lnd = refs[:n], refs[n:2 * n]
        send_sems, recv_sems = refs[2 * n + 1], refs[2 * n + 2]
        token = refs[-1]
        pos = _my_pos()
        for a in range(n):
            for r in peers:
                _exchange_copy(ins, lnd, send_sems, recv_sems, scatter, pos, a, r, receiving=False).start()
        token[...] = jnp.zeros_like(token)

    bufs = [pltpu.with_memory_space_constraint(t, pltpu.HBM) for t in list(arrays) + lands]
    sems = pltpu.SemaphoreType.DMA((n * (N_DEV - 1),))
    res = pl.pallas_call(
        body, name=name,
        out_shape=(sems, sems, *[pltpu.HBM(t.shape, t.dtype) for t in bufs], jax.ShapeDtypeStruct((8, 128), F32)),
        in_specs=[_HBM] * (2 * n) + [pl.BlockSpec(memory_space=pl.ANY)],
        out_specs=(_SEM, _SEM, *[_HBM] * (2 * n), pl.BlockSpec(memory_space=pltpu.VMEM)),
        input_output_aliases={i: 2 + i for i in range(2 * n)},
        compiler_params=pltpu.CompilerParams(has_side_effects=_EFFECT),
    )(*bufs, after)
    return res[0], res[1], list(res[2:2 + n]), list(res[2 + n:2 + 2 * n]), res[-1]


def _exchange_wait(started, scatter, after, *, name, peers=ALL_PEERS):
    send_sems, recv_sems, arrays, lands, _ = started
    n = len(arrays)

    def body(*refs):
        ins, lnd = refs[:n], refs[n:2 * n]
        send_sems, recv_sems = refs[2 * n], refs[2 * n + 1]
        pos = _my_pos()
        for a in range(n):
            for r in peers:
                _exchange_copy(ins, lnd, send_sems, recv_sems, scatter, pos, a, r, receiving=False).wait_send()
                _exchange_copy(ins, lnd, send_sems, recv_sems, scatter, pos, a, r, receiving=True).wait_recv()

    bufs = list(arrays) + list(lands)
    after = list(after) if isinstance(after, (list, tuple)) else [after]
    res = pl.pallas_call(
        body, name=name,
        out_shape=tuple(pltpu.HBM(t.shape, t.dtype) for t in bufs),
        in_specs=[_HBM] * (2 * n) + [_SEM, _SEM] + [pl.BlockSpec(memory_space=pl.ANY)] * len(after),
        out_specs=tuple([_HBM] * (2 * n)),
        input_output_aliases={i: i for i in range(2 * n)},
        compiler_params=pltpu.CompilerParams(has_side_effects=_EFFECT),
    )(*bufs, send_sems, recv_sems, *after)
    return list(res[n:])


def _forward_copy(lands, send_sems, recv_sems, pos, a, q, receiving):
    slot = _dev_index(_flip(pos, q ^ 1 if receiving else q))
    idx = a * len(FORWARDED) + FORWARDED.index(q)
    return pltpu.make_async_remote_copy(
        src_ref=lands[a].at[slot], dst_ref=lands[a].at[slot], send_sem=send_sems.at[idx], recv_sem=recv_sems.at[idx],
        device_id=_flip(pos, 1), device_id_type=pl.DeviceIdType.MESH)


def _forward_start(lands, after, *, name):
    n = len(lands)

    def body(*refs):
        lnd = refs[:n]
        send_sems, recv_sems = refs[n + 1], refs[n + 2]
        token = refs[-1]
        pos = _my_pos()
        for a in range(n):
            for q in FORWARDED:
                _forward_copy(lnd, send_sems, recv_sems, pos, a, q, receiving=False).start()
        token[...] = jnp.zeros_like(token)

    sems = pltpu.SemaphoreType.DMA((n * len(FORWARDED),))
    res = pl.pallas_call(
        body, name=name,
        out_shape=(sems, sems, *[pltpu.HBM(t.shape, t.dtype) for t in lands], jax.ShapeDtypeStruct((8, 128), F32)),
        in_specs=[_HBM] * n + [pl.BlockSpec(memory_space=pl.ANY)],
        out_specs=(_SEM, _SEM, *[_HBM] * n, pl.BlockSpec(memory_space=pltpu.VMEM)),
        input_output_aliases={i: 2 + i for i in range(n)},
        compiler_params=pltpu.CompilerParams(has_side_effects=_EFFECT),
    )(*lands, after)
    return res[0], res[1], list(res[2:2 + n]), res[-1]


def _forward_wait(forwarded, after, *, name):
    send_sems, recv_sems, lands, _ = forwarded
    n = len(lands)

    def body(*refs):
        lnd = refs[:n]
        send_sems, recv_sems = refs[n], refs[n + 1]
        pos = _my_pos()
        for a in range(n):
            for q in FORWARDED:
                _forward_copy(lnd, send_sems, recv_sems, pos, a, q, receiving=False).wait_send()
                _forward_copy(lnd, send_sems, recv_sems, pos, a, q, receiving=True).wait_recv()

    res = pl.pallas_call(
        body, name=name,
        out_shape=tuple(pltpu.HBM(t.shape, t.dtype) for t in lands),
        in_specs=[_HBM] * n + [_SEM, _SEM, pl.BlockSpec(memory_space=pl.ANY)],
        out_specs=tuple([_HBM] * n),
        input_output_aliases={i: i for i in range(n)},
        compiler_params=pltpu.CompilerParams(has_side_effects=_EFFECT),
    )(*lands, send_sems, recv_sems, after)
    return list(res)


def _adamw_update(p_ref, w_ref, m_ref, v_ref, g_ref, d_ref, mo_ref, vo_ref):
    g = p_ref[0].astype(F32)
    for k in range(1, N_DEV):
        g = g + p_ref[k].astype(F32)
    mn = ADAM_B1 * m_ref[...] + (1.0 - ADAM_B1) * g
    vn = ADAM_B2 * v_ref[...] + (1.0 - ADAM_B2) * (g * g)
    m_hat = mn / (1.0 - ADAM_B1 ** ADAM_STEP)
    v_hat = vn / (1.0 - ADAM_B2 ** ADAM_STEP)
    g_ref[...] = g
    d_ref[...] = -ADAM_LR * (m_hat / (jnp.sqrt(v_hat) + ADAM_EPS) + ADAM_WD * w_ref[...])
    mo_ref[...] = mn
    vo_ref[...] = vn


def _adamw_small(parts, ws, ms, vs, *, name):
    n = len(ws)

    def body(*refs):
        ins, outs = refs[:4 * n], refs[4 * n:]
        for j in range(n):
            _adamw_update(ins[j], ins[n + j], ins[2 * n + j], ins[3 * n + j], *outs[4 * j:4 * j + 4])

    return pl.pallas_call(
        body, name=name,
        out_shape=[jax.ShapeDtypeStruct(w.shape, F32) for w in ws for _ in range(4)],
    )(*parts, *ws, *ms, *vs)


def _adamw(parts, w, m, v, *, name, tr):
    nl, r, c = w.shape
    assert len(parts) == nl and r % tr == 0
    n_i = r // tr

    def body(*refs):
        p_refs = refs[:nl]
        w_ref, m_ref, v_ref = refs[nl:nl + 3]

        def update(p_ref):
            _adamw_update(p_ref, w_ref, m_ref, v_ref, *refs[nl + 3:])

        if nl == 1:
            update(p_refs[0])
        else:
            for layer in range(nl):
                pl.when(pl.program_id(0) == layer)(lambda p_ref=p_refs[layer]: update(p_ref))

    def parts_spec(layer):
        def index(l, i):
            return (0, jnp.where(l < layer, 0, jnp.where(l > layer, n_i - 1, i)), 0)
        return pl.BlockSpec((N_DEV, tr, c), index)

    blk = pl.BlockSpec((None, tr, c), lambda l, i: (l, i, 0))
    return pl.pallas_call(
        body, name=name, grid=(nl, n_i),
        in_specs=[parts_spec(layer) for layer in range(nl)] + [blk, blk, blk],
        out_specs=[blk] * 4, out_shape=[jax.ShapeDtypeStruct((nl, r, c), F32)] * 4,
        compiler_params=_params(("arbitrary", "arbitrary")),
    )(*parts, w, m, v)


def _pack(parts, lead=()):
    flat = jnp.concatenate([p.reshape(lead + (-1,)) for p in parts], axis=-1)
    n = flat.shape[-1]
    rows = -(-n // (8 * 128)) * 8
    flat = jnp.pad(flat, [(0, 0)] * len(lead) + [(0, rows * 128 - n)])
    return flat.reshape(lead + (rows, 128))


def _to_dev_major(g, axis):
    shp = g.shape
    g = g.reshape(shp[:axis] + (N_DEV, shp[axis] // N_DEV) + shp[axis + 1:])
    return jnp.moveaxis(g, axis, 0)


def _from_dev_major(g, axis):
    g = jnp.moveaxis(g, 0, axis)
    shp = g.shape
    return g.reshape(shp[:axis] + (shp[axis] * shp[axis + 1],) + shp[axis + 2:])


SMALL_SHARDED = ("conv_a", "conv_b", "norm_mix_odd", "pool_scale", "conv_ffn_w")
SMALL_REPLICATED = ("norm_mix_even", "ln_a_g", "ln_a_b", "norm_ffn", "norm_final")
BIG = {"w_in": ("w_in", 0, 320), "w_out": ("w_out", 0, 128), "w_pool": ("w_pool", 0, 128),
       "w_up0": ("w_up", 0, 352), "w_up1": ("w_up", 1, 352),
       "w_down0": ("w_down", 0, 352), "w_down1": ("w_down", 1, 352)}
COLUMN_SHARDED = ("w_in", "w_up")


def kernel(x, norm_mix_even, w_in, conv_a, ln_a_g, ln_a_b, conv_b, w_out, norm_mix_odd, w_pool, pool_scale, norm_ffn, w_up, conv_ffn_w, w_down, norm_final, loss_target, m_norm_mix_even, m_w_in, m_conv_a, m_ln_a_g, m_ln_a_b, m_conv_b, m_w_out, m_norm_mix_odd, m_w_pool, m_pool_scale, m_norm_ffn, m_w_up, m_conv_ffn_w, m_w_down, m_norm_final, v_norm_mix_even, v_w_in, v_conv_a, v_ln_a_g, v_ln_a_b, v_conv_b, v_w_out, v_norm_mix_odd, v_w_pool, v_pool_scale, v_norm_ffn, v_w_up, v_conv_ffn_w, v_w_down, v_norm_final):
    names = ("norm_mix_even", "w_in", "conv_a", "ln_a_g", "ln_a_b", "conv_b", "w_out", "norm_mix_odd", "w_pool",
             "pool_scale", "norm_ffn", "w_up", "conv_ffn_w", "w_down", "norm_final")
    wts = dict(zip(names, (norm_mix_even, w_in, conv_a, ln_a_g, ln_a_b, conv_b, w_out, norm_mix_odd, w_pool,
                           pool_scale, norm_ffn, w_up, conv_ffn_w, w_down, norm_final)))
    mom = dict(zip(names, (m_norm_mix_even, m_w_in, m_conv_a, m_ln_a_g, m_ln_a_b, m_conv_b, m_w_out, m_norm_mix_odd,
                           m_w_pool, m_pool_scale, m_norm_ffn, m_w_up, m_conv_ffn_w, m_w_down, m_norm_final)))
    var = dict(zip(names, (v_norm_mix_even, v_w_in, v_conv_a, v_ln_a_g, v_ln_a_b, v_conv_b, v_w_out, v_norm_mix_odd,
                           v_w_pool, v_pool_scale, v_norm_ffn, v_w_up, v_conv_ffn_w, v_w_down, v_norm_final)))
    d = x.shape[-1]

    def shard3d(t, param):
        a = t[param]
        if param in COLUMN_SHARDED:
            return jnp.swapaxes(a, 1, 2)
        return a.reshape(a.shape[0], -1, a.shape[-1])

    def unshard3d(a, param):
        if param in COLUMN_SHARDED:
            return jnp.swapaxes(a, 1, 2)
        return a.reshape(wts[param].shape)

    def shard2d(t, key):
        param, layer, _ = BIG[key]
        return shard3d(t, param)[layer]

    small_w = _pack([wts[k] for k in SMALL_SHARDED])
    bf = {k: shard2d(wts, k).astype(BF16) for k in BIG}
    gather_groups = {"mix0": ("w_in", "w_out", "small"), "ffn0": ("w_up0", "w_down0"),
                     "ffn1": ("w_pool", "w_up1", "w_down1")}
    order = list(gather_groups)
    started = {}

    def start_gather(grp, after):
        arrs = [small_w if k == "small" else bf[k] for k in gather_groups[grp]]
        started[grp] = _exchange_start(arrs, [False] * len(arrs), after, name=f"gather_{grp}_start", peers=CHIP_PEERS)

    start_gather(order[0], small_w)

    def weights(grp, after):
        keys = gather_groups[grp]
        lands = _exchange_wait(started[grp], [False] * len(keys), after, name=f"gather_{grp}_wait", peers=CHIP_PEERS)
        forwarded = _forward_start(lands, small_w, name=f"gather_{grp}_forward")
        if grp != order[-1]:
            start_gather(order[order.index(grp) + 1], forwarded[-1])
            behind = started[order[order.index(grp) + 1]][-1]
        else:
            behind = forwarded[-1]
        gw = dict(zip(keys, _forward_wait(forwarded, behind, name=f"gather_{grp}_forward_wait")))
        if grp == "ffn0":
            return {"w_up": gw["w_up0"], "w_down": gw["w_down0"].reshape(N_PAIR, -1, d)}
        if grp == "ffn1":
            return {"w_up": gw["w_up1"], "w_down": gw["w_down1"].reshape(N_PAIR, -1, d),
                    "w_pool": _from_dev_major(gw["w_pool"].reshape(N_DEV, len(POOL_WINDOWS), -1, POOL_GROUP), 1)}
        per_dev = gw["small"].reshape(N_DEV, -1)
        sizes = [wts[k].size for k in SMALL_SHARDED]
        offs = [sum(sizes[:i]) for i in range(len(sizes))]
        small_full = {k: per_dev[:, o:o + n_].reshape((N_DEV,) + wts[k].shape)
                      for k, o, n_ in zip(SMALL_SHARDED, offs, sizes)}
        return {
            "norm_mix_even": norm_mix_even, "ln_a_g": ln_a_g, "ln_a_b": ln_a_b, "norm_ffn": norm_ffn,
            "norm_final": norm_final[None],
            "w_in_t": gw["w_in"].reshape(-1, d),
            "w_out": gw["w_out"].reshape(-1, d),
            "conv_a": _from_dev_major(small_full["conv_a"][:, 0], 1),
            "conv_b": _from_dev_major(small_full["conv_b"][:, 0], 1),
            "norm_mix_odd": _from_dev_major(small_full["norm_mix_odd"], 1),
            "pool_scale": _from_dev_major(small_full["pool_scale"], 1),
            "conv_ffn": [small_full["conv_ffn_w"][:, l] for l in range(2)],
        }

    def dev_major(k, g):
        if k == "w_pool":
            return _to_dev_major(g, 1).reshape(N_DEV, -1, POOL_GROUP)
        return g.reshape(N_DEV, -1, g.shape[-1])

    sent = {}

    def grads_ready(grp, grads):
        keys = tuple(grads)
        parts = [dev_major(k, grads[k]) for k in keys]
        sent[grp] = (keys, _exchange_start(parts, [True] * len(keys), small_w, name=f"grads_{grp}_start"))
        return sent[grp][1][-1]

    loss, dx, g = _local_step(x[0], loss_target[0], weights, grads_ready)

    def small2d(a):
        return a.reshape(-1, a.shape[-1])

    small_keys = SMALL_SHARDED + SMALL_REPLICATED
    small_parts = [_to_dev_major(g["conv_a"], 1), _to_dev_major(g["conv_b"], 1),
                   _to_dev_major(g["norm_mix_odd"], 1), _to_dev_major(g["pool_scale"], 1),
                   jnp.stack(g["conv_ffn"], axis=1).reshape(N_DEV, -1, w_up.shape[-1])]
    small_parts += [g[k] for k in SMALL_REPLICATED]
    small_scatter = [True] * len(SMALL_SHARDED) + [False] * len(SMALL_REPLICATED)
    small_parts.append(jnp.full((1, 128), loss, F32))
    small_scatter.append(False)
    sent["small"] = (small_keys + ("loss",),
                     _exchange_start(small_parts, small_scatter, dx, name="grads_small_start"))

    landed, out, raw = {}, {}, {}

    def wait_grads(grp, after, scat=None):
        keys, st = sent[grp]
        scat = [True] * len(keys) if scat is None else scat
        landed.update(zip(keys, _exchange_wait(st, scat, after, name=f"grads_{grp}_wait")))

    def update(param, keys):
        raw[param] = _adamw([landed[k] for k in keys], shard3d(wts, param), shard3d(mom, param),
                            shard3d(var, param), name=f"adamw_{param}", tr=BIG[keys[0]][2])
        out[param] = [unshard3d(t, param) for t in raw[param]]

    wait_grads("ffn1", sent["small"][1][-1])
    wait_grads("ffn0", landed["w_up1"])
    update("w_up", ("w_up0", "w_up1"))
    update("w_down", ("w_down0", "w_down1"))
    wait_grads("mix0", [raw["w_up"][1], raw["w_down"][1]])
    update("w_in", ("w_in",))
    update("w_out", ("w_out",))
    update("w_pool", ("w_pool",))
    wait_grads("small", raw["w_pool"][1], small_scatter)
    res = _adamw_small([landed[k] for k in small_keys], [small2d(wts[k]) for k in small_keys],
                       [small2d(mom[k]) for k in small_keys], [small2d(var[k]) for k in small_keys],
                       name="adamw_small")
    for j, k in enumerate(small_keys):
        out[k] = [t.reshape(wts[k].shape) for t in res[4 * j:4 * j + 4]]

    loss = jnp.sum(landed["loss"][:, 0, 0])
    return (loss, dx[None], *[out[k][0] for k in names], *[out[k][1] for k in names],
            *[out[k][2] for k in names], *[out[k][3] for k in names])
```

```python
import jax
import jax.numpy as jnp
from jax import lax
from jax.experimental import pallas as pl
from jax.experimental.pallas import tpu as pltpu

F32 = jnp.float32
BF16 = jnp.bfloat16

RMS_EPS = 1e-6
LN_EPS = 1e-5
ADAM_LR = 0.001
ADAM_B1 = 0.9
ADAM_B2 = 0.999
ADAM_EPS = 1e-08
ADAM_WD = 0.01
ADAM_STEP = 10

N_DEV = 8
N_PAIR = N_DEV // 2
A_WIDTH = 512
A_TAPS = 31
POOL_WINDOWS = (2, 4, 8, 16)
POOL_GROUP = 256
HALO_A = 32
HALO_S = 16
VMEM_LIMIT = 56 * 1024 * 1024


def _params(sem, vmem=VMEM_LIMIT):
    return pltpu.CompilerParams(dimension_semantics=sem, vmem_limit_bytes=vmem)


def _sigmoid(x):
    return 0.5 * jnp.tanh(0.5 * x) + 0.5


def _prev_blk(i, ts, hb):
    return jnp.maximum(i * (ts // hb) - 1, 0)


def _next_blk(i, ts, hb, s):
    return jnp.minimum((i + 1) * (ts // hb), s // hb - 1)


def _mm(a, b, *, name, ta=False, tb=False, out_dtype=F32, tm=512, tn=512):
    m, k = (a.shape[1], a.shape[0]) if ta else a.shape
    n = b.shape[0] if tb else b.shape[1]
    tm, tn = min(tm, m), min(tn, n)
    assert m % tm == 0 and n % tn == 0, (name, m, n, tm, tn)
    dims = (((0,) if ta else (1,), (1,) if tb else (0,)), ((), ()))

    def body(a_ref, b_ref, o_ref):
        o_ref[...] = lax.dot_general(a_ref[...], b_ref[...], dims, preferred_element_type=F32).astype(out_dtype)

    a_spec = pl.BlockSpec((k, tm), lambda i, j: (0, i)) if ta else pl.BlockSpec((tm, k), lambda i, j: (i, 0))
    b_spec = pl.BlockSpec((tn, k), lambda i, j: (j, 0)) if tb else pl.BlockSpec((k, tn), lambda i, j: (0, j))
    return pl.pallas_call(
        body, name=name, grid=(m // tm, n // tn),
        in_specs=[a_spec, b_spec], out_specs=pl.BlockSpec((tm, tn), lambda i, j: (i, j)),
        out_shape=jax.ShapeDtypeStruct((m, n), out_dtype),
        compiler_params=_params(("parallel", "parallel")),
    )(a, b)


def _rms_rows(xv, gv):
    return xv * lax.rsqrt(jnp.mean(xv * xv, axis=-1, keepdims=True) + RMS_EPS) * gv


_NT = (((1,), (1,)), ((), ()))
_TN = (((0,), (0,)), ((), ()))


def _rms_mm(x, g, wt, *, name, out_dtype, tm=1024, tn=512):
    s, d = x.shape
    n = wt.shape[0]
    tm = min(tm, s)
    assert s % tm == 0 and n % tn == 0

    def body(x_ref, g_ref, w_ref, h_ref, z_ref, hs_ref):
        @pl.when(pl.program_id(1) == 0)
        def _():
            hv = _rms_rows(x_ref[...], g_ref[...]).astype(BF16)
            hs_ref[...] = hv
            h_ref[...] = hv

        z_ref[...] = lax.dot_general(hs_ref[...], w_ref[...], _NT, preferred_element_type=F32).astype(out_dtype)

    return pl.pallas_call(
        body, name=name, grid=(s // tm, n // tn),
        in_specs=[pl.BlockSpec((tm, d), lambda i, j: (i, 0)),
                  pl.BlockSpec((1, d), lambda i, j: (0, 0)),
                  pl.BlockSpec((tn, d), lambda i, j: (j, 0))],
        out_specs=[pl.BlockSpec((tm, d), lambda i, j: (i, 0)),
                   pl.BlockSpec((tm, tn), lambda i, j: (i, j))],
        out_shape=[jax.ShapeDtypeStruct((s, d), BF16), jax.ShapeDtypeStruct((s, n), out_dtype)],
        scratch_shapes=[pltpu.VMEM((tm, d), BF16)],
        compiler_params=_params(("parallel", "arbitrary")),
    )(x, g, wt)


def _rms_bwd_rows(xv, gv, dh):
    r = lax.rsqrt(jnp.mean(xv * xv, axis=-1, keepdims=True) + RMS_EPS)
    xh = xv * r
    dn = dh * gv
    dx = r * (dn - xh * jnp.mean(dn * xh, axis=-1, keepdims=True))
    return dx, dh * xh


def _mm_rms_bwd(a, b, x, g, dres, dep, *, name, tm=512):
    s, k = a.shape
    d = b.shape[1]
    tm = min(tm, s)

    def body(a_ref, b_ref, x_ref, g_ref, dres_ref, dep_ref, dx_ref, dg_ref):
        @pl.when(pl.program_id(0) == 0)
        def _():
            dg_ref[...] = jnp.zeros_like(dg_ref)

        dh = jnp.dot(a_ref[...], b_ref[...], preferred_element_type=F32)
        dx, dgr = _rms_bwd_rows(x_ref[...], g_ref[...], dh)
        dx_ref[...] = dres_ref[...] + dx
        dg_ref[...] += jnp.sum(dgr, axis=0, keepdims=True)

    row = pl.BlockSpec((tm, d), lambda i: (i, 0))
    vec = pl.BlockSpec((1, d), lambda i: (0, 0))
    return pl.pallas_call(
        body, name=name, grid=(s // tm,),
        in_specs=[pl.BlockSpec((tm, k), lambda i: (i, 0)), pl.BlockSpec((k, d), lambda i: (0, 0)), row, vec, row,
                  pl.BlockSpec(memory_space=pl.ANY)],
        out_specs=[row, vec],
        out_shape=[jax.ShapeDtypeStruct((s, d), F32), jax.ShapeDtypeStruct((1, d), F32)],
        compiler_params=_params(("arbitrary",)),
    )(a, b, x, g, dres, dep)


def _conv_taps(ext_ref, w_ref, n_taps, base, r0, rows, reverse=False):
    acc = None
    for k in range(n_taps):
        off = r0 + (base - k if reverse else base + k)
        term = w_ref[k:k + 1, :] * ext_ref[pl.ds(off, rows), :]
        acc = term if acc is None else acc + term
    return acc


def _shift_copies(src_ref, sh_ref, rows):
    for b in range(1, 8):
        sh_ref[b, 0:rows, :] = src_ref[pl.ds(b, rows), :]


def _shifted(src_ref, sh_ref, start, rows, off):
    a, b = divmod(off, 8)
    ref = src_ref if b == 0 else sh_ref.at[b]
    return ref[pl.ds(start + 8 * a, rows), :]


def _mix0_fwd(z, conv_a, ln_g, ln_b, conv_b, w_out, x, *, name, ts=512, rc=32):
    s = z.shape[0]
    c = A_WIDTH
    hb = HALO_A
    half = ts // 2

    def body(z_ref, zp_ref, wa_ref, lg_ref, lb_ref, wb_ref, wo_ref, x_ref, ab_ref, ca_ref, o_ref, exta, extb, sha):
        keep = jnp.where(pl.program_id(0) > 0, 1.0, 0.0)
        zp = zp_ref[...]
        exta[0:hb, :] = zp[:, 0:c] * _sigmoid(zp[:, c:2 * c]) * keep
        extb[0:hb, :] = zp[:, 3 * c:4 * c] * zp[:, 4 * c:5 * c] * keep
        exta[hb:hb + ts, :] = z_ref[:, 0:c] * _sigmoid(z_ref[:, c:2 * c])
        extb[hb:hb + ts, :] = z_ref[:, 3 * c:4 * c] * z_ref[:, 4 * c:5 * c]
        _shift_copies(exta, sha, hb + ts - 8)
        lg = lg_ref[...]
        lb = lb_ref[...]
        for q in range(ts // rc):
            r0 = q * rc
            ca = None
            for k in range(A_TAPS):
                term = wa_ref[k:k + 1, :] * _shifted(exta, sha, r0, rc, hb - (A_TAPS - 1) + k)
                ca = term if ca is None else ca + term
            ca_ref[r0:r0 + rc, :] = ca
            mu = jnp.mean(ca, axis=-1, keepdims=True)
            xc = ca - mu
            rs = lax.rsqrt(jnp.mean(xc * xc, axis=-1, keepdims=True) + LN_EPS)
            l = xc * rs * lg + lb
            ab_ref[r0:r0 + rc, 0:c] = (l * _sigmoid(l)).astype(BF16)
            cbc = _conv_taps(extb, wb_ref, 3, hb - 2, r0, rc)
            ab_ref[r0:r0 + rc, c:2 * c] = (z_ref[r0:r0 + rc, 2 * c:3 * c] * cbc).astype(BF16)
            if (r0 + rc) % half == 0:
                rows = slice(r0 + rc - half, r0 + rc)
                o_ref[rows, :] = x_ref[rows, :] + jnp.dot(ab_ref[rows, :], wo_ref[...], preferred_element_type=F32)

    d = x.shape[1]
    row = pl.BlockSpec((ts, d), lambda i: (i, 0))
    return pl.pallas_call(
        body, name=name, grid=(s // ts,),
        in_specs=[pl.BlockSpec((ts, 5 * c), lambda i: (i, 0)),
                  pl.BlockSpec((hb, 5 * c), lambda i: (_prev_blk(i, ts, hb), 0)),
                  pl.BlockSpec((32, c), lambda i: (0, 0)),
                  pl.BlockSpec((1, c), lambda i: (0, 0)),
                  pl.BlockSpec((1, c), lambda i: (0, 0)),
                  pl.BlockSpec((8, c), lambda i: (0, 0)),
                  pl.BlockSpec((2 * c, d), lambda i: (0, 0)), row],
        out_specs=[pl.BlockSpec((ts, 2 * c), lambda i: (i, 0)),
                   pl.BlockSpec((ts, c), lambda i: (i, 0)), row],
        out_shape=[jax.ShapeDtypeStruct((s, 2 * c), BF16), jax.ShapeDtypeStruct((s, c), F32),
                   jax.ShapeDtypeStruct((s, d), F32)],
        scratch_shapes=[pltpu.VMEM((hb + ts, c), F32), pltpu.VMEM((hb + ts, c), F32),
                        pltpu.VMEM((8, hb + ts - 8, c), F32)],
        compiler_params=_params(("parallel",)),
    )(z, z, conv_a, ln_g, ln_b, conv_b, w_out, x)


def _mix0_bwd(z, ca, dab, conv_a, ln_g, ln_b, conv_b, *, name, ts=512, rc=32):
    s = z.shape[0]
    c = A_WIDTH
    hb = HALO_A
    ta = A_TAPS

    def body(z_ref, zp_ref, zn_ref, ca_ref, can_ref, d_ref, dn_ref, wa_ref, lg_ref, lb_ref, wb_ref,
             dz_ref, dwa_ref, dwb_ref, dlg_ref, dlb_ref, exta, extb, extdca, extdcb, shd):
        i = pl.program_id(0)
        keep_p = jnp.where(i > 0, 1.0, 0.0)
        keep_n = jnp.where(i < s // ts - 1, 1.0, 0.0)

        @pl.when(i == 0)
        def _():
            dwa_ref[...] = jnp.zeros_like(dwa_ref)
            dwb_ref[...] = jnp.zeros_like(dwb_ref)
            dlg_ref[...] = jnp.zeros_like(dlg_ref)
            dlb_ref[...] = jnp.zeros_like(dlb_ref)

        lg = lg_ref[...]
        lb = lb_ref[...]
        zp = zp_ref[...]
        exta[0:hb, :] = zp[:, 0:c] * _sigmoid(zp[:, c:2 * c]) * keep_p
        extb[0:hb, :] = zp[:, 3 * c:4 * c] * zp[:, 4 * c:5 * c] * keep_p
        exta[hb:hb + ts, :] = z_ref[:, 0:c] * _sigmoid(z_ref[:, c:2 * c])
        extb[hb:hb + ts, :] = z_ref[:, 3 * c:4 * c] * z_ref[:, 4 * c:5 * c]

        def ln_bwd(cav, dav):
            mu = jnp.mean(cav, axis=-1, keepdims=True)
            xc = cav - mu
            rs = lax.rsqrt(jnp.mean(xc * xc, axis=-1, keepdims=True) + LN_EPS)
            nv = xc * rs
            l = nv * lg + lb
            sg = _sigmoid(l)
            dl = dav * (sg * (1.0 + l * (1.0 - sg)))
            dnv = dl * lg
            dca = rs * (dnv - jnp.mean(dnv, axis=-1, keepdims=True)
                        - nv * jnp.mean(dnv * nv, axis=-1, keepdims=True))
            return dca, dl, nv

        dlg_acc = jnp.zeros((1, c), F32)
        dlb_acc = jnp.zeros((1, c), F32)
        for q in range(ts // rc):
            r0 = q * rc
            dca, dl, nv = ln_bwd(ca_ref[r0:r0 + rc, :], d_ref[r0:r0 + rc, 0:c])
            extdca[r0:r0 + rc, :] = dca
            dlg_acc = dlg_acc + jnp.sum(dl * nv, axis=0, keepdims=True)
            dlb_acc = dlb_acc + jnp.sum(dl, axis=0, keepdims=True)
            extdcb[r0:r0 + rc, :] = d_ref[r0:r0 + rc, c:2 * c] * z_ref[r0:r0 + rc, 2 * c:3 * c]
        dca_n, _, _ = ln_bwd(can_ref[...], dn_ref[:, 0:c])
        extdca[ts:ts + hb, :] = dca_n * keep_n
        extdcb[ts:ts + hb, :] = dn_ref[:, c:2 * c] * zn_ref[:, 2 * c:3 * c] * keep_n
        dlg_ref[...] += dlg_acc
        dlb_ref[...] += dlb_acc
        _shift_copies(extdca, shd, ts + hb - 8)

        for q in range(ts // rc):
            r0 = q * rc
            zr = z_ref[r0:r0 + rc, :]
            dga = None
            for k in range(ta):
                term = wa_ref[k:k + 1, :] * _shifted(extdca, shd, r0, rc, ta - 1 - k)
                dga = term if dga is None else dga + term
            sg = _sigmoid(zr[:, c:2 * c])
            dz_ref[r0:r0 + rc, 0:c] = (dga * sg).astype(BF16)
            dz_ref[r0:r0 + rc, c:2 * c] = (dga * zr[:, 0:c] * sg * (1.0 - sg)).astype(BF16)
            cbc = _conv_taps(extb, wb_ref, 3, hb - 2, r0, rc)
            dz_ref[r0:r0 + rc, 2 * c:3 * c] = (d_ref[r0:r0 + rc, c:2 * c] * cbc).astype(BF16)
            dcb = _conv_taps(extdcb, wb_ref, 3, 2, r0, rc, reverse=True)
            dz_ref[r0:r0 + rc, 3 * c:4 * c] = (dcb * zr[:, 4 * c:5 * c]).astype(BF16)
            dz_ref[r0:r0 + rc, 4 * c:5 * c] = (dcb * zr[:, 3 * c:4 * c]).astype(BF16)

        for k in range(ta):
            part = None
            for q in range(ts // rc):
                r0 = q * rc
                p = exta[hb + r0:hb + r0 + rc, :] * _shifted(extdca, shd, r0, rc, ta - 1 - k)
                for r in range(0, rc, 8):
                    part = p[r:r + 8, :] if part is None else part + p[r:r + 8, :]
            dwa_ref[k:k + 1, :] += jnp.sum(part, axis=0, keepdims=True)
        dcb_t = extdcb[0:ts, :]
        for k in range(3):
            dwb_ref[k:k + 1, :] += jnp.sum(dcb_t * extb[pl.ds(hb - 2 + k, ts), :], axis=0, keepdims=True)

    def tile(w):
        return pl.BlockSpec((ts, w), lambda i: (i, 0))

    def prev(w):
        return pl.BlockSpec((hb, w), lambda i: (_prev_blk(i, ts, hb), 0))

    def nxt(w):
        return pl.BlockSpec((hb, w), lambda i: (_next_blk(i, ts, hb, s), 0))

    def const(r, w):
        return pl.BlockSpec((r, w), lambda i: (0, 0))

    return pl.pallas_call(
        body, name=name, grid=(s // ts,),
        in_specs=[tile(5 * c), prev(5 * c), nxt(5 * c), tile(c), nxt(c), tile(2 * c), nxt(2 * c),
                  const(32, c), const(1, c), const(1, c), const(8, c)],
        out_specs=[tile(5 * c), const(32, c), const(8, c), const(1, c), const(1, c)],
        out_shape=[jax.ShapeDtypeStruct((s, 5 * c), BF16), jax.ShapeDtypeStruct((32, c), F32),
                   jax.ShapeDtypeStruct((8, c), F32), jax.ShapeDtypeStruct((1, c), F32),
                   jax.ShapeDtypeStruct((1, c), F32)],
        scratch_shapes=[pltpu.VMEM((hb + ts, c), F32), pltpu.VMEM((hb + ts, c), F32),
                        pltpu.VMEM((ts + hb, c), F32), pltpu.VMEM((ts + hb, c), F32),
                        pltpu.VMEM((8, ts + hb - 8, c), F32)],
        compiler_params=_params(("arbitrary",)),
    )(z, z, z, ca, ca, dab, dab, conv_a, ln_g, ln_b, conv_b)


def _ffn_up(x, g, w8, *, name, tm=1024):
    s, d = x.shape
    nb, c, _ = w8.shape
    tm = min(tm, s)
    grp = N_PAIR

    def body(x_ref, g_ref, w_ref, h_ref, u_ref, hs_ref):
        @pl.when(pl.program_id(1) == 0)
        def _():
            hv = _rms_rows(x_ref[...], g_ref[...]).astype(BF16)
            hs_ref[...] = hv
            h_ref[...] = hv

        res = lax.dot_general(hs_ref[...], w_ref[...].reshape(grp * c, d), _NT, preferred_element_type=F32)
        for b in range(grp):
            u_ref[b] = res[:, b * c:(b + 1) * c].astype(BF16)

    return pl.pallas_call(
        body, name=name, grid=(s // tm, nb // grp),
        in_specs=[pl.BlockSpec((tm, d), lambda i, k: (i, 0)),
                  pl.BlockSpec((1, d), lambda i, k: (0, 0)),
                  pl.BlockSpec((grp, c, d), lambda i, k: (k, 0, 0))],
        out_specs=[pl.BlockSpec((tm, d), lambda i, k: (i, 0)),
                   pl.BlockSpec((grp, tm, c), lambda i, k: (k, i, 0))],
        out_shape=[jax.ShapeDtypeStruct((s, d), BF16), jax.ShapeDtypeStruct((nb, s, c), BF16)],
        scratch_shapes=[pltpu.VMEM((tm, d), BF16)],
        compiler_params=_params(("parallel", "arbitrary")),
    )(x, g, w8)


def _ffn_mid_down(up8, wc8, wd4, x, head=None, *, name, ts=256, rc=32):
    nb, s, c = up8.shape
    d = x.shape[1]
    hb = HALO_S
    ts = min(ts, s)

    def body(*refs):
        u_ref, up_ref, wc_ref, wd_ref, x_ref = refs[:5]
        n_out = 4 if head is None else 7
        outs = refs[-1 - n_out:-1]
        act_ref, ug_ref, uv_ref = outs[-3:]
        ext = refs[-1]
        keep = jnp.where(pl.program_id(0) > 0, 1.0, 0.0)
        acc = x_ref[...]
        for j in range(N_PAIR):
            eg, ev = ext.at[2 * (j % 2)], ext.at[2 * (j % 2) + 1]
            eg[0:hb, :] = up_ref[j].astype(F32) * keep
            ev[0:hb, :] = up_ref[j + N_PAIR].astype(F32) * keep
            eg[hb:hb + ts, :] = u_ref[j].astype(F32)
            ev[hb:hb + ts, :] = u_ref[j + N_PAIR].astype(F32)
            for q in range(ts // rc):
                r0 = q * rc
                gg = _conv_taps(eg, wc_ref.at[j], 3, hb - 2, r0, rc)
                vv = _conv_taps(ev, wc_ref.at[j + N_PAIR], 3, hb - 2, r0, rc)
                ug_ref[j, r0:r0 + rc, :] = gg.astype(BF16)
                uv_ref[j, r0:r0 + rc, :] = vv.astype(BF16)
                act_ref[j, r0:r0 + rc, :] = (gg * _sigmoid(gg) * vv).astype(BF16)
            acc = acc + jnp.dot(act_ref[j], wd_ref[j], preferred_element_type=F32)
        if head is None:
            outs[0][...] = acc
            return
        g_ref, t_ref = refs[5], refs[6]
        loss_ref, dx_ref, dxb_ref, dg_ref = outs[:4]

        @pl.when(pl.program_id(0) == 0)
        def _():
            loss_ref[...] = jnp.zeros_like(loss_ref)
            dg_ref[...] = jnp.zeros_like(dg_ref)

        gv = g_ref[...]
        r = lax.rsqrt(jnp.mean(acc * acc, axis=-1, keepdims=True) + RMS_EPS)
        xh = acc * r
        err = xh * gv - t_ref[...]
        loss_ref[...] += 0.5 * jnp.sum(jnp.mean(err * err, axis=-1, keepdims=True), axis=0, keepdims=True)
        dy = err * (1.0 / d)
        dn = dy * gv
        dx = r * (dn - xh * jnp.mean(dn * xh, axis=-1, keepdims=True))
        dx_ref[...] = dx
        dxb_ref[...] = dx.astype(BF16)
        dg_ref[...] += jnp.sum(dy * xh, axis=0, keepdims=True)

    row = pl.BlockSpec((ts, d), lambda i: (i, 0))
    vec = pl.BlockSpec((1, d), lambda i: (0, 0))
    tile = pl.BlockSpec((N_PAIR, ts, c), lambda i: (0, i, 0))
    half = jax.ShapeDtypeStruct((N_PAIR, s, c), BF16)
    in_specs = [pl.BlockSpec((nb, ts, c), lambda i: (0, i, 0)),
                pl.BlockSpec((nb, hb, c), lambda i: (0, _prev_blk(i, ts, hb), 0)),
                pl.BlockSpec((nb, 8, c), lambda i: (0, 0, 0)),
                pl.BlockSpec((N_PAIR, c, d), lambda i: (0, 0, 0)), row]
    args = [up8, up8, wc8, wd4, x]
    if head is None:
        out_specs = [row, tile, tile, tile]
        out_shape = [jax.ShapeDtypeStruct((s, d), F32), half, half, half]
    else:
        in_specs += [vec, row]
        args += list(head)
        out_specs = [pl.BlockSpec((1, 1), lambda i: (0, 0)), row, row, vec, tile, tile, tile]
        out_shape = [jax.ShapeDtypeStruct((1, 1), F32), jax.ShapeDtypeStruct((s, d), F32),
                     jax.ShapeDtypeStruct((s, d), BF16), jax.ShapeDtypeStruct((1, d), F32), half, half, half]
    return pl.pallas_call(
        body, name=name, grid=(s // ts,), in_specs=in_specs, out_specs=out_specs, out_shape=out_shape,
        scratch_shapes=[pltpu.VMEM((4, hb + ts, c), F32)],
        compiler_params=_params(("parallel",) if head is None else ("arbitrary",)),
    )(*args)


def _ffn_dwdown(act4, db, *, name):
    npair, s, c = act4.shape
    d = db.shape[1]

    def body(a_ref, d_ref, o_ref):
        o_ref[...] = lax.dot_general(a_ref[...], d_ref[...], (((0,), (0,)), ((), ())),
                                     preferred_element_type=F32).astype(BF16)

    return pl.pallas_call(
        body, name=name, grid=(npair,),
        in_specs=[pl.BlockSpec((None, s, c), lambda j: (j, 0, 0)),
                  pl.BlockSpec((s, d), lambda j: (0, 0))],
        out_specs=pl.BlockSpec((None, c, d), lambda j: (j, 0, 0)),
        out_shape=jax.ShapeDtypeStruct((npair, c, d), BF16),
        compiler_params=_params(("parallel",)),
    )(act4, db)


def _ffn_dact(db, wd4, *, name, tm=1024):
    s, d = db.shape
    npair, c, _ = wd4.shape
    tm = min(tm, s)

    def body(d_ref, w_ref, o_ref):
        res = lax.dot_general(d_ref[...], w_ref[...].reshape(npair * c, d), _NT, preferred_element_type=F32)
        for j in range(npair):
            o_ref[j] = res[:, j * c:(j + 1) * c].astype(BF16)

    return pl.pallas_call(
        body, name=name, grid=(s // tm,),
        in_specs=[pl.BlockSpec((tm, d), lambda i: (i, 0)),
                  pl.BlockSpec((npair, c, d), lambda i: (0, 0, 0))],
        out_specs=pl.BlockSpec((npair, tm, c), lambda i: (0, i, 0)),
        out_shape=jax.ShapeDtypeStruct((npair, s, c), BF16),
        compiler_params=_params(("parallel",)),
    )(db, wd4)


def _ffn_midbwd(up8, ug4, uv4, dact4, wc8, *, name, ts=1024, rc=32):
    nb, s, c = up8.shape
    hb = HALO_S
    ts = min(ts, s)
    n_i = s // ts

    def body(pg_ref, pv_ref, ug_ref, ugn_ref, uv_ref, uvn_ref, d_ref, dn_ref, wg_ref, wv_ref,
             dg_ref, dv_ref, dwg_ref, dwv_ref, extdg, extdv):
        i = pl.program_id(1)
        keep_n = jnp.where(i < n_i - 1, 1.0, 0.0)

        @pl.when(i == 0)
        def _():
            dwg_ref[...] = jnp.zeros_like(dwg_ref)
            dwv_ref[...] = jnp.zeros_like(dwv_ref)

        def du_rows(r0, rows, gg, vv, dav):
            sg = _sigmoid(gg)
            extdg[r0:r0 + rows, :] = dav * vv * (sg * (1.0 + gg * (1.0 - sg)))
            extdv[r0:r0 + rows, :] = dav * (gg * sg)

        for q in range(ts // rc):
            rows = slice(q * rc, q * rc + rc)
            du_rows(q * rc, rc, ug_ref[rows, :].astype(F32), uv_ref[rows, :].astype(F32),
                    d_ref[rows, :].astype(F32))
        du_rows(ts, 8, ugn_ref[0:8, :].astype(F32), uvn_ref[0:8, :].astype(F32),
                dn_ref[0:8, :].astype(F32) * keep_n)

        def fold(p):
            acc = p[0:8, :]
            for r in range(8, rc, 8):
                acc = acc + p[r:r + 8, :]
            return acc

        for extd, p_ref, w_ref, out_ref, dw_ref in ((extdg, pg_ref, wg_ref, dg_ref, dwg_ref),
                                                    (extdv, pv_ref, wv_ref, dv_ref, dwv_ref)):
            part = [None, None, None]
            for q in range(ts // rc):
                r0 = q * rc
                pre = p_ref[r0:r0 + rc, :].astype(F32)
                dup = None
                for k in range(3):
                    sh = extd[pl.ds(r0 + 2 - k, rc), :]
                    term = w_ref[k:k + 1, :] * sh
                    dup = term if dup is None else dup + term
                    prod = fold(pre * sh)
                    part[k] = prod if part[k] is None else part[k] + prod
                out_ref[r0:r0 + rc, :] = dup.astype(BF16)
            for k in range(3):
                dw_ref[k:k + 1, :] += jnp.sum(part[k], axis=0, keepdims=True)

    def blk(off, nxt):
        if nxt:
            return pl.BlockSpec((None, hb, c), lambda j, i: (j + off, _next_blk(i, ts, hb, s), 0))
        return pl.BlockSpec((None, ts, c), lambda j, i: (j + off, i, 0))

    def taps(off):
        return pl.BlockSpec((None, 8, c), lambda j, i: (j + off, 0, 0))

    tile = blk(0, False)
    acc = pl.BlockSpec((None, 8, c), lambda j, i: (j, 0, 0))
    return pl.pallas_call(
        body, name=name, grid=(N_PAIR, n_i),
        in_specs=[tile, blk(N_PAIR, False), tile, blk(0, True), tile, blk(0, True), tile, blk(0, True),
                  taps(0), taps(N_PAIR)],
        out_specs=[tile, tile, acc, acc],
        out_shape=[jax.ShapeDtypeStruct((N_PAIR, s, c), BF16), jax.ShapeDtypeStruct((N_PAIR, s, c), BF16),
                   jax.ShapeDtypeStruct((N_PAIR, 8, c), F32), jax.ShapeDtypeStruct((N_PAIR, 8, c), F32)],
        scratch_shapes=[pltpu.VMEM((ts + 8, c), F32), pltpu.VMEM((ts + 8, c), F32)],
        compiler_params=_params(("parallel", "arbitrary")),
    )(up8, up8, ug4, ug4, uv4, uv4, dact4, dact4, wc8, wc8)


def _ffn_dh(dupg, dupv, w8, x, g, dres, dep, *, name, tm=512):
    npair, s, c = dupg.shape
    d = x.shape[1]
    tm = min(tm, s)

    def body(dg_ref, dv_ref, w_ref, x_ref, g_ref, dres_ref, dep_ref, dx_ref, dxb_ref, dgain_ref):
        @pl.when(pl.program_id(0) == 0)
        def _():
            dgain_ref[...] = jnp.zeros_like(dgain_ref)

        dh = None
        for half, src in enumerate((dg_ref, dv_ref)):
            lhs = jnp.concatenate([src[j] for j in range(npair)], axis=1)
            rhs = w_ref[half * npair:(half + 1) * npair].reshape(npair * c, d)
            part = jnp.dot(lhs, rhs, preferred_element_type=F32)
            dh = part if dh is None else dh + part
        dx, dgr = _rms_bwd_rows(x_ref[...], g_ref[...], dh)
        dx = dres_ref[...] + dx
        dx_ref[...] = dx
        dxb_ref[...] = dx.astype(BF16)
        dgain_ref[...] += jnp.sum(dgr, axis=0, keepdims=True)

    row = pl.BlockSpec((tm, d), lambda i: (i, 0))
    vec = pl.BlockSpec((1, d), lambda i: (0, 0))
    dup = pl.BlockSpec((npair, tm, c), lambda i: (0, i, 0))
    return pl.pallas_call(
        body, name=name, grid=(s // tm,),
        in_specs=[dup, dup, pl.BlockSpec((2 * npair, c, d), lambda i: (0, 0, 0)), row, vec, row,
                  pl.BlockSpec(memory_space=pl.ANY)],
        out_specs=[row, row, vec],
        out_shape=[jax.ShapeDtypeStruct((s, d), F32), jax.ShapeDtypeStruct((s, d), BF16),
                   jax.ShapeDtypeStruct((1, d), F32)],
        compiler_params=_params(("arbitrary",)),
    )(dupg, dupv, w8, x, g, dres, dep)


def _ffn_dwup(h, dupg, dupv, *, name, tm=1024):
    npair, s, c = dupg.shape
    d = h.shape[1]

    def body(h_ref, dg_ref, dv_ref, o_ref):
        k = pl.program_id(1)

        @pl.when(k < npair)
        def _():
            o_ref[...] = lax.dot_general(dg_ref[...], h_ref[...], _TN, preferred_element_type=F32).astype(BF16)

        @pl.when(k >= npair)
        def _():
            o_ref[...] = lax.dot_general(dv_ref[...], h_ref[...], _TN, preferred_element_type=F32).astype(BF16)

    return pl.pallas_call(
        body, name=name, grid=(d // tm, 2 * npair),
        in_specs=[pl.BlockSpec((s, tm), lambda m, k: (0, m)),
                  pl.BlockSpec((None, s, c), lambda m, k: (jnp.minimum(k, npair - 1), 0, 0)),
                  pl.BlockSpec((None, s, c), lambda m, k: (jnp.maximum(k - npair, 0), 0, 0))],
        out_specs=pl.BlockSpec((None, c, tm), lambda m, k: (k, 0, m)),
        out_shape=jax.ShapeDtypeStruct((2 * npair, c, d), BF16),
        compiler_params=_params(("parallel", "arbitrary")),
    )(h, dupg, dupv)


def _pool_counts(i, ts, rows, window):
    t = lax.broadcasted_iota(jnp.int32, (rows, 1), 0) + i * ts + 1
    return jnp.minimum(t, window).astype(F32)


def _pool_fwd(x, g, w_pool, scale, *, name, ts=512):
    s, d = x.shape
    hb = HALO_A
    pg = POOL_GROUP
    ts = min(ts, s)

    def body(x_ref, xp_ref, g_ref, w_ref, s_ref, o_ref, p_ref, yu_ref, ext, sa, sb):
        i = pl.program_id(0)
        keep = jnp.where(i > 0, 1.0, 0.0)
        gv = g_ref[...]
        ext[0:hb, :] = _rms_rows(xp_ref[...], gv) * keep
        ext[hb:hb + ts, :] = _rms_rows(x_ref[...], gv)
        rows = hb + ts - 8
        sa[0:8, :] = jnp.zeros((8, d), F32)
        sb[0:8, :] = jnp.zeros((8, d), F32)
        for gi, w in enumerate(POOL_WINDOWS):
            cols = slice(gi * pg, (gi + 1) * pg)
            cur, nxt, k = ext, sa, 1
            while k < w:
                nxt[8:8 + rows, cols] = cur[8:8 + rows, cols] + cur[pl.ds(8 - k, rows), cols]
                cur, nxt, k = nxt, (sb if nxt is sa else sa), 2 * k
            h = ext[hb:hb + ts, cols]
            pv = (cur[hb:hb + ts, cols] / _pool_counts(i, ts, ts, w) - h).astype(BF16)
            p_ref[:, cols] = pv
            yu = jnp.dot(pv, w_ref[gi], preferred_element_type=F32)
            yu_ref[:, cols] = yu.astype(BF16)
            o_ref[:, cols] = x_ref[:, cols] + yu * s_ref[:, cols]

    row = pl.BlockSpec((ts, d), lambda i: (i, 0))
    vec = pl.BlockSpec((1, d), lambda i: (0, 0))
    return pl.pallas_call(
        body, name=name, grid=(s // ts,),
        in_specs=[row, pl.BlockSpec((hb, d), lambda i: (_prev_blk(i, ts, hb), 0)), vec,
                  pl.BlockSpec((d // pg, pg, pg), lambda i: (0, 0, 0)), vec],
        out_specs=[row, row, row],
        out_shape=[jax.ShapeDtypeStruct((s, d), F32), jax.ShapeDtypeStruct((s, d), BF16),
                   jax.ShapeDtypeStruct((s, d), BF16)],
        scratch_shapes=[pltpu.VMEM((hb + ts, d), F32)] * 3,
        compiler_params=_params(("parallel",)),
    )(x, x, g, w_pool, scale)


def _pool_dw(p, dyc, *, name):
    s, d = p.shape
    pg = POOL_GROUP

    def body(p_ref, d_ref, o_ref):
        o_ref[...] = lax.dot_general(p_ref[...], d_ref[...], (((0,), (0,)), ((), ())),
                                     preferred_element_type=F32).astype(BF16)

    blk = pl.BlockSpec((s, pg), lambda gi: (0, gi))
    return pl.pallas_call(
        body, name=name, grid=(d // pg,),
        in_specs=[blk, blk], out_specs=pl.BlockSpec((None, pg, pg), lambda gi: (gi, 0, 0)),
        out_shape=jax.ShapeDtypeStruct((d // pg, pg, pg), BF16),
        compiler_params=_params(("parallel",)),
    )(p, dyc)


def _pool_bwd(dres, w_pool, scale, yu, x, g, *, name, ts=512):
    s, d = x.shape
    hb = HALO_A
    pg = POOL_GROUP
    ts = min(ts, s)
    n_i = s // ts

    def body(dres_ref, dresn_ref, w_ref, s_ref, yu_ref, x_ref, g_ref,
             dx_ref, dxb_ref, dg_ref, dyc_ref, ds_ref, ext, dh, sa, sb, dp_s):
        i = pl.program_id(0)
        keep_n = jnp.where(i < n_i - 1, 1.0, 0.0)

        @pl.when(i == 0)
        def _():
            dg_ref[...] = jnp.zeros_like(dg_ref)
            ds_ref[...] = jnp.zeros_like(ds_ref)

        ds_ref[...] += jnp.sum(dres_ref[...] * yu_ref[...].astype(F32), axis=0, keepdims=True)
        for gi, w in enumerate(POOL_WINDOWS):
            cols = slice(gi * pg, (gi + 1) * pg)
            dyc = (dres_ref[:, cols] * s_ref[:, cols]).astype(BF16)
            dyc_ref[:, cols] = dyc
            dp = lax.dot_general(dyc, w_ref[gi], _NT, preferred_element_type=F32)
            dp_s[:, cols] = dp
            ext[0:ts, cols] = dp / _pool_counts(i, ts, ts, w)
            dycn = (dresn_ref[:, cols] * s_ref[:, cols]).astype(BF16)
            dpn = lax.dot_general(dycn, w_ref[gi], _NT, preferred_element_type=F32)
            ext[ts:ts + hb, cols] = dpn / _pool_counts(i + 1, ts, hb, w) * keep_n
        rows = ts + hb - 8
        sa[rows:rows + 8, :] = jnp.zeros((8, d), F32)
        sb[rows:rows + 8, :] = jnp.zeros((8, d), F32)
        for gi, w in enumerate(POOL_WINDOWS):
            cols = slice(gi * pg, (gi + 1) * pg)
            cur, nxt, k = ext, sa, 1
            while k < w:
                nxt[0:rows, cols] = cur[0:rows, cols] + cur[pl.ds(k, rows), cols]
                cur, nxt, k = nxt, (sb if nxt is sa else sa), 2 * k
            dh[:, cols] = cur[0:ts, cols] - dp_s[:, cols]
        dx, dgr = _rms_bwd_rows(x_ref[...], g_ref[...], dh[...])
        dx = dres_ref[...] + dx
        dx_ref[...] = dx
        dxb_ref[...] = dx.astype(BF16)
        dg_ref[...] += jnp.sum(dgr, axis=0, keepdims=True)

    row = pl.BlockSpec((ts, d), lambda i: (i, 0))
    vec = pl.BlockSpec((1, d), lambda i: (0, 0))
    return pl.pallas_call(
        body, name=name, grid=(n_i,),
        in_specs=[row, pl.BlockSpec((hb, d), lambda i: (_next_blk(i, ts, hb, s), 0)),
                  pl.BlockSpec((d // pg, pg, pg), lambda i: (0, 0, 0)), vec, row, row, vec],
        out_specs=[row, row, vec, row, vec],
        out_shape=[jax.ShapeDtypeStruct((s, d), F32), jax.ShapeDtypeStruct((s, d), BF16),
                   jax.ShapeDtypeStruct((1, d), F32), jax.ShapeDtypeStruct((s, d), BF16),
                   jax.ShapeDtypeStruct((1, d), F32)],
        scratch_shapes=[pltpu.VMEM((ts + hb, d), F32), pltpu.VMEM((ts, d), F32),
                        pltpu.VMEM((ts + hb, d), F32), pltpu.VMEM((ts + hb, d), F32),
                        pltpu.VMEM((ts, d), F32)],
        compiler_params=_params(("arbitrary",)),
    )(dres, dres, w_pool, scale, yu, x, g)


def _pad_rows(w, rows):
    pad = [(0, 0)] * (w.ndim - 2) + [(0, rows - w.shape[-2]), (0, 0)]
    return jnp.pad(w, pad)


def _ffn_layer_fwd(x, nf, w8, wc8, wd4, tag, head=None):
    h, up8 = _ffn_up(x, nf, w8, name=f"ffn{tag}_up")
    res = _ffn_mid_down(up8, wc8, wd4, x, head, name=f"ffn{tag}_mid_down")
    act4, ug4, uv4 = res[-3:]
    out = res[0] if head is None else tuple(res[:4])
    return out, (h, up8, ug4, uv4, act4)


def _ffn_layer_bwd(d, db, x, nf, w8, wc8, wd4, saved, tag, grads_ready):
    h, up8, ug4, uv4, act4 = saved
    dact4 = _ffn_dact(db, wd4, name=f"ffn{tag}_dact")
    dwd4 = _ffn_dwdown(act4, db, name=f"ffn{tag}_dwdown")
    dupg, dupv, dwg, dwv = _ffn_midbwd(up8, ug4, uv4, dact4, wc8, name=f"ffn{tag}_midbwd")
    dw8 = _ffn_dwup(h, dupg, dupv, name=f"ffn{tag}_dwup")
    sent = grads_ready(f"ffn{tag}", {f"w_up{tag}": dw8, f"w_down{tag}": dwd4})
    dx, dxb, dnf = _ffn_dh(dupg, dupv, w8, x, nf, d, sent, name=f"ffn{tag}_dh")
    dwc8 = jnp.concatenate([dwg, dwv], axis=0)[:, :3]
    return dx, dxb, dnf, dwc8


def _local_step(x, target, weights, grads_ready):
    w0 = weights("mix0", x)
    wa = _pad_rows(w0["conv_a"], 32)
    wb = _pad_rows(w0["conv_b"], 8)
    wc = [_pad_rows(w0["conv_ffn"][l], 8) for l in range(2)]
    h0, z = _rms_mm(x, w0["norm_mix_even"], w0["w_in_t"], name="mix0_in", out_dtype=F32, tn=2560)
    ab, ca, x1 = _mix0_fwd(z, wa, w0["ln_a_g"], w0["ln_a_b"], wb, w0["w_out"], x, name="mix0_mid")
    w1 = weights("ffn0", x1)
    x2, ffn0 = _ffn_layer_fwd(x1, w0["norm_ffn"][0:1], w1["w_up"], wc[0], w1["w_down"], 0)
    w2 = weights("ffn1", x2)
    x3, p, yu = _pool_fwd(x2, w0["norm_mix_odd"], w2["w_pool"], w0["pool_scale"], name="pool_fwd")
    (loss, d4, d4b, g_norm_final), ffn1 = _ffn_layer_fwd(
        x3, w0["norm_ffn"][1:2], w2["w_up"], wc[1], w2["w_down"], 1, head=(w0["norm_final"], target))

    d3, d3b, g_nf1, g_wc1 = _ffn_layer_bwd(
        d4, d4b, x3, w0["norm_ffn"][1:2], w2["w_up"], wc[1], w2["w_down"], ffn1, 1, grads_ready)
    d2, d2b, g_nmo, dyc, g_scale = _pool_bwd(d3, w2["w_pool"], w0["pool_scale"], yu, x2, w0["norm_mix_odd"],
                                             name="pool_bwd")
    g_pool = _pool_dw(p, dyc, name="pool_dw")
    d1, d1b, g_nf0, g_wc0 = _ffn_layer_bwd(
        d2, d2b, x1, w0["norm_ffn"][0:1], w1["w_up"], wc[0], w1["w_down"], ffn0, 0, grads_ready)
    dab = _mm(d1b, w0["w_out"], tb=True, name="mix0_dab", tm=1024, tn=1024)
    g_out = _mm(ab, d1b, ta=True, out_dtype=BF16, name="mix0_dwout", tm=1024, tn=512)
    dz, g_wa, g_wb, g_lg, g_lb = _mix0_bwd(z, ca, dab, wa, w0["ln_a_g"], w0["ln_a_b"], wb, name="mix0_midbwd")
    g_in = _mm(dz, h0, ta=True, out_dtype=BF16, name="mix0_dwin", tm=512, tn=1024)
    sent_mix = grads_ready("mix0", {"w_in": g_in, "w_out": g_out, "w_pool": g_pool})
    dx, g_nme = _mm_rms_bwd(dz, w0["w_in_t"], x, w0["norm_mix_even"], d1, sent_mix, name="mix0_dh")

    small = {
        "norm_mix_even": g_nme, "conv_a": g_wa[:A_TAPS], "ln_a_g": g_lg, "ln_a_b": g_lb, "conv_b": g_wb[:3],
        "norm_mix_odd": g_nmo, "pool_scale": g_scale, "norm_ffn": jnp.concatenate([g_nf0, g_nf1], axis=0),
        "conv_ffn": [g_wc0, g_wc1], "norm_final": g_norm_final,
    }
    return loss[0, 0], dx, small


def _my_pos():
    return lax.axis_index("x"), lax.axis_index("y"), lax.axis_index("c")


def _flip(pos, r):
    x, y, c = pos
    return (1 - x if r & 4 else x, 1 - y if r & 2 else y, 1 - c if r & 1 else c)


def _dev_index(pos):
    return 4 * pos[0] + 2 * pos[1] + pos[2]


_HBM = pl.BlockSpec(memory_space=pltpu.HBM)
_SEM = pl.BlockSpec(memory_space=pltpu.SEMAPHORE)
_EFFECT = pltpu.SideEffectType.DATAFLOW_SIDE_EFFECTING


def _exchange_copy(ins, lands, send_sems, recv_sems, scatter, pos, a, r, receiving):
    me = _dev_index(pos)
    peer = _flip(pos, r)
    dest = _dev_index(pos) if receiving else _dev_index(peer)
    src = ins[a].at[dest] if scatter[a] else ins[a]
    slot = _dev_index(peer) if receiving else me
    return pltpu.make_async_remote_copy(
        src_ref=src, dst_ref=lands[a].at[slot], send_sem=send_sems.at[a * (N_DEV - 1) + r - 1],
        recv_sem=recv_sems.at[a * (N_DEV - 1) + r - 1],
        device_id=peer, device_id_type=pl.DeviceIdType.MESH)


ALL_PEERS = tuple(range(1, N_DEV))
CHIP_PEERS = (1, 2, 4, 6)
FORWARDED = (2, 4, 6)


def _exchange_start(arrays, scatter, after, *, name, peers=ALL_PEERS):
    n = len(arrays)
    me = _dev_index(_my_pos())
    lands = []
    for arr, sc in zip(arrays, scatter):
        own = lax.dynamic_index_in_dim(arr, me, 0, keepdims=True) if sc else arr[None]
        shape = arr.shape if sc else (N_DEV,) + arr.shape
        lands.append(lax.dynamic_update_slice(lax.empty(shape, arr.dtype), own, (me,) + (0,) * (len(shape) - 1)))

    def body(*refs):
        ins, lnd = refs[:n], refs[n:2 * n]
        send_sems, recv_sems = refs[2 * n + 1], refs[2 * n + 2]
        token = refs[-1]
        pos = _my_pos()
        for a in range(n):
            for r in peers:
                _exchange_copy(ins, lnd, send_sems, recv_sems, scatter, pos, a, r, receiving=False).start()
        token[...] = jnp.zeros_like(token)

    bufs = [pltpu.with_memory_space_constraint(t, pltpu.HBM) for t in list(arrays) + lands]
    sems = pltpu.SemaphoreType.DMA((n * (N_DEV - 1),))
    res = pl.pallas_call(
        body, name=name,
        out_shape=(sems, sems, *[pltpu.HBM(t.shape, t.dtype) for t in bufs], jax.ShapeDtypeStruct((8, 128), F32)),
        in_specs=[_HBM] * (2 * n) + [pl.BlockSpec(memory_space=pl.ANY)],
        out_specs=(_SEM, _SEM, *[_HBM] * (2 * n), pl.BlockSpec(memory_space=pltpu.VMEM)),
        input_output_aliases={i: 2 + i for i in range(2 * n)},
        compiler_params=pltpu.CompilerParams(has_side_effects=_EFFECT),
    )(*bufs, after)
    return res[0], res[1], list(res[2:2 + n]), list(res[2 + n:2 + 2 * n]), res[-1]


def _exchange_wait(started, scatter, after, *, name, peers=ALL_PEERS):
    send_sems, recv_sems, arrays, lands, _ = started
    n = len(arrays)

    def body(*refs):
        ins, lnd = refs[:n], refs[n:2 * n]
        send_sems, recv_sems = refs[2 * n], refs[2 * n + 1]
        pos = _my_pos()
        for a in range(n):
            for r in peers:
                _exchange_copy(ins, lnd, send_sems, recv_sems, scatter, pos, a, r, receiving=False).wait_send()
                _exchange_copy(ins, lnd, send_sems, recv_sems, scatter, pos, a, r, receiving=True).wait_recv()

    bufs = list(arrays) + list(lands)
    after = list(after) if isinstance(after, (list, tuple)) else [after]
    res = pl.pallas_call(
        body, name=name,
        out_shape=tuple(pltpu.HBM(t.shape, t.dtype) for t in bufs),
        in_specs=[_HBM] * (2 * n) + [_SEM, _SEM] + [pl.BlockSpec(memory_space=pl.ANY)] * len(after),
        out_specs=tuple([_HBM] * (2 * n)),
        input_output_aliases={i: i for i in range(2 * n)},
        compiler_params=pltpu.CompilerParams(has_side_effects=_EFFECT),
    )(*bufs, send_sems, recv_sems, *after)
    return list(res[n:])


def _forward_copy(lands, send_sems, recv_sems, pos, a, q, receiving):
    slot = _dev_index(_flip(pos, q ^ 1 if receiving else q))
    idx = a * len(FORWARDED) + FORWARDED.index(q)
    return pltpu.make_async_remote_copy(
        src_ref=lands[a].at[slot], dst_ref=lands[a].at[slot], send_sem=send_sems.at[idx], recv_sem=recv_sems.at[idx],
        device_id=_flip(pos, 1), device_id_type=pl.DeviceIdType.MESH)


def _forward_start(lands, after, *, name):
    n = len(lands)

    def body(*refs):
        lnd = refs[:n]
        send_sems, recv_sems = refs[n + 1], refs[n + 2]
        token = refs[-1]
        pos = _my_pos()
        for a in range(n):
            for q in FORWARDED:
                _forward_copy(lnd, send_sems, recv_sems, pos, a, q, receiving=False).start()
        token[...] = jnp.zeros_like(token)

    sems = pltpu.SemaphoreType.DMA((n * len(FORWARDED),))
    res = pl.pallas_call(
        body, name=name,
        out_shape=(sems, sems, *[pltpu.HBM(t.shape, t.dtype) for t in lands], jax.ShapeDtypeStruct((8, 128), F32)),
        in_specs=[_HBM] * n + [pl.BlockSpec(memory_space=pl.ANY)],
        out_specs=(_SEM, _SEM, *[_HBM] * n, pl.BlockSpec(memory_space=pltpu.VMEM)),
        input_output_aliases={i: 2 + i for i in range(n)},
        compiler_params=pltpu.CompilerParams(has_side_effects=_EFFECT),
    )(*lands, after)
    return res[0], res[1], list(res[2:2 + n]), res[-1]


def _forward_wait(forwarded, after, *, name):
    send_sems, recv_sems, lands, _ = forwarded
    n = len(lands)

    def body(*refs):
        lnd = refs[:n]
        send_sems, recv_sems = refs[n], refs[n + 1]
        pos = _my_pos()
        for a in range(n):
            for q in FORWARDED:
                _forward_copy(lnd, send_sems, recv_sems, pos, a, q, receiving=False).wait_send()
                _forward_copy(lnd, send_sems, recv_sems, pos, a, q, receiving=True).wait_recv()

    res = pl.pallas_call(
        body, name=name,
        out_shape=tuple(pltpu.HBM(t.shape, t.dtype) for t in lands),
        in_specs=[_HBM] * n + [_SEM, _SEM, pl.BlockSpec(memory_space=pl.ANY)],
        out_specs=tuple([_HBM] * n),
        input_output_aliases={i: i for i in range(n)},
        compiler_params=pltpu.CompilerParams(has_side_effects=_EFFECT),
    )(*lands, send_sems, recv_sems, after)
    return list(res)


def _adamw_update(p_ref, w_ref, m_ref, v_ref, g_ref, d_ref, mo_ref, vo_ref):
    g = p_ref[0].astype(F32)
    for k in range(1, N_DEV):
        g = g + p_ref[k].astype(F32)
    mn = ADAM_B1 * m_ref[...] + (1.0 - ADAM_B1) * g
    vn = ADAM_B2 * v_ref[...] + (1.0 - ADAM_B2) * (g * g)
    m_hat = mn / (1.0 - ADAM_B1 ** ADAM_STEP)
    v_hat = vn / (1.0 - ADAM_B2 ** ADAM_STEP)
    g_ref[...] = g
    d_ref[...] = -ADAM_LR * (m_hat / (jnp.sqrt(v_hat) + ADAM_EPS) + ADAM_WD * w_ref[...])
    mo_ref[...] = mn
    vo_ref[...] = vn


def _adamw_small(parts, ws, ms, vs, *, name):
    n = len(ws)

    def body(*refs):
        ins, outs = refs[:4 * n], refs[4 * n:]
        for j in range(n):
            _adamw_update(ins[j], ins[n + j], ins[2 * n + j], ins[3 * n + j], *outs[4 * j:4 * j + 4])

    return pl.pallas_call(
        body, name=name,
        out_shape=[jax.ShapeDtypeStruct(w.shape, F32) for w in ws for _ in range(4)],
    )(*parts, *ws, *ms, *vs)


def _adamw(parts, w, m, v, *, name, tr):
    nl, r, c = w.shape
    assert len(parts) == nl and r % tr == 0
    n_i = r // tr

    def body(*refs):
        p_refs = refs[:nl]
        w_ref, m_ref, v_ref = refs[nl:nl + 3]

        def update(p_ref):
            _adamw_update(p_ref, w_ref, m_ref, v_ref, *refs[nl + 3:])

        if nl == 1:
            update(p_refs[0])
        else:
            for layer in range(nl):
                pl.when(pl.program_id(0) == layer)(lambda p_ref=p_refs[layer]: update(p_ref))

    def parts_spec(layer):
        def index(l, i):
            return (0, jnp.where(l < layer, 0, jnp.where(l > layer, n_i - 1, i)), 0)
        return pl.BlockSpec((N_DEV, tr, c), index)

    blk = pl.BlockSpec((None, tr, c), lambda l, i: (l, i, 0))
    return pl.pallas_call(
        body, name=name, grid=(nl, n_i),
        in_specs=[parts_spec(layer) for layer in range(nl)] + [blk, blk, blk],
        out_specs=[blk] * 4, out_shape=[jax.ShapeDtypeStruct((nl, r, c), F32)] * 4,
        compiler_params=_params(("arbitrary", "arbitrary")),
    )(*parts, w, m, v)


def _pack(parts, lead=()):
    flat = jnp.concatenate([p.reshape(lead + (-1,)) for p in parts], axis=-1)
    n = flat.shape[-1]
    rows = -(-n // (8 * 128)) * 8
    flat = jnp.pad(flat, [(0, 0)] * len(lead) + [(0, rows * 128 - n)])
    return flat.reshape(lead + (rows, 128))


def _to_dev_major(g, axis):
    shp = g.shape
    g = g.reshape(shp[:axis] + (N_DEV, shp[axis] // N_DEV) + shp[axis + 1:])
    return jnp.moveaxis(g, axis, 0)


def _from_dev_major(g, axis):
    g = jnp.moveaxis(g, 0, axis)
    shp = g.shape
    return g.reshape(shp[:axis] + (shp[axis] * shp[axis + 1],) + shp[axis + 2:])


SMALL_SHARDED = ("conv_a", "conv_b", "norm_mix_odd", "pool_scale", "conv_ffn_w")
SMALL_REPLICATED = ("norm_mix_even", "ln_a_g", "ln_a_b", "norm_ffn", "norm_final")
BIG = {"w_in": ("w_in", 0, 320), "w_out": ("w_out", 0, 128), "w_pool": ("w_pool", 0, 128),
       "w_up0": ("w_up", 0, 352), "w_up1": ("w_up", 1, 352),
       "w_down0": ("w_down", 0, 352), "w_down1": ("w_down", 1, 352)}
COLUMN_SHARDED = ("w_in", "w_up")


def kernel(x, norm_mix_even, w_in, conv_a, ln_a_g, ln_a_b, conv_b, w_out, norm_mix_odd, w_pool, pool_scale, norm_ffn, w_up, conv_ffn_w, w_down, norm_final, loss_target, m_norm_mix_even, m_w_in, m_conv_a, m_ln_a_g, m_ln_a_b, m_conv_b, m_w_out, m_norm_mix_odd, m_w_pool, m_pool_scale, m_norm_ffn, m_w_up, m_conv_ffn_w, m_w_down, m_norm_final, v_norm_mix_even, v_w_in, v_conv_a, v_ln_a_g, v_ln_a_b, v_conv_b, v_w_out, v_norm_mix_odd, v_w_pool, v_pool_scale, v_norm_ffn, v_w_up, v_conv_ffn_w, v_w_down, v_norm_final):
    names = ("norm_mix_even", "w_in", "conv_a", "ln_a_g", "ln_a_b", "conv_b", "w_out", "norm_mix_odd", "w_pool",
             "pool_scale", "norm_ffn", "w_up", "conv_ffn_w", "w_down", "norm_final")
    wts = dict(zip(names, (norm_mix_even, w_in, conv_a, ln_a_g, ln_a_b, conv_b, w_out, norm_mix_odd, w_pool,
                           pool_scale, norm_ffn, w_up, conv_ffn_w, w_down, norm_final)))
    mom = dict(zip(names, (m_norm_mix_even, m_w_in, m_conv_a, m_ln_a_g, m_ln_a_b, m_conv_b, m_w_out, m_norm_mix_odd,
                           m_w_pool, m_pool_scale, m_norm_ffn, m_w_up, m_conv_ffn_w, m_w_down, m_norm_final)))
    var = dict(zip(names, (v_norm_mix_even, v_w_in, v_conv_a, v_ln_a_g, v_ln_a_b, v_conv_b, v_w_out, v_norm_mix_odd,
                           v_w_pool, v_pool_scale, v_norm_ffn, v_w_up, v_conv_ffn_w, v_w_down, v_norm_final)))
    d = x.shape[-1]

    def shard3d(t, param):
        a = t[param]
        if param in COLUMN_SHARDED:
            return jnp.swapaxes(a, 1, 2)
        return a.reshape(a.shape[0], -1, a.shape[-1])

    def unshard3d(a, param):
        if param in COLUMN_SHARDED:
            return jnp.swapaxes(a, 1, 2)
        return a.reshape(wts[param].shape)

    def shard2d(t, key):
        param, layer, _ = BIG[key]
        return shard3d(t, param)[layer]

    small_w = _pack([wts[k] for k in SMALL_SHARDED])
    bf = {k: shard2d(wts, k).astype(BF16) for k in BIG}
    gather_groups = {"mix0": ("w_in", "w_out", "small"), "ffn0": ("w_up0", "w_down0"),
                     "ffn1": ("w_pool", "w_up1", "w_down1")}
    order = list(gather_groups)
    started = {}

    def start_gather(grp, after):
        arrs = [small_w if k == "small" else bf[k] for k in gather_groups[grp]]
        started[grp] = _exchange_start(arrs, [False] * len(arrs), after, name=f"gather_{grp}_start", peers=CHIP_PEERS)

    start_gather(order[0], small_w)

    def weights(grp, after):
        keys = gather_groups[grp]
        lands = _exchange_wait(started[grp], [False] * len(keys), after, name=f"gather_{grp}_wait", peers=CHIP_PEERS)
        forwarded = _forward_start(lands, small_w, name=f"gather_{grp}_forward")
        if grp != order[-1]:
            start_gather(order[order.index(grp) + 1], forwarded[-1])
            behind = started[order[order.index(grp) + 1]][-1]
        else:
            behind = forwarded[-1]
        gw = dict(zip(keys, _forward_wait(forwarded, behind, name=f"gather_{grp}_forward_wait")))
        if grp == "ffn0":
            return {"w_up": gw["w_up0"], "w_down": gw["w_down0"].reshape(N_PAIR, -1, d)}
        if grp == "ffn1":
            return {"w_up": gw["w_up1"], "w_down": gw["w_down1"].reshape(N_PAIR, -1, d),
                    "w_pool": _from_dev_major(gw["w_pool"].reshape(N_DEV, len(POOL_WINDOWS), -1, POOL_GROUP), 1)}
        per_dev = gw["small"].reshape(N_DEV, -1)
        sizes = [wts[k].size for k in SMALL_SHARDED]
        offs = [sum(sizes[:i]) for i in range(len(sizes))]
        small_full = {k: per_dev[:, o:o + n_].reshape((N_DEV,) + wts[k].shape)
                      for k, o, n_ in zip(SMALL_SHARDED, offs, sizes)}
        return {
            "norm_mix_even": norm_mix_even, "ln_a_g": ln_a_g, "ln_a_b": ln_a_b, "norm_ffn": norm_ffn,
            "norm_final": norm_final[None],
            "w_in_t": gw["w_in"].reshape(-1, d),
            "w_out": gw["w_out"].reshape(-1, d),
            "conv_a": _from_dev_major(small_full["conv_a"][:, 0], 1),
            "conv_b": _from_dev_major(small_full["conv_b"][:, 0], 1),
            "norm_mix_odd": _from_dev_major(small_full["norm_mix_odd"], 1),
            "pool_scale": _from_dev_major(small_full["pool_scale"], 1),
            "conv_ffn": [small_full["conv_ffn_w"][:, l] for l in range(2)],
        }

    def dev_major(k, g):
        if k == "w_pool":
            return _to_dev_major(g, 1).reshape(N_DEV, -1, POOL_GROUP)
        return g.reshape(N_DEV, -1, g.shape[-1])

    sent = {}

    def grads_ready(grp, grads):
        keys = tuple(grads)
        parts = [dev_major(k, grads[k]) for k in keys]
        sent[grp] = (keys, _exchange_start(parts, [True] * len(keys), small_w, name=f"grads_{grp}_start"))
        return sent[grp][1][-1]

    loss, dx, g = _local_step(x[0], loss_target[0], weights, grads_ready)

    def small2d(a):
        return a.reshape(-1, a.shape[-1])

    small_keys = SMALL_SHARDED + SMALL_REPLICATED
    small_parts = [_to_dev_major(g["conv_a"], 1), _to_dev_major(g["conv_b"], 1),
                   _to_dev_major(g["norm_mix_odd"], 1), _to_dev_major(g["pool_scale"], 1),
                   jnp.stack(g["conv_ffn"], axis=1).reshape(N_DEV, -1, w_up.shape[-1])]
    small_parts += [g[k] for k in SMALL_REPLICATED]
    small_scatter = [True] * len(SMALL_SHARDED) + [False] * len(SMALL_REPLICATED)
    small_parts.append(jnp.full((1, 128), loss, F32))
    small_scatter.append(False)
    sent["small"] = (small_keys + ("loss",),
                     _exchange_start(small_parts, small_scatter, dx, name="grads_small_start"))

    landed, out, raw = {}, {}, {}

    def wait_grads(grp, after, scat=None):
        keys, st = sent[grp]
        scat = [True] * len(keys) if scat is None else scat
        landed.update(zip(keys, _exchange_wait(st, scat, after, name=f"grads_{grp}_wait")))

    def update(param, keys):
        raw[param] = _adamw([landed[k] for k in keys], shard3d(wts, param), shard3d(mom, param),
                            shard3d(var, param), name=f"adamw_{param}", tr=BIG[keys[0]][2])
        out[param] = [unshard3d(t, param) for t in raw[param]]

    wait_grads("ffn1", sent["small"][1][-1])
    wait_grads("ffn0", landed["w_up1"])
    update("w_up", ("w_up0", "w_up1"))
    update("w_down", ("w_down0", "w_down1"))
    wait_grads("mix0", [raw["w_up"][1], raw["w_down"][1]])
    update("w_in", ("w_in",))
    update("w_out", ("w_out",))
    update("w_pool", ("w_pool",))
    wait_grads("small", raw["w_pool"][1], small_scatter)
    res = _adamw_small([landed[k] for k in small_keys], [small2d(wts[k]) for k in small_keys],
                       [small2d(mom[k]) for k in small_keys], [small2d(var[k]) for k in small_keys],
                       name="adamw_small")
    for j, k in enumerate(small_keys):
        out[k] = [t.reshape(wts[k].shape) for t in res[4 * j:4 * j + 4]]

    loss = jnp.sum(landed["loss"][:, 0, 0])
    return (loss, dx[None], *[out[k][0] for k in names], *[out[k][1] for k in names],
            *[out[k][2] for k in names], *[out[k][3] for k in names])
```

```python
import jax
import jax.numpy as jnp
from jax import lax
from jax.experimental import pallas as pl
from jax.experimental.pallas import tpu as pltpu

F32 = jnp.float32
BF16 = jnp.bfloat16

RMS_EPS = 1e-6
LN_EPS = 1e-5
ADAM_LR = 0.001
ADAM_B1 = 0.9
ADAM_B2 = 0.999
ADAM_EPS = 1e-08
ADAM_WD = 0.01
ADAM_STEP = 10

N_DEV = 8
N_PAIR = N_DEV // 2
A_WIDTH = 512
A_TAPS = 31
POOL_WINDOWS = (2, 4, 8, 16)
POOL_GROUP = 256
HALO_A = 32
HALO_S = 16
VMEM_LIMIT = 56 * 1024 * 1024


def _params(sem, vmem=VMEM_LIMIT):
    return pltpu.CompilerParams(dimension_semantics=sem, vmem_limit_bytes=vmem)


def _sigmoid(x):
    return 0.5 * jnp.tanh(0.5 * x) + 0.5


def _prev_blk(i, ts, hb):
    return jnp.maximum(i * (ts // hb) - 1, 0)


def _next_blk(i, ts, hb, s):
    return jnp.minimum((i + 1) * (ts // hb), s // hb - 1)


def _mm(a, b, *, name, ta=False, tb=False, out_dtype=F32, tm=512, tn=512):
    m, k = (a.shape[1], a.shape[0]) if ta else a.shape
    n = b.shape[0] if tb else b.shape[1]
    tm, tn = min(tm, m), min(tn, n)
    assert m % tm == 0 and n % tn == 0, (name, m, n, tm, tn)
    dims = (((0,) if ta else (1,), (1,) if tb else (0,)), ((), ()))

    def body(a_ref, b_ref, o_ref):
        o_ref[...] = lax.dot_general(a_ref[...], b_ref[...], dims, preferred_element_type=F32).astype(out_dtype)

    a_spec = pl.BlockSpec((k, tm), lambda i, j: (0, i)) if ta else pl.BlockSpec((tm, k), lambda i, j: (i, 0))
    b_spec = pl.BlockSpec((tn, k), lambda i, j: (j, 0)) if tb else pl.BlockSpec((k, tn), lambda i, j: (0, j))
    return pl.pallas_call(
        body, name=name, grid=(m // tm, n // tn),
        in_specs=[a_spec, b_spec], out_specs=pl.BlockSpec((tm, tn), lambda i, j: (i, j)),
        out_shape=jax.ShapeDtypeStruct((m, n), out_dtype),
        compiler_params=_params(("parallel", "parallel")),
    )(a, b)


def _rms_rows(xv, gv):
    return xv * lax.rsqrt(jnp.mean(xv * xv, axis=-1, keepdims=True) + RMS_EPS) * gv


_NT = (((1,), (1,)), ((), ()))
_TN = (((0,), (0,)), ((), ()))


def _rms_mm(x, g, wt, *, name, out_dtype, tm=1024, tn=512):
    s, d = x.shape
    n = wt.shape[0]
    tm = min(tm, s)
    assert s % tm == 0 and n % tn == 0

    def body(x_ref, g_ref, w_ref, h_ref, z_ref, hs_ref):
        @pl.when(pl.program_id(1) == 0)
        def _():
            hv = _rms_rows(x_ref[...], g_ref[...]).astype(BF16)
            hs_ref[...] = hv
            h_ref[...] = hv

        z_ref[...] = lax.dot_general(hs_ref[...], w_ref[...], _NT, preferred_element_type=F32).astype(out_dtype)

    return pl.pallas_call(
        body, name=name, grid=(s // tm, n // tn),
        in_specs=[pl.BlockSpec((tm, d), lambda i, j: (i, 0)),
                  pl.BlockSpec((1, d), lambda i, j: (0, 0)),
                  pl.BlockSpec((tn, d), lambda i, j: (j, 0))],
        out_specs=[pl.BlockSpec((tm, d), lambda i, j: (i, 0)),
                   pl.BlockSpec((tm, tn), lambda i, j: (i, j))],
        out_shape=[jax.ShapeDtypeStruct((s, d), BF16), jax.ShapeDtypeStruct((s, n), out_dtype)],
        scratch_shapes=[pltpu.VMEM((tm, d), BF16)],
        compiler_params=_params(("parallel", "arbitrary")),
    )(x, g, wt)


def _rms_bwd_rows(xv, gv, dh):
    r = lax.rsqrt(jnp.mean(xv * xv, axis=-1, keepdims=True) + RMS_EPS)
    xh = xv * r
    dn = dh * gv
    dx = r * (dn - xh * jnp.mean(dn * xh, axis=-1, keepdims=True))
    return dx, dh * xh


def _mm_rms_bwd(a, b, x, g, dres, dep, *, name, tm=512):
    s, k = a.shape
    d = b.shape[1]
    tm = min(tm, s)

    def body(a_ref, b_ref, x_ref, g_ref, dres_ref, dep_ref, dx_ref, dg_ref):
        @pl.when(pl.program_id(0) == 0)
        def _():
            dg_ref[...] = jnp.zeros_like(dg_ref)

        dh = jnp.dot(a_ref[...], b_ref[...], preferred_element_type=F32)
        dx, dgr = _rms_bwd_rows(x_ref[...], g_ref[...], dh)
        dx_ref[...] = dres_ref[...] + dx
        dg_ref[...] += jnp.sum(dgr, axis=0, keepdims=True)

    row = pl.BlockSpec((tm, d), lambda i: (i, 0))
    vec = pl.BlockSpec((1, d), lambda i: (0, 0))
    return pl.pallas_call(
        body, name=name, grid=(s // tm,),
        in_specs=[pl.BlockSpec((tm, k), lambda i: (i, 0)), pl.BlockSpec((k, d), lambda i: (0, 0)), row, vec, row,
                  pl.BlockSpec(memory_space=pl.ANY)],
        out_specs=[row, vec],
        out_shape=[jax.ShapeDtypeStruct((s, d), F32), jax.ShapeDtypeStruct((1, d), F32)],
        compiler_params=_params(("arbitrary",)),
    )(a, b, x, g, dres, dep)


def _conv_taps(ext_ref, w_ref, n_taps, base, r0, rows, reverse=False):
    acc = None
    for k in range(n_taps):
        off = r0 + (base - k if reverse else base + k)
        term = w_ref[k:k + 1, :] * ext_ref[pl.ds(off, rows), :]
        acc = term if acc is None else acc + term
    return acc


def _shift_copies(src_ref, sh_ref, rows):
    for b in range(1, 8):
        sh_ref[b, 0:rows, :] = src_ref[pl.ds(b, rows), :]


def _shifted(src_ref, sh_ref, start, rows, off):
    a, b = divmod(off, 8)
    ref = src_ref if b == 0 else sh_ref.at[b]
    return ref[pl.ds(start + 8 * a, rows), :]


def _mix0_fwd(z, conv_a, ln_g, ln_b, conv_b, w_out, x, *, name, ts=512, rc=32):
    s = z.shape[0]
    c = A_WIDTH
    hb = HALO_A
    half = ts // 2

    def body(z_ref, zp_ref, wa_ref, lg_ref, lb_ref, wb_ref, wo_ref, x_ref, ab_ref, ca_ref, o_ref, exta, extb, sha):
        keep = jnp.where(pl.program_id(0) > 0, 1.0, 0.0)
        zp = zp_ref[...]
        exta[0:hb, :] = zp[:, 0:c] * _sigmoid(zp[:, c:2 * c]) * keep
        extb[0:hb, :] = zp[:, 3 * c:4 * c] * zp[:, 4 * c:5 * c] * keep
        exta[hb:hb + ts, :] = z_ref[:, 0:c] * _sigmoid(z_ref[:, c:2 * c])
        extb[hb:hb + ts, :] = z_ref[:, 3 * c:4 * c] * z_ref[:, 4 * c:5 * c]
        _shift_copies(exta, sha, hb + ts - 8)
        lg = lg_ref[...]
        lb = lb_ref[...]
        for q in range(ts // rc):
            r0 = q * rc
            ca = None
            for k in range(A_TAPS):
                term = wa_ref[k:k + 1, :] * _shifted(exta, sha, r0, rc, hb - (A_TAPS - 1) + k)
                ca = term if ca is None else ca + term
            ca_ref[r0:r0 + rc, :] = ca
            mu = jnp.mean(ca, axis=-1, keepdims=True)
            xc = ca - mu
            rs = lax.rsqrt(jnp.mean(xc * xc, axis=-1, keepdims=True) + LN_EPS)
            l = xc * rs * lg + lb
            ab_ref[r0:r0 + rc, 0:c] = (l * _sigmoid(l)).astype(BF16)
            cbc = _conv_taps(extb, wb_ref, 3, hb - 2, r0, rc)
            ab_ref[r0:r0 + rc, c:2 * c] = (z_ref[r0:r0 + rc, 2 * c:3 * c] * cbc).astype(BF16)
            if (r0 + rc) % half == 0:
                rows = slice(r0 + rc - half, r0 + rc)
                o_ref[rows, :] = x_ref[rows, :] + jnp.dot(ab_ref[rows, :], wo_ref[...], preferred_element_type=F32)

    d = x.shape[1]
    row = pl.BlockSpec((ts, d), lambda i: (i, 0))
    return pl.pallas_call(
        body, name=name, grid=(s // ts,),
        in_specs=[pl.BlockSpec((ts, 5 * c), lambda i: (i, 0)),
                  pl.BlockSpec((hb, 5 * c), lambda i: (_prev_blk(i, ts, hb), 0)),
                  pl.BlockSpec((32, c), lambda i: (0, 0)),
                  pl.BlockSpec((1, c), lambda i: (0, 0)),
                  pl.BlockSpec((1, c), lambda i: (0, 0)),
                  pl.BlockSpec((8, c), lambda i: (0, 0)),
                  pl.BlockSpec((2 * c, d), lambda i: (0, 0)), row],
        out_specs=[pl.BlockSpec((ts, 2 * c), lambda i: (i, 0)),
                   pl.BlockSpec((ts, c), lambda i: (i, 0)), row],
        out_shape=[jax.ShapeDtypeStruct((s, 2 * c), BF16), jax.ShapeDtypeStruct((s, c), F32),
                   jax.ShapeDtypeStruct((s, d), F32)],
        scratch_shapes=[pltpu.VMEM((hb + ts, c), F32), pltpu.VMEM((hb + ts, c), F32),
                        pltpu.VMEM((8, hb + ts - 8, c), F32)],
        compiler_params=_params(("parallel",)),
    )(z, z, conv_a, ln_g, ln_b, conv_b, w_out, x)


def _mix0_bwd(z, ca, dab, conv_a, ln_g, ln_b, conv_b, *, name, ts=512, rc=32):
    s = z.shape[0]
    c = A_WIDTH
    hb = HALO_A
    ta = A_TAPS

    def body(z_ref, zp_ref, zn_ref, ca_ref, can_ref, d_ref, dn_ref, wa_ref, lg_ref, lb_ref, wb_ref,
             dz_ref, dwa_ref, dwb_ref, dlg_ref, dlb_ref, exta, extb, extdca, extdcb, shd):
        i = pl.program_id(0)
        keep_p = jnp.where(i > 0, 1.0, 0.0)
        keep_n = jnp.where(i < s // ts - 1, 1.0, 0.0)

        @pl.when(i == 0)
        def _():
            dwa_ref[...] = jnp.zeros_like(dwa_ref)
            dwb_ref[...] = jnp.zeros_like(dwb_ref)
            dlg_ref[...] = jnp.zeros_like(dlg_ref)
            dlb_ref[...] = jnp.zeros_like(dlb_ref)

        lg = lg_ref[...]
        lb = lb_ref[...]
        zp = zp_ref[...]
        exta[0:hb, :] = zp[:, 0:c] * _sigmoid(zp[:, c:2 * c]) * keep_p
        extb[0:hb, :] = zp[:, 3 * c:4 * c] * zp[:, 4 * c:5 * c] * keep_p
        exta[hb:hb + ts, :] = z_ref[:, 0:c] * _sigmoid(z_ref[:, c:2 * c])
        extb[hb:hb + ts, :] = z_ref[:, 3 * c:4 * c] * z_ref[:, 4 * c:5 * c]

        def ln_bwd(cav, dav):
            mu = jnp.mean(cav, axis=-1, keepdims=True)
            xc = cav - mu
            rs = lax.rsqrt(jnp.mean(xc * xc, axis=-1, keepdims=True) + LN_EPS)
            nv = xc * rs
            l = nv * lg + lb
            sg = _sigmoid(l)
            dl = dav * (sg * (1.0 + l * (1.0 - sg)))
            dnv = dl * lg
            dca = rs * (dnv - jnp.mean(dnv, axis=-1, keepdims=True)
                        - nv * jnp.mean(dnv * nv, axis=-1, keepdims=True))
            return dca, dl, nv

        dlg_acc = jnp.zeros((1, c), F32)
        dlb_acc = jnp.zeros((1, c), F32)
        for q in range(ts // rc):
            r0 = q * rc
            dca, dl, nv = ln_bwd(ca_ref[r0:r0 + rc, :], d_ref[r0:r0 + rc, 0:c])
            extdca[r0:r0 + rc, :] = dca
            dlg_acc = dlg_acc + jnp.sum(dl * nv, axis=0, keepdims=True)
            dlb_acc = dlb_acc + jnp.sum(dl, axis=0, keepdims=True)
            extdcb[r0:r0 + rc, :] = d_ref[r0:r0 + rc, c:2 * c] * z_ref[r0:r0 + rc, 2 * c:3 * c]
        dca_n, _, _ = ln_bwd(can_ref[...], dn_ref[:, 0:c])
        extdca[ts:ts + hb, :] = dca_n * keep_n
        extdcb[ts:ts + hb, :] = dn_ref[:, c:2 * c] * zn_ref[:, 2 * c:3 * c] * keep_n
        dlg_ref[...] += dlg_acc
        dlb_ref[...] += dlb_acc
        _shift_copies(extdca, shd, ts + hb - 8)

        for q in range(ts // rc):
            r0 = q * rc
            zr = z_ref[r0:r0 + rc, :]
            dga = None
            for k in range(ta):
                term = wa_ref[k:k + 1, :] * _shifted(extdca, shd, r0, rc, ta - 1 - k)
                dga = term if dga is None else dga + term
            sg = _sigmoid(zr[:, c:2 * c])
            dz_ref[r0:r0 + rc, 0:c] = (dga * sg).astype(BF16)
            dz_ref[r0:r0 + rc, c:2 * c] = (dga * zr[:, 0:c] * sg * (1.0 - sg)).astype(BF16)
            cbc = _conv_taps(extb, wb_ref, 3, hb - 2, r0, rc)
            dz_ref[r0:r0 + rc, 2 * c:3 * c] = (d_ref[r0:r0 + rc, c:2 * c] * cbc).astype(BF16)
            dcb = _conv_taps(extdcb, wb_ref, 3, 2, r0, rc, reverse=True)
            dz_ref[r0:r0 + rc, 3 * c:4 * c] = (dcb * zr[:, 4 * c:5 * c]).astype(BF16)
            dz_ref[r0:r0 + rc, 4 * c:5 * c] = (dcb * zr[:, 3 * c:4 * c]).astype(BF16)

        for k in range(ta):
            part = None
            for q in range(ts // rc):
                r0 = q * rc
                p = exta[hb + r0:hb + r0 + rc, :] * _shifted(extdca, shd, r0, rc, ta - 1 - k)
                for r in range(0, rc, 8):
                    part = p[r:r + 8, :] if part is None else part + p[r:r + 8, :]
            dwa_ref[k:k + 1, :] += jnp.sum(part, axis=0, keepdims=True)
        dcb_t = extdcb[0:ts, :]
        for k in range(3):
            dwb_ref[k:k + 1, :] += jnp.sum(dcb_t * extb[pl.ds(hb - 2 + k, ts), :], axis=0, keepdims=True)

    def tile(w):
        return pl.BlockSpec((ts, w), lambda i: (i, 0))

    def prev(w):
        return pl.BlockSpec((hb, w), lambda i: (_prev_blk(i, ts, hb), 0))

    def nxt(w):
        return pl.BlockSpec((hb, w), lambda i: (_next_blk(i, ts, hb, s), 0))

    def const(r, w):
        return pl.BlockSpec((r, w), lambda i: (0, 0))

    return pl.pallas_call(
        body, name=name, grid=(s // ts,),
        in_specs=[tile(5 * c), prev(5 * c), nxt(5 * c), tile(c), nxt(c), tile(2 * c), nxt(2 * c),
                  const(32, c), const(1, c), const(1, c), const(8, c)],
        out_specs=[tile(5 * c), const(32, c), const(8, c), const(1, c), const(1, c)],
        out_shape=[jax.ShapeDtypeStruct((s, 5 * c), BF16), jax.ShapeDtypeStruct((32, c), F32),
                   jax.ShapeDtypeStruct((8, c), F32), jax.ShapeDtypeStruct((1, c), F32),
                   jax.ShapeDtypeStruct((1, c), F32)],
        scratch_shapes=[pltpu.VMEM((hb + ts, c), F32), pltpu.VMEM((hb + ts, c), F32),
                        pltpu.VMEM((ts + hb, c), F32), pltpu.VMEM((ts + hb, c), F32),
                        pltpu.VMEM((8, ts + hb - 8, c), F32)],
        compiler_params=_params(("arbitrary",)),
    )(z, z, z, ca, ca, dab, dab, conv_a, ln_g, ln_b, conv_b)


def _ffn_up(x, g, w8, *, name, tm=1024):
    s, d = x.shape
    nb, c, _ = w8.shape
    tm = min(tm, s)
    grp = N_PAIR

    def body(x_ref, g_ref, w_ref, h_ref, u_ref, hs_ref):
        @pl.when(pl.program_id(1) == 0)
        def _():
            hv = _rms_rows(x_ref[...], g_ref[...]).astype(BF16)
            hs_ref[...] = hv
            h_ref[...] = hv

        res = lax.dot_general(hs_ref[...], w_ref[...].reshape(grp * c, d), _NT, preferred_element_type=F32)
        for b in range(grp):
            u_ref[b] = res[:, b * c:(b + 1) * c].astype(BF16)

    return pl.pallas_call(
        body, name=name, grid=(s // tm, nb // grp),
        in_specs=[pl.BlockSpec((tm, d), lambda i, k: (i, 0)),
                  pl.BlockSpec((1, d), lambda i, k: (0, 0)),
                  pl.BlockSpec((grp, c, d), lambda i, k: (k, 0, 0))],
        out_specs=[pl.BlockSpec((tm, d), lambda i, k: (i, 0)),
                   pl.BlockSpec((grp, tm, c), lambda i, k: (k, i, 0))],
        out_shape=[jax.ShapeDtypeStruct((s, d), BF16), jax.ShapeDtypeStruct((nb, s, c), BF16)],
        scratch_shapes=[pltpu.VMEM((tm, d), BF16)],
        compiler_params=_params(("parallel", "arbitrary")),
    )(x, g, w8)


def _ffn_mid_down(up8, wc8, wd4, x, head=None, *, name, ts=256, rc=256):
    nb, s, c = up8.shape
    d = x.shape[1]
    hb = HALO_S
    ts = min(ts, s)

    def body(*refs):
        u_ref, up_ref, wc_ref, wd_ref, x_ref = refs[:5]
        n_out = 4 if head is None else 7
        outs = refs[-1 - n_out:-1]
        act_ref, ug_ref, uv_ref = outs[-3:]
        ext = refs[-1]
        keep = jnp.where(pl.program_id(0) > 0, 1.0, 0.0)
        acc = x_ref[...]
        for j in range(N_PAIR):
            eg, ev = ext.at[2 * (j % 2)], ext.at[2 * (j % 2) + 1]
            eg[0:hb, :] = up_ref[j].astype(F32) * keep
            ev[0:hb, :] = up_ref[j + N_PAIR].astype(F32) * keep
            eg[hb:hb + ts, :] = u_ref[j].astype(F32)
            ev[hb:hb + ts, :] = u_ref[j + N_PAIR].astype(F32)
            for q in range(ts // rc):
                r0 = q * rc
                gg = _conv_taps(eg, wc_ref.at[j], 3, hb - 2, r0, rc)
                vv = _conv_taps(ev, wc_ref.at[j + N_PAIR], 3, hb - 2, r0, rc)
                ug_ref[j, r0:r0 + rc, :] = gg.astype(BF16)
                uv_ref[j, r0:r0 + rc, :] = vv.astype(BF16)
                act_ref[j, r0:r0 + rc, :] = (gg * _sigmoid(gg) * vv).astype(BF16)
            acc = acc + jnp.dot(act_ref[j], wd_ref[j], preferred_element_type=F32)
        if head is None:
            outs[0][...] = acc
            return
        g_ref, t_ref = refs[5], refs[6]
        loss_ref, dx_ref, dxb_ref, dg_ref = outs[:4]

        @pl.when(pl.program_id(0) == 0)
        def _():
            loss_ref[...] = jnp.zeros_like(loss_ref)
            dg_ref[...] = jnp.zeros_like(dg_ref)

        gv = g_ref[...]
        r = lax.rsqrt(jnp.mean(acc * acc, axis=-1, keepdims=True) + RMS_EPS)
        xh = acc * r
        err = xh * gv - t_ref[...]
        loss_ref[...] += 0.5 * jnp.sum(jnp.mean(err * err, axis=-1, keepdims=True), axis=0, keepdims=True)
        dy = err * (1.0 / d)
        dn = dy * gv
        dx = r * (dn - xh * jnp.mean(dn * xh, axis=-1, keepdims=True))
        dx_ref[...] = dx
        dxb_ref[...] = dx.astype(BF16)
        dg_ref[...] += jnp.sum(dy * xh, axis=0, keepdims=True)

    row = pl.BlockSpec((ts, d), lambda i: (i, 0))
    vec = pl.BlockSpec((1, d), lambda i: (0, 0))
    tile = pl.BlockSpec((N_PAIR, ts, c), lambda i: (0, i, 0))
    half = jax.ShapeDtypeStruct((N_PAIR, s, c), BF16)
    in_specs = [pl.BlockSpec((nb, ts, c), lambda i: (0, i, 0)),
                pl.BlockSpec((nb, hb, c), lambda i: (0, _prev_blk(i, ts, hb), 0)),
                pl.BlockSpec((nb, 8, c), lambda i: (0, 0, 0)),
                pl.BlockSpec((N_PAIR, c, d), lambda i: (0, 0, 0)), row]
    args = [up8, up8, wc8, wd4, x]
    if head is None:
        out_specs = [row, tile, tile, tile]
        out_shape = [jax.ShapeDtypeStruct((s, d), F32), half, half, half]
    else:
        in_specs += [vec, row]
        args += list(head)
        out_specs = [pl.BlockSpec((1, 1), lambda i: (0, 0)), row, row, vec, tile, tile, tile]
        out_shape = [jax.ShapeDtypeStruct((1, 1), F32), jax.ShapeDtypeStruct((s, d), F32),
                     jax.ShapeDtypeStruct((s, d), BF16), jax.ShapeDtypeStruct((1, d), F32), half, half, half]
    return pl.pallas_call(
        body, name=name, grid=(s // ts,), in_specs=in_specs, out_specs=out_specs, out_shape=out_shape,
        scratch_shapes=[pltpu.VMEM((4, hb + ts, c), F32)],
        compiler_params=_params(("parallel",) if head is None else ("arbitrary",)),
    )(*args)


def _ffn_dwdown(act4, db, *, name):
    npair, s, c = act4.shape
    d = db.shape[1]

    def body(a_ref, d_ref, o_ref):
        o_ref[...] = lax.dot_general(a_ref[...], d_ref[...], (((0,), (0,)), ((), ())),
                                     preferred_element_type=F32).astype(BF16)

    return pl.pallas_call(
        body, name=name, grid=(npair,),
        in_specs=[pl.BlockSpec((None, s, c), lambda j: (j, 0, 0)),
                  pl.BlockSpec((s, d), lambda j: (0, 0))],
        out_specs=pl.BlockSpec((None, c, d), lambda j: (j, 0, 0)),
        out_shape=jax.ShapeDtypeStruct((npair, c, d), BF16),
        compiler_params=_params(("parallel",)),
    )(act4, db)


def _ffn_dact(db, wd4, *, name, tm=1024):
    s, d = db.shape
    npair, c, _ = wd4.shape
    tm = min(tm, s)

    def body(d_ref, w_ref, o_ref):
        res = lax.dot_general(d_ref[...], w_ref[...].reshape(npair * c, d), _NT, preferred_element_type=F32)
        for j in range(npair):
            o_ref[j] = res[:, j * c:(j + 1) * c].astype(BF16)

    return pl.pallas_call(
        body, name=name, grid=(s // tm,),
        in_specs=[pl.BlockSpec((tm, d), lambda i: (i, 0)),
                  pl.BlockSpec((npair, c, d), lambda i: (0, 0, 0))],
        out_specs=pl.BlockSpec((npair, tm, c), lambda i: (0, i, 0)),
        out_shape=jax.ShapeDtypeStruct((npair, s, c), BF16),
        compiler_params=_params(("parallel",)),
    )(db, wd4)


def _ffn_midbwd(up8, ug4, uv4, dact4, wc8, *, name, ts=1024, rc=32):
    nb, s, c = up8.shape
    hb = HALO_S
    ts = min(ts, s)
    n_i = s // ts

    def body(pg_ref, pv_ref, ug_ref, ugn_ref, uv_ref, uvn_ref, d_ref, dn_ref, wg_ref, wv_ref,
             dg_ref, dv_ref, dwg_ref, dwv_ref, extdg, extdv):
        i = pl.program_id(1)
        keep_n = jnp.where(i < n_i - 1, 1.0, 0.0)

        @pl.when(i == 0)
        def _():
            dwg_ref[...] = jnp.zeros_like(dwg_ref)
            dwv_ref[...] = jnp.zeros_like(dwv_ref)

        def du_rows(r0, rows, gg, vv, dav):
            sg = _sigmoid(gg)
            extdg[r0:r0 + rows, :] = dav * vv * (sg * (1.0 + gg * (1.0 - sg)))
            extdv[r0:r0 + rows, :] = dav * (gg * sg)

        for q in range(ts // rc):
            rows = slice(q * rc, q * rc + rc)
            du_rows(q * rc, rc, ug_ref[rows, :].astype(F32), uv_ref[rows, :].astype(F32),
                    d_ref[rows, :].astype(F32))
        du_rows(ts, 8, ugn_ref[0:8, :].astype(F32), uvn_ref[0:8, :].astype(F32),
                dn_ref[0:8, :].astype(F32) * keep_n)

        def fold(p):
            acc = p[0:8, :]
            for r in range(8, rc, 8):
                acc = acc + p[r:r + 8, :]
            return acc

        for extd, p_ref, w_ref, out_ref, dw_ref in ((extdg, pg_ref, wg_ref, dg_ref, dwg_ref),
                                                    (extdv, pv_ref, wv_ref, dv_ref, dwv_ref)):
            part = [None, None, None]
            for q in range(ts // rc):
                r0 = q * rc
                pre = p_ref[r0:r0 + rc, :].astype(F32)
                dup = None
                for k in range(3):
                    sh = extd[pl.ds(r0 + 2 - k, rc), :]
                    term = w_ref[k:k + 1, :] * sh
                    dup = term if dup is None else dup + term
                    prod = fold(pre * sh)
                    part[k] = prod if part[k] is None else part[k] + prod
                out_ref[r0:r0 + rc, :] = dup.astype(BF16)
            for k in range(3):
                dw_ref[k:k + 1, :] += jnp.sum(part[k], axis=0, keepdims=True)

    def blk(off, nxt):
        if nxt:
            return pl.BlockSpec((None, hb, c), lambda j, i: (j + off, _next_blk(i, ts, hb, s), 0))
        return pl.BlockSpec((None, ts, c), lambda j, i: (j + off, i, 0))

    def taps(off):
        return pl.BlockSpec((None, 8, c), lambda j, i: (j + off, 0, 0))

    tile = blk(0, False)
    acc = pl.BlockSpec((None, 8, c), lambda j, i: (j, 0, 0))
    return pl.pallas_call(
        body, name=name, grid=(N_PAIR, n_i),
        in_specs=[tile, blk(N_PAIR, False), tile, blk(0, True), tile, blk(0, True), tile, blk(0, True),
                  taps(0), taps(N_PAIR)],
        out_specs=[tile, tile, acc, acc],
        out_shape=[jax.ShapeDtypeStruct((N_PAIR, s, c), BF16), jax.ShapeDtypeStruct((N_PAIR, s, c), BF16),
                   jax.ShapeDtypeStruct((N_PAIR, 8, c), F32), jax.ShapeDtypeStruct((N_PAIR, 8, c), F32)],
        scratch_shapes=[pltpu.VMEM((ts + 8, c), F32), pltpu.VMEM((ts + 8, c), F32)],
        compiler_params=_params(("parallel", "arbitrary")),
    )(up8, up8, ug4, ug4, uv4, uv4, dact4, dact4, wc8, wc8)


def _ffn_dh(dupg, dupv, w8, x, g, dres, dep, *, name, tm=512):
    npair, s, c = dupg.shape
    d = x.shape[1]
    tm = min(tm, s)

    def body(dg_ref, dv_ref, w_ref, x_ref, g_ref, dres_ref, dep_ref, dx_ref, dxb_ref, dgain_ref):
        @pl.when(pl.program_id(0) == 0)
        def _():
            dgain_ref[...] = jnp.zeros_like(dgain_ref)

        dh = None
        for half, src in enumerate((dg_ref, dv_ref)):
            lhs = jnp.concatenate([src[j] for j in range(npair)], axis=1)
            rhs = w_ref[half * npair:(half + 1) * npair].reshape(npair * c, d)
            part = jnp.dot(lhs, rhs, preferred_element_type=F32)
            dh = part if dh is None else dh + part
        dx, dgr = _rms_bwd_rows(x_ref[...], g_ref[...], dh)
        dx = dres_ref[...] + dx
        dx_ref[...] = dx
        dxb_ref[...] = dx.astype(BF16)
        dgain_ref[...] += jnp.sum(dgr, axis=0, keepdims=True)

    row = pl.BlockSpec((tm, d), lambda i: (i, 0))
    vec = pl.BlockSpec((1, d), lambda i: (0, 0))
    dup = pl.BlockSpec((npair, tm, c), lambda i: (0, i, 0))
    return pl.pallas_call(
        body, name=name, grid=(s // tm,),
        in_specs=[dup, dup, pl.BlockSpec((2 * npair, c, d), lambda i: (0, 0, 0)), row, vec, row,
                  pl.BlockSpec(memory_space=pl.ANY)],
        out_specs=[row, row, vec],
        out_shape=[jax.ShapeDtypeStruct((s, d), F32), jax.ShapeDtypeStruct((s, d), BF16),
                   jax.ShapeDtypeStruct((1, d), F32)],
        compiler_params=_params(("arbitrary",)),
    )(dupg, dupv, w8, x, g, dres, dep)


def _ffn_dwup(h, dupg, dupv, *, name, tm=1024):
    npair, s, c = dupg.shape
    d = h.shape[1]

    def body(h_ref, dg_ref, dv_ref, o_ref):
        k = pl.program_id(1)

        @pl.when(k < npair)
        def _():
            o_ref[...] = lax.dot_general(dg_ref[...], h_ref[...], _TN, preferred_element_type=F32).astype(BF16)

        @pl.when(k >= npair)
        def _():
            o_ref[...] = lax.dot_general(dv_ref[...], h_ref[...], _TN, preferred_element_type=F32).astype(BF16)

    return pl.pallas_call(
        body, name=name, grid=(d // tm, 2 * npair),
        in_specs=[pl.BlockSpec((s, tm), lambda m, k: (0, m)),
                  pl.BlockSpec((None, s, c), lambda m, k: (jnp.minimum(k, npair - 1), 0, 0)),
                  pl.BlockSpec((None, s, c), lambda m, k: (jnp.maximum(k - npair, 0), 0, 0))],
        out_specs=pl.BlockSpec((None, c, tm), lambda m, k: (k, 0, m)),
        out_shape=jax.ShapeDtypeStruct((2 * npair, c, d), BF16),
        compiler_params=_params(("parallel", "arbitrary")),
    )(h, dupg, dupv)


def _pool_counts(i, ts, rows, window):
    t = lax.broadcasted_iota(jnp.int32, (rows, 1), 0) + i * ts + 1
    return jnp.minimum(t, window).astype(F32)


def _pool_fwd(x, g, w_pool, scale, *, name, ts=512):
    s, d = x.shape
    hb = HALO_A
    pg = POOL_GROUP
    ts = min(ts, s)

    def body(x_ref, xp_ref, g_ref, w_ref, s_ref, o_ref, p_ref, yu_ref, ext, sa, sb):
        i = pl.program_id(0)
        keep = jnp.where(i > 0, 1.0, 0.0)
        gv = g_ref[...]
        ext[0:hb, :] = _rms_rows(xp_ref[...], gv) * keep
        ext[hb:hb + ts, :] = _rms_rows(x_ref[...], gv)
        rows = hb + ts - 8
        sa[0:8, :] = jnp.zeros((8, d), F32)
        sb[0:8, :] = jnp.zeros((8, d), F32)
        for gi, w in enumerate(POOL_WINDOWS):
            cols = slice(gi * pg, (gi + 1) * pg)
            cur, nxt, k = ext, sa, 1
            while k < w:
                nxt[8:8 + rows, cols] = cur[8:8 + rows, cols] + cur[pl.ds(8 - k, rows), cols]
                cur, nxt, k = nxt, (sb if nxt is sa else sa), 2 * k
            h = ext[hb:hb + ts, cols]
            pv = (cur[hb:hb + ts, cols] / _pool_counts(i, ts, ts, w) - h).astype(BF16)
            p_ref[:, cols] = pv
            yu = jnp.dot(pv, w_ref[gi], preferred_element_type=F32)
            yu_ref[:, cols] = yu.astype(BF16)
            o_ref[:, cols] = x_ref[:, cols] + yu * s_ref[:, cols]

    row = pl.BlockSpec((ts, d), lambda i: (i, 0))
    vec = pl.BlockSpec((1, d), lambda i: (0, 0))
    return pl.pallas_call(
        body, name=name, grid=(s // ts,),
        in_specs=[row, pl.BlockSpec((hb, d), lambda i: (_prev_blk(i, ts, hb), 0)), vec,
                  pl.BlockSpec((d // pg, pg, pg), lambda i: (0, 0, 0)), vec],
        out_specs=[row, row, row],
        out_shape=[jax.ShapeDtypeStruct((s, d), F32), jax.ShapeDtypeStruct((s, d), BF16),
                   jax.ShapeDtypeStruct((s, d), BF16)],
        scratch_shapes=[pltpu.VMEM((hb + ts, d), F32)] * 3,
        compiler_params=_params(("parallel",)),
    )(x, x, g, w_pool, scale)


def _pool_dw(p, dyc, *, name):
    s, d = p.shape
    pg = POOL_GROUP

    def body(p_ref, d_ref, o_ref):
        o_ref[...] = lax.dot_general(p_ref[...], d_ref[...], (((0,), (0,)), ((), ())),
                                     preferred_element_type=F32).astype(BF16)

    blk = pl.BlockSpec((s, pg), lambda gi: (0, gi))
    return pl.pallas_call(
        body, name=name, grid=(d // pg,),
        in_specs=[blk, blk], out_specs=pl.BlockSpec((None, pg, pg), lambda gi: (gi, 0, 0)),
        out_shape=jax.ShapeDtypeStruct((d // pg, pg, pg), BF16),
        compiler_params=_params(("parallel",)),
    )(p, dyc)


def _pool_bwd(dres, w_pool, scale, yu, x, g, *, name, ts=512):
    s, d = x.shape
    hb = HALO_A
    pg = POOL_GROUP
    ts = min(ts, s)
    n_i = s // ts

    def body(dres_ref, dresn_ref, w_ref, s_ref, yu_ref, x_ref, g_ref,
             dx_ref, dxb_ref, dg_ref, dyc_ref, ds_ref, ext, dh, sa, sb, dp_s):
        i = pl.program_id(0)
        keep_n = jnp.where(i < n_i - 1, 1.0, 0.0)

        @pl.when(i == 0)
        def _():
            dg_ref[...] = jnp.zeros_like(dg_ref)
            ds_ref[...] = jnp.zeros_like(ds_ref)

        ds_ref[...] += jnp.sum(dres_ref[...] * yu_ref[...].astype(F32), axis=0, keepdims=True)
        for gi, w in enumerate(POOL_WINDOWS):
            cols = slice(gi * pg, (gi + 1) * pg)
            dyc = (dres_ref[:, cols] * s_ref[:, cols]).astype(BF16)
            dyc_ref[:, cols] = dyc
            dp = lax.dot_general(dyc, w_ref[gi], _NT, preferred_element_type=F32)
            dp_s[:, cols] = dp
            ext[0:ts, cols] = dp / _pool_counts(i, ts, ts, w)
            dycn = (dresn_ref[:, cols] * s_ref[:, cols]).astype(BF16)
            dpn = lax.dot_general(dycn, w_ref[gi], _NT, preferred_element_type=F32)
            ext[ts:ts + hb, cols] = dpn / _pool_counts(i + 1, ts, hb, w) * keep_n
        rows = ts + hb - 8
        sa[rows:rows + 8, :] = jnp.zeros((8, d), F32)
        sb[rows:rows + 8, :] = jnp.zeros((8, d), F32)
        for gi, w in enumerate(POOL_WINDOWS):
            cols = slice(gi * pg, (gi + 1) * pg)
            cur, nxt, k = ext, sa, 1
            while k < w:
                nxt[0:rows, cols] = cur[0:rows, cols] + cur[pl.ds(k, rows), cols]
                cur, nxt, k = nxt, (sb if nxt is sa else sa), 2 * k
            dh[:, cols] = cur[0:ts, cols] - dp_s[:, cols]
        dx, dgr = _rms_bwd_rows(x_ref[...], g_ref[...], dh[...])
        dx = dres_ref[...] + dx
        dx_ref[...] = dx
        dxb_ref[...] = dx.astype(BF16)
        dg_ref[...] += jnp.sum(dgr, axis=0, keepdims=True)

    row = pl.BlockSpec((ts, d), lambda i: (i, 0))
    vec = pl.BlockSpec((1, d), lambda i: (0, 0))
    return pl.pallas_call(
        body, name=name, grid=(n_i,),
        in_specs=[row, pl.BlockSpec((hb, d), lambda i: (_next_blk(i, ts, hb, s), 0)),
                  pl.BlockSpec((d // pg, pg, pg), lambda i: (0, 0, 0)), vec, row, row, vec],
        out_specs=[row, row, vec, row, vec],
        out_shape=[jax.ShapeDtypeStruct((s, d), F32), jax.ShapeDtypeStruct((s, d), BF16),
                   jax.ShapeDtypeStruct((1, d), F32), jax.ShapeDtypeStruct((s, d), BF16),
                   jax.ShapeDtypeStruct((1, d), F32)],
        scratch_shapes=[pltpu.VMEM((ts + hb, d), F32), pltpu.VMEM((ts, d), F32),
                        pltpu.VMEM((ts + hb, d), F32), pltpu.VMEM((ts + hb, d), F32),
                        pltpu.VMEM((ts, d), F32)],
        compiler_params=_params(("arbitrary",)),
    )(dres, dres, w_pool, scale, yu, x, g)


def _pad_rows(w, rows):
    pad = [(0, 0)] * (w.ndim - 2) + [(0, rows - w.shape[-2]), (0, 0)]
    return jnp.pad(w, pad)


def _ffn_layer_fwd(x, nf, w8, wc8, wd4, tag, head=None):
    h, up8 = _ffn_up(x, nf, w8, name=f"ffn{tag}_up")
    res = _ffn_mid_down(up8, wc8, wd4, x, head, name=f"ffn{tag}_mid_down")
    act4, ug4, uv4 = res[-3:]
    out = res[0] if head is None else tuple(res[:4])
    return out, (h, up8, ug4, uv4, act4)


def _ffn_layer_bwd(d, db, x, nf, w8, wc8, wd4, saved, tag, grads_ready):
    h, up8, ug4, uv4, act4 = saved
    dact4 = _ffn_dact(db, wd4, name=f"ffn{tag}_dact")
    dwd4 = _ffn_dwdown(act4, db, name=f"ffn{tag}_dwdown")
    dupg, dupv, dwg, dwv = _ffn_midbwd(up8, ug4, uv4, dact4, wc8, name=f"ffn{tag}_midbwd")
    dw8 = _ffn_dwup(h, dupg, dupv, name=f"ffn{tag}_dwup")
    sent = grads_ready(f"ffn{tag}", {f"w_up{tag}": dw8, f"w_down{tag}": dwd4})
    dx, dxb, dnf = _ffn_dh(dupg, dupv, w8, x, nf, d, sent, name=f"ffn{tag}_dh")
    dwc8 = jnp.concatenate([dwg, dwv], axis=0)[:, :3]
    return dx, dxb, dnf, dwc8


def _local_step(x, target, weights, grads_ready):
    w0 = weights("mix0", x)
    wa = _pad_rows(w0["conv_a"], 32)
    wb = _pad_rows(w0["conv_b"], 8)
    wc = [_pad_rows(w0["conv_ffn"][l], 8) for l in range(2)]
    h0, z = _rms_mm(x, w0["norm_mix_even"], w0["w_in_t"], name="mix0_in", out_dtype=F32, tn=2560)
    ab, ca, x1 = _mix0_fwd(z, wa, w0["ln_a_g"], w0["ln_a_b"], wb, w0["w_out"], x, name="mix0_mid")
    w1 = weights("ffn0", x1)
    x2, ffn0 = _ffn_layer_fwd(x1, w0["norm_ffn"][0:1], w1["w_up"], wc[0], w1["w_down"], 0)
    w2 = weights("ffn1", x2)
    x3, p, yu = _pool_fwd(x2, w0["norm_mix_odd"], w2["w_pool"], w0["pool_scale"], name="pool_fwd")
    (loss, d4, d4b, g_norm_final), ffn1 = _ffn_layer_fwd(
        x3, w0["norm_ffn"][1:2], w2["w_up"], wc[1], w2["w_down"], 1, head=(w0["norm_final"], target))

    d3, d3b, g_nf1, g_wc1 = _ffn_layer_bwd(
        d4, d4b, x3, w0["norm_ffn"][1:2], w2["w_up"], wc[1], w2["w_down"], ffn1, 1, grads_ready)
    d2, d2b, g_nmo, dyc, g_scale = _pool_bwd(d3, w2["w_pool"], w0["pool_scale"], yu, x2, w0["norm_mix_odd"],
                                             name="pool_bwd")
    g_pool = _pool_dw(p, dyc, name="pool_dw")
    d1, d1b, g_nf0, g_wc0 = _ffn_layer_bwd(
        d2, d2b, x1, w0["norm_ffn"][0:1], w1["w_up"], wc[0], w1["w_down"], ffn0, 0, grads_ready)
    dab = _mm(d1b, w0["w_out"], tb=True, name="mix0_dab", tm=1024, tn=1024)
    g_out = _mm(ab, d1b, ta=True, out_dtype=BF16, name="mix0_dwout", tm=1024, tn=512)
    dz, g_wa, g_wb, g_lg, g_lb = _mix0_bwd(z, ca, dab, wa, w0["ln_a_g"], w0["ln_a_b"], wb, name="mix0_midbwd")
    g_in = _mm(dz, h0, ta=True, out_dtype=BF16, name="mix0_dwin", tm=512, tn=1024)
    sent_mix = grads_ready("mix0", {"w_in": g_in, "w_out": g_out, "w_pool": g_pool})
    dx, g_nme = _mm_rms_bwd(dz, w0["w_in_t"], x, w0["norm_mix_even"], d1, sent_mix, name="mix0_dh")

    small = {
        "norm_mix_even": g_nme, "conv_a": g_wa[:A_TAPS], "ln_a_g": g_lg, "ln_a_b": g_lb, "conv_b": g_wb[:3],
        "norm_mix_odd": g_nmo, "pool_scale": g_scale, "norm_ffn": jnp.concatenate([g_nf0, g_nf1], axis=0),
        "conv_ffn": [g_wc0, g_wc1], "norm_final": g_norm_final,
    }
    return loss[0, 0], dx, small


def _my_pos():
    return lax.axis_index("x"), lax.axis_index("y"), lax.axis_index("c")


def _flip(pos, r):
    x, y, c = pos
    return (1 - x if r & 4 else x, 1 - y if r & 2 else y, 1 - c if r & 1 else c)


def _dev_index(pos):
    return 4 * pos[0] + 2 * pos[1] + pos[2]


_HBM = pl.BlockSpec(memory_space=pltpu.HBM)
_SEM = pl.BlockSpec(memory_space=pltpu.SEMAPHORE)
_EFFECT = pltpu.SideEffectType.DATAFLOW_SIDE_EFFECTING


def _exchange_copy(ins, lands, send_sems, recv_sems, scatter, pos, a, r, receiving):
    me = _dev_index(pos)
    peer = _flip(pos, r)
    dest = _dev_index(pos) if receiving else _dev_index(peer)
    src = ins[a].at[dest] if scatter[a] else ins[a]
    slot = _dev_index(peer) if receiving else me
    return pltpu.make_async_remote_copy(
        src_ref=src, dst_ref=lands[a].at[slot], send_sem=send_sems.at[a * (N_DEV - 1) + r - 1],
        recv_sem=recv_sems.at[a * (N_DEV - 1) + r - 1],
        device_id=peer, device_id_type=pl.DeviceIdType.MESH)


ALL_PEERS = tuple(range(1, N_DEV))
CHIP_PEERS = (1, 2, 4, 6)
FORWARDED = (2, 4, 6)


def _exchange_start(arrays, scatter, after, *, name, peers=ALL_PEERS):
    n = len(arrays)
    me = _dev_index(_my_pos())
    lands = []
    for arr, sc in zip(arrays, scatter):
        own = lax.dynamic_index_in_dim(arr, me, 0, keepdims=True) if sc else arr[None]
        shape = arr.shape if sc else (N_DEV,) + arr.shape
        lands.append(lax.dynamic_update_slice(lax.empty(shape, arr.dtype), own, (me,) + (0,) * (len(shape) - 1)))

    def body(*refs):
        ins, lnd = refs[:n], refs[n:2 * n]
        send_sems, recv_sems = refs[2 * n + 1], refs[2 * n + 2]
        token = refs[-1]
        pos = _my_pos()
        for a in range(n):
            for r in peers:
                _exchange_copy(ins, lnd, send_sems, recv_sems, scatter, pos, a, r, receiving=False).start()
        token[...] = jnp.zeros_like(token)

    bufs = [pltpu.with_memory_space_constraint(t, pltpu.HBM) for t in list(arrays) + lands]
    sems = pltpu.SemaphoreType.DMA((n * (N_DEV - 1),))
    res = pl.pallas_call(
        body, name=name,
        out_shape=(sems, sems, *[pltpu.HBM(t.shape, t.dtype) for t in bufs], jax.ShapeDtypeStruct((8, 128), F32)),
        in_specs=[_HBM] * (2 * n) + [pl.BlockSpec(memory_space=pl.ANY)],
        out_specs=(_SEM, _SEM, *[_HBM] * (2 * n), pl.BlockSpec(memory_space=pltpu.VMEM)),
        input_output_aliases={i: 2 + i for i in range(2 * n)},
        compiler_params=pltpu.CompilerParams(has_side_effects=_EFFECT),
    )(*bufs, after)
    return res[0], res[1], list(res[2:2 + n]), list(res[2 + n:2 + 2 * n]), res[-1]


def _exchange_wait(started, scatter, after, *, name, peers=ALL_PEERS):
    send_sems, recv_sems, arrays, lands, _ = started
    n = len(arrays)

    def body(*refs):
        ins, lnd = refs[:n], refs[n:2 * n]
        send_sems, recv_sems = refs[2 * n], refs[2 * n + 1]
        pos = _my_pos()
        for a in range(n):
            for r in peers:
                _exchange_copy(ins, lnd, send_sems, recv_sems, scatter, pos, a, r, receiving=False).wait_send()
                _exchange_copy(ins, lnd, send_sems, recv_sems, scatter, pos, a, r, receiving=True).wait_recv()

    bufs = list(arrays) + list(lands)
    after = list(after) if isinstance(after, (list, tuple)) else [after]
    res = pl.pallas_call(
        body, name=name,
        out_shape=tuple(pltpu.HBM(t.shape, t.dtype) for t in bufs),
        in_specs=[_HBM] * (2 * n) + [_SEM, _SEM] + [pl.BlockSpec(memory_space=pl.ANY)] * len(after),
        out_specs=tuple([_HBM] * (2 * n)),
        input_output_aliases={i: i for i in range(2 * n)},
        compiler_params=pltpu.CompilerParams(has_side_effects=_EFFECT),
    )(*bufs, send_sems, recv_sems, *after)
    return list(res[n:])


def _forward_copy(lands, send_sems, recv_sems, pos, a, q, receiving):
    slot = _dev_index(_flip(pos, q ^ 1 if receiving else q))
    idx = a * len(FORWARDED) + FORWARDED.index(q)
    return pltpu.make_async_remote_copy(
        src_ref=lands[a].at[slot], dst_ref=lands[a].at[slot], send_sem=send_sems.at[idx], recv_sem=recv_sems.at[idx],
        device_id=_flip(pos, 1), device_id_type=pl.DeviceIdType.MESH)


def _forward_start(lands, after, *, name):
    n = len(lands)

    def body(*refs):
        lnd = refs[:n]
        send_sems, recv_sems = refs[n + 1], refs[n + 2]
        token = refs[-1]
        pos = _my_pos()
        for a in range(n):
            for q in FORWARDED:
                _forward_copy(lnd, send_sems, recv_sems, pos, a, q, receiving=False).start()
        token[...] = jnp.zeros_like(token)

    sems = pltpu.SemaphoreType.DMA((n * len(FORWARDED),))
    res = pl.pallas_call(
        body, name=name,
        out_shape=(sems, sems, *[pltpu.HBM(t.shape, t.dtype) for t in lands], jax.ShapeDtypeStruct((8, 128), F32)),
        in_specs=[_HBM] * n + [pl.BlockSpec(memory_space=pl.ANY)],
        out_specs=(_SEM, _SEM, *[_HBM] * n, pl.BlockSpec(memory_space=pltpu.VMEM)),
        input_output_aliases={i: 2 + i for i in range(n)},
        compiler_params=pltpu.CompilerParams(has_side_effects=_EFFECT),
    )(*lands, after)
    return res[0], res[1], list(res[2:2 + n]), res[-1]


def _forward_wait(forwarded, after, *, name):
    send_sems, recv_sems, lands, _ = forwarded
    n = len(lands)

    def body(*refs):
        lnd = refs[:n]
        send_sems, recv_sems = refs[n], refs[n + 1]
        pos = _my_pos()
        for a in range(n):
            for q in FORWARDED:
                _forward_copy(lnd, send_sems, recv_sems, pos, a, q, receiving=False).wait_send()
                _forward_copy(lnd, send_sems, recv_sems, pos, a, q, receiving=True).wait_recv()

    res = pl.pallas_call(
        body, name=name,
        out_shape=tuple(pltpu.HBM(t.shape, t.dtype) for t in lands),
        in_specs=[_HBM] * n + [_SEM, _SEM, pl.BlockSpec(memory_space=pl.ANY)],
        out_specs=tuple([_HBM] * n),
        input_output_aliases={i: i for i in range(n)},
        compiler_params=pltpu.CompilerParams(has_side_effects=_EFFECT),
    )(*lands, send_sems, recv_sems, after)
    return list(res)


def _adamw_update(p_ref, w_ref, m_ref, v_ref, g_ref, d_ref, mo_ref, vo_ref):
    g = p_ref[0].astype(F32)
    for k in range(1, N_DEV):
        g = g + p_ref[k].astype(F32)
    mn = ADAM_B1 * m_ref[...] + (1.0 - ADAM_B1) * g
    vn = ADAM_B2 * v_ref[...] + (1.0 - ADAM_B2) * (g * g)
    m_hat = mn / (1.0 - ADAM_B1 ** ADAM_STEP)
    v_hat = vn / (1.0 - ADAM_B2 ** ADAM_STEP)
    g_ref[...] = g
    d_ref[...] = -ADAM_LR * (m_hat / (jnp.sqrt(v_hat) + ADAM_EPS) + ADAM_WD * w_ref[...])
    mo_ref[...] = mn
    vo_ref[...] = vn


def _adamw_small(parts, ws, ms, vs, *, name):
    n = len(ws)

    def body(*refs):
        ins, outs = refs[:4 * n], refs[4 * n:]
        for j in range(n):
            _adamw_update(ins[j], ins[n + j], ins[2 * n + j], ins[3 * n + j], *outs[4 * j:4 * j + 4])

    return pl.pallas_call(
        body, name=name,
        out_shape=[jax.ShapeDtypeStruct(w.shape, F32) for w in ws for _ in range(4)],
    )(*parts, *ws, *ms, *vs)


def _adamw(parts, w, m, v, *, name, tr):
    nl, r, c = w.shape
    assert len(parts) == nl and r % tr == 0
    n_i = r // tr

    def body(*refs):
        p_refs = refs[:nl]
        w_ref, m_ref, v_ref = refs[nl:nl + 3]

        def update(p_ref):
            _adamw_update(p_ref, w_ref, m_ref, v_ref, *refs[nl + 3:])

        if nl == 1:
            update(p_refs[0])
        else:
            for layer in range(nl):
                pl.when(pl.program_id(0) == layer)(lambda p_ref=p_refs[layer]: update(p_ref))

    def parts_spec(layer):
        def index(l, i):
            return (0, jnp.where(l < layer, 0, jnp.where(l > layer, n_i - 1, i)), 0)
        return pl.BlockSpec((N_DEV, tr, c), index)

    blk = pl.BlockSpec((None, tr, c), lambda l, i: (l, i, 0))
    return pl.pallas_call(
        body, name=name, grid=(nl, n_i),
        in_specs=[parts_spec(layer) for layer in range(nl)] + [blk, blk, blk],
        out_specs=[blk] * 4, out_shape=[jax.ShapeDtypeStruct((nl, r, c), F32)] * 4,
        compiler_params=_params(("arbitrary", "arbitrary")),
    )(*parts, w, m, v)


def _pack(parts, lead=()):
    flat = jnp.concatenate([p.reshape(lead + (-1,)) for p in parts], axis=-1)
    n = flat.shape[-1]
    rows = -(-n // (8 * 128)) * 8
    flat = jnp.pad(flat, [(0, 0)] * len(lead) + [(0, rows * 128 - n)])
    return flat.reshape(lead + (rows, 128))


def _to_dev_major(g, axis):
    shp = g.shape
    g = g.reshape(shp[:axis] + (N_DEV, shp[axis] // N_DEV) + shp[axis + 1:])
    return jnp.moveaxis(g, axis, 0)


def _from_dev_major(g, axis):
    g = jnp.moveaxis(g, 0, axis)
    shp = g.shape
    return g.reshape(shp[:axis] + (shp[axis] * shp[axis + 1],) + shp[axis + 2:])


SMALL_SHARDED = ("conv_a", "conv_b", "norm_mix_odd", "pool_scale", "conv_ffn_w")
SMALL_REPLICATED = ("norm_mix_even", "ln_a_g", "ln_a_b", "norm_ffn", "norm_final")
BIG = {"w_in": ("w_in", 0, 320), "w_out": ("w_out", 0, 128), "w_pool": ("w_pool", 0, 128),
       "w_up0": ("w_up", 0, 352), "w_up1": ("w_up", 1, 352),
       "w_down0": ("w_down", 0, 352), "w_down1": ("w_down", 1, 352)}
COLUMN_SHARDED = ("w_in", "w_up")


def kernel(x, norm_mix_even, w_in, conv_a, ln_a_g, ln_a_b, conv_b, w_out, norm_mix_odd, w_pool, pool_scale, norm_ffn, w_up, conv_ffn_w, w_down, norm_final, loss_target, m_norm_mix_even, m_w_in, m_conv_a, m_ln_a_g, m_ln_a_b, m_conv_b, m_w_out, m_norm_mix_odd, m_w_pool, m_pool_scale, m_norm_ffn, m_w_up, m_conv_ffn_w, m_w_down, m_norm_final, v_norm_mix_even, v_w_in, v_conv_a, v_ln_a_g, v_ln_a_b, v_conv_b, v_w_out, v_norm_mix_odd, v_w_pool, v_pool_scale, v_norm_ffn, v_w_up, v_conv_ffn_w, v_w_down, v_norm_final):
    names = ("norm_mix_even", "w_in", "conv_a", "ln_a_g", "ln_a_b", "conv_b", "w_out", "norm_mix_odd", "w_pool",
             "pool_scale", "norm_ffn", "w_up", "conv_ffn_w", "w_down", "norm_final")
    wts = dict(zip(names, (norm_mix_even, w_in, conv_a, ln_a_g, ln_a_b, conv_b, w_out, norm_mix_odd, w_pool,
                           pool_scale, norm_ffn, w_up, conv_ffn_w, w_down, norm_final)))
    mom = dict(zip(names, (m_norm_mix_even, m_w_in, m_conv_a, m_ln_a_g, m_ln_a_b, m_conv_b, m_w_out, m_norm_mix_odd,
                           m_w_pool, m_pool_scale, m_norm_ffn, m_w_up, m_conv_ffn_w, m_w_down, m_norm_final)))
    var = dict(zip(names, (v_norm_mix_even, v_w_in, v_conv_a, v_ln_a_g, v_ln_a_b, v_conv_b, v_w_out, v_norm_mix_odd,
                           v_w_pool, v_pool_scale, v_norm_ffn, v_w_up, v_conv_ffn_w, v_w_down, v_norm_final)))
    d = x.shape[-1]

    def shard3d(t, param):
        a = t[param]
        if param in COLUMN_SHARDED:
            return jnp.swapaxes(a, 1, 2)
        return a.reshape(a.shape[0], -1, a.shape[-1])

    def unshard3d(a, param):
        if param in COLUMN_SHARDED:
            return jnp.swapaxes(a, 1, 2)
        return a.reshape(wts[param].shape)

    def shard2d(t, key):
        param, layer, _ = BIG[key]
        return shard3d(t, param)[layer]

    small_w = _pack([wts[k] for k in SMALL_SHARDED])
    bf = {k: shard2d(wts, k).astype(BF16) for k in BIG}
    gather_groups = {"mix0": ("w_in", "w_out", "small"), "ffn0": ("w_up0", "w_down0"),
                     "ffn1": ("w_pool", "w_up1", "w_down1")}
    order = list(gather_groups)
    started = {}

    def start_gather(grp, after):
        arrs = [small_w if k == "small" else bf[k] for k in gather_groups[grp]]
        started[grp] = _exchange_start(arrs, [False] * len(arrs), after, name=f"gather_{grp}_start", peers=CHIP_PEERS)

    start_gather(order[0], small_w)

    def weights(grp, after):
        keys = gather_groups[grp]
        lands = _exchange_wait(started[grp], [False] * len(keys), after, name=f"gather_{grp}_wait", peers=CHIP_PEERS)
        forwarded = _forward_start(lands, small_w, name=f"gather_{grp}_forward")
        if grp != order[-1]:
            start_gather(order[order.index(grp) + 1], forwarded[-1])
            behind = started[order[order.index(grp) + 1]][-1]
        else:
            behind = forwarded[-1]
        gw = dict(zip(keys, _forward_wait(forwarded, behind, name=f"gather_{grp}_forward_wait")))
        if grp == "ffn0":
            return {"w_up": gw["w_up0"], "w_down": gw["w_down0"].reshape(N_PAIR, -1, d)}
        if grp == "ffn1":
            return {"w_up": gw["w_up1"], "w_down": gw["w_down1"].reshape(N_PAIR, -1, d),
                    "w_pool": _from_dev_major(gw["w_pool"].reshape(N_DEV, len(POOL_WINDOWS), -1, POOL_GROUP), 1)}
        per_dev = gw["small"].reshape(N_DEV, -1)
        sizes = [wts[k].size for k in SMALL_SHARDED]
        offs = [sum(sizes[:i]) for i in range(len(sizes))]
        small_full = {k: per_dev[:, o:o + n_].reshape((N_DEV,) + wts[k].shape)
                      for k, o, n_ in zip(SMALL_SHARDED, offs, sizes)}
        return {
            "norm_mix_even": norm_mix_even, "ln_a_g": ln_a_g, "ln_a_b": ln_a_b, "norm_ffn": norm_ffn,
            "norm_final": norm_final[None],
            "w_in_t": gw["w_in"].reshape(-1, d),
            "w_out": gw["w_out"].reshape(-1, d),
            "conv_a": _from_dev_major(small_full["conv_a"][:, 0], 1),
            "conv_b": _from_dev_major(small_full["conv_b"][:, 0], 1),
            "norm_mix_odd": _from_dev_major(small_full["norm_mix_odd"], 1),
            "pool_scale": _from_dev_major(small_full["pool_scale"], 1),
            "conv_ffn": [small_full["conv_ffn_w"][:, l] for l in range(2)],
        }

    def dev_major(k, g):
        if k == "w_pool":
            return _to_dev_major(g, 1).reshape(N_DEV, -1, POOL_GROUP)
        return g.reshape(N_DEV, -1, g.shape[-1])

    sent = {}

    def grads_ready(grp, grads):
        keys = tuple(grads)
        parts = [dev_major(k, grads[k]) for k in keys]
        sent[grp] = (keys, _exchange_start(parts, [True] * len(keys), small_w, name=f"grads_{grp}_start"))
        return sent[grp][1][-1]

    loss, dx, g = _local_step(x[0], loss_target[0], weights, grads_ready)

    def small2d(a):
        return a.reshape(-1, a.shape[-1])

    small_keys = SMALL_SHARDED + SMALL_REPLICATED
    small_parts = [_to_dev_major(g["conv_a"], 1), _to_dev_major(g["conv_b"], 1),
                   _to_dev_major(g["norm_mix_odd"], 1), _to_dev_major(g["pool_scale"], 1),
                   jnp.stack(g["conv_ffn"], axis=1).reshape(N_DEV, -1, w_up.shape[-1])]
    small_parts += [g[k] for k in SMALL_REPLICATED]
    small_scatter = [True] * len(SMALL_SHARDED) + [False] * len(SMALL_REPLICATED)
    small_parts.append(jnp.full((1, 128), loss, F32))
    small_scatter.append(False)
    sent["small"] = (small_keys + ("loss",),
                     _exchange_start(small_parts, small_scatter, dx, name="grads_small_start"))

    landed, out, raw = {}, {}, {}

    def wait_grads(grp, after, scat=None):
        keys, st = sent[grp]
        scat = [True] * len(keys) if scat is None else scat
        landed.update(zip(keys, _exchange_wait(st, scat, after, name=f"grads_{grp}_wait")))

    def update(param, keys):
        raw[param] = _adamw([landed[k] for k in keys], shard3d(wts, param), shard3d(mom, param),
                            shard3d(var, param), name=f"adamw_{param}", tr=BIG[keys[0]][2])
        out[param] = [unshard3d(t, param) for t in raw[param]]

    wait_grads("ffn1", sent["small"][1][-1])
    wait_grads("ffn0", landed["w_up1"])
    update("w_up", ("w_up0", "w_up1"))
    update("w_down", ("w_down0", "w_down1"))
    wait_grads("mix0", [raw["w_up"][1], raw["w_down"][1]])
    update("w_in", ("w_in",))
    update("w_out", ("w_out",))
    update("w_pool", ("w_pool",))
    wait_grads("small", raw["w_pool"][1], small_scatter)
    res = _adamw_small([landed[k] for k in small_keys], [small2d(wts[k]) for k in small_keys],
                       [small2d(mom[k]) for k in small_keys], [small2d(var[k]) for k in small_keys],
                       name="adamw_small")
    for j, k in enumerate(small_keys):
        out[k] = [t.reshape(wts[k].shape) for t in res[4 * j:4 * j + 4]]

    loss = jnp.sum(landed["loss"][:, 0, 0])
    return (loss, dx[None], *[out[k][0] for k in names], *[out[k][1] for k in names],
            *[out[k][2] for k in names], *[out[k][3] for k in names])
```

```python
import jax
import jax.numpy as jnp
from jax import lax
from jax.experimental import pallas as pl
from jax.experimental.pallas import tpu as pltpu

F32 = jnp.float32
BF16 = jnp.bfloat16

RMS_EPS = 1e-6
LN_EPS = 1e-5
ADAM_LR = 0.001
ADAM_B1 = 0.9
ADAM_B2 = 0.999
ADAM_EPS = 1e-08
ADAM_WD = 0.01
ADAM_STEP = 10

N_DEV = 8
N_PAIR = N_DEV // 2
A_WIDTH = 512
A_TAPS = 31
POOL_WINDOWS = (2, 4, 8, 16)
POOL_GROUP = 256
HALO_A = 32
HALO_S = 16
VMEM_LIMIT = 56 * 1024 * 1024


def _params(sem, vmem=VMEM_LIMIT):
    return pltpu.CompilerParams(dimension_semantics=sem, vmem_limit_bytes=vmem)


def _sigmoid(x):
    return 0.5 * jnp.tanh(0.5 * x) + 0.5


def _prev_blk(i, ts, hb):
    return jnp.maximum(i * (ts // hb) - 1, 0)


def _next_blk(i, ts, hb, s):
    return jnp.minimum((i + 1) * (ts // hb), s // hb - 1)


def _mm(a, b, *, name, ta=False, tb=False, out_dtype=F32, tm=512, tn=512):
    m, k = (a.shape[1], a.shape[0]) if ta else a.shape
    n = b.shape[0] if tb else b.shape[1]
    tm, tn = min(tm, m), min(tn, n)
    assert m % tm == 0 and n % tn == 0, (name, m, n, tm, tn)
    dims = (((0,) if ta else (1,), (1,) if tb else (0,)), ((), ()))

    def body(a_ref, b_ref, o_ref):
        o_ref[...] = lax.dot_general(a_ref[...], b_ref[...], dims, preferred_element_type=F32).astype(out_dtype)

    a_spec = pl.BlockSpec((k, tm), lambda i, j: (0, i)) if ta else pl.BlockSpec((tm, k), lambda i, j: (i, 0))
    b_spec = pl.BlockSpec((tn, k), lambda i, j: (j, 0)) if tb else pl.BlockSpec((k, tn), lambda i, j: (0, j))
    return pl.pallas_call(
        body, name=name, grid=(m // tm, n // tn),
        in_specs=[a_spec, b_spec], out_specs=pl.BlockSpec((tm, tn), lambda i, j: (i, j)),
        out_shape=jax.ShapeDtypeStruct((m, n), out_dtype),
        compiler_params=_params(("parallel", "parallel")),
    )(a, b)


def _rms_rows(xv, gv):
    return xv * lax.rsqrt(jnp.mean(xv * xv, axis=-1, keepdims=True) + RMS_EPS) * gv


_NT = (((1,), (1,)), ((), ()))
_TN = (((0,), (0,)), ((), ()))


def _rms_mm(x, g, wt, *, name, out_dtype, tm=1024, tn=512):
    s, d = x.shape
    n = wt.shape[0]
    tm = min(tm, s)
    assert s % tm == 0 and n % tn == 0

    def body(x_ref, g_ref, w_ref, h_ref, z_ref, hs_ref):
        @pl.when(pl.program_id(1) == 0)
        def _():
            hv = _rms_rows(x_ref[...], g_ref[...]).astype(BF16)
            hs_ref[...] = hv
            h_ref[...] = hv

        z_ref[...] = lax.dot_general(hs_ref[...], w_ref[...], _NT, preferred_element_type=F32).astype(out_dtype)

    return pl.pallas_call(
        body, name=name, grid=(s // tm, n // tn),
        in_specs=[pl.BlockSpec((tm, d), lambda i, j: (i, 0)),
                  pl.BlockSpec((1, d), lambda i, j: (0, 0)),
                  pl.BlockSpec((tn, d), lambda i, j: (j, 0))],
        out_specs=[pl.BlockSpec((tm, d), lambda i, j: (i, 0)),
                   pl.BlockSpec((tm, tn), lambda i, j: (i, j))],
        out_shape=[jax.ShapeDtypeStruct((s, d), BF16), jax.ShapeDtypeStruct((s, n), out_dtype)],
        scratch_shapes=[pltpu.VMEM((tm, d), BF16)],
        compiler_params=_params(("parallel", "arbitrary")),
    )(x, g, wt)


def _rms_bwd_rows(xv, gv, dh):
    r = lax.rsqrt(jnp.mean(xv * xv, axis=-1, keepdims=True) + RMS_EPS)
    xh = xv * r
    dn = dh * gv
    dx = r * (dn - xh * jnp.mean(dn * xh, axis=-1, keepdims=True))
    return dx, dh * xh


def _mm_rms_bwd(a, b, x, g, dres, dep, *, name, tm=512):
    s, k = a.shape
    d = b.shape[1]
    tm = min(tm, s)

    def body(a_ref, b_ref, x_ref, g_ref, dres_ref, dep_ref, dx_ref, dg_ref):
        @pl.when(pl.program_id(0) == 0)
        def _():
            dg_ref[...] = jnp.zeros_like(dg_ref)

        dh = jnp.dot(a_ref[...], b_ref[...], preferred_element_type=F32)
        dx, dgr = _rms_bwd_rows(x_ref[...], g_ref[...], dh)
        dx_ref[...] = dres_ref[...] + dx
        dg_ref[...] += jnp.sum(dgr, axis=0, keepdims=True)

    row = pl.BlockSpec((tm, d), lambda i: (i, 0))
    vec = pl.BlockSpec((1, d), lambda i: (0, 0))
    return pl.pallas_call(
        body, name=name, grid=(s // tm,),
        in_specs=[pl.BlockSpec((tm, k), lambda i: (i, 0)), pl.BlockSpec((k, d), lambda i: (0, 0)), row, vec, row,
                  pl.BlockSpec(memory_space=pl.ANY)],
        out_specs=[row, vec],
        out_shape=[jax.ShapeDtypeStruct((s, d), F32), jax.ShapeDtypeStruct((1, d), F32)],
        compiler_params=_params(("arbitrary",)),
    )(a, b, x, g, dres, dep)


def _conv_taps(ext_ref, w_ref, n_taps, base, r0, rows, reverse=False):
    acc = None
    for k in range(n_taps):
        off = r0 + (base - k if reverse else base + k)
        term = w_ref[k:k + 1, :] * ext_ref[pl.ds(off, rows), :]
        acc = term if acc is None else acc + term
    return acc


def _shift_copies(src_ref, sh_ref, rows):
    for b in range(1, 8):
        sh_ref[b, 0:rows, :] = src_ref[pl.ds(b, rows), :]


def _shifted(src_ref, sh_ref, start, rows, off):
    a, b = divmod(off, 8)
    ref = src_ref if b == 0 else sh_ref.at[b]
    return ref[pl.ds(start + 8 * a, rows), :]


def _mix0_fwd(z, conv_a, ln_g, ln_b, conv_b, w_out, x, *, name, ts=512, rc=32):
    s = z.shape[0]
    c = A_WIDTH
    hb = HALO_A
    half = ts // 2

    def body(z_ref, zp_ref, wa_ref, lg_ref, lb_ref, wb_ref, wo_ref, x_ref, ab_ref, ca_ref, o_ref, exta, extb, sha):
        keep = jnp.where(pl.program_id(0) > 0, 1.0, 0.0)
        zp = zp_ref[...]
        exta[0:hb, :] = zp[:, 0:c] * _sigmoid(zp[:, c:2 * c]) * keep
        extb[0:hb, :] = zp[:, 3 * c:4 * c] * zp[:, 4 * c:5 * c] * keep
        exta[hb:hb + ts, :] = z_ref[:, 0:c] * _sigmoid(z_ref[:, c:2 * c])
        extb[hb:hb + ts, :] = z_ref[:, 3 * c:4 * c] * z_ref[:, 4 * c:5 * c]
        _shift_copies(exta, sha, hb + ts - 8)
        lg = lg_ref[...]
        lb = lb_ref[...]
        for q in range(ts // rc):
            r0 = q * rc
            ca = None
            for k in range(A_TAPS):
                term = wa_ref[k:k + 1, :] * _shifted(exta, sha, r0, rc, hb - (A_TAPS - 1) + k)
                ca = term if ca is None else ca + term
            ca_ref[r0:r0 + rc, :] = ca
            mu = jnp.mean(ca, axis=-1, keepdims=True)
            xc = ca - mu
            rs = lax.rsqrt(jnp.mean(xc * xc, axis=-1, keepdims=True) + LN_EPS)
            l = xc * rs * lg + lb
            ab_ref[r0:r0 + rc, 0:c] = (l * _sigmoid(l)).astype(BF16)
            cbc = _conv_taps(extb, wb_ref, 3, hb - 2, r0, rc)
            ab_ref[r0:r0 + rc, c:2 * c] = (z_ref[r0:r0 + rc, 2 * c:3 * c] * cbc).astype(BF16)
            if (r0 + rc) % half == 0:
                rows = slice(r0 + rc - half, r0 + rc)
                o_ref[rows, :] = x_ref[rows, :] + jnp.dot(ab_ref[rows, :], wo_ref[...], preferred_element_type=F32)

    d = x.shape[1]
    row = pl.BlockSpec((ts, d), lambda i: (i, 0))
    return pl.pallas_call(
        body, name=name, grid=(s // ts,),
        in_specs=[pl.BlockSpec((ts, 5 * c), lambda i: (i, 0)),
                  pl.BlockSpec((hb, 5 * c), lambda i: (_prev_blk(i, ts, hb), 0)),
                  pl.BlockSpec((32, c), lambda i: (0, 0)),
                  pl.BlockSpec((1, c), lambda i: (0, 0)),
                  pl.BlockSpec((1, c), lambda i: (0, 0)),
                  pl.BlockSpec((8, c), lambda i: (0, 0)),
                  pl.BlockSpec((2 * c, d), lambda i: (0, 0)), row],
        out_specs=[pl.BlockSpec((ts, 2 * c), lambda i: (i, 0)),
                   pl.BlockSpec((ts, c), lambda i: (i, 0)), row],
        out_shape=[jax.ShapeDtypeStruct((s, 2 * c), BF16), jax.ShapeDtypeStruct((s, c), F32),
                   jax.ShapeDtypeStruct((s, d), F32)],
        scratch_shapes=[pltpu.VMEM((hb + ts, c), F32), pltpu.VMEM((hb + ts, c), F32),
                        pltpu.VMEM((8, hb + ts - 8, c), F32)],
        compiler_params=_params(("parallel",)),
    )(z, z, conv_a, ln_g, ln_b, conv_b, w_out, x)


def _mix0_bwd(z, ca, dab, conv_a, ln_g, ln_b, conv_b, *, name, ts=512, rc=32):
    s = z.shape[0]
    c = A_WIDTH
    hb = HALO_A
    ta = A_TAPS

    def body(z_ref, zp_ref, zn_ref, ca_ref, can_ref, d_ref, dn_ref, wa_ref, lg_ref, lb_ref, wb_ref,
             dz_ref, dwa_ref, dwb_ref, dlg_ref, dlb_ref, exta, extb, extdca, extdcb, shd):
        i = pl.program_id(0)
        keep_p = jnp.where(i > 0, 1.0, 0.0)
        keep_n = jnp.where(i < s // ts - 1, 1.0, 0.0)

        @pl.when(i == 0)
        def _():
            dwa_ref[...] = jnp.zeros_like(dwa_ref)
            dwb_ref[...] = jnp.zeros_like(dwb_ref)
            dlg_ref[...] = jnp.zeros_like(dlg_ref)
            dlb_ref[...] = jnp.zeros_like(dlb_ref)

        lg = lg_ref[...]
        lb = lb_ref[...]
        zp = zp_ref[...]
        exta[0:hb, :] = zp[:, 0:c] * _sigmoid(zp[:, c:2 * c]) * keep_p
        extb[0:hb, :] = zp[:, 3 * c:4 * c] * zp[:, 4 * c:5 * c] * keep_p
        exta[hb:hb + ts, :] = z_ref[:, 0:c] * _sigmoid(z_ref[:, c:2 * c])
        extb[hb:hb + ts, :] = z_ref[:, 3 * c:4 * c] * z_ref[:, 4 * c:5 * c]

        def ln_bwd(cav, dav):
            mu = jnp.mean(cav, axis=-1, keepdims=True)
            xc = cav - mu
            rs = lax.rsqrt(jnp.mean(xc * xc, axis=-1, keepdims=True) + LN_EPS)
            nv = xc * rs
            l = nv * lg + lb
            sg = _sigmoid(l)
            dl = dav * (sg * (1.0 + l * (1.0 - sg)))
            dnv = dl * lg
            dca = rs * (dnv - jnp.mean(dnv, axis=-1, keepdims=True)
                        - nv * jnp.mean(dnv * nv, axis=-1, keepdims=True))
            return dca, dl, nv

        dlg_acc = jnp.zeros((1, c), F32)
        dlb_acc = jnp.zeros((1, c), F32)
        for q in range(ts // rc):
            r0 = q * rc
            dca, dl, nv = ln_bwd(ca_ref[r0:r0 + rc, :], d_ref[r0:r0 + rc, 0:c])
            extdca[r0:r0 + rc, :] = dca
            dlg_acc = dlg_acc + jnp.sum(dl * nv, axis=0, keepdims=True)
            dlb_acc = dlb_acc + jnp.sum(dl, axis=0, keepdims=True)
            extdcb[r0:r0 + rc, :] = d_ref[r0:r0 + rc, c:2 * c] * z_ref[r0:r0 + rc, 2 * c:3 * c]
        dca_n, _, _ = ln_bwd(can_ref[...], dn_ref[:, 0:c])
        extdca[ts:ts + hb, :] = dca_n * keep_n
        extdcb[ts:ts + hb, :] = dn_ref[:, c:2 * c] * zn_ref[:, 2 * c:3 * c] * keep_n
        dlg_ref[...] += dlg_acc
        dlb_ref[...] += dlb_acc
        _shift_copies(extdca, shd, ts + hb - 8)

        for q in range(ts // rc):
            r0 = q * rc
            zr = z_ref[r0:r0 + rc, :]
            dga = None
            for k in range(ta):
                term = wa_ref[k:k + 1, :] * _shifted(extdca, shd, r0, rc, ta - 1 - k)
                dga = term if dga is None else dga + term
            sg = _sigmoid(zr[:, c:2 * c])
            dz_ref[r0:r0 + rc, 0:c] = (dga * sg).astype(BF16)
            dz_ref[r0:r0 + rc, c:2 * c] = (dga * zr[:, 0:c] * sg * (1.0 - sg)).astype(BF16)
            cbc = _conv_taps(extb, wb_ref, 3, hb - 2, r0, rc)
            dz_ref[r0:r0 + rc, 2 * c:3 * c] = (d_ref[r0:r0 + rc, c:2 * c] * cbc).astype(BF16)
            dcb = _conv_taps(extdcb, wb_ref, 3, 2, r0, rc, reverse=True)
            dz_ref[r0:r0 + rc, 3 * c:4 * c] = (dcb * zr[:, 4 * c:5 * c]).astype(BF16)
            dz_ref[r0:r0 + rc, 4 * c:5 * c] = (dcb * zr[:, 3 * c:4 * c]).astype(BF16)

        for k in range(ta):
            part = None
            for q in range(ts // rc):
                r0 = q * rc
                p = exta[hb + r0:hb + r0 + rc, :] * _shifted(extdca, shd, r0, rc, ta - 1 - k)
                for r in range(0, rc, 8):
                    part = p[r:r + 8, :] if part is None else part + p[r:r + 8, :]
            dwa_ref[k:k + 1, :] += jnp.sum(part, axis=0, keepdims=True)
        dcb_t = extdcb[0:ts, :]
        for k in range(3):
            dwb_ref[k:k + 1, :] += jnp.sum(dcb_t * extb[pl.ds(hb - 2 + k, ts), :], axis=0, keepdims=True)

    def tile(w):
        return pl.BlockSpec((ts, w), lambda i: (i, 0))

    def prev(w):
        return pl.BlockSpec((hb, w), lambda i: (_prev_blk(i, ts, hb), 0))

    def nxt(w):
        return pl.BlockSpec((hb, w), lambda i: (_next_blk(i, ts, hb, s), 0))

    def const(r, w):
        return pl.BlockSpec((r, w), lambda i: (0, 0))

    return pl.pallas_call(
        body, name=name, grid=(s // ts,),
        in_specs=[tile(5 * c), prev(5 * c), nxt(5 * c), tile(c), nxt(c), tile(2 * c), nxt(2 * c),
                  const(32, c), const(1, c), const(1, c), const(8, c)],
        out_specs=[tile(5 * c), const(32, c), const(8, c), const(1, c), const(1, c)],
        out_shape=[jax.ShapeDtypeStruct((s, 5 * c), BF16), jax.ShapeDtypeStruct((32, c), F32),
                   jax.ShapeDtypeStruct((8, c), F32), jax.ShapeDtypeStruct((1, c), F32),
                   jax.ShapeDtypeStruct((1, c), F32)],
        scratch_shapes=[pltpu.VMEM((hb + ts, c), F32), pltpu.VMEM((hb + ts, c), F32),
                        pltpu.VMEM((ts + hb, c), F32), pltpu.VMEM((ts + hb, c), F32),
                        pltpu.VMEM((8, ts + hb - 8, c), F32)],
        compiler_params=_params(("arbitrary",)),
    )(z, z, z, ca, ca, dab, dab, conv_a, ln_g, ln_b, conv_b)


def _ffn_up(x, g, w8, *, name, tm=1024):
    s, d = x.shape
    nb, c, _ = w8.shape
    tm = min(tm, s)
    grp = N_PAIR

    def body(x_ref, g_ref, w_ref, h_ref, u_ref, hs_ref):
        @pl.when(pl.program_id(1) == 0)
        def _():
            hv = _rms_rows(x_ref[...], g_ref[...]).astype(BF16)
            hs_ref[...] = hv
            h_ref[...] = hv

        res = lax.dot_general(hs_ref[...], w_ref[...].reshape(grp * c, d), _NT, preferred_element_type=F32)
        for b in range(grp):
            u_ref[b] = res[:, b * c:(b + 1) * c].astype(BF16)

    return pl.pallas_call(
        body, name=name, grid=(s // tm, nb // grp),
        in_specs=[pl.BlockSpec((tm, d), lambda i, k: (i, 0)),
                  pl.BlockSpec((1, d), lambda i, k: (0, 0)),
                  pl.BlockSpec((grp, c, d), lambda i, k: (k, 0, 0))],
        out_specs=[pl.BlockSpec((tm, d), lambda i, k: (i, 0)),
                   pl.BlockSpec((grp, tm, c), lambda i, k: (k, i, 0))],
        out_shape=[jax.ShapeDtypeStruct((s, d), BF16), jax.ShapeDtypeStruct((nb, s, c), BF16)],
        scratch_shapes=[pltpu.VMEM((tm, d), BF16)],
        compiler_params=_params(("parallel", "arbitrary")),
    )(x, g, w8)


def _ffn_mid_down(up8, wc8, wd4, x, head=None, *, name, ts=256, rc=256):
    nb, s, c = up8.shape
    d = x.shape[1]
    hb = HALO_S
    ts = min(ts, s)

    def body(*refs):
        u_ref, up_ref, wc_ref, wd_ref, x_ref = refs[:5]
        n_out = 4 if head is None else 7
        outs = refs[-1 - n_out:-1]
        act_ref, ug_ref, uv_ref = outs[-3:]
        ext = refs[-1]
        keep = jnp.where(pl.program_id(0) > 0, 1.0, 0.0)
        acc = x_ref[...]
        for j in range(N_PAIR):
            eg, ev = ext.at[2 * (j % 2)], ext.at[2 * (j % 2) + 1]
            eg[0:hb, :] = up_ref[j].astype(F32) * keep
            ev[0:hb, :] = up_ref[j + N_PAIR].astype(F32) * keep
            eg[hb:hb + ts, :] = u_ref[j].astype(F32)
            ev[hb:hb + ts, :] = u_ref[j + N_PAIR].astype(F32)
            for q in range(ts // rc):
                r0 = q * rc
                gg = _conv_taps(eg, wc_ref.at[j], 3, hb - 2, r0, rc)
                vv = _conv_taps(ev, wc_ref.at[j + N_PAIR], 3, hb - 2, r0, rc)
                ug_ref[j, r0:r0 + rc, :] = gg.astype(BF16)
                uv_ref[j, r0:r0 + rc, :] = vv.astype(BF16)
                act_ref[j, r0:r0 + rc, :] = (gg * _sigmoid(gg) * vv).astype(BF16)
            acc = acc + jnp.dot(act_ref[j], wd_ref[j], preferred_element_type=F32)
        if head is None:
            outs[0][...] = acc
            return
        g_ref, t_ref = refs[5], refs[6]
        loss_ref, dx_ref, dxb_ref, dg_ref = outs[:4]

        @pl.when(pl.program_id(0) == 0)
        def _():
            loss_ref[...] = jnp.zeros_like(loss_ref)
            dg_ref[...] = jnp.zeros_like(dg_ref)

        gv = g_ref[...]
        r = lax.rsqrt(jnp.mean(acc * acc, axis=-1, keepdims=True) + RMS_EPS)
        xh = acc * r
        err = xh * gv - t_ref[...]
        loss_ref[...] += 0.5 * jnp.sum(jnp.mean(err * err, axis=-1, keepdims=True), axis=0, keepdims=True)
        dy = err * (1.0 / d)
        dn = dy * gv
        dx = r * (dn - xh * jnp.mean(dn * xh, axis=-1, keepdims=True))
        dx_ref[...] = dx
        dxb_ref[...] = dx.astype(BF16)
        dg_ref[...] += jnp.sum(dy * xh, axis=0, keepdims=True)

    row = pl.BlockSpec((ts, d), lambda i: (i, 0))
    vec = pl.BlockSpec((1, d), lambda i: (0, 0))
    tile = pl.BlockSpec((N_PAIR, ts, c), lambda i: (0, i, 0))
    half = jax.ShapeDtypeStruct((N_PAIR, s, c), BF16)
    in_specs = [pl.BlockSpec((nb, ts, c), lambda i: (0, i, 0)),
                pl.BlockSpec((nb, hb, c), lambda i: (0, _prev_blk(i, ts, hb), 0)),
                pl.BlockSpec((nb, 8, c), lambda i: (0, 0, 0)),
                pl.BlockSpec((N_PAIR, c, d), lambda i: (0, 0, 0)), row]
    args = [up8, up8, wc8, wd4, x]
    if head is None:
        out_specs = [row, tile, tile, tile]
        out_shape = [jax.ShapeDtypeStruct((s, d), F32), half, half, half]
    else:
        in_specs += [vec, row]
        args += list(head)
        out_specs = [pl.BlockSpec((1, 1), lambda i: (0, 0)), row, row, vec, tile, tile, tile]
        out_shape = [jax.ShapeDtypeStruct((1, 1), F32), jax.ShapeDtypeStruct((s, d), F32),
                     jax.ShapeDtypeStruct((s, d), BF16), jax.ShapeDtypeStruct((1, d), F32), half, half, half]
    return pl.pallas_call(
        body, name=name, grid=(s // ts,), in_specs=in_specs, out_specs=out_specs, out_shape=out_shape,
        scratch_shapes=[pltpu.VMEM((4, hb + ts, c), F32)],
        compiler_params=_params(("parallel",) if head is None else ("arbitrary",)),
    )(*args)


def _ffn_dwdown(act4, db, *, name):
    npair, s, c = act4.shape
    d = db.shape[1]

    def body(a_ref, d_ref, o_ref):
        o_ref[...] = lax.dot_general(a_ref[...], d_ref[...], (((0,), (0,)), ((), ())),
                                     preferred_element_type=F32).astype(BF16)

    return pl.pallas_call(
        body, name=name, grid=(npair,),
        in_specs=[pl.BlockSpec((None, s, c), lambda j: (j, 0, 0)),
                  pl.BlockSpec((s, d), lambda j: (0, 0))],
        out_specs=pl.BlockSpec((None, c, d), lambda j: (j, 0, 0)),
        out_shape=jax.ShapeDtypeStruct((npair, c, d), BF16),
        compiler_params=_params(("parallel",)),
    )(act4, db)


def _ffn_dact(db, wd4, *, name, tm=1024):
    s, d = db.shape
    npair, c, _ = wd4.shape
    tm = min(tm, s)

    def body(d_ref, w_ref, o_ref):
        res = lax.dot_general(d_ref[...], w_ref[...].reshape(npair * c, d), _NT, preferred_element_type=F32)
        for j in range(npair):
            o_ref[j] = res[:, j * c:(j + 1) * c].astype(BF16)

    return pl.pallas_call(
        body, name=name, grid=(s // tm,),
        in_specs=[pl.BlockSpec((tm, d), lambda i: (i, 0)),
                  pl.BlockSpec((npair, c, d), lambda i: (0, 0, 0))],
        out_specs=pl.BlockSpec((npair, tm, c), lambda i: (0, i, 0)),
        out_shape=jax.ShapeDtypeStruct((npair, s, c), BF16),
        compiler_params=_params(("parallel",)),
    )(db, wd4)


def _ffn_midbwd(up8, ug4, uv4, dact4, wc8, *, name, ts=1024, rc=32):
    nb, s, c = up8.shape
    hb = HALO_S
    ts = min(ts, s)
    n_i = s // ts

    def body(pg_ref, pv_ref, ug_ref, ugn_ref, uv_ref, uvn_ref, d_ref, dn_ref, wg_ref, wv_ref,
             dg_ref, dv_ref, dwg_ref, dwv_ref, extdg, extdv):
        i = pl.program_id(1)
        keep_n = jnp.where(i < n_i - 1, 1.0, 0.0)

        @pl.when(i == 0)
        def _():
            dwg_ref[...] = jnp.zeros_like(dwg_ref)
            dwv_ref[...] = jnp.zeros_like(dwv_ref)

        def du_rows(r0, rows, gg, vv, dav):
            sg = _sigmoid(gg)
            extdg[r0:r0 + rows, :] = dav * vv * (sg * (1.0 + gg * (1.0 - sg)))
            extdv[r0:r0 + rows, :] = dav * (gg * sg)

        for q in range(ts // rc):
            rows = slice(q * rc, q * rc + rc)
            du_rows(q * rc, rc, ug_ref[rows, :].astype(F32), uv_ref[rows, :].astype(F32),
                    d_ref[rows, :].astype(F32))
        du_rows(ts, 8, ugn_ref[0:8, :].astype(F32), uvn_ref[0:8, :].astype(F32),
                dn_ref[0:8, :].astype(F32) * keep_n)

        def fold(p):
            acc = p[0:8, :]
            for r in range(8, rc, 8):
                acc = acc + p[r:r + 8, :]
            return acc

        for extd, p_ref, w_ref, out_ref, dw_ref in ((extdg, pg_ref, wg_ref, dg_ref, dwg_ref),
                                                    (extdv, pv_ref, wv_ref, dv_ref, dwv_ref)):
            part = [None, None, None]
            for q in range(ts // rc):
                r0 = q * rc
                pre = p_ref[r0:r0 + rc, :].astype(F32)
                dup = None
                for k in range(3):
                    sh = extd[pl.ds(r0 + 2 - k, rc), :]
                    term = w_ref[k:k + 1, :] * sh
                    dup = term if dup is None else dup + term
                    prod = fold(pre * sh)
                    part[k] = prod if part[k] is None else part[k] + prod
                out_ref[r0:r0 + rc, :] = dup.astype(BF16)
            for k in range(3):
                dw_ref[k:k + 1, :] += jnp.sum(part[k], axis=0, keepdims=True)

    def blk(off, nxt):
        if nxt:
            return pl.BlockSpec((None, hb, c), lambda j, i: (j + off, _next_blk(i, ts, hb, s), 0))
        return pl.BlockSpec((None, ts, c), lambda j, i: (j + off, i, 0))

    def taps(off):
        return pl.BlockSpec((None, 8, c), lambda j, i: (j + off, 0, 0))

    tile = blk(0, False)
    acc = pl.BlockSpec((None, 8, c), lambda j, i: (j, 0, 0))
    return pl.pallas_call(
        body, name=name, grid=(N_PAIR, n_i),
        in_specs=[tile, blk(N_PAIR, False), tile, blk(0, True), tile, blk(0, True), tile, blk(0, True),
                  taps(0), taps(N_PAIR)],
        out_specs=[tile, tile, acc, acc],
        out_shape=[jax.ShapeDtypeStruct((N_PAIR, s, c), BF16), jax.ShapeDtypeStruct((N_PAIR, s, c), BF16),
                   jax.ShapeDtypeStruct((N_PAIR, 8, c), F32), jax.ShapeDtypeStruct((N_PAIR, 8, c), F32)],
        scratch_shapes=[pltpu.VMEM((ts + 8, c), F32), pltpu.VMEM((ts + 8, c), F32)],
        compiler_params=_params(("parallel", "arbitrary")),
    )(up8, up8, ug4, ug4, uv4, uv4, dact4, dact4, wc8, wc8)


def _ffn_dh(dupg, dupv, w8, x, g, dres, dep, *, name, tm=512):
    npair, s, c = dupg.shape
    d = x.shape[1]
    tm = min(tm, s)

    def body(dg_ref, dv_ref, w_ref, x_ref, g_ref, dres_ref, dep_ref, dx_ref, dxb_ref, dgain_ref):
        @pl.when(pl.program_id(0) == 0)
        def _():
            dgain_ref[...] = jnp.zeros_like(dgain_ref)

        dh = None
        for half, src in enumerate((dg_ref, dv_ref)):
            lhs = jnp.concatenate([src[j] for j in range(npair)], axis=1)
            rhs = w_ref[half * npair:(half + 1) * npair].reshape(npair * c, d)
            part = jnp.dot(lhs, rhs, preferred_element_type=F32)
            dh = part if dh is None else dh + part
        dx, dgr = _rms_bwd_rows(x_ref[...], g_ref[...], dh)
        dx = dres_ref[...] + dx
        dx_ref[...] = dx
        dxb_ref[...] = dx.astype(BF16)
        dgain_ref[...] += jnp.sum(dgr, axis=0, keepdims=True)

    row = pl.BlockSpec((tm, d), lambda i: (i, 0))
    vec = pl.BlockSpec((1, d), lambda i: (0, 0))
    dup = pl.BlockSpec((npair, tm, c), lambda i: (0, i, 0))
    return pl.pallas_call(
        body, name=name, grid=(s // tm,),
        in_specs=[dup, dup, pl.BlockSpec((2 * npair, c, d), lambda i: (0, 0, 0)), row, vec, row,
                  pl.BlockSpec(memory_space=pl.ANY)],
        out_specs=[row, row, vec],
        out_shape=[jax.ShapeDtypeStruct((s, d), F32), jax.ShapeDtypeStruct((s, d), BF16),
                   jax.ShapeDtypeStruct((1, d), F32)],
        compiler_params=_params(("arbitrary",)),
    )(dupg, dupv, w8, x, g, dres, dep)


def _ffn_dwup(h, dupg, dupv, *, name, tm=1024):
    npair, s, c = dupg.shape
    d = h.shape[1]

    def body(h_ref, dg_ref, dv_ref, o_ref):
        k = pl.program_id(1)

        @pl.when(k < npair)
        def _():
            o_ref[...] = lax.dot_general(dg_ref[...], h_ref[...], _TN, preferred_element_type=F32).astype(BF16)

        @pl.when(k >= npair)
        def _():
            o_ref[...] = lax.dot_general(dv_ref[...], h_ref[...], _TN, preferred_element_type=F32).astype(BF16)

    return pl.pallas_call(
        body, name=name, grid=(d // tm, 2 * npair),
        in_specs=[pl.BlockSpec((s, tm), lambda m, k: (0, m)),
                  pl.BlockSpec((None, s, c), lambda m, k: (jnp.minimum(k, npair - 1), 0, 0)),
                  pl.BlockSpec((None, s, c), lambda m, k: (jnp.maximum(k - npair, 0), 0, 0))],
        out_specs=pl.BlockSpec((None, c, tm), lambda m, k: (k, 0, m)),
        out_shape=jax.ShapeDtypeStruct((2 * npair, c, d), BF16),
        compiler_params=_params(("parallel", "arbitrary")),
    )(h, dupg, dupv)


def _pool_counts(i, ts, rows, window):
    t = lax.broadcasted_iota(jnp.int32, (rows, 1), 0) + i * ts + 1
    return jnp.minimum(t, window).astype(F32)


def _pool_fwd(x, g, w_pool, scale, *, name, ts=512):
    s, d = x.shape
    hb = HALO_A
    pg = POOL_GROUP
    ts = min(ts, s)

    def body(x_ref, xp_ref, g_ref, w_ref, s_ref, o_ref, p_ref, yu_ref, ext, sa, sb):
        i = pl.program_id(0)
        keep = jnp.where(i > 0, 1.0, 0.0)
        gv = g_ref[...]
        ext[0:hb, :] = _rms_rows(xp_ref[...], gv) * keep
        ext[hb:hb + ts, :] = _rms_rows(x_ref[...], gv)
        rows = hb + ts - 8
        sa[0:8, :] = jnp.zeros((8, d), F32)
        sb[0:8, :] = jnp.zeros((8, d), F32)
        for gi, w in enumerate(POOL_WINDOWS):
            cols = slice(gi * pg, (gi + 1) * pg)
            cur, nxt, k = ext, sa, 1
            while k < w:
                nxt[8:8 + rows, cols] = cur[8:8 + rows, cols] + cur[pl.ds(8 - k, rows), cols]
                cur, nxt, k = nxt, (sb if nxt is sa else sa), 2 * k
            h = ext[hb:hb + ts, cols]
            pv = (cur[hb:hb + ts, cols] / _pool_counts(i, ts, ts, w) - h).astype(BF16)
            p_ref[:, cols] = pv
            yu = jnp.dot(pv, w_ref[gi], preferred_element_type=F32)
            yu_ref[:, cols] = yu.astype(BF16)
            o_ref[:, cols] = x_ref[:, cols] + yu * s_ref[:, cols]

    row = pl.BlockSpec((ts, d), lambda i: (i, 0))
    vec = pl.BlockSpec((1, d), lambda i: (0, 0))
    return pl.pallas_call(
        body, name=name, grid=(s // ts,),
        in_specs=[row, pl.BlockSpec((hb, d), lambda i: (_prev_blk(i, ts, hb), 0)), vec,
                  pl.BlockSpec((d // pg, pg, pg), lambda i: (0, 0, 0)), vec],
        out_specs=[row, row, row],
        out_shape=[jax.ShapeDtypeStruct((s, d), F32), jax.ShapeDtypeStruct((s, d), BF16),
                   jax.ShapeDtypeStruct((s, d), BF16)],
        scratch_shapes=[pltpu.VMEM((hb + ts, d), F32)] * 3,
        compiler_params=_params(("parallel",)),
    )(x, x, g, w_pool, scale)


def _pool_dw(p, dyc, *, name):
    s, d = p.shape
    pg = POOL_GROUP

    def body(p_ref, d_ref, o_ref):
        o_ref[...] = lax.dot_general(p_ref[...], d_ref[...], (((0,), (0,)), ((), ())),
                                     preferred_element_type=F32).astype(BF16)

    blk = pl.BlockSpec((s, pg), lambda gi: (0, gi))
    return pl.pallas_call(
        body, name=name, grid=(d // pg,),
        in_specs=[blk, blk], out_specs=pl.BlockSpec((None, pg, pg), lambda gi: (gi, 0, 0)),
        out_shape=jax.ShapeDtypeStruct((d // pg, pg, pg), BF16),
        compiler_params=_params(("parallel",)),
    )(p, dyc)


def _pool_bwd(dres, w_pool, scale, yu, x, g, *, name, ts=512):
    s, d = x.shape
    hb = HALO_A
    pg = POOL_GROUP
    ts = min(ts, s)
    n_i = s // ts

    def body(dres_ref, dresn_ref, w_ref, s_ref, yu_ref, x_ref, g_ref,
             dx_ref, dxb_ref, dg_ref, dyc_ref, ds_ref, ext, dh, sa, sb, dp_s):
        i = pl.program_id(0)
        keep_n = jnp.where(i < n_i - 1, 1.0, 0.0)

        @pl.when(i == 0)
        def _():
            dg_ref[...] = jnp.zeros_like(dg_ref)
            ds_ref[...] = jnp.zeros_like(ds_ref)

        ds_ref[...] += jnp.sum(dres_ref[...] * yu_ref[...].astype(F32), axis=0, keepdims=True)
        for gi, w in enumerate(POOL_WINDOWS):
            cols = slice(gi * pg, (gi + 1) * pg)
            dyc = (dres_ref[:, cols] * s_ref[:, cols]).astype(BF16)
            dyc_ref[:, cols] = dyc
            dp = lax.dot_general(dyc, w_ref[gi], _NT, preferred_element_type=F32)
            dp_s[:, cols] = dp
            ext[0:ts, cols] = dp / _pool_counts(i, ts, ts, w)
            dycn = (dresn_ref[:, cols] * s_ref[:, cols]).astype(BF16)
            dpn = lax.dot_general(dycn, w_ref[gi], _NT, preferred_element_type=F32)
            ext[ts:ts + hb, cols] = dpn / _pool_counts(i + 1, ts, hb, w) * keep_n
        rows = ts + hb - 8
        sa[rows:rows + 8, :] = jnp.zeros((8, d), F32)
        sb[rows:rows + 8, :] = jnp.zeros((8, d), F32)
        for gi, w in enumerate(POOL_WINDOWS):
            cols = slice(gi * pg, (gi + 1) * pg)
            cur, nxt, k = ext, sa, 1
            while k < w:
                nxt[0:rows, cols] = cur[0:rows, cols] + cur[pl.ds(k, rows), cols]
                cur, nxt, k = nxt, (sb if nxt is sa else sa), 2 * k
            dh[:, cols] = cur[0:ts, cols] - dp_s[:, cols]
        dx, dgr = _rms_bwd_rows(x_ref[...], g_ref[...], dh[...])
        dx = dres_ref[...] + dx
        dx_ref[...] = dx
        dxb_ref[...] = dx.astype(BF16)
        dg_ref[...] += jnp.sum(dgr, axis=0, keepdims=True)

    row = pl.BlockSpec((ts, d), lambda i: (i, 0))
    vec = pl.BlockSpec((1, d), lambda i: (0, 0))
    return pl.pallas_call(
        body, name=name, grid=(n_i,),
        in_specs=[row, pl.BlockSpec((hb, d), lambda i: (_next_blk(i, ts, hb, s), 0)),
                  pl.BlockSpec((d // pg, pg, pg), lambda i: (0, 0, 0)), vec, row, row, vec],
        out_specs=[row, row, vec, row, vec],
        out_shape=[jax.ShapeDtypeStruct((s, d), F32), jax.ShapeDtypeStruct((s, d), BF16),
                   jax.ShapeDtypeStruct((1, d), F32), jax.ShapeDtypeStruct((s, d), BF16),
                   jax.ShapeDtypeStruct((1, d), F32)],
        scratch_shapes=[pltpu.VMEM((ts + hb, d), F32), pltpu.VMEM((ts, d), F32),
                        pltpu.VMEM((ts + hb, d), F32), pltpu.VMEM((ts + hb, d), F32),
                        pltpu.VMEM((ts, d), F32)],
        compiler_params=_params(("arbitrary",)),
    )(dres, dres, w_pool, scale, yu, x, g)


def _pad_rows(w, rows):
    pad = [(0, 0)] * (w.ndim - 2) + [(0, rows - w.shape[-2]), (0, 0)]
    return jnp.pad(w, pad)


def _ffn_layer_fwd(x, nf, w8, wc8, wd4, tag, head=None):
    h, up8 = _ffn_up(x, nf, w8, name=f"ffn{tag}_up")
    res = _ffn_mid_down(up8, wc8, wd4, x, head, name=f"ffn{tag}_mid_down")
    act4, ug4, uv4 = res[-3:]
    out = res[0] if head is None else tuple(res[:4])
    return out, (h, up8, ug4, uv4, act4)


def _ffn_layer_bwd(d, db, x, nf, w8, wc8, wd4, saved, tag, grads_ready):
    h, up8, ug4, uv4, act4 = saved
    dact4 = _ffn_dact(db, wd4, name=f"ffn{tag}_dact")
    dwd4 = _ffn_dwdown(act4, db, name=f"ffn{tag}_dwdown")
    dupg, dupv, dwg, dwv = _ffn_midbwd(up8, ug4, uv4, dact4, wc8, name=f"ffn{tag}_midbwd")
    dw8 = _ffn_dwup(h, dupg, dupv, name=f"ffn{tag}_dwup")
    sent = grads_ready(f"ffn{tag}", {f"w_up{tag}": dw8, f"w_down{tag}": dwd4})
    dx, dxb, dnf = _ffn_dh(dupg, dupv, w8, x, nf, d, sent, name=f"ffn{tag}_dh")
    dwc8 = jnp.concatenate([dwg, dwv], axis=0)[:, :3]
    return dx, dxb, dnf, dwc8


def _local_step(x, target, weights, grads_ready):
    w0 = weights("mix0", x)
    wa = _pad_rows(w0["conv_a"], 32)
    wb = _pad_rows(w0["conv_b"], 8)
    wc = [_pad_rows(w0["conv_ffn"][l], 8) for l in range(2)]
    h0, z = _rms_mm(x, w0["norm_mix_even"], w0["w_in_t"], name="mix0_in", out_dtype=F32, tn=2560)
    ab, ca, x1 = _mix0_fwd(z, wa, w0["ln_a_g"], w0["ln_a_b"], wb, w0["w_out"], x, name="mix0_mid")
    w1 = weights("ffn0", x1)
    x2, ffn0 = _ffn_layer_fwd(x1, w0["norm_ffn"][0:1], w1["w_up"], wc[0], w1["w_down"], 0)
    w2 = weights("ffn1", x2)
    x3, p, yu = _pool_fwd(x2, w0["norm_mix_odd"], w2["w_pool"], w0["pool_scale"], name="pool_fwd")
    (loss, d4, d4b, g_norm_final), ffn1 = _ffn_layer_fwd(
        x3, w0["norm_ffn"][1:2], w2["w_up"], wc[1], w2["w_down"], 1, head=(w0["norm_final"], target))

    d3, d3b, g_nf1, g_wc1 = _ffn_layer_bwd(
        d4, d4b, x3, w0["norm_ffn"][1:2], w2["w_up"], wc[1], w2["w_down"], ffn1, 1, grads_ready)
    d2, d2b, g_nmo, dyc, g_scale = _pool_bwd(d3, w2["w_pool"], w0["pool_scale"], yu, x2, w0["norm_mix_odd"],
                                             name="pool_bwd")
    g_pool = _pool_dw(p, dyc, name="pool_dw")
    d1, d1b, g_nf0, g_wc0 = _ffn_layer_bwd(
        d2, d2b, x1, w0["norm_ffn"][0:1], w1["w_up"], wc[0], w1["w_down"], ffn0, 0, grads_ready)
    dab = _mm(d1b, w0["w_out"], tb=True, name="mix0_dab", tm=1024, tn=1024)
    g_out = _mm(ab, d1b, ta=True, out_dtype=BF16, name="mix0_dwout", tm=1024, tn=512)
    dz, g_wa, g_wb, g_lg, g_lb = _mix0_bwd(z, ca, dab, wa, w0["ln_a_g"], w0["ln_a_b"], wb, name="mix0_midbwd")
    g_in = _mm(dz, h0, ta=True, out_dtype=BF16, name="mix0_dwin", tm=512, tn=1024)
    sent_mix = grads_ready("mix0", {"w_in": g_in, "w_out": g_out, "w_pool": g_pool})
    dx, g_nme = _mm_rms_bwd(dz, w0["w_in_t"], x, w0["norm_mix_even"], d1, sent_mix, name="mix0_dh")

    small = {
        "norm_mix_even": g_nme, "conv_a": g_wa[:A_TAPS], "ln_a_g": g_lg, "ln_a_b": g_lb, "conv_b": g_wb[:3],
        "norm_mix_odd": g_nmo, "pool_scale": g_scale, "norm_ffn": jnp.concatenate([g_nf0, g_nf1], axis=0),
        "conv_ffn": [g_wc0, g_wc1], "norm_final": g_norm_final,
    }
    return loss[0, 0], dx, small


def _my_pos():
    return lax.axis_index("x"), lax.axis_index("y"), lax.axis_index("c")


def _flip(pos, r):
    x, y, c = pos
    return (1 - x if r & 4 else x, 1 - y if r & 2 else y, 1 - c if r & 1 else c)


def _dev_index(pos):
    return 4 * pos[0] + 2 * pos[1] + pos[2]


_HBM = pl.BlockSpec(memory_space=pltpu.HBM)
_SEM = pl.BlockSpec(memory_space=pltpu.SEMAPHORE)
_EFFECT = pltpu.SideEffectType.DATAFLOW_SIDE_EFFECTING


def _exchange_copy(ins, lands, send_sems, recv_sems, scatter, pos, a, r, receiving):
    me = _dev_index(pos)
    peer = _flip(pos, r)
    dest = _dev_index(pos) if receiving else _dev_index(peer)
    src = ins[a].at[dest] if scatter[a] else ins[a]
    slot = _dev_index(peer) if receiving else me
    return pltpu.make_async_remote_copy(
        src_ref=src, dst_ref=lands[a].at[slot], send_sem=send_sems.at[a * (N_DEV - 1) + r - 1],
        recv_sem=recv_sems.at[a * (N_DEV - 1) + r - 1],
        device_id=peer, device_id_type=pl.DeviceIdType.MESH)


ALL_PEERS = tuple(range(1, N_DEV))
CHIP_PEERS = (1, 2, 4, 6)
FORWARDED = (2, 4, 6)


def _exchange_start(arrays, scatter, after, *, name, peers=ALL_PEERS):
    n = len(arrays)
    me = _dev_index(_my_pos())
    lands = []
    for arr, sc in zip(arrays, scatter):
        own = lax.dynamic_index_in_dim(arr, me, 0, keepdims=True) if sc else arr[None]
        shape = arr.shape if sc else (N_DEV,) + arr.shape
        lands.append(lax.dynamic_update_slice(lax.empty(shape, arr.dtype), own, (me,) + (0,) * (len(shape) - 1)))

    def body(*refs):
        ins, lnd = refs[:n], refs[n:2 * n]
        send_sems, recv_sems = refs[2 * n + 1], refs[2 * n + 2]
        token = refs[-1]
        pos = _my_pos()
        for a in range(n):
            for r in peers:
                _exchange_copy(ins, lnd, send_sems, recv_sems, scatter, pos, a, r, receiving=False).start()
        token[...] = jnp.zeros_like(token)

    bufs = [pltpu.with_memory_space_constraint(t, pltpu.HBM) for t in list(arrays) + lands]
    sems = pltpu.SemaphoreType.DMA((n * (N_DEV - 1),))
    res = pl.pallas_call(
        body, name=name,
        out_shape=(sems, sems, *[pltpu.HBM(t.shape, t.dtype) for t in bufs], jax.ShapeDtypeStruct((8, 128), F32)),
        in_specs=[_HBM] * (2 * n) + [pl.BlockSpec(memory_space=pl.ANY)],
        out_specs=(_SEM, _SEM, *[_HBM] * (2 * n), pl.BlockSpec(memory_space=pltpu.VMEM)),
        input_output_aliases={i: 2 + i for i in range(2 * n)},
        compiler_params=pltpu.CompilerParams(has_side_effects=_EFFECT),
    )(*bufs, after)
    return res[0], res[1], list(res[2:2 + n]), list(res[2 + n:2 + 2 * n]), res[-1]


def _exchange_wait(started, scatter, after, *, name, peers=ALL_PEERS):
    send_sems, recv_sems, arrays, lands, _ = started
    n = len(arrays)

    def body(*refs):
        ins, lnd = refs[:n], refs[n:2 * n]
        send_sems, recv_sems = refs[2 * n], refs[2 * n + 1]
        pos = _my_pos()
        for a in range(n):
            for r in peers:
                _exchange_copy(ins, lnd, send_sems, recv_sems, scatter, pos, a, r, receiving=False).wait_send()
                _exchange_copy(ins, lnd, send_sems, recv_sems, scatter, pos, a, r, receiving=True).wait_recv()

    bufs = list(arrays) + list(lands)
    after = list(after) if isinstance(after, (list, tuple)) else [after]
    res = pl.pallas_call(
        body, name=name,
        out_shape=tuple(pltpu.HBM(t.shape, t.dtype) for t in bufs),
        in_specs=[_HBM] * (2 * n) + [_SEM, _SEM] + [pl.BlockSpec(memory_space=pl.ANY)] * len(after),
        out_specs=tuple([_HBM] * (2 * n)),
        input_output_aliases={i: i for i in range(2 * n)},
        compiler_params=pltpu.CompilerParams(has_side_effects=_EFFECT),
    )(*bufs, send_sems, recv_sems, *after)
    return list(res[n:])


def _forward_copy(lands, send_sems, recv_sems, pos, a, q, receiving):
    slot = _dev_index(_flip(pos, q ^ 1 if receiving else q))
    idx = a * len(FORWARDED) + FORWARDED.index(q)
    return pltpu.make_async_remote_copy(
        src_ref=lands[a].at[slot], dst_ref=lands[a].at[slot], send_sem=send_sems.at[idx], recv_sem=recv_sems.at[idx],
        device_id=_flip(pos, 1), device_id_type=pl.DeviceIdType.MESH)


def _forward_start(lands, after, *, name):
    n = len(lands)

    def body(*refs):
        lnd = refs[:n]
        send_sems, recv_sems = refs[n + 1], refs[n + 2]
        token = refs[-1]
        pos = _my_pos()
        for a in range(n):
            for q in FORWARDED:
                _forward_copy(lnd, send_sems, recv_sems, pos, a, q, receiving=False).start()
        token[...] = jnp.zeros_like(token)

    sems = pltpu.SemaphoreType.DMA((n * len(FORWARDED),))
    res = pl.pallas_call(
        body, name=name,
        out_shape=(sems, sems, *[pltpu.HBM(t.shape, t.dtype) for t in lands], jax.ShapeDtypeStruct((8, 128), F32)),
        in_specs=[_HBM] * n + [pl.BlockSpec(memory_space=pl.ANY)],
        out_specs=(_SEM, _SEM, *[_HBM] * n, pl.BlockSpec(memory_space=pltpu.VMEM)),
        input_output_aliases={i: 2 + i for i in range(n)},
        compiler_params=pltpu.CompilerParams(has_side_effects=_EFFECT),
    )(*lands, after)
    return res[0], res[1], list(res[2:2 + n]), res[-1]


def _forward_wait(forwarded, after, *, name):
    send_sems, recv_sems, lands, _ = forwarded
    n = len(lands)

    def body(*refs):
        lnd = refs[:n]
        send_sems, recv_sems = refs[n], refs[n + 1]
        pos = _my_pos()
        for a in range(n):
            for q in FORWARDED:
                _forward_copy(lnd, send_sems, recv_sems, pos, a, q, receiving=False).wait_send()
                _forward_copy(lnd, send_sems, recv_sems, pos, a, q, receiving=True).wait_recv()

    res = pl.pallas_call(
        body, name=name,
        out_shape=tuple(pltpu.HBM(t.shape, t.dtype) for t in lands),
        in_specs=[_HBM] * n + [_SEM, _SEM, pl.BlockSpec(memory_space=pl.ANY)],
        out_specs=tuple([_HBM] * n),
        input_output_aliases={i: i for i in range(n)},
        compiler_params=pltpu.CompilerParams(has_side_effects=_EFFECT),
    )(*lands, send_sems, recv_sems, after)
    return list(res)


def _adamw_update(p_ref, w_ref, m_ref, v_ref, g_ref, d_ref, mo_ref, vo_ref):
    g = p_ref[0].astype(F32)
    for k in range(1, N_DEV):
        g = g + p_ref[k].astype(F32)
    mn = ADAM_B1 * m_ref[...] + (1.0 - ADAM_B1) * g
    vn = ADAM_B2 * v_ref[...] + (1.0 - ADAM_B2) * (g * g)
    m_hat = mn / (1.0 - ADAM_B1 ** ADAM_STEP)
    v_hat = vn / (1.0 - ADAM_B2 ** ADAM_STEP)
    g_ref[...] = g
    d_ref[...] = -ADAM_LR * (m_hat / (jnp.sqrt(v_hat) + ADAM_EPS) + ADAM_WD * w_ref[...])
    mo_ref[...] = mn
    vo_ref[...] = vn


def _adamw_small(parts, ws, ms, vs, *, name):
    n = len(ws)

    def body(*refs):
        ins, outs = refs[:4 * n], refs[4 * n:]
        for j in range(n):
            _adamw_update(ins[j], ins[n + j], ins[2 * n + j], ins[3 * n + j], *outs[4 * j:4 * j + 4])

    return pl.pallas_call(
        body, name=name,
        out_shape=[jax.ShapeDtypeStruct(w.shape, F32) for w in ws for _ in range(4)],
    )(*parts, *ws, *ms, *vs)


def _adamw(parts, w, m, v, *, name, tr):
    nl, r, c = w.shape
    assert len(parts) == nl and r % tr == 0
    n_i = r // tr

    def body(*refs):
        p_refs = refs[:nl]
        w_ref, m_ref, v_ref = refs[nl:nl + 3]

        def update(p_ref):
            _adamw_update(p_ref, w_ref, m_ref, v_ref, *refs[nl + 3:])

        if nl == 1:
            update(p_refs[0])
        else:
            for layer in range(nl):
                pl.when(pl.program_id(0) == layer)(lambda p_ref=p_refs[layer]: update(p_ref))

    def parts_spec(layer):
        def index(l, i):
            return (0, jnp.where(l < layer, 0, jnp.where(l > layer, n_i - 1, i)), 0)
        return pl.BlockSpec((N_DEV, tr, c), index)

    blk = pl.BlockSpec((None, tr, c), lambda l, i: (l, i, 0))
    return pl.pallas_call(
        body, name=name, grid=(nl, n_i),
        in_specs=[parts_spec(layer) for layer in range(nl)] + [blk, blk, blk],
        out_specs=[blk] * 4, out_shape=[jax.ShapeDtypeStruct((nl, r, c), F32)] * 4,
        compiler_params=_params(("arbitrary", "arbitrary")),
    )(*parts, w, m, v)


def _pack(parts, lead=()):
    flat = jnp.concatenate([p.reshape(lead + (-1,)) for p in parts], axis=-1)
    n = flat.shape[-1]
    rows = -(-n // (8 * 128)) * 8
    flat = jnp.pad(flat, [(0, 0)] * len(lead) + [(0, rows * 128 - n)])
    return flat.reshape(lead + (rows, 128))


def _to_dev_major(g, axis):
    shp = g.shape
    g = g.reshape(shp[:axis] + (N_DEV, shp[axis] // N_DEV) + shp[axis + 1:])
    return jnp.moveaxis(g, axis, 0)


def _from_dev_major(g, axis):
    g = jnp.moveaxis(g, 0, axis)
    shp = g.shape
    return g.reshape(shp[:axis] + (shp[axis] * shp[axis + 1],) + shp[axis + 2:])


SMALL_SHARDED = ("conv_a", "conv_b", "norm_mix_odd", "pool_scale", "conv_ffn_w")
SMALL_REPLICATED = ("norm_mix_even", "ln_a_g", "ln_a_b", "norm_ffn", "norm_final")
BIG = {"w_in": ("w_in", 0, 64), "w_out": ("w_out", 0, 32), "w_pool": ("w_pool", 0, 32),
       "w_up0": ("w_up", 0, 176), "w_up1": ("w_up", 1, 176),
       "w_down0": ("w_down", 0, 176), "w_down1": ("w_down", 1, 176)}
COLUMN_SHARDED = ("w_in", "w_up")


def kernel(x, norm_mix_even, w_in, conv_a, ln_a_g, ln_a_b, conv_b, w_out, norm_mix_odd, w_pool, pool_scale, norm_ffn, w_up, conv_ffn_w, w_down, norm_final, loss_target, m_norm_mix_even, m_w_in, m_conv_a, m_ln_a_g, m_ln_a_b, m_conv_b, m_w_out, m_norm_mix_odd, m_w_pool, m_pool_scale, m_norm_ffn, m_w_up, m_conv_ffn_w, m_w_down, m_norm_final, v_norm_mix_even, v_w_in, v_conv_a, v_ln_a_g, v_ln_a_b, v_conv_b, v_w_out, v_norm_mix_odd, v_w_pool, v_pool_scale, v_norm_ffn, v_w_up, v_conv_ffn_w, v_w_down, v_norm_final):
    names = ("norm_mix_even", "w_in", "conv_a", "ln_a_g", "ln_a_b", "conv_b", "w_out", "norm_mix_odd", "w_pool",
             "pool_scale", "norm_ffn", "w_up", "conv_ffn_w", "w_down", "norm_final")
    wts = dict(zip(names, (norm_mix_even, w_in, conv_a, ln_a_g, ln_a_b, conv_b, w_out, norm_mix_odd, w_pool,
                           pool_scale, norm_ffn, w_up, conv_ffn_w, w_down, norm_final)))
    mom = dict(zip(names, (m_norm_mix_even, m_w_in, m_conv_a, m_ln_a_g, m_ln_a_b, m_conv_b, m_w_out, m_norm_mix_odd,
                           m_w_pool, m_pool_scale, m_norm_ffn, m_w_up, m_conv_ffn_w, m_w_down, m_norm_final)))
    var = dict(zip(names, (v_norm_mix_even, v_w_in, v_conv_a, v_ln_a_g, v_ln_a_b, v_conv_b, v_w_out, v_norm_mix_odd,
                           v_w_pool, v_pool_scale, v_norm_ffn, v_w_up, v_conv_ffn_w, v_w_down, v_norm_final)))
    d = x.shape[-1]

    def shard3d(t, param):
        a = t[param]
        if param in COLUMN_SHARDED:
            return jnp.swapaxes(a, 1, 2)
        return a.reshape(a.shape[0], -1, a.shape[-1])

    def unshard3d(a, param):
        if param in COLUMN_SHARDED:
            return jnp.swapaxes(a, 1, 2)
        return a.reshape(wts[param].shape)

    def shard2d(t, key):
        param, layer, _ = BIG[key]
        return shard3d(t, param)[layer]

    small_w = _pack([wts[k] for k in SMALL_SHARDED])
    bf = {k: shard2d(wts, k).astype(BF16) for k in BIG}
    gather_groups = {"mix0": ("w_in", "w_out", "small"), "ffn0": ("w_up0", "w_down0"),
                     "ffn1": ("w_pool", "w_up1", "w_down1")}
    order = list(gather_groups)
    started = {}

    def start_gather(grp, after):
        arrs = [small_w if k == "small" else bf[k] for k in gather_groups[grp]]
        started[grp] = _exchange_start(arrs, [False] * len(arrs), after, name=f"gather_{grp}_start", peers=CHIP_PEERS)

    start_gather(order[0], small_w)

    def weights(grp, after):
        keys = gather_groups[grp]
        lands = _exchange_wait(started[grp], [False] * len(keys), after, name=f"gather_{grp}_wait", peers=CHIP_PEERS)
        forwarded = _forward_start(lands, small_w, name=f"gather_{grp}_forward")
        if grp != order[-1]:
            start_gather(order[order.index(grp) + 1], forwarded[-1])
            behind = started[order[order.index(grp) + 1]][-1]
        else:
            behind = forwarded[-1]
        gw = dict(zip(keys, _forward_wait(forwarded, behind, name=f"gather_{grp}_forward_wait")))
        if grp == "ffn0":
            return {"w_up": gw["w_up0"], "w_down": gw["w_down0"].reshape(N_PAIR, -1, d)}
        if grp == "ffn1":
            return {"w_up": gw["w_up1"], "w_down": gw["w_down1"].reshape(N_PAIR, -1, d),
                    "w_pool": _from_dev_major(gw["w_pool"].reshape(N_DEV, len(POOL_WINDOWS), -1, POOL_GROUP), 1)}
        per_dev = gw["small"].reshape(N_DEV, -1)
        sizes = [wts[k].size for k in SMALL_SHARDED]
        offs = [sum(sizes[:i]) for i in range(len(sizes))]
        small_full = {k: per_dev[:, o:o + n_].reshape((N_DEV,) + wts[k].shape)
                      for k, o, n_ in zip(SMALL_SHARDED, offs, sizes)}
        return {
            "norm_mix_even": norm_mix_even, "ln_a_g": ln_a_g, "ln_a_b": ln_a_b, "norm_ffn": norm_ffn,
            "norm_final": norm_final[None],
            "w_in_t": gw["w_in"].reshape(-1, d),
            "w_out": gw["w_out"].reshape(-1, d),
            "conv_a": _from_dev_major(small_full["conv_a"][:, 0], 1),
            "conv_b": _from_dev_major(small_full["conv_b"][:, 0], 1),
            "norm_mix_odd": _from_dev_major(small_full["norm_mix_odd"], 1),
            "pool_scale": _from_dev_major(small_full["pool_scale"], 1),
            "conv_ffn": [small_full["conv_ffn_w"][:, l] for l in range(2)],
        }

    def dev_major(k, g):
        if k == "w_pool":
            return _to_dev_major(g, 1).reshape(N_DEV, -1, POOL_GROUP)
        return g.reshape(N_DEV, -1, g.shape[-1])

    sent = {}

    def grads_ready(grp, grads):
        keys = tuple(grads)
        parts = [dev_major(k, grads[k]) for k in keys]
        sent[grp] = (keys, _exchange_start(parts, [True] * len(keys), small_w, name=f"grads_{grp}_start"))
        return sent[grp][1][-1]

    loss, dx, g = _local_step(x[0], loss_target[0], weights, grads_ready)

    def small2d(a):
        return a.reshape(-1, a.shape[-1])

    small_keys = SMALL_SHARDED + SMALL_REPLICATED
    small_parts = [_to_dev_major(g["conv_a"], 1), _to_dev_major(g["conv_b"], 1),
                   _to_dev_major(g["norm_mix_odd"], 1), _to_dev_major(g["pool_scale"], 1),
                   jnp.stack(g["conv_ffn"], axis=1).reshape(N_DEV, -1, w_up.shape[-1])]
    small_parts += [g[k] for k in SMALL_REPLICATED]
    small_scatter = [True] * len(SMALL_SHARDED) + [False] * len(SMALL_REPLICATED)
    small_parts.append(jnp.full((1, 128), loss, F32))
    small_scatter.append(False)
    sent["small"] = (small_keys + ("loss",),
                     _exchange_start(small_parts, small_scatter, dx, name="grads_small_start"))

    landed, out, raw = {}, {}, {}

    def wait_grads(grp, after, scat=None):
        keys, st = sent[grp]
        scat = [True] * len(keys) if scat is None else scat
        landed.update(zip(keys, _exchange_wait(st, scat, after, name=f"grads_{grp}_wait")))

    def update(param, keys):
        raw[param] = _adamw([landed[k] for k in keys], shard3d(wts, param), shard3d(mom, param),
                            shard3d(var, param), name=f"adamw_{param}", tr=BIG[keys[0]][2])
        out[param] = [unshard3d(t, param) for t in raw[param]]

    wait_grads("ffn1", sent["small"][1][-1])
    wait_grads("ffn0", landed["w_up1"])
    update("w_up", ("w_up0", "w_up1"))
    update("w_down", ("w_down0", "w_down1"))
    wait_grads("mix0", [raw["w_up"][1], raw["w_down"][1]])
    update("w_in", ("w_in",))
    update("w_out", ("w_out",))
    update("w_pool", ("w_pool",))
    wait_grads("small", raw["w_pool"][1], small_scatter)
    res = _adamw_small([landed[k] for k in small_keys], [small2d(wts[k]) for k in small_keys],
                       [small2d(mom[k]) for k in small_keys], [small2d(var[k]) for k in small_keys],
                       name="adamw_small")
    for j, k in enumerate(small_keys):
        out[k] = [t.reshape(wts[k].shape) for t in res[4 * j:4 * j + 4]]

    loss = jnp.sum(landed["loss"][:, 0, 0])
    return (loss, dx[None], *[out[k][0] for k in names], *[out[k][1] for k in names],
            *[out[k][2] for k in names], *[out[k][3] for k in names])
```
